```python
import math
import jax, jax.numpy as jnp
from jax import lax
import numpy as np

D_MODEL = 2048
BATCH = 4
SEQ = 4096
DEPTH = 1

CHUNK = 64
MIX_WIDTH = D_MODEL
S5_WIDTH = MIX_WIDTH // 2
S5_GROUP_WIDTH = 16
S5_GROUPS = S5_WIDTH // S5_GROUP_WIDTH
S5_STATE = 64
S5_DT_MIN = 1e-3
S5_DT_MAX = 1e-1
RWKV_WIDTH = MIX_WIDTH - S5_WIDTH
RWKV_HEAD = 64
RWKV_HEADS = RWKV_WIDTH // RWKV_HEAD
RWKV_DECAY_LORA = max(32, int(round(1.8 * RWKV_WIDTH ** 0.5 / 32)) * 32)
RWKV_AAA_LORA = max(32, int(round(1.8 * RWKV_WIDTH ** 0.5 / 32)) * 32)
RWKV_GATE_LORA = max(32, int(round(0.6 * RWKV_WIDTH ** 0.8 / 32)) * 32)
RWKV_SPLITS = (RWKV_WIDTH, RWKV_WIDTH, RWKV_WIDTH, RWKV_DECAY_LORA, RWKV_AAA_LORA, RWKV_GATE_LORA)
RWKV_PROJ = sum(RWKV_SPLITS)
PROJ_WIDTH = S5_WIDTH + RWKV_PROJ
RWKV_GN_EPS = 64e-5
N_GROUPS = 8
EXPERTS_PER_GROUP = 8
N_EXPERTS = N_GROUPS * EXPERTS_PER_GROUP
TOP_K = 2
D_EXPERT = D_MODEL // 4
MOE_BLOCK = 256
RMS_EPS = 1e-6

kernel_name = "hymba_s5_rwkv7_hiermoe_block"

F32 = jnp.float32


def rms_norm(x, g):
    xf = x.astype(F32)
    y = xf * lax.rsqrt(jnp.mean(xf * xf, axis=-1, keepdims=True) + RMS_EPS)
    return (y * g.astype(F32)).astype(x.dtype)


def cmul(ar, ai, br, bi):
    return ar * br - ai * bi, ar * bi + ai * br


def linear_recurrence_combine(e_i, e_j):
    ai_r, ai_i, bi_r, bi_i = e_i
    aj_r, aj_i, bj_r, bj_i = e_j
    a_r, a_i = cmul(aj_r, aj_i, ai_r, ai_i)
    t_r, t_i = cmul(aj_r, aj_i, bi_r, bi_i)
    return a_r, a_i, t_r + bj_r, t_i + bj_i


def s5_mixer(u, lam_re, lam_im, log_step, b_re, b_im, c_re, c_im, d_skip, w_glu, b_glu):
    bsz, seq, width = u.shape
    n_chunks = seq // CHUNK
    G, P = S5_GROUPS, S5_STATE
    uf = u.astype(F32).reshape(bsz, n_chunks, CHUNK, G, S5_GROUP_WIDTH)
    lr = lam_re.astype(F32)
    li = lam_im.astype(F32)
    dt = jnp.exp(log_step.astype(F32))[:, None]
    mag = jnp.exp(lr * dt)
    ang = li * dt
    abar_re, abar_im = mag * jnp.cos(ang), mag * jnp.sin(ang)
    den = lr * lr + li * li
    num_re, num_im = abar_re - 1.0, abar_im
    f_re = (num_re * lr + num_im * li) / den
    f_im = (num_im * lr - num_re * li) / den
    bb_re, bb_im = cmul(f_re[..., None], f_im[..., None], b_re.astype(F32), b_im.astype(F32))
    bu_re = jnp.einsum('bctgh,gph->bctgp', uf, bb_re)
    bu_im = jnp.einsum('bctgh,gph->bctgp', uf, bb_im)
    a_shape = (1, 1, CHUNK, G, P)
    _, _, loc_re, loc_im = lax.associative_scan(
        linear_recurrence_combine,
        (jnp.broadcast_to(abar_re, a_shape), jnp.broadcast_to(abar_im, a_shape), bu_re, bu_im),
        axis=2)
    tau = jnp.arange(1, CHUNK + 1, dtype=F32)[:, None, None]
    pmag = jnp.exp(lr * dt * tau)
    pang = li * dt * tau
    apow_re, apow_im = pmag * jnp.cos(pang), pmag * jnp.sin(pang)
    c_shape = (1, n_chunks, G, P)
    _, _, car_re, car_im = lax.associative_scan(
        linear_recurrence_combine,
        (jnp.broadcast_to(apow_re[-1], c_shape), jnp.broadcast_to(apow_im[-1], c_shape),
         loc_re[:, :, -1], loc_im[:, :, -1]),
        axis=1)
    pad = ((0, 0), (1, 0), (0, 0), (0, 0))
    prev_re = jnp.pad(car_re, pad)[:, :-1]
    prev_im = jnp.pad(car_im, pad)[:, :-1]
    cor_re, cor_im = cmul(apow_re, apow_im, prev_re[:, :, None], prev_im[:, :, None])
    s_re = loc_re + cor_re
    s_im = loc_im + cor_im
    y = (jnp.einsum('ghp,bctgp->bctgh', c_re.astype(F32), s_re)
         - jnp.einsum('ghp,bctgp->bctgh', c_im.astype(F32), s_im))
    y = y.reshape(bsz, seq, width) + d_skip.astype(F32) * u.astype(F32)
    y = jax.nn.gelu(y)
    y = y * jax.nn.sigmoid(y @ w_glu.astype(F32) + b_glu.astype(F32))
    return y.astype(u.dtype)


def token_shift_lerp(z, mu):
    prev = jnp.pad(z, ((0, 0), (1, 0), (0, 0)))[:, :-1]
    return z + (prev - z) * mu


def rwkv7_recurrence(r, w, k, v, a_vec, b_vec):
    bsz, _, H, N = r.shape

    def step(S, inp):
        r_t, w_t, k_t, v_t, a_t, b_t = inp
        sa = jnp.einsum('bhvk,bhk->bhv', S, a_t)
        S = S * w_t[:, :, None, :] + sa[..., None] * b_t[:, :, None, :] + v_t[..., None] * k_t[:, :, None, :]
        return S, jnp.einsum('bhvk,bhk->bhv', S, r_t)

    xs = tuple(jnp.moveaxis(t, 1, 0) for t in (r, w, k, v, a_vec, b_vec))
    _, y = lax.scan(step, jnp.zeros((bsz, H, N, N), F32), xs)
    return jnp.moveaxis(y, 0, 1)


def rwkv7_mixer(z, mu, w0, w2, a0, a2, g2, k_k, k_a, r_k, ln_w, ln_b):
    bsz, seq, _ = z.shape
    H, N = RWKV_HEADS, RWKV_HEAD
    zs = token_shift_lerp(z, mu).astype(F32)
    r, k, v, xw, xa, xg = jnp.split(zs, np.cumsum(RWKV_SPLITS)[:-1].tolist(), axis=-1)
    w_log = -jax.nn.softplus(-(w0.astype(F32) + jnp.tanh(xw) @ w2.astype(F32))) - 0.5
    decay = jnp.exp(-jnp.exp(w_log))
    a = jax.nn.sigmoid(a0.astype(F32) + xa @ a2.astype(F32))
    g = jax.nn.sigmoid(xg) @ g2.astype(F32)
    heads = lambda t: t.reshape(bsz, seq, H, N)
    kk = heads(k * k_k.astype(F32))
    kk = kk / jnp.maximum(jnp.sqrt(jnp.sum(kk * kk, axis=-1, keepdims=True)), 1e-12)
    k = k * (1.0 + (a - 1.0) * k_a.astype(F32))
    rh, kh, vh = heads(r), heads(k), heads(v)
    y = rwkv7_recurrence(rh, heads(decay), kh, vh, -kk, kk * heads(a))
    mean = jnp.mean(y, axis=-1, keepdims=True)
    var = jnp.mean(jnp.square(y - mean), axis=-1, keepdims=True)
    yn = ((y - mean) * lax.rsqrt(var + RWKV_GN_EPS)).reshape(bsz, seq, RWKV_WIDTH)
    yn = yn * ln_w.astype(F32) + ln_b.astype(F32)
    bonus = jnp.sum(rh * kh * r_k.astype(F32), axis=-1, keepdims=True) * vh
    out = (yn + bonus.reshape(bsz, seq, RWKV_WIDTH)) * g
    return out.astype(z.dtype)


def hier_moe(h, w_route_grp, b_route_grp, w_route_exp, b_route_exp, w_gate, w_up, w_down):
    bsz, seq, D = h.shape
    T = bsz * seq
    xt = h.reshape(T, D)
    grp_prob = jax.nn.softmax((xt @ w_route_grp).astype(F32) + b_route_grp.astype(F32), axis=-1)
    grp_p, grp_idx = lax.top_k(grp_prob, 1)
    exp_logits = ((xt @ w_route_exp).astype(F32) + b_route_exp.astype(F32)).reshape(T, N_GROUPS, EXPERTS_PER_GROUP)
    sel_logits = jnp.take_along_axis(exp_logits, grp_idx[:, :, None], axis=1)[:, 0]
    top_p, top_j = lax.top_k(jax.nn.softmax(sel_logits, axis=-1), TOP_K)
    top_p = top_p / jnp.sum(top_p, axis=-1, keepdims=True)
    gate = (grp_p * top_p).reshape(-1)
    eid = (grp_idx * EXPERTS_PER_GROUP + top_j).reshape(-1).astype(jnp.int32)
    tok = jnp.repeat(jnp.arange(T, dtype=jnp.int32), TOP_K)
    S = T * TOP_K
    order = jnp.argsort(eid)
    e_sorted = eid[order]
    counts = jnp.bincount(eid, length=N_EXPERTS)
    padded = (counts + MOE_BLOCK - 1) // MOE_BLOCK * MOE_BLOCK
    start = jnp.cumsum(counts) - counts
    pend = jnp.cumsum(padded)
    pstart = pend - padded
    dest = pstart[e_sorted] + jnp.arange(S, dtype=jnp.int32) - start[e_sorted]
    n_blocks = -(-S // MOE_BLOCK) + N_EXPERTS
    rows = n_blocks * MOE_BLOCK
    buf_tok = jnp.full((rows,), T, jnp.int32).at[dest].set(tok[order])
    buf_gate = jnp.zeros((rows,), F32).at[dest].set(gate[order])
    blk_exp = jnp.minimum(jnp.searchsorted(pend, jnp.arange(n_blocks) * MOE_BLOCK, side='right'),
                          N_EXPERTS - 1).astype(jnp.int32)
    x_pad = jnp.concatenate([xt, jnp.zeros((1, D), xt.dtype)], axis=0)
    xb = x_pad[buf_tok].reshape(n_blocks, MOE_BLOCK, D)

    def expert_block(args):
        xblk, e = args
        return (jax.nn.silu(xblk @ w_gate[e]) * (xblk @ w_up[e])) @ w_down[e]

    yb = lax.map(expert_block, (xb, blk_exp)).reshape(rows, D).astype(F32) * buf_gate[:, None]
    y = jnp.zeros((T + 1, D), F32).at[buf_tok].add(yb)[:T]
    return y.reshape(bsz, seq, D).astype(h.dtype)


def setup_inputs(seed: int = 0) -> dict:
    key = jax.random.key(seed)
    keys = jax.random.split(key, 48)
    cnt = [0]

    def nk():
        cnt[0] += 1
        return keys[cnt[0] - 1]

    def nrm(shape, scale):
        return jax.random.normal(nk(), shape, F32) * scale

    def uni(shape, lo, hi):
        return jax.random.uniform(nk(), shape, F32, lo, hi)

    L, G, P, Hg = DEPTH, S5_GROUPS, S5_STATE, S5_GROUP_WIDTH
    RW, NH, N = RWKV_WIDTH, RWKV_HEADS, RWKV_HEAD
    return {
        "x": nrm((BATCH, SEQ, D_MODEL), 1.0),
        "norm_mix_g": 1.0 + nrm((L, D_MODEL), 0.02),
        "w_in": nrm((L, D_MODEL, PROJ_WIDTH), D_MODEL ** -0.5),
        "s5_lambda_re": -0.5 + nrm((L, G, P), 0.01),
        "s5_lambda_im": jnp.pi * jnp.arange(P, dtype=F32) + nrm((L, G, P), 0.01),
        "s5_log_step": uni((L, G), math.log(S5_DT_MIN), math.log(S5_DT_MAX)),
        "s5_b_re": nrm((L, G, P, Hg), (2 * Hg) ** -0.5),
        "s5_b_im": nrm((L, G, P, Hg), (2 * Hg) ** -0.5),
        "s5_c_re": nrm((L, G, Hg, P), P ** -0.5),
        "s5_c_im": nrm((L, G, Hg, P), P ** -0.5),
        "s5_d": nrm((L, S5_WIDTH), 1.0),
        "s5_w_glu": nrm((L, S5_WIDTH, S5_WIDTH), S5_WIDTH ** -0.5),
        "s5_b_glu": nrm((L, S5_WIDTH), 0.01),
        "rwkv_mu": uni((L, RWKV_PROJ), 0.0, 1.0),
        "rwkv_w0": uni((L, RW), -5.0, -1.0),
        "rwkv_w2": nrm((L, RWKV_DECAY_LORA, RW), 0.1),
        "rwkv_a0": nrm((L, RW), 0.1),
        "rwkv_a2": nrm((L, RWKV_AAA_LORA, RW), 0.1),
        "rwkv_g2": nrm((L, RWKV_GATE_LORA, RW), RWKV_GATE_LORA ** -0.5),
        "rwkv_k_k": 0.85 + nrm((L, RW), 0.02),
        "rwkv_k_a": 1.0 + nrm((L, RW), 0.02),
        "rwkv_r_k": nrm((L, NH, N), 0.1),
        "rwkv_ln_w": 1.0 + nrm((L, RW), 0.02),
        "rwkv_ln_b": nrm((L, RW), 0.01),
        "w_out": nrm((L, MIX_WIDTH, D_MODEL), MIX_WIDTH ** -0.5),
        "norm_ffn_g": 1.0 + nrm((L, D_MODEL), 0.02),
        "w_route_grp": nrm((L, D_MODEL, N_GROUPS), D_MODEL ** -0.5),
        "b_route_grp": nrm((L, N_GROUPS), 0.01),
        "w_route_exp": nrm((L, D_MODEL, N_EXPERTS), D_MODEL ** -0.5),
        "b_route_exp": nrm((L, N_EXPERTS), 0.01),
        "w_gate": nrm((L, N_EXPERTS, D_MODEL, D_EXPERT), D_MODEL ** -0.5),
        "w_up": nrm((L, N_EXPERTS, D_MODEL, D_EXPERT), D_MODEL ** -0.5),
        "w_down": nrm((L, N_EXPERTS, D_EXPERT, D_MODEL), D_EXPERT ** -0.5),
        "norm_final_g": 1.0 + nrm((D_MODEL,), 0.02),
    }


def reference(x, norm_mix_g, w_in, s5_lambda_re, s5_lambda_im, s5_log_step, s5_b_re, s5_b_im,
              s5_c_re, s5_c_im, s5_d, s5_w_glu, s5_b_glu, rwkv_mu, rwkv_w0, rwkv_w2, rwkv_a0,
              rwkv_a2, rwkv_g2, rwkv_k_k, rwkv_k_a, rwkv_r_k, rwkv_ln_w, rwkv_ln_b, w_out,
              norm_ffn_g, w_route_grp, b_route_grp, w_route_exp, b_route_exp, w_gate, w_up,
              w_down, norm_final_g):
    h = x
    for l in range(DEPTH):
        hn = rms_norm(h, norm_mix_g[l])
        proj = hn @ w_in[l]
        u_s5 = proj[..., :S5_WIDTH]
        z_rw = proj[..., S5_WIDTH:]
        y_s5 = s5_mixer(u_s5, s5_lambda_re[l], s5_lambda_im[l], s5_log_step[l], s5_b_re[l],
                        s5_b_im[l], s5_c_re[l], s5_c_im[l], s5_d[l], s5_w_glu[l], s5_b_glu[l])
        y_rw = rwkv7_mixer(z_rw, rwkv_mu[l], rwkv_w0[l], rwkv_w2[l], rwkv_a0[l], rwkv_a2[l],
                           rwkv_g2[l], rwkv_k_k[l], rwkv_k_a[l], rwkv_r_k[l], rwkv_ln_w[l],
                           rwkv_ln_b[l])
        mixed = jnp.concatenate([y_s5.astype(h.dtype), y_rw.astype(h.dtype)], axis=-1)
        h = h + mixed @ w_out[l]
        h = h + hier_moe(rms_norm(h, norm_ffn_g[l]), w_route_grp[l], b_route_grp[l],
                         w_route_exp[l], b_route_exp[l], w_gate[l], w_up[l], w_down[l])
    return rms_norm(h, norm_final_g)
```

```python
import functools

import jax
import jax.numpy as jnp
from jax import lax
from jax.experimental import pallas as pl
from jax.experimental.pallas import tpu as pltpu

F32 = jnp.float32
BF16 = jnp.bfloat16
I32 = jnp.int32

RMS_EPS = 1e-6
S5_CHUNK = 64
S5_HG = 16
S5_P = 64
RW_N = 64
RW_CHUNK = 64
RW_GN_EPS = 64e-5
N_GROUPS = 8
EPG = 8
N_EXPERTS = 64
MOE_BLOCK = 256
LANES = 128
VMEM_LIMIT = 56 * 1024 * 1024


def _cparams(*sem):
    return pltpu.CompilerParams(dimension_semantics=sem, vmem_limit_bytes=VMEM_LIMIT)


def _split2(x):
    hi = x.astype(BF16)
    lo = (x - hi.astype(F32)).astype(BF16)
    return hi, lo


def _split3(x):
    hi = x.astype(BF16)
    r = x - hi.astype(F32)
    mid = r.astype(BF16)
    lo = (r - mid.astype(F32)).astype(BF16)
    return hi, mid, lo


def _dot(a, b):
    return jnp.dot(a, b, preferred_element_type=F32)


def _dot_nt(a, b):
    return lax.dot_general(a, b, (((1,), (1,)), ((), ())), preferred_element_type=F32)


def _dot_tn(a, b):
    return lax.dot_general(a, b, (((0,), (0,)), ((), ())), preferred_element_type=F32)


def _dot_x3(a, b):
    ah, al = _split2(a)
    bh, bl = _split2(b)
    return _dot(ah, bh) + (_dot(ah, bl) + _dot(al, bh))


def _dot_exact_lhs(a_bf16, b):
    bh, bm, bl = _split3(b)
    return _dot(a_bf16, bh) + (_dot(a_bf16, bm) + _dot(a_bf16, bl))


def _cmul(ar, ai, br, bi):
    return ar * br - ai * bi, ar * bi + ai * br


def _inproj_kernel(x_ref, g_ref, w_ref, o_ref, hn_ref):
    @pl.when(pl.program_id(1) == 0)
    def _():
        x = x_ref[...]
        ms = jnp.mean(x * x, axis=-1, keepdims=True)
        hn_ref[...] = (x * lax.rsqrt(ms + RMS_EPS) * g_ref[...]).astype(BF16)

    o_ref[...] = _dot(hn_ref[...], w_ref[...])


def _inproj(x2, g, w_bf16, tm, tn):
    T, D = x2.shape
    N = w_bf16.shape[1]
    return pl.pallas_call(
        _inproj_kernel,
        grid=(T // tm, N // tn),
        in_specs=[
            pl.BlockSpec((tm, D), lambda i, j: (i, 0)),
            pl.BlockSpec((1, D), lambda i, j: (0, 0)),
            pl.BlockSpec((D, tn), lambda i, j: (0, j)),
        ],
        out_specs=pl.BlockSpec((tm, tn), lambda i, j: (i, j)),
        out_shape=jax.ShapeDtypeStruct((T, N), F32),
        scratch_shapes=[pltpu.VMEM((tm, D), BF16)],
        compiler_params=_cparams("arbitrary", "arbitrary"),
        name="inproj",
    )(x2, g, w_bf16)


def _binpow(ar, ai, expo, nbits):
    pr = jnp.ones(expo.shape, F32)
    pi = jnp.zeros(expo.shape, F32)
    sr, si = ar, ai
    for bit in range(nbits):
        m = ((expo >> bit) & 1) == 1
        nr, ni = _cmul(pr, pi, sr, si)
        pr = jnp.where(m, nr, pr)
        pi = jnp.where(m, ni, pi)
        if bit + 1 < nbits:
            sr, si = _cmul(sr, si, sr, si)
    return pr, pi


def _zoh(lr, li, dt):
    mag = jnp.exp(lr * dt)
    ang = li * dt
    ar, ai = mag * jnp.cos(ang), mag * jnp.sin(ang)
    den = lr * lr + li * li
    nr, ni = ar - 1.0, ai
    fr = (nr * lr + ni * li) / den
    fi = (ni * lr - nr * li) / den
    return ar, ai, fr, fi


def _s5_ops_kernel(lrc_ref, lic_ref, lrr_ref, lir_ref, dt_ref, ct_re_ref, ct_im_ref,
                   bt_re_ref, bt_im_ref, btr_re_ref, btr_im_ref,
                   mt_ref, wbt_ref, wc_ref, a64_ref):
    C, HG, P = S5_CHUNK, S5_HG, S5_P
    W = C * HG
    dt = jnp.exp(dt_ref[0])
    ar_c, ai_c, fr_c, fi_c = _zoh(lrc_ref[0], lic_ref[0], dt)
    ar_r, ai_r, fr_r, fi_r = _zoh(lrr_ref[0], lir_ref[0], dt)

    tau = lax.broadcasted_iota(I32, (1, W), 1) // HG
    nbits = max(1, (C - 1).bit_length())
    pr, pi = _binpow(ar_c, ai_c, jnp.broadcast_to(tau, (P, W)), nbits)
    qr, qi = _binpow(ar_c, ai_c, jnp.broadcast_to(C - 1 - tau, (P, W)), nbits)

    ca_re, ca_im = _cmul(ct_re_ref[0], ct_im_ref[0], pr, pi)
    bbt_re, bbt_im = _cmul(btr_re_ref[0], btr_im_ref[0], fr_r, fi_r)
    kt = _dot_x3(bbt_re, ca_re) - _dot_x3(bbt_im, ca_im)

    c1_re, c1_im = _cmul(ca_re, ca_im, ar_c, ai_c)
    wc_ref[0, :P, :] = c1_re.astype(BF16)
    wc_ref[0, P:, :] = (-c1_im).astype(BF16)

    bb_re, bb_im = _cmul(bt_re_ref[0], bt_im_ref[0], fr_c, fi_c)
    ab_re, ab_im = _cmul(bb_re, bb_im, qr, qi)
    wbt_ref[0, :P, :] = ab_re.astype(BF16)
    wbt_ref[0, P:, :] = ab_im.astype(BF16)

    cr, ci = ar_r, ai_r
    for _ in range(C.bit_length() - 1):
        cr, ci = _cmul(cr, ci, cr, ci)
    a64_ref[0] = jnp.concatenate([cr, ci], axis=1)

    ext = jnp.concatenate([jnp.zeros((HG, W), F32), kt], axis=1)
    for s in range(C):
        off = W - s * HG
        mt_ref[0, s * HG:(s + 1) * HG, :] = ext[:, off:off + W].astype(BF16)


def _s5_ops(lam_re, lam_im, log_step, ct_re, ct_im, bt_re, bt_im, btr_re, btr_im):
    G, P = lam_re.shape
    C, HG = S5_CHUNK, S5_HG
    W = C * HG
    col = lambda a: a.reshape(G, P, 1)
    row = lambda a: a.reshape(G, 1, P)
    g3 = lambda s1, s2: pl.BlockSpec((1, s1, s2), lambda g: (g, 0, 0))
    return pl.pallas_call(
        _s5_ops_kernel,
        grid=(G,),
        in_specs=[g3(P, 1), g3(P, 1), g3(1, P), g3(1, P), g3(1, 1),
                  g3(P, W), g3(P, W), g3(P, W), g3(P, W), g3(HG, P), g3(HG, P)],
        out_specs=[g3(W, W), g3(2 * P, W), g3(2 * P, W), g3(1, 2 * P)],
        out_shape=[jax.ShapeDtypeStruct((G, W, W), BF16),
                   jax.ShapeDtypeStruct((G, 2 * P, W), BF16),
                   jax.ShapeDtypeStruct((G, 2 * P, W), BF16),
                   jax.ShapeDtypeStruct((G, 1, 2 * P), F32)],
        compiler_params=_cparams("arbitrary"),
        name="s5_ops",
    )(col(lam_re), col(lam_im), row(lam_re), row(lam_im), log_step.reshape(G, 1, 1),
      ct_re, ct_im, bt_re, bt_im, btr_re, btr_im)


def _s5_chunk_kernel(u_ref, mt_ref, wbt_ref, wc_ref, a64_ref, y_ref, *, n_chunks):
    P = S5_P
    u = u_ref[0]
    R = u.shape[0]
    y = _dot(u, mt_ref[0])
    x = _dot_nt(u, wbt_ref[0])
    a = a64_ref[0]
    lane = lax.broadcasted_iota(I32, (1, 2 * P), 1)
    ar = jnp.where(lane < P, a, pltpu.roll(a, P, 1))
    ai = jnp.where(lane < P, -pltpu.roll(a, P, 1), a)
    cidx = lax.broadcasted_iota(I32, (R, 1), 0) % n_chunks
    shift = 1
    while shift < n_chunks:
        xs = jnp.where(cidx >= shift, pltpu.roll(x, shift, 0), 0.0)
        x = x + ar * xs + ai * pltpu.roll(xs, P, 1)
        nar = ar * ar - ai * ai
        ai = 2.0 * ar * ai
        ar = nar
        shift *= 2
    s_in = jnp.where(cidx >= 1, pltpu.roll(x, 1, 0), 0.0)
    sh, sl = _split2(s_in)
    wc = wc_ref[0]
    y_ref[0] = y + (_dot(sh, wc) + _dot(sl, wc))


def _s5_chunk(u_g, mt, wbt, wc, a64, n_chunks):
    G, R, W = u_g.shape
    P2 = wbt.shape[1]
    g3 = lambda s1, s2: pl.BlockSpec((1, s1, s2), lambda g: (g, 0, 0))
    return pl.pallas_call(
        functools.partial(_s5_chunk_kernel, n_chunks=n_chunks),
        grid=(G,),
        in_specs=[g3(R, W), g3(W, W), g3(P2, W), g3(P2, W), g3(1, P2)],
        out_specs=g3(R, W),
        out_shape=jax.ShapeDtypeStruct((G, R, W), F32),
        compiler_params=_cparams("arbitrary"),
        name="s5_chunk",
    )(u_g, mt, wbt, wc, a64)


def _gelu_tanh(x):
    return 0.5 * x * (1.0 + jnp.tanh(0.7978845608028654 * (x + 0.044715 * (x * x * x))))


def _s5_glu_kernel(y_ref, u_ref, d_ref, w_ref, b_ref, o_ref):
    y = y_ref[...] + d_ref[...] * u_ref[...]
    y = _gelu_tanh(y)
    z = _dot(y.astype(BF16), w_ref[...]) + b_ref[...]
    o_ref[...] = (y * jax.nn.sigmoid(z)).astype(BF16)


def _s5_glu(y_ssm, proj, d, w_bf16, b, tm):
    T, W = y_ssm.shape
    return pl.pallas_call(
        _s5_glu_kernel,
        grid=(T // tm,),
        in_specs=[
            pl.BlockSpec((tm, W), lambda i: (i, 0)),
            pl.BlockSpec((tm, W), lambda i: (i, 0)),
            pl.BlockSpec((1, W), lambda i: (0, 0)),
            pl.BlockSpec((W, W), lambda i: (0, 0)),
            pl.BlockSpec((1, W), lambda i: (0, 0)),
        ],
        out_specs=pl.BlockSpec((tm, W), lambda i: (i, 0)),
        out_shape=jax.ShapeDtypeStruct((T, W), BF16),
        compiler_params=_cparams("arbitrary"),
        name="s5_glu",
    )(y_ssm, proj, d, w_bf16, b)


def _head_sum(x, e_ref, et_ref):
    xh, xl = _split2(x)
    e = e_ref[...]
    s = _dot(xh, e) + _dot(xl, e)
    sh, sl = _split2(s)
    et = et_ref[...]
    return _dot(sh, et) + _dot(sl, et)


def _shift(z, prev_row):
    rolled = pltpu.roll(z, 1, 0)
    first = lax.broadcasted_iota(I32, (z.shape[0], 1), 0) == 0
    return jnp.where(first, prev_row, rolled)


def _rwkv_prep_kernel(zr_ref, zk_ref, zv_ref, zl_ref, mu_ref, mul_ref, w0_ref, a0_ref, kk_ref, ka_ref,
                      w2_ref, a2_ref, g2_ref, e_ref, et_ref,
                      r_ref, k_ref, v_ref, nk_ref, b_ref, lw_ref, g_ref,
                      car_ref, carl_ref):
    W = r_ref.shape[1]

    @pl.when(pl.program_id(1) == 0)
    def _():
        car_ref[...] = jnp.zeros_like(car_ref)
        carl_ref[...] = jnp.zeros_like(carl_ref)

    def lerp(z, prev_row, mu):
        return z + (_shift(z, prev_row) - z) * mu

    tm = zr_ref.shape[0]
    zr, zk, zv, zl = zr_ref[...], zk_ref[...], zv_ref[...], zl_ref[...]
    r = lerp(zr, car_ref[0:1, :], mu_ref[0:1, :])
    k = lerp(zk, car_ref[1:2, :], mu_ref[1:2, :])
    v = lerp(zv, car_ref[2:3, :], mu_ref[2:3, :])
    xl = lerp(zl, carl_ref[0:1, :], mul_ref[...])
    car_ref[0:1, :] = zr[tm - 1:tm, :]
    car_ref[1:2, :] = zk[tm - 1:tm, :]
    car_ref[2:3, :] = zv[tm - 1:tm, :]
    carl_ref[0:1, :] = zl[tm - 1:tm, :]

    xw, xa, xg = xl[:, 0:128], xl[:, 128:256], xl[:, 256:512]
    dw = _dot(jnp.tanh(xw).astype(BF16), w2_ref[...])
    da = _dot(xa.astype(BF16), a2_ref[...])
    g = _dot(jax.nn.sigmoid(xg).astype(BF16), g2_ref[...])

    zw = -(w0_ref[...] + dw)
    softplus = jnp.maximum(zw, 0.0) + jnp.log(1.0 + jnp.exp(-jnp.abs(zw)))
    w_log = -softplus - 0.5
    a = jax.nn.sigmoid(a0_ref[...] + da)

    kk = k * kk_ref[...]
    n2 = _head_sum(kk * kk, e_ref, et_ref)
    kk = kk / jnp.maximum(jnp.sqrt(n2), 1e-12)

    r_ref[...] = r
    k_ref[...] = k * (1.0 + (a - 1.0) * ka_ref[...])
    v_ref[...] = v
    nk_ref[...] = kk
    b_ref[...] = kk * a
    lw_ref[...] = -jnp.exp(w_log)
    g_ref[...] = g


def _rwkv_prep(proj, B, L, mu3, mul, w0, a0, k_k, k_a, w2p, a2p, g2p, e_mat, et_mat, tm):
    T = B * L
    W = w0.shape[1]
    nt = L // tm
    row = lambda c: pl.BlockSpec((tm, W), lambda b, i, c=c: (b * nt + i, c))
    full = lambda a: pl.BlockSpec(a.shape, lambda b, i: (0, 0))
    out = pl.BlockSpec((tm, W), lambda b, i: (b * nt + i, 0))
    lw = 512
    return pl.pallas_call(
        _rwkv_prep_kernel,
        grid=(B, nt),
        in_specs=[row(1), row(2), row(3),
                  pl.BlockSpec((tm, lw), lambda b, i: (b * nt + i, 4 * W // lw)),
                  full(mu3), full(mul), full(w0), full(a0), full(k_k), full(k_a),
                  full(w2p), full(a2p), full(g2p), full(e_mat), full(et_mat)],
        out_specs=[out] * 7,
        out_shape=[jax.ShapeDtypeStruct((T, W), F32)] * 7,
        scratch_shapes=[pltpu.VMEM((8, W), F32), pltpu.VMEM((8, lw), F32)],
        compiler_params=_cparams("arbitrary", "arbitrary"),
        name="rwkv_prep",
    )(proj, proj, proj, proj, mu3, mul, w0, a0, k_k, k_a, w2p, a2p, g2p, e_mat, et_mat)


def _rwkv_chunk_kernel(r_ref, k_ref, v_ref, nk_ref, b_ref, lw_ref, y_ref, z_ref):
    C, N = RW_CHUNK, RW_N
    H = r_ref.shape[1] // N

    @pl.when(pl.program_id(1) == 0)
    def _():
        z_ref[...] = jnp.zeros_like(z_ref)

    ri = lax.broadcasted_iota(I32, (C, C), 0)
    ci = lax.broadcasted_iota(I32, (C, C), 1)
    tril = ri >= ci
    stril = ri > ci
    eye = (ri == ci).astype(F32)

    lw = lw_ref[...]
    cs = _dot_exact_lhs(tril.astype(BF16), lw)
    tot = cs[C - 1:C, :]
    p_inc = jnp.exp(cs)
    p_inv = jnp.exp(-cs)
    p_prev = jnp.exp(cs - lw)
    p_rest = jnp.exp(tot - cs)
    p_tot = jnp.exp(tot)

    r_t = r_ref[...] * p_inc
    k_t = k_ref[...] * p_inv
    a_t = -nk_ref[...] * p_prev
    b_t = b_ref[...] * p_inv
    k_h = k_ref[...] * p_rest
    b_h = b_ref[...] * p_rest
    v_all = v_ref[...]

    for h in range(H):
        sl = slice(h * N, (h + 1) * N)
        v = v_all[:, sl].astype(BF16)
        ar = jnp.concatenate([a_t[:, sl], r_t[:, sl]], axis=0).astype(BF16)
        bk = jnp.concatenate([b_t[:, sl], k_t[:, sl]], axis=0).astype(BF16)
        m = _dot_nt(ar, bk)
        l_ab = jnp.where(stril, m[:C, :C], 0.0)
        l_ak = jnp.where(stril, m[:C, C:], 0.0)
        m_rb = jnp.where(tril, m[C:, :C], 0.0)
        m_rk = jnp.where(tril, m[C:, C:], 0.0)

        x = jnp.concatenate([_dot(l_ak.astype(BF16), v), a_t[:, sl]], axis=1)
        lp = l_ab
        step = 1
        while step < C:
            lpb = lp.astype(BF16)
            x = x + _dot(lpb, x.astype(BF16))
            step *= 2
            if step < C:
                lp = _dot(lpb, lpb)
        xb = x.astype(BF16)
        yq = _dot(m_rb.astype(BF16), xb)
        y0 = yq[:, :N] + _dot(m_rk.astype(BF16), v)
        q = yq[:, N:] + r_t[:, sl]
        gh = _dot_tn(b_h[:, sl].astype(BF16), xb)
        hmat = gh[:, :N] + _dot_tn(k_h[:, sl].astype(BF16), v)
        gmat = gh[:, N:] + eye * p_tot[:, sl]

        z = z_ref[h]
        zh, zl = _split2(z)
        qb = q.astype(BF16)
        y_ref[:, sl] = y0 + (_dot(qb, zh) + _dot(qb, zl))
        gb = gmat.astype(BF16)
        z_ref[h] = hmat + (_dot(gb, zh) + _dot(gb, zl))


def _rwkv_chunk(r, k, v, nk, b, lw, B, L):
    T, W = r.shape
    C = RW_CHUNK
    nc = L // C
    H = W // RW_N
    spec = pl.BlockSpec((C, W), lambda bi, c: (bi * nc + c, 0))
    return pl.pallas_call(
        _rwkv_chunk_kernel,
        grid=(B, nc),
        in_specs=[spec] * 6,
        out_specs=spec,
        out_shape=jax.ShapeDtypeStruct((T, W), F32),
        scratch_shapes=[pltpu.VMEM((H, RW_N, RW_N), F32)],
        compiler_params=_cparams("arbitrary", "arbitrary"),
        name="rwkv_chunk",
    )(r, k, v, nk, b, lw)


def _rwkv_post_kernel(y_ref, r_ref, k_ref, v_ref, g_ref, rk_ref, lnw_ref, lnb_ref, e_ref, et_ref, o_ref):
    inv_n = 1.0 / RW_N
    y = y_ref[...]
    mean = _head_sum(y, e_ref, et_ref) * inv_n
    yc = y - mean
    var = _head_sum(yc * yc, e_ref, et_ref) * inv_n
    yn = yc * lax.rsqrt(var + RW_GN_EPS) * lnw_ref[...] + lnb_ref[...]
    v = v_ref[...]
    bonus = _head_sum(r_ref[...] * k_ref[...] * rk_ref[...], e_ref, et_ref) * v
    o_ref[...] = ((yn + bonus) * g_ref[...]).astype(BF16)


def _rwkv_post(y, r, k, v, g, r_k, ln_w, ln_b, e_mat, et_mat, tm):
    T, W = y.shape
    row = pl.BlockSpec((tm, W), lambda i: (i, 0))
    full = lambda a: pl.BlockSpec(a.shape, lambda i: (0, 0))
    return pl.pallas_call(
        _rwkv_post_kernel,
        grid=(T // tm,),
        in_specs=[row] * 5 + [full(r_k), full(ln_w), full(ln_b), full(e_mat), full(et_mat)],
        out_specs=row,
        out_shape=jax.ShapeDtypeStruct((T, W), BF16),
        compiler_params=_cparams("arbitrary"),
        name="rwkv_post",
    )(y, r, k, v, g, r_k, ln_w, ln_b, e_mat, et_mat)


def _first_index_of_max(vals, lane, valid):
    neg = jnp.float32(-jnp.inf)
    masked = jnp.where(valid, vals, neg)
    m = jnp.max(masked, axis=-1, keepdims=True)
    idx = jnp.min(jnp.where(valid & (masked == m), lane, LANES), axis=-1, keepdims=True)
    return m, idx


def _outproj_kernel(s5_ref, rw_ref, x_ref, wt_ref, wb_ref, g_ref, wrh_ref, wrl_ref, br_ref,
                    h_ref, hn_ref, rec_ref):
    h = x_ref[...] + (_dot(s5_ref[...], wt_ref[...]) + _dot(rw_ref[...], wb_ref[...]))
    h_ref[...] = h
    ms = jnp.mean(h * h, axis=-1, keepdims=True)
    hn = h * lax.rsqrt(ms + RMS_EPS) * g_ref[...]
    hn_ref[...] = hn.reshape(hn_ref.shape)

    hh, hl = _split2(hn)
    wh, wl = wrh_ref[...], wrl_ref[...]
    logits = _dot(hh, wh) + (_dot(hh, wl) + _dot(hl, wh)) + br_ref[...]
    lane = lax.broadcasted_iota(I32, logits.shape, 1)
    is_grp = (lane >= N_EXPERTS) & (lane < N_EXPERTS + N_GROUPS)
    gmax, gidx = _first_index_of_max(logits, lane, is_grp)
    gsum = jnp.sum(jnp.where(is_grp, jnp.exp(logits - gmax), 0.0), axis=-1, keepdims=True)
    p_grp = 1.0 / gsum
    grp = gidx - N_EXPERTS
    in_grp = (lane >= grp * EPG) & (lane < (grp + 1) * EPG)
    m1, i1 = _first_index_of_max(logits, lane, in_grp)
    m2, i2 = _first_index_of_max(logits, lane, in_grp & (lane != i1))
    e = jnp.exp(m2 - m1)
    g1 = p_grp / (1.0 + e)
    g2 = p_grp * e / (1.0 + e)
    rec = jnp.where(lane == 0, i1.astype(F32),
          jnp.where(lane == 1, i2.astype(F32),
          jnp.where(lane == 2, g1, jnp.where(lane == 3, g2, 0.0))))
    rec_ref[...] = rec


def _outproj(s5o, rwo, x2, w_top, w_bot, g, wr_hi, wr_lo, b_route, tm):
    T, D = x2.shape
    W = s5o.shape[1]
    full = lambda a: pl.BlockSpec(a.shape, lambda i: (0, 0))
    return pl.pallas_call(
        _outproj_kernel,
        grid=(T // tm,),
        in_specs=[pl.BlockSpec((tm, W), lambda i: (i, 0)), pl.BlockSpec((tm, W), lambda i: (i, 0)),
                  pl.BlockSpec((tm, D), lambda i: (i, 0)),
                  full(w_top), full(w_bot), full(g), full(wr_hi), full(wr_lo), full(b_route)],
        out_specs=[pl.BlockSpec((tm, D), lambda i: (i, 0)), pl.BlockSpec((tm, 1, D), lambda i: (i, 0, 0)),
                   pl.BlockSpec((tm, LANES), lambda i: (i, 0))],
        out_shape=[jax.ShapeDtypeStruct((T, D), F32), jax.ShapeDtypeStruct((T, 1, D), F32),
                   jax.ShapeDtypeStruct((T, LANES), F32)],
        compiler_params=_cparams("arbitrary"),
        name="outproj_route",
    )(s5o, rwo, x2, w_top, w_bot, g, wr_hi, wr_lo, b_route)


def _onehots(rec, lane):
    oh0 = (lane == rec[:, 0:1].astype(I32)).astype(F32)
    oh1 = (lane == rec[:, 1:2].astype(I32)).astype(F32)
    return oh0, oh1


def _rank_kernel(rec_ref, rank_ref, cnt_ref, base_ref):
    tb = rec_ref.shape[0]

    @pl.when(pl.program_id(0) == 0)
    def _():
        base_ref[...] = jnp.zeros_like(base_ref)

    lane = lax.broadcasted_iota(I32, (tb, LANES), 1)
    oh0, oh1 = _onehots(rec_ref[...], lane)
    both = oh0 + oh1
    ri = lax.broadcasted_iota(I32, (tb, tb), 0)
    ci = lax.broadcasted_iota(I32, (tb, tb), 1)
    before = _dot((ri > ci).astype(BF16), both.astype(BF16)) + base_ref[0:1, :]
    rank0 = jnp.sum(oh0 * before, axis=-1, keepdims=True)
    rank1 = jnp.sum(oh1 * before, axis=-1, keepdims=True)
    rank_ref[...] = jnp.where(lane == 0, rank0, jnp.where(lane == 1, rank1, 0.0))
    total = base_ref[0:1, :] + jnp.sum(both, axis=0, keepdims=True)
    base_ref[0:1, :] = total
    cnt_ref[...] = jnp.broadcast_to(total, cnt_ref.shape)


def _rank(rec, tb):
    T = rec.shape[0]
    return pl.pallas_call(
        _rank_kernel,
        grid=(T // tb,),
        in_specs=[pl.BlockSpec((tb, LANES), lambda i: (i, 0))],
        out_specs=[pl.BlockSpec((tb, LANES), lambda i: (i, 0)), pl.BlockSpec((8, LANES), lambda i: (0, 0))],
        out_shape=[jax.ShapeDtypeStruct((T, LANES), F32), jax.ShapeDtypeStruct((8, LANES), F32)],
        scratch_shapes=[pltpu.VMEM((8, LANES), F32)],
        compiler_params=_cparams("arbitrary"),
        name="moe_rank",
    )(rec)


def _padded_starts(cnt):
    padded = jnp.ceil(cnt * (1.0 / MOE_BLOCK)) * MOE_BLOCK
    ri = lax.broadcasted_iota(I32, (LANES, LANES), 0)
    ci = lax.broadcasted_iota(I32, (LANES, LANES), 1)
    p8 = jnp.broadcast_to(padded, (8, LANES))
    pend = _dot_exact_lhs_rhs(p8, (ri <= ci).astype(BF16))[0:1, :]
    return pend - padded, pend


def _dot_exact_lhs_rhs(a, b_bf16):
    ah, am, al = _split3(a)
    return _dot(ah, b_bf16) + (_dot(am, b_bf16) + _dot(al, b_bf16))


def _dest_kernel(rec_ref, rank_ref, cnt_ref, dest_ref, blk_ref):
    tb = rec_ref.shape[0]
    cnt = cnt_ref[0:1, :]
    pstart, pend = _padded_starts(cnt)
    lane = lax.broadcasted_iota(I32, (tb, LANES), 1)
    oh0, oh1 = _onehots(rec_ref[...], lane)
    rank = rank_ref[...]
    d0 = jnp.sum(oh0 * pstart, axis=-1, keepdims=True) + rank[:, 0:1]
    d1 = jnp.sum(oh1 * pstart, axis=-1, keepdims=True) + rank[:, 1:2]
    dest_ref[...] = jnp.where(lane == 0, d0, jnp.where(lane == 1, d1, 0.0)).astype(I32)

    @pl.when(pl.program_id(0) == 0)
    def _():
        nb = blk_ref.shape[0]
        blane = lax.broadcasted_iota(I32, (nb, LANES), 1)
        bstart = (lax.broadcasted_iota(I32, (nb, 1), 0) * MOE_BLOCK).astype(F32)
        is_e = blane < N_EXPERTS
        bexp = jnp.sum(jnp.where(is_e & (pend <= bstart), 1.0, 0.0), axis=-1, keepdims=True)
        bexp = jnp.minimum(bexp, N_EXPERTS - 1.0)
        nact = jnp.max(jnp.where(is_e, pend, 0.0), axis=-1, keepdims=True) * (1.0 / MOE_BLOCK)
        lastblk = jnp.where(is_e & (pend > pstart), pend - MOE_BLOCK, -1.0)
        brow = lax.broadcasted_iota(I32, (nb, 1), 0)
        out = jnp.where(blane == 0, bexp, jnp.where(blane == 1, nact, 0.0))
        out = jnp.where(brow == nb - 1, lastblk, out)
        blk_ref[...] = out.astype(I32)


def _dest(rec, rank, cnt, tb, nb_rows):
    T = rec.shape[0]
    return pl.pallas_call(
        _dest_kernel,
        grid=(T // tb,),
        in_specs=[pl.BlockSpec((tb, LANES), lambda i: (i, 0)), pl.BlockSpec((tb, LANES), lambda i: (i, 0)),
                  pl.BlockSpec((8, LANES), lambda i: (0, 0))],
        out_specs=[pl.BlockSpec((tb, LANES), lambda i: (i, 0)), pl.BlockSpec((nb_rows, LANES), lambda i: (0, 0))],
        out_shape=[jax.ShapeDtypeStruct((T, LANES), I32), jax.ShapeDtypeStruct((nb_rows, LANES), I32)],
        compiler_params=_cparams("arbitrary"),
        name="moe_dest",
    )(rec, rank, cnt)


def _scatter_kernel(pad_ref, na_ref, dest_ref, hn_ref, xs_ref, zbuf_ref, idx_ref, sem, isem, zsem,
                    *, tb, n_blocks, n_tail):
    i = pl.program_id(0)

    @pl.when(i == 0)
    def _():
        zbuf_ref[...] = jnp.zeros_like(zbuf_ref)

        def zero_rows(start):
            return pltpu.make_async_copy(zbuf_ref, xs_ref.at[pl.ds(start, MOE_BLOCK)], zsem)

        def pad_start(e, c):
            @pl.when(pad_ref[e] >= 0)
            def _():
                zero_rows(pad_ref[e]).start()
            return c

        def pad_wait(e, c):
            @pl.when(pad_ref[e] >= 0)
            def _():
                zero_rows(pad_ref[e]).wait()
            return c

        def tail_start(k, c):
            @pl.when(na_ref[0] + k < n_blocks)
            def _():
                zero_rows((na_ref[0] + k) * MOE_BLOCK).start()
            return c

        def tail_wait(k, c):
            @pl.when(na_ref[0] + k < n_blocks)
            def _():
                zero_rows((na_ref[0] + k) * MOE_BLOCK).wait()
            return c

        lax.fori_loop(0, N_EXPERTS, pad_start, 0)
        lax.fori_loop(0, n_tail, tail_start, 0)
        lax.fori_loop(0, N_EXPERTS, pad_wait, 0)
        lax.fori_loop(0, n_tail, tail_wait, 0)

    icp = pltpu.make_async_copy(dest_ref.at[i], idx_ref, isem)
    icp.start()
    icp.wait()

    def row_copy(t, s):
        return pltpu.make_async_copy(hn_ref.at[pl.ds(i * tb + t, 1)],
                                     xs_ref.at[pl.ds(idx_ref[2 * t + s], 1)], sem)

    def issue(t, c):
        row_copy(t, 0).start()
        row_copy(t, 1).start()
        return c

    lax.fori_loop(0, tb, issue, 0, unroll=8)

    def drain(t, c):
        row_copy(t, 0).wait()
        row_copy(t, 1).wait()
        return c

    lax.fori_loop(0, tb, drain, 0, unroll=8)


def _scatter(lastblk, nact, dest2, hn, tb, n_blocks):
    T, _, D = hn.shape
    n_tail = n_blocks - (-(-2 * T // MOE_BLOCK))
    return pl.pallas_call(
        functools.partial(_scatter_kernel, tb=tb, n_blocks=n_blocks, n_tail=n_tail),
        grid_spec=pltpu.PrefetchScalarGridSpec(
            num_scalar_prefetch=2, grid=(T // tb,),
            in_specs=[pl.BlockSpec(memory_space=pl.ANY), pl.BlockSpec(memory_space=pl.ANY)],
            out_specs=pl.BlockSpec(memory_space=pl.ANY),
            scratch_shapes=[pltpu.VMEM((MOE_BLOCK, 1, D), F32), pltpu.SMEM((2 * tb,), I32),
                            pltpu.SemaphoreType.DMA, pltpu.SemaphoreType.DMA, pltpu.SemaphoreType.DMA]),
        out_shape=jax.ShapeDtypeStruct((n_blocks * MOE_BLOCK, 1, D), F32),
        compiler_params=_cparams("arbitrary"),
        name="moe_scatter",
    )(lastblk, nact, dest2, hn)


def _expert_kernel(be_ref, na_ref, x_ref, wg_ref, wu_ref, wd_ref, y_ref, wgb_ref, wub_ref, wdb_ref, x2_ref):
    j = pl.program_id(0)

    @pl.when(j < na_ref[0])
    def _():
        prev = be_ref[jnp.maximum(j - 1, 0)]

        @pl.when((j == 0) | (be_ref[j] != prev))
        def _():
            wgb_ref[...] = wg_ref[0].astype(BF16)
            wub_ref[...] = wu_ref[0].astype(BF16)
            wdb_ref[...] = wd_ref[0].astype(BF16)

        x2_ref[...] = x_ref[...].reshape(x2_ref.shape)
        x = x2_ref[...].astype(BF16)
        hg = _dot(x, wgb_ref[...])
        hu = _dot(x, wub_ref[...])
        act = (hg * jax.nn.sigmoid(hg)) * hu
        y_ref[...] = _dot(act.astype(BF16), wdb_ref[...]).reshape(y_ref.shape)

    @pl.when(j >= na_ref[0])
    def _():
        y_ref[...] = jnp.zeros_like(y_ref)


def _experts(blk_exp, nact, xs, w_gate, w_up, w_down, n_blocks):
    D = xs.shape[2]
    rows = n_blocks * MOE_BLOCK
    DE = w_gate.shape[2]
    blk = lambda j, be, na: (jnp.minimum(j, na[0] - 1), 0, 0)
    wmap = lambda j, be, na: (be[jnp.minimum(j, na[0] - 1)], 0, 0)
    return pl.pallas_call(
        _expert_kernel,
        grid_spec=pltpu.PrefetchScalarGridSpec(
            num_scalar_prefetch=2, grid=(n_blocks,),
            in_specs=[pl.BlockSpec((MOE_BLOCK, 1, D), blk),
                      pl.BlockSpec((1, D, DE), wmap), pl.BlockSpec((1, D, DE), wmap),
                      pl.BlockSpec((1, DE, D), wmap)],
            out_specs=pl.BlockSpec((MOE_BLOCK, 1, D), lambda j, be, na: (j, 0, 0)),
            scratch_shapes=[pltpu.VMEM((D, DE), BF16), pltpu.VMEM((D, DE), BF16), pltpu.VMEM((DE, D), BF16),
                            pltpu.VMEM((MOE_BLOCK, D), F32)]),
        out_shape=jax.ShapeDtypeStruct((rows, 1, D), F32),
        compiler_params=_cparams("arbitrary"),
        name="moe_experts",
    )(blk_exp, nact, xs, w_gate, w_up, w_down)


def _combine_kernel(dest_ref, ys_ref, h_ref, rec_ref, g_ref, o_ref, idx_ref, y0_ref, y1_ref, y2_ref, sem, isem,
                    *, normalize):
    i = pl.program_id(0)
    tb = h_ref.shape[0]
    icp = pltpu.make_async_copy(dest_ref.at[i], idx_ref, isem)
    icp.start()
    icp.wait()

    def row_copy(t, s, buf):
        return pltpu.make_async_copy(ys_ref.at[pl.ds(idx_ref[2 * t + s], 1)], buf.at[pl.ds(t, 1)], sem)

    def issue(t, c):
        row_copy(t, 0, y0_ref).start()
        row_copy(t, 1, y1_ref).start()
        return c

    lax.fori_loop(0, tb, issue, 0, unroll=8)

    def drain(t, c):
        row_copy(t, 0, y0_ref).wait()
        row_copy(t, 1, y1_ref).wait()
        return c

    lax.fori_loop(0, tb, drain, 0, unroll=8)

    rec = rec_ref[...]
    y2_ref[...] = y0_ref[...].reshape(y2_ref.shape)
    h = h_ref[...] + rec[:, 2:3] * y2_ref[...]
    y2_ref[...] = y1_ref[...].reshape(y2_ref.shape)
    h = h + rec[:, 3:4] * y2_ref[...]
    if normalize:
        ms = jnp.mean(h * h, axis=-1, keepdims=True)
        h = h * lax.rsqrt(ms + RMS_EPS) * g_ref[...]
    o_ref[...] = h


def _combine(dest2, ys, h, rec, g, tb, normalize):
    T, D = h.shape
    return pl.pallas_call(
        functools.partial(_combine_kernel, normalize=normalize),
        grid=(T // tb,),
        in_specs=[pl.BlockSpec(memory_space=pl.ANY), pl.BlockSpec(memory_space=pl.ANY),
                  pl.BlockSpec((tb, D), lambda i: (i, 0)), pl.BlockSpec((tb, LANES), lambda i: (i, 0)),
                  pl.BlockSpec((1, D), lambda i: (0, 0))],
        out_specs=pl.BlockSpec((tb, D), lambda i: (i, 0)),
        out_shape=jax.ShapeDtypeStruct((T, D), F32),
        scratch_shapes=[pltpu.SMEM((2 * tb,), I32), pltpu.VMEM((tb, 1, D), F32), pltpu.VMEM((tb, 1, D), F32),
                        pltpu.VMEM((tb, D), F32), pltpu.SemaphoreType.DMA, pltpu.SemaphoreType.DMA],
        compiler_params=_cparams("arbitrary"),
        name="moe_combine",
    )(dest2, ys, h, rec, g)


def _pad_cols(a, n):
    return jnp.pad(a, ((0, 0), (0, n - a.shape[1])))


def _pad_rows(a, n, at=0):
    return jnp.pad(a, ((at, n - a.shape[0] - at), (0, 0)))


def _layer(x2, B, L, p):
    T, D = x2.shape
    W = p["s5_d"].shape[0]
    G, P = p["s5_lambda_re"].shape
    HG = W // G
    dl, al, gl = p["rwkv_w2"].shape[0], p["rwkv_a2"].shape[0], p["rwkv_g2"].shape[0]
    H = W // RW_N

    w_in = p["w_in"]
    o = W
    cols = [w_in[:, :W], w_in[:, o:o + 3 * W]]
    o += 3 * W
    cols += [_pad_cols(w_in[:, o:o + dl], 128), _pad_cols(w_in[:, o + dl:o + dl + al], 128),
             _pad_cols(w_in[:, o + dl + al:o + dl + al + gl], 256)]
    w_in_r = jnp.concatenate(cols, axis=1).astype(BF16)
    tm_in = min(1024, T)
    proj = _inproj(x2, p["norm_mix_g"].reshape(1, D), w_in_r, tm_in, 1152)

    C = S5_CHUNK
    nc = L // C
    tile_t = lambda a: jnp.tile(a, (1, 1, C))
    ct_re = tile_t(jnp.swapaxes(p["s5_c_re"], 1, 2))
    ct_im = tile_t(jnp.swapaxes(p["s5_c_im"], 1, 2))
    bt_re, bt_im = tile_t(p["s5_b_re"]), tile_t(p["s5_b_im"])
    btr_re, btr_im = jnp.swapaxes(p["s5_b_re"], 1, 2), jnp.swapaxes(p["s5_b_im"], 1, 2)
    mt, wbt, wc, a64 = _s5_ops(p["s5_lambda_re"], p["s5_lambda_im"], p["s5_log_step"],
                               ct_re, ct_im, bt_re, bt_im, btr_re, btr_im)
    u_g = proj[:, :W].astype(BF16).reshape(B * nc, C, G, HG).transpose(2, 0, 1, 3).reshape(G, B * nc, C * HG)
    y_g = _s5_chunk(u_g, mt, wbt, wc, a64, nc)
    y_ssm = y_g.reshape(G, B * nc, C, HG).transpose(1, 2, 0, 3).reshape(T, W)
    tm = min(512, T)
    s5_out = _s5_glu(y_ssm, proj, p["s5_d"].reshape(1, W), p["s5_w_glu"].astype(BF16),
                     p["s5_b_glu"].reshape(1, W), tm)

    mu = p["rwkv_mu"]
    mu3 = _pad_rows(mu[:3 * W].reshape(3, W), 8)
    o = 3 * W
    mul = jnp.concatenate([_pad_cols(mu[None, o:o + dl], 128), _pad_cols(mu[None, o + dl:o + dl + al], 128),
                           _pad_cols(mu[None, o + dl + al:], 256)], axis=1)
    w2p = _pad_rows(p["rwkv_w2"], 128).astype(BF16)
    a2p = _pad_rows(p["rwkv_a2"], 128).astype(BF16)
    g2p = _pad_rows(p["rwkv_g2"], 256).astype(BF16)
    head_of = jnp.arange(W, dtype=I32) // RW_N
    e_mat = (head_of[:, None] == jnp.arange(LANES, dtype=I32)[None, :]).astype(BF16)
    et_mat = e_mat.T
    row = lambda a: a.reshape(1, W)
    tm_rw = min(256, L)
    r, k, v, nk, bv, lw, g = _rwkv_prep(proj, B, L, mu3, mul, row(p["rwkv_w0"]), row(p["rwkv_a0"]),
                                        row(p["rwkv_k_k"]), row(p["rwkv_k_a"]), w2p, a2p, g2p,
                                        e_mat, et_mat, tm_rw)
    y_rw = _rwkv_chunk(r, k, v, nk, bv, lw, B, L)
    rw_out = _rwkv_post(y_rw, r, k, v, g, p["rwkv_r_k"].reshape(1, W), row(p["rwkv_ln_w"]),
                        row(p["rwkv_ln_b"]), e_mat, et_mat, tm)

    w_out = p["w_out"].astype(BF16)
    w_route = jnp.concatenate([p["w_route_exp"], p["w_route_grp"]], axis=1)
    w_route = _pad_cols(w_route, LANES)
    wr_hi = w_route.astype(BF16)
    wr_lo = (w_route - wr_hi.astype(F32)).astype(BF16)
    b_route = _pad_cols(jnp.concatenate([p["b_route_exp"], p["b_route_grp"]])[None, :], LANES)
    tm_o = min(256, T)
    h, hn, rec = _outproj(s5_out, rw_out, x2, w_out[:W], w_out[W:], p["norm_ffn_g"].reshape(1, D),
                          wr_hi, wr_lo, b_route, tm_o)

    tb = min(256, T)
    rank, cnt = _rank(rec, tb)
    n_blocks = -(-2 * T // MOE_BLOCK) + N_EXPERTS
    nb_rows = -(-(n_blocks + 1) // 8) * 8
    dest, blk = _dest(rec, rank, cnt, tb, nb_rows)
    blk_exp = blk[:n_blocks, 0]
    nact = blk[0:1, 1]
    lastblk = blk[nb_rows - 1, :N_EXPERTS]
    ts = min(128, T)
    dest2 = dest[:, :2].reshape(T // ts, 2 * ts)

    xs = _scatter(lastblk, nact, dest2, hn, ts, n_blocks)
    ys = _experts(blk_exp, nact, xs, p["w_gate"], p["w_up"], p["w_down"], n_blocks)
    return dest2, ys, h, rec, ts


def kernel(x, norm_mix_g, w_in, s5_lambda_re, s5_lambda_im, s5_log_step, s5_b_re, s5_b_im, s5_c_re, s5_c_im, s5_d, s5_w_glu, s5_b_glu, rwkv_mu, rwkv_w0, rwkv_w2, rwkv_a0, rwkv_a2, rwkv_g2, rwkv_k_k, rwkv_k_a, rwkv_r_k, rwkv_ln_w, rwkv_ln_b, w_out, norm_ffn_g, w_route_grp, b_route_grp, w_route_exp, b_route_exp, w_gate, w_up, w_down, norm_final_g):
    B, L, D = x.shape
    params = dict(
        norm_mix_g=norm_mix_g, w_in=w_in, s5_lambda_re=s5_lambda_re, s5_lambda_im=s5_lambda_im,
        s5_log_step=s5_log_step, s5_b_re=s5_b_re, s5_b_im=s5_b_im, s5_c_re=s5_c_re, s5_c_im=s5_c_im,
        s5_d=s5_d, s5_w_glu=s5_w_glu, s5_b_glu=s5_b_glu, rwkv_mu=rwkv_mu, rwkv_w0=rwkv_w0, rwkv_w2=rwkv_w2,
        rwkv_a0=rwkv_a0, rwkv_a2=rwkv_a2, rwkv_g2=rwkv_g2, rwkv_k_k=rwkv_k_k, rwkv_k_a=rwkv_k_a,
        rwkv_r_k=rwkv_r_k, rwkv_ln_w=rwkv_ln_w, rwkv_ln_b=rwkv_ln_b, w_out=w_out, norm_ffn_g=norm_ffn_g,
        w_route_grp=w_route_grp, b_route_grp=b_route_grp, w_route_exp=w_route_exp, b_route_exp=b_route_exp,
        w_gate=w_gate, w_up=w_up, w_down=w_down)
    depth = norm_mix_g.shape[0]
    h2 = x.reshape(B * L, D)
    for l in range(depth):
        p = {k_: v_[l] for k_, v_ in params.items()}
        dest2, ys, h, rec, ts = _layer(h2, B, L, p)
        h2 = _combine(dest2, ys, h, rec, norm_final_g.reshape(1, D), ts, normalize=(l == depth - 1))
    return h2.reshape(B, L, D)
```

```python
import functools

import jax
import jax.numpy as jnp
from jax import lax
from jax.experimental import pallas as pl
from jax.experimental.pallas import tpu as pltpu

F32 = jnp.float32
BF16 = jnp.bfloat16
I32 = jnp.int32

RMS_EPS = 1e-6
S5_CHUNK = 64
S5_HG = 16
S5_P = 64
RW_N = 64
RW_CHUNK = 64
RW_GN_EPS = 64e-5
N_GROUPS = 8
EPG = 8
N_EXPERTS = 64
MOE_BLOCK = 256
LANES = 128
VMEM_LIMIT = 56 * 1024 * 1024


def _cparams(*sem):
    return pltpu.CompilerParams(dimension_semantics=sem, vmem_limit_bytes=VMEM_LIMIT)


def _split2(x):
    hi = x.astype(BF16)
    lo = (x - hi.astype(F32)).astype(BF16)
    return hi, lo


def _split3(x):
    hi = x.astype(BF16)
    r = x - hi.astype(F32)
    mid = r.astype(BF16)
    lo = (r - mid.astype(F32)).astype(BF16)
    return hi, mid, lo


def _dot(a, b):
    return jnp.dot(a, b, preferred_element_type=F32)


def _dot_nt(a, b):
    return lax.dot_general(a, b, (((1,), (1,)), ((), ())), preferred_element_type=F32)


def _dot_tn(a, b):
    return lax.dot_general(a, b, (((0,), (0,)), ((), ())), preferred_element_type=F32)


def _dot_x3(a, b):
    ah, al = _split2(a)
    bh, bl = _split2(b)
    return _dot(ah, bh) + (_dot(ah, bl) + _dot(al, bh))


def _dot_exact_lhs(a_bf16, b):
    bh, bm, bl = _split3(b)
    return _dot(a_bf16, bh) + (_dot(a_bf16, bm) + _dot(a_bf16, bl))


def _cmul(ar, ai, br, bi):
    return ar * br - ai * bi, ar * bi + ai * br


def _inproj_kernel(x_ref, g_ref, w_ref, o_ref, hn_ref):
    @pl.when(pl.program_id(1) == 0)
    def _():
        x = x_ref[...]
        ms = jnp.mean(x * x, axis=-1, keepdims=True)
        hn_ref[...] = (x * lax.rsqrt(ms + RMS_EPS) * g_ref[...]).astype(BF16)

    o_ref[...] = _dot(hn_ref[...], w_ref[...])


def _inproj(x2, g, w_bf16, tm, tn):
    T, D = x2.shape
    N = w_bf16.shape[1]
    return pl.pallas_call(
        _inproj_kernel,
        grid=(T // tm, N // tn),
        in_specs=[
            pl.BlockSpec((tm, D), lambda i, j: (i, 0)),
            pl.BlockSpec((1, D), lambda i, j: (0, 0)),
            pl.BlockSpec((D, tn), lambda i, j: (0, j)),
        ],
        out_specs=pl.BlockSpec((tm, tn), lambda i, j: (i, j)),
        out_shape=jax.ShapeDtypeStruct((T, N), F32),
        scratch_shapes=[pltpu.VMEM((tm, D), BF16)],
        compiler_params=_cparams("arbitrary", "arbitrary"),
        name="inproj",
    )(x2, g, w_bf16)


def _binpow(ar, ai, expo, nbits):
    pr = jnp.ones(expo.shape, F32)
    pi = jnp.zeros(expo.shape, F32)
    sr, si = ar, ai
    for bit in range(nbits):
        m = ((expo >> bit) & 1) == 1
        nr, ni = _cmul(pr, pi, sr, si)
        pr = jnp.where(m, nr, pr)
        pi = jnp.where(m, ni, pi)
        if bit + 1 < nbits:
            sr, si = _cmul(sr, si, sr, si)
    return pr, pi


def _zoh(lr, li, dt):
    mag = jnp.exp(lr * dt)
    ang = li * dt
    ar, ai = mag * jnp.cos(ang), mag * jnp.sin(ang)
    den = lr * lr + li * li
    nr, ni = ar - 1.0, ai
    fr = (nr * lr + ni * li) / den
    fi = (ni * lr - nr * li) / den
    return ar, ai, fr, fi


def _s5_ops_kernel(lrc_ref, lic_ref, lrr_ref, lir_ref, dt_ref, ct_re_ref, ct_im_ref,
                   bt_re_ref, bt_im_ref, btr_re_ref, btr_im_ref,
                   mt_ref, wbt_ref, wc_ref, a64_ref):
    C, HG, P = S5_CHUNK, S5_HG, S5_P
    W = C * HG
    dt = jnp.exp(dt_ref[0])
    ar_c, ai_c, fr_c, fi_c = _zoh(lrc_ref[0], lic_ref[0], dt)
    ar_r, ai_r, fr_r, fi_r = _zoh(lrr_ref[0], lir_ref[0], dt)

    tau = lax.broadcasted_iota(I32, (1, W), 1) // HG
    nbits = max(1, (C - 1).bit_length())
    pr, pi = _binpow(ar_c, ai_c, jnp.broadcast_to(tau, (P, W)), nbits)
    qr, qi = _binpow(ar_c, ai_c, jnp.broadcast_to(C - 1 - tau, (P, W)), nbits)

    ca_re, ca_im = _cmul(ct_re_ref[0], ct_im_ref[0], pr, pi)
    bbt_re, bbt_im = _cmul(btr_re_ref[0], btr_im_ref[0], fr_r, fi_r)
    kt = _dot_x3(bbt_re, ca_re) - _dot_x3(bbt_im, ca_im)

    c1_re, c1_im = _cmul(ca_re, ca_im, ar_c, ai_c)
    wc_ref[0, :P, :] = c1_re.astype(BF16)
    wc_ref[0, P:, :] = (-c1_im).astype(BF16)

    bb_re, bb_im = _cmul(bt_re_ref[0], bt_im_ref[0], fr_c, fi_c)
    ab_re, ab_im = _cmul(bb_re, bb_im, qr, qi)
    wbt_ref[0, :P, :] = ab_re.astype(BF16)
    wbt_ref[0, P:, :] = ab_im.astype(BF16)

    cr, ci = ar_r, ai_r
    for _ in range(C.bit_length() - 1):
        cr, ci = _cmul(cr, ci, cr, ci)
    a64_ref[0] = jnp.concatenate([cr, ci], axis=1)

    ext = jnp.concatenate([jnp.zeros((HG, W), F32), kt], axis=1)
    for s in range(C):
        off = W - s * HG
        mt_ref[0, s * HG:(s + 1) * HG, :] = ext[:, off:off + W].astype(BF16)


def _s5_ops(lam_re, lam_im, log_step, ct_re, ct_im, bt_re, bt_im, btr_re, btr_im):
    G, P = lam_re.shape
    C, HG = S5_CHUNK, S5_HG
    W = C * HG
    col = lambda a: a.reshape(G, P, 1)
    row = lambda a: a.reshape(G, 1, P)
    g3 = lambda s1, s2: pl.BlockSpec((1, s1, s2), lambda g: (g, 0, 0))
    return pl.pallas_call(
        _s5_ops_kernel,
        grid=(G,),
        in_specs=[g3(P, 1), g3(P, 1), g3(1, P), g3(1, P), g3(1, 1),
                  g3(P, W), g3(P, W), g3(P, W), g3(P, W), g3(HG, P), g3(HG, P)],
        out_specs=[g3(W, W), g3(2 * P, W), g3(2 * P, W), g3(1, 2 * P)],
        out_shape=[jax.ShapeDtypeStruct((G, W, W), BF16),
                   jax.ShapeDtypeStruct((G, 2 * P, W), BF16),
                   jax.ShapeDtypeStruct((G, 2 * P, W), BF16),
                   jax.ShapeDtypeStruct((G, 1, 2 * P), F32)],
        compiler_params=_cparams("arbitrary"),
        name="s5_ops",
    )(col(lam_re), col(lam_im), row(lam_re), row(lam_im), log_step.reshape(G, 1, 1),
      ct_re, ct_im, bt_re, bt_im, btr_re, btr_im)


def _s5_chunk_kernel(u_ref, mt_ref, wbt_ref, wc_ref, a64_ref, y_ref, *, n_chunks):
    P = S5_P
    u = u_ref[0]
    R = u.shape[0]
    y = _dot(u, mt_ref[0])
    x = _dot_nt(u, wbt_ref[0])
    a = a64_ref[0]
    lane = lax.broadcasted_iota(I32, (1, 2 * P), 1)
    ar = jnp.where(lane < P, a, pltpu.roll(a, P, 1))
    ai = jnp.where(lane < P, -pltpu.roll(a, P, 1), a)
    cidx = lax.broadcasted_iota(I32, (R, 1), 0) % n_chunks
    shift = 1
    while shift < n_chunks:
        xs = jnp.where(cidx >= shift, pltpu.roll(x, shift, 0), 0.0)
        x = x + ar * xs + ai * pltpu.roll(xs, P, 1)
        nar = ar * ar - ai * ai
        ai = 2.0 * ar * ai
        ar = nar
        shift *= 2
    s_in = jnp.where(cidx >= 1, pltpu.roll(x, 1, 0), 0.0)
    sh, sl = _split2(s_in)
    wc = wc_ref[0]
    y_ref[0] = y + (_dot(sh, wc) + _dot(sl, wc))


def _s5_chunk(u_g, mt, wbt, wc, a64, n_chunks):
    G, R, W = u_g.shape
    P2 = wbt.shape[1]
    g3 = lambda s1, s2: pl.BlockSpec((1, s1, s2), lambda g: (g, 0, 0))
    return pl.pallas_call(
        functools.partial(_s5_chunk_kernel, n_chunks=n_chunks),
        grid=(G,),
        in_specs=[g3(R, W), g3(W, W), g3(P2, W), g3(P2, W), g3(1, P2)],
        out_specs=g3(R, W),
        out_shape=jax.ShapeDtypeStruct((G, R, W), F32),
        compiler_params=_cparams("arbitrary"),
        name="s5_chunk",
    )(u_g, mt, wbt, wc, a64)


def _gelu_tanh(x):
    return 0.5 * x * (1.0 + jnp.tanh(0.7978845608028654 * (x + 0.044715 * (x * x * x))))


def _s5_glu_kernel(y_ref, u_ref, d_ref, w_ref, b_ref, o_ref):
    y = y_ref[...] + d_ref[...] * u_ref[...]
    y = _gelu_tanh(y)
    z = _dot(y.astype(BF16), w_ref[...]) + b_ref[...]
    o_ref[...] = (y * jax.nn.sigmoid(z)).astype(BF16)


def _s5_glu(y_ssm, proj, d, w_bf16, b, tm):
    T, W = y_ssm.shape
    return pl.pallas_call(
        _s5_glu_kernel,
        grid=(T // tm,),
        in_specs=[
            pl.BlockSpec((tm, W), lambda i: (i, 0)),
            pl.BlockSpec((tm, W), lambda i: (i, 0)),
            pl.BlockSpec((1, W), lambda i: (0, 0)),
            pl.BlockSpec((W, W), lambda i: (0, 0)),
            pl.BlockSpec((1, W), lambda i: (0, 0)),
        ],
        out_specs=pl.BlockSpec((tm, W), lambda i: (i, 0)),
        out_shape=jax.ShapeDtypeStruct((T, W), BF16),
        compiler_params=_cparams("arbitrary"),
        name="s5_glu",
    )(y_ssm, proj, d, w_bf16, b)


def _head_sum(x, e_ref, et_ref):
    xh, xl = _split2(x)
    e = e_ref[...]
    s = _dot(xh, e) + _dot(xl, e)
    sh, sl = _split2(s)
    et = et_ref[...]
    return _dot(sh, et) + _dot(sl, et)


def _shift(z, prev_row):
    rolled = pltpu.roll(z, 1, 0)
    first = lax.broadcasted_iota(I32, (z.shape[0], 1), 0) == 0
    return jnp.where(first, prev_row, rolled)


def _rwkv_prep_kernel(zr_ref, zk_ref, zv_ref, zl_ref, mu_ref, mul_ref, w0_ref, a0_ref, kk_ref, ka_ref,
                      w2_ref, a2_ref, g2_ref, e_ref, et_ref,
                      r_ref, k_ref, v_ref, nk_ref, b_ref, lw_ref, g_ref,
                      car_ref, carl_ref):
    W = r_ref.shape[1]

    @pl.when(pl.program_id(1) == 0)
    def _():
        car_ref[...] = jnp.zeros_like(car_ref)
        carl_ref[...] = jnp.zeros_like(carl_ref)

    def lerp(z, prev_row, mu):
        return z + (_shift(z, prev_row) - z) * mu

    tm = zr_ref.shape[0]
    zr, zk, zv, zl = zr_ref[...], zk_ref[...], zv_ref[...], zl_ref[...]
    r = lerp(zr, car_ref[0:1, :], mu_ref[0:1, :])
    k = lerp(zk, car_ref[1:2, :], mu_ref[1:2, :])
    v = lerp(zv, car_ref[2:3, :], mu_ref[2:3, :])
    xl = lerp(zl, carl_ref[0:1, :], mul_ref[...])
    car_ref[0:1, :] = zr[tm - 1:tm, :]
    car_ref[1:2, :] = zk[tm - 1:tm, :]
    car_ref[2:3, :] = zv[tm - 1:tm, :]
    carl_ref[0:1, :] = zl[tm - 1:tm, :]

    xw, xa, xg = xl[:, 0:128], xl[:, 128:256], xl[:, 256:512]
    dw = _dot(jnp.tanh(xw).astype(BF16), w2_ref[...])
    da = _dot(xa.astype(BF16), a2_ref[...])
    g = _dot(jax.nn.sigmoid(xg).astype(BF16), g2_ref[...])

    zw = -(w0_ref[...] + dw)
    softplus = jnp.maximum(zw, 0.0) + jnp.log(1.0 + jnp.exp(-jnp.abs(zw)))
    w_log = -softplus - 0.5
    a = jax.nn.sigmoid(a0_ref[...] + da)

    kk = k * kk_ref[...]
    n2 = _head_sum(kk * kk, e_ref, et_ref)
    kk = kk / jnp.maximum(jnp.sqrt(n2), 1e-12)

    r_ref[...] = r
    k_ref[...] = k * (1.0 + (a - 1.0) * ka_ref[...])
    v_ref[...] = v
    nk_ref[...] = kk
    b_ref[...] = kk * a
    lw_ref[...] = -jnp.exp(w_log)
    g_ref[...] = g


def _rwkv_prep(proj, B, L, mu3, mul, w0, a0, k_k, k_a, w2p, a2p, g2p, e_mat, et_mat, tm):
    T = B * L
    W = w0.shape[1]
    nt = L // tm
    row = lambda c: pl.BlockSpec((tm, W), lambda b, i, c=c: (b * nt + i, c))
    full = lambda a: pl.BlockSpec(a.shape, lambda b, i: (0, 0))
    out = pl.BlockSpec((tm, W), lambda b, i: (b * nt + i, 0))
    lw = 512
    return pl.pallas_call(
        _rwkv_prep_kernel,
        grid=(B, nt),
        in_specs=[row(1), row(2), row(3),
                  pl.BlockSpec((tm, lw), lambda b, i: (b * nt + i, 4 * W // lw)),
                  full(mu3), full(mul), full(w0), full(a0), full(k_k), full(k_a),
                  full(w2p), full(a2p), full(g2p), full(e_mat), full(et_mat)],
        out_specs=[out] * 7,
        out_shape=[jax.ShapeDtypeStruct((T, W), F32)] * 7,
        scratch_shapes=[pltpu.VMEM((8, W), F32), pltpu.VMEM((8, lw), F32)],
        compiler_params=_cparams("arbitrary", "arbitrary"),
        name="rwkv_prep",
    )(proj, proj, proj, proj, mu3, mul, w0, a0, k_k, k_a, w2p, a2p, g2p, e_mat, et_mat)


def _rwkv_chunk_kernel(r_ref, k_ref, v_ref, nk_ref, b_ref, lw_ref, y_ref, z_ref):
    C, N = RW_CHUNK, RW_N
    H = r_ref.shape[1] // N

    @pl.when(pl.program_id(1) == 0)
    def _():
        z_ref[...] = jnp.zeros_like(z_ref)

    ri = lax.broadcasted_iota(I32, (C, C), 0)
    ci = lax.broadcasted_iota(I32, (C, C), 1)
    tril = ri >= ci
    stril = ri > ci
    eye = (ri == ci).astype(F32)

    lw = lw_ref[...]
    cs = _dot_exact_lhs(tril.astype(BF16), lw)
    tot = cs[C - 1:C, :]
    p_inc = jnp.exp(cs)
    p_inv = jnp.exp(-cs)
    p_prev = jnp.exp(cs - lw)
    p_rest = jnp.exp(tot - cs)
    p_tot = jnp.exp(tot)

    r_t = r_ref[...] * p_inc
    k_t = k_ref[...] * p_inv
    a_t = -nk_ref[...] * p_prev
    b_t = b_ref[...] * p_inv
    k_h = k_ref[...] * p_rest
    b_h = b_ref[...] * p_rest
    v_all = v_ref[...]

    hs = range(H)
    sl = [slice(h * N, (h + 1) * N) for h in hs]
    each = lambda f: [f(h) for h in hs]
    v = each(lambda h: v_all[:, sl[h]].astype(BF16))
    ar = each(lambda h: jnp.concatenate([a_t[:, sl[h]], r_t[:, sl[h]]], axis=0).astype(BF16))
    m_b = each(lambda h: _dot_nt(ar[h], b_t[:, sl[h]].astype(BF16)))
    m_k = each(lambda h: _dot_nt(ar[h], k_t[:, sl[h]].astype(BF16)))
    l_ab = each(lambda h: jnp.where(stril, m_b[h][:C], 0.0).astype(BF16))
    m_rb = each(lambda h: jnp.where(tril, m_b[h][C:], 0.0).astype(BF16))
    l_ak = each(lambda h: jnp.where(stril, m_k[h][:C], 0.0).astype(BF16))
    m_rk = each(lambda h: jnp.where(tril, m_k[h][C:], 0.0).astype(BF16))

    x = each(lambda h: jnp.concatenate([_dot(l_ak[h], v[h]), a_t[:, sl[h]]], axis=1))
    y0 = each(lambda h: _dot(m_rk[h], v[h]))
    hk = each(lambda h: _dot_tn(k_h[:, sl[h]].astype(BF16), v[h]))
    lp = l_ab
    step = 1
    while step < C:
        x = each(lambda h: x[h] + _dot(lp[h], x[h].astype(BF16)))
        step *= 2
        if step < C:
            lp = each(lambda h: _dot(lp[h], lp[h]).astype(BF16))
    xb = each(lambda h: x[h].astype(BF16))
    yq = each(lambda h: _dot(m_rb[h], xb[h]))
    gh = each(lambda h: _dot_tn(b_h[:, sl[h]].astype(BF16), xb[h]))

    z = each(lambda h: _split2(z_ref[h]))
    qb = each(lambda h: (yq[h][:, N:] + r_t[:, sl[h]]).astype(BF16))
    gb = each(lambda h: (gh[h][:, N:] + eye * p_tot[:, sl[h]]).astype(BF16))
    yz = each(lambda h: _dot(qb[h], z[h][0]) + _dot(qb[h], z[h][1]))
    gz = each(lambda h: _dot(gb[h], z[h][0]) + _dot(gb[h], z[h][1]))
    for h in hs:
        y_ref[:, sl[h]] = (yq[h][:, :N] + y0[h]) + yz[h]
        z_ref[h] = (gh[h][:, :N] + hk[h]) + gz[h]


def _rwkv_chunk(r, k, v, nk, b, lw, B, L):
    T, W = r.shape
    C = RW_CHUNK
    nc = L // C
    H = W // RW_N
    spec = pl.BlockSpec((C, W), lambda bi, c: (bi * nc + c, 0))
    return pl.pallas_call(
        _rwkv_chunk_kernel,
        grid=(B, nc),
        in_specs=[spec] * 6,
        out_specs=spec,
        out_shape=jax.ShapeDtypeStruct((T, W), F32),
        scratch_shapes=[pltpu.VMEM((H, RW_N, RW_N), F32)],
        compiler_params=_cparams("arbitrary", "arbitrary"),
        name="rwkv_chunk",
    )(r, k, v, nk, b, lw)


def _rwkv_post_kernel(y_ref, r_ref, k_ref, v_ref, g_ref, rk_ref, lnw_ref, lnb_ref, e_ref, et_ref, o_ref):
    inv_n = 1.0 / RW_N
    y = y_ref[...]
    mean = _head_sum(y, e_ref, et_ref) * inv_n
    yc = y - mean
    var = _head_sum(yc * yc, e_ref, et_ref) * inv_n
    yn = yc * lax.rsqrt(var + RW_GN_EPS) * lnw_ref[...] + lnb_ref[...]
    v = v_ref[...]
    bonus = _head_sum(r_ref[...] * k_ref[...] * rk_ref[...], e_ref, et_ref) * v
    o_ref[...] = ((yn + bonus) * g_ref[...]).astype(BF16)


def _rwkv_post(y, r, k, v, g, r_k, ln_w, ln_b, e_mat, et_mat, tm):
    T, W = y.shape
    row = pl.BlockSpec((tm, W), lambda i: (i, 0))
    full = lambda a: pl.BlockSpec(a.shape, lambda i: (0, 0))
    return pl.pallas_call(
        _rwkv_post_kernel,
        grid=(T // tm,),
        in_specs=[row] * 5 + [full(r_k), full(ln_w), full(ln_b), full(e_mat), full(et_mat)],
        out_specs=row,
        out_shape=jax.ShapeDtypeStruct((T, W), BF16),
        compiler_params=_cparams("arbitrary"),
        name="rwkv_post",
    )(y, r, k, v, g, r_k, ln_w, ln_b, e_mat, et_mat)


def _first_index_of_max(vals, lane, valid):
    neg = jnp.float32(-jnp.inf)
    masked = jnp.where(valid, vals, neg)
    m = jnp.max(masked, axis=-1, keepdims=True)
    idx = jnp.min(jnp.where(valid & (masked == m), lane, LANES), axis=-1, keepdims=True)
    return m, idx


def _outproj_kernel(s5_ref, rw_ref, x_ref, wt_ref, wb_ref, g_ref, wrh_ref, wrl_ref, br_ref,
                    h_ref, hn_ref, rec_ref):
    h = x_ref[...] + (_dot(s5_ref[...], wt_ref[...]) + _dot(rw_ref[...], wb_ref[...]))
    h_ref[...] = h
    ms = jnp.mean(h * h, axis=-1, keepdims=True)
    hn = h * lax.rsqrt(ms + RMS_EPS) * g_ref[...]
    hn_ref[...] = hn.reshape(hn_ref.shape)

    hh, hl = _split2(hn)
    wh, wl = wrh_ref[...], wrl_ref[...]
    logits = _dot(hh, wh) + (_dot(hh, wl) + _dot(hl, wh)) + br_ref[...]
    lane = lax.broadcasted_iota(I32, logits.shape, 1)
    is_grp = (lane >= N_EXPERTS) & (lane < N_EXPERTS + N_GROUPS)
    gmax, gidx = _first_index_of_max(logits, lane, is_grp)
    gsum = jnp.sum(jnp.where(is_grp, jnp.exp(logits - gmax), 0.0), axis=-1, keepdims=True)
    p_grp = 1.0 / gsum
    grp = gidx - N_EXPERTS
    in_grp = (lane >= grp * EPG) & (lane < (grp + 1) * EPG)
    m1, i1 = _first_index_of_max(logits, lane, in_grp)
    m2, i2 = _first_index_of_max(logits, lane, in_grp & (lane != i1))
    e = jnp.exp(m2 - m1)
    g1 = p_grp / (1.0 + e)
    g2 = p_grp * e / (1.0 + e)
    rec = jnp.where(lane == 0, i1.astype(F32),
          jnp.where(lane == 1, i2.astype(F32),
          jnp.where(lane == 2, g1, jnp.where(lane == 3, g2, 0.0))))
    rec_ref[...] = rec


def _outproj(s5o, rwo, x2, w_top, w_bot, g, wr_hi, wr_lo, b_route, tm):
    T, D = x2.shape
    W = s5o.shape[1]
    full = lambda a: pl.BlockSpec(a.shape, lambda i: (0, 0))
    return pl.pallas_call(
        _outproj_kernel,
        grid=(T // tm,),
        in_specs=[pl.BlockSpec((tm, W), lambda i: (i, 0)), pl.BlockSpec((tm, W), lambda i: (i, 0)),
                  pl.BlockSpec((tm, D), lambda i: (i, 0)),
                  full(w_top), full(w_bot), full(g), full(wr_hi), full(wr_lo), full(b_route)],
        out_specs=[pl.BlockSpec((tm, D), lambda i: (i, 0)), pl.BlockSpec((tm, 1, D), lambda i: (i, 0, 0)),
                   pl.BlockSpec((tm, LANES), lambda i: (i, 0))],
        out_shape=[jax.ShapeDtypeStruct((T, D), F32), jax.ShapeDtypeStruct((T, 1, D), F32),
                   jax.ShapeDtypeStruct((T, LANES), F32)],
        compiler_params=_cparams("arbitrary"),
        name="outproj_route",
    )(s5o, rwo, x2, w_top, w_bot, g, wr_hi, wr_lo, b_route)


def _onehots(rec, lane):
    oh0 = (lane == rec[:, 0:1].astype(I32)).astype(F32)
    oh1 = (lane == rec[:, 1:2].astype(I32)).astype(F32)
    return oh0, oh1


def _rank_kernel(rec_ref, rank_ref, cnt_ref, base_ref):
    tb = rec_ref.shape[0]

    @pl.when(pl.program_id(0) == 0)
    def _():
        base_ref[...] = jnp.zeros_like(base_ref)

    lane = lax.broadcasted_iota(I32, (tb, LANES), 1)
    oh0, oh1 = _onehots(rec_ref[...], lane)
    both = oh0 + oh1
    ri = lax.broadcasted_iota(I32, (tb, tb), 0)
    ci = lax.broadcasted_iota(I32, (tb, tb), 1)
    before = _dot((ri > ci).astype(BF16), both.astype(BF16)) + base_ref[0:1, :]
    rank0 = jnp.sum(oh0 * before, axis=-1, keepdims=True)
    rank1 = jnp.sum(oh1 * before, axis=-1, keepdims=True)
    rank_ref[...] = jnp.where(lane == 0, rank0, jnp.where(lane == 1, rank1, 0.0))
    total = base_ref[0:1, :] + jnp.sum(both, axis=0, keepdims=True)
    base_ref[0:1, :] = total
    cnt_ref[...] = jnp.broadcast_to(total, cnt_ref.shape)


def _rank(rec, tb):
    T = rec.shape[0]
    return pl.pallas_call(
        _rank_kernel,
        grid=(T // tb,),
        in_specs=[pl.BlockSpec((tb, LANES), lambda i: (i, 0))],
        out_specs=[pl.BlockSpec((tb, LANES), lambda i: (i, 0)), pl.BlockSpec((8, LANES), lambda i: (0, 0))],
        out_shape=[jax.ShapeDtypeStruct((T, LANES), F32), jax.ShapeDtypeStruct((8, LANES), F32)],
        scratch_shapes=[pltpu.VMEM((8, LANES), F32)],
        compiler_params=_cparams("arbitrary"),
        name="moe_rank",
    )(rec)


def _padded_starts(cnt):
    padded = jnp.ceil(cnt * (1.0 / MOE_BLOCK)) * MOE_BLOCK
    ri = lax.broadcasted_iota(I32, (LANES, LANES), 0)
    ci = lax.broadcasted_iota(I32, (LANES, LANES), 1)
    p8 = jnp.broadcast_to(padded, (8, LANES))
    pend = _dot_exact_lhs_rhs(p8, (ri <= ci).astype(BF16))[0:1, :]
    return pend - padded, pend


def _dot_exact_lhs_rhs(a, b_bf16):
    ah, am, al = _split3(a)
    return _dot(ah, b_bf16) + (_dot(am, b_bf16) + _dot(al, b_bf16))


def _dest_kernel(rec_ref, rank_ref, cnt_ref, dest_ref, blk_ref):
    tb = rec_ref.shape[0]
    cnt = cnt_ref[0:1, :]
    pstart, pend = _padded_starts(cnt)
    lane = lax.broadcasted_iota(I32, (tb, LANES), 1)
    oh0, oh1 = _onehots(rec_ref[...], lane)
    rank = rank_ref[...]
    d0 = jnp.sum(oh0 * pstart, axis=-1, keepdims=True) + rank[:, 0:1]
    d1 = jnp.sum(oh1 * pstart, axis=-1, keepdims=True) + rank[:, 1:2]
    dest_ref[...] = jnp.where(lane == 0, d0, jnp.where(lane == 1, d1, 0.0)).astype(I32)

    @pl.when(pl.program_id(0) == 0)
    def _():
        nb = blk_ref.shape[0]
        blane = lax.broadcasted_iota(I32, (nb, LANES), 1)
        bstart = (lax.broadcasted_iota(I32, (nb, 1), 0) * MOE_BLOCK).astype(F32)
        is_e = blane < N_EXPERTS
        bexp = jnp.sum(jnp.where(is_e & (pend <= bstart), 1.0, 0.0), axis=-1, keepdims=True)
        bexp = jnp.minimum(bexp, N_EXPERTS - 1.0)
        nact = jnp.max(jnp.where(is_e, pend, 0.0), axis=-1, keepdims=True) * (1.0 / MOE_BLOCK)
        lastblk = jnp.where(is_e & (pend > pstart), pend - MOE_BLOCK, -1.0)
        brow = lax.broadcasted_iota(I32, (nb, 1), 0)
        out = jnp.where(blane == 0, bexp, jnp.where(blane == 1, nact, 0.0))
        out = jnp.where(brow == nb - 1, lastblk, out)
        blk_ref[...] = out.astype(I32)


def _dest(rec, rank, cnt, tb, nb_rows):
    T = rec.shape[0]
    return pl.pallas_call(
        _dest_kernel,
        grid=(T // tb,),
        in_specs=[pl.BlockSpec((tb, LANES), lambda i: (i, 0)), pl.BlockSpec((tb, LANES), lambda i: (i, 0)),
                  pl.BlockSpec((8, LANES), lambda i: (0, 0))],
        out_specs=[pl.BlockSpec((tb, LANES), lambda i: (i, 0)), pl.BlockSpec((nb_rows, LANES), lambda i: (0, 0))],
        out_shape=[jax.ShapeDtypeStruct((T, LANES), I32), jax.ShapeDtypeStruct((nb_rows, LANES), I32)],
        compiler_params=_cparams("arbitrary"),
        name="moe_dest",
    )(rec, rank, cnt)


def _scatter_kernel(pad_ref, na_ref, dest_ref, hn_ref, xs_ref, zbuf_ref, idx_ref, sem, isem, zsem,
                    *, tb, n_blocks, n_tail):
    i = pl.program_id(0)

    @pl.when(i == 0)
    def _():
        zbuf_ref[...] = jnp.zeros_like(zbuf_ref)

        def zero_rows(start):
            return pltpu.make_async_copy(zbuf_ref, xs_ref.at[pl.ds(start, MOE_BLOCK)], zsem)

        def pad_start(e, c):
            @pl.when(pad_ref[e] >= 0)
            def _():
                zero_rows(pad_ref[e]).start()
            return c

        def pad_wait(e, c):
            @pl.when(pad_ref[e] >= 0)
            def _():
                zero_rows(pad_ref[e]).wait()
            return c

        def tail_start(k, c):
            @pl.when(na_ref[0] + k < n_blocks)
            def _():
                zero_rows((na_ref[0] + k) * MOE_BLOCK).start()
            return c

        def tail_wait(k, c):
            @pl.when(na_ref[0] + k < n_blocks)
            def _():
                zero_rows((na_ref[0] + k) * MOE_BLOCK).wait()
            return c

        lax.fori_loop(0, N_EXPERTS, pad_start, 0)
        lax.fori_loop(0, n_tail, tail_start, 0)
        lax.fori_loop(0, N_EXPERTS, pad_wait, 0)
        lax.fori_loop(0, n_tail, tail_wait, 0)

    icp = pltpu.make_async_copy(dest_ref.at[i], idx_ref, isem)
    icp.start()
    icp.wait()

    def row_copy(t, s):
        return pltpu.make_async_copy(hn_ref.at[pl.ds(t, 1)], xs_ref.at[pl.ds(idx_ref[2 * t + s], 1)], sem)

    def issue(t, c):
        row_copy(t, 0).start()
        row_copy(t, 1).start()
        return c

    lax.fori_loop(0, tb, issue, 0, unroll=8)

    def drain(t, c):
        row_copy(t, 0).wait()
        row_copy(t, 1).wait()
        return c

    lax.fori_loop(0, tb, drain, 0, unroll=8)


def _scatter(lastblk, nact, dest2, hn, tb, n_blocks):
    T, _, D = hn.shape
    n_tail = n_blocks - (-(-2 * T // MOE_BLOCK))
    return pl.pallas_call(
        functools.partial(_scatter_kernel, tb=tb, n_blocks=n_blocks, n_tail=n_tail),
        grid_spec=pltpu.PrefetchScalarGridSpec(
            num_scalar_prefetch=2, grid=(T // tb,),
            in_specs=[pl.BlockSpec(memory_space=pl.ANY),
                      pl.BlockSpec((tb, 1, D), lambda i, pad, na: (i, 0, 0))],
            out_specs=pl.BlockSpec(memory_space=pl.ANY),
            scratch_shapes=[pltpu.VMEM((MOE_BLOCK, 1, D), F32), pltpu.SMEM((2 * tb,), I32),
                            pltpu.SemaphoreType.DMA, pltpu.SemaphoreType.DMA, pltpu.SemaphoreType.DMA]),
        out_shape=jax.ShapeDtypeStruct((n_blocks * MOE_BLOCK, 1, D), F32),
        compiler_params=_cparams("arbitrary"),
        name="moe_scatter",
    )(lastblk, nact, dest2, hn)


def _expert_kernel(be_ref, na_ref, x_ref, wg_ref, wu_ref, wd_ref, y_ref, wgb_ref, wub_ref, wdb_ref, x2_ref):
    j = pl.program_id(0)

    @pl.when(j < na_ref[0])
    def _():
        prev = be_ref[jnp.maximum(j - 1, 0)]

        @pl.when((j == 0) | (be_ref[j] != prev))
        def _():
            wgb_ref[...] = wg_ref[0].astype(BF16)
            wub_ref[...] = wu_ref[0].astype(BF16)
            wdb_ref[...] = wd_ref[0].astype(BF16)

        x2_ref[...] = x_ref[...].reshape(x2_ref.shape)
        x = x2_ref[...].astype(BF16)
        hg = _dot(x, wgb_ref[...])
        hu = _dot(x, wub_ref[...])
        act = (hg * jax.nn.sigmoid(hg)) * hu
        y_ref[...] = _dot(act.astype(BF16), wdb_ref[...]).reshape(y_ref.shape)

    @pl.when(j >= na_ref[0])
    def _():
        y_ref[...] = jnp.zeros_like(y_ref)


def _experts(blk_exp, nact, xs, w_gate, w_up, w_down, n_blocks):
    D = xs.shape[2]
    rows = n_blocks * MOE_BLOCK
    DE = w_gate.shape[2]
    blk = lambda j, be, na: (jnp.minimum(j, na[0] - 1), 0, 0)
    wmap = lambda j, be, na: (be[jnp.minimum(j, na[0] - 1)], 0, 0)
    return pl.pallas_call(
        _expert_kernel,
        grid_spec=pltpu.PrefetchScalarGridSpec(
            num_scalar_prefetch=2, grid=(n_blocks,),
            in_specs=[pl.BlockSpec((MOE_BLOCK, 1, D), blk),
                      pl.BlockSpec((1, D, DE), wmap), pl.BlockSpec((1, D, DE), wmap),
                      pl.BlockSpec((1, DE, D), wmap)],
            out_specs=pl.BlockSpec((MOE_BLOCK, 1, D), lambda j, be, na: (j, 0, 0)),
            scratch_shapes=[pltpu.VMEM((D, DE), BF16), pltpu.VMEM((D, DE), BF16), pltpu.VMEM((DE, D), BF16),
                            pltpu.VMEM((MOE_BLOCK, D), F32)]),
        out_shape=jax.ShapeDtypeStruct((rows, 1, D), F32),
        compiler_params=_cparams("arbitrary"),
        name="moe_experts",
    )(blk_exp, nact, xs, w_gate, w_up, w_down)


def _combine_kernel(dest_ref, ys_ref, h_ref, rec_ref, g_ref, o_ref, idx_ref, y0_ref, y1_ref, y2_ref, sem, isem,
                    *, normalize):
    i = pl.program_id(0)
    tb = h_ref.shape[0]
    icp = pltpu.make_async_copy(dest_ref.at[i], idx_ref, isem)
    icp.start()
    icp.wait()

    def row_copy(t, s, buf):
        return pltpu.make_async_copy(ys_ref.at[pl.ds(idx_ref[2 * t + s], 1)], buf.at[pl.ds(t, 1)], sem)

    def issue(t, c):
        row_copy(t, 0, y0_ref).start()
        row_copy(t, 1, y1_ref).start()
        return c

    lax.fori_loop(0, tb, issue, 0, unroll=8)

    def drain(t, c):
        row_copy(t, 0, y0_ref).wait()
        row_copy(t, 1, y1_ref).wait()
        return c

    lax.fori_loop(0, tb, drain, 0, unroll=8)

    rec = rec_ref[...]
    y2_ref[...] = y0_ref[...].reshape(y2_ref.shape)
    h = h_ref[...] + rec[:, 2:3] * y2_ref[...]
    y2_ref[...] = y1_ref[...].reshape(y2_ref.shape)
    h = h + rec[:, 3:4] * y2_ref[...]
    if normalize:
        ms = jnp.mean(h * h, axis=-1, keepdims=True)
        h = h * lax.rsqrt(ms + RMS_EPS) * g_ref[...]
    o_ref[...] = h


def _combine(dest2, ys, h, rec, g, tb, normalize):
    T, D = h.shape
    return pl.pallas_call(
        functools.partial(_combine_kernel, normalize=normalize),
        grid=(T // tb,),
        in_specs=[pl.BlockSpec(memory_space=pl.ANY), pl.BlockSpec(memory_space=pl.ANY),
                  pl.BlockSpec((tb, D), lambda i: (i, 0)), pl.BlockSpec((tb, LANES), lambda i: (i, 0)),
                  pl.BlockSpec((1, D), lambda i: (0, 0))],
        out_specs=pl.BlockSpec((tb, D), lambda i: (i, 0)),
        out_shape=jax.ShapeDtypeStruct((T, D), F32),
        scratch_shapes=[pltpu.SMEM((2 * tb,), I32), pltpu.VMEM((tb, 1, D), F32), pltpu.VMEM((tb, 1, D), F32),
                        pltpu.VMEM((tb, D), F32), pltpu.SemaphoreType.DMA, pltpu.SemaphoreType.DMA],
        compiler_params=_cparams("arbitrary"),
        name="moe_combine",
    )(dest2, ys, h, rec, g)


def _pad_cols(a, n):
    return jnp.pad(a, ((0, 0), (0, n - a.shape[1])))


def _pad_rows(a, n, at=0):
    return jnp.pad(a, ((at, n - a.shape[0] - at), (0, 0)))


def _layer(x2, B, L, p):
    T, D = x2.shape
    W = p["s5_d"].shape[0]
    G, P = p["s5_lambda_re"].shape
    HG = W // G
    dl, al, gl = p["rwkv_w2"].shape[0], p["rwkv_a2"].shape[0], p["rwkv_g2"].shape[0]
    H = W // RW_N

    w_in = p["w_in"]
    o = W
    cols = [w_in[:, :W], w_in[:, o:o + 3 * W]]
    o += 3 * W
    cols += [_pad_cols(w_in[:, o:o + dl], 128), _pad_cols(w_in[:, o + dl:o + dl + al], 128),
             _pad_cols(w_in[:, o + dl + al:o + dl + al + gl], 256)]
    w_in_r = jnp.concatenate(cols, axis=1).astype(BF16)
    tm_in = min(1024, T)
    proj = _inproj(x2, p["norm_mix_g"].reshape(1, D), w_in_r, tm_in, 1152)

    C = S5_CHUNK
    nc = L // C
    tile_t = lambda a: jnp.tile(a, (1, 1, C))
    ct_re = tile_t(jnp.swapaxes(p["s5_c_re"], 1, 2))
    ct_im = tile_t(jnp.swapaxes(p["s5_c_im"], 1, 2))
    bt_re, bt_im = tile_t(p["s5_b_re"]), tile_t(p["s5_b_im"])
    btr_re, btr_im = jnp.swapaxes(p["s5_b_re"], 1, 2), jnp.swapaxes(p["s5_b_im"], 1, 2)
    mt, wbt, wc, a64 = _s5_ops(p["s5_lambda_re"], p["s5_lambda_im"], p["s5_log_step"],
                               ct_re, ct_im, bt_re, bt_im, btr_re, btr_im)
    u_g = proj[:, :W].astype(BF16).reshape(B * nc, C, G, HG).transpose(2, 0, 1, 3).reshape(G, B * nc, C * HG)
    y_g = _s5_chunk(u_g, mt, wbt, wc, a64, nc)
    y_ssm = y_g.reshape(G, B * nc, C, HG).transpose(1, 2, 0, 3).reshape(T, W)
    tm = min(512, T)
    s5_out = _s5_glu(y_ssm, proj, p["s5_d"].reshape(1, W), p["s5_w_glu"].astype(BF16),
                     p["s5_b_glu"].reshape(1, W), tm)

    mu = p["rwkv_mu"]
    mu3 = _pad_rows(mu[:3 * W].reshape(3, W), 8)
    o = 3 * W
    mul = jnp.concatenate([_pad_cols(mu[None, o:o + dl], 128), _pad_cols(mu[None, o + dl:o + dl + al], 128),
                           _pad_cols(mu[None, o + dl + al:], 256)], axis=1)
    w2p = _pad_rows(p["rwkv_w2"], 128).astype(BF16)
    a2p = _pad_rows(p["rwkv_a2"], 128).astype(BF16)
    g2p = _pad_rows(p["rwkv_g2"], 256).astype(BF16)
    head_of = jnp.arange(W, dtype=I32) // RW_N
    e_mat = (head_of[:, None] == jnp.arange(LANES, dtype=I32)[None, :]).astype(BF16)
    et_mat = e_mat.T
    row = lambda a: a.reshape(1, W)
    tm_rw = min(256, L)
    r, k, v, nk, bv, lw, g = _rwkv_prep(proj, B, L, mu3, mul, row(p["rwkv_w0"]), row(p["rwkv_a0"]),
                                        row(p["rwkv_k_k"]), row(p["rwkv_k_a"]), w2p, a2p, g2p,
                                        e_mat, et_mat, tm_rw)
    y_rw = _rwkv_chunk(r, k, v, nk, bv, lw, B, L)
    rw_out = _rwkv_post(y_rw, r, k, v, g, p["rwkv_r_k"].reshape(1, W), row(p["rwkv_ln_w"]),
                        row(p["rwkv_ln_b"]), e_mat, et_mat, tm)

    w_out = p["w_out"].astype(BF16)
    w_route = jnp.concatenate([p["w_route_exp"], p["w_route_grp"]], axis=1)
    w_route = _pad_cols(w_route, LANES)
    wr_hi = w_route.astype(BF16)
    wr_lo = (w_route - wr_hi.astype(F32)).astype(BF16)
    b_route = _pad_cols(jnp.concatenate([p["b_route_exp"], p["b_route_grp"]])[None, :], LANES)
    tm_o = min(256, T)
    h, hn, rec = _outproj(s5_out, rw_out, x2, w_out[:W], w_out[W:], p["norm_ffn_g"].reshape(1, D),
                          wr_hi, wr_lo, b_route, tm_o)

    tb = min(256, T)
    rank, cnt = _rank(rec, tb)
    n_blocks = -(-2 * T // MOE_BLOCK) + N_EXPERTS
    nb_rows = -(-(n_blocks + 1) // 8) * 8
    dest, blk = _dest(rec, rank, cnt, tb, nb_rows)
    blk_exp = blk[:n_blocks, 0]
    nact = blk[0:1, 1]
    lastblk = blk[nb_rows - 1, :N_EXPERTS]
    ts = min(128, T)
    dest2 = dest[:, :2].reshape(T // ts, 2 * ts)

    xs = _scatter(lastblk, nact, dest2, hn, ts, n_blocks)
    ys = _experts(blk_exp, nact, xs, p["w_gate"], p["w_up"], p["w_down"], n_blocks)
    return dest2, ys, h, rec, ts


def kernel(x, norm_mix_g, w_in, s5_lambda_re, s5_lambda_im, s5_log_step, s5_b_re, s5_b_im, s5_c_re, s5_c_im, s5_d, s5_w_glu, s5_b_glu, rwkv_mu, rwkv_w0, rwkv_w2, rwkv_a0, rwkv_a2, rwkv_g2, rwkv_k_k, rwkv_k_a, rwkv_r_k, rwkv_ln_w, rwkv_ln_b, w_out, norm_ffn_g, w_route_grp, b_route_grp, w_route_exp, b_route_exp, w_gate, w_up, w_down, norm_final_g):
    B, L, D = x.shape
    params = dict(
        norm_mix_g=norm_mix_g, w_in=w_in, s5_lambda_re=s5_lambda_re, s5_lambda_im=s5_lambda_im,
        s5_log_step=s5_log_step, s5_b_re=s5_b_re, s5_b_im=s5_b_im, s5_c_re=s5_c_re, s5_c_im=s5_c_im,
        s5_d=s5_d, s5_w_glu=s5_w_glu, s5_b_glu=s5_b_glu, rwkv_mu=rwkv_mu, rwkv_w0=rwkv_w0, rwkv_w2=rwkv_w2,
        rwkv_a0=rwkv_a0, rwkv_a2=rwkv_a2, rwkv_g2=rwkv_g2, rwkv_k_k=rwkv_k_k, rwkv_k_a=rwkv_k_a,
        rwkv_r_k=rwkv_r_k, rwkv_ln_w=rwkv_ln_w, rwkv_ln_b=rwkv_ln_b, w_out=w_out, norm_ffn_g=norm_ffn_g,
        w_route_grp=w_route_grp, b_route_grp=b_route_grp, w_route_exp=w_route_exp, b_route_exp=b_route_exp,
        w_gate=w_gate, w_up=w_up, w_down=w_down)
    depth = norm_mix_g.shape[0]
    h2 = x.reshape(B * L, D)
    for l in range(depth):
        p = {k_: v_[l] for k_, v_ in params.items()}
        dest2, ys, h, rec, ts = _layer(h2, B, L, p)
        h2 = _combine(dest2, ys, h, rec, norm_final_g.reshape(1, D), ts, normalize=(l == depth - 1))
    return h2.reshape(B, L, D)
```

```python
import functools

import jax
import jax.numpy as jnp
from jax import lax
from jax.experimental import pallas as pl
from jax.experimental.pallas import tpu as pltpu

F32 = jnp.float32
BF16 = jnp.bfloat16
I32 = jnp.int32

RMS_EPS = 1e-6
S5_CHUNK = 64
S5_HG = 16
S5_P = 64
RW_N = 64
RW_CHUNK = 64
RW_GN_EPS = 64e-5
N_GROUPS = 8
EPG = 8
N_EXPERTS = 64
MOE_BLOCK = 256
LANES = 128
VMEM_LIMIT = 56 * 1024 * 1024


def _cparams(*sem):
    return pltpu.CompilerParams(dimension_semantics=sem, vmem_limit_bytes=VMEM_LIMIT)


def _split2(x):
    hi = x.astype(BF16)
    lo = (x - hi.astype(F32)).astype(BF16)
    return hi, lo


def _split3(x):
    hi = x.astype(BF16)
    r = x - hi.astype(F32)
    mid = r.astype(BF16)
    lo = (r - mid.astype(F32)).astype(BF16)
    return hi, mid, lo


def _dot(a, b):
    return jnp.dot(a, b, preferred_element_type=F32)


def _dot_nt(a, b):
    return lax.dot_general(a, b, (((1,), (1,)), ((), ())), preferred_element_type=F32)


def _dot_tn(a, b):
    return lax.dot_general(a, b, (((0,), (0,)), ((), ())), preferred_element_type=F32)


def _dot_x3(a, b):
    ah, al = _split2(a)
    bh, bl = _split2(b)
    return _dot(ah, bh) + (_dot(ah, bl) + _dot(al, bh))


def _dot_exact_lhs(a_bf16, b):
    bh, bm, bl = _split3(b)
    return _dot(a_bf16, bh) + (_dot(a_bf16, bm) + _dot(a_bf16, bl))


def _cmul(ar, ai, br, bi):
    return ar * br - ai * bi, ar * bi + ai * br


U32 = jnp.uint32


def _pack_halves(x):
    half = x.shape[1] // 2
    lo = lax.bitcast_convert_type(x[:, :half].astype(BF16).astype(F32), U32)
    hi = lax.bitcast_convert_type(x[:, half:].astype(BF16).astype(F32), U32)
    return (lo >> 16) | hi


def _unpack_halves(p):
    lo = lax.bitcast_convert_type(p << 16, F32)
    hi = lax.bitcast_convert_type(p & jnp.uint32(0xFFFF0000), F32)
    return lo, hi


def _inproj_kernel(x_ref, g_ref, w_ref, o_ref, hn_ref):
    @pl.when(pl.program_id(1) == 0)
    def _():
        x = x_ref[...]
        ms = jnp.mean(x * x, axis=-1, keepdims=True)
        hn_ref[...] = (x * lax.rsqrt(ms + RMS_EPS) * g_ref[...]).astype(BF16)

    o_ref[...] = _dot(hn_ref[...], w_ref[...])


def _inproj(x2, g, w_bf16, tm, tn):
    T, D = x2.shape
    N = w_bf16.shape[1]
    return pl.pallas_call(
        _inproj_kernel,
        grid=(T // tm, N // tn),
        in_specs=[
            pl.BlockSpec((tm, D), lambda i, j: (i, 0)),
            pl.BlockSpec((1, D), lambda i, j: (0, 0)),
            pl.BlockSpec((D, tn), lambda i, j: (0, j)),
        ],
        out_specs=pl.BlockSpec((tm, tn), lambda i, j: (i, j)),
        out_shape=jax.ShapeDtypeStruct((T, N), F32),
        scratch_shapes=[pltpu.VMEM((tm, D), BF16)],
        compiler_params=_cparams("arbitrary", "arbitrary"),
        name="inproj",
    )(x2, g, w_bf16)


def _binpow(ar, ai, expo, nbits):
    pr = jnp.ones(expo.shape, F32)
    pi = jnp.zeros(expo.shape, F32)
    sr, si = ar, ai
    for bit in range(nbits):
        m = ((expo >> bit) & 1) == 1
        nr, ni = _cmul(pr, pi, sr, si)
        pr = jnp.where(m, nr, pr)
        pi = jnp.where(m, ni, pi)
        if bit + 1 < nbits:
            sr, si = _cmul(sr, si, sr, si)
    return pr, pi


def _zoh(lr, li, dt):
    mag = jnp.exp(lr * dt)
    ang = li * dt
    ar, ai = mag * jnp.cos(ang), mag * jnp.sin(ang)
    den = lr * lr + li * li
    nr, ni = ar - 1.0, ai
    fr = (nr * lr + ni * li) / den
    fi = (ni * lr - nr * li) / den
    return ar, ai, fr, fi


def _s5_ops_kernel(lrc_ref, lic_ref, lrr_ref, lir_ref, dt_ref, ct_re_ref, ct_im_ref,
                   bt_re_ref, bt_im_ref, btr_re_ref, btr_im_ref,
                   mt_ref, wbt_ref, wc_ref, a64_ref):
    C, HG, P = S5_CHUNK, S5_HG, S5_P
    W = C * HG
    dt = jnp.exp(dt_ref[0])
    ar_c, ai_c, fr_c, fi_c = _zoh(lrc_ref[0], lic_ref[0], dt)
    ar_r, ai_r, fr_r, fi_r = _zoh(lrr_ref[0], lir_ref[0], dt)

    tau = lax.broadcasted_iota(I32, (1, W), 1) // HG
    nbits = max(1, (C - 1).bit_length())
    pr, pi = _binpow(ar_c, ai_c, jnp.broadcast_to(tau, (P, W)), nbits)
    qr, qi = _binpow(ar_c, ai_c, jnp.broadcast_to(C - 1 - tau, (P, W)), nbits)

    sel = (lax.broadcasted_iota(I32, (HG, W), 1) % HG == lax.broadcasted_iota(I32, (HG, W), 0)).astype(BF16)
    tile = lambda ref: _dot_exact_lhs_rhs(ref[0], sel)

    ca_re, ca_im = _cmul(tile(ct_re_ref), tile(ct_im_ref), pr, pi)
    bbt_re, bbt_im = _cmul(btr_re_ref[0], btr_im_ref[0], fr_r, fi_r)
    kt = _dot_x3(bbt_re, ca_re) - _dot_x3(bbt_im, ca_im)

    c1_re, c1_im = _cmul(ca_re, ca_im, ar_c, ai_c)
    wc_ref[0, :P, :] = c1_re.astype(BF16)
    wc_ref[0, P:, :] = (-c1_im).astype(BF16)

    bb_re, bb_im = _cmul(tile(bt_re_ref), tile(bt_im_ref), fr_c, fi_c)
    ab_re, ab_im = _cmul(bb_re, bb_im, qr, qi)
    wbt_ref[0, :P, :] = ab_re.astype(BF16)
    wbt_ref[0, P:, :] = ab_im.astype(BF16)

    cr, ci = ar_r, ai_r
    for _ in range(C.bit_length() - 1):
        cr, ci = _cmul(cr, ci, cr, ci)
    a64_ref[0] = jnp.concatenate([cr, ci], axis=1)

    ext = jnp.concatenate([jnp.zeros((HG, W), F32), kt], axis=1)
    for s in range(C):
        off = W - s * HG
        mt_ref[0, s * HG:(s + 1) * HG, :] = ext[:, off:off + W].astype(BF16)


def _s5_ops(lam_re, lam_im, log_step, ct_re, ct_im, bt_re, bt_im, btr_re, btr_im):
    G, P = lam_re.shape
    C, HG = S5_CHUNK, S5_HG
    W = C * HG
    col = lambda a: a.reshape(G, P, 1)
    row = lambda a: a.reshape(G, 1, P)
    g3 = lambda s1, s2: pl.BlockSpec((1, s1, s2), lambda g: (g, 0, 0))
    return pl.pallas_call(
        _s5_ops_kernel,
        grid=(G,),
        in_specs=[g3(P, 1), g3(P, 1), g3(1, P), g3(1, P), g3(1, 1),
                  g3(P, HG), g3(P, HG), g3(P, HG), g3(P, HG), g3(HG, P), g3(HG, P)],
        out_specs=[g3(W, W), g3(2 * P, W), g3(2 * P, W), g3(1, 2 * P)],
        out_shape=[jax.ShapeDtypeStruct((G, W, W), BF16),
                   jax.ShapeDtypeStruct((G, 2 * P, W), BF16),
                   jax.ShapeDtypeStruct((G, 2 * P, W), BF16),
                   jax.ShapeDtypeStruct((G, 1, 2 * P), F32)],
        compiler_params=_cparams("arbitrary"),
        name="s5_ops",
    )(col(lam_re), col(lam_im), row(lam_re), row(lam_im), log_step.reshape(G, 1, 1),
      ct_re, ct_im, bt_re, bt_im, btr_re, btr_im)


def _s5_chunk_kernel(u_ref, mt_ref, wbt_ref, wc_ref, a64_ref, y_ref, *, n_chunks):
    P = S5_P
    u = u_ref[0]
    R = u.shape[0]
    y = _dot(u, mt_ref[0])
    x = _dot_nt(u, wbt_ref[0])
    a = a64_ref[0]
    lane = lax.broadcasted_iota(I32, (1, 2 * P), 1)
    ar = jnp.where(lane < P, a, pltpu.roll(a, P, 1))
    ai = jnp.where(lane < P, -pltpu.roll(a, P, 1), a)
    cidx = lax.broadcasted_iota(I32, (R, 1), 0) % n_chunks
    shift = 1
    while shift < n_chunks:
        xs = jnp.where(cidx >= shift, pltpu.roll(x, shift, 0), 0.0)
        x = x + ar * xs + ai * pltpu.roll(xs, P, 1)
        nar = ar * ar - ai * ai
        ai = 2.0 * ar * ai
        ar = nar
        shift *= 2
    s_in = jnp.where(cidx >= 1, pltpu.roll(x, 1, 0), 0.0)
    sh, sl = _split2(s_in)
    wc = wc_ref[0]
    y_ref[0] = y + (_dot(sh, wc) + _dot(sl, wc))


def _s5_chunk(u_g, mt, wbt, wc, a64, n_chunks):
    G, R, W = u_g.shape
    P2 = wbt.shape[1]
    g3 = lambda s1, s2: pl.BlockSpec((1, s1, s2), lambda g: (g, 0, 0))
    return pl.pallas_call(
        functools.partial(_s5_chunk_kernel, n_chunks=n_chunks),
        grid=(G,),
        in_specs=[g3(R, W), g3(W, W), g3(P2, W), g3(P2, W), g3(1, P2)],
        out_specs=g3(R, W),
        out_shape=jax.ShapeDtypeStruct((G, R, W), F32),
        compiler_params=_cparams("arbitrary"),
        name="s5_chunk",
    )(u_g, mt, wbt, wc, a64)


def _gelu_tanh(x):
    return 0.5 * x * (1.0 + jnp.tanh(0.7978845608028654 * (x + 0.044715 * (x * x * x))))


def _s5_glu_kernel(y_ref, u_ref, d_ref, w_ref, b_ref, o_ref):
    y = y_ref[...] + d_ref[...] * u_ref[...]
    y = _gelu_tanh(y)
    z = _dot(y.astype(BF16), w_ref[...]) + b_ref[...]
    o_ref[...] = (y * jax.nn.sigmoid(z)).astype(BF16)


def _s5_glu(y_ssm, proj, d, w_bf16, b, tm):
    T, W = y_ssm.shape
    return pl.pallas_call(
        _s5_glu_kernel,
        grid=(T // tm,),
        in_specs=[
            pl.BlockSpec((tm, W), lambda i: (i, 0)),
            pl.BlockSpec((tm, W), lambda i: (i, 0)),
            pl.BlockSpec((1, W), lambda i: (0, 0)),
            pl.BlockSpec((W, W), lambda i: (0, 0)),
            pl.BlockSpec((1, W), lambda i: (0, 0)),
        ],
        out_specs=pl.BlockSpec((tm, W), lambda i: (i, 0)),
        out_shape=jax.ShapeDtypeStruct((T, W), BF16),
        compiler_params=_cparams("arbitrary"),
        name="s5_glu",
    )(y_ssm, proj, d, w_bf16, b)


def _head_sum(x, e_ref, et_ref):
    xh, xl = _split2(x)
    e = e_ref[...]
    s = _dot(xh, e) + _dot(xl, e)
    sh, sl = _split2(s)
    et = et_ref[...]
    return _dot(sh, et) + _dot(sl, et)


def _shift(z, prev_row):
    rolled = pltpu.roll(z, 1, 0)
    first = lax.broadcasted_iota(I32, (z.shape[0], 1), 0) == 0
    return jnp.where(first, prev_row, rolled)


def _rwkv_prep_kernel(zr_ref, zk_ref, zv_ref, zl_ref, mu_ref, mul_ref, w0_ref, a0_ref, kk_ref, ka_ref,
                      w2_ref, a2_ref, g2_ref, e_ref, et_ref,
                      r_ref, k_ref, v_ref, nk_ref, b_ref, lw_ref, g_ref,
                      car_ref, carl_ref):
    W = r_ref.shape[1]

    @pl.when(pl.program_id(1) == 0)
    def _():
        car_ref[...] = jnp.zeros_like(car_ref)
        carl_ref[...] = jnp.zeros_like(carl_ref)

    def lerp(z, prev_row, mu):
        return z + (_shift(z, prev_row) - z) * mu

    tm = zr_ref.shape[0]
    zr, zk, zv, zl = zr_ref[...], zk_ref[...], zv_ref[...], zl_ref[...]
    r = lerp(zr, car_ref[0:1, :], mu_ref[0:1, :])
    k = lerp(zk, car_ref[1:2, :], mu_ref[1:2, :])
    v = lerp(zv, car_ref[2:3, :], mu_ref[2:3, :])
    xl = lerp(zl, carl_ref[0:1, :], mul_ref[...])
    car_ref[0:1, :] = zr[tm - 1:tm, :]
    car_ref[1:2, :] = zk[tm - 1:tm, :]
    car_ref[2:3, :] = zv[tm - 1:tm, :]
    carl_ref[0:1, :] = zl[tm - 1:tm, :]

    xw, xa, xg = xl[:, 0:128], xl[:, 128:256], xl[:, 256:512]
    dw = _dot(jnp.tanh(xw).astype(BF16), w2_ref[...])
    da = _dot(xa.astype(BF16), a2_ref[...])
    g = _dot(jax.nn.sigmoid(xg).astype(BF16), g2_ref[...])

    zw = -(w0_ref[...] + dw)
    softplus = jnp.maximum(zw, 0.0) + jnp.log(1.0 + jnp.exp(-jnp.abs(zw)))
    w_log = -softplus - 0.5
    a = jax.nn.sigmoid(a0_ref[...] + da)

    kk = k * kk_ref[...]
    n2 = _head_sum(kk * kk, e_ref, et_ref)
    kk = kk / jnp.maximum(jnp.sqrt(n2), 1e-12)

    r_ref[...] = r
    k_ref[...] = k * (1.0 + (a - 1.0) * ka_ref[...])
    v_ref[...] = v
    nk_ref[...] = kk
    b_ref[...] = kk * a
    lw_ref[...] = -jnp.exp(w_log)
    g_ref[...] = g


def _rwkv_prep(proj, B, L, mu3, mul, w0, a0, k_k, k_a, w2p, a2p, g2p, e_mat, et_mat, tm):
    T = B * L
    W = w0.shape[1]
    nt = L // tm
    row = lambda c: pl.BlockSpec((tm, W), lambda b, i, c=c: (b * nt + i, c))
    full = lambda a: pl.BlockSpec(a.shape, lambda b, i: (0, 0))
    out = pl.BlockSpec((tm, W), lambda b, i: (b * nt + i, 0))
    lw = 512
    return pl.pallas_call(
        _rwkv_prep_kernel,
        grid=(B, nt),
        in_specs=[row(1), row(2), row(3),
                  pl.BlockSpec((tm, lw), lambda b, i: (b * nt + i, 4 * W // lw)),
                  full(mu3), full(mul), full(w0), full(a0), full(k_k), full(k_a),
                  full(w2p), full(a2p), full(g2p), full(e_mat), full(et_mat)],
        out_specs=[out] * 7,
        out_shape=[jax.ShapeDtypeStruct((T, W), F32)] * 7,
        scratch_shapes=[pltpu.VMEM((8, W), F32), pltpu.VMEM((8, lw), F32)],
        compiler_params=_cparams("arbitrary", "arbitrary"),
        name="rwkv_prep",
    )(proj, proj, proj, proj, mu3, mul, w0, a0, k_k, k_a, w2p, a2p, g2p, e_mat, et_mat)


def _rwkv_chunk_kernel(r_ref, k_ref, v_ref, nk_ref, b_ref, lw_ref, y_ref, z_ref):
    C, N = RW_CHUNK, RW_N
    H = r_ref.shape[1] // N

    @pl.when(pl.program_id(1) == 0)
    def _():
        z_ref[...] = jnp.zeros_like(z_ref)

    ri = lax.broadcasted_iota(I32, (C, C), 0)
    ci = lax.broadcasted_iota(I32, (C, C), 1)
    tril = ri >= ci
    stril = ri > ci
    eye = (ri == ci).astype(F32)

    lw = lw_ref[...]
    cs = _dot_exact_lhs(tril.astype(BF16), lw)
    tot = cs[C - 1:C, :]
    p_inc = jnp.exp(cs)
    p_inv = jnp.exp(-cs)
    p_prev = jnp.exp(cs - lw)
    p_rest = jnp.exp(tot - cs)
    p_tot = jnp.exp(tot)

    r_t = r_ref[...] * p_inc
    k_t = k_ref[...] * p_inv
    a_t = -nk_ref[...] * p_prev
    b_t = b_ref[...] * p_inv
    k_h = k_ref[...] * p_rest
    b_h = b_ref[...] * p_rest
    v_all = v_ref[...]

    hs = range(H)
    sl = [slice(h * N, (h + 1) * N) for h in hs]
    each = lambda f: [f(h) for h in hs]
    v = each(lambda h: v_all[:, sl[h]].astype(BF16))
    ar = each(lambda h: jnp.concatenate([a_t[:, sl[h]], r_t[:, sl[h]]], axis=0).astype(BF16))
    m_b = each(lambda h: _dot_nt(ar[h], b_t[:, sl[h]].astype(BF16)))
    m_k = each(lambda h: _dot_nt(ar[h], k_t[:, sl[h]].astype(BF16)))
    l_ab = each(lambda h: jnp.where(stril, m_b[h][:C], 0.0).astype(BF16))
    m_rb = each(lambda h: jnp.where(tril, m_b[h][C:], 0.0).astype(BF16))
    l_ak = each(lambda h: jnp.where(stril, m_k[h][:C], 0.0).astype(BF16))
    m_rk = each(lambda h: jnp.where(tril, m_k[h][C:], 0.0).astype(BF16))

    x = each(lambda h: jnp.concatenate([_dot(l_ak[h], v[h]), a_t[:, sl[h]]], axis=1))
    y0 = each(lambda h: _dot(m_rk[h], v[h]))
    hk = each(lambda h: _dot_tn(k_h[:, sl[h]].astype(BF16), v[h]))
    lp = l_ab
    step = 1
    while step < C:
        x = each(lambda h: x[h] + _dot(lp[h], x[h].astype(BF16)))
        step *= 2
        if step < C:
            lp = each(lambda h: _dot(lp[h], lp[h]).astype(BF16))
    xb = each(lambda h: x[h].astype(BF16))
    yq = each(lambda h: _dot(m_rb[h], xb[h]))
    gh = each(lambda h: _dot_tn(b_h[:, sl[h]].astype(BF16), xb[h]))

    z = each(lambda h: _split2(z_ref[h]))
    qb = each(lambda h: (yq[h][:, N:] + r_t[:, sl[h]]).astype(BF16))
    gb = each(lambda h: (gh[h][:, N:] + eye * p_tot[:, sl[h]]).astype(BF16))
    yz = each(lambda h: _dot(qb[h], z[h][0]) + _dot(qb[h], z[h][1]))
    gz = each(lambda h: _dot(gb[h], z[h][0]) + _dot(gb[h], z[h][1]))
    for h in hs:
        y_ref[:, sl[h]] = (yq[h][:, :N] + y0[h]) + yz[h]
        z_ref[h] = (gh[h][:, :N] + hk[h]) + gz[h]


def _rwkv_chunk(r, k, v, nk, b, lw, B, L):
    T, W = r.shape
    C = RW_CHUNK
    nc = L // C
    H = W // RW_N
    spec = pl.BlockSpec((C, W), lambda bi, c: (bi * nc + c, 0))
    return pl.pallas_call(
        _rwkv_chunk_kernel,
        grid=(B, nc),
        in_specs=[spec] * 6,
        out_specs=spec,
        out_shape=jax.ShapeDtypeStruct((T, W), F32),
        scratch_shapes=[pltpu.VMEM((H, RW_N, RW_N), F32)],
        compiler_params=_cparams("arbitrary", "arbitrary"),
        name="rwkv_chunk",
    )(r, k, v, nk, b, lw)


def _rwkv_post_kernel(y_ref, r_ref, k_ref, v_ref, g_ref, rk_ref, lnw_ref, lnb_ref, e_ref, et_ref, o_ref):
    inv_n = 1.0 / RW_N
    y = y_ref[...]
    mean = _head_sum(y, e_ref, et_ref) * inv_n
    yc = y - mean
    var = _head_sum(yc * yc, e_ref, et_ref) * inv_n
    yn = yc * lax.rsqrt(var + RW_GN_EPS) * lnw_ref[...] + lnb_ref[...]
    v = v_ref[...]
    bonus = _head_sum(r_ref[...] * k_ref[...] * rk_ref[...], e_ref, et_ref) * v
    o_ref[...] = ((yn + bonus) * g_ref[...]).astype(BF16)


def _rwkv_post(y, r, k, v, g, r_k, ln_w, ln_b, e_mat, et_mat, tm):
    T, W = y.shape
    row = pl.BlockSpec((tm, W), lambda i: (i, 0))
    full = lambda a: pl.BlockSpec(a.shape, lambda i: (0, 0))
    return pl.pallas_call(
        _rwkv_post_kernel,
        grid=(T // tm,),
        in_specs=[row] * 5 + [full(r_k), full(ln_w), full(ln_b), full(e_mat), full(et_mat)],
        out_specs=row,
        out_shape=jax.ShapeDtypeStruct((T, W), BF16),
        compiler_params=_cparams("arbitrary"),
        name="rwkv_post",
    )(y, r, k, v, g, r_k, ln_w, ln_b, e_mat, et_mat)


def _first_index_of_max(vals, lane, valid):
    neg = jnp.float32(-jnp.inf)
    masked = jnp.where(valid, vals, neg)
    m = jnp.max(masked, axis=-1, keepdims=True)
    idx = jnp.min(jnp.where(valid & (masked == m), lane, LANES), axis=-1, keepdims=True)
    return m, idx


def _outproj_kernel(s5_ref, rw_ref, x_ref, wt_ref, wb_ref, g_ref, wrh_ref, wrl_ref, br_ref,
                    h_ref, hn_ref, rec_ref):
    h = x_ref[...] + (_dot(s5_ref[...], wt_ref[...]) + _dot(rw_ref[...], wb_ref[...]))
    h_ref[...] = h
    ms = jnp.mean(h * h, axis=-1, keepdims=True)
    hn = h * lax.rsqrt(ms + RMS_EPS) * g_ref[...]
    hn_ref[...] = _pack_halves(hn).reshape(hn_ref.shape)

    hh, hl = _split2(hn)
    wh, wl = wrh_ref[...], wrl_ref[...]
    logits = _dot(hh, wh) + (_dot(hh, wl) + _dot(hl, wh)) + br_ref[...]
    lane = lax.broadcasted_iota(I32, logits.shape, 1)
    is_grp = (lane >= N_EXPERTS) & (lane < N_EXPERTS + N_GROUPS)
    gmax, gidx = _first_index_of_max(logits, lane, is_grp)
    gsum = jnp.sum(jnp.where(is_grp, jnp.exp(logits - gmax), 0.0), axis=-1, keepdims=True)
    p_grp = 1.0 / gsum
    grp = gidx - N_EXPERTS
    in_grp = (lane >= grp * EPG) & (lane < (grp + 1) * EPG)
    m1, i1 = _first_index_of_max(logits, lane, in_grp)
    m2, i2 = _first_index_of_max(logits, lane, in_grp & (lane != i1))
    e = jnp.exp(m2 - m1)
    g1 = p_grp / (1.0 + e)
    g2 = p_grp * e / (1.0 + e)
    rec = jnp.where(lane == 0, i1.astype(F32),
          jnp.where(lane == 1, i2.astype(F32),
          jnp.where(lane == 2, g1, jnp.where(lane == 3, g2, 0.0))))
    rec_ref[...] = rec


def _outproj(s5o, rwo, x2, w_top, w_bot, g, wr_hi, wr_lo, b_route, tm):
    T, D = x2.shape
    W = s5o.shape[1]
    full = lambda a: pl.BlockSpec(a.shape, lambda i: (0, 0))
    return pl.pallas_call(
        _outproj_kernel,
        grid=(T // tm,),
        in_specs=[pl.BlockSpec((tm, W), lambda i: (i, 0)), pl.BlockSpec((tm, W), lambda i: (i, 0)),
                  pl.BlockSpec((tm, D), lambda i: (i, 0)),
                  full(w_top), full(w_bot), full(g), full(wr_hi), full(wr_lo), full(b_route)],
        out_specs=[pl.BlockSpec((tm, D), lambda i: (i, 0)), pl.BlockSpec((tm, 1, D // 2), lambda i: (i, 0, 0)),
                   pl.BlockSpec((tm, LANES), lambda i: (i, 0))],
        out_shape=[jax.ShapeDtypeStruct((T, D), F32), jax.ShapeDtypeStruct((T, 1, D // 2), U32),
                   jax.ShapeDtypeStruct((T, LANES), F32)],
        compiler_params=_cparams("arbitrary"),
        name="outproj_route",
    )(s5o, rwo, x2, w_top, w_bot, g, wr_hi, wr_lo, b_route)


def _onehots(rec, lane):
    oh0 = (lane == rec[:, 0:1].astype(I32)).astype(F32)
    oh1 = (lane == rec[:, 1:2].astype(I32)).astype(F32)
    return oh0, oh1


def _rank_kernel(rec_ref, rank_ref, cnt_ref, base_ref):
    tb = rec_ref.shape[0]

    @pl.when(pl.program_id(0) == 0)
    def _():
        base_ref[...] = jnp.zeros_like(base_ref)

    lane = lax.broadcasted_iota(I32, (tb, LANES), 1)
    oh0, oh1 = _onehots(rec_ref[...], lane)
    both = oh0 + oh1
    ri = lax.broadcasted_iota(I32, (tb, tb), 0)
    ci = lax.broadcasted_iota(I32, (tb, tb), 1)
    before = _dot((ri > ci).astype(BF16), both.astype(BF16)) + base_ref[0:1, :]
    rank0 = jnp.sum(oh0 * before, axis=-1, keepdims=True)
    rank1 = jnp.sum(oh1 * before, axis=-1, keepdims=True)
    rank_ref[...] = jnp.where(lane == 0, rank0, jnp.where(lane == 1, rank1, 0.0))
    total = base_ref[0:1, :] + jnp.sum(both, axis=0, keepdims=True)
    base_ref[0:1, :] = total
    cnt_ref[...] = jnp.broadcast_to(total, cnt_ref.shape)


def _rank(rec, tb):
    T = rec.shape[0]
    return pl.pallas_call(
        _rank_kernel,
        grid=(T // tb,),
        in_specs=[pl.BlockSpec((tb, LANES), lambda i: (i, 0))],
        out_specs=[pl.BlockSpec((tb, LANES), lambda i: (i, 0)), pl.BlockSpec((8, LANES), lambda i: (0, 0))],
        out_shape=[jax.ShapeDtypeStruct((T, LANES), F32), jax.ShapeDtypeStruct((8, LANES), F32)],
        scratch_shapes=[pltpu.VMEM((8, LANES), F32)],
        compiler_params=_cparams("arbitrary"),
        name="moe_rank",
    )(rec)


def _padded_starts(cnt):
    padded = jnp.ceil(cnt * (1.0 / MOE_BLOCK)) * MOE_BLOCK
    ri = lax.broadcasted_iota(I32, (LANES, LANES), 0)
    ci = lax.broadcasted_iota(I32, (LANES, LANES), 1)
    p8 = jnp.broadcast_to(padded, (8, LANES))
    pend = _dot_exact_lhs_rhs(p8, (ri <= ci).astype(BF16))[0:1, :]
    return pend - padded, pend


def _dot_exact_lhs_rhs(a, b_bf16):
    ah, am, al = _split3(a)
    return _dot(ah, b_bf16) + (_dot(am, b_bf16) + _dot(al, b_bf16))


def _dest_kernel(rec_ref, rank_ref, cnt_ref, dest_ref, blk_ref):
    tb = rec_ref.shape[0]
    cnt = cnt_ref[0:1, :]
    pstart, pend = _padded_starts(cnt)
    lane = lax.broadcasted_iota(I32, (tb, LANES), 1)
    oh0, oh1 = _onehots(rec_ref[...], lane)
    rank = rank_ref[...]
    d0 = jnp.sum(oh0 * pstart, axis=-1, keepdims=True) + rank[:, 0:1]
    d1 = jnp.sum(oh1 * pstart, axis=-1, keepdims=True) + rank[:, 1:2]
    dest_ref[...] = jnp.where(lane == 0, d0, jnp.where(lane == 1, d1, 0.0)).astype(I32)

    @pl.when(pl.program_id(0) == 0)
    def _():
        nb = blk_ref.shape[0]
        blane = lax.broadcasted_iota(I32, (nb, LANES), 1)
        bstart = (lax.broadcasted_iota(I32, (nb, 1), 0) * MOE_BLOCK).astype(F32)
        is_e = blane < N_EXPERTS
        bexp = jnp.sum(jnp.where(is_e & (pend <= bstart), 1.0, 0.0), axis=-1, keepdims=True)
        bexp = jnp.minimum(bexp, N_EXPERTS - 1.0)
        nact = jnp.max(jnp.where(is_e, pend, 0.0), axis=-1, keepdims=True) * (1.0 / MOE_BLOCK)
        lastblk = jnp.where(is_e & (pend > pstart), pend - MOE_BLOCK, -1.0)
        brow = lax.broadcasted_iota(I32, (nb, 1), 0)
        out = jnp.where(blane == 0, bexp, jnp.where(blane == 1, nact, 0.0))
        out = jnp.where(brow == nb - 1, lastblk, out)
        blk_ref[...] = out.astype(I32)


def _dest(rec, rank, cnt, tb, nb_rows):
    T = rec.shape[0]
    return pl.pallas_call(
        _dest_kernel,
        grid=(T // tb,),
        in_specs=[pl.BlockSpec((tb, LANES), lambda i: (i, 0)), pl.BlockSpec((tb, LANES), lambda i: (i, 0)),
                  pl.BlockSpec((8, LANES), lambda i: (0, 0))],
        out_specs=[pl.BlockSpec((tb, LANES), lambda i: (i, 0)), pl.BlockSpec((nb_rows, LANES), lambda i: (0, 0))],
        out_shape=[jax.ShapeDtypeStruct((T, LANES), I32), jax.ShapeDtypeStruct((nb_rows, LANES), I32)],
        compiler_params=_cparams("arbitrary"),
        name="moe_dest",
    )(rec, rank, cnt)


def _scatter_kernel(pad_ref, na_ref, dest_ref, hn_ref, xs_ref, zbuf_ref, idx_ref, sa_ref, sb_ref,
                    sem_a, sem_b, isem, zsem, *, tb, n_blocks, n_tail, n_steps):
    i = pl.program_id(0)

    @pl.when(i == 0)
    def _():
        zbuf_ref[...] = jnp.zeros_like(zbuf_ref)

        def zero_rows(start):
            return pltpu.make_async_copy(zbuf_ref, xs_ref.at[pl.ds(start, MOE_BLOCK)], zsem)

        def pad_start(e, c):
            @pl.when(pad_ref[e] >= 0)
            def _():
                zero_rows(pad_ref[e]).start()
            return c

        def pad_wait(e, c):
            @pl.when(pad_ref[e] >= 0)
            def _():
                zero_rows(pad_ref[e]).wait()
            return c

        def tail_start(k, c):
            @pl.when(na_ref[0] + k < n_blocks)
            def _():
                zero_rows((na_ref[0] + k) * MOE_BLOCK).start()
            return c

        def tail_wait(k, c):
            @pl.when(na_ref[0] + k < n_blocks)
            def _():
                zero_rows((na_ref[0] + k) * MOE_BLOCK).wait()
            return c

        lax.fori_loop(0, N_EXPERTS, pad_start, 0)
        lax.fori_loop(0, n_tail, tail_start, 0)
        lax.fori_loop(0, N_EXPERTS, pad_wait, 0)
        lax.fori_loop(0, n_tail, tail_wait, 0)

    def idx_copy(step, slot):
        return pltpu.make_async_copy(dest_ref.at[step], idx_ref.at[slot], isem)

    @pl.when(i == 0)
    def _():
        idx_copy(0, 0).start()

    slot = i % 2
    idx_copy(i, slot).wait()

    @pl.when(i + 1 < n_steps)
    def _():
        idx_copy(i + 1, 1 - slot).start()

    hb = tb // 2
    for half, (buf, sem) in enumerate(((sa_ref, sem_a), (sb_ref, sem_b))):
        def wait_rows(buf=buf, sem=sem):
            for _ in range(2):
                pltpu.make_async_copy(buf, xs_ref.at[pl.ds(0, hb)], sem).wait()

        @pl.when(i > 0)
        def _():
            wait_rows()

        buf[...] = hn_ref[half * hb:(half + 1) * hb]

        def issue(t, c, buf=buf, sem=sem, half=half):
            for s in range(2):
                d = idx_ref[slot, 2 * (half * hb + t) + s]
                pltpu.make_async_copy(buf.at[pl.ds(t, 1)], xs_ref.at[pl.ds(d, 1)], sem).start()
            return c

        lax.fori_loop(0, hb, issue, 0, unroll=8)

    @pl.when(i == n_steps - 1)
    def _():
        for buf, sem in ((sa_ref, sem_a), (sb_ref, sem_b)):
            for _ in range(2):
                pltpu.make_async_copy(buf, xs_ref.at[pl.ds(0, hb)], sem).wait()


def _scatter(lastblk, nact, dest2, hn, tb, n_blocks):
    T, _, D = hn.shape
    n_tail = n_blocks - (-(-2 * T // MOE_BLOCK))
    hb = tb // 2
    return pl.pallas_call(
        functools.partial(_scatter_kernel, tb=tb, n_blocks=n_blocks, n_tail=n_tail, n_steps=T // tb),
        grid_spec=pltpu.PrefetchScalarGridSpec(
            num_scalar_prefetch=2, grid=(T // tb,),
            in_specs=[pl.BlockSpec(memory_space=pl.ANY),
                      pl.BlockSpec((tb, 1, D), lambda i, pad, na: (i, 0, 0))],
            out_specs=pl.BlockSpec(memory_space=pl.ANY),
            scratch_shapes=[pltpu.VMEM((MOE_BLOCK, 1, D), hn.dtype), pltpu.SMEM((2, 2 * tb), I32),
                            pltpu.VMEM((hb, 1, D), hn.dtype), pltpu.VMEM((hb, 1, D), hn.dtype),
                            pltpu.SemaphoreType.DMA, pltpu.SemaphoreType.DMA,
                            pltpu.SemaphoreType.DMA, pltpu.SemaphoreType.DMA]),
        out_shape=jax.ShapeDtypeStruct((n_blocks * MOE_BLOCK, 1, D), hn.dtype),
        compiler_params=_cparams("arbitrary"),
        name="moe_scatter",
    )(lastblk, nact, dest2, hn)


def _expert_kernel(be_ref, na_ref, x_ref, wg_ref, wu_ref, wd_ref, y_ref, wgb_ref, wub_ref, wdb_ref, x2_ref):
    j = pl.program_id(0)

    @pl.when(j < na_ref[0])
    def _():
        prev = be_ref[jnp.maximum(j - 1, 0)]

        @pl.when((j == 0) | (be_ref[j] != prev))
        def _():
            wgb_ref[...] = wg_ref[0].astype(BF16)
            wub_ref[...] = wu_ref[0].astype(BF16)
            wdb_ref[...] = wd_ref[0].astype(BF16)

        x2_ref[...] = x_ref[...].reshape(x2_ref.shape)
        lo, hi = _unpack_halves(x2_ref[...])
        lo, hi = lo.astype(BF16), hi.astype(BF16)
        half = lo.shape[1]
        hg = _dot(lo, wgb_ref[:half, :]) + _dot(hi, wgb_ref[half:, :])
        hu = _dot(lo, wub_ref[:half, :]) + _dot(hi, wub_ref[half:, :])
        act = (hg * jax.nn.sigmoid(hg)) * hu
        y = _dot(act.astype(BF16), wdb_ref[...])
        y_ref[...] = _pack_halves(y).reshape(y_ref.shape)

    @pl.when(j >= na_ref[0])
    def _():
        y_ref[...] = jnp.zeros_like(y_ref)


def _experts(blk_exp, nact, xs, w_gate, w_up, w_down, n_blocks):
    DP = xs.shape[2]
    D = 2 * DP
    rows = n_blocks * MOE_BLOCK
    DE = w_gate.shape[2]
    blk = lambda j, be, na: (jnp.minimum(j, na[0] - 1), 0, 0)
    wmap = lambda j, be, na: (be[jnp.minimum(j, na[0] - 1)], 0, 0)
    return pl.pallas_call(
        _expert_kernel,
        grid_spec=pltpu.PrefetchScalarGridSpec(
            num_scalar_prefetch=2, grid=(n_blocks,),
            in_specs=[pl.BlockSpec((MOE_BLOCK, 1, DP), blk),
                      pl.BlockSpec((1, D, DE), wmap), pl.BlockSpec((1, D, DE), wmap),
                      pl.BlockSpec((1, DE, D), wmap)],
            out_specs=pl.BlockSpec((MOE_BLOCK, 1, DP), lambda j, be, na: (j, 0, 0)),
            scratch_shapes=[pltpu.VMEM((D, DE), BF16), pltpu.VMEM((D, DE), BF16), pltpu.VMEM((DE, D), BF16),
                            pltpu.VMEM((MOE_BLOCK, DP), U32)]),
        out_shape=jax.ShapeDtypeStruct((rows, 1, DP), U32),
        compiler_params=_cparams("arbitrary"),
        name="moe_experts",
    )(blk_exp, nact, xs, w_gate, w_up, w_down)


def _combine_kernel(dest_ref, ys_ref, h_ref, rec_ref, g_ref, o_ref, idx_ref, ya0_ref, ya1_ref, yb0_ref, yb1_ref,
                    y2_ref, sem_a, sem_b, isem, *, normalize, n_steps):
    i = pl.program_id(0)
    tb = h_ref.shape[0]
    hb = tb // 2
    halves = ((ya0_ref, ya1_ref, sem_a), (yb0_ref, yb1_ref, sem_b))

    def idx_copy(step, slot):
        return pltpu.make_async_copy(dest_ref.at[step], idx_ref.at[slot], isem)

    def issue(half, slot):
        bufs, sem = halves[half][:2], halves[half][2]

        def body(t, c):
            for s in range(2):
                d = idx_ref[slot, 2 * (half * hb + t) + s]
                pltpu.make_async_copy(ys_ref.at[pl.ds(d, 1)], bufs[s].at[pl.ds(t, 1)], sem).start()
            return c

        lax.fori_loop(0, hb, body, 0, unroll=8)

    @pl.when(i == 0)
    def _():
        first = idx_copy(0, 0)
        first.start()
        first.wait()
        issue(0, 0)
        issue(1, 0)
        if n_steps > 1:
            idx_copy(1, 1).start()

    nslot = (i + 1) % 2

    @pl.when(i + 1 < n_steps)
    def _():
        idx_copy(i + 1, nslot).wait()

    rec = rec_ref[...]
    for half, (y0_ref, y1_ref, sem) in enumerate(halves):
        rows = slice(half * hb, (half + 1) * hb)
        pltpu.make_async_copy(ys_ref.at[pl.ds(0, hb)], y0_ref, sem).wait()
        pltpu.make_async_copy(ys_ref.at[pl.ds(0, hb)], y1_ref, sem).wait()
        y2_ref[...] = y0_ref[...].reshape(y2_ref.shape)
        h = h_ref[rows, :] + rec[rows, 2:3] * jnp.concatenate(_unpack_halves(y2_ref[...]), axis=1)
        y2_ref[...] = y1_ref[...].reshape(y2_ref.shape)
        h = h + rec[rows, 3:4] * jnp.concatenate(_unpack_halves(y2_ref[...]), axis=1)
        if normalize:
            ms = jnp.mean(h * h, axis=-1, keepdims=True)
            h = h * lax.rsqrt(ms + RMS_EPS) * g_ref[...]
        o_ref[rows, :] = h

        @pl.when(i + 1 < n_steps)
        def _():
            issue(half, nslot)

    @pl.when(i + 2 < n_steps)
    def _():
        idx_copy(i + 2, i % 2).start()


def _combine(dest2, ys, h, rec, g, tb, normalize):
    T, D = h.shape
    hb = tb // 2
    return pl.pallas_call(
        functools.partial(_combine_kernel, normalize=normalize, n_steps=T // tb),
        grid=(T // tb,),
        in_specs=[pl.BlockSpec(memory_space=pl.ANY), pl.BlockSpec(memory_space=pl.ANY),
                  pl.BlockSpec((tb, D), lambda i: (i, 0)), pl.BlockSpec((tb, LANES), lambda i: (i, 0)),
                  pl.BlockSpec((1, D), lambda i: (0, 0))],
        out_specs=pl.BlockSpec((tb, D), lambda i: (i, 0)),
        out_shape=jax.ShapeDtypeStruct((T, D), F32),
        scratch_shapes=[pltpu.SMEM((2, 2 * tb), I32)] + [pltpu.VMEM((hb, 1, D // 2), U32)] * 4 +
                       [pltpu.VMEM((hb, D // 2), U32)] + [pltpu.SemaphoreType.DMA] * 3,
        compiler_params=_cparams("arbitrary"),
        name="moe_combine",
    )(dest2, ys, h, rec, g)


def _pad_cols(a, n):
    return jnp.pad(a, ((0, 0), (0, n - a.shape[1])))


def _pad_rows(a, n, at=0):
    return jnp.pad(a, ((at, n - a.shape[0] - at), (0, 0)))


def _layer(x2, B, L, p):
    T, D = x2.shape
    W = p["s5_d"].shape[0]
    G, P = p["s5_lambda_re"].shape
    HG = W // G
    dl, al, gl = p["rwkv_w2"].shape[0], p["rwkv_a2"].shape[0], p["rwkv_g2"].shape[0]
    H = W // RW_N

    w_in = p["w_in"]
    o = W
    cols = [w_in[:, :W], w_in[:, o:o + 3 * W]]
    o += 3 * W
    cols += [_pad_cols(w_in[:, o:o + dl], 128), _pad_cols(w_in[:, o + dl:o + dl + al], 128),
             _pad_cols(w_in[:, o + dl + al:o + dl + al + gl], 256)]
    w_in_r = jnp.concatenate(cols, axis=1).astype(BF16)
    tm_in = min(1024, T)
    proj = _inproj(x2, p["norm_mix_g"].reshape(1, D), w_in_r, tm_in, 1152)

    C = S5_CHUNK
    nc = L // C
    ct_re = jnp.swapaxes(p["s5_c_re"], 1, 2)
    ct_im = jnp.swapaxes(p["s5_c_im"], 1, 2)
    bt_re, bt_im = p["s5_b_re"], p["s5_b_im"]
    btr_re, btr_im = jnp.swapaxes(p["s5_b_re"], 1, 2), jnp.swapaxes(p["s5_b_im"], 1, 2)
    mt, wbt, wc, a64 = _s5_ops(p["s5_lambda_re"], p["s5_lambda_im"], p["s5_log_step"],
                               ct_re, ct_im, bt_re, bt_im, btr_re, btr_im)
    u_g = proj[:, :W].astype(BF16).reshape(B * nc, C, G, HG).transpose(2, 0, 1, 3).reshape(G, B * nc, C * HG)
    y_g = _s5_chunk(u_g, mt, wbt, wc, a64, nc)
    y_ssm = y_g.reshape(G, B * nc, C, HG).transpose(1, 2, 0, 3).reshape(T, W)
    tm = min(512, T)
    s5_out = _s5_glu(y_ssm, proj, p["s5_d"].reshape(1, W), p["s5_w_glu"].astype(BF16),
                     p["s5_b_glu"].reshape(1, W), tm)

    mu = p["rwkv_mu"]
    mu3 = _pad_rows(mu[:3 * W].reshape(3, W), 8)
    o = 3 * W
    mul = jnp.concatenate([_pad_cols(mu[None, o:o + dl], 128), _pad_cols(mu[None, o + dl:o + dl + al], 128),
                           _pad_cols(mu[None, o + dl + al:], 256)], axis=1)
    w2p = _pad_rows(p["rwkv_w2"], 128).astype(BF16)
    a2p = _pad_rows(p["rwkv_a2"], 128).astype(BF16)
    g2p = _pad_rows(p["rwkv_g2"], 256).astype(BF16)
    head_of = jnp.arange(W, dtype=I32) // RW_N
    e_mat = (head_of[:, None] == jnp.arange(LANES, dtype=I32)[None, :]).astype(BF16)
    et_mat = e_mat.T
    row = lambda a: a.reshape(1, W)
    tm_rw = min(256, L)
    r, k, v, nk, bv, lw, g = _rwkv_prep(proj, B, L, mu3, mul, row(p["rwkv_w0"]), row(p["rwkv_a0"]),
                                        row(p["rwkv_k_k"]), row(p["rwkv_k_a"]), w2p, a2p, g2p,
                                        e_mat, et_mat, tm_rw)
    y_rw = _rwkv_chunk(r, k, v, nk, bv, lw, B, L)
    rw_out = _rwkv_post(y_rw, r, k, v, g, p["rwkv_r_k"].reshape(1, W), row(p["rwkv_ln_w"]),
                        row(p["rwkv_ln_b"]), e_mat, et_mat, tm)

    w_out = p["w_out"].astype(BF16)
    w_route = jnp.concatenate([p["w_route_exp"], p["w_route_grp"]], axis=1)
    w_route = _pad_cols(w_route, LANES)
    wr_hi = w_route.astype(BF16)
    wr_lo = (w_route - wr_hi.astype(F32)).astype(BF16)
    b_route = _pad_cols(jnp.concatenate([p["b_route_exp"], p["b_route_grp"]])[None, :], LANES)
    tm_o = min(256, T)
    h, hn, rec = _outproj(s5_out, rw_out, x2, w_out[:W], w_out[W:], p["norm_ffn_g"].reshape(1, D),
                          wr_hi, wr_lo, b_route, tm_o)

    tb = min(256, T)
    rank, cnt = _rank(rec, tb)
    n_blocks = -(-2 * T // MOE_BLOCK) + N_EXPERTS
    nb_rows = -(-(n_blocks + 1) // 8) * 8
    dest, blk = _dest(rec, rank, cnt, tb, nb_rows)
    blk_exp = blk[:n_blocks, 0]
    nact = blk[0:1, 1]
    lastblk = blk[nb_rows - 1, :N_EXPERTS]
    ts = min(256, T)
    dest2 = dest[:, :2].reshape(T // ts, 2 * ts)

    xs = _scatter(lastblk, nact, dest2, hn, ts, n_blocks)
    ys = _experts(blk_exp, nact, xs, p["w_gate"], p["w_up"], p["w_down"], n_blocks)
    return dest2, ys, h, rec, ts


def kernel(x, norm_mix_g, w_in, s5_lambda_re, s5_lambda_im, s5_log_step, s5_b_re, s5_b_im, s5_c_re, s5_c_im, s5_d, s5_w_glu, s5_b_glu, rwkv_mu, rwkv_w0, rwkv_w2, rwkv_a0, rwkv_a2, rwkv_g2, rwkv_k_k, rwkv_k_a, rwkv_r_k, rwkv_ln_w, rwkv_ln_b, w_out, norm_ffn_g, w_route_grp, b_route_grp, w_route_exp, b_route_exp, w_gate, w_up, w_down, norm_final_g):
    B, L, D = x.shape
    params = dict(
        norm_mix_g=norm_mix_g, w_in=w_in, s5_lambda_re=s5_lambda_re, s5_lambda_im=s5_lambda_im,
        s5_log_step=s5_log_step, s5_b_re=s5_b_re, s5_b_im=s5_b_im, s5_c_re=s5_c_re, s5_c_im=s5_c_im,
        s5_d=s5_d, s5_w_glu=s5_w_glu, s5_b_glu=s5_b_glu, rwkv_mu=rwkv_mu, rwkv_w0=rwkv_w0, rwkv_w2=rwkv_w2,
        rwkv_a0=rwkv_a0, rwkv_a2=rwkv_a2, rwkv_g2=rwkv_g2, rwkv_k_k=rwkv_k_k, rwkv_k_a=rwkv_k_a,
        rwkv_r_k=rwkv_r_k, rwkv_ln_w=rwkv_ln_w, rwkv_ln_b=rwkv_ln_b, w_out=w_out, norm_ffn_g=norm_ffn_g,
        w_route_grp=w_route_grp, b_route_grp=b_route_grp, w_route_exp=w_route_exp, b_route_exp=b_route_exp,
        w_gate=w_gate, w_up=w_up, w_down=w_down)
    depth = norm_mix_g.shape[0]
    h2 = x.reshape(B * L, D)
    for l in range(depth):
        p = {k_: v_[l] for k_, v_ in params.items()}
        dest2, ys, h, rec, ts = _layer(h2, B, L, p)
        h2 = _combine(dest2, ys, h, rec, norm_final_g.reshape(1, D), ts, normalize=(l == depth - 1))
    return h2.reshape(B, L, D)
```

```python
import functools

import jax
import jax.numpy as jnp
from jax import lax
from jax.experimental import pallas as pl
from jax.experimental.pallas import tpu as pltpu

F32 = jnp.float32
BF16 = jnp.bfloat16
I32 = jnp.int32

RMS_EPS = 1e-6
S5_CHUNK = 64
S5_HG = 16
S5_P = 64
RW_N = 64
RW_CHUNK = 64
RW_GN_EPS = 64e-5
N_GROUPS = 8
EPG = 8
N_EXPERTS = 64
MOE_BLOCK = 256
LANES = 128
VMEM_LIMIT = 56 * 1024 * 1024


def _cparams(*sem):
    return pltpu.CompilerParams(dimension_semantics=sem, vmem_limit_bytes=VMEM_LIMIT)


def _split2(x):
    hi = x.astype(BF16)
    lo = (x - hi.astype(F32)).astype(BF16)
    return hi, lo


def _split3(x):
    hi = x.astype(BF16)
    r = x - hi.astype(F32)
    mid = r.astype(BF16)
    lo = (r - mid.astype(F32)).astype(BF16)
    return hi, mid, lo


def _dot(a, b):
    return jnp.dot(a, b, preferred_element_type=F32)


def _dot_nt(a, b):
    return lax.dot_general(a, b, (((1,), (1,)), ((), ())), preferred_element_type=F32)


def _dot_tn(a, b):
    return lax.dot_general(a, b, (((0,), (0,)), ((), ())), preferred_element_type=F32)


def _dot_x3(a, b):
    ah, al = _split2(a)
    bh, bl = _split2(b)
    return _dot(ah, bh) + (_dot(ah, bl) + _dot(al, bh))


def _dot_exact_lhs(a_bf16, b):
    bh, bm, bl = _split3(b)
    return _dot(a_bf16, bh) + (_dot(a_bf16, bm) + _dot(a_bf16, bl))


def _cmul(ar, ai, br, bi):
    return ar * br - ai * bi, ar * bi + ai * br


U32 = jnp.uint32


def _pack_halves(x):
    half = x.shape[1] // 2
    lo = lax.bitcast_convert_type(x[:, :half].astype(BF16).astype(F32), U32)
    hi = lax.bitcast_convert_type(x[:, half:].astype(BF16).astype(F32), U32)
    return (lo >> 16) | hi


def _unpack_halves(p):
    lo = lax.bitcast_convert_type(p << 16, F32)
    hi = lax.bitcast_convert_type(p & jnp.uint32(0xFFFF0000), F32)
    return lo, hi


def _inproj_kernel(x_ref, g_ref, w_ref, o_ref, hn_ref):
    @pl.when(pl.program_id(1) == 0)
    def _():
        x = x_ref[...]
        ms = jnp.mean(x * x, axis=-1, keepdims=True)
        hn_ref[...] = (x * lax.rsqrt(ms + RMS_EPS) * g_ref[...]).astype(BF16)

    o_ref[...] = _dot(hn_ref[...], w_ref[...])


def _inproj(x2, g, w_bf16, tm, tn):
    T, D = x2.shape
    N = w_bf16.shape[1]
    return pl.pallas_call(
        _inproj_kernel,
        grid=(T // tm, N // tn),
        in_specs=[
            pl.BlockSpec((tm, D), lambda i, j: (i, 0)),
            pl.BlockSpec((1, D), lambda i, j: (0, 0)),
            pl.BlockSpec((D, tn), lambda i, j: (0, j)),
        ],
        out_specs=pl.BlockSpec((tm, tn), lambda i, j: (i, j)),
        out_shape=jax.ShapeDtypeStruct((T, N), F32),
        scratch_shapes=[pltpu.VMEM((tm, D), BF16)],
        compiler_params=_cparams("arbitrary", "arbitrary"),
        name="inproj",
    )(x2, g, w_bf16)


def _binpow(ar, ai, expo, nbits):
    pr = jnp.ones(expo.shape, F32)
    pi = jnp.zeros(expo.shape, F32)
    sr, si = ar, ai
    for bit in range(nbits):
        m = ((expo >> bit) & 1) == 1
        nr, ni = _cmul(pr, pi, sr, si)
        pr = jnp.where(m, nr, pr)
        pi = jnp.where(m, ni, pi)
        if bit + 1 < nbits:
            sr, si = _cmul(sr, si, sr, si)
    return pr, pi


def _zoh(lr, li, dt):
    mag = jnp.exp(lr * dt)
    ang = li * dt
    ar, ai = mag * jnp.cos(ang), mag * jnp.sin(ang)
    den = lr * lr + li * li
    nr, ni = ar - 1.0, ai
    fr = (nr * lr + ni * li) / den
    fi = (ni * lr - nr * li) / den
    return ar, ai, fr, fi


def _s5_ops_kernel(lrc_ref, lic_ref, dt_ref, ct_re_ref, ct_im_ref, bt_re_ref, bt_im_ref, c_re_ref, c_im_ref,
                   m_ref, wbt_ref, wct_ref, a64_ref):
    C, HG, P = S5_CHUNK, S5_HG, S5_P
    W = C * HG
    dt = jnp.exp(dt_ref[0])
    ar_c, ai_c, fr_c, fi_c = _zoh(lrc_ref[0], lic_ref[0], dt)

    tau = lax.broadcasted_iota(I32, (1, W), 1) // HG
    nbits = max(1, (C - 1).bit_length())
    pr, pi = _binpow(ar_c, ai_c, jnp.broadcast_to(tau, (P, W)), nbits)
    qr, qi = _binpow(ar_c, ai_c, jnp.broadcast_to(C - 1 - tau, (P, W)), nbits)

    sel = (lax.broadcasted_iota(I32, (HG, W), 1) % HG == lax.broadcasted_iota(I32, (HG, W), 0)).astype(BF16)
    tile = lambda ref: _dot_exact_lhs_rhs(ref[0], sel)

    ca_re, ca_im = _cmul(tile(ct_re_ref), tile(ct_im_ref), pr, pi)
    c1_re, c1_im = _cmul(ca_re, ca_im, ar_c, ai_c)
    wct_ref[0] = jnp.concatenate([c1_re, -c1_im], axis=0).T.astype(BF16)

    bb_re, bb_im = _cmul(tile(bt_re_ref), tile(bt_im_ref), fr_c, fi_c)
    ab_re, ab_im = _cmul(bb_re, bb_im, qr, qi)
    wbt_ref[0, :P, :] = ab_re.astype(BF16)
    wbt_ref[0, P:, :] = ab_im.astype(BF16)

    cr, ci = ar_c, ai_c
    for _ in range(C.bit_length() - 1):
        cr, ci = _cmul(cr, ci, cr, ci)
    a64_ref[0] = jnp.concatenate([cr, ci], axis=0)

    strip = _dot_x3(c_re_ref[0], ab_re) - _dot_x3(c_im_ref[0], ab_im)
    ext = jnp.concatenate([strip, jnp.zeros((HG, W), F32)], axis=1)
    for t in range(C):
        off = (C - 1 - t) * HG
        m_ref[0, t * HG:(t + 1) * HG, :] = ext[:, off:off + W].astype(BF16)


def _s5_ops(lam_re, lam_im, log_step, ct_re, ct_im, bt_re, bt_im, c_re, c_im):
    G, P = lam_re.shape
    C, HG = S5_CHUNK, S5_HG
    W = C * HG
    col = lambda a: a.reshape(G, P, 1)
    g3 = lambda s1, s2: pl.BlockSpec((1, s1, s2), lambda g: (g, 0, 0))
    return pl.pallas_call(
        _s5_ops_kernel,
        grid=(G,),
        in_specs=[g3(P, 1), g3(P, 1), g3(1, 1),
                  g3(P, HG), g3(P, HG), g3(P, HG), g3(P, HG), g3(HG, P), g3(HG, P)],
        out_specs=[g3(W, W), g3(2 * P, W), g3(W, 2 * P), g3(2 * P, 1)],
        out_shape=[jax.ShapeDtypeStruct((G, W, W), BF16),
                   jax.ShapeDtypeStruct((G, 2 * P, W), BF16),
                   jax.ShapeDtypeStruct((G, W, 2 * P), BF16),
                   jax.ShapeDtypeStruct((G, 2 * P, 1), F32)],
        compiler_params=_cparams("arbitrary"),
        name="s5_ops",
    )(col(lam_re), col(lam_im), log_step.reshape(G, 1, 1), ct_re, ct_im, bt_re, bt_im, c_re, c_im)


S5_FB = 8


def _s5_in_kernel(x_ref, z_ref):
    G, _, rb = z_ref.shape
    for f in range(S5_FB):
        xt = x_ref[:, f, :].T
        z_ref[:, f * S5_HG:(f + 1) * S5_HG, :] = xt.reshape(G, S5_HG, rb).astype(BF16)


def _s5_in(proj3, G, rb):
    R, C, _ = proj3.shape
    W = G * S5_HG
    return pl.pallas_call(
        _s5_in_kernel,
        grid=(R // rb, C // S5_FB),
        in_specs=[pl.BlockSpec((rb, S5_FB, W), lambda i, j: (i, j, 0))],
        out_specs=pl.BlockSpec((G, S5_FB * S5_HG, rb), lambda i, j: (0, j, i)),
        out_shape=jax.ShapeDtypeStruct((G, C * S5_HG, R), BF16),
        compiler_params=_cparams("arbitrary", "arbitrary"),
        name="s5_in",
    )(proj3)


def _s5_out_kernel(yt_ref, o_ref):
    G, _, rb = yt_ref.shape
    for f in range(S5_FB):
        slab = yt_ref[:, f * S5_HG:(f + 1) * S5_HG, :].reshape(G * S5_HG, rb)
        o_ref[:, f, :] = slab.T


def _s5_out(yt, rb):
    G, CW, R = yt.shape
    C = CW // S5_HG
    W = G * S5_HG
    return pl.pallas_call(
        _s5_out_kernel,
        grid=(R // rb, C // S5_FB),
        in_specs=[pl.BlockSpec((G, S5_FB * S5_HG, rb), lambda i, j: (0, j, i))],
        out_specs=pl.BlockSpec((rb, S5_FB, W), lambda i, j: (i, j, 0)),
        out_shape=jax.ShapeDtypeStruct((R, C, W), F32),
        compiler_params=_cparams("arbitrary", "arbitrary"),
        name="s5_out",
    )(yt)


def _s5_chunk_kernel(z_ref, m_ref, wbt_ref, wct_ref, a64_ref, y_ref, *, n_chunks):
    P = S5_P
    z = z_ref[0]
    R = z.shape[1]
    y = _dot(m_ref[0], z)
    x = _dot(wbt_ref[0], z)
    xr, xi = x[:P], x[P:]
    a = a64_ref[0]
    ar, ai = a[:P], a[P:]
    cidx = lax.broadcasted_iota(I32, (1, R), 1) % n_chunks
    shift = 1
    while shift < n_chunks:
        keep = cidx >= shift
        sr = jnp.where(keep, pltpu.roll(xr, shift, 1), 0.0)
        si = jnp.where(keep, pltpu.roll(xi, shift, 1), 0.0)
        xr, xi = xr + (ar * sr - ai * si), xi + (ar * si + ai * sr)
        ar, ai = ar * ar - ai * ai, 2.0 * ar * ai
        shift *= 2
    keep = cidx >= 1
    s_in = jnp.concatenate([jnp.where(keep, pltpu.roll(xr, 1, 1), 0.0),
                            jnp.where(keep, pltpu.roll(xi, 1, 1), 0.0)], axis=0)
    sh, sl = _split2(s_in)
    wct = wct_ref[0]
    y_ref[0] = y + (_dot(wct, sh) + _dot(wct, sl))


def _s5_chunk(z, m, wbt, wct, a64, n_chunks):
    G, W, R = z.shape
    P2 = wbt.shape[1]
    g3 = lambda s1, s2: pl.BlockSpec((1, s1, s2), lambda g: (g, 0, 0))
    return pl.pallas_call(
        functools.partial(_s5_chunk_kernel, n_chunks=n_chunks),
        grid=(G,),
        in_specs=[g3(W, R), g3(W, W), g3(P2, W), g3(W, P2), g3(P2, 1)],
        out_specs=g3(W, R),
        out_shape=jax.ShapeDtypeStruct((G, W, R), F32),
        compiler_params=_cparams("arbitrary"),
        name="s5_chunk",
    )(z, m, wbt, wct, a64)


def _gelu_tanh(x):
    return 0.5 * x * (1.0 + jnp.tanh(0.7978845608028654 * (x + 0.044715 * (x * x * x))))


def _s5_glu_kernel(y_ref, u_ref, d_ref, w_ref, b_ref, o_ref):
    y = y_ref[...] + d_ref[...] * u_ref[...]
    y = _gelu_tanh(y)
    z = _dot(y.astype(BF16), w_ref[...]) + b_ref[...]
    o_ref[...] = (y * jax.nn.sigmoid(z)).astype(BF16)


def _s5_glu(y_ssm, proj, d, w_bf16, b, tm):
    T, W = y_ssm.shape
    return pl.pallas_call(
        _s5_glu_kernel,
        grid=(T // tm,),
        in_specs=[
            pl.BlockSpec((tm, W), lambda i: (i, 0)),
            pl.BlockSpec((tm, W), lambda i: (i, 0)),
            pl.BlockSpec((1, W), lambda i: (0, 0)),
            pl.BlockSpec((W, W), lambda i: (0, 0)),
            pl.BlockSpec((1, W), lambda i: (0, 0)),
        ],
        out_specs=pl.BlockSpec((tm, W), lambda i: (i, 0)),
        out_shape=jax.ShapeDtypeStruct((T, W), BF16),
        compiler_params=_cparams("arbitrary"),
        name="s5_glu",
    )(y_ssm, proj, d, w_bf16, b)


def _head_sum(x, e_ref, et_ref):
    xh, xl = _split2(x)
    e = e_ref[...]
    s = _dot(xh, e) + _dot(xl, e)
    sh, sl = _split2(s)
    et = et_ref[...]
    return _dot(sh, et) + _dot(sl, et)


def _shift(z, prev_row):
    rolled = pltpu.roll(z, 1, 0)
    first = lax.broadcasted_iota(I32, (z.shape[0], 1), 0) == 0
    return jnp.where(first, prev_row, rolled)


def _rwkv_prep_kernel(zr_ref, zk_ref, zv_ref, zl_ref, mu_ref, mul_ref, w0_ref, a0_ref, kk_ref, ka_ref,
                      w2_ref, a2_ref, g2_ref, e_ref, et_ref,
                      r_ref, k_ref, v_ref, nk_ref, b_ref, lw_ref, g_ref,
                      car_ref, carl_ref):
    W = r_ref.shape[1]

    @pl.when(pl.program_id(1) == 0)
    def _():
        car_ref[...] = jnp.zeros_like(car_ref)
        carl_ref[...] = jnp.zeros_like(carl_ref)

    def lerp(z, prev_row, mu):
        return z + (_shift(z, prev_row) - z) * mu

    tm = zr_ref.shape[0]
    zr, zk, zv, zl = zr_ref[...], zk_ref[...], zv_ref[...], zl_ref[...]
    r = lerp(zr, car_ref[0:1, :], mu_ref[0:1, :])
    k = lerp(zk, car_ref[1:2, :], mu_ref[1:2, :])
    v = lerp(zv, car_ref[2:3, :], mu_ref[2:3, :])
    xl = lerp(zl, carl_ref[0:1, :], mul_ref[...])
    car_ref[0:1, :] = zr[tm - 1:tm, :]
    car_ref[1:2, :] = zk[tm - 1:tm, :]
    car_ref[2:3, :] = zv[tm - 1:tm, :]
    carl_ref[0:1, :] = zl[tm - 1:tm, :]

    xw, xa, xg = xl[:, 0:128], xl[:, 128:256], xl[:, 256:512]
    dw = _dot(jnp.tanh(xw).astype(BF16), w2_ref[...])
    da = _dot(xa.astype(BF16), a2_ref[...])
    g = _dot(jax.nn.sigmoid(xg).astype(BF16), g2_ref[...])

    zw = -(w0_ref[...] + dw)
    softplus = jnp.maximum(zw, 0.0) + jnp.log(1.0 + jnp.exp(-jnp.abs(zw)))
    w_log = -softplus - 0.5
    a = jax.nn.sigmoid(a0_ref[...] + da)

    kk = k * kk_ref[...]
    n2 = _head_sum(kk * kk, e_ref, et_ref)
    kk = kk / jnp.maximum(jnp.sqrt(n2), 1e-12)

    r_ref[...] = r.astype(BF16)
    k_ref[...] = (k * (1.0 + (a - 1.0) * ka_ref[...])).astype(BF16)
    v_ref[...] = v.astype(BF16)
    nk_ref[...] = kk.astype(BF16)
    b_ref[...] = (kk * a).astype(BF16)
    lw_ref[...] = -jnp.exp(w_log)
    g_ref[...] = g.astype(BF16)


def _rwkv_prep(proj, B, L, mu3, mul, w0, a0, k_k, k_a, w2p, a2p, g2p, e_mat, et_mat, tm):
    T = B * L
    W = w0.shape[1]
    nt = L // tm
    row = lambda c: pl.BlockSpec((tm, W), lambda b, i, c=c: (b * nt + i, c))
    full = lambda a: pl.BlockSpec(a.shape, lambda b, i: (0, 0))
    out = pl.BlockSpec((tm, W), lambda b, i: (b * nt + i, 0))
    lw = 512
    return pl.pallas_call(
        _rwkv_prep_kernel,
        grid=(B, nt),
        in_specs=[row(1), row(2), row(3),
                  pl.BlockSpec((tm, lw), lambda b, i: (b * nt + i, 4 * W // lw)),
                  full(mu3), full(mul), full(w0), full(a0), full(k_k), full(k_a),
                  full(w2p), full(a2p), full(g2p), full(e_mat), full(et_mat)],
        out_specs=[out] * 7,
        out_shape=[jax.ShapeDtypeStruct((T, W), dt) for dt in (BF16, BF16, BF16, BF16, BF16, F32, BF16)],
        scratch_shapes=[pltpu.VMEM((8, W), F32), pltpu.VMEM((8, lw), F32)],
        compiler_params=_cparams("arbitrary", "arbitrary"),
        name="rwkv_prep",
    )(proj, proj, proj, proj, mu3, mul, w0, a0, k_k, k_a, w2p, a2p, g2p, e_mat, et_mat)


def _rwkv_chunk_kernel(r_ref, k_ref, v_ref, nk_ref, b_ref, lw_ref, y_ref, z_ref):
    C, N = RW_CHUNK, RW_N
    H = r_ref.shape[1] // N

    @pl.when(pl.program_id(1) == 0)
    def _():
        z_ref[...] = jnp.zeros_like(z_ref)

    ri = lax.broadcasted_iota(I32, (C, C), 0)
    ci = lax.broadcasted_iota(I32, (C, C), 1)
    tril = ri >= ci
    stril = ri > ci
    eye = (ri == ci).astype(F32)

    lw = lw_ref[...]
    cs = _dot_exact_lhs(tril.astype(BF16), lw)
    tot = cs[C - 1:C, :]
    p_inc = jnp.exp(cs)
    p_inv = jnp.exp(-cs)
    p_prev = jnp.exp(cs - lw)
    p_rest = jnp.exp(tot - cs)
    p_tot = jnp.exp(tot)

    k_f, b_f = k_ref[...].astype(F32), b_ref[...].astype(F32)
    r_t = r_ref[...].astype(F32) * p_inc
    k_t = k_f * p_inv
    a_t = -nk_ref[...].astype(F32) * p_prev
    b_t = b_f * p_inv
    k_h = k_f * p_rest
    b_h = b_f * p_rest
    v_all = v_ref[...]

    hs = range(H)
    sl = [slice(h * N, (h + 1) * N) for h in hs]
    each = lambda f: [f(h) for h in hs]
    v = each(lambda h: v_all[:, sl[h]].astype(BF16))
    ar = each(lambda h: jnp.concatenate([a_t[:, sl[h]], r_t[:, sl[h]]], axis=0).astype(BF16))
    m_b = each(lambda h: _dot_nt(ar[h], b_t[:, sl[h]].astype(BF16)))
    m_k = each(lambda h: _dot_nt(ar[h], k_t[:, sl[h]].astype(BF16)))
    l_ab = each(lambda h: jnp.where(stril, m_b[h][:C], 0.0).astype(BF16))
    m_rb = each(lambda h: jnp.where(tril, m_b[h][C:], 0.0).astype(BF16))
    l_ak = each(lambda h: jnp.where(stril, m_k[h][:C], 0.0).astype(BF16))
    m_rk = each(lambda h: jnp.where(tril, m_k[h][C:], 0.0).astype(BF16))

    x = each(lambda h: jnp.concatenate([_dot(l_ak[h], v[h]), a_t[:, sl[h]]], axis=1))
    y0 = each(lambda h: _dot(m_rk[h], v[h]))
    hk = each(lambda h: _dot_tn(k_h[:, sl[h]].astype(BF16), v[h]))
    lp = l_ab
    step = 1
    while step < C:
        x = each(lambda h: x[h] + _dot(lp[h], x[h].astype(BF16)))
        step *= 2
        if step < C:
            lp = each(lambda h: _dot(lp[h], lp[h]).astype(BF16))
    xb = each(lambda h: x[h].astype(BF16))
    yq = each(lambda h: _dot(m_rb[h], xb[h]))
    gh = each(lambda h: _dot_tn(b_h[:, sl[h]].astype(BF16), xb[h]))

    z = each(lambda h: _split2(z_ref[h]))
    qb = each(lambda h: (yq[h][:, N:] + r_t[:, sl[h]]).astype(BF16))
    gb = each(lambda h: (gh[h][:, N:] + eye * p_tot[:, sl[h]]).astype(BF16))
    yz = each(lambda h: _dot(qb[h], z[h][0]) + _dot(qb[h], z[h][1]))
    gz = each(lambda h: _dot(gb[h], z[h][0]) + _dot(gb[h], z[h][1]))
    for h in hs:
        y_ref[:, sl[h]] = (yq[h][:, :N] + y0[h]) + yz[h]
        z_ref[h] = (gh[h][:, :N] + hk[h]) + gz[h]


def _rwkv_chunk(r, k, v, nk, b, lw, B, L):
    T, W = r.shape
    C = RW_CHUNK
    nc = L // C
    H = W // RW_N
    spec = pl.BlockSpec((C, W), lambda bi, c: (bi * nc + c, 0))
    return pl.pallas_call(
        _rwkv_chunk_kernel,
        grid=(B, nc),
        in_specs=[spec] * 6,
        out_specs=spec,
        out_shape=jax.ShapeDtypeStruct((T, W), F32),
        scratch_shapes=[pltpu.VMEM((H, RW_N, RW_N), F32)],
        compiler_params=_cparams("arbitrary", "arbitrary"),
        name="rwkv_chunk",
    )(r, k, v, nk, b, lw)


def _rwkv_post_kernel(y_ref, r_ref, k_ref, v_ref, g_ref, rk_ref, lnw_ref, lnb_ref, e_ref, et_ref, o_ref):
    inv_n = 1.0 / RW_N
    y = y_ref[...]
    mean = _head_sum(y, e_ref, et_ref) * inv_n
    yc = y - mean
    var = _head_sum(yc * yc, e_ref, et_ref) * inv_n
    yn = yc * lax.rsqrt(var + RW_GN_EPS) * lnw_ref[...] + lnb_ref[...]
    r, k, v = r_ref[...].astype(F32), k_ref[...].astype(F32), v_ref[...].astype(F32)
    bonus = _head_sum(r * k * rk_ref[...], e_ref, et_ref) * v
    o_ref[...] = ((yn + bonus) * g_ref[...].astype(F32)).astype(BF16)


def _rwkv_post(y, r, k, v, g, r_k, ln_w, ln_b, e_mat, et_mat, tm):
    T, W = y.shape
    row = pl.BlockSpec((tm, W), lambda i: (i, 0))
    full = lambda a: pl.BlockSpec(a.shape, lambda i: (0, 0))
    return pl.pallas_call(
        _rwkv_post_kernel,
        grid=(T // tm,),
        in_specs=[row] * 5 + [full(r_k), full(ln_w), full(ln_b), full(e_mat), full(et_mat)],
        out_specs=row,
        out_shape=jax.ShapeDtypeStruct((T, W), BF16),
        compiler_params=_cparams("arbitrary"),
        name="rwkv_post",
    )(y, r, k, v, g, r_k, ln_w, ln_b, e_mat, et_mat)


def _first_index_of_max(vals, lane, valid):
    neg = jnp.float32(-jnp.inf)
    masked = jnp.where(valid, vals, neg)
    m = jnp.max(masked, axis=-1, keepdims=True)
    idx = jnp.min(jnp.where(valid & (masked == m), lane, LANES), axis=-1, keepdims=True)
    return m, idx


def _outproj_kernel(s5_ref, rw_ref, x_ref, wt_ref, wb_ref, g_ref, wrh_ref, wrl_ref, br_ref,
                    h_ref, hn_ref, rec_ref):
    h = x_ref[...] + (_dot(s5_ref[...], wt_ref[...]) + _dot(rw_ref[...], wb_ref[...]))
    h_ref[...] = h
    ms = jnp.mean(h * h, axis=-1, keepdims=True)
    hn = h * lax.rsqrt(ms + RMS_EPS) * g_ref[...]
    hn_ref[...] = _pack_halves(hn).reshape(hn_ref.shape)

    hh, hl = _split2(hn)
    wh, wl = wrh_ref[...], wrl_ref[...]
    logits = _dot(hh, wh) + (_dot(hh, wl) + _dot(hl, wh)) + br_ref[...]
    lane = lax.broadcasted_iota(I32, logits.shape, 1)
    is_grp = (lane >= N_EXPERTS) & (lane < N_EXPERTS + N_GROUPS)
    gmax, gidx = _first_index_of_max(logits, lane, is_grp)
    gsum = jnp.sum(jnp.where(is_grp, jnp.exp(logits - gmax), 0.0), axis=-1, keepdims=True)
    p_grp = 1.0 / gsum
    grp = gidx - N_EXPERTS
    in_grp = (lane >= grp * EPG) & (lane < (grp + 1) * EPG)
    m1, i1 = _first_index_of_max(logits, lane, in_grp)
    m2, i2 = _first_index_of_max(logits, lane, in_grp & (lane != i1))
    e = jnp.exp(m2 - m1)
    g1 = p_grp / (1.0 + e)
    g2 = p_grp * e / (1.0 + e)
    rec = jnp.where(lane == 0, i1.astype(F32),
          jnp.where(lane == 1, i2.astype(F32),
          jnp.where(lane == 2, g1, jnp.where(lane == 3, g2, 0.0))))
    rec_ref[...] = rec


def _outproj(s5o, rwo, x2, w_top, w_bot, g, wr_hi, wr_lo, b_route, tm):
    T, D = x2.shape
    W = s5o.shape[1]
    full = lambda a: pl.BlockSpec(a.shape, lambda i: (0, 0))
    return pl.pallas_call(
        _outproj_kernel,
        grid=(T // tm,),
        in_specs=[pl.BlockSpec((tm, W), lambda i: (i, 0)), pl.BlockSpec((tm, W), lambda i: (i, 0)),
                  pl.BlockSpec((tm, D), lambda i: (i, 0)),
                  full(w_top), full(w_bot), full(g), full(wr_hi), full(wr_lo), full(b_route)],
        out_specs=[pl.BlockSpec((tm, D), lambda i: (i, 0)), pl.BlockSpec((tm, 1, D // 2), lambda i: (i, 0, 0)),
                   pl.BlockSpec((tm, LANES), lambda i: (i, 0))],
        out_shape=[jax.ShapeDtypeStruct((T, D), F32), jax.ShapeDtypeStruct((T, 1, D // 2), U32),
                   jax.ShapeDtypeStruct((T, LANES), F32)],
        compiler_params=_cparams("arbitrary"),
        name="outproj_route",
    )(s5o, rwo, x2, w_top, w_bot, g, wr_hi, wr_lo, b_route)


def _onehots(rec, lane):
    oh0 = (lane == rec[:, 0:1].astype(I32)).astype(F32)
    oh1 = (lane == rec[:, 1:2].astype(I32)).astype(F32)
    return oh0, oh1


def _rank_kernel(rec_ref, rank_ref, cnt_ref, base_ref):
    tb = rec_ref.shape[0]

    @pl.when(pl.program_id(0) == 0)
    def _():
        base_ref[...] = jnp.zeros_like(base_ref)

    lane = lax.broadcasted_iota(I32, (tb, LANES), 1)
    oh0, oh1 = _onehots(rec_ref[...], lane)
    both = oh0 + oh1
    ri = lax.broadcasted_iota(I32, (tb, tb), 0)
    ci = lax.broadcasted_iota(I32, (tb, tb), 1)
    before = _dot((ri > ci).astype(BF16), both.astype(BF16)) + base_ref[0:1, :]
    rank0 = jnp.sum(oh0 * before, axis=-1, keepdims=True)
    rank1 = jnp.sum(oh1 * before, axis=-1, keepdims=True)
    rank_ref[...] = jnp.where(lane == 0, rank0, jnp.where(lane == 1, rank1, 0.0))
    total = base_ref[0:1, :] + jnp.sum(both, axis=0, keepdims=True)
    base_ref[0:1, :] = total
    cnt_ref[...] = jnp.broadcast_to(total, cnt_ref.shape)


def _rank(rec, tb):
    T = rec.shape[0]
    return pl.pallas_call(
        _rank_kernel,
        grid=(T // tb,),
        in_specs=[pl.BlockSpec((tb, LANES), lambda i: (i, 0))],
        out_specs=[pl.BlockSpec((tb, LANES), lambda i: (i, 0)), pl.BlockSpec((8, LANES), lambda i: (0, 0))],
        out_shape=[jax.ShapeDtypeStruct((T, LANES), F32), jax.ShapeDtypeStruct((8, LANES), F32)],
        scratch_shapes=[pltpu.VMEM((8, LANES), F32)],
        compiler_params=_cparams("arbitrary"),
        name="moe_rank",
    )(rec)


def _padded_starts(cnt):
    padded = jnp.ceil(cnt * (1.0 / MOE_BLOCK)) * MOE_BLOCK
    ri = lax.broadcasted_iota(I32, (LANES, LANES), 0)
    ci = lax.broadcasted_iota(I32, (LANES, LANES), 1)
    p8 = jnp.broadcast_to(padded, (8, LANES))
    pend = _dot_exact_lhs_rhs(p8, (ri <= ci).astype(BF16))[0:1, :]
    return pend - padded, pend


def _dot_exact_lhs_rhs(a, b_bf16):
    ah, am, al = _split3(a)
    return _dot(ah, b_bf16) + (_dot(am, b_bf16) + _dot(al, b_bf16))


def _dest_kernel(rec_ref, rank_ref, cnt_ref, dest_ref, blk_ref):
    tb = rec_ref.shape[0]
    cnt = cnt_ref[0:1, :]
    pstart, pend = _padded_starts(cnt)
    lane = lax.broadcasted_iota(I32, (tb, LANES), 1)
    oh0, oh1 = _onehots(rec_ref[...], lane)
    rank = rank_ref[...]
    d0 = jnp.sum(oh0 * pstart, axis=-1, keepdims=True) + rank[:, 0:1]
    d1 = jnp.sum(oh1 * pstart, axis=-1, keepdims=True) + rank[:, 1:2]
    dest_ref[...] = jnp.where(lane == 0, d0, jnp.where(lane == 1, d1, 0.0)).astype(I32)

    @pl.when(pl.program_id(0) == 0)
    def _():
        nb = blk_ref.shape[0]
        blane = lax.broadcasted_iota(I32, (nb, LANES), 1)
        bstart = (lax.broadcasted_iota(I32, (nb, 1), 0) * MOE_BLOCK).astype(F32)
        is_e = blane < N_EXPERTS
        bexp = jnp.sum(jnp.where(is_e & (pend <= bstart), 1.0, 0.0), axis=-1, keepdims=True)
        bexp = jnp.minimum(bexp, N_EXPERTS - 1.0)
        nact = jnp.max(jnp.where(is_e, pend, 0.0), axis=-1, keepdims=True) * (1.0 / MOE_BLOCK)
        lastblk = jnp.where(is_e & (pend > pstart), pend - MOE_BLOCK, -1.0)
        brow = lax.broadcasted_iota(I32, (nb, 1), 0)
        out = jnp.where(blane == 0, bexp, jnp.where(blane == 1, nact, 0.0))
        out = jnp.where(brow == nb - 1, lastblk, out)
        blk_ref[...] = out.astype(I32)


def _dest(rec, rank, cnt, tb, nb_rows):
    T = rec.shape[0]
    return pl.pallas_call(
        _dest_kernel,
        grid=(T // tb,),
        in_specs=[pl.BlockSpec((tb, LANES), lambda i: (i, 0)), pl.BlockSpec((tb, LANES), lambda i: (i, 0)),
                  pl.BlockSpec((8, LANES), lambda i: (0, 0))],
        out_specs=[pl.BlockSpec((tb, LANES), lambda i: (i, 0)), pl.BlockSpec((nb_rows, LANES), lambda i: (0, 0))],
        out_shape=[jax.ShapeDtypeStruct((T, LANES), I32), jax.ShapeDtypeStruct((nb_rows, LANES), I32)],
        compiler_params=_cparams("arbitrary"),
        name="moe_dest",
    )(rec, rank, cnt)


def _scatter_kernel(pad_ref, na_ref, dest_ref, hn_ref, xs_ref, zbuf_ref, idx_ref, sa_ref, sb_ref,
                    sem_a, sem_b, isem, zsem, *, tb, n_blocks, n_tail, n_steps):
    i = pl.program_id(0)

    @pl.when(i == 0)
    def _():
        zbuf_ref[...] = jnp.zeros_like(zbuf_ref)

        def zero_rows(start):
            return pltpu.make_async_copy(zbuf_ref, xs_ref.at[pl.ds(start, MOE_BLOCK)], zsem)

        def pad_start(e, c):
            @pl.when(pad_ref[e] >= 0)
            def _():
                zero_rows(pad_ref[e]).start()
            return c

        def pad_wait(e, c):
            @pl.when(pad_ref[e] >= 0)
            def _():
                zero_rows(pad_ref[e]).wait()
            return c

        def tail_start(k, c):
            @pl.when(na_ref[0] + k < n_blocks)
            def _():
                zero_rows((na_ref[0] + k) * MOE_BLOCK).start()
            return c

        def tail_wait(k, c):
            @pl.when(na_ref[0] + k < n_blocks)
            def _():
                zero_rows((na_ref[0] + k) * MOE_BLOCK).wait()
            return c

        lax.fori_loop(0, N_EXPERTS, pad_start, 0)
        lax.fori_loop(0, n_tail, tail_start, 0)
        lax.fori_loop(0, N_EXPERTS, pad_wait, 0)
        lax.fori_loop(0, n_tail, tail_wait, 0)

    def idx_copy(step, slot):
        return pltpu.make_async_copy(dest_ref.at[step], idx_ref.at[slot], isem)

    @pl.when(i == 0)
    def _():
        idx_copy(0, 0).start()

    slot = i % 2
    idx_copy(i, slot).wait()

    @pl.when(i + 1 < n_steps)
    def _():
        idx_copy(i + 1, 1 - slot).start()

    hb = tb // 2
    for half, (buf, sem) in enumerate(((sa_ref, sem_a), (sb_ref, sem_b))):
        def wait_rows(buf=buf, sem=sem):
            for _ in range(2):
                pltpu.make_async_copy(buf, xs_ref.at[pl.ds(0, hb)], sem).wait()

        @pl.when(i > 0)
        def _():
            wait_rows()

        buf[...] = hn_ref[half * hb:(half + 1) * hb]

        def issue(t, c, buf=buf, sem=sem, half=half):
            for s in range(2):
                d = idx_ref[slot, 2 * (half * hb + t) + s]
                pltpu.make_async_copy(buf.at[pl.ds(t, 1)], xs_ref.at[pl.ds(d, 1)], sem).start()
            return c

        lax.fori_loop(0, hb, issue, 0, unroll=8)

    @pl.when(i == n_steps - 1)
    def _():
        for buf, sem in ((sa_ref, sem_a), (sb_ref, sem_b)):
            for _ in range(2):
                pltpu.make_async_copy(buf, xs_ref.at[pl.ds(0, hb)], sem).wait()


def _scatter(lastblk, nact, dest2, hn, tb, n_blocks):
    T, _, D = hn.shape
    n_tail = n_blocks - (-(-2 * T // MOE_BLOCK))
    hb = tb // 2
    return pl.pallas_call(
        functools.partial(_scatter_kernel, tb=tb, n_blocks=n_blocks, n_tail=n_tail, n_steps=T // tb),
        grid_spec=pltpu.PrefetchScalarGridSpec(
            num_scalar_prefetch=2, grid=(T // tb,),
            in_specs=[pl.BlockSpec(memory_space=pl.ANY),
                      pl.BlockSpec((tb, 1, D), lambda i, pad, na: (i, 0, 0))],
            out_specs=pl.BlockSpec(memory_space=pl.ANY),
            scratch_shapes=[pltpu.VMEM((MOE_BLOCK, 1, D), hn.dtype), pltpu.SMEM((2, 2 * tb), I32),
                            pltpu.VMEM((hb, 1, D), hn.dtype), pltpu.VMEM((hb, 1, D), hn.dtype),
                            pltpu.SemaphoreType.DMA, pltpu.SemaphoreType.DMA,
                            pltpu.SemaphoreType.DMA, pltpu.SemaphoreType.DMA]),
        out_shape=jax.ShapeDtypeStruct((n_blocks * MOE_BLOCK, 1, D), hn.dtype),
        compiler_params=_cparams("arbitrary"),
        name="moe_scatter",
    )(lastblk, nact, dest2, hn)


def _expert_kernel(be_ref, na_ref, x_ref, wg_ref, wu_ref, wd_ref, y_ref, wgb_ref, wub_ref, wdb_ref, x2_ref):
    j = pl.program_id(0)

    @pl.when(j < na_ref[0])
    def _():
        prev = be_ref[jnp.maximum(j - 1, 0)]

        @pl.when((j == 0) | (be_ref[j] != prev))
        def _():
            wgb_ref[...] = wg_ref[0].astype(BF16)
            wub_ref[...] = wu_ref[0].astype(BF16)
            wdb_ref[...] = wd_ref[0].astype(BF16)

        x2_ref[...] = x_ref[...].reshape(x2_ref.shape)
        lo, hi = _unpack_halves(x2_ref[...])
        lo, hi = lo.astype(BF16), hi.astype(BF16)
        half = lo.shape[1]
        hg = _dot(lo, wgb_ref[:half, :]) + _dot(hi, wgb_ref[half:, :])
        hu = _dot(lo, wub_ref[:half, :]) + _dot(hi, wub_ref[half:, :])
        act = (hg * jax.nn.sigmoid(hg)) * hu
        y = _dot(act.astype(BF16), wdb_ref[...])
        y_ref[...] = _pack_halves(y).reshape(y_ref.shape)

    @pl.when(j >= na_ref[0])
    def _():
        y_ref[...] = jnp.zeros_like(y_ref)


def _experts(blk_exp, nact, xs, w_gate, w_up, w_down, n_blocks):
    DP = xs.shape[2]
    D = 2 * DP
    rows = n_blocks * MOE_BLOCK
    DE = w_gate.shape[2]
    blk = lambda j, be, na: (jnp.minimum(j, na[0] - 1), 0, 0)
    wmap = lambda j, be, na: (be[jnp.minimum(j, na[0] - 1)], 0, 0)
    return pl.pallas_call(
        _expert_kernel,
        grid_spec=pltpu.PrefetchScalarGridSpec(
            num_scalar_prefetch=2, grid=(n_blocks,),
            in_specs=[pl.BlockSpec((MOE_BLOCK, 1, DP), blk),
                      pl.BlockSpec((1, D, DE), wmap), pl.BlockSpec((1, D, DE), wmap),
                      pl.BlockSpec((1, DE, D), wmap)],
            out_specs=pl.BlockSpec((MOE_BLOCK, 1, DP), lambda j, be, na: (j, 0, 0)),
            scratch_shapes=[pltpu.VMEM((D, DE), BF16), pltpu.VMEM((D, DE), BF16), pltpu.VMEM((DE, D), BF16),
                            pltpu.VMEM((MOE_BLOCK, DP), U32)]),
        out_shape=jax.ShapeDtypeStruct((rows, 1, DP), U32),
        compiler_params=_cparams("arbitrary"),
        name="moe_experts",
    )(blk_exp, nact, xs, w_gate, w_up, w_down)


def _combine_kernel(dest_ref, ys_ref, h_ref, rec_ref, g_ref, o_ref, idx_ref, ya0_ref, ya1_ref, yb0_ref, yb1_ref,
                    y2_ref, sem_a, sem_b, isem, *, normalize, n_steps):
    i = pl.program_id(0)
    tb = h_ref.shape[0]
    hb = tb // 2
    halves = ((ya0_ref, ya1_ref, sem_a), (yb0_ref, yb1_ref, sem_b))

    def idx_copy(step, slot):
        return pltpu.make_async_copy(dest_ref.at[step], idx_ref.at[slot], isem)

    def issue(half, slot):
        bufs, sem = halves[half][:2], halves[half][2]

        def body(t, c):
            for s in range(2):
                d = idx_ref[slot, 2 * (half * hb + t) + s]
                pltpu.make_async_copy(ys_ref.at[pl.ds(d, 1)], bufs[s].at[pl.ds(t, 1)], sem).start()
            return c

        lax.fori_loop(0, hb, body, 0, unroll=8)

    @pl.when(i == 0)
    def _():
        first = idx_copy(0, 0)
        first.start()
        first.wait()
        issue(0, 0)
        issue(1, 0)
        if n_steps > 1:
            idx_copy(1, 1).start()

    nslot = (i + 1) % 2

    @pl.when(i + 1 < n_steps)
    def _():
        idx_copy(i + 1, nslot).wait()

    rec = rec_ref[...]
    for half, (y0_ref, y1_ref, sem) in enumerate(halves):
        rows = slice(half * hb, (half + 1) * hb)
        pltpu.make_async_copy(ys_ref.at[pl.ds(0, hb)], y0_ref, sem).wait()
        pltpu.make_async_copy(ys_ref.at[pl.ds(0, hb)], y1_ref, sem).wait()
        y2_ref[...] = y0_ref[...].reshape(y2_ref.shape)
        h = h_ref[rows, :] + rec[rows, 2:3] * jnp.concatenate(_unpack_halves(y2_ref[...]), axis=1)
        y2_ref[...] = y1_ref[...].reshape(y2_ref.shape)
        h = h + rec[rows, 3:4] * jnp.concatenate(_unpack_halves(y2_ref[...]), axis=1)
        if normalize:
            ms = jnp.mean(h * h, axis=-1, keepdims=True)
            h = h * lax.rsqrt(ms + RMS_EPS) * g_ref[...]
        o_ref[rows, :] = h

        @pl.when(i + 1 < n_steps)
        def _():
            issue(half, nslot)

    @pl.when(i + 2 < n_steps)
    def _():
        idx_copy(i + 2, i % 2).start()


def _combine(dest2, ys, h, rec, g, tb, normalize):
    T, D = h.shape
    hb = tb // 2
    return pl.pallas_call(
        functools.partial(_combine_kernel, normalize=normalize, n_steps=T // tb),
        grid=(T // tb,),
        in_specs=[pl.BlockSpec(memory_space=pl.ANY), pl.BlockSpec(memory_space=pl.ANY),
                  pl.BlockSpec((tb, D), lambda i: (i, 0)), pl.BlockSpec((tb, LANES), lambda i: (i, 0)),
                  pl.BlockSpec((1, D), lambda i: (0, 0))],
        out_specs=pl.BlockSpec((tb, D), lambda i: (i, 0)),
        out_shape=jax.ShapeDtypeStruct((T, D), F32),
        scratch_shapes=[pltpu.SMEM((2, 2 * tb), I32)] + [pltpu.VMEM((hb, 1, D // 2), U32)] * 4 +
                       [pltpu.VMEM((hb, D // 2), U32)] + [pltpu.SemaphoreType.DMA] * 3,
        compiler_params=_cparams("arbitrary"),
        name="moe_combine",
    )(dest2, ys, h, rec, g)


def _pad_cols(a, n):
    return jnp.pad(a, ((0, 0), (0, n - a.shape[1])))


def _pad_rows(a, n, at=0):
    return jnp.pad(a, ((at, n - a.shape[0] - at), (0, 0)))


def _layer(x2, B, L, p):
    T, D = x2.shape
    W = p["s5_d"].shape[0]
    G, P = p["s5_lambda_re"].shape
    HG = W // G
    dl, al, gl = p["rwkv_w2"].shape[0], p["rwkv_a2"].shape[0], p["rwkv_g2"].shape[0]
    H = W // RW_N

    w_in = p["w_in"]
    o = W
    cols = [w_in[:, :W], w_in[:, o:o + 3 * W]]
    o += 3 * W
    cols += [_pad_cols(w_in[:, o:o + dl], 128), _pad_cols(w_in[:, o + dl:o + dl + al], 128),
             _pad_cols(w_in[:, o + dl + al:o + dl + al + gl], 256)]
    w_in_r = jnp.concatenate(cols, axis=1).astype(BF16)
    tm_in = min(1024, T)
    proj = _inproj(x2, p["norm_mix_g"].reshape(1, D), w_in_r, tm_in, 1152)

    C = S5_CHUNK
    nc = L // C
    ct_re = jnp.swapaxes(p["s5_c_re"], 1, 2)
    ct_im = jnp.swapaxes(p["s5_c_im"], 1, 2)
    m_op, wbt, wct, a64 = _s5_ops(p["s5_lambda_re"], p["s5_lambda_im"], p["s5_log_step"],
                                  ct_re, ct_im, p["s5_b_re"], p["s5_b_im"], p["s5_c_re"], p["s5_c_im"])
    rb = min(LANES, B * nc)
    z = _s5_in(proj.reshape(B * nc, C, proj.shape[1]), G, rb)
    y_t = _s5_chunk(z, m_op, wbt, wct, a64, nc)
    y_ssm = _s5_out(y_t, rb).reshape(T, W)
    tm = min(512, T)
    s5_out = _s5_glu(y_ssm, proj, p["s5_d"].reshape(1, W), p["s5_w_glu"].astype(BF16),
                     p["s5_b_glu"].reshape(1, W), tm)

    mu = p["rwkv_mu"]
    mu3 = _pad_rows(mu[:3 * W].reshape(3, W), 8)
    o = 3 * W
    mul = jnp.concatenate([_pad_cols(mu[None, o:o + dl], 128), _pad_cols(mu[None, o + dl:o + dl + al], 128),
                           _pad_cols(mu[None, o + dl + al:], 256)], axis=1)
    w2p = _pad_rows(p["rwkv_w2"], 128).astype(BF16)
    a2p = _pad_rows(p["rwkv_a2"], 128).astype(BF16)
    g2p = _pad_rows(p["rwkv_g2"], 256).astype(BF16)
    head_of = jnp.arange(W, dtype=I32) // RW_N
    e_mat = (head_of[:, None] == jnp.arange(LANES, dtype=I32)[None, :]).astype(BF16)
    et_mat = e_mat.T
    row = lambda a: a.reshape(1, W)
    tm_rw = min(256, L)
    r, k, v, nk, bv, lw, g = _rwkv_prep(proj, B, L, mu3, mul, row(p["rwkv_w0"]), row(p["rwkv_a0"]),
                                        row(p["rwkv_k_k"]), row(p["rwkv_k_a"]), w2p, a2p, g2p,
                                        e_mat, et_mat, tm_rw)
    y_rw = _rwkv_chunk(r, k, v, nk, bv, lw, B, L)
    rw_out = _rwkv_post(y_rw, r, k, v, g, p["rwkv_r_k"].reshape(1, W), row(p["rwkv_ln_w"]),
                        row(p["rwkv_ln_b"]), e_mat, et_mat, tm)

    w_out = p["w_out"].astype(BF16)
    w_route = jnp.concatenate([p["w_route_exp"], p["w_route_grp"]], axis=1)
    w_route = _pad_cols(w_route, LANES)
    wr_hi = w_route.astype(BF16)
    wr_lo = (w_route - wr_hi.astype(F32)).astype(BF16)
    b_route = _pad_cols(jnp.concatenate([p["b_route_exp"], p["b_route_grp"]])[None, :], LANES)
    tm_o = min(256, T)
    h, hn, rec = _outproj(s5_out, rw_out, x2, w_out[:W], w_out[W:], p["norm_ffn_g"].reshape(1, D),
                          wr_hi, wr_lo, b_route, tm_o)

    tb = min(256, T)
    rank, cnt = _rank(rec, tb)
    n_blocks = -(-2 * T // MOE_BLOCK) + N_EXPERTS
    nb_rows = -(-(n_blocks + 1) // 8) * 8
    dest, blk = _dest(rec, rank, cnt, tb, nb_rows)
    blk_exp = blk[:n_blocks, 0]
    nact = blk[0:1, 1]
    lastblk = blk[nb_rows - 1, :N_EXPERTS]
    ts = min(256, T)
    dest2 = dest[:, :2].reshape(T // ts, 2 * ts)

    xs = _scatter(lastblk, nact, dest2, hn, ts, n_blocks)
    ys = _experts(blk_exp, nact, xs, p["w_gate"], p["w_up"], p["w_down"], n_blocks)
    return dest2, ys, h, rec, ts


def kernel(x, norm_mix_g, w_in, s5_lambda_re, s5_lambda_im, s5_log_step, s5_b_re, s5_b_im, s5_c_re, s5_c_im, s5_d, s5_w_glu, s5_b_glu, rwkv_mu, rwkv_w0, rwkv_w2, rwkv_a0, rwkv_a2, rwkv_g2, rwkv_k_k, rwkv_k_a, rwkv_r_k, rwkv_ln_w, rwkv_ln_b, w_out, norm_ffn_g, w_route_grp, b_route_grp, w_route_exp, b_route_exp, w_gate, w_up, w_down, norm_final_g):
    B, L, D = x.shape
    params = dict(
        norm_mix_g=norm_mix_g, w_in=w_in, s5_lambda_re=s5_lambda_re, s5_lambda_im=s5_lambda_im,
        s5_log_step=s5_log_step, s5_b_re=s5_b_re, s5_b_im=s5_b_im, s5_c_re=s5_c_re, s5_c_im=s5_c_im,
        s5_d=s5_d, s5_w_glu=s5_w_glu, s5_b_glu=s5_b_glu, rwkv_mu=rwkv_mu, rwkv_w0=rwkv_w0, rwkv_w2=rwkv_w2,
        rwkv_a0=rwkv_a0, rwkv_a2=rwkv_a2, rwkv_g2=rwkv_g2, rwkv_k_k=rwkv_k_k, rwkv_k_a=rwkv_k_a,
        rwkv_r_k=rwkv_r_k, rwkv_ln_w=rwkv_ln_w, rwkv_ln_b=rwkv_ln_b, w_out=w_out, norm_ffn_g=norm_ffn_g,
        w_route_grp=w_route_grp, b_route_grp=b_route_grp, w_route_exp=w_route_exp, b_route_exp=b_route_exp,
        w_gate=w_gate, w_up=w_up, w_down=w_down)
    depth = norm_mix_g.shape[0]
    h2 = x.reshape(B * L, D)
    for l in range(depth):
        p = {k_: v_[l] for k_, v_ in params.items()}
        dest2, ys, h, rec, ts = _layer(h2, B, L, p)
        h2 = _combine(dest2, ys, h, rec, norm_final_g.reshape(1, D), ts, normalize=(l == depth - 1))
    return h2.reshape(B, L, D)
```

```python
import functools

import jax
import jax.numpy as jnp
from jax import lax
from jax.experimental import pallas as pl
from jax.experimental.pallas import tpu as pltpu

F32 = jnp.float32
BF16 = jnp.bfloat16
I32 = jnp.int32

RMS_EPS = 1e-6
S5_CHUNK = 64
S5_HG = 16
S5_P = 64
RW_N = 64
RW_CHUNK = 64
RW_GN_EPS = 64e-5
N_GROUPS = 8
EPG = 8
N_EXPERTS = 64
MOE_BLOCK = 256
LANES = 128
VMEM_LIMIT = 56 * 1024 * 1024


def _cparams(*sem):
    return pltpu.CompilerParams(dimension_semantics=sem, vmem_limit_bytes=VMEM_LIMIT)


def _split2(x):
    hi = x.astype(BF16)
    lo = (x - hi.astype(F32)).astype(BF16)
    return hi, lo


def _split3(x):
    hi = x.astype(BF16)
    r = x - hi.astype(F32)
    mid = r.astype(BF16)
    lo = (r - mid.astype(F32)).astype(BF16)
    return hi, mid, lo


def _dot(a, b):
    return jnp.dot(a, b, preferred_element_type=F32)


def _dot_nt(a, b):
    return lax.dot_general(a, b, (((1,), (1,)), ((), ())), preferred_element_type=F32)


def _dot_tn(a, b):
    return lax.dot_general(a, b, (((0,), (0,)), ((), ())), preferred_element_type=F32)


def _dot_x3(a, b):
    ah, al = _split2(a)
    bh, bl = _split2(b)
    return _dot(ah, bh) + (_dot(ah, bl) + _dot(al, bh))


def _dot_exact_lhs(a_bf16, b):
    bh, bm, bl = _split3(b)
    return _dot(a_bf16, bh) + (_dot(a_bf16, bm) + _dot(a_bf16, bl))


def _cmul(ar, ai, br, bi):
    return ar * br - ai * bi, ar * bi + ai * br


U32 = jnp.uint32


def _pack_halves(x):
    half = x.shape[1] // 2
    lo = lax.bitcast_convert_type(x[:, :half].astype(BF16).astype(F32), U32)
    hi = lax.bitcast_convert_type(x[:, half:].astype(BF16).astype(F32), U32)
    return (lo >> 16) | hi


def _unpack_halves(p):
    lo = lax.bitcast_convert_type(p << 16, F32)
    hi = lax.bitcast_convert_type(p & jnp.uint32(0xFFFF0000), F32)
    return lo, hi


def _inproj_kernel(x_ref, g_ref, w_ref, o_ref, hn_ref):
    @pl.when(pl.program_id(1) == 0)
    def _():
        x = x_ref[...]
        ms = jnp.mean(x * x, axis=-1, keepdims=True)
        hn_ref[...] = (x * lax.rsqrt(ms + RMS_EPS) * g_ref[...]).astype(BF16)

    o_ref[...] = _dot(hn_ref[...], w_ref[...])


def _inproj(x2, g, w_bf16, tm, tn):
    T, D = x2.shape
    N = w_bf16.shape[1]
    return pl.pallas_call(
        _inproj_kernel,
        grid=(T // tm, N // tn),
        in_specs=[
            pl.BlockSpec((tm, D), lambda i, j: (i, 0)),
            pl.BlockSpec((1, D), lambda i, j: (0, 0)),
            pl.BlockSpec((D, tn), lambda i, j: (0, j)),
        ],
        out_specs=pl.BlockSpec((tm, tn), lambda i, j: (i, j)),
        out_shape=jax.ShapeDtypeStruct((T, N), F32),
        scratch_shapes=[pltpu.VMEM((tm, D), BF16)],
        compiler_params=_cparams("arbitrary", "arbitrary"),
        name="inproj",
    )(x2, g, w_bf16)


def _binpow(ar, ai, expo, nbits):
    pr = jnp.ones(expo.shape, F32)
    pi = jnp.zeros(expo.shape, F32)
    sr, si = ar, ai
    for bit in range(nbits):
        m = ((expo >> bit) & 1) == 1
        nr, ni = _cmul(pr, pi, sr, si)
        pr = jnp.where(m, nr, pr)
        pi = jnp.where(m, ni, pi)
        if bit + 1 < nbits:
            sr, si = _cmul(sr, si, sr, si)
    return pr, pi


def _zoh(lr, li, dt):
    mag = jnp.exp(lr * dt)
    ang = li * dt
    ar, ai = mag * jnp.cos(ang), mag * jnp.sin(ang)
    den = lr * lr + li * li
    nr, ni = ar - 1.0, ai
    fr = (nr * lr + ni * li) / den
    fi = (ni * lr - nr * li) / den
    return ar, ai, fr, fi


def _s5_ops_kernel(lrc_ref, lic_ref, dt_ref, ct_re_ref, ct_im_ref, bt_re_ref, bt_im_ref, c_re_ref, c_im_ref,
                   m_ref, wbt_ref, wct_ref, a64_ref):
    C, HG, P = S5_CHUNK, S5_HG, S5_P
    W = C * HG
    dt = jnp.exp(dt_ref[0])
    ar_c, ai_c, fr_c, fi_c = _zoh(lrc_ref[0], lic_ref[0], dt)

    tau = lax.broadcasted_iota(I32, (1, W), 1) // HG
    nbits = max(1, (C - 1).bit_length())
    pr, pi = _binpow(ar_c, ai_c, jnp.broadcast_to(tau, (P, W)), nbits)
    qr, qi = _binpow(ar_c, ai_c, jnp.broadcast_to(C - 1 - tau, (P, W)), nbits)

    sel = (lax.broadcasted_iota(I32, (HG, W), 1) % HG == lax.broadcasted_iota(I32, (HG, W), 0)).astype(BF16)
    tile = lambda ref: _dot_exact_lhs_rhs(ref[0], sel)

    ca_re, ca_im = _cmul(tile(ct_re_ref), tile(ct_im_ref), pr, pi)
    c1_re, c1_im = _cmul(ca_re, ca_im, ar_c, ai_c)
    wct_ref[0] = jnp.concatenate([c1_re, -c1_im], axis=0).T.astype(BF16)

    bb_re, bb_im = _cmul(tile(bt_re_ref), tile(bt_im_ref), fr_c, fi_c)
    ab_re, ab_im = _cmul(bb_re, bb_im, qr, qi)
    wbt_ref[0, :P, :] = ab_re.astype(BF16)
    wbt_ref[0, P:, :] = ab_im.astype(BF16)

    cr, ci = ar_c, ai_c
    for _ in range(C.bit_length() - 1):
        cr, ci = _cmul(cr, ci, cr, ci)
    a64_ref[0] = jnp.concatenate([cr, ci], axis=0)

    strip = _dot_x3(c_re_ref[0], ab_re) - _dot_x3(c_im_ref[0], ab_im)
    ext = jnp.concatenate([strip, jnp.zeros((HG, W), F32)], axis=1)
    for t in range(C):
        off = (C - 1 - t) * HG
        m_ref[0, t * HG:(t + 1) * HG, :] = ext[:, off:off + W].astype(BF16)


def _s5_ops(lam_re, lam_im, log_step, ct_re, ct_im, bt_re, bt_im, c_re, c_im):
    G, P = lam_re.shape
    C, HG = S5_CHUNK, S5_HG
    W = C * HG
    col = lambda a: a.reshape(G, P, 1)
    g3 = lambda s1, s2: pl.BlockSpec((1, s1, s2), lambda g: (g, 0, 0))
    return pl.pallas_call(
        _s5_ops_kernel,
        grid=(G,),
        in_specs=[g3(P, 1), g3(P, 1), g3(1, 1),
                  g3(P, HG), g3(P, HG), g3(P, HG), g3(P, HG), g3(HG, P), g3(HG, P)],
        out_specs=[g3(W, W), g3(2 * P, W), g3(W, 2 * P), g3(2 * P, 1)],
        out_shape=[jax.ShapeDtypeStruct((G, W, W), BF16),
                   jax.ShapeDtypeStruct((G, 2 * P, W), BF16),
                   jax.ShapeDtypeStruct((G, W, 2 * P), BF16),
                   jax.ShapeDtypeStruct((G, 2 * P, 1), F32)],
        compiler_params=_cparams("arbitrary"),
        name="s5_ops",
    )(col(lam_re), col(lam_im), log_step.reshape(G, 1, 1), ct_re, ct_im, bt_re, bt_im, c_re, c_im)


S5_FB = 8


def _s5_in_kernel(x_ref, z_ref):
    G, _, rb = z_ref.shape
    for f in range(S5_FB):
        xt = x_ref[:, f, :].T
        z_ref[:, f * S5_HG:(f + 1) * S5_HG, :] = xt.reshape(G, S5_HG, rb).astype(BF16)


def _s5_in(proj3, G, rb):
    R, C, _ = proj3.shape
    W = G * S5_HG
    return pl.pallas_call(
        _s5_in_kernel,
        grid=(R // rb, C // S5_FB),
        in_specs=[pl.BlockSpec((rb, S5_FB, W), lambda i, j: (i, j, 0))],
        out_specs=pl.BlockSpec((G, S5_FB * S5_HG, rb), lambda i, j: (0, j, i)),
        out_shape=jax.ShapeDtypeStruct((G, C * S5_HG, R), BF16),
        compiler_params=_cparams("arbitrary", "arbitrary"),
        name="s5_in",
    )(proj3)


def _s5_out_kernel(yt_ref, o_ref):
    G, _, rb = yt_ref.shape
    for f in range(S5_FB):
        slab = yt_ref[:, f * S5_HG:(f + 1) * S5_HG, :].reshape(G * S5_HG, rb)
        o_ref[:, f, :] = slab.T


def _s5_out(yt, rb):
    G, CW, R = yt.shape
    C = CW // S5_HG
    W = G * S5_HG
    return pl.pallas_call(
        _s5_out_kernel,
        grid=(R // rb, C // S5_FB),
        in_specs=[pl.BlockSpec((G, S5_FB * S5_HG, rb), lambda i, j: (0, j, i))],
        out_specs=pl.BlockSpec((rb, S5_FB, W), lambda i, j: (i, j, 0)),
        out_shape=jax.ShapeDtypeStruct((R, C, W), F32),
        compiler_params=_cparams("arbitrary", "arbitrary"),
        name="s5_out",
    )(yt)


def _s5_chunk_kernel(z_ref, m_ref, wbt_ref, wct_ref, a64_ref, y_ref, *, n_chunks):
    P = S5_P
    z = z_ref[0]
    R = z.shape[1]
    y = _dot(m_ref[0], z)
    x = _dot(wbt_ref[0], z)
    xr, xi = x[:P], x[P:]
    a = a64_ref[0]
    ar, ai = a[:P], a[P:]
    cidx = lax.broadcasted_iota(I32, (1, R), 1) % n_chunks
    shift = 1
    while shift < n_chunks:
        keep = cidx >= shift
        sr = jnp.where(keep, pltpu.roll(xr, shift, 1), 0.0)
        si = jnp.where(keep, pltpu.roll(xi, shift, 1), 0.0)
        xr, xi = xr + (ar * sr - ai * si), xi + (ar * si + ai * sr)
        ar, ai = ar * ar - ai * ai, 2.0 * ar * ai
        shift *= 2
    keep = cidx >= 1
    s_in = jnp.concatenate([jnp.where(keep, pltpu.roll(xr, 1, 1), 0.0),
                            jnp.where(keep, pltpu.roll(xi, 1, 1), 0.0)], axis=0)
    sh, sl = _split2(s_in)
    wct = wct_ref[0]
    y_ref[0] = y + (_dot(wct, sh) + _dot(wct, sl))


def _s5_chunk(z, m, wbt, wct, a64, n_chunks):
    G, W, R = z.shape
    P2 = wbt.shape[1]
    g3 = lambda s1, s2: pl.BlockSpec((1, s1, s2), lambda g: (g, 0, 0))
    return pl.pallas_call(
        functools.partial(_s5_chunk_kernel, n_chunks=n_chunks),
        grid=(G,),
        in_specs=[g3(W, R), g3(W, W), g3(P2, W), g3(W, P2), g3(P2, 1)],
        out_specs=g3(W, R),
        out_shape=jax.ShapeDtypeStruct((G, W, R), F32),
        compiler_params=_cparams("arbitrary"),
        name="s5_chunk",
    )(z, m, wbt, wct, a64)


def _gelu_tanh(x):
    return 0.5 * x * (1.0 + jnp.tanh(0.7978845608028654 * (x + 0.044715 * (x * x * x))))


def _s5_glu_kernel(y_ref, u_ref, d_ref, w_ref, b_ref, o_ref):
    y = y_ref[...] + d_ref[...] * u_ref[...]
    y = _gelu_tanh(y)
    z = _dot(y.astype(BF16), w_ref[...]) + b_ref[...]
    o_ref[...] = (y * jax.nn.sigmoid(z)).astype(BF16)


def _s5_glu(y_ssm, proj, d, w_bf16, b, tm):
    T, W = y_ssm.shape
    return pl.pallas_call(
        _s5_glu_kernel,
        grid=(T // tm,),
        in_specs=[
            pl.BlockSpec((tm, W), lambda i: (i, 0)),
            pl.BlockSpec((tm, W), lambda i: (i, 0)),
            pl.BlockSpec((1, W), lambda i: (0, 0)),
            pl.BlockSpec((W, W), lambda i: (0, 0)),
            pl.BlockSpec((1, W), lambda i: (0, 0)),
        ],
        out_specs=pl.BlockSpec((tm, W), lambda i: (i, 0)),
        out_shape=jax.ShapeDtypeStruct((T, W), BF16),
        compiler_params=_cparams("arbitrary"),
        name="s5_glu",
    )(y_ssm, proj, d, w_bf16, b)


def _head_sum(x, e_ref, et_ref):
    xh, xl = _split2(x)
    e = e_ref[...]
    s = _dot(xh, e) + _dot(xl, e)
    sh, sl = _split2(s)
    et = et_ref[...]
    return _dot(sh, et) + _dot(sl, et)


def _shift(z, prev_row):
    rolled = pltpu.roll(z, 1, 0)
    first = lax.broadcasted_iota(I32, (z.shape[0], 1), 0) == 0
    return jnp.where(first, prev_row, rolled)


def _rwkv_prep_kernel(zr_ref, zk_ref, zv_ref, zl_ref, mu_ref, mul_ref, w0_ref, a0_ref, kk_ref, ka_ref,
                      w2_ref, a2_ref, g2_ref, e_ref, et_ref,
                      r_ref, k_ref, v_ref, nk_ref, b_ref, lw_ref, g_ref,
                      car_ref, carl_ref):
    W = r_ref.shape[1]

    @pl.when(pl.program_id(1) == 0)
    def _():
        car_ref[...] = jnp.zeros_like(car_ref)
        carl_ref[...] = jnp.zeros_like(carl_ref)

    def lerp(z, prev_row, mu):
        return z + (_shift(z, prev_row) - z) * mu

    tm = zr_ref.shape[0]
    zr, zk, zv, zl = zr_ref[...], zk_ref[...], zv_ref[...], zl_ref[...]
    r = lerp(zr, car_ref[0:1, :], mu_ref[0:1, :])
    k = lerp(zk, car_ref[1:2, :], mu_ref[1:2, :])
    v = lerp(zv, car_ref[2:3, :], mu_ref[2:3, :])
    xl = lerp(zl, carl_ref[0:1, :], mul_ref[...])
    car_ref[0:1, :] = zr[tm - 1:tm, :]
    car_ref[1:2, :] = zk[tm - 1:tm, :]
    car_ref[2:3, :] = zv[tm - 1:tm, :]
    carl_ref[0:1, :] = zl[tm - 1:tm, :]

    xw, xa, xg = xl[:, 0:128], xl[:, 128:256], xl[:, 256:512]
    dw = _dot(jnp.tanh(xw).astype(BF16), w2_ref[...])
    da = _dot(xa.astype(BF16), a2_ref[...])
    g = _dot(jax.nn.sigmoid(xg).astype(BF16), g2_ref[...])

    zw = -(w0_ref[...] + dw)
    softplus = jnp.maximum(zw, 0.0) + jnp.log(1.0 + jnp.exp(-jnp.abs(zw)))
    w_log = -softplus - 0.5
    a = jax.nn.sigmoid(a0_ref[...] + da)

    kk = k * kk_ref[...]
    n2 = _head_sum(kk * kk, e_ref, et_ref)
    kk = kk / jnp.maximum(jnp.sqrt(n2), 1e-12)

    r_ref[...] = r.astype(BF16)
    k_ref[...] = (k * (1.0 + (a - 1.0) * ka_ref[...])).astype(BF16)
    v_ref[...] = v.astype(BF16)
    nk_ref[...] = kk.astype(BF16)
    b_ref[...] = (kk * a).astype(BF16)
    lw_ref[...] = -jnp.exp(w_log)
    g_ref[...] = g.astype(BF16)


def _rwkv_prep(proj, B, L, mu3, mul, w0, a0, k_k, k_a, w2p, a2p, g2p, e_mat, et_mat, tm):
    T = B * L
    W = w0.shape[1]
    nt = L // tm
    row = lambda c: pl.BlockSpec((tm, W), lambda b, i, c=c: (b * nt + i, c))
    full = lambda a: pl.BlockSpec(a.shape, lambda b, i: (0, 0))
    out = pl.BlockSpec((tm, W), lambda b, i: (b * nt + i, 0))
    lw = 512
    return pl.pallas_call(
        _rwkv_prep_kernel,
        grid=(B, nt),
        in_specs=[row(1), row(2), row(3),
                  pl.BlockSpec((tm, lw), lambda b, i: (b * nt + i, 4 * W // lw)),
                  full(mu3), full(mul), full(w0), full(a0), full(k_k), full(k_a),
                  full(w2p), full(a2p), full(g2p), full(e_mat), full(et_mat)],
        out_specs=[out] * 7,
        out_shape=[jax.ShapeDtypeStruct((T, W), dt) for dt in (BF16, BF16, BF16, BF16, BF16, F32, BF16)],
        scratch_shapes=[pltpu.VMEM((8, W), F32), pltpu.VMEM((8, lw), F32)],
        compiler_params=_cparams("arbitrary", "arbitrary"),
        name="rwkv_prep",
    )(proj, proj, proj, proj, mu3, mul, w0, a0, k_k, k_a, w2p, a2p, g2p, e_mat, et_mat)


def _rwkv_chunk_kernel(r_ref, k_ref, v_ref, nk_ref, b_ref, lw_ref, y_ref, z_ref):
    C, N = RW_CHUNK, RW_N
    H = r_ref.shape[1] // N

    @pl.when(pl.program_id(1) == 0)
    def _():
        z_ref[...] = jnp.zeros_like(z_ref)

    ri = lax.broadcasted_iota(I32, (C, C), 0)
    ci = lax.broadcasted_iota(I32, (C, C), 1)
    tril = ri >= ci
    stril = ri > ci
    eye = (ri == ci).astype(F32)

    lw = lw_ref[...]
    cs = _dot_exact_lhs(tril.astype(BF16), lw)
    tot = cs[C - 1:C, :]
    p_inc = jnp.exp(cs)
    p_inv = jnp.exp(-cs)
    p_prev = jnp.exp(cs - lw)
    p_rest = jnp.exp(tot - cs)
    p_tot = jnp.exp(tot)

    k_f, b_f = k_ref[...].astype(F32), b_ref[...].astype(F32)
    r_t = r_ref[...].astype(F32) * p_inc
    k_t = k_f * p_inv
    a_t = -nk_ref[...].astype(F32) * p_prev
    b_t = b_f * p_inv
    k_h = k_f * p_rest
    b_h = b_f * p_rest
    v_all = v_ref[...]

    hs = range(H)
    sl = [slice(h * N, (h + 1) * N) for h in hs]
    each = lambda f: [f(h) for h in hs]
    v = each(lambda h: v_all[:, sl[h]].astype(BF16))
    ar = each(lambda h: jnp.concatenate([a_t[:, sl[h]], r_t[:, sl[h]]], axis=0).astype(BF16))
    m_b = each(lambda h: _dot_nt(ar[h], b_t[:, sl[h]].astype(BF16)))
    m_k = each(lambda h: _dot_nt(ar[h], k_t[:, sl[h]].astype(BF16)))
    l_ab = each(lambda h: jnp.where(stril, m_b[h][:C], 0.0).astype(BF16))
    m_rb = each(lambda h: jnp.where(tril, m_b[h][C:], 0.0).astype(BF16))
    l_ak = each(lambda h: jnp.where(stril, m_k[h][:C], 0.0).astype(BF16))
    m_rk = each(lambda h: jnp.where(tril, m_k[h][C:], 0.0).astype(BF16))

    x = each(lambda h: jnp.concatenate([_dot(l_ak[h], v[h]), a_t[:, sl[h]]], axis=1))
    y0 = each(lambda h: _dot(m_rk[h], v[h]))
    hk = each(lambda h: _dot_tn(k_h[:, sl[h]].astype(BF16), v[h]))
    lp = l_ab
    step = 1
    while step < C:
        x = each(lambda h: x[h] + _dot(lp[h], x[h].astype(BF16)))
        step *= 2
        if step < C:
            lp = each(lambda h: _dot(lp[h], lp[h]).astype(BF16))
    xb = each(lambda h: x[h].astype(BF16))
    yq = each(lambda h: _dot(m_rb[h], xb[h]))
    gh = each(lambda h: _dot_tn(b_h[:, sl[h]].astype(BF16), xb[h]))

    z = each(lambda h: _split2(z_ref[h]))
    qb = each(lambda h: (yq[h][:, N:] + r_t[:, sl[h]]).astype(BF16))
    gb = each(lambda h: (gh[h][:, N:] + eye * p_tot[:, sl[h]]).astype(BF16))
    yz = each(lambda h: _dot(qb[h], z[h][0]) + _dot(qb[h], z[h][1]))
    gz = each(lambda h: _dot(gb[h], z[h][0]) + _dot(gb[h], z[h][1]))
    for h in hs:
        y_ref[:, sl[h]] = (yq[h][:, :N] + y0[h]) + yz[h]
        z_ref[h] = (gh[h][:, :N] + hk[h]) + gz[h]


def _rwkv_chunk(r, k, v, nk, b, lw, B, L):
    T, W = r.shape
    C = RW_CHUNK
    nc = L // C
    H = W // RW_N
    spec = pl.BlockSpec((C, W), lambda bi, c: (bi * nc + c, 0))
    return pl.pallas_call(
        _rwkv_chunk_kernel,
        grid=(B, nc),
        in_specs=[spec] * 6,
        out_specs=spec,
        out_shape=jax.ShapeDtypeStruct((T, W), F32),
        scratch_shapes=[pltpu.VMEM((H, RW_N, RW_N), F32)],
        compiler_params=_cparams("arbitrary", "arbitrary"),
        name="rwkv_chunk",
    )(r, k, v, nk, b, lw)


def _rwkv_post_kernel(y_ref, r_ref, k_ref, v_ref, g_ref, rk_ref, lnw_ref, lnb_ref, e_ref, et_ref, o_ref):
    inv_n = 1.0 / RW_N
    y = y_ref[...]
    mean = _head_sum(y, e_ref, et_ref) * inv_n
    yc = y - mean
    var = _head_sum(yc * yc, e_ref, et_ref) * inv_n
    yn = yc * lax.rsqrt(var + RW_GN_EPS) * lnw_ref[...] + lnb_ref[...]
    r, k, v = r_ref[...].astype(F32), k_ref[...].astype(F32), v_ref[...].astype(F32)
    bonus = _head_sum(r * k * rk_ref[...], e_ref, et_ref) * v
    o_ref[...] = ((yn + bonus) * g_ref[...].astype(F32)).astype(BF16)


def _rwkv_post(y, r, k, v, g, r_k, ln_w, ln_b, e_mat, et_mat, tm):
    T, W = y.shape
    row = pl.BlockSpec((tm, W), lambda i: (i, 0))
    full = lambda a: pl.BlockSpec(a.shape, lambda i: (0, 0))
    return pl.pallas_call(
        _rwkv_post_kernel,
        grid=(T // tm,),
        in_specs=[row] * 5 + [full(r_k), full(ln_w), full(ln_b), full(e_mat), full(et_mat)],
        out_specs=row,
        out_shape=jax.ShapeDtypeStruct((T, W), BF16),
        compiler_params=_cparams("arbitrary"),
        name="rwkv_post",
    )(y, r, k, v, g, r_k, ln_w, ln_b, e_mat, et_mat)


def _first_index_of_max(vals, lane, valid):
    neg = jnp.float32(-jnp.inf)
    masked = jnp.where(valid, vals, neg)
    m = jnp.max(masked, axis=-1, keepdims=True)
    idx = jnp.min(jnp.where(valid & (masked == m), lane, LANES), axis=-1, keepdims=True)
    return m, idx


def _outproj_kernel(s5_ref, rw_ref, x_ref, wt_ref, wb_ref, g_ref, wrh_ref, wrl_ref, br_ref,
                    h_ref, hn_ref, rec_ref):
    h = x_ref[...] + (_dot(s5_ref[...], wt_ref[...]) + _dot(rw_ref[...], wb_ref[...]))
    h_ref[...] = h
    ms = jnp.mean(h * h, axis=-1, keepdims=True)
    hn = h * lax.rsqrt(ms + RMS_EPS) * g_ref[...]
    hn_ref[...] = _pack_halves(hn).reshape(hn_ref.shape)

    hh, hl = _split2(hn)
    wh, wl = wrh_ref[...], wrl_ref[...]
    logits = _dot(hh, wh) + (_dot(hh, wl) + _dot(hl, wh)) + br_ref[...]
    lane = lax.broadcasted_iota(I32, logits.shape, 1)
    is_grp = (lane >= N_EXPERTS) & (lane < N_EXPERTS + N_GROUPS)
    gmax, gidx = _first_index_of_max(logits, lane, is_grp)
    gsum = jnp.sum(jnp.where(is_grp, jnp.exp(logits - gmax), 0.0), axis=-1, keepdims=True)
    p_grp = 1.0 / gsum
    grp = gidx - N_EXPERTS
    in_grp = (lane >= grp * EPG) & (lane < (grp + 1) * EPG)
    m1, i1 = _first_index_of_max(logits, lane, in_grp)
    m2, i2 = _first_index_of_max(logits, lane, in_grp & (lane != i1))
    e = jnp.exp(m2 - m1)
    g1 = p_grp / (1.0 + e)
    g2 = p_grp * e / (1.0 + e)
    rec = jnp.where(lane == 0, i1.astype(F32),
          jnp.where(lane == 1, i2.astype(F32),
          jnp.where(lane == 2, g1, jnp.where(lane == 3, g2, 0.0))))
    rec_ref[...] = rec


def _outproj(s5o, rwo, x2, w_top, w_bot, g, wr_hi, wr_lo, b_route, tm):
    T, D = x2.shape
    W = s5o.shape[1]
    full = lambda a: pl.BlockSpec(a.shape, lambda i: (0, 0))
    return pl.pallas_call(
        _outproj_kernel,
        grid=(T // tm,),
        in_specs=[pl.BlockSpec((tm, W), lambda i: (i, 0)), pl.BlockSpec((tm, W), lambda i: (i, 0)),
                  pl.BlockSpec((tm, D), lambda i: (i, 0)),
                  full(w_top), full(w_bot), full(g), full(wr_hi), full(wr_lo), full(b_route)],
        out_specs=[pl.BlockSpec((tm, D), lambda i: (i, 0)), pl.BlockSpec((tm, 1, D // 2), lambda i: (i, 0, 0)),
                   pl.BlockSpec((tm, LANES), lambda i: (i, 0))],
        out_shape=[jax.ShapeDtypeStruct((T, D), F32), jax.ShapeDtypeStruct((T, 1, D // 2), U32),
                   jax.ShapeDtypeStruct((T, LANES), F32)],
        compiler_params=_cparams("arbitrary"),
        name="outproj_route",
    )(s5o, rwo, x2, w_top, w_bot, g, wr_hi, wr_lo, b_route)


def _onehots(rec, lane):
    oh0 = (lane == rec[:, 0:1].astype(I32)).astype(F32)
    oh1 = (lane == rec[:, 1:2].astype(I32)).astype(F32)
    return oh0, oh1


def _rank_kernel(rec_ref, rank_ref, cnt_ref, base_ref):
    tb = rec_ref.shape[0]

    @pl.when(pl.program_id(0) == 0)
    def _():
        base_ref[...] = jnp.zeros_like(base_ref)

    lane = lax.broadcasted_iota(I32, (tb, LANES), 1)
    oh0, oh1 = _onehots(rec_ref[...], lane)
    both = oh0 + oh1
    ri = lax.broadcasted_iota(I32, (tb, tb), 0)
    ci = lax.broadcasted_iota(I32, (tb, tb), 1)
    before = _dot((ri > ci).astype(BF16), both.astype(BF16)) + base_ref[0:1, :]
    rank0 = jnp.sum(oh0 * before, axis=-1, keepdims=True)
    rank1 = jnp.sum(oh1 * before, axis=-1, keepdims=True)
    rank_ref[...] = jnp.where(lane == 0, rank0, jnp.where(lane == 1, rank1, 0.0))
    total = base_ref[0:1, :] + jnp.sum(both, axis=0, keepdims=True)
    base_ref[0:1, :] = total
    cnt_ref[...] = jnp.broadcast_to(total, cnt_ref.shape)


def _rank(rec, tb):
    T = rec.shape[0]
    return pl.pallas_call(
        _rank_kernel,
        grid=(T // tb,),
        in_specs=[pl.BlockSpec((tb, LANES), lambda i: (i, 0))],
        out_specs=[pl.BlockSpec((tb, LANES), lambda i: (i, 0)), pl.BlockSpec((8, LANES), lambda i: (0, 0))],
        out_shape=[jax.ShapeDtypeStruct((T, LANES), F32), jax.ShapeDtypeStruct((8, LANES), F32)],
        scratch_shapes=[pltpu.VMEM((8, LANES), F32)],
        compiler_params=_cparams("arbitrary"),
        name="moe_rank",
    )(rec)


def _padded_starts(cnt):
    padded = jnp.ceil(cnt * (1.0 / MOE_BLOCK)) * MOE_BLOCK
    ri = lax.broadcasted_iota(I32, (LANES, LANES), 0)
    ci = lax.broadcasted_iota(I32, (LANES, LANES), 1)
    p8 = jnp.broadcast_to(padded, (8, LANES))
    pend = _dot_exact_lhs_rhs(p8, (ri <= ci).astype(BF16))[0:1, :]
    return pend - padded, pend


def _dot_exact_lhs_rhs(a, b_bf16):
    ah, am, al = _split3(a)
    return _dot(ah, b_bf16) + (_dot(am, b_bf16) + _dot(al, b_bf16))


def _dest_kernel(rec_ref, rank_ref, cnt_ref, dest_ref, blk_ref):
    tb = rec_ref.shape[0]
    cnt = cnt_ref[0:1, :]
    pstart, pend = _padded_starts(cnt)
    lane = lax.broadcasted_iota(I32, (tb, LANES), 1)
    oh0, oh1 = _onehots(rec_ref[...], lane)
    rank = rank_ref[...]
    d0 = jnp.sum(oh0 * pstart, axis=-1, keepdims=True) + rank[:, 0:1]
    d1 = jnp.sum(oh1 * pstart, axis=-1, keepdims=True) + rank[:, 1:2]
    dest_ref[...] = jnp.where(lane == 0, d0, jnp.where(lane == 1, d1, 0.0)).astype(I32)

    @pl.when(pl.program_id(0) == 0)
    def _():
        nb = blk_ref.shape[0]
        blane = lax.broadcasted_iota(I32, (nb, LANES), 1)
        bstart = (lax.broadcasted_iota(I32, (nb, 1), 0) * MOE_BLOCK).astype(F32)
        is_e = blane < N_EXPERTS
        bexp = jnp.sum(jnp.where(is_e & (pend <= bstart), 1.0, 0.0), axis=-1, keepdims=True)
        bexp = jnp.minimum(bexp, N_EXPERTS - 1.0)
        nact = jnp.max(jnp.where(is_e, pend, 0.0), axis=-1, keepdims=True) * (1.0 / MOE_BLOCK)
        lastblk = jnp.where(is_e & (pend > pstart), pend - MOE_BLOCK, -1.0)
        brow = lax.broadcasted_iota(I32, (nb, 1), 0)
        out = jnp.where(blane == 0, bexp, jnp.where(blane == 1, nact, 0.0))
        out = jnp.where(brow == nb - 1, lastblk, out)
        blk_ref[...] = out.astype(I32)


def _dest(rec, rank, cnt, tb, nb_rows):
    T = rec.shape[0]
    return pl.pallas_call(
        _dest_kernel,
        grid=(T // tb,),
        in_specs=[pl.BlockSpec((tb, LANES), lambda i: (i, 0)), pl.BlockSpec((tb, LANES), lambda i: (i, 0)),
                  pl.BlockSpec((8, LANES), lambda i: (0, 0))],
        out_specs=[pl.BlockSpec((tb, LANES), lambda i: (i, 0)), pl.BlockSpec((nb_rows, LANES), lambda i: (0, 0))],
        out_shape=[jax.ShapeDtypeStruct((T, LANES), I32), jax.ShapeDtypeStruct((nb_rows, LANES), I32)],
        compiler_params=_cparams("arbitrary"),
        name="moe_dest",
    )(rec, rank, cnt)


def _scatter_kernel(pad_ref, na_ref, dest_ref, hn_ref, xs_ref, zbuf_ref, idx_ref, sa_ref, sb_ref,
                    sem_a, sem_b, isem, zsem, *, tb, n_blocks, n_tail, n_steps):
    i = pl.program_id(0)

    @pl.when(i == 0)
    def _():
        zbuf_ref[...] = jnp.zeros_like(zbuf_ref)

        def zero_rows(start):
            return pltpu.make_async_copy(zbuf_ref, xs_ref.at[pl.ds(start, MOE_BLOCK)], zsem)

        def pad_start(e, c):
            @pl.when(pad_ref[e] >= 0)
            def _():
                zero_rows(pad_ref[e]).start()
            return c

        def pad_wait(e, c):
            @pl.when(pad_ref[e] >= 0)
            def _():
                zero_rows(pad_ref[e]).wait()
            return c

        def tail_start(k, c):
            @pl.when(na_ref[0] + k < n_blocks)
            def _():
                zero_rows((na_ref[0] + k) * MOE_BLOCK).start()
            return c

        def tail_wait(k, c):
            @pl.when(na_ref[0] + k < n_blocks)
            def _():
                zero_rows((na_ref[0] + k) * MOE_BLOCK).wait()
            return c

        lax.fori_loop(0, N_EXPERTS, pad_start, 0)
        lax.fori_loop(0, n_tail, tail_start, 0)
        lax.fori_loop(0, N_EXPERTS, pad_wait, 0)
        lax.fori_loop(0, n_tail, tail_wait, 0)

    def idx_copy(step, slot):
        return pltpu.make_async_copy(dest_ref.at[step], idx_ref.at[slot], isem)

    @pl.when(i == 0)
    def _():
        idx_copy(0, 0).start()

    slot = i % 2
    idx_copy(i, slot).wait()

    @pl.when(i + 1 < n_steps)
    def _():
        idx_copy(i + 1, 1 - slot).start()

    hb = tb // 2
    for half, (buf, sem) in enumerate(((sa_ref, sem_a), (sb_ref, sem_b))):
        def wait_rows(buf=buf, sem=sem):
            for _ in range(2):
                pltpu.make_async_copy(buf, xs_ref.at[pl.ds(0, hb)], sem).wait()

        @pl.when(i > 0)
        def _():
            wait_rows()

        buf[...] = hn_ref[half * hb:(half + 1) * hb]

        def issue(t, c, buf=buf, sem=sem, half=half):
            for s in range(2):
                d = idx_ref[slot, 2 * (half * hb + t) + s]
                pltpu.make_async_copy(buf.at[pl.ds(t, 1)], xs_ref.at[pl.ds(d, 1)], sem).start()
            return c

        lax.fori_loop(0, hb, issue, 0, unroll=8)

    @pl.when(i == n_steps - 1)
    def _():
        for buf, sem in ((sa_ref, sem_a), (sb_ref, sem_b)):
            for _ in range(2):
                pltpu.make_async_copy(buf, xs_ref.at[pl.ds(0, hb)], sem).wait()


def _scatter(lastblk, nact, dest2, hn, tb, n_blocks):
    T, _, D = hn.shape
    n_tail = n_blocks - (-(-2 * T // MOE_BLOCK))
    hb = tb // 2
    return pl.pallas_call(
        functools.partial(_scatter_kernel, tb=tb, n_blocks=n_blocks, n_tail=n_tail, n_steps=T // tb),
        grid_spec=pltpu.PrefetchScalarGridSpec(
            num_scalar_prefetch=2, grid=(T // tb,),
            in_specs=[pl.BlockSpec(memory_space=pl.ANY),
                      pl.BlockSpec((tb, 1, D), lambda i, pad, na: (i, 0, 0))],
            out_specs=pl.BlockSpec(memory_space=pl.ANY),
            scratch_shapes=[pltpu.VMEM((MOE_BLOCK, 1, D), hn.dtype), pltpu.SMEM((2, 2 * tb), I32),
                            pltpu.VMEM((hb, 1, D), hn.dtype), pltpu.VMEM((hb, 1, D), hn.dtype),
                            pltpu.SemaphoreType.DMA, pltpu.SemaphoreType.DMA,
                            pltpu.SemaphoreType.DMA, pltpu.SemaphoreType.DMA]),
        out_shape=jax.ShapeDtypeStruct((n_blocks * MOE_BLOCK, 1, D), hn.dtype),
        compiler_params=_cparams("arbitrary"),
        name="moe_scatter",
    )(lastblk, nact, dest2, hn)


def _expert_kernel(be_ref, na_ref, x_ref, wg_hbm, wu_hbm, wd_hbm, y_ref,
                   wg_buf, wu_buf, wd_buf, wgb_ref, wub_ref, wdb_ref, x2_ref, slot_ref, sems):
    j = pl.program_id(0)
    na = na_ref[0]

    def weight_copies(e, slot):
        return [pltpu.make_async_copy(src.at[e], buf.at[slot], sems.at[slot, n])
                for n, (src, buf) in enumerate(((wg_hbm, wg_buf), (wu_hbm, wu_buf), (wd_hbm, wd_buf)))]

    @pl.when(j == 0)
    def _():
        slot_ref[0] = 0
        for cp in weight_copies(be_ref[0], 0):
            cp.start()

    @pl.when(j < na)
    def _():
        cur = be_ref[j]
        prev = be_ref[jnp.maximum(j - 1, 0)]

        @pl.when((j == 0) | (cur != prev))
        def _():
            slot = slot_ref[0]
            nxt = lax.while_loop(lambda k: (k < na) & (be_ref[jnp.minimum(k, na - 1)] == cur),
                                 lambda k: k + 1, j + 1)

            @pl.when(nxt < na)
            def _():
                for cp in weight_copies(be_ref[jnp.minimum(nxt, na - 1)], 1 - slot):
                    cp.start()

            for cp in weight_copies(cur, slot):
                cp.wait()
            wgb_ref[...] = wg_buf[slot].astype(BF16)
            wub_ref[...] = wu_buf[slot].astype(BF16)
            wdb_ref[...] = wd_buf[slot].astype(BF16)
            slot_ref[0] = 1 - slot

        x2_ref[...] = x_ref[...].reshape(x2_ref.shape)
        lo, hi = _unpack_halves(x2_ref[...])
        lo, hi = lo.astype(BF16), hi.astype(BF16)
        half = lo.shape[1]
        hg = _dot(lo, wgb_ref[:half, :]) + _dot(hi, wgb_ref[half:, :])
        hu = _dot(lo, wub_ref[:half, :]) + _dot(hi, wub_ref[half:, :])
        act = (hg * jax.nn.sigmoid(hg)) * hu
        y = _dot(act.astype(BF16), wdb_ref[...])
        y_ref[...] = _pack_halves(y).reshape(y_ref.shape)

    @pl.when(j >= na)
    def _():
        y_ref[...] = jnp.zeros_like(y_ref)


def _experts(blk_exp, nact, xs, w_gate, w_up, w_down, n_blocks):
    DP = xs.shape[2]
    D = 2 * DP
    rows = n_blocks * MOE_BLOCK
    DE = w_gate.shape[2]
    blk = lambda j, be, na: (jnp.minimum(j, na[0] - 1), 0, 0)
    hbm = pl.BlockSpec(memory_space=pl.ANY)
    return pl.pallas_call(
        _expert_kernel,
        grid_spec=pltpu.PrefetchScalarGridSpec(
            num_scalar_prefetch=2, grid=(n_blocks,),
            in_specs=[pl.BlockSpec((MOE_BLOCK, 1, DP), blk), hbm, hbm, hbm],
            out_specs=pl.BlockSpec((MOE_BLOCK, 1, DP), lambda j, be, na: (j, 0, 0)),
            scratch_shapes=[pltpu.VMEM((2, D, DE), F32), pltpu.VMEM((2, D, DE), F32), pltpu.VMEM((2, DE, D), F32),
                            pltpu.VMEM((D, DE), BF16), pltpu.VMEM((D, DE), BF16), pltpu.VMEM((DE, D), BF16),
                            pltpu.VMEM((MOE_BLOCK, DP), U32), pltpu.SMEM((1,), I32),
                            pltpu.SemaphoreType.DMA((2, 3))]),
        out_shape=jax.ShapeDtypeStruct((rows, 1, DP), U32),
        compiler_params=_cparams("arbitrary"),
        name="moe_experts",
    )(blk_exp, nact, xs, w_gate, w_up, w_down)


def _combine_kernel(dest_ref, ys_ref, h_ref, rec_ref, g_ref, o_ref, idx_ref, ya0_ref, ya1_ref, yb0_ref, yb1_ref,
                    y2_ref, sem_a, sem_b, isem, *, normalize, n_steps):
    i = pl.program_id(0)
    tb = h_ref.shape[0]
    hb = tb // 2
    halves = ((ya0_ref, ya1_ref, sem_a), (yb0_ref, yb1_ref, sem_b))

    def idx_copy(step, slot):
        return pltpu.make_async_copy(dest_ref.at[step], idx_ref.at[slot], isem)

    def issue(half, slot):
        bufs, sem = halves[half][:2], halves[half][2]

        def body(t, c):
            for s in range(2):
                d = idx_ref[slot, 2 * (half * hb + t) + s]
                pltpu.make_async_copy(ys_ref.at[pl.ds(d, 1)], bufs[s].at[pl.ds(t, 1)], sem).start()
            return c

        lax.fori_loop(0, hb, body, 0, unroll=8)

    @pl.when(i == 0)
    def _():
        first = idx_copy(0, 0)
        first.start()
        first.wait()
        issue(0, 0)
        issue(1, 0)
        if n_steps > 1:
            idx_copy(1, 1).start()

    nslot = (i + 1) % 2

    @pl.when(i + 1 < n_steps)
    def _():
        idx_copy(i + 1, nslot).wait()

    rec = rec_ref[...]
    for half, (y0_ref, y1_ref, sem) in enumerate(halves):
        rows = slice(half * hb, (half + 1) * hb)
        pltpu.make_async_copy(ys_ref.at[pl.ds(0, hb)], y0_ref, sem).wait()
        pltpu.make_async_copy(ys_ref.at[pl.ds(0, hb)], y1_ref, sem).wait()
        y2_ref[...] = y0_ref[...].reshape(y2_ref.shape)
        h = h_ref[rows, :] + rec[rows, 2:3] * jnp.concatenate(_unpack_halves(y2_ref[...]), axis=1)
        y2_ref[...] = y1_ref[...].reshape(y2_ref.shape)
        h = h + rec[rows, 3:4] * jnp.concatenate(_unpack_halves(y2_ref[...]), axis=1)
        if normalize:
            ms = jnp.mean(h * h, axis=-1, keepdims=True)
            h = h * lax.rsqrt(ms + RMS_EPS) * g_ref[...]
        o_ref[rows, :] = h

        @pl.when(i + 1 < n_steps)
        def _():
            issue(half, nslot)

    @pl.when(i + 2 < n_steps)
    def _():
        idx_copy(i + 2, i % 2).start()


def _combine(dest2, ys, h, rec, g, tb, normalize):
    T, D = h.shape
    hb = tb // 2
    return pl.pallas_call(
        functools.partial(_combine_kernel, normalize=normalize, n_steps=T // tb),
        grid=(T // tb,),
        in_specs=[pl.BlockSpec(memory_space=pl.ANY), pl.BlockSpec(memory_space=pl.ANY),
                  pl.BlockSpec((tb, D), lambda i: (i, 0)), pl.BlockSpec((tb, LANES), lambda i: (i, 0)),
                  pl.BlockSpec((1, D), lambda i: (0, 0))],
        out_specs=pl.BlockSpec((tb, D), lambda i: (i, 0)),
        out_shape=jax.ShapeDtypeStruct((T, D), F32),
        scratch_shapes=[pltpu.SMEM((2, 2 * tb), I32)] + [pltpu.VMEM((hb, 1, D // 2), U32)] * 4 +
                       [pltpu.VMEM((hb, D // 2), U32)] + [pltpu.SemaphoreType.DMA] * 3,
        compiler_params=_cparams("arbitrary"),
        name="moe_combine",
    )(dest2, ys, h, rec, g)


def _pad_cols(a, n):
    return jnp.pad(a, ((0, 0), (0, n - a.shape[1])))


def _pad_rows(a, n, at=0):
    return jnp.pad(a, ((at, n - a.shape[0] - at), (0, 0)))


def _layer(x2, B, L, p):
    T, D = x2.shape
    W = p["s5_d"].shape[0]
    G, P = p["s5_lambda_re"].shape
    HG = W // G
    dl, al, gl = p["rwkv_w2"].shape[0], p["rwkv_a2"].shape[0], p["rwkv_g2"].shape[0]
    H = W // RW_N

    w_in = p["w_in"]
    o = W
    cols = [w_in[:, :W], w_in[:, o:o + 3 * W]]
    o += 3 * W
    cols += [_pad_cols(w_in[:, o:o + dl], 128), _pad_cols(w_in[:, o + dl:o + dl + al], 128),
             _pad_cols(w_in[:, o + dl + al:o + dl + al + gl], 256)]
    w_in_r = jnp.concatenate(cols, axis=1).astype(BF16)
    tm_in = min(1024, T)
    proj = _inproj(x2, p["norm_mix_g"].reshape(1, D), w_in_r, tm_in, 1152)

    C = S5_CHUNK
    nc = L // C
    ct_re = jnp.swapaxes(p["s5_c_re"], 1, 2)
    ct_im = jnp.swapaxes(p["s5_c_im"], 1, 2)
    m_op, wbt, wct, a64 = _s5_ops(p["s5_lambda_re"], p["s5_lambda_im"], p["s5_log_step"],
                                  ct_re, ct_im, p["s5_b_re"], p["s5_b_im"], p["s5_c_re"], p["s5_c_im"])
    rb = min(LANES, B * nc)
    z = _s5_in(proj.reshape(B * nc, C, proj.shape[1]), G, rb)
    y_t = _s5_chunk(z, m_op, wbt, wct, a64, nc)
    y_ssm = _s5_out(y_t, rb).reshape(T, W)
    tm = min(512, T)
    s5_out = _s5_glu(y_ssm, proj, p["s5_d"].reshape(1, W), p["s5_w_glu"].astype(BF16),
                     p["s5_b_glu"].reshape(1, W), tm)

    mu = p["rwkv_mu"]
    mu3 = _pad_rows(mu[:3 * W].reshape(3, W), 8)
    o = 3 * W
    mul = jnp.concatenate([_pad_cols(mu[None, o:o + dl], 128), _pad_cols(mu[None, o + dl:o + dl + al], 128),
                           _pad_cols(mu[None, o + dl + al:], 256)], axis=1)
    w2p = _pad_rows(p["rwkv_w2"], 128).astype(BF16)
    a2p = _pad_rows(p["rwkv_a2"], 128).astype(BF16)
    g2p = _pad_rows(p["rwkv_g2"], 256).astype(BF16)
    head_of = jnp.arange(W, dtype=I32) // RW_N
    e_mat = (head_of[:, None] == jnp.arange(LANES, dtype=I32)[None, :]).astype(BF16)
    et_mat = e_mat.T
    row = lambda a: a.reshape(1, W)
    tm_rw = min(256, L)
    r, k, v, nk, bv, lw, g = _rwkv_prep(proj, B, L, mu3, mul, row(p["rwkv_w0"]), row(p["rwkv_a0"]),
                                        row(p["rwkv_k_k"]), row(p["rwkv_k_a"]), w2p, a2p, g2p,
                                        e_mat, et_mat, tm_rw)
    y_rw = _rwkv_chunk(r, k, v, nk, bv, lw, B, L)
    rw_out = _rwkv_post(y_rw, r, k, v, g, p["rwkv_r_k"].reshape(1, W), row(p["rwkv_ln_w"]),
                        row(p["rwkv_ln_b"]), e_mat, et_mat, tm)

    w_out = p["w_out"].astype(BF16)
    w_route = jnp.concatenate([p["w_route_exp"], p["w_route_grp"]], axis=1)
    w_route = _pad_cols(w_route, LANES)
    wr_hi = w_route.astype(BF16)
    wr_lo = (w_route - wr_hi.astype(F32)).astype(BF16)
    b_route = _pad_cols(jnp.concatenate([p["b_route_exp"], p["b_route_grp"]])[None, :], LANES)
    tm_o = min(256, T)
    h, hn, rec = _outproj(s5_out, rw_out, x2, w_out[:W], w_out[W:], p["norm_ffn_g"].reshape(1, D),
                          wr_hi, wr_lo, b_route, tm_o)

    tb = min(1024, T)
    rank, cnt = _rank(rec, tb)
    n_blocks = -(-2 * T // MOE_BLOCK) + N_EXPERTS
    nb_rows = -(-(n_blocks + 1) // 8) * 8
    dest, blk = _dest(rec, rank, cnt, tb, nb_rows)
    blk_exp = blk[:n_blocks, 0]
    nact = blk[0:1, 1]
    lastblk = blk[nb_rows - 1, :N_EXPERTS]
    ts = min(256, T)
    dest2 = dest[:, :2].reshape(T // ts, 2 * ts)

    xs = _scatter(lastblk, nact, dest2, hn, ts, n_blocks)
    ys = _experts(blk_exp, nact, xs, p["w_gate"], p["w_up"], p["w_down"], n_blocks)
    return dest2, ys, h, rec, ts


def kernel(x, norm_mix_g, w_in, s5_lambda_re, s5_lambda_im, s5_log_step, s5_b_re, s5_b_im, s5_c_re, s5_c_im, s5_d, s5_w_glu, s5_b_glu, rwkv_mu, rwkv_w0, rwkv_w2, rwkv_a0, rwkv_a2, rwkv_g2, rwkv_k_k, rwkv_k_a, rwkv_r_k, rwkv_ln_w, rwkv_ln_b, w_out, norm_ffn_g, w_route_grp, b_route_grp, w_route_exp, b_route_exp, w_gate, w_up, w_down, norm_final_g):
    B, L, D = x.shape
    params = dict(
        norm_mix_g=norm_mix_g, w_in=w_in, s5_lambda_re=s5_lambda_re, s5_lambda_im=s5_lambda_im,
        s5_log_step=s5_log_step, s5_b_re=s5_b_re, s5_b_im=s5_b_im, s5_c_re=s5_c_re, s5_c_im=s5_c_im,
        s5_d=s5_d, s5_w_glu=s5_w_glu, s5_b_glu=s5_b_glu, rwkv_mu=rwkv_mu, rwkv_w0=rwkv_w0, rwkv_w2=rwkv_w2,
        rwkv_a0=rwkv_a0, rwkv_a2=rwkv_a2, rwkv_g2=rwkv_g2, rwkv_k_k=rwkv_k_k, rwkv_k_a=rwkv_k_a,
        rwkv_r_k=rwkv_r_k, rwkv_ln_w=rwkv_ln_w, rwkv_ln_b=rwkv_ln_b, w_out=w_out, norm_ffn_g=norm_ffn_g,
        w_route_grp=w_route_grp, b_route_grp=b_route_grp, w_route_exp=w_route_exp, b_route_exp=b_route_exp,
        w_gate=w_gate, w_up=w_up, w_down=w_down)
    depth = norm_mix_g.shape[0]
    h2 = x.reshape(B * L, D)
    for l in range(depth):
        p = {k_: v_[l] for k_, v_ in params.items()}
        dest2, ys, h, rec, ts = _layer(h2, B, L, p)
        h2 = _combine(dest2, ys, h, rec, norm_final_g.reshape(1, D), ts, normalize=(l == depth - 1))
    return h2.reshape(B, L, D)
```

```python
import functools

import jax
import jax.numpy as jnp
from jax import lax
from jax.experimental import pallas as pl
from jax.experimental.pallas import tpu as pltpu

F32 = jnp.float32
BF16 = jnp.bfloat16
I32 = jnp.int32

RMS_EPS = 1e-6
S5_CHUNK = 64
S5_HG = 16
S5_P = 64
RW_N = 64
RW_CHUNK = 64
RW_GN_EPS = 64e-5
N_GROUPS = 8
EPG = 8
N_EXPERTS = 64
MOE_BLOCK = 256
LANES = 128
VMEM_LIMIT = 56 * 1024 * 1024


def _cparams(*sem):
    return pltpu.CompilerParams(dimension_semantics=sem, vmem_limit_bytes=VMEM_LIMIT)


def _split2(x):
    hi = x.astype(BF16)
    lo = (x - hi.astype(F32)).astype(BF16)
    return hi, lo


def _split3(x):
    hi = x.astype(BF16)
    r = x - hi.astype(F32)
    mid = r.astype(BF16)
    lo = (r - mid.astype(F32)).astype(BF16)
    return hi, mid, lo


def _dot(a, b):
    return jnp.dot(a, b, preferred_element_type=F32)


def _dot_nt(a, b):
    return lax.dot_general(a, b, (((1,), (1,)), ((), ())), preferred_element_type=F32)


def _dot_tn(a, b):
    return lax.dot_general(a, b, (((0,), (0,)), ((), ())), preferred_element_type=F32)


def _dot_x3(a, b):
    ah, al = _split2(a)
    bh, bl = _split2(b)
    return _dot(ah, bh) + (_dot(ah, bl) + _dot(al, bh))


def _dot_exact_lhs(a_bf16, b):
    bh, bm, bl = _split3(b)
    return _dot(a_bf16, bh) + (_dot(a_bf16, bm) + _dot(a_bf16, bl))


def _cmul(ar, ai, br, bi):
    return ar * br - ai * bi, ar * bi + ai * br


U32 = jnp.uint32


def _pack_halves(x):
    half = x.shape[1] // 2
    lo = lax.bitcast_convert_type(x[:, :half].astype(BF16).astype(F32), U32)
    hi = lax.bitcast_convert_type(x[:, half:].astype(BF16).astype(F32), U32)
    return (lo >> 16) | hi


def _unpack_halves(p):
    lo = lax.bitcast_convert_type(p << 16, F32)
    hi = lax.bitcast_convert_type(p & jnp.uint32(0xFFFF0000), F32)
    return lo, hi


def _inproj_kernel(x_ref, g_ref, w_ref, o_ref, hn_ref):
    @pl.when(pl.program_id(1) == 0)
    def _():
        x = x_ref[...]
        ms = jnp.mean(x * x, axis=-1, keepdims=True)
        hn_ref[...] = (x * lax.rsqrt(ms + RMS_EPS) * g_ref[...]).astype(BF16)

    o_ref[...] = _dot(hn_ref[...], w_ref[...])


def _inproj(x2, g, w_bf16, tm, tn):
    T, D = x2.shape
    N = w_bf16.shape[1]
    return pl.pallas_call(
        _inproj_kernel,
        grid=(T // tm, N // tn),
        in_specs=[
            pl.BlockSpec((tm, D), lambda i, j: (i, 0)),
            pl.BlockSpec((1, D), lambda i, j: (0, 0)),
            pl.BlockSpec((D, tn), lambda i, j: (0, j)),
        ],
        out_specs=pl.BlockSpec((tm, tn), lambda i, j: (i, j)),
        out_shape=jax.ShapeDtypeStruct((T, N), F32),
        scratch_shapes=[pltpu.VMEM((tm, D), BF16)],
        compiler_params=_cparams("arbitrary", "arbitrary"),
        name="inproj",
    )(x2, g, w_bf16)


def _binpow(ar, ai, expo, nbits):
    pr = jnp.ones(expo.shape, F32)
    pi = jnp.zeros(expo.shape, F32)
    sr, si = ar, ai
    for bit in range(nbits):
        m = ((expo >> bit) & 1) == 1
        nr, ni = _cmul(pr, pi, sr, si)
        pr = jnp.where(m, nr, pr)
        pi = jnp.where(m, ni, pi)
        if bit + 1 < nbits:
            sr, si = _cmul(sr, si, sr, si)
    return pr, pi


def _frame_powers(ar, ai, reverse):
    C, HG, P = S5_CHUNK, S5_HG, ar.shape[0]
    fpt = LANES // HG
    j = lax.broadcasted_iota(I32, (P, LANES), 1) // HG
    inner_r, inner_i = _binpow(ar, ai, (fpt - 1 - j) if reverse else j, (fpt - 1).bit_length())
    sr, si = ar, ai
    for _ in range(fpt.bit_length() - 1):
        sr, si = _cmul(sr, si, sr, si)
    outer = [(jnp.ones_like(ar), jnp.zeros_like(ai))]
    for _ in range(C // fpt - 1):
        outer.append(_cmul(outer[-1][0], outer[-1][1], sr, si))
    if reverse:
        outer = outer[::-1]
    tiles = [_cmul(inner_r, inner_i, o_r, o_i) for o_r, o_i in outer]
    return (jnp.concatenate([t[0] for t in tiles], axis=1), jnp.concatenate([t[1] for t in tiles], axis=1))


def _zoh(lr, li, dt):
    mag = jnp.exp(lr * dt)
    ang = li * dt
    ar, ai = mag * jnp.cos(ang), mag * jnp.sin(ang)
    den = lr * lr + li * li
    nr, ni = ar - 1.0, ai
    fr = (nr * lr + ni * li) / den
    fi = (ni * lr - nr * li) / den
    return ar, ai, fr, fi


def _s5_ops_kernel(lrc_ref, lic_ref, dt_ref, ct_re_ref, ct_im_ref, bt_re_ref, bt_im_ref, c_re_ref, c_im_ref,
                   m_ref, wbt_ref, wct_ref, a64_ref):
    C, HG, P = S5_CHUNK, S5_HG, S5_P
    W = C * HG
    dt = jnp.exp(dt_ref[0])
    ar_c, ai_c, fr_c, fi_c = _zoh(lrc_ref[0], lic_ref[0], dt)

    pr, pi = _frame_powers(ar_c, ai_c, False)
    qr, qi = _frame_powers(ar_c, ai_c, True)

    sel = (lax.broadcasted_iota(I32, (HG, W), 1) % HG == lax.broadcasted_iota(I32, (HG, W), 0)).astype(BF16)
    tile = lambda ref: _dot_exact_lhs_rhs(ref[0], sel)

    ca_re, ca_im = _cmul(tile(ct_re_ref), tile(ct_im_ref), pr, pi)
    c1_re, c1_im = _cmul(ca_re, ca_im, ar_c, ai_c)
    wct_ref[0] = jnp.concatenate([c1_re, -c1_im], axis=0).T.astype(BF16)

    bb_re, bb_im = _cmul(tile(bt_re_ref), tile(bt_im_ref), fr_c, fi_c)
    ab_re, ab_im = _cmul(bb_re, bb_im, qr, qi)
    wbt_ref[0, :P, :] = ab_re.astype(BF16)
    wbt_ref[0, P:, :] = ab_im.astype(BF16)

    cr, ci = ar_c, ai_c
    for _ in range(C.bit_length() - 1):
        cr, ci = _cmul(cr, ci, cr, ci)
    a64_ref[0] = jnp.concatenate([cr, ci], axis=0)

    strip = _dot_x3(c_re_ref[0], ab_re) - _dot_x3(c_im_ref[0], ab_im)
    ext = jnp.concatenate([strip, jnp.zeros((HG, W), F32)], axis=1)
    for t in range(C):
        off = (C - 1 - t) * HG
        m_ref[0, t * HG:(t + 1) * HG, :] = ext[:, off:off + W].astype(BF16)


def _s5_ops(lam_re, lam_im, log_step, ct_re, ct_im, bt_re, bt_im, c_re, c_im):
    G, P = lam_re.shape
    C, HG = S5_CHUNK, S5_HG
    W = C * HG
    col = lambda a: a.reshape(G, P, 1)
    g3 = lambda s1, s2: pl.BlockSpec((1, s1, s2), lambda g: (g, 0, 0))
    return pl.pallas_call(
        _s5_ops_kernel,
        grid=(G,),
        in_specs=[g3(P, 1), g3(P, 1), g3(1, 1),
                  g3(P, HG), g3(P, HG), g3(P, HG), g3(P, HG), g3(HG, P), g3(HG, P)],
        out_specs=[g3(W, W), g3(2 * P, W), g3(W, 2 * P), g3(2 * P, 1)],
        out_shape=[jax.ShapeDtypeStruct((G, W, W), BF16),
                   jax.ShapeDtypeStruct((G, 2 * P, W), BF16),
                   jax.ShapeDtypeStruct((G, W, 2 * P), BF16),
                   jax.ShapeDtypeStruct((G, 2 * P, 1), F32)],
        compiler_params=_cparams("arbitrary"),
        name="s5_ops",
    )(col(lam_re), col(lam_im), log_step.reshape(G, 1, 1), ct_re, ct_im, bt_re, bt_im, c_re, c_im)


S5_FB = 8


def _s5_in_kernel(x_ref, z_ref):
    G, _, rb = z_ref.shape
    for f in range(S5_FB):
        xt = x_ref[:, f, :].T
        z_ref[:, f * S5_HG:(f + 1) * S5_HG, :] = xt.reshape(G, S5_HG, rb).astype(BF16)


def _s5_in(proj3, G, rb):
    R, C, _ = proj3.shape
    W = G * S5_HG
    return pl.pallas_call(
        _s5_in_kernel,
        grid=(R // rb, C // S5_FB),
        in_specs=[pl.BlockSpec((rb, S5_FB, W), lambda i, j: (i, j, 0))],
        out_specs=pl.BlockSpec((G, S5_FB * S5_HG, rb), lambda i, j: (0, j, i)),
        out_shape=jax.ShapeDtypeStruct((G, C * S5_HG, R), BF16),
        compiler_params=_cparams("arbitrary", "arbitrary"),
        name="s5_in",
    )(proj3)


def _s5_out_kernel(yt_ref, o_ref):
    G, _, rb = yt_ref.shape
    for f in range(S5_FB):
        slab = yt_ref[:, f * S5_HG:(f + 1) * S5_HG, :].reshape(G * S5_HG, rb)
        o_ref[:, f, :] = slab.T


def _s5_out(yt, rb):
    G, CW, R = yt.shape
    C = CW // S5_HG
    W = G * S5_HG
    return pl.pallas_call(
        _s5_out_kernel,
        grid=(R // rb, C // S5_FB),
        in_specs=[pl.BlockSpec((G, S5_FB * S5_HG, rb), lambda i, j: (0, j, i))],
        out_specs=pl.BlockSpec((rb, S5_FB, W), lambda i, j: (i, j, 0)),
        out_shape=jax.ShapeDtypeStruct((R, C, W), F32),
        compiler_params=_cparams("arbitrary", "arbitrary"),
        name="s5_out",
    )(yt)


def _s5_chunk_kernel(z_ref, m_ref, wbt_ref, wct_ref, a64_ref, y_ref, *, n_chunks):
    for g in range(z_ref.shape[0]):
        _s5_chunk_group(z_ref.at[g], m_ref.at[g], wbt_ref.at[g], wct_ref.at[g], a64_ref.at[g], y_ref.at[g],
                        n_chunks)


def _s5_chunk_group(z_ref, m_ref, wbt_ref, wct_ref, a64_ref, y_ref, n_chunks):
    P = S5_P
    z = z_ref[...]
    R = z.shape[1]
    x = _dot(wbt_ref[...], z)
    y_ref[...] = _dot(m_ref[...], z)
    xr, xi = x[:P], x[P:]
    a = a64_ref[...]
    ar, ai = a[:P], a[P:]
    cidx = lax.broadcasted_iota(I32, (1, R), 1) % n_chunks
    shift = 1
    while shift < n_chunks:
        keep = cidx >= shift
        sr = jnp.where(keep, pltpu.roll(xr, shift, 1), 0.0)
        si = jnp.where(keep, pltpu.roll(xi, shift, 1), 0.0)
        xr, xi = xr + (ar * sr - ai * si), xi + (ar * si + ai * sr)
        ar, ai = ar * ar - ai * ai, 2.0 * ar * ai
        shift *= 2
    keep = cidx >= 1
    s_in = jnp.concatenate([jnp.where(keep, pltpu.roll(xr, 1, 1), 0.0),
                            jnp.where(keep, pltpu.roll(xi, 1, 1), 0.0)], axis=0)
    sh, sl = _split2(s_in)
    wct = wct_ref[...]
    y_ref[...] = y_ref[...] + (_dot(wct, sh) + _dot(wct, sl))


def _s5_chunk(z, m, wbt, wct, a64, n_chunks):
    G, W, R = z.shape
    P2 = wbt.shape[1]
    gps = 2 if G % 2 == 0 else 1
    g3 = lambda s1, s2: pl.BlockSpec((gps, s1, s2), lambda g: (g, 0, 0))
    return pl.pallas_call(
        functools.partial(_s5_chunk_kernel, n_chunks=n_chunks),
        grid=(G // gps,),
        in_specs=[g3(W, R), g3(W, W), g3(P2, W), g3(W, P2), g3(P2, 1)],
        out_specs=g3(W, R),
        out_shape=jax.ShapeDtypeStruct((G, W, R), F32),
        compiler_params=_cparams("arbitrary"),
        name="s5_chunk",
    )(z, m, wbt, wct, a64)


def _gelu_tanh(x):
    return 0.5 * x * (1.0 + jnp.tanh(0.7978845608028654 * (x + 0.044715 * (x * x * x))))


def _s5_glu_kernel(y_ref, u_ref, d_ref, w_ref, b_ref, o_ref):
    y = y_ref[...] + d_ref[...] * u_ref[...]
    y = _gelu_tanh(y)
    z = _dot(y.astype(BF16), w_ref[...]) + b_ref[...]
    o_ref[...] = (y * jax.nn.sigmoid(z)).astype(BF16)


def _s5_glu(y_ssm, proj, d, w_bf16, b, tm):
    T, W = y_ssm.shape
    return pl.pallas_call(
        _s5_glu_kernel,
        grid=(T // tm,),
        in_specs=[
            pl.BlockSpec((tm, W), lambda i: (i, 0)),
            pl.BlockSpec((tm, W), lambda i: (i, 0)),
            pl.BlockSpec((1, W), lambda i: (0, 0)),
            pl.BlockSpec((W, W), lambda i: (0, 0)),
            pl.BlockSpec((1, W), lambda i: (0, 0)),
        ],
        out_specs=pl.BlockSpec((tm, W), lambda i: (i, 0)),
        out_shape=jax.ShapeDtypeStruct((T, W), BF16),
        compiler_params=_cparams("arbitrary"),
        name="s5_glu",
    )(y_ssm, proj, d, w_bf16, b)


def _head_sum(x, e_ref, et_ref):
    xh, xl = _split2(x)
    e = e_ref[...]
    s = _dot(xh, e) + _dot(xl, e)
    sh, sl = _split2(s)
    et = et_ref[...]
    return _dot(sh, et) + _dot(sl, et)


def _shift(z, prev_row):
    rolled = pltpu.roll(z, 1, 0)
    first = lax.broadcasted_iota(I32, (z.shape[0], 1), 0) == 0
    return jnp.where(first, prev_row, rolled)


def _rwkv_prep_kernel(zr_ref, zk_ref, zv_ref, zl_ref, mu_ref, mul_ref, w0_ref, a0_ref, kk_ref, ka_ref,
                      w2_ref, a2_ref, g2_ref, e_ref, et_ref,
                      r_ref, k_ref, v_ref, nk_ref, b_ref, lw_ref, g_ref,
                      car_ref, carl_ref):
    W = r_ref.shape[1]

    @pl.when(pl.program_id(1) == 0)
    def _():
        car_ref[...] = jnp.zeros_like(car_ref)
        carl_ref[...] = jnp.zeros_like(carl_ref)

    def lerp(z, prev_row, mu):
        return z + (_shift(z, prev_row) - z) * mu

    tm = zr_ref.shape[0]
    zr, zk, zv, zl = zr_ref[...], zk_ref[...], zv_ref[...], zl_ref[...]
    r = lerp(zr, car_ref[0:1, :], mu_ref[0:1, :])
    k = lerp(zk, car_ref[1:2, :], mu_ref[1:2, :])
    v = lerp(zv, car_ref[2:3, :], mu_ref[2:3, :])
    xl = lerp(zl, carl_ref[0:1, :], mul_ref[...])
    car_ref[0:1, :] = zr[tm - 1:tm, :]
    car_ref[1:2, :] = zk[tm - 1:tm, :]
    car_ref[2:3, :] = zv[tm - 1:tm, :]
    carl_ref[0:1, :] = zl[tm - 1:tm, :]

    xw, xa, xg = xl[:, 0:128], xl[:, 128:256], xl[:, 256:512]
    dw = _dot(jnp.tanh(xw).astype(BF16), w2_ref[...])
    da = _dot(xa.astype(BF16), a2_ref[...])
    g = _dot(jax.nn.sigmoid(xg).astype(BF16), g2_ref[...])

    zw = -(w0_ref[...] + dw)
    softplus = jnp.maximum(zw, 0.0) + jnp.log(1.0 + jnp.exp(-jnp.abs(zw)))
    w_log = -softplus - 0.5
    a = jax.nn.sigmoid(a0_ref[...] + da)

    kk = k * kk_ref[...]
    n2 = _head_sum(kk * kk, e_ref, et_ref)
    kk = kk / jnp.maximum(jnp.sqrt(n2), 1e-12)

    r_ref[...] = r.astype(BF16)
    k_ref[...] = (k * (1.0 + (a - 1.0) * ka_ref[...])).astype(BF16)
    v_ref[...] = v.astype(BF16)
    nk_ref[...] = kk.astype(BF16)
    b_ref[...] = (kk * a).astype(BF16)
    lw_ref[...] = -jnp.exp(w_log)
    g_ref[...] = g.astype(BF16)


def _rwkv_prep(proj, B, L, mu3, mul, w0, a0, k_k, k_a, w2p, a2p, g2p, e_mat, et_mat, tm):
    T = B * L
    W = w0.shape[1]
    nt = L // tm
    row = lambda c: pl.BlockSpec((tm, W), lambda b, i, c=c: (b * nt + i, c))
    full = lambda a: pl.BlockSpec(a.shape, lambda b, i: (0, 0))
    out = pl.BlockSpec((tm, W), lambda b, i: (b * nt + i, 0))
    lw = 512
    return pl.pallas_call(
        _rwkv_prep_kernel,
        grid=(B, nt),
        in_specs=[row(1), row(2), row(3),
                  pl.BlockSpec((tm, lw), lambda b, i: (b * nt + i, 4 * W // lw)),
                  full(mu3), full(mul), full(w0), full(a0), full(k_k), full(k_a),
                  full(w2p), full(a2p), full(g2p), full(e_mat), full(et_mat)],
        out_specs=[out] * 7,
        out_shape=[jax.ShapeDtypeStruct((T, W), dt) for dt in (BF16, BF16, BF16, BF16, BF16, F32, BF16)],
        scratch_shapes=[pltpu.VMEM((8, W), F32), pltpu.VMEM((8, lw), F32)],
        compiler_params=_cparams("arbitrary", "arbitrary"),
        name="rwkv_prep",
    )(proj, proj, proj, proj, mu3, mul, w0, a0, k_k, k_a, w2p, a2p, g2p, e_mat, et_mat)


def _rwkv_chunk_kernel(r_ref, k_ref, v_ref, nk_ref, b_ref, lw_ref, y_ref, z_ref):
    C, N = RW_CHUNK, RW_N
    H = r_ref.shape[1] // N

    @pl.when(pl.program_id(1) == 0)
    def _():
        z_ref[...] = jnp.zeros_like(z_ref)

    ri = lax.broadcasted_iota(I32, (C, C), 0)
    ci = lax.broadcasted_iota(I32, (C, C), 1)
    tril = ri >= ci
    stril = ri > ci
    eye = (ri == ci).astype(F32)

    lw = lw_ref[...]
    cs = _dot_exact_lhs(tril.astype(BF16), lw)
    tot = cs[C - 1:C, :]
    p_inc = jnp.exp(cs)
    p_inv = jnp.exp(-cs)
    p_prev = jnp.exp(cs - lw)
    p_rest = jnp.exp(tot - cs)
    p_tot = jnp.exp(tot)

    k_f, b_f = k_ref[...].astype(F32), b_ref[...].astype(F32)
    r_t = r_ref[...].astype(F32) * p_inc
    k_t = k_f * p_inv
    a_t = -nk_ref[...].astype(F32) * p_prev
    b_t = b_f * p_inv
    k_h = k_f * p_rest
    b_h = b_f * p_rest
    v_all = v_ref[...]

    hs = range(H)
    sl = [slice(h * N, (h + 1) * N) for h in hs]
    each = lambda f: [f(h) for h in hs]
    v = each(lambda h: v_all[:, sl[h]].astype(BF16))
    ar = each(lambda h: jnp.concatenate([a_t[:, sl[h]], r_t[:, sl[h]]], axis=0).astype(BF16))
    m_b = each(lambda h: _dot_nt(ar[h], b_t[:, sl[h]].astype(BF16)))
    m_k = each(lambda h: _dot_nt(ar[h], k_t[:, sl[h]].astype(BF16)))
    l_ab = each(lambda h: jnp.where(stril, m_b[h][:C], 0.0).astype(BF16))
    m_rb = each(lambda h: jnp.where(tril, m_b[h][C:], 0.0).astype(BF16))
    l_ak = each(lambda h: jnp.where(stril, m_k[h][:C], 0.0).astype(BF16))
    m_rk = each(lambda h: jnp.where(tril, m_k[h][C:], 0.0).astype(BF16))

    x = each(lambda h: jnp.concatenate([_dot(l_ak[h], v[h]), a_t[:, sl[h]]], axis=1))
    y0 = each(lambda h: _dot(m_rk[h], v[h]))
    hk = each(lambda h: _dot_tn(k_h[:, sl[h]].astype(BF16), v[h]))
    lp = l_ab
    step = 1
    while step < C:
        x = each(lambda h: x[h] + _dot(lp[h], x[h].astype(BF16)))
        step *= 2
        if step < C:
            lp = each(lambda h: _dot(lp[h], lp[h]).astype(BF16))
    xb = each(lambda h: x[h].astype(BF16))
    yq = each(lambda h: _dot(m_rb[h], xb[h]))
    gh = each(lambda h: _dot_tn(b_h[:, sl[h]].astype(BF16), xb[h]))

    z = each(lambda h: _split2(z_ref[h]))
    qb = each(lambda h: (yq[h][:, N:] + r_t[:, sl[h]]).astype(BF16))
    gb = each(lambda h: (gh[h][:, N:] + eye * p_tot[:, sl[h]]).astype(BF16))
    yz = each(lambda h: _dot(qb[h], z[h][0]) + _dot(qb[h], z[h][1]))
    gz = each(lambda h: _dot(gb[h], z[h][0]) + _dot(gb[h], z[h][1]))
    for h in hs:
        y_ref[:, sl[h]] = (yq[h][:, :N] + y0[h]) + yz[h]
        z_ref[h] = (gh[h][:, :N] + hk[h]) + gz[h]


def _rwkv_chunk(r, k, v, nk, b, lw, B, L):
    T, W = r.shape
    C = RW_CHUNK
    nc = L // C
    H = W // RW_N
    spec = pl.BlockSpec((C, W), lambda bi, c: (bi * nc + c, 0))
    return pl.pallas_call(
        _rwkv_chunk_kernel,
        grid=(B, nc),
        in_specs=[spec] * 6,
        out_specs=spec,
        out_shape=jax.ShapeDtypeStruct((T, W), F32),
        scratch_shapes=[pltpu.VMEM((H, RW_N, RW_N), F32)],
        compiler_params=_cparams("arbitrary", "arbitrary"),
        name="rwkv_chunk",
    )(r, k, v, nk, b, lw)


def _rwkv_post_kernel(y_ref, r_ref, k_ref, v_ref, g_ref, rk_ref, lnw_ref, lnb_ref, e_ref, et_ref, o_ref):
    inv_n = 1.0 / RW_N
    y = y_ref[...]
    mean = _head_sum(y, e_ref, et_ref) * inv_n
    yc = y - mean
    var = _head_sum(yc * yc, e_ref, et_ref) * inv_n
    yn = yc * lax.rsqrt(var + RW_GN_EPS) * lnw_ref[...] + lnb_ref[...]
    r, k, v = r_ref[...].astype(F32), k_ref[...].astype(F32), v_ref[...].astype(F32)
    bonus = _head_sum(r * k * rk_ref[...], e_ref, et_ref) * v
    o_ref[...] = ((yn + bonus) * g_ref[...].astype(F32)).astype(BF16)


def _rwkv_post(y, r, k, v, g, r_k, ln_w, ln_b, e_mat, et_mat, tm):
    T, W = y.shape
    row = pl.BlockSpec((tm, W), lambda i: (i, 0))
    full = lambda a: pl.BlockSpec(a.shape, lambda i: (0, 0))
    return pl.pallas_call(
        _rwkv_post_kernel,
        grid=(T // tm,),
        in_specs=[row] * 5 + [full(r_k), full(ln_w), full(ln_b), full(e_mat), full(et_mat)],
        out_specs=row,
        out_shape=jax.ShapeDtypeStruct((T, W), BF16),
        compiler_params=_cparams("arbitrary"),
        name="rwkv_post",
    )(y, r, k, v, g, r_k, ln_w, ln_b, e_mat, et_mat)


def _first_index_of_max(vals, lane, valid):
    neg = jnp.float32(-jnp.inf)
    masked = jnp.where(valid, vals, neg)
    m = jnp.max(masked, axis=-1, keepdims=True)
    idx = jnp.min(jnp.where(valid & (masked == m), lane, LANES), axis=-1, keepdims=True)
    return m, idx


def _outproj_kernel(s5_ref, rw_ref, x_ref, wt_ref, wb_ref, g_ref, wrh_ref, wrl_ref, br_ref,
                    h_ref, hn_ref, rec_ref):
    h = x_ref[...] + (_dot(s5_ref[...], wt_ref[...]) + _dot(rw_ref[...], wb_ref[...]))
    h_ref[...] = h
    ms = jnp.mean(h * h, axis=-1, keepdims=True)
    hn = h * lax.rsqrt(ms + RMS_EPS) * g_ref[...]
    hn_ref[...] = _pack_halves(hn).reshape(hn_ref.shape)

    hh, hl = _split2(hn)
    wh, wl = wrh_ref[...], wrl_ref[...]
    logits = _dot(hh, wh) + (_dot(hh, wl) + _dot(hl, wh)) + br_ref[...]
    lane = lax.broadcasted_iota(I32, logits.shape, 1)
    is_grp = (lane >= N_EXPERTS) & (lane < N_EXPERTS + N_GROUPS)
    gmax, gidx = _first_index_of_max(logits, lane, is_grp)
    gsum = jnp.sum(jnp.where(is_grp, jnp.exp(logits - gmax), 0.0), axis=-1, keepdims=True)
    p_grp = 1.0 / gsum
    grp = gidx - N_EXPERTS
    in_grp = (lane >= grp * EPG) & (lane < (grp + 1) * EPG)
    m1, i1 = _first_index_of_max(logits, lane, in_grp)
    m2, i2 = _first_index_of_max(logits, lane, in_grp & (lane != i1))
    e = jnp.exp(m2 - m1)
    g1 = p_grp / (1.0 + e)
    g2 = p_grp * e / (1.0 + e)
    rec = jnp.where(lane == 0, i1.astype(F32),
          jnp.where(lane == 1, i2.astype(F32),
          jnp.where(lane == 2, g1, jnp.where(lane == 3, g2, 0.0))))
    rec_ref[...] = rec


def _outproj(s5o, rwo, x2, w_top, w_bot, g, wr_hi, wr_lo, b_route, tm):
    T, D = x2.shape
    W = s5o.shape[1]
    full = lambda a: pl.BlockSpec(a.shape, lambda i: (0, 0))
    return pl.pallas_call(
        _outproj_kernel,
        grid=(T // tm,),
        in_specs=[pl.BlockSpec((tm, W), lambda i: (i, 0)), pl.BlockSpec((tm, W), lambda i: (i, 0)),
                  pl.BlockSpec((tm, D), lambda i: (i, 0)),
                  full(w_top), full(w_bot), full(g), full(wr_hi), full(wr_lo), full(b_route)],
        out_specs=[pl.BlockSpec((tm, D), lambda i: (i, 0)), pl.BlockSpec((tm, 1, D // 2), lambda i: (i, 0, 0)),
                   pl.BlockSpec((tm, LANES), lambda i: (i, 0))],
        out_shape=[jax.ShapeDtypeStruct((T, D), F32), jax.ShapeDtypeStruct((T, 1, D // 2), U32),
                   jax.ShapeDtypeStruct((T, LANES), F32)],
        compiler_params=_cparams("arbitrary"),
        name="outproj_route",
    )(s5o, rwo, x2, w_top, w_bot, g, wr_hi, wr_lo, b_route)


def _onehots(rec, lane):
    oh0 = (lane == rec[:, 0:1].astype(I32)).astype(F32)
    oh1 = (lane == rec[:, 1:2].astype(I32)).astype(F32)
    return oh0, oh1


def _rank_kernel(rec_ref, rank_ref, cnt_ref, base_ref):
    tb = rec_ref.shape[0]

    @pl.when(pl.program_id(0) == 0)
    def _():
        base_ref[...] = jnp.zeros_like(base_ref)

    lane = lax.broadcasted_iota(I32, (tb, LANES), 1)
    oh0, oh1 = _onehots(rec_ref[...], lane)
    both = oh0 + oh1
    ri = lax.broadcasted_iota(I32, (tb, tb), 0)
    ci = lax.broadcasted_iota(I32, (tb, tb), 1)
    before = _dot((ri > ci).astype(BF16), both.astype(BF16)) + base_ref[0:1, :]
    rank0 = jnp.sum(oh0 * before, axis=-1, keepdims=True)
    rank1 = jnp.sum(oh1 * before, axis=-1, keepdims=True)
    rank_ref[...] = jnp.where(lane == 0, rank0, jnp.where(lane == 1, rank1, 0.0))
    total = base_ref[0:1, :] + jnp.sum(both, axis=0, keepdims=True)
    base_ref[0:1, :] = total
    cnt_ref[...] = jnp.broadcast_to(total, cnt_ref.shape)


def _rank(rec, tb):
    T = rec.shape[0]
    return pl.pallas_call(
        _rank_kernel,
        grid=(T // tb,),
        in_specs=[pl.BlockSpec((tb, LANES), lambda i: (i, 0))],
        out_specs=[pl.BlockSpec((tb, LANES), lambda i: (i, 0)), pl.BlockSpec((8, LANES), lambda i: (0, 0))],
        out_shape=[jax.ShapeDtypeStruct((T, LANES), F32), jax.ShapeDtypeStruct((8, LANES), F32)],
        scratch_shapes=[pltpu.VMEM((8, LANES), F32)],
        compiler_params=_cparams("arbitrary"),
        name="moe_rank",
    )(rec)


def _padded_starts(cnt):
    padded = jnp.ceil(cnt * (1.0 / MOE_BLOCK)) * MOE_BLOCK
    ri = lax.broadcasted_iota(I32, (LANES, LANES), 0)
    ci = lax.broadcasted_iota(I32, (LANES, LANES), 1)
    p8 = jnp.broadcast_to(padded, (8, LANES))
    pend = _dot_exact_lhs_rhs(p8, (ri <= ci).astype(BF16))[0:1, :]
    return pend - padded, pend


def _dot_exact_lhs_rhs(a, b_bf16):
    ah, am, al = _split3(a)
    return _dot(ah, b_bf16) + (_dot(am, b_bf16) + _dot(al, b_bf16))


def _dest_kernel(rec_ref, rank_ref, cnt_ref, dest_ref, blk_ref):
    tb = rec_ref.shape[0]
    cnt = cnt_ref[0:1, :]
    pstart, pend = _padded_starts(cnt)
    lane = lax.broadcasted_iota(I32, (tb, LANES), 1)
    oh0, oh1 = _onehots(rec_ref[...], lane)
    rank = rank_ref[...]
    d0 = jnp.sum(oh0 * pstart, axis=-1, keepdims=True) + rank[:, 0:1]
    d1 = jnp.sum(oh1 * pstart, axis=-1, keepdims=True) + rank[:, 1:2]
    dest_ref[...] = jnp.where(lane == 0, d0, jnp.where(lane == 1, d1, 0.0)).astype(I32)

    @pl.when(pl.program_id(0) == 0)
    def _():
        nb = blk_ref.shape[0]
        blane = lax.broadcasted_iota(I32, (nb, LANES), 1)
        bstart = (lax.broadcasted_iota(I32, (nb, 1), 0) * MOE_BLOCK).astype(F32)
        is_e = blane < N_EXPERTS
        bexp = jnp.sum(jnp.where(is_e & (pend <= bstart), 1.0, 0.0), axis=-1, keepdims=True)
        bexp = jnp.minimum(bexp, N_EXPERTS - 1.0)
        nact = jnp.max(jnp.where(is_e, pend, 0.0), axis=-1, keepdims=True) * (1.0 / MOE_BLOCK)
        lastblk = jnp.where(is_e & (pend > pstart), pend - MOE_BLOCK, -1.0)
        brow = lax.broadcasted_iota(I32, (nb, 1), 0)
        out = jnp.where(blane == 0, bexp, jnp.where(blane == 1, nact, 0.0))
        out = jnp.where(brow == nb - 1, lastblk, out)
        blk_ref[...] = out.astype(I32)


def _dest(rec, rank, cnt, tb, nb_rows):
    T = rec.shape[0]
    return pl.pallas_call(
        _dest_kernel,
        grid=(T // tb,),
        in_specs=[pl.BlockSpec((tb, LANES), lambda i: (i, 0)), pl.BlockSpec((tb, LANES), lambda i: (i, 0)),
                  pl.BlockSpec((8, LANES), lambda i: (0, 0))],
        out_specs=[pl.BlockSpec((tb, LANES), lambda i: (i, 0)), pl.BlockSpec((nb_rows, LANES), lambda i: (0, 0))],
        out_shape=[jax.ShapeDtypeStruct((T, LANES), I32), jax.ShapeDtypeStruct((nb_rows, LANES), I32)],
        compiler_params=_cparams("arbitrary"),
        name="moe_dest",
    )(rec, rank, cnt)


def _scatter_kernel(pad_ref, na_ref, dest_ref, hn_ref, xs_ref, zbuf_ref, idx_ref, sa_ref, sb_ref,
                    sem_a, sem_b, isem, zsem, *, tb, n_blocks, n_tail, n_steps):
    i = pl.program_id(0)

    @pl.when(i == 0)
    def _():
        zbuf_ref[...] = jnp.zeros_like(zbuf_ref)

        def zero_rows(start):
            return pltpu.make_async_copy(zbuf_ref, xs_ref.at[pl.ds(start, MOE_BLOCK)], zsem)

        def pad_start(e, c):
            @pl.when(pad_ref[e] >= 0)
            def _():
                zero_rows(pad_ref[e]).start()
            return c

        def pad_wait(e, c):
            @pl.when(pad_ref[e] >= 0)
            def _():
                zero_rows(pad_ref[e]).wait()
            return c

        def tail_start(k, c):
            @pl.when(na_ref[0] + k < n_blocks)
            def _():
                zero_rows((na_ref[0] + k) * MOE_BLOCK).start()
            return c

        def tail_wait(k, c):
            @pl.when(na_ref[0] + k < n_blocks)
            def _():
                zero_rows((na_ref[0] + k) * MOE_BLOCK).wait()
            return c

        lax.fori_loop(0, N_EXPERTS, pad_start, 0)
        lax.fori_loop(0, n_tail, tail_start, 0)
        lax.fori_loop(0, N_EXPERTS, pad_wait, 0)
        lax.fori_loop(0, n_tail, tail_wait, 0)

    def idx_copy(step, slot):
        return pltpu.make_async_copy(dest_ref.at[step], idx_ref.at[slot], isem)

    @pl.when(i == 0)
    def _():
        idx_copy(0, 0).start()

    slot = i % 2
    idx_copy(i, slot).wait()

    @pl.when(i + 1 < n_steps)
    def _():
        idx_copy(i + 1, 1 - slot).start()

    hb = tb // 2
    for half, (buf, sem) in enumerate(((sa_ref, sem_a), (sb_ref, sem_b))):
        def wait_rows(buf=buf, sem=sem):
            for _ in range(2):
                pltpu.make_async_copy(buf, xs_ref.at[pl.ds(0, hb)], sem).wait()

        @pl.when(i > 0)
        def _():
            wait_rows()

        buf[...] = hn_ref[half * hb:(half + 1) * hb]

        def issue(t, c, buf=buf, sem=sem, half=half):
            for s in range(2):
                d = idx_ref[slot, 2 * (half * hb + t) + s]
                pltpu.make_async_copy(buf.at[pl.ds(t, 1)], xs_ref.at[pl.ds(d, 1)], sem).start()
            return c

        lax.fori_loop(0, hb, issue, 0, unroll=8)

    @pl.when(i == n_steps - 1)
    def _():
        for buf, sem in ((sa_ref, sem_a), (sb_ref, sem_b)):
            for _ in range(2):
                pltpu.make_async_copy(buf, xs_ref.at[pl.ds(0, hb)], sem).wait()


def _scatter(lastblk, nact, dest2, hn, tb, n_blocks):
    T, _, D = hn.shape
    n_tail = n_blocks - (-(-2 * T // MOE_BLOCK))
    hb = tb // 2
    return pl.pallas_call(
        functools.partial(_scatter_kernel, tb=tb, n_blocks=n_blocks, n_tail=n_tail, n_steps=T // tb),
        grid_spec=pltpu.PrefetchScalarGridSpec(
            num_scalar_prefetch=2, grid=(T // tb,),
            in_specs=[pl.BlockSpec(memory_space=pl.ANY),
                      pl.BlockSpec((tb, 1, D), lambda i, pad, na: (i, 0, 0))],
            out_specs=pl.BlockSpec(memory_space=pl.ANY),
            scratch_shapes=[pltpu.VMEM((MOE_BLOCK, 1, D), hn.dtype), pltpu.SMEM((2, 2 * tb), I32),
                            pltpu.VMEM((hb, 1, D), hn.dtype), pltpu.VMEM((hb, 1, D), hn.dtype),
                            pltpu.SemaphoreType.DMA, pltpu.SemaphoreType.DMA,
                            pltpu.SemaphoreType.DMA, pltpu.SemaphoreType.DMA]),
        out_shape=jax.ShapeDtypeStruct((n_blocks * MOE_BLOCK, 1, D), hn.dtype),
        compiler_params=_cparams("arbitrary"),
        name="moe_scatter",
    )(lastblk, nact, dest2, hn)


def _expert_kernel(be_ref, na_ref, x_ref, wg_hbm, wu_hbm, wd_hbm, y_ref,
                   wg_buf, wu_buf, wd_buf, wgb_ref, wub_ref, wdb_ref, x2_ref, slot_ref, sems):
    j = pl.program_id(0)
    na = na_ref[0]

    def weight_copies(e, slot):
        return [pltpu.make_async_copy(src.at[e], buf.at[slot], sems.at[slot, n])
                for n, (src, buf) in enumerate(((wg_hbm, wg_buf), (wu_hbm, wu_buf), (wd_hbm, wd_buf)))]

    @pl.when(j == 0)
    def _():
        slot_ref[0] = 0
        for cp in weight_copies(be_ref[0], 0):
            cp.start()

    @pl.when(j < na)
    def _():
        cur = be_ref[j]
        prev = be_ref[jnp.maximum(j - 1, 0)]

        @pl.when((j == 0) | (cur != prev))
        def _():
            slot = slot_ref[0]
            nxt = lax.while_loop(lambda k: (k < na) & (be_ref[jnp.minimum(k, na - 1)] == cur),
                                 lambda k: k + 1, j + 1)

            @pl.when(nxt < na)
            def _():
                for cp in weight_copies(be_ref[jnp.minimum(nxt, na - 1)], 1 - slot):
                    cp.start()

            for cp in weight_copies(cur, slot):
                cp.wait()
            wgb_ref[...] = wg_buf[slot].astype(BF16)
            wub_ref[...] = wu_buf[slot].astype(BF16)
            wdb_ref[...] = wd_buf[slot].astype(BF16)
            slot_ref[0] = 1 - slot

        x2_ref[...] = x_ref[...].reshape(x2_ref.shape)
        lo, hi = _unpack_halves(x2_ref[...])
        lo, hi = lo.astype(BF16), hi.astype(BF16)
        half = lo.shape[1]
        hg = _dot(lo, wgb_ref[:half, :]) + _dot(hi, wgb_ref[half:, :])
        hu = _dot(lo, wub_ref[:half, :]) + _dot(hi, wub_ref[half:, :])
        act = (hg * jax.nn.sigmoid(hg)) * hu
        y = _dot(act.astype(BF16), wdb_ref[...])
        y_ref[...] = _pack_halves(y).reshape(y_ref.shape)

    @pl.when(j >= na)
    def _():
        y_ref[...] = jnp.zeros_like(y_ref)


def _experts(blk_exp, nact, xs, w_gate, w_up, w_down, n_blocks):
    DP = xs.shape[2]
    D = 2 * DP
    rows = n_blocks * MOE_BLOCK
    DE = w_gate.shape[2]
    blk = lambda j, be, na: (jnp.minimum(j, na[0] - 1), 0, 0)
    hbm = pl.BlockSpec(memory_space=pl.ANY)
    return pl.pallas_call(
        _expert_kernel,
        grid_spec=pltpu.PrefetchScalarGridSpec(
            num_scalar_prefetch=2, grid=(n_blocks,),
            in_specs=[pl.BlockSpec((MOE_BLOCK, 1, DP), blk), hbm, hbm, hbm],
            out_specs=pl.BlockSpec((MOE_BLOCK, 1, DP), lambda j, be, na: (j, 0, 0)),
            scratch_shapes=[pltpu.VMEM((2, D, DE), F32), pltpu.VMEM((2, D, DE), F32), pltpu.VMEM((2, DE, D), F32),
                            pltpu.VMEM((D, DE), BF16), pltpu.VMEM((D, DE), BF16), pltpu.VMEM((DE, D), BF16),
                            pltpu.VMEM((MOE_BLOCK, DP), U32), pltpu.SMEM((1,), I32),
                            pltpu.SemaphoreType.DMA((2, 3))]),
        out_shape=jax.ShapeDtypeStruct((rows, 1, DP), U32),
        compiler_params=_cparams("arbitrary"),
        name="moe_experts",
    )(blk_exp, nact, xs, w_gate, w_up, w_down)


def _combine_kernel(dest_ref, ys_ref, h_ref, rec_ref, g_ref, o_ref, idx_ref, ya0_ref, ya1_ref, yb0_ref, yb1_ref,
                    y2_ref, sem_a, sem_b, isem, *, normalize, n_steps):
    i = pl.program_id(0)
    tb = h_ref.shape[0]
    hb = tb // 2
    halves = ((ya0_ref, ya1_ref, sem_a), (yb0_ref, yb1_ref, sem_b))

    def idx_copy(step, slot):
        return pltpu.make_async_copy(dest_ref.at[step], idx_ref.at[slot], isem)

    def issue(half, slot):
        bufs, sem = halves[half][:2], halves[half][2]

        def body(t, c):
            for s in range(2):
                d = idx_ref[slot, 2 * (half * hb + t) + s]
                pltpu.make_async_copy(ys_ref.at[pl.ds(d, 1)], bufs[s].at[pl.ds(t, 1)], sem).start()
            return c

        lax.fori_loop(0, hb, body, 0, unroll=8)

    @pl.when(i == 0)
    def _():
        first = idx_copy(0, 0)
        first.start()
        first.wait()
        issue(0, 0)
        issue(1, 0)
        if n_steps > 1:
            idx_copy(1, 1).start()

    nslot = (i + 1) % 2

    @pl.when(i + 1 < n_steps)
    def _():
        idx_copy(i + 1, nslot).wait()

    rec = rec_ref[...]
    for half, (y0_ref, y1_ref, sem) in enumerate(halves):
        rows = slice(half * hb, (half + 1) * hb)
        pltpu.make_async_copy(ys_ref.at[pl.ds(0, hb)], y0_ref, sem).wait()
        pltpu.make_async_copy(ys_ref.at[pl.ds(0, hb)], y1_ref, sem).wait()
        y2_ref[...] = y0_ref[...].reshape(y2_ref.shape)
        h = h_ref[rows, :] + rec[rows, 2:3] * jnp.concatenate(_unpack_halves(y2_ref[...]), axis=1)
        y2_ref[...] = y1_ref[...].reshape(y2_ref.shape)
        h = h + rec[rows, 3:4] * jnp.concatenate(_unpack_halves(y2_ref[...]), axis=1)
        if normalize:
            ms = jnp.mean(h * h, axis=-1, keepdims=True)
            h = h * lax.rsqrt(ms + RMS_EPS) * g_ref[...]
        o_ref[rows, :] = h

        @pl.when(i + 1 < n_steps)
        def _():
            issue(half, nslot)

    @pl.when(i + 2 < n_steps)
    def _():
        idx_copy(i + 2, i % 2).start()


def _combine(dest2, ys, h, rec, g, tb, normalize):
    T, D = h.shape
    hb = tb // 2
    return pl.pallas_call(
        functools.partial(_combine_kernel, normalize=normalize, n_steps=T // tb),
        grid=(T // tb,),
        in_specs=[pl.BlockSpec(memory_space=pl.ANY), pl.BlockSpec(memory_space=pl.ANY),
                  pl.BlockSpec((tb, D), lambda i: (i, 0)), pl.BlockSpec((tb, LANES), lambda i: (i, 0)),
                  pl.BlockSpec((1, D), lambda i: (0, 0))],
        out_specs=pl.BlockSpec((tb, D), lambda i: (i, 0)),
        out_shape=jax.ShapeDtypeStruct((T, D), F32),
        scratch_shapes=[pltpu.SMEM((2, 2 * tb), I32)] + [pltpu.VMEM((hb, 1, D // 2), U32)] * 4 +
                       [pltpu.VMEM((hb, D // 2), U32)] + [pltpu.SemaphoreType.DMA] * 3,
        compiler_params=_cparams("arbitrary"),
        name="moe_combine",
    )(dest2, ys, h, rec, g)


def _pad_cols(a, n):
    return jnp.pad(a, ((0, 0), (0, n - a.shape[1])))


def _pad_rows(a, n, at=0):
    return jnp.pad(a, ((at, n - a.shape[0] - at), (0, 0)))


def _layer(x2, B, L, p):
    T, D = x2.shape
    W = p["s5_d"].shape[0]
    G, P = p["s5_lambda_re"].shape
    HG = W // G
    dl, al, gl = p["rwkv_w2"].shape[0], p["rwkv_a2"].shape[0], p["rwkv_g2"].shape[0]
    H = W // RW_N

    w_in = p["w_in"]
    o = W
    cols = [w_in[:, :W], w_in[:, o:o + 3 * W]]
    o += 3 * W
    cols += [_pad_cols(w_in[:, o:o + dl], 128), _pad_cols(w_in[:, o + dl:o + dl + al], 128),
             _pad_cols(w_in[:, o + dl + al:o + dl + al + gl], 256)]
    w_in_r = jnp.concatenate(cols, axis=1).astype(BF16)
    tm_in = min(1024, T)
    proj = _inproj(x2, p["norm_mix_g"].reshape(1, D), w_in_r, tm_in, 1152)

    C = S5_CHUNK
    nc = L // C
    ct_re = jnp.swapaxes(p["s5_c_re"], 1, 2)
    ct_im = jnp.swapaxes(p["s5_c_im"], 1, 2)
    m_op, wbt, wct, a64 = _s5_ops(p["s5_lambda_re"], p["s5_lambda_im"], p["s5_log_step"],
                                  ct_re, ct_im, p["s5_b_re"], p["s5_b_im"], p["s5_c_re"], p["s5_c_im"])
    rb = min(LANES, B * nc)
    z = _s5_in(proj.reshape(B * nc, C, proj.shape[1]), G, rb)
    y_t = _s5_chunk(z, m_op, wbt, wct, a64, nc)
    y_ssm = _s5_out(y_t, rb).reshape(T, W)
    tm = min(512, T)
    s5_out = _s5_glu(y_ssm, proj, p["s5_d"].reshape(1, W), p["s5_w_glu"].astype(BF16),
                     p["s5_b_glu"].reshape(1, W), tm)

    mu = p["rwkv_mu"]
    mu3 = _pad_rows(mu[:3 * W].reshape(3, W), 8)
    o = 3 * W
    mul = jnp.concatenate([_pad_cols(mu[None, o:o + dl], 128), _pad_cols(mu[None, o + dl:o + dl + al], 128),
                           _pad_cols(mu[None, o + dl + al:], 256)], axis=1)
    w2p = _pad_rows(p["rwkv_w2"], 128).astype(BF16)
    a2p = _pad_rows(p["rwkv_a2"], 128).astype(BF16)
    g2p = _pad_rows(p["rwkv_g2"], 256).astype(BF16)
    head_of = jnp.arange(W, dtype=I32) // RW_N
    e_mat = (head_of[:, None] == jnp.arange(LANES, dtype=I32)[None, :]).astype(BF16)
    et_mat = e_mat.T
    row = lambda a: a.reshape(1, W)
    tm_rw = min(256, L)
    r, k, v, nk, bv, lw, g = _rwkv_prep(proj, B, L, mu3, mul, row(p["rwkv_w0"]), row(p["rwkv_a0"]),
                                        row(p["rwkv_k_k"]), row(p["rwkv_k_a"]), w2p, a2p, g2p,
                                        e_mat, et_mat, tm_rw)
    y_rw = _rwkv_chunk(r, k, v, nk, bv, lw, B, L)
    rw_out = _rwkv_post(y_rw, r, k, v, g, p["rwkv_r_k"].reshape(1, W), row(p["rwkv_ln_w"]),
                        row(p["rwkv_ln_b"]), e_mat, et_mat, tm)

    w_out = p["w_out"].astype(BF16)
    w_route = jnp.concatenate([p["w_route_exp"], p["w_route_grp"]], axis=1)
    w_route = _pad_cols(w_route, LANES)
    wr_hi = w_route.astype(BF16)
    wr_lo = (w_route - wr_hi.astype(F32)).astype(BF16)
    b_route = _pad_cols(jnp.concatenate([p["b_route_exp"], p["b_route_grp"]])[None, :], LANES)
    tm_o = min(256, T)
    h, hn, rec = _outproj(s5_out, rw_out, x2, w_out[:W], w_out[W:], p["norm_ffn_g"].reshape(1, D),
                          wr_hi, wr_lo, b_route, tm_o)

    tb = min(1024, T)
    rank, cnt = _rank(rec, tb)
    n_blocks = -(-2 * T // MOE_BLOCK) + N_EXPERTS
    nb_rows = -(-(n_blocks + 1) // 8) * 8
    dest, blk = _dest(rec, rank, cnt, tb, nb_rows)
    blk_exp = blk[:n_blocks, 0]
    nact = blk[0:1, 1]
    lastblk = blk[nb_rows - 1, :N_EXPERTS]
    ts = min(256, T)
    dest2 = dest[:, :2].reshape(T // ts, 2 * ts)

    xs = _scatter(lastblk, nact, dest2, hn, ts, n_blocks)
    ys = _experts(blk_exp, nact, xs, p["w_gate"], p["w_up"], p["w_down"], n_blocks)
    return dest2, ys, h, rec, ts


def kernel(x, norm_mix_g, w_in, s5_lambda_re, s5_lambda_im, s5_log_step, s5_b_re, s5_b_im, s5_c_re, s5_c_im, s5_d, s5_w_glu, s5_b_glu, rwkv_mu, rwkv_w0, rwkv_w2, rwkv_a0, rwkv_a2, rwkv_g2, rwkv_k_k, rwkv_k_a, rwkv_r_k, rwkv_ln_w, rwkv_ln_b, w_out, norm_ffn_g, w_route_grp, b_route_grp, w_route_exp, b_route_exp, w_gate, w_up, w_down, norm_final_g):
    B, L, D = x.shape
    params = dict(
        norm_mix_g=norm_mix_g, w_in=w_in, s5_lambda_re=s5_lambda_re, s5_lambda_im=s5_lambda_im,
        s5_log_step=s5_log_step, s5_b_re=s5_b_re, s5_b_im=s5_b_im, s5_c_re=s5_c_re, s5_c_im=s5_c_im,
        s5_d=s5_d, s5_w_glu=s5_w_glu, s5_b_glu=s5_b_glu, rwkv_mu=rwkv_mu, rwkv_w0=rwkv_w0, rwkv_w2=rwkv_w2,
        rwkv_a0=rwkv_a0, rwkv_a2=rwkv_a2, rwkv_g2=rwkv_g2, rwkv_k_k=rwkv_k_k, rwkv_k_a=rwkv_k_a,
        rwkv_r_k=rwkv_r_k, rwkv_ln_w=rwkv_ln_w, rwkv_ln_b=rwkv_ln_b, w_out=w_out, norm_ffn_g=norm_ffn_g,
        w_route_grp=w_route_grp, b_route_grp=b_route_grp, w_route_exp=w_route_exp, b_route_exp=b_route_exp,
        w_gate=w_gate, w_up=w_up, w_down=w_down)
    depth = norm_mix_g.shape[0]
    h2 = x.reshape(B * L, D)
    for l in range(depth):
        p = {k_: v_[l] for k_, v_ in params.items()}
        dest2, ys, h, rec, ts = _layer(h2, B, L, p)
        h2 = _combine(dest2, ys, h, rec, norm_final_g.reshape(1, D), ts, normalize=(l == depth - 1))
    return h2.reshape(B, L, D)
```

```python
import functools

import jax
import jax.numpy as jnp
from jax import lax
from jax.experimental import pallas as pl
from jax.experimental.pallas import tpu as pltpu

F32 = jnp.float32
BF16 = jnp.bfloat16
I32 = jnp.int32

RMS_EPS = 1e-6
S5_CHUNK = 64
S5_HG = 16
S5_P = 64
RW_N = 64
RW_CHUNK = 64
RW_GN_EPS = 64e-5
N_GROUPS = 8
EPG = 8
N_EXPERTS = 64
MOE_BLOCK = 256
LANES = 128
VMEM_LIMIT = 56 * 1024 * 1024


def _cparams(*sem):
    return pltpu.CompilerParams(dimension_semantics=sem, vmem_limit_bytes=VMEM_LIMIT)


def _split2(x):
    hi = x.astype(BF16)
    lo = (x - hi.astype(F32)).astype(BF16)
    return hi, lo


def _split3(x):
    hi = x.astype(BF16)
    r = x - hi.astype(F32)
    mid = r.astype(BF16)
    lo = (r - mid.astype(F32)).astype(BF16)
    return hi, mid, lo


def _dot(a, b):
    return jnp.dot(a, b, preferred_element_type=F32)


def _dot_nt(a, b):
    return lax.dot_general(a, b, (((1,), (1,)), ((), ())), preferred_element_type=F32)


def _dot_tn(a, b):
    return lax.dot_general(a, b, (((0,), (0,)), ((), ())), preferred_element_type=F32)


def _dot_x3(a, b):
    ah, al = _split2(a)
    bh, bl = _split2(b)
    return _dot(ah, bh) + (_dot(ah, bl) + _dot(al, bh))


def _dot_exact_lhs(a_bf16, b):
    bh, bm, bl = _split3(b)
    return _dot(a_bf16, bh) + (_dot(a_bf16, bm) + _dot(a_bf16, bl))


def _cmul(ar, ai, br, bi):
    return ar * br - ai * bi, ar * bi + ai * br


U32 = jnp.uint32


def _pack_halves(x):
    half = x.shape[1] // 2
    lo = lax.bitcast_convert_type(x[:, :half].astype(BF16).astype(F32), U32)
    hi = lax.bitcast_convert_type(x[:, half:].astype(BF16).astype(F32), U32)
    return (lo >> 16) | hi


def _unpack_halves(p):
    lo = lax.bitcast_convert_type(p << 16, F32)
    hi = lax.bitcast_convert_type(p & jnp.uint32(0xFFFF0000), F32)
    return lo, hi


def _inproj_kernel(x_ref, g_ref, w_ref, o_ref, hn_ref):
    @pl.when(pl.program_id(1) == 0)
    def _():
        x = x_ref[...]
        ms = jnp.mean(x * x, axis=-1, keepdims=True)
        hn_ref[...] = (x * lax.rsqrt(ms + RMS_EPS) * g_ref[...]).astype(BF16)

    o_ref[...] = _dot(hn_ref[...], w_ref[...])


def _inproj(x2, g, w_bf16, tm, tn):
    T, D = x2.shape
    N = w_bf16.shape[1]
    return pl.pallas_call(
        _inproj_kernel,
        grid=(T // tm, N // tn),
        in_specs=[
            pl.BlockSpec((tm, D), lambda i, j: (i, 0)),
            pl.BlockSpec((1, D), lambda i, j: (0, 0)),
            pl.BlockSpec((D, tn), lambda i, j: (0, j)),
        ],
        out_specs=pl.BlockSpec((tm, tn), lambda i, j: (i, j)),
        out_shape=jax.ShapeDtypeStruct((T, N), F32),
        scratch_shapes=[pltpu.VMEM((tm, D), BF16)],
        compiler_params=_cparams("arbitrary", "arbitrary"),
        name="inproj",
    )(x2, g, w_bf16)


def _binpow(ar, ai, expo, nbits):
    pr = jnp.ones(expo.shape, F32)
    pi = jnp.zeros(expo.shape, F32)
    sr, si = ar, ai
    for bit in range(nbits):
        m = ((expo >> bit) & 1) == 1
        nr, ni = _cmul(pr, pi, sr, si)
        pr = jnp.where(m, nr, pr)
        pi = jnp.where(m, ni, pi)
        if bit + 1 < nbits:
            sr, si = _cmul(sr, si, sr, si)
    return pr, pi


def _frame_powers(ar, ai, reverse):
    C, HG, P = S5_CHUNK, S5_HG, ar.shape[0]
    fpt = LANES // HG
    j = lax.broadcasted_iota(I32, (P, LANES), 1) // HG
    inner_r, inner_i = _binpow(ar, ai, (fpt - 1 - j) if reverse else j, (fpt - 1).bit_length())
    sr, si = ar, ai
    for _ in range(fpt.bit_length() - 1):
        sr, si = _cmul(sr, si, sr, si)
    outer = [(jnp.ones_like(ar), jnp.zeros_like(ai))]
    for _ in range(C // fpt - 1):
        outer.append(_cmul(outer[-1][0], outer[-1][1], sr, si))
    if reverse:
        outer = outer[::-1]
    tiles = [_cmul(inner_r, inner_i, o_r, o_i) for o_r, o_i in outer]
    return (jnp.concatenate([t[0] for t in tiles], axis=1), jnp.concatenate([t[1] for t in tiles], axis=1))


def _zoh(lr, li, dt):
    mag = jnp.exp(lr * dt)
    ang = li * dt
    ar, ai = mag * jnp.cos(ang), mag * jnp.sin(ang)
    den = lr * lr + li * li
    nr, ni = ar - 1.0, ai
    fr = (nr * lr + ni * li) / den
    fi = (ni * lr - nr * li) / den
    return ar, ai, fr, fi


def _s5_ops_kernel(lrc_ref, lic_ref, dt_ref, ct_re_ref, ct_im_ref, bt_re_ref, bt_im_ref, c_re_ref, c_im_ref,
                   m_ref, wbt_ref, wct_ref, a64_ref):
    C, HG, P = S5_CHUNK, S5_HG, S5_P
    W = C * HG
    dt = jnp.exp(dt_ref[0])
    ar_c, ai_c, fr_c, fi_c = _zoh(lrc_ref[0], lic_ref[0], dt)

    pr, pi = _frame_powers(ar_c, ai_c, False)
    qr, qi = _frame_powers(ar_c, ai_c, True)

    sel = (lax.broadcasted_iota(I32, (HG, W), 1) % HG == lax.broadcasted_iota(I32, (HG, W), 0)).astype(BF16)
    tile = lambda ref: _dot_exact_lhs_rhs(ref[0], sel)

    ca_re, ca_im = _cmul(tile(ct_re_ref), tile(ct_im_ref), pr, pi)
    c1_re, c1_im = _cmul(ca_re, ca_im, ar_c, ai_c)
    wct_ref[0] = jnp.concatenate([c1_re, -c1_im], axis=0).T.astype(BF16)

    bb_re, bb_im = _cmul(tile(bt_re_ref), tile(bt_im_ref), fr_c, fi_c)
    ab_re, ab_im = _cmul(bb_re, bb_im, qr, qi)
    wbt_ref[0, :P, :] = ab_re.astype(BF16)
    wbt_ref[0, P:, :] = ab_im.astype(BF16)

    cr, ci = ar_c, ai_c
    for _ in range(C.bit_length() - 1):
        cr, ci = _cmul(cr, ci, cr, ci)
    a64_ref[0] = jnp.concatenate([cr, ci], axis=0)

    strip = _dot_x3(c_re_ref[0], ab_re) - _dot_x3(c_im_ref[0], ab_im)
    ext = jnp.concatenate([strip, jnp.zeros((HG, W), F32)], axis=1)
    for t in range(C):
        off = (C - 1 - t) * HG
        m_ref[0, t * HG:(t + 1) * HG, :] = ext[:, off:off + W].astype(BF16)


def _s5_ops(lam_re, lam_im, log_step, ct_re, ct_im, bt_re, bt_im, c_re, c_im):
    G, P = lam_re.shape
    C, HG = S5_CHUNK, S5_HG
    W = C * HG
    col = lambda a: a.reshape(G, P, 1)
    g3 = lambda s1, s2: pl.BlockSpec((1, s1, s2), lambda g: (g, 0, 0))
    return pl.pallas_call(
        _s5_ops_kernel,
        grid=(G,),
        in_specs=[g3(P, 1), g3(P, 1), g3(1, 1),
                  g3(P, HG), g3(P, HG), g3(P, HG), g3(P, HG), g3(HG, P), g3(HG, P)],
        out_specs=[g3(W, W), g3(2 * P, W), g3(W, 2 * P), g3(2 * P, 1)],
        out_shape=[jax.ShapeDtypeStruct((G, W, W), BF16),
                   jax.ShapeDtypeStruct((G, 2 * P, W), BF16),
                   jax.ShapeDtypeStruct((G, W, 2 * P), BF16),
                   jax.ShapeDtypeStruct((G, 2 * P, 1), F32)],
        compiler_params=_cparams("arbitrary"),
        name="s5_ops",
    )(col(lam_re), col(lam_im), log_step.reshape(G, 1, 1), ct_re, ct_im, bt_re, bt_im, c_re, c_im)


S5_FB = 8


def _s5_in_kernel(x_ref, z_ref):
    G, _, rb = z_ref.shape
    for f in range(S5_FB):
        xt = x_ref[:, f, :].T
        z_ref[:, f * S5_HG:(f + 1) * S5_HG, :] = xt.reshape(G, S5_HG, rb).astype(BF16)


def _s5_in(proj3, G, rb):
    R, C, _ = proj3.shape
    W = G * S5_HG
    return pl.pallas_call(
        _s5_in_kernel,
        grid=(R // rb, C // S5_FB),
        in_specs=[pl.BlockSpec((rb, S5_FB, W), lambda i, j: (i, j, 0))],
        out_specs=pl.BlockSpec((G, S5_FB * S5_HG, rb), lambda i, j: (0, j, i)),
        out_shape=jax.ShapeDtypeStruct((G, C * S5_HG, R), BF16),
        compiler_params=_cparams("arbitrary", "arbitrary"),
        name="s5_in",
    )(proj3)


def _s5_out_kernel(yt_ref, o_ref):
    G, _, rb = yt_ref.shape
    for f in range(S5_FB):
        slab = yt_ref[:, f * S5_HG:(f + 1) * S5_HG, :].reshape(G * S5_HG, rb)
        o_ref[:, f, :] = slab.T


def _s5_out(yt, rb):
    G, CW, R = yt.shape
    C = CW // S5_HG
    W = G * S5_HG
    return pl.pallas_call(
        _s5_out_kernel,
        grid=(R // rb, C // S5_FB),
        in_specs=[pl.BlockSpec((G, S5_FB * S5_HG, rb), lambda i, j: (0, j, i))],
        out_specs=pl.BlockSpec((rb, S5_FB, W), lambda i, j: (i, j, 0)),
        out_shape=jax.ShapeDtypeStruct((R, C, W), F32),
        compiler_params=_cparams("arbitrary", "arbitrary"),
        name="s5_out",
    )(yt)


def _s5_chunk_kernel(z_ref, m_ref, wbt_ref, wct_ref, a64_ref, y_ref, *, n_chunks):
    for g in range(z_ref.shape[0]):
        _s5_chunk_group(z_ref.at[g], m_ref.at[g], wbt_ref.at[g], wct_ref.at[g], a64_ref.at[g], y_ref.at[g],
                        n_chunks)


def _s5_chunk_group(z_ref, m_ref, wbt_ref, wct_ref, a64_ref, y_ref, n_chunks):
    P = S5_P
    z = z_ref[...]
    R = z.shape[1]
    x = _dot(wbt_ref[...], z)
    y_ref[...] = _dot(m_ref[...], z)
    xr, xi = x[:P], x[P:]
    a = a64_ref[...]
    ar, ai = a[:P], a[P:]
    cidx = lax.broadcasted_iota(I32, (1, R), 1) % n_chunks
    shift = 1
    while shift < n_chunks:
        keep = cidx >= shift
        sr = jnp.where(keep, pltpu.roll(xr, shift, 1), 0.0)
        si = jnp.where(keep, pltpu.roll(xi, shift, 1), 0.0)
        xr, xi = xr + (ar * sr - ai * si), xi + (ar * si + ai * sr)
        ar, ai = ar * ar - ai * ai, 2.0 * ar * ai
        shift *= 2
    keep = cidx >= 1
    s_in = jnp.concatenate([jnp.where(keep, pltpu.roll(xr, 1, 1), 0.0),
                            jnp.where(keep, pltpu.roll(xi, 1, 1), 0.0)], axis=0)
    sh, sl = _split2(s_in)
    wct = wct_ref[...]
    y_ref[...] = y_ref[...] + (_dot(wct, sh) + _dot(wct, sl))


def _s5_chunk(z, m, wbt, wct, a64, n_chunks):
    G, W, R = z.shape
    P2 = wbt.shape[1]
    gps = 2 if G % 2 == 0 else 1
    g3 = lambda s1, s2: pl.BlockSpec((gps, s1, s2), lambda g: (g, 0, 0))
    return pl.pallas_call(
        functools.partial(_s5_chunk_kernel, n_chunks=n_chunks),
        grid=(G // gps,),
        in_specs=[g3(W, R), g3(W, W), g3(P2, W), g3(W, P2), g3(P2, 1)],
        out_specs=g3(W, R),
        out_shape=jax.ShapeDtypeStruct((G, W, R), F32),
        compiler_params=_cparams("arbitrary"),
        name="s5_chunk",
    )(z, m, wbt, wct, a64)


def _gelu_tanh(x):
    return 0.5 * x * (1.0 + jnp.tanh(0.7978845608028654 * (x + 0.044715 * (x * x * x))))


def _s5_glu_kernel(y_ref, u_ref, d_ref, w_ref, b_ref, o_ref):
    y = y_ref[...] + d_ref[...] * u_ref[...]
    y = _gelu_tanh(y)
    z = _dot(y.astype(BF16), w_ref[...]) + b_ref[...]
    o_ref[...] = (y * jax.nn.sigmoid(z)).astype(BF16)


def _s5_glu(y_ssm, proj, d, w_bf16, b, tm):
    T, W = y_ssm.shape
    return pl.pallas_call(
        _s5_glu_kernel,
        grid=(T // tm,),
        in_specs=[
            pl.BlockSpec((tm, W), lambda i: (i, 0)),
            pl.BlockSpec((tm, W), lambda i: (i, 0)),
            pl.BlockSpec((1, W), lambda i: (0, 0)),
            pl.BlockSpec((W, W), lambda i: (0, 0)),
            pl.BlockSpec((1, W), lambda i: (0, 0)),
        ],
        out_specs=pl.BlockSpec((tm, W), lambda i: (i, 0)),
        out_shape=jax.ShapeDtypeStruct((T, W), BF16),
        compiler_params=_cparams("arbitrary"),
        name="s5_glu",
    )(y_ssm, proj, d, w_bf16, b)


def _head_sum(x, e_ref, et_ref):
    xh, xl = _split2(x)
    e = e_ref[...]
    s = _dot(xh, e) + _dot(xl, e)
    sh, sl = _split2(s)
    et = et_ref[...]
    return _dot(sh, et) + _dot(sl, et)


def _shift(z, prev_row):
    rolled = pltpu.roll(z, 1, 0)
    first = lax.broadcasted_iota(I32, (z.shape[0], 1), 0) == 0
    return jnp.where(first, prev_row, rolled)


def _rwkv_prep_kernel(zr_ref, zk_ref, zv_ref, zl_ref, mu_ref, mul_ref, w0_ref, a0_ref, kk_ref, ka_ref,
                      w2_ref, a2_ref, g2_ref, e_ref, et_ref,
                      r_ref, k_ref, v_ref, nk_ref, b_ref, lw_ref, g_ref,
                      car_ref, carl_ref):
    W = r_ref.shape[1]

    @pl.when(pl.program_id(1) == 0)
    def _():
        car_ref[...] = jnp.zeros_like(car_ref)
        carl_ref[...] = jnp.zeros_like(carl_ref)

    def lerp(z, prev_row, mu):
        return z + (_shift(z, prev_row) - z) * mu

    tm = zr_ref.shape[0]
    zr, zk, zv, zl = zr_ref[...], zk_ref[...], zv_ref[...], zl_ref[...]
    r = lerp(zr, car_ref[0:1, :], mu_ref[0:1, :])
    k = lerp(zk, car_ref[1:2, :], mu_ref[1:2, :])
    v = lerp(zv, car_ref[2:3, :], mu_ref[2:3, :])
    xl = lerp(zl, carl_ref[0:1, :], mul_ref[...])
    car_ref[0:1, :] = zr[tm - 1:tm, :]
    car_ref[1:2, :] = zk[tm - 1:tm, :]
    car_ref[2:3, :] = zv[tm - 1:tm, :]
    carl_ref[0:1, :] = zl[tm - 1:tm, :]

    xw, xa, xg = xl[:, 0:128], xl[:, 128:256], xl[:, 256:512]
    dw = _dot(jnp.tanh(xw).astype(BF16), w2_ref[...])
    da = _dot(xa.astype(BF16), a2_ref[...])
    g = _dot(jax.nn.sigmoid(xg).astype(BF16), g2_ref[...])

    zw = -(w0_ref[...] + dw)
    softplus = jnp.maximum(zw, 0.0) + jnp.log(1.0 + jnp.exp(-jnp.abs(zw)))
    w_log = -softplus - 0.5
    a = jax.nn.sigmoid(a0_ref[...] + da)

    kk = k * kk_ref[...]
    n2 = _head_sum(kk * kk, e_ref, et_ref)
    kk = kk / jnp.maximum(jnp.sqrt(n2), 1e-12)

    r_ref[...] = r.astype(BF16)
    k_ref[...] = (k * (1.0 + (a - 1.0) * ka_ref[...])).astype(BF16)
    v_ref[...] = v.astype(BF16)
    nk_ref[...] = kk.astype(BF16)
    b_ref[...] = (kk * a).astype(BF16)
    lw_ref[...] = -jnp.exp(w_log)
    g_ref[...] = g.astype(BF16)


def _rwkv_prep(proj, B, L, mu3, mul, w0, a0, k_k, k_a, w2p, a2p, g2p, e_mat, et_mat, tm):
    T = B * L
    W = w0.shape[1]
    nt = L // tm
    row = lambda c: pl.BlockSpec((tm, W), lambda b, i, c=c: (b * nt + i, c))
    full = lambda a: pl.BlockSpec(a.shape, lambda b, i: (0, 0))
    out = pl.BlockSpec((tm, W), lambda b, i: (b * nt + i, 0))
    lw = 512
    return pl.pallas_call(
        _rwkv_prep_kernel,
        grid=(B, nt),
        in_specs=[row(1), row(2), row(3),
                  pl.BlockSpec((tm, lw), lambda b, i: (b * nt + i, 4 * W // lw)),
                  full(mu3), full(mul), full(w0), full(a0), full(k_k), full(k_a),
                  full(w2p), full(a2p), full(g2p), full(e_mat), full(et_mat)],
        out_specs=[out] * 7,
        out_shape=[jax.ShapeDtypeStruct((T, W), dt) for dt in (BF16, BF16, BF16, BF16, BF16, F32, BF16)],
        scratch_shapes=[pltpu.VMEM((8, W), F32), pltpu.VMEM((8, lw), F32)],
        compiler_params=_cparams("arbitrary", "arbitrary"),
        name="rwkv_prep",
    )(proj, proj, proj, proj, mu3, mul, w0, a0, k_k, k_a, w2p, a2p, g2p, e_mat, et_mat)


def _rwkv_chunk_kernel(r_ref, k_ref, v_ref, nk_ref, b_ref, lw_ref, g_ref, rk_ref, lnw_ref, lnb_ref,
                       o_ref, z_ref, ys_ref, qs_ref, vs_ref, gs_ref):
    C, N = RW_CHUNK, RW_N
    H = r_ref.shape[1] // N

    @pl.when(pl.program_id(1) == 0)
    def _():
        z_ref[...] = jnp.zeros_like(z_ref)
        ys_ref[...] = jnp.zeros_like(ys_ref)
        qs_ref[...] = jnp.zeros_like(qs_ref)
        vs_ref[...] = jnp.zeros_like(vs_ref)
        gs_ref[...] = jnp.zeros_like(gs_ref)

    low = lax.broadcasted_iota(I32, (C, 2 * N), 1) < N
    inv_n = 1.0 / N

    def head_sums(t):
        s0 = jnp.sum(jnp.where(low, t, 0.0), axis=-1, keepdims=True)
        s1 = jnp.sum(jnp.where(low, 0.0, t), axis=-1, keepdims=True)
        return jnp.where(low, s0, s1)

    for p in range(H // 2):
        ps = slice(2 * p * N, 2 * (p + 1) * N)
        yp = ys_ref[:, ps]
        yc = yp - head_sums(yp) * inv_n
        var = head_sums(yc * yc) * inv_n
        yn = yc * lax.rsqrt(var + RW_GN_EPS) * lnw_ref[:, ps] + lnb_ref[:, ps]
        bonus = head_sums(qs_ref[:, ps]) * vs_ref[:, ps].astype(F32)
        o_ref[:, ps] = ((yn + bonus) * gs_ref[:, ps].astype(F32)).astype(BF16)

    ri = lax.broadcasted_iota(I32, (C, C), 0)
    ci = lax.broadcasted_iota(I32, (C, C), 1)
    tril = ri >= ci
    stril = ri > ci
    eye = (ri == ci).astype(F32)

    lw = lw_ref[...]
    cs = _dot_exact_lhs(tril.astype(BF16), lw)
    tot = cs[C - 1:C, :]
    p_inc = jnp.exp(cs)
    p_inv = jnp.exp(-cs)
    p_prev = jnp.exp(cs - lw)
    p_rest = jnp.exp(tot - cs)
    p_tot = jnp.exp(tot)

    r_f, k_f, b_f = r_ref[...].astype(F32), k_ref[...].astype(F32), b_ref[...].astype(F32)
    r_t = r_f * p_inc
    k_t = k_f * p_inv
    a_t = -nk_ref[...].astype(F32) * p_prev
    b_t = b_f * p_inv
    k_h = k_f * p_rest
    b_h = b_f * p_rest
    v_all = v_ref[...]

    hs = range(H)
    sl = [slice(h * N, (h + 1) * N) for h in hs]
    each = lambda f: [f(h) for h in hs]
    v = each(lambda h: v_all[:, sl[h]].astype(BF16))
    ar = each(lambda h: jnp.concatenate([a_t[:, sl[h]], r_t[:, sl[h]]], axis=0).astype(BF16))
    m_b = each(lambda h: _dot_nt(ar[h], b_t[:, sl[h]].astype(BF16)))
    m_k = each(lambda h: _dot_nt(ar[h], k_t[:, sl[h]].astype(BF16)))
    l_ab = each(lambda h: jnp.where(stril, m_b[h][:C], 0.0).astype(BF16))
    m_rb = each(lambda h: jnp.where(tril, m_b[h][C:], 0.0).astype(BF16))
    l_ak = each(lambda h: jnp.where(stril, m_k[h][:C], 0.0).astype(BF16))
    m_rk = each(lambda h: jnp.where(tril, m_k[h][C:], 0.0).astype(BF16))

    x = each(lambda h: jnp.concatenate([_dot(l_ak[h], v[h]), a_t[:, sl[h]]], axis=1))
    y0 = each(lambda h: _dot(m_rk[h], v[h]))
    hk = each(lambda h: _dot_tn(k_h[:, sl[h]].astype(BF16), v[h]))
    lp = l_ab
    step = 1
    while step < C:
        x = each(lambda h: x[h] + _dot(lp[h], x[h].astype(BF16)))
        step *= 2
        if step < C:
            lp = each(lambda h: _dot(lp[h], lp[h]).astype(BF16))
    xb = each(lambda h: x[h].astype(BF16))
    yq = each(lambda h: _dot(m_rb[h], xb[h]))
    gh = each(lambda h: _dot_tn(b_h[:, sl[h]].astype(BF16), xb[h]))

    z = each(lambda h: _split2(z_ref[h]))
    qb = each(lambda h: (yq[h][:, N:] + r_t[:, sl[h]]).astype(BF16))
    gb = each(lambda h: (gh[h][:, N:] + eye * p_tot[:, sl[h]]).astype(BF16))
    yz = each(lambda h: _dot(qb[h], z[h][0]) + _dot(qb[h], z[h][1]))
    gz = each(lambda h: _dot(gb[h], z[h][0]) + _dot(gb[h], z[h][1]))
    for h in hs:
        z_ref[h] = (gh[h][:, :N] + hk[h]) + gz[h]
        ys_ref[:, sl[h]] = (yq[h][:, :N] + y0[h]) + yz[h]
    qs_ref[...] = r_f * k_f * rk_ref[...]
    vs_ref[...] = v_all
    gs_ref[...] = g_ref[...]


def _rwkv_chunk(r, k, v, nk, b, lw, g, r_k, ln_w, ln_b, B, L):
    T, W = r.shape
    C = RW_CHUNK
    nc = L // C
    H = W // RW_N
    assert H % 2 == 0
    spec = pl.BlockSpec((C, W), lambda bi, c: (bi * nc + jnp.minimum(c, nc - 1), 0))
    full = pl.BlockSpec((1, W), lambda bi, c: (0, 0))
    return pl.pallas_call(
        _rwkv_chunk_kernel,
        grid=(B, nc + 1),
        in_specs=[spec] * 7 + [full] * 3,
        out_specs=pl.BlockSpec((C, W), lambda bi, c: (bi * nc + jnp.maximum(c - 1, 0), 0)),
        out_shape=jax.ShapeDtypeStruct((T, W), BF16),
        scratch_shapes=[pltpu.VMEM((H, RW_N, RW_N), F32), pltpu.VMEM((C, W), F32), pltpu.VMEM((C, W), F32),
                        pltpu.VMEM((C, W), BF16), pltpu.VMEM((C, W), BF16)],
        compiler_params=_cparams("arbitrary", "arbitrary"),
        name="rwkv_chunk",
    )(r, k, v, nk, b, lw, g, r_k, ln_w, ln_b)


def _first_index_of_max(vals, lane, valid):
    neg = jnp.float32(-jnp.inf)
    masked = jnp.where(valid, vals, neg)
    m = jnp.max(masked, axis=-1, keepdims=True)
    idx = jnp.min(jnp.where(valid & (masked == m), lane, LANES), axis=-1, keepdims=True)
    return m, idx


def _outproj_kernel(s5_ref, rw_ref, x_ref, wt_ref, wb_ref, g_ref, wrh_ref, wrl_ref, br_ref,
                    h_ref, hn_ref, rec_ref):
    h = x_ref[...] + (_dot(s5_ref[...], wt_ref[...]) + _dot(rw_ref[...], wb_ref[...]))
    h_ref[...] = h
    ms = jnp.mean(h * h, axis=-1, keepdims=True)
    hn = h * lax.rsqrt(ms + RMS_EPS) * g_ref[...]
    hn_ref[...] = _pack_halves(hn).reshape(hn_ref.shape)

    hh, hl = _split2(hn)
    wh, wl = wrh_ref[...], wrl_ref[...]
    logits = _dot(hh, wh) + (_dot(hh, wl) + _dot(hl, wh)) + br_ref[...]
    lane = lax.broadcasted_iota(I32, logits.shape, 1)
    is_grp = (lane >= N_EXPERTS) & (lane < N_EXPERTS + N_GROUPS)
    gmax, gidx = _first_index_of_max(logits, lane, is_grp)
    gsum = jnp.sum(jnp.where(is_grp, jnp.exp(logits - gmax), 0.0), axis=-1, keepdims=True)
    p_grp = 1.0 / gsum
    grp = gidx - N_EXPERTS
    in_grp = (lane >= grp * EPG) & (lane < (grp + 1) * EPG)
    m1, i1 = _first_index_of_max(logits, lane, in_grp)
    m2, i2 = _first_index_of_max(logits, lane, in_grp & (lane != i1))
    e = jnp.exp(m2 - m1)
    g1 = p_grp / (1.0 + e)
    g2 = p_grp * e / (1.0 + e)
    rec = jnp.where(lane == 0, i1.astype(F32),
          jnp.where(lane == 1, i2.astype(F32),
          jnp.where(lane == 2, g1, jnp.where(lane == 3, g2, 0.0))))
    rec_ref[...] = rec


def _outproj(s5o, rwo, x2, w_top, w_bot, g, wr_hi, wr_lo, b_route, tm):
    T, D = x2.shape
    W = s5o.shape[1]
    full = lambda a: pl.BlockSpec(a.shape, lambda i: (0, 0))
    return pl.pallas_call(
        _outproj_kernel,
        grid=(T // tm,),
        in_specs=[pl.BlockSpec((tm, W), lambda i: (i, 0)), pl.BlockSpec((tm, W), lambda i: (i, 0)),
                  pl.BlockSpec((tm, D), lambda i: (i, 0)),
                  full(w_top), full(w_bot), full(g), full(wr_hi), full(wr_lo), full(b_route)],
        out_specs=[pl.BlockSpec((tm, D), lambda i: (i, 0)), pl.BlockSpec((tm, 1, D // 2), lambda i: (i, 0, 0)),
                   pl.BlockSpec((tm, LANES), lambda i: (i, 0))],
        out_shape=[jax.ShapeDtypeStruct((T, D), F32), jax.ShapeDtypeStruct((T, 1, D // 2), U32),
                   jax.ShapeDtypeStruct((T, LANES), F32)],
        compiler_params=_cparams("arbitrary"),
        name="outproj_route",
    )(s5o, rwo, x2, w_top, w_bot, g, wr_hi, wr_lo, b_route)


def _onehots(rec, lane):
    oh0 = (lane == rec[:, 0:1].astype(I32)).astype(F32)
    oh1 = (lane == rec[:, 1:2].astype(I32)).astype(F32)
    return oh0, oh1


def _rank_kernel(rec_ref, rank_ref, cnt_ref, base_ref):
    tb = rec_ref.shape[0]

    @pl.when(pl.program_id(0) == 0)
    def _():
        base_ref[...] = jnp.zeros_like(base_ref)

    lane = lax.broadcasted_iota(I32, (tb, LANES), 1)
    oh0, oh1 = _onehots(rec_ref[...], lane)
    both = oh0 + oh1
    ri = lax.broadcasted_iota(I32, (tb, tb), 0)
    ci = lax.broadcasted_iota(I32, (tb, tb), 1)
    before = _dot((ri > ci).astype(BF16), both.astype(BF16)) + base_ref[0:1, :]
    rank0 = jnp.sum(oh0 * before, axis=-1, keepdims=True)
    rank1 = jnp.sum(oh1 * before, axis=-1, keepdims=True)
    rank_ref[...] = jnp.where(lane == 0, rank0, jnp.where(lane == 1, rank1, 0.0))
    total = base_ref[0:1, :] + jnp.sum(both, axis=0, keepdims=True)
    base_ref[0:1, :] = total
    cnt_ref[...] = jnp.broadcast_to(total, cnt_ref.shape)


def _rank(rec, tb):
    T = rec.shape[0]
    return pl.pallas_call(
        _rank_kernel,
        grid=(T // tb,),
        in_specs=[pl.BlockSpec((tb, LANES), lambda i: (i, 0))],
        out_specs=[pl.BlockSpec((tb, LANES), lambda i: (i, 0)), pl.BlockSpec((8, LANES), lambda i: (0, 0))],
        out_shape=[jax.ShapeDtypeStruct((T, LANES), F32), jax.ShapeDtypeStruct((8, LANES), F32)],
        scratch_shapes=[pltpu.VMEM((8, LANES), F32)],
        compiler_params=_cparams("arbitrary"),
        name="moe_rank",
    )(rec)


def _padded_starts(cnt):
    padded = jnp.ceil(cnt * (1.0 / MOE_BLOCK)) * MOE_BLOCK
    ri = lax.broadcasted_iota(I32, (LANES, LANES), 0)
    ci = lax.broadcasted_iota(I32, (LANES, LANES), 1)
    p8 = jnp.broadcast_to(padded, (8, LANES))
    pend = _dot_exact_lhs_rhs(p8, (ri <= ci).astype(BF16))[0:1, :]
    return pend - padded, pend


def _dot_exact_lhs_rhs(a, b_bf16):
    ah, am, al = _split3(a)
    return _dot(ah, b_bf16) + (_dot(am, b_bf16) + _dot(al, b_bf16))


def _dest_kernel(rec_ref, rank_ref, cnt_ref, dest_ref, blk_ref):
    tb = rec_ref.shape[0]
    cnt = cnt_ref[0:1, :]
    pstart, pend = _padded_starts(cnt)
    lane = lax.broadcasted_iota(I32, (tb, LANES), 1)
    oh0, oh1 = _onehots(rec_ref[...], lane)
    rank = rank_ref[...]
    d0 = jnp.sum(oh0 * pstart, axis=-1, keepdims=True) + rank[:, 0:1]
    d1 = jnp.sum(oh1 * pstart, axis=-1, keepdims=True) + rank[:, 1:2]
    dest_ref[...] = jnp.where(lane == 0, d0, jnp.where(lane == 1, d1, 0.0)).astype(I32)

    @pl.when(pl.program_id(0) == 0)
    def _():
        nb = blk_ref.shape[0]
        blane = lax.broadcasted_iota(I32, (nb, LANES), 1)
        bstart = (lax.broadcasted_iota(I32, (nb, 1), 0) * MOE_BLOCK).astype(F32)
        is_e = blane < N_EXPERTS
        bexp = jnp.sum(jnp.where(is_e & (pend <= bstart), 1.0, 0.0), axis=-1, keepdims=True)
        bexp = jnp.minimum(bexp, N_EXPERTS - 1.0)
        nact = jnp.max(jnp.where(is_e, pend, 0.0), axis=-1, keepdims=True) * (1.0 / MOE_BLOCK)
        lastblk = jnp.where(is_e & (pend > pstart), pend - MOE_BLOCK, -1.0)
        brow = lax.broadcasted_iota(I32, (nb, 1), 0)
        out = jnp.where(blane == 0, bexp, jnp.where(blane == 1, nact, 0.0))
        out = jnp.where(brow == nb - 1, lastblk, out)
        blk_ref[...] = out.astype(I32)


def _dest(rec, rank, cnt, tb, nb_rows):
    T = rec.shape[0]
    return pl.pallas_call(
        _dest_kernel,
        grid=(T // tb,),
        in_specs=[pl.BlockSpec((tb, LANES), lambda i: (i, 0)), pl.BlockSpec((tb, LANES), lambda i: (i, 0)),
                  pl.BlockSpec((8, LANES), lambda i: (0, 0))],
        out_specs=[pl.BlockSpec((tb, LANES), lambda i: (i, 0)), pl.BlockSpec((nb_rows, LANES), lambda i: (0, 0))],
        out_shape=[jax.ShapeDtypeStruct((T, LANES), I32), jax.ShapeDtypeStruct((nb_rows, LANES), I32)],
        compiler_params=_cparams("arbitrary"),
        name="moe_dest",
    )(rec, rank, cnt)


def _scatter_kernel(pad_ref, na_ref, dest_ref, hn_ref, xs_ref, zbuf_ref, idx_ref, sa_ref, sb_ref,
                    sem_a, sem_b, isem, zsem, *, tb, n_blocks, n_tail, n_steps):
    i = pl.program_id(0)

    @pl.when(i == 0)
    def _():
        zbuf_ref[...] = jnp.zeros_like(zbuf_ref)

        def zero_rows(start):
            return pltpu.make_async_copy(zbuf_ref, xs_ref.at[pl.ds(start, MOE_BLOCK)], zsem)

        def pad_start(e, c):
            @pl.when(pad_ref[e] >= 0)
            def _():
                zero_rows(pad_ref[e]).start()
            return c

        def pad_wait(e, c):
            @pl.when(pad_ref[e] >= 0)
            def _():
                zero_rows(pad_ref[e]).wait()
            return c

        def tail_start(k, c):
            @pl.when(na_ref[0] + k < n_blocks)
            def _():
                zero_rows((na_ref[0] + k) * MOE_BLOCK).start()
            return c

        def tail_wait(k, c):
            @pl.when(na_ref[0] + k < n_blocks)
            def _():
                zero_rows((na_ref[0] + k) * MOE_BLOCK).wait()
            return c

        lax.fori_loop(0, N_EXPERTS, pad_start, 0)
        lax.fori_loop(0, n_tail, tail_start, 0)
        lax.fori_loop(0, N_EXPERTS, pad_wait, 0)
        lax.fori_loop(0, n_tail, tail_wait, 0)

    def idx_copy(step, slot):
        return pltpu.make_async_copy(dest_ref.at[step], idx_ref.at[slot], isem)

    @pl.when(i == 0)
    def _():
        idx_copy(0, 0).start()

    slot = i % 2
    idx_copy(i, slot).wait()

    @pl.when(i + 1 < n_steps)
    def _():
        idx_copy(i + 1, 1 - slot).start()

    hb = tb // 2
    for half, (buf, sem) in enumerate(((sa_ref, sem_a), (sb_ref, sem_b))):
        def wait_rows(buf=buf, sem=sem):
            for _ in range(2):
                pltpu.make_async_copy(buf, xs_ref.at[pl.ds(0, hb)], sem).wait()

        @pl.when(i > 0)
        def _():
            wait_rows()

        buf[...] = hn_ref[half * hb:(half + 1) * hb]

        def issue(t, c, buf=buf, sem=sem, half=half):
            for s in range(2):
                d = idx_ref[slot, 2 * (half * hb + t) + s]
                pltpu.make_async_copy(buf.at[pl.ds(t, 1)], xs_ref.at[pl.ds(d, 1)], sem).start()
            return c

        lax.fori_loop(0, hb, issue, 0, unroll=8)

    @pl.when(i == n_steps - 1)
    def _():
        for buf, sem in ((sa_ref, sem_a), (sb_ref, sem_b)):
            for _ in range(2):
                pltpu.make_async_copy(buf, xs_ref.at[pl.ds(0, hb)], sem).wait()


def _scatter(lastblk, nact, dest2, hn, tb, n_blocks):
    T, _, D = hn.shape
    n_tail = n_blocks - (-(-2 * T // MOE_BLOCK))
    hb = tb // 2
    return pl.pallas_call(
        functools.partial(_scatter_kernel, tb=tb, n_blocks=n_blocks, n_tail=n_tail, n_steps=T // tb),
        grid_spec=pltpu.PrefetchScalarGridSpec(
            num_scalar_prefetch=2, grid=(T // tb,),
            in_specs=[pl.BlockSpec(memory_space=pl.ANY),
                      pl.BlockSpec((tb, 1, D), lambda i, pad, na: (i, 0, 0))],
            out_specs=pl.BlockSpec(memory_space=pl.ANY),
            scratch_shapes=[pltpu.VMEM((MOE_BLOCK, 1, D), hn.dtype), pltpu.SMEM((2, 2 * tb), I32),
                            pltpu.VMEM((hb, 1, D), hn.dtype), pltpu.VMEM((hb, 1, D), hn.dtype),
                            pltpu.SemaphoreType.DMA, pltpu.SemaphoreType.DMA,
                            pltpu.SemaphoreType.DMA, pltpu.SemaphoreType.DMA]),
        out_shape=jax.ShapeDtypeStruct((n_blocks * MOE_BLOCK, 1, D), hn.dtype),
        compiler_params=_cparams("arbitrary"),
        name="moe_scatter",
    )(lastblk, nact, dest2, hn)


def _expert_kernel(be_ref, na_ref, x_ref, wg_hbm, wu_hbm, wd_hbm, y_ref,
                   wg_buf, wu_buf, wd_buf, wgb_ref, wub_ref, wdb_ref, x2_ref, slot_ref, sems):
    j = pl.program_id(0)
    na = na_ref[0]

    def weight_copies(e, slot):
        return [pltpu.make_async_copy(src.at[e], buf.at[slot], sems.at[slot, n])
                for n, (src, buf) in enumerate(((wg_hbm, wg_buf), (wu_hbm, wu_buf), (wd_hbm, wd_buf)))]

    @pl.when(j == 0)
    def _():
        slot_ref[0] = 0
        for cp in weight_copies(be_ref[0], 0):
            cp.start()

    @pl.when(j < na)
    def _():
        cur = be_ref[j]
        prev = be_ref[jnp.maximum(j - 1, 0)]

        @pl.when((j == 0) | (cur != prev))
        def _():
            slot = slot_ref[0]
            nxt = lax.while_loop(lambda k: (k < na) & (be_ref[jnp.minimum(k, na - 1)] == cur),
                                 lambda k: k + 1, j + 1)

            @pl.when(nxt < na)
            def _():
                for cp in weight_copies(be_ref[jnp.minimum(nxt, na - 1)], 1 - slot):
                    cp.start()

            for cp in weight_copies(cur, slot):
                cp.wait()
            wgb_ref[...] = wg_buf[slot].astype(BF16)
            wub_ref[...] = wu_buf[slot].astype(BF16)
            wdb_ref[...] = wd_buf[slot].astype(BF16)
            slot_ref[0] = 1 - slot

        x2_ref[...] = x_ref[...].reshape(x2_ref.shape)
        lo, hi = _unpack_halves(x2_ref[...])
        lo, hi = lo.astype(BF16), hi.astype(BF16)
        half = lo.shape[1]
        hg = _dot(lo, wgb_ref[:half, :]) + _dot(hi, wgb_ref[half:, :])
        hu = _dot(lo, wub_ref[:half, :]) + _dot(hi, wub_ref[half:, :])
        act = (hg * jax.nn.sigmoid(hg)) * hu
        y = _dot(act.astype(BF16), wdb_ref[...])
        y_ref[...] = _pack_halves(y).reshape(y_ref.shape)

    @pl.when(j >= na)
    def _():
        y_ref[...] = jnp.zeros_like(y_ref)


def _experts(blk_exp, nact, xs, w_gate, w_up, w_down, n_blocks):
    DP = xs.shape[2]
    D = 2 * DP
    rows = n_blocks * MOE_BLOCK
    DE = w_gate.shape[2]
    blk = lambda j, be, na: (jnp.minimum(j, na[0] - 1), 0, 0)
    hbm = pl.BlockSpec(memory_space=pl.ANY)
    return pl.pallas_call(
        _expert_kernel,
        grid_spec=pltpu.PrefetchScalarGridSpec(
            num_scalar_prefetch=2, grid=(n_blocks,),
            in_specs=[pl.BlockSpec((MOE_BLOCK, 1, DP), blk), hbm, hbm, hbm],
            out_specs=pl.BlockSpec((MOE_BLOCK, 1, DP), lambda j, be, na: (j, 0, 0)),
            scratch_shapes=[pltpu.VMEM((2, D, DE), F32), pltpu.VMEM((2, D, DE), F32), pltpu.VMEM((2, DE, D), F32),
                            pltpu.VMEM((D, DE), BF16), pltpu.VMEM((D, DE), BF16), pltpu.VMEM((DE, D), BF16),
                            pltpu.VMEM((MOE_BLOCK, DP), U32), pltpu.SMEM((1,), I32),
                            pltpu.SemaphoreType.DMA((2, 3))]),
        out_shape=jax.ShapeDtypeStruct((rows, 1, DP), U32),
        compiler_params=_cparams("arbitrary"),
        name="moe_experts",
    )(blk_exp, nact, xs, w_gate, w_up, w_down)


def _combine_kernel(dest_ref, ys_ref, h_ref, rec_ref, g_ref, o_ref, idx_ref, ya0_ref, ya1_ref, yb0_ref, yb1_ref,
                    y2_ref, sem_a, sem_b, isem, *, normalize, n_steps):
    i = pl.program_id(0)
    tb = h_ref.shape[0]
    hb = tb // 2
    halves = ((ya0_ref, ya1_ref, sem_a), (yb0_ref, yb1_ref, sem_b))

    def idx_copy(step, slot):
        return pltpu.make_async_copy(dest_ref.at[step], idx_ref.at[slot], isem)

    def issue(half, slot):
        bufs, sem = halves[half][:2], halves[half][2]

        def body(t, c):
            for s in range(2):
                d = idx_ref[slot, 2 * (half * hb + t) + s]
                pltpu.make_async_copy(ys_ref.at[pl.ds(d, 1)], bufs[s].at[pl.ds(t, 1)], sem).start()
            return c

        lax.fori_loop(0, hb, body, 0, unroll=8)

    @pl.when(i == 0)
    def _():
        first = idx_copy(0, 0)
        first.start()
        first.wait()
        issue(0, 0)
        issue(1, 0)
        if n_steps > 1:
            idx_copy(1, 1).start()

    nslot = (i + 1) % 2

    @pl.when(i + 1 < n_steps)
    def _():
        idx_copy(i + 1, nslot).wait()

    rec = rec_ref[...]
    for half, (y0_ref, y1_ref, sem) in enumerate(halves):
        rows = slice(half * hb, (half + 1) * hb)
        pltpu.make_async_copy(ys_ref.at[pl.ds(0, hb)], y0_ref, sem).wait()
        pltpu.make_async_copy(ys_ref.at[pl.ds(0, hb)], y1_ref, sem).wait()
        y2_ref[...] = y0_ref[...].reshape(y2_ref.shape)
        h = h_ref[rows, :] + rec[rows, 2:3] * jnp.concatenate(_unpack_halves(y2_ref[...]), axis=1)
        y2_ref[...] = y1_ref[...].reshape(y2_ref.shape)
        h = h + rec[rows, 3:4] * jnp.concatenate(_unpack_halves(y2_ref[...]), axis=1)
        if normalize:
            ms = jnp.mean(h * h, axis=-1, keepdims=True)
            h = h * lax.rsqrt(ms + RMS_EPS) * g_ref[...]
        o_ref[rows, :] = h

        @pl.when(i + 1 < n_steps)
        def _():
            issue(half, nslot)

    @pl.when(i + 2 < n_steps)
    def _():
        idx_copy(i + 2, i % 2).start()


def _combine(dest2, ys, h, rec, g, tb, normalize):
    T, D = h.shape
    hb = tb // 2
    return pl.pallas_call(
        functools.partial(_combine_kernel, normalize=normalize, n_steps=T // tb),
        grid=(T // tb,),
        in_specs=[pl.BlockSpec(memory_space=pl.ANY), pl.BlockSpec(memory_space=pl.ANY),
                  pl.BlockSpec((tb, D), lambda i: (i, 0)), pl.BlockSpec((tb, LANES), lambda i: (i, 0)),
                  pl.BlockSpec((1, D), lambda i: (0, 0))],
        out_specs=pl.BlockSpec((tb, D), lambda i: (i, 0)),
        out_shape=jax.ShapeDtypeStruct((T, D), F32),
        scratch_shapes=[pltpu.SMEM((2, 2 * tb), I32)] + [pltpu.VMEM((hb, 1, D // 2), U32)] * 4 +
                       [pltpu.VMEM((hb, D // 2), U32)] + [pltpu.SemaphoreType.DMA] * 3,
        compiler_params=_cparams("arbitrary"),
        name="moe_combine",
    )(dest2, ys, h, rec, g)


def _pad_cols(a, n):
    return jnp.pad(a, ((0, 0), (0, n - a.shape[1])))


def _pad_rows(a, n, at=0):
    return jnp.pad(a, ((at, n - a.shape[0] - at), (0, 0)))


def _layer(x2, B, L, p):
    T, D = x2.shape
    W = p["s5_d"].shape[0]
    G, P = p["s5_lambda_re"].shape
    HG = W // G
    dl, al, gl = p["rwkv_w2"].shape[0], p["rwkv_a2"].shape[0], p["rwkv_g2"].shape[0]
    H = W // RW_N

    w_in = p["w_in"]
    o = W
    cols = [w_in[:, :W], w_in[:, o:o + 3 * W]]
    o += 3 * W
    cols += [_pad_cols(w_in[:, o:o + dl], 128), _pad_cols(w_in[:, o + dl:o + dl + al], 128),
             _pad_cols(w_in[:, o + dl + al:o + dl + al + gl], 256)]
    w_in_r = jnp.concatenate(cols, axis=1).astype(BF16)
    tm_in = min(1024, T)
    proj = _inproj(x2, p["norm_mix_g"].reshape(1, D), w_in_r, tm_in, 1152)

    C = S5_CHUNK
    nc = L // C
    ct_re = jnp.swapaxes(p["s5_c_re"], 1, 2)
    ct_im = jnp.swapaxes(p["s5_c_im"], 1, 2)
    m_op, wbt, wct, a64 = _s5_ops(p["s5_lambda_re"], p["s5_lambda_im"], p["s5_log_step"],
                                  ct_re, ct_im, p["s5_b_re"], p["s5_b_im"], p["s5_c_re"], p["s5_c_im"])
    rb = min(LANES, B * nc)
    z = _s5_in(proj.reshape(B * nc, C, proj.shape[1]), G, rb)
    y_t = _s5_chunk(z, m_op, wbt, wct, a64, nc)
    y_ssm = _s5_out(y_t, rb).reshape(T, W)
    tm = min(512, T)
    s5_out = _s5_glu(y_ssm, proj, p["s5_d"].reshape(1, W), p["s5_w_glu"].astype(BF16),
                     p["s5_b_glu"].reshape(1, W), tm)

    mu = p["rwkv_mu"]
    mu3 = _pad_rows(mu[:3 * W].reshape(3, W), 8)
    o = 3 * W
    mul = jnp.concatenate([_pad_cols(mu[None, o:o + dl], 128), _pad_cols(mu[None, o + dl:o + dl + al], 128),
                           _pad_cols(mu[None, o + dl + al:], 256)], axis=1)
    w2p = _pad_rows(p["rwkv_w2"], 128).astype(BF16)
    a2p = _pad_rows(p["rwkv_a2"], 128).astype(BF16)
    g2p = _pad_rows(p["rwkv_g2"], 256).astype(BF16)
    head_of = jnp.arange(W, dtype=I32) // RW_N
    e_mat = (head_of[:, None] == jnp.arange(LANES, dtype=I32)[None, :]).astype(BF16)
    et_mat = e_mat.T
    row = lambda a: a.reshape(1, W)
    tm_rw = min(256, L)
    r, k, v, nk, bv, lw, g = _rwkv_prep(proj, B, L, mu3, mul, row(p["rwkv_w0"]), row(p["rwkv_a0"]),
                                        row(p["rwkv_k_k"]), row(p["rwkv_k_a"]), w2p, a2p, g2p,
                                        e_mat, et_mat, tm_rw)
    rw_out = _rwkv_chunk(r, k, v, nk, bv, lw, g, p["rwkv_r_k"].reshape(1, W), row(p["rwkv_ln_w"]),
                         row(p["rwkv_ln_b"]), B, L)

    w_out = p["w_out"].astype(BF16)
    w_route = jnp.concatenate([p["w_route_exp"], p["w_route_grp"]], axis=1)
    w_route = _pad_cols(w_route, LANES)
    wr_hi = w_route.astype(BF16)
    wr_lo = (w_route - wr_hi.astype(F32)).astype(BF16)
    b_route = _pad_cols(jnp.concatenate([p["b_route_exp"], p["b_route_grp"]])[None, :], LANES)
    tm_o = min(256, T)
    h, hn, rec = _outproj(s5_out, rw_out, x2, w_out[:W], w_out[W:], p["norm_ffn_g"].reshape(1, D),
                          wr_hi, wr_lo, b_route, tm_o)

    tb = min(1024, T)
    rank, cnt = _rank(rec, tb)
    n_blocks = -(-2 * T // MOE_BLOCK) + N_EXPERTS
    nb_rows = -(-(n_blocks + 1) // 8) * 8
    dest, blk = _dest(rec, rank, cnt, tb, nb_rows)
    blk_exp = blk[:n_blocks, 0]
    nact = blk[0:1, 1]
    lastblk = blk[nb_rows - 1, :N_EXPERTS]
    ts = min(256, T)
    dest2 = dest[:, :2].reshape(T // ts, 2 * ts)

    xs = _scatter(lastblk, nact, dest2, hn, ts, n_blocks)
    ys = _experts(blk_exp, nact, xs, p["w_gate"], p["w_up"], p["w_down"], n_blocks)
    return dest2, ys, h, rec, ts


def kernel(x, norm_mix_g, w_in, s5_lambda_re, s5_lambda_im, s5_log_step, s5_b_re, s5_b_im, s5_c_re, s5_c_im, s5_d, s5_w_glu, s5_b_glu, rwkv_mu, rwkv_w0, rwkv_w2, rwkv_a0, rwkv_a2, rwkv_g2, rwkv_k_k, rwkv_k_a, rwkv_r_k, rwkv_ln_w, rwkv_ln_b, w_out, norm_ffn_g, w_route_grp, b_route_grp, w_route_exp, b_route_exp, w_gate, w_up, w_down, norm_final_g):
    B, L, D = x.shape
    params = dict(
        norm_mix_g=norm_mix_g, w_in=w_in, s5_lambda_re=s5_lambda_re, s5_lambda_im=s5_lambda_im,
        s5_log_step=s5_log_step, s5_b_re=s5_b_re, s5_b_im=s5_b_im, s5_c_re=s5_c_re, s5_c_im=s5_c_im,
        s5_d=s5_d, s5_w_glu=s5_w_glu, s5_b_glu=s5_b_glu, rwkv_mu=rwkv_mu, rwkv_w0=rwkv_w0, rwkv_w2=rwkv_w2,
        rwkv_a0=rwkv_a0, rwkv_a2=rwkv_a2, rwkv_g2=rwkv_g2, rwkv_k_k=rwkv_k_k, rwkv_k_a=rwkv_k_a,
        rwkv_r_k=rwkv_r_k, rwkv_ln_w=rwkv_ln_w, rwkv_ln_b=rwkv_ln_b, w_out=w_out, norm_ffn_g=norm_ffn_g,
        w_route_grp=w_route_grp, b_route_grp=b_route_grp, w_route_exp=w_route_exp, b_route_exp=b_route_exp,
        w_gate=w_gate, w_up=w_up, w_down=w_down)
    depth = norm_mix_g.shape[0]
    h2 = x.reshape(B * L, D)
    for l in range(depth):
        p = {k_: v_[l] for k_, v_ in params.items()}
        dest2, ys, h, rec, ts = _layer(h2, B, L, p)
        h2 = _combine(dest2, ys, h, rec, norm_final_g.reshape(1, D), ts, normalize=(l == depth - 1))
    return h2.reshape(B, L, D)
```

```python
import functools

import jax
import jax.numpy as jnp
from jax import lax
from jax.experimental import pallas as pl
from jax.experimental.pallas import tpu as pltpu

F32 = jnp.float32
BF16 = jnp.bfloat16
I32 = jnp.int32

RMS_EPS = 1e-6
S5_CHUNK = 64
S5_HG = 16
S5_P = 64
RW_N = 64
RW_CHUNK = 64
RW_GN_EPS = 64e-5
N_GROUPS = 8
EPG = 8
N_EXPERTS = 64
MOE_BLOCK = 256
LANES = 128
VMEM_LIMIT = 56 * 1024 * 1024


def _cparams(*sem):
    return pltpu.CompilerParams(dimension_semantics=sem, vmem_limit_bytes=VMEM_LIMIT)


def _split2(x):
    hi = x.astype(BF16)
    lo = (x - hi.astype(F32)).astype(BF16)
    return hi, lo


def _split3(x):
    hi = x.astype(BF16)
    r = x - hi.astype(F32)
    mid = r.astype(BF16)
    lo = (r - mid.astype(F32)).astype(BF16)
    return hi, mid, lo


def _dot(a, b):
    return jnp.dot(a, b, preferred_element_type=F32)


def _dot_nt(a, b):
    return lax.dot_general(a, b, (((1,), (1,)), ((), ())), preferred_element_type=F32)


def _dot_tn(a, b):
    return lax.dot_general(a, b, (((0,), (0,)), ((), ())), preferred_element_type=F32)


def _dot_x3(a, b):
    ah, al = _split2(a)
    bh, bl = _split2(b)
    return _dot(ah, bh) + (_dot(ah, bl) + _dot(al, bh))


def _dot_exact_lhs(a_bf16, b):
    bh, bm, bl = _split3(b)
    return _dot(a_bf16, bh) + (_dot(a_bf16, bm) + _dot(a_bf16, bl))


def _cmul(ar, ai, br, bi):
    return ar * br - ai * bi, ar * bi + ai * br


U32 = jnp.uint32


def _pack_halves(x):
    half = x.shape[1] // 2
    lo = lax.bitcast_convert_type(x[:, :half].astype(BF16).astype(F32), U32)
    hi = lax.bitcast_convert_type(x[:, half:].astype(BF16).astype(F32), U32)
    return (lo >> 16) | hi


def _unpack_halves(p):
    lo = lax.bitcast_convert_type(p << 16, F32)
    hi = lax.bitcast_convert_type(p & jnp.uint32(0xFFFF0000), F32)
    return lo, hi


def _inproj_kernel(x_ref, g_ref, w_ref, o_ref, hn_ref):
    @pl.when(pl.program_id(1) == 0)
    def _():
        x = x_ref[...]
        ms = jnp.mean(x * x, axis=-1, keepdims=True)
        hn_ref[...] = (x * lax.rsqrt(ms + RMS_EPS) * g_ref[...]).astype(BF16)

    o_ref[...] = _dot(hn_ref[...], w_ref[...])


def _inproj(x2, g, w_bf16, tm, tn):
    T, D = x2.shape
    N = w_bf16.shape[1]
    return pl.pallas_call(
        _inproj_kernel,
        grid=(T // tm, N // tn),
        in_specs=[
            pl.BlockSpec((tm, D), lambda i, j: (i, 0)),
            pl.BlockSpec((1, D), lambda i, j: (0, 0)),
            pl.BlockSpec((D, tn), lambda i, j: (0, j)),
        ],
        out_specs=pl.BlockSpec((tm, tn), lambda i, j: (i, j)),
        out_shape=jax.ShapeDtypeStruct((T, N), F32),
        scratch_shapes=[pltpu.VMEM((tm, D), BF16)],
        compiler_params=_cparams("arbitrary", "arbitrary"),
        name="inproj",
    )(x2, g, w_bf16)


def _binpow(ar, ai, expo, nbits):
    pr = jnp.ones(expo.shape, F32)
    pi = jnp.zeros(expo.shape, F32)
    sr, si = ar, ai
    for bit in range(nbits):
        m = ((expo >> bit) & 1) == 1
        nr, ni = _cmul(pr, pi, sr, si)
        pr = jnp.where(m, nr, pr)
        pi = jnp.where(m, ni, pi)
        if bit + 1 < nbits:
            sr, si = _cmul(sr, si, sr, si)
    return pr, pi


def _frame_powers(ar, ai, reverse):
    C, HG, P = S5_CHUNK, S5_HG, ar.shape[0]
    fpt = LANES // HG
    j = lax.broadcasted_iota(I32, (P, LANES), 1) // HG
    inner_r, inner_i = _binpow(ar, ai, (fpt - 1 - j) if reverse else j, (fpt - 1).bit_length())
    sr, si = ar, ai
    for _ in range(fpt.bit_length() - 1):
        sr, si = _cmul(sr, si, sr, si)
    outer = [(jnp.ones_like(ar), jnp.zeros_like(ai))]
    for _ in range(C // fpt - 1):
        outer.append(_cmul(outer[-1][0], outer[-1][1], sr, si))
    if reverse:
        outer = outer[::-1]
    tiles = [_cmul(inner_r, inner_i, o_r, o_i) for o_r, o_i in outer]
    return (jnp.concatenate([t[0] for t in tiles], axis=1), jnp.concatenate([t[1] for t in tiles], axis=1))


def _zoh(lr, li, dt):
    mag = jnp.exp(lr * dt)
    ang = li * dt
    ar, ai = mag * jnp.cos(ang), mag * jnp.sin(ang)
    den = lr * lr + li * li
    nr, ni = ar - 1.0, ai
    fr = (nr * lr + ni * li) / den
    fi = (ni * lr - nr * li) / den
    return ar, ai, fr, fi


def _s5_ops_kernel(lrc_ref, lic_ref, dt_ref, ct_re_ref, ct_im_ref, bt_re_ref, bt_im_ref, c_re_ref, c_im_ref,
                   m_ref, wbt_ref, wct_ref, a64_ref):
    C, HG, P = S5_CHUNK, S5_HG, S5_P
    W = C * HG
    dt = jnp.exp(dt_ref[0])
    ar_c, ai_c, fr_c, fi_c = _zoh(lrc_ref[0], lic_ref[0], dt)

    pr, pi = _frame_powers(ar_c, ai_c, False)
    qr, qi = _frame_powers(ar_c, ai_c, True)

    sel = (lax.broadcasted_iota(I32, (HG, W), 1) % HG == lax.broadcasted_iota(I32, (HG, W), 0)).astype(BF16)
    tile = lambda ref: _dot_exact_lhs_rhs(ref[0], sel)

    ca_re, ca_im = _cmul(tile(ct_re_ref), tile(ct_im_ref), pr, pi)
    c1_re, c1_im = _cmul(ca_re, ca_im, ar_c, ai_c)
    wct_ref[0] = jnp.concatenate([c1_re, -c1_im], axis=0).T.astype(BF16)

    bb_re, bb_im = _cmul(tile(bt_re_ref), tile(bt_im_ref), fr_c, fi_c)
    ab_re, ab_im = _cmul(bb_re, bb_im, qr, qi)
    wbt_ref[0, :P, :] = ab_re.astype(BF16)
    wbt_ref[0, P:, :] = ab_im.astype(BF16)

    cr, ci = ar_c, ai_c
    for _ in range(C.bit_length() - 1):
        cr, ci = _cmul(cr, ci, cr, ci)
    a64_ref[0] = jnp.concatenate([cr, ci], axis=0)

    strip = _dot_x3(c_re_ref[0], ab_re) - _dot_x3(c_im_ref[0], ab_im)
    ext = jnp.concatenate([strip, jnp.zeros((HG, W), F32)], axis=1)
    for t in range(C):
        off = (C - 1 - t) * HG
        m_ref[0, t * HG:(t + 1) * HG, :] = ext[:, off:off + W].astype(BF16)


def _s5_ops(lam_re, lam_im, log_step, ct_re, ct_im, bt_re, bt_im, c_re, c_im):
    G, P = lam_re.shape
    C, HG = S5_CHUNK, S5_HG
    W = C * HG
    col = lambda a: a.reshape(G, P, 1)
    g3 = lambda s1, s2: pl.BlockSpec((1, s1, s2), lambda g: (g, 0, 0))
    return pl.pallas_call(
        _s5_ops_kernel,
        grid=(G,),
        in_specs=[g3(P, 1), g3(P, 1), g3(1, 1),
                  g3(P, HG), g3(P, HG), g3(P, HG), g3(P, HG), g3(HG, P), g3(HG, P)],
        out_specs=[g3(W, W), g3(2 * P, W), g3(W, 2 * P), g3(2 * P, 1)],
        out_shape=[jax.ShapeDtypeStruct((G, W, W), BF16),
                   jax.ShapeDtypeStruct((G, 2 * P, W), BF16),
                   jax.ShapeDtypeStruct((G, W, 2 * P), BF16),
                   jax.ShapeDtypeStruct((G, 2 * P, 1), F32)],
        compiler_params=_cparams("arbitrary"),
        name="s5_ops",
    )(col(lam_re), col(lam_im), log_step.reshape(G, 1, 1), ct_re, ct_im, bt_re, bt_im, c_re, c_im)


S5_FB = 8


def _s5_in_kernel(x_ref, z_ref):
    G, _, rb = z_ref.shape
    for f in range(S5_FB):
        xt = x_ref[:, f, :].T
        z_ref[:, f * S5_HG:(f + 1) * S5_HG, :] = xt.reshape(G, S5_HG, rb).astype(BF16)


def _s5_in(proj3, G, rb):
    R, C, _ = proj3.shape
    W = G * S5_HG
    return pl.pallas_call(
        _s5_in_kernel,
        grid=(R // rb, C // S5_FB),
        in_specs=[pl.BlockSpec((rb, S5_FB, W), lambda i, j: (i, j, 0))],
        out_specs=pl.BlockSpec((G, S5_FB * S5_HG, rb), lambda i, j: (0, j, i)),
        out_shape=jax.ShapeDtypeStruct((G, C * S5_HG, R), BF16),
        compiler_params=_cparams("arbitrary", "arbitrary"),
        name="s5_in",
    )(proj3)


def _s5_out_kernel(yt_ref, o_ref):
    G, _, rb = yt_ref.shape
    for f in range(S5_FB):
        slab = yt_ref[:, f * S5_HG:(f + 1) * S5_HG, :].reshape(G * S5_HG, rb)
        o_ref[:, f, :] = slab.T


def _s5_out(yt, rb):
    G, CW, R = yt.shape
    C = CW // S5_HG
    W = G * S5_HG
    return pl.pallas_call(
        _s5_out_kernel,
        grid=(R // rb, C // S5_FB),
        in_specs=[pl.BlockSpec((G, S5_FB * S5_HG, rb), lambda i, j: (0, j, i))],
        out_specs=pl.BlockSpec((rb, S5_FB, W), lambda i, j: (i, j, 0)),
        out_shape=jax.ShapeDtypeStruct((R, C, W), F32),
        compiler_params=_cparams("arbitrary", "arbitrary"),
        name="s5_out",
    )(yt)


def _s5_chunk_kernel(z_ref, m_ref, wbt_ref, wct_ref, a64_ref, y_ref, *, n_chunks):
    for g in range(z_ref.shape[0]):
        _s5_chunk_group(z_ref.at[g], m_ref.at[g], wbt_ref.at[g], wct_ref.at[g], a64_ref.at[g], y_ref.at[g],
                        n_chunks)


def _s5_chunk_group(z_ref, m_ref, wbt_ref, wct_ref, a64_ref, y_ref, n_chunks):
    P = S5_P
    z = z_ref[...]
    R = z.shape[1]
    x = _dot(wbt_ref[...], z)
    y_ref[...] = _dot(m_ref[...], z)
    xr, xi = x[:P], x[P:]
    a = a64_ref[...]
    ar, ai = a[:P], a[P:]
    cidx = lax.broadcasted_iota(I32, (1, R), 1) % n_chunks
    shift = 1
    while shift < n_chunks:
        keep = cidx >= shift
        sr = jnp.where(keep, pltpu.roll(xr, shift, 1), 0.0)
        si = jnp.where(keep, pltpu.roll(xi, shift, 1), 0.0)
        xr, xi = xr + (ar * sr - ai * si), xi + (ar * si + ai * sr)
        ar, ai = ar * ar - ai * ai, 2.0 * ar * ai
        shift *= 2
    keep = cidx >= 1
    s_in = jnp.concatenate([jnp.where(keep, pltpu.roll(xr, 1, 1), 0.0),
                            jnp.where(keep, pltpu.roll(xi, 1, 1), 0.0)], axis=0)
    sh, sl = _split2(s_in)
    wct = wct_ref[...]
    y_ref[...] = y_ref[...] + (_dot(wct, sh) + _dot(wct, sl))


def _s5_chunk(z, m, wbt, wct, a64, n_chunks):
    G, W, R = z.shape
    P2 = wbt.shape[1]
    gps = 2 if G % 2 == 0 else 1
    g3 = lambda s1, s2: pl.BlockSpec((gps, s1, s2), lambda g: (g, 0, 0))
    return pl.pallas_call(
        functools.partial(_s5_chunk_kernel, n_chunks=n_chunks),
        grid=(G // gps,),
        in_specs=[g3(W, R), g3(W, W), g3(P2, W), g3(W, P2), g3(P2, 1)],
        out_specs=g3(W, R),
        out_shape=jax.ShapeDtypeStruct((G, W, R), F32),
        compiler_params=_cparams("arbitrary"),
        name="s5_chunk",
    )(z, m, wbt, wct, a64)


def _gelu_tanh(x):
    return 0.5 * x * (1.0 + jnp.tanh(0.7978845608028654 * (x + 0.044715 * (x * x * x))))


def _s5_glu_kernel(y_ref, u_ref, d_ref, w_ref, b_ref, o_ref):
    y = y_ref[...] + d_ref[...] * u_ref[...]
    y = _gelu_tanh(y)
    z = _dot(y.astype(BF16), w_ref[...]) + b_ref[...]
    o_ref[...] = (y * jax.nn.sigmoid(z)).astype(BF16)


def _s5_glu(y_ssm, proj, d, w_bf16, b, tm):
    T, W = y_ssm.shape
    return pl.pallas_call(
        _s5_glu_kernel,
        grid=(T // tm,),
        in_specs=[
            pl.BlockSpec((tm, W), lambda i: (i, 0)),
            pl.BlockSpec((tm, W), lambda i: (i, 0)),
            pl.BlockSpec((1, W), lambda i: (0, 0)),
            pl.BlockSpec((W, W), lambda i: (0, 0)),
            pl.BlockSpec((1, W), lambda i: (0, 0)),
        ],
        out_specs=pl.BlockSpec((tm, W), lambda i: (i, 0)),
        out_shape=jax.ShapeDtypeStruct((T, W), BF16),
        compiler_params=_cparams("arbitrary"),
        name="s5_glu",
    )(y_ssm, proj, d, w_bf16, b)


def _head_sum(x, e_ref, et_ref):
    xh, xl = _split2(x)
    e = e_ref[...]
    s = _dot(xh, e) + _dot(xl, e)
    sh, sl = _split2(s)
    et = et_ref[...]
    return _dot(sh, et) + _dot(sl, et)


def _shift(z, prev_row):
    rolled = pltpu.roll(z, 1, 0)
    first = lax.broadcasted_iota(I32, (z.shape[0], 1), 0) == 0
    return jnp.where(first, prev_row, rolled)


def _rwkv_prep_kernel(zr_ref, zk_ref, zv_ref, zl_ref, mu_ref, mul_ref, w0_ref, a0_ref, kk_ref, ka_ref,
                      w2_ref, a2_ref, g2_ref, e_ref, et_ref,
                      r_ref, k_ref, v_ref, nk_ref, b_ref, lw_ref, g_ref,
                      car_ref, carl_ref):
    W = r_ref.shape[1]

    @pl.when(pl.program_id(1) == 0)
    def _():
        car_ref[...] = jnp.zeros_like(car_ref)
        carl_ref[...] = jnp.zeros_like(carl_ref)

    def lerp(z, prev_row, mu):
        return z + (_shift(z, prev_row) - z) * mu

    tm = zr_ref.shape[0]
    zr, zk, zv, zl = zr_ref[...], zk_ref[...], zv_ref[...], zl_ref[...]
    r = lerp(zr, car_ref[0:1, :], mu_ref[0:1, :])
    k = lerp(zk, car_ref[1:2, :], mu_ref[1:2, :])
    v = lerp(zv, car_ref[2:3, :], mu_ref[2:3, :])
    xl = lerp(zl, carl_ref[0:1, :], mul_ref[...])
    car_ref[0:1, :] = zr[tm - 1:tm, :]
    car_ref[1:2, :] = zk[tm - 1:tm, :]
    car_ref[2:3, :] = zv[tm - 1:tm, :]
    carl_ref[0:1, :] = zl[tm - 1:tm, :]

    xw, xa, xg = xl[:, 0:128], xl[:, 128:256], xl[:, 256:512]
    dw = _dot(jnp.tanh(xw).astype(BF16), w2_ref[...])
    da = _dot(xa.astype(BF16), a2_ref[...])
    g = _dot(jax.nn.sigmoid(xg).astype(BF16), g2_ref[...])

    zw = -(w0_ref[...] + dw)
    softplus = jnp.maximum(zw, 0.0) + jnp.log(1.0 + jnp.exp(-jnp.abs(zw)))
    w_log = -softplus - 0.5
    a = jax.nn.sigmoid(a0_ref[...] + da)

    kk = k * kk_ref[...]
    n2 = _head_sum(kk * kk, e_ref, et_ref)
    kk = kk / jnp.maximum(jnp.sqrt(n2), 1e-12)

    r_ref[...] = r.astype(BF16)
    k_ref[...] = (k * (1.0 + (a - 1.0) * ka_ref[...])).astype(BF16)
    v_ref[...] = v.astype(BF16)
    nk_ref[...] = kk.astype(BF16)
    b_ref[...] = (kk * a).astype(BF16)
    lw_ref[...] = -jnp.exp(w_log)
    g_ref[...] = g.astype(BF16)


def _rwkv_prep(proj, B, L, mu3, mul, w0, a0, k_k, k_a, w2p, a2p, g2p, e_mat, et_mat, tm):
    T = B * L
    W = w0.shape[1]
    nt = L // tm
    row = lambda c: pl.BlockSpec((tm, W), lambda b, i, c=c: (b * nt + i, c))
    full = lambda a: pl.BlockSpec(a.shape, lambda b, i: (0, 0))
    out = pl.BlockSpec((tm, W), lambda b, i: (b * nt + i, 0))
    lw = 512
    return pl.pallas_call(
        _rwkv_prep_kernel,
        grid=(B, nt),
        in_specs=[row(1), row(2), row(3),
                  pl.BlockSpec((tm, lw), lambda b, i: (b * nt + i, 4 * W // lw)),
                  full(mu3), full(mul), full(w0), full(a0), full(k_k), full(k_a),
                  full(w2p), full(a2p), full(g2p), full(e_mat), full(et_mat)],
        out_specs=[out] * 7,
        out_shape=[jax.ShapeDtypeStruct((T, W), dt) for dt in (BF16, BF16, BF16, BF16, BF16, F32, BF16)],
        scratch_shapes=[pltpu.VMEM((8, W), F32), pltpu.VMEM((8, lw), F32)],
        compiler_params=_cparams("arbitrary", "arbitrary"),
        name="rwkv_prep",
    )(proj, proj, proj, proj, mu3, mul, w0, a0, k_k, k_a, w2p, a2p, g2p, e_mat, et_mat)


def _rwkv_chunk_kernel(r_ref, k_ref, v_ref, nk_ref, b_ref, lw_ref, g_ref, rk_ref, lnw_ref, lnb_ref,
                       o_ref, z_ref, ys_ref, qs_ref, vs_ref, gs_ref):
    C, N = RW_CHUNK, RW_N
    H = r_ref.shape[1] // N

    @pl.when(pl.program_id(1) == 0)
    def _():
        z_ref[...] = jnp.zeros_like(z_ref)
        ys_ref[...] = jnp.zeros_like(ys_ref)
        qs_ref[...] = jnp.zeros_like(qs_ref)
        vs_ref[...] = jnp.zeros_like(vs_ref)
        gs_ref[...] = jnp.zeros_like(gs_ref)

    low = lax.broadcasted_iota(I32, (C, 2 * N), 1) < N
    inv_n = 1.0 / N

    def head_sums(t):
        s0 = jnp.sum(jnp.where(low, t, 0.0), axis=-1, keepdims=True)
        s1 = jnp.sum(jnp.where(low, 0.0, t), axis=-1, keepdims=True)
        return jnp.where(low, s0, s1)

    for p in range(H // 2):
        ps = slice(2 * p * N, 2 * (p + 1) * N)
        yp = ys_ref[:, ps]
        yc = yp - head_sums(yp) * inv_n
        var = head_sums(yc * yc) * inv_n
        yn = yc * lax.rsqrt(var + RW_GN_EPS) * lnw_ref[:, ps] + lnb_ref[:, ps]
        bonus = head_sums(qs_ref[:, ps]) * vs_ref[:, ps].astype(F32)
        o_ref[:, ps] = ((yn + bonus) * gs_ref[:, ps].astype(F32)).astype(BF16)

    ri = lax.broadcasted_iota(I32, (C, C), 0)
    ci = lax.broadcasted_iota(I32, (C, C), 1)
    tril = ri >= ci
    stril = ri > ci
    eye = (ri == ci).astype(F32)

    lw = lw_ref[...]
    cs = _dot_exact_lhs(tril.astype(BF16), lw)
    tot = cs[C - 1:C, :]
    p_inc = jnp.exp(cs)
    p_inv = jnp.exp(-cs)
    p_prev = jnp.exp(cs - lw)
    p_rest = jnp.exp(tot - cs)
    p_tot = jnp.exp(tot)

    r_f, k_f, b_f = r_ref[...].astype(F32), k_ref[...].astype(F32), b_ref[...].astype(F32)
    r_t = r_f * p_inc
    k_t = k_f * p_inv
    a_t = -nk_ref[...].astype(F32) * p_prev
    b_t = b_f * p_inv
    k_h = k_f * p_rest
    b_h = b_f * p_rest
    v_all = v_ref[...]

    hs = range(H)
    sl = [slice(h * N, (h + 1) * N) for h in hs]
    each = lambda f: [f(h) for h in hs]
    v = each(lambda h: v_all[:, sl[h]].astype(BF16))
    ar = each(lambda h: jnp.concatenate([a_t[:, sl[h]], r_t[:, sl[h]]], axis=0).astype(BF16))
    m_b = each(lambda h: _dot_nt(ar[h], b_t[:, sl[h]].astype(BF16)))
    m_k = each(lambda h: _dot_nt(ar[h], k_t[:, sl[h]].astype(BF16)))
    l_ab = each(lambda h: jnp.where(stril, m_b[h][:C], 0.0).astype(BF16))
    m_rb = each(lambda h: jnp.where(tril, m_b[h][C:], 0.0).astype(BF16))
    l_ak = each(lambda h: jnp.where(stril, m_k[h][:C], 0.0).astype(BF16))
    m_rk = each(lambda h: jnp.where(tril, m_k[h][C:], 0.0).astype(BF16))

    x = each(lambda h: jnp.concatenate([_dot(l_ak[h], v[h]), a_t[:, sl[h]]], axis=1))
    y0 = each(lambda h: _dot(m_rk[h], v[h]))
    hk = each(lambda h: _dot_tn(k_h[:, sl[h]].astype(BF16), v[h]))
    lp = l_ab
    step = 1
    while step < C:
        x = each(lambda h: x[h] + _dot(lp[h], x[h].astype(BF16)))
        step *= 2
        if step < C:
            lp = each(lambda h: _dot(lp[h], lp[h]).astype(BF16))
    xb = each(lambda h: x[h].astype(BF16))
    yq = each(lambda h: _dot(m_rb[h], xb[h]))
    gh = each(lambda h: _dot_tn(b_h[:, sl[h]].astype(BF16), xb[h]))

    z = each(lambda h: _split2(z_ref[h]))
    qb = each(lambda h: (yq[h][:, N:] + r_t[:, sl[h]]).astype(BF16))
    gb = each(lambda h: (gh[h][:, N:] + eye * p_tot[:, sl[h]]).astype(BF16))
    yz = each(lambda h: _dot(qb[h], z[h][0]) + _dot(qb[h], z[h][1]))
    gz = each(lambda h: _dot(gb[h], z[h][0]) + _dot(gb[h], z[h][1]))
    for h in hs:
        z_ref[h] = (gh[h][:, :N] + hk[h]) + gz[h]
        ys_ref[:, sl[h]] = (yq[h][:, :N] + y0[h]) + yz[h]
    qs_ref[...] = r_f * k_f * rk_ref[...]
    vs_ref[...] = v_all
    gs_ref[...] = g_ref[...]


def _rwkv_chunk(r, k, v, nk, b, lw, g, r_k, ln_w, ln_b, B, L):
    T, W = r.shape
    C = RW_CHUNK
    nc = L // C
    H = W // RW_N
    assert H % 2 == 0
    spec = pl.BlockSpec((C, W), lambda bi, c: (bi * nc + jnp.minimum(c, nc - 1), 0))
    full = pl.BlockSpec((1, W), lambda bi, c: (0, 0))
    return pl.pallas_call(
        _rwkv_chunk_kernel,
        grid=(B, nc + 1),
        in_specs=[spec] * 7 + [full] * 3,
        out_specs=pl.BlockSpec((C, W), lambda bi, c: (bi * nc + jnp.maximum(c - 1, 0), 0)),
        out_shape=jax.ShapeDtypeStruct((T, W), BF16),
        scratch_shapes=[pltpu.VMEM((H, RW_N, RW_N), F32), pltpu.VMEM((C, W), F32), pltpu.VMEM((C, W), F32),
                        pltpu.VMEM((C, W), BF16), pltpu.VMEM((C, W), BF16)],
        compiler_params=_cparams("arbitrary", "arbitrary"),
        name="rwkv_chunk",
    )(r, k, v, nk, b, lw, g, r_k, ln_w, ln_b)


def _first_index_of_max(vals, lane, valid):
    neg = jnp.float32(-jnp.inf)
    masked = jnp.where(valid, vals, neg)
    m = jnp.max(masked, axis=-1, keepdims=True)
    idx = jnp.min(jnp.where(valid & (masked == m), lane, LANES), axis=-1, keepdims=True)
    return m, idx


def _outproj_kernel(s5_ref, rw_ref, x_ref, wt_ref, wb_ref, g_ref, wrh_ref, wrl_ref, br_ref,
                    h_ref, hn_ref, rec_ref):
    h = x_ref[...] + (_dot(s5_ref[...], wt_ref[...]) + _dot(rw_ref[...], wb_ref[...]))
    h_ref[...] = h
    ms = jnp.mean(h * h, axis=-1, keepdims=True)
    hn = h * lax.rsqrt(ms + RMS_EPS) * g_ref[...]
    hn_ref[...] = _pack_halves(hn).reshape(hn_ref.shape)

    hh, hl = _split2(hn)
    wh, wl = wrh_ref[...], wrl_ref[...]
    logits = _dot(hh, wh) + (_dot(hh, wl) + _dot(hl, wh)) + br_ref[...]
    lane = lax.broadcasted_iota(I32, logits.shape, 1)
    is_grp = (lane >= N_EXPERTS) & (lane < N_EXPERTS + N_GROUPS)
    gmax, gidx = _first_index_of_max(logits, lane, is_grp)
    gsum = jnp.sum(jnp.where(is_grp, jnp.exp(logits - gmax), 0.0), axis=-1, keepdims=True)
    p_grp = 1.0 / gsum
    grp = gidx - N_EXPERTS
    in_grp = (lane >= grp * EPG) & (lane < (grp + 1) * EPG)
    m1, i1 = _first_index_of_max(logits, lane, in_grp)
    m2, i2 = _first_index_of_max(logits, lane, in_grp & (lane != i1))
    e = jnp.exp(m2 - m1)
    g1 = p_grp / (1.0 + e)
    g2 = p_grp * e / (1.0 + e)
    rec = jnp.where(lane == 0, i1.astype(F32),
          jnp.where(lane == 1, i2.astype(F32),
          jnp.where(lane == 2, g1, jnp.where(lane == 3, g2, 0.0))))
    rec_ref[...] = rec


def _outproj(s5o, rwo, x2, w_top, w_bot, g, wr_hi, wr_lo, b_route, tm):
    T, D = x2.shape
    W = s5o.shape[1]
    full = lambda a: pl.BlockSpec(a.shape, lambda i: (0, 0))
    return pl.pallas_call(
        _outproj_kernel,
        grid=(T // tm,),
        in_specs=[pl.BlockSpec((tm, W), lambda i: (i, 0)), pl.BlockSpec((tm, W), lambda i: (i, 0)),
                  pl.BlockSpec((tm, D), lambda i: (i, 0)),
                  full(w_top), full(w_bot), full(g), full(wr_hi), full(wr_lo), full(b_route)],
        out_specs=[pl.BlockSpec((tm, D), lambda i: (i, 0)), pl.BlockSpec((tm, 1, D // 2), lambda i: (i, 0, 0)),
                   pl.BlockSpec((tm, LANES), lambda i: (i, 0))],
        out_shape=[jax.ShapeDtypeStruct((T, D), F32), jax.ShapeDtypeStruct((T, 1, D // 2), U32),
                   jax.ShapeDtypeStruct((T, LANES), F32)],
        compiler_params=_cparams("arbitrary"),
        name="outproj_route",
    )(s5o, rwo, x2, w_top, w_bot, g, wr_hi, wr_lo, b_route)


def _onehots(rec, lane):
    oh0 = (lane == rec[:, 0:1].astype(I32)).astype(F32)
    oh1 = (lane == rec[:, 1:2].astype(I32)).astype(F32)
    return oh0, oh1


def _rank_kernel(rec_ref, rank_ref, cnt_ref, base_ref):
    tb = rec_ref.shape[0]

    @pl.when(pl.program_id(0) == 0)
    def _():
        base_ref[...] = jnp.zeros_like(base_ref)

    lane = lax.broadcasted_iota(I32, (tb, LANES), 1)
    oh0, oh1 = _onehots(rec_ref[...], lane)
    both = oh0 + oh1
    ri = lax.broadcasted_iota(I32, (tb, tb), 0)
    ci = lax.broadcasted_iota(I32, (tb, tb), 1)
    before = _dot((ri > ci).astype(BF16), both.astype(BF16)) + base_ref[0:1, :]
    rank0 = jnp.sum(oh0 * before, axis=-1, keepdims=True)
    rank1 = jnp.sum(oh1 * before, axis=-1, keepdims=True)
    rank_ref[...] = jnp.where(lane == 0, rank0, jnp.where(lane == 1, rank1, 0.0))
    total = base_ref[0:1, :] + jnp.sum(both, axis=0, keepdims=True)
    base_ref[0:1, :] = total
    cnt_ref[...] = jnp.broadcast_to(total, cnt_ref.shape)


def _rank(rec, tb):
    T = rec.shape[0]
    return pl.pallas_call(
        _rank_kernel,
        grid=(T // tb,),
        in_specs=[pl.BlockSpec((tb, LANES), lambda i: (i, 0))],
        out_specs=[pl.BlockSpec((tb, LANES), lambda i: (i, 0)), pl.BlockSpec((8, LANES), lambda i: (0, 0))],
        out_shape=[jax.ShapeDtypeStruct((T, LANES), F32), jax.ShapeDtypeStruct((8, LANES), F32)],
        scratch_shapes=[pltpu.VMEM((8, LANES), F32)],
        compiler_params=_cparams("arbitrary"),
        name="moe_rank",
    )(rec)


def _padded_starts(cnt):
    padded = jnp.ceil(cnt * (1.0 / MOE_BLOCK)) * MOE_BLOCK
    ri = lax.broadcasted_iota(I32, (LANES, LANES), 0)
    ci = lax.broadcasted_iota(I32, (LANES, LANES), 1)
    p8 = jnp.broadcast_to(padded, (8, LANES))
    pend = _dot_exact_lhs_rhs(p8, (ri <= ci).astype(BF16))[0:1, :]
    return pend - padded, pend


def _dot_exact_lhs_rhs(a, b_bf16):
    ah, am, al = _split3(a)
    return _dot(ah, b_bf16) + (_dot(am, b_bf16) + _dot(al, b_bf16))


def _dest_kernel(rec_ref, rank_ref, cnt_ref, dest_ref, blk_ref):
    tb = rec_ref.shape[0]
    cnt = cnt_ref[0:1, :]
    pstart, pend = _padded_starts(cnt)
    lane = lax.broadcasted_iota(I32, (tb, LANES), 1)
    oh0, oh1 = _onehots(rec_ref[...], lane)
    rank = rank_ref[...]
    d0 = jnp.sum(oh0 * pstart, axis=-1, keepdims=True) + rank[:, 0:1]
    d1 = jnp.sum(oh1 * pstart, axis=-1, keepdims=True) + rank[:, 1:2]
    dest_ref[...] = jnp.where(lane == 0, d0, jnp.where(lane == 1, d1, 0.0)).astype(I32)

    @pl.when(pl.program_id(0) == 0)
    def _():
        nb = blk_ref.shape[0]
        blane = lax.broadcasted_iota(I32, (nb, LANES), 1)
        bstart = (lax.broadcasted_iota(I32, (nb, 1), 0) * MOE_BLOCK).astype(F32)
        is_e = blane < N_EXPERTS
        bexp = jnp.sum(jnp.where(is_e & (pend <= bstart), 1.0, 0.0), axis=-1, keepdims=True)
        bexp = jnp.minimum(bexp, N_EXPERTS - 1.0)
        nact = jnp.max(jnp.where(is_e, pend, 0.0), axis=-1, keepdims=True) * (1.0 / MOE_BLOCK)
        lastblk = jnp.where(is_e & (pend > pstart), pend - MOE_BLOCK, -1.0)
        brow = lax.broadcasted_iota(I32, (nb, 1), 0)
        out = jnp.where(blane == 0, bexp, jnp.where(blane == 1, nact, 0.0))
        out = jnp.where(brow == nb - 1, lastblk, out)
        blk_ref[...] = out.astype(I32)


def _dest(rec, rank, cnt, tb, nb_rows):
    T = rec.shape[0]
    return pl.pallas_call(
        _dest_kernel,
        grid=(T // tb,),
        in_specs=[pl.BlockSpec((tb, LANES), lambda i: (i, 0)), pl.BlockSpec((tb, LANES), lambda i: (i, 0)),
                  pl.BlockSpec((8, LANES), lambda i: (0, 0))],
        out_specs=[pl.BlockSpec((tb, LANES), lambda i: (i, 0)), pl.BlockSpec((nb_rows, LANES), lambda i: (0, 0))],
        out_shape=[jax.ShapeDtypeStruct((T, LANES), I32), jax.ShapeDtypeStruct((nb_rows, LANES), I32)],
        compiler_params=_cparams("arbitrary"),
        name="moe_dest",
    )(rec, rank, cnt)


def _scatter_kernel(pad_ref, na_ref, dest_ref, hn_ref, xs_ref, zbuf_ref, idx_ref, sa_ref, sb_ref,
                    sem_a, sem_b, isem, zsem, *, tb, n_blocks, n_tail, n_steps):
    i = pl.program_id(0)

    @pl.when(i == 0)
    def _():
        zbuf_ref[...] = jnp.zeros_like(zbuf_ref)

        def zero_rows(start):
            return pltpu.make_async_copy(zbuf_ref, xs_ref.at[pl.ds(start, MOE_BLOCK)], zsem)

        def pad_start(e, c):
            @pl.when(pad_ref[e] >= 0)
            def _():
                zero_rows(pad_ref[e]).start()
            return c

        def pad_wait(e, c):
            @pl.when(pad_ref[e] >= 0)
            def _():
                zero_rows(pad_ref[e]).wait()
            return c

        def tail_start(k, c):
            @pl.when(na_ref[0] + k < n_blocks)
            def _():
                zero_rows((na_ref[0] + k) * MOE_BLOCK).start()
            return c

        def tail_wait(k, c):
            @pl.when(na_ref[0] + k < n_blocks)
            def _():
                zero_rows((na_ref[0] + k) * MOE_BLOCK).wait()
            return c

        lax.fori_loop(0, N_EXPERTS, pad_start, 0)
        lax.fori_loop(0, n_tail, tail_start, 0)
        lax.fori_loop(0, N_EXPERTS, pad_wait, 0)
        lax.fori_loop(0, n_tail, tail_wait, 0)

    def idx_copy(step, slot):
        return pltpu.make_async_copy(dest_ref.at[step], idx_ref.at[slot], isem)

    @pl.when(i == 0)
    def _():
        idx_copy(0, 0).start()

    slot = i % 2
    idx_copy(i, slot).wait()

    @pl.when(i + 1 < n_steps)
    def _():
        idx_copy(i + 1, 1 - slot).start()

    hb = tb // 2
    for half, (buf, sem) in enumerate(((sa_ref, sem_a), (sb_ref, sem_b))):
        def wait_rows(buf=buf, sem=sem):
            for _ in range(2):
                pltpu.make_async_copy(buf, xs_ref.at[pl.ds(0, hb)], sem).wait()

        @pl.when(i > 0)
        def _():
            wait_rows()

        buf[...] = hn_ref[half * hb:(half + 1) * hb]

        def issue(t, c, buf=buf, sem=sem, half=half):
            for s in range(2):
                d = idx_ref[slot, 2 * (half * hb + t) + s]
                pltpu.make_async_copy(buf.at[pl.ds(t, 1)], xs_ref.at[pl.ds(d, 1)], sem).start(priority=s)
            return c

        lax.fori_loop(0, hb, issue, 0, unroll=8)

    @pl.when(i == n_steps - 1)
    def _():
        for buf, sem in ((sa_ref, sem_a), (sb_ref, sem_b)):
            for _ in range(2):
                pltpu.make_async_copy(buf, xs_ref.at[pl.ds(0, hb)], sem).wait()


def _scatter(lastblk, nact, dest2, hn, tb, n_blocks):
    T, _, D = hn.shape
    n_tail = n_blocks - (-(-2 * T // MOE_BLOCK))
    hb = tb // 2
    return pl.pallas_call(
        functools.partial(_scatter_kernel, tb=tb, n_blocks=n_blocks, n_tail=n_tail, n_steps=T // tb),
        grid_spec=pltpu.PrefetchScalarGridSpec(
            num_scalar_prefetch=2, grid=(T // tb,),
            in_specs=[pl.BlockSpec(memory_space=pl.ANY),
                      pl.BlockSpec((tb, 1, D), lambda i, pad, na: (i, 0, 0))],
            out_specs=pl.BlockSpec(memory_space=pl.ANY),
            scratch_shapes=[pltpu.VMEM((MOE_BLOCK, 1, D), hn.dtype), pltpu.SMEM((2, 2 * tb), I32),
                            pltpu.VMEM((hb, 1, D), hn.dtype), pltpu.VMEM((hb, 1, D), hn.dtype),
                            pltpu.SemaphoreType.DMA, pltpu.SemaphoreType.DMA,
                            pltpu.SemaphoreType.DMA, pltpu.SemaphoreType.DMA]),
        out_shape=jax.ShapeDtypeStruct((n_blocks * MOE_BLOCK, 1, D), hn.dtype),
        compiler_params=_cparams("arbitrary"),
        name="moe_scatter",
    )(lastblk, nact, dest2, hn)


def _expert_kernel(be_ref, na_ref, x_ref, wg_hbm, wu_hbm, wd_hbm, y_ref,
                   wg_buf, wu_buf, wd_buf, wgb_ref, wub_ref, wdb_ref, x2_ref, slot_ref, sems):
    j = pl.program_id(0)
    na = na_ref[0]

    def weight_copies(e, slot):
        return [pltpu.make_async_copy(src.at[e], buf.at[slot], sems.at[slot, n])
                for n, (src, buf) in enumerate(((wg_hbm, wg_buf), (wu_hbm, wu_buf), (wd_hbm, wd_buf)))]

    @pl.when(j == 0)
    def _():
        slot_ref[0] = 0
        for cp in weight_copies(be_ref[0], 0):
            cp.start()

    @pl.when(j < na)
    def _():
        cur = be_ref[j]
        prev = be_ref[jnp.maximum(j - 1, 0)]

        @pl.when((j == 0) | (cur != prev))
        def _():
            slot = slot_ref[0]
            nxt = lax.while_loop(lambda k: (k < na) & (be_ref[jnp.minimum(k, na - 1)] == cur),
                                 lambda k: k + 1, j + 1)

            @pl.when(nxt < na)
            def _():
                for cp in weight_copies(be_ref[jnp.minimum(nxt, na - 1)], 1 - slot):
                    cp.start()

            for cp in weight_copies(cur, slot):
                cp.wait()
            wgb_ref[...] = wg_buf[slot].astype(BF16)
            wub_ref[...] = wu_buf[slot].astype(BF16)
            wdb_ref[...] = wd_buf[slot].astype(BF16)
            slot_ref[0] = 1 - slot

        x2_ref[...] = x_ref[...].reshape(x2_ref.shape)
        lo, hi = _unpack_halves(x2_ref[...])
        lo, hi = lo.astype(BF16), hi.astype(BF16)
        half = lo.shape[1]
        hg = _dot(lo, wgb_ref[:half, :]) + _dot(hi, wgb_ref[half:, :])
        hu = _dot(lo, wub_ref[:half, :]) + _dot(hi, wub_ref[half:, :])
        act = (hg * jax.nn.sigmoid(hg)) * hu
        y = _dot(act.astype(BF16), wdb_ref[...])
        y_ref[...] = _pack_halves(y).reshape(y_ref.shape)

    @pl.when(j >= na)
    def _():
        y_ref[...] = jnp.zeros_like(y_ref)


def _experts(blk_exp, nact, xs, w_gate, w_up, w_down, n_blocks):
    DP = xs.shape[2]
    D = 2 * DP
    rows = n_blocks * MOE_BLOCK
    DE = w_gate.shape[2]
    blk = lambda j, be, na: (jnp.minimum(j, na[0] - 1), 0, 0)
    hbm = pl.BlockSpec(memory_space=pl.ANY)
    return pl.pallas_call(
        _expert_kernel,
        grid_spec=pltpu.PrefetchScalarGridSpec(
            num_scalar_prefetch=2, grid=(n_blocks,),
            in_specs=[pl.BlockSpec((MOE_BLOCK, 1, DP), blk), hbm, hbm, hbm],
            out_specs=pl.BlockSpec((MOE_BLOCK, 1, DP), lambda j, be, na: (j, 0, 0)),
            scratch_shapes=[pltpu.VMEM((2, D, DE), F32), pltpu.VMEM((2, D, DE), F32), pltpu.VMEM((2, DE, D), F32),
                            pltpu.VMEM((D, DE), BF16), pltpu.VMEM((D, DE), BF16), pltpu.VMEM((DE, D), BF16),
                            pltpu.VMEM((MOE_BLOCK, DP), U32), pltpu.SMEM((1,), I32),
                            pltpu.SemaphoreType.DMA((2, 3))]),
        out_shape=jax.ShapeDtypeStruct((rows, 1, DP), U32),
        compiler_params=_cparams("arbitrary"),
        name="moe_experts",
    )(blk_exp, nact, xs, w_gate, w_up, w_down)


def _combine_kernel(dest_ref, ys_ref, h_ref, rec_ref, g_ref, o_ref, idx_ref, ya0_ref, ya1_ref, yb0_ref, yb1_ref,
                    y2_ref, sem_a, sem_b, isem, *, normalize, n_steps):
    i = pl.program_id(0)
    tb = h_ref.shape[0]
    hb = tb // 2
    halves = ((ya0_ref, ya1_ref, sem_a), (yb0_ref, yb1_ref, sem_b))

    def idx_copy(step, slot):
        return pltpu.make_async_copy(dest_ref.at[step], idx_ref.at[slot], isem)

    def issue(half, slot):
        bufs, sem = halves[half][:2], halves[half][2]

        def body(t, c):
            for s in range(2):
                d = idx_ref[slot, 2 * (half * hb + t) + s]
                pltpu.make_async_copy(ys_ref.at[pl.ds(d, 1)], bufs[s].at[pl.ds(t, 1)], sem).start(priority=s)
            return c

        lax.fori_loop(0, hb, body, 0, unroll=8)

    @pl.when(i == 0)
    def _():
        first = idx_copy(0, 0)
        first.start()
        first.wait()
        issue(0, 0)
        issue(1, 0)
        if n_steps > 1:
            idx_copy(1, 1).start()

    nslot = (i + 1) % 2

    @pl.when(i + 1 < n_steps)
    def _():
        idx_copy(i + 1, nslot).wait()

    rec = rec_ref[...]
    for half, (y0_ref, y1_ref, sem) in enumerate(halves):
        rows = slice(half * hb, (half + 1) * hb)
        pltpu.make_async_copy(ys_ref.at[pl.ds(0, hb)], y0_ref, sem).wait()
        pltpu.make_async_copy(ys_ref.at[pl.ds(0, hb)], y1_ref, sem).wait()
        y2_ref[...] = y0_ref[...].reshape(y2_ref.shape)
        h = h_ref[rows, :] + rec[rows, 2:3] * jnp.concatenate(_unpack_halves(y2_ref[...]), axis=1)
        y2_ref[...] = y1_ref[...].reshape(y2_ref.shape)
        h = h + rec[rows, 3:4] * jnp.concatenate(_unpack_halves(y2_ref[...]), axis=1)
        if normalize:
            ms = jnp.mean(h * h, axis=-1, keepdims=True)
            h = h * lax.rsqrt(ms + RMS_EPS) * g_ref[...]
        o_ref[rows, :] = h

        @pl.when(i + 1 < n_steps)
        def _():
            issue(half, nslot)

    @pl.when(i + 2 < n_steps)
    def _():
        idx_copy(i + 2, i % 2).start()


def _combine(dest2, ys, h, rec, g, tb, normalize):
    T, D = h.shape
    hb = tb // 2
    return pl.pallas_call(
        functools.partial(_combine_kernel, normalize=normalize, n_steps=T // tb),
        grid=(T // tb,),
        in_specs=[pl.BlockSpec(memory_space=pl.ANY), pl.BlockSpec(memory_space=pl.ANY),
                  pl.BlockSpec((tb, D), lambda i: (i, 0)), pl.BlockSpec((tb, LANES), lambda i: (i, 0)),
                  pl.BlockSpec((1, D), lambda i: (0, 0))],
        out_specs=pl.BlockSpec((tb, D), lambda i: (i, 0)),
        out_shape=jax.ShapeDtypeStruct((T, D), F32),
        scratch_shapes=[pltpu.SMEM((2, 2 * tb), I32)] + [pltpu.VMEM((hb, 1, D // 2), U32)] * 4 +
                       [pltpu.VMEM((hb, D // 2), U32)] + [pltpu.SemaphoreType.DMA] * 3,
        compiler_params=_cparams("arbitrary"),
        name="moe_combine",
    )(dest2, ys, h, rec, g)


def _pad_cols(a, n):
    return jnp.pad(a, ((0, 0), (0, n - a.shape[1])))


def _pad_rows(a, n, at=0):
    return jnp.pad(a, ((at, n - a.shape[0] - at), (0, 0)))


def _layer(x2, B, L, p):
    T, D = x2.shape
    W = p["s5_d"].shape[0]
    G, P = p["s5_lambda_re"].shape
    HG = W // G
    dl, al, gl = p["rwkv_w2"].shape[0], p["rwkv_a2"].shape[0], p["rwkv_g2"].shape[0]
    H = W // RW_N

    w_in = p["w_in"]
    o = W
    cols = [w_in[:, :W], w_in[:, o:o + 3 * W]]
    o += 3 * W
    cols += [_pad_cols(w_in[:, o:o + dl], 128), _pad_cols(w_in[:, o + dl:o + dl + al], 128),
             _pad_cols(w_in[:, o + dl + al:o + dl + al + gl], 256)]
    w_in_r = jnp.concatenate(cols, axis=1).astype(BF16)
    tm_in = min(1024, T)
    proj = _inproj(x2, p["norm_mix_g"].reshape(1, D), w_in_r, tm_in, 1152)

    C = S5_CHUNK
    nc = L // C
    ct_re = jnp.swapaxes(p["s5_c_re"], 1, 2)
    ct_im = jnp.swapaxes(p["s5_c_im"], 1, 2)
    m_op, wbt, wct, a64 = _s5_ops(p["s5_lambda_re"], p["s5_lambda_im"], p["s5_log_step"],
                                  ct_re, ct_im, p["s5_b_re"], p["s5_b_im"], p["s5_c_re"], p["s5_c_im"])
    rb = min(LANES, B * nc)
    z = _s5_in(proj.reshape(B * nc, C, proj.shape[1]), G, rb)
    y_t = _s5_chunk(z, m_op, wbt, wct, a64, nc)
    y_ssm = _s5_out(y_t, rb).reshape(T, W)
    tm = min(512, T)
    s5_out = _s5_glu(y_ssm, proj, p["s5_d"].reshape(1, W), p["s5_w_glu"].astype(BF16),
                     p["s5_b_glu"].reshape(1, W), tm)

    mu = p["rwkv_mu"]
    mu3 = _pad_rows(mu[:3 * W].reshape(3, W), 8)
    o = 3 * W
    mul = jnp.concatenate([_pad_cols(mu[None, o:o + dl], 128), _pad_cols(mu[None, o + dl:o + dl + al], 128),
                           _pad_cols(mu[None, o + dl + al:], 256)], axis=1)
    w2p = _pad_rows(p["rwkv_w2"], 128).astype(BF16)
    a2p = _pad_rows(p["rwkv_a2"], 128).astype(BF16)
    g2p = _pad_rows(p["rwkv_g2"], 256).astype(BF16)
    head_of = jnp.arange(W, dtype=I32) // RW_N
    e_mat = (head_of[:, None] == jnp.arange(LANES, dtype=I32)[None, :]).astype(BF16)
    et_mat = e_mat.T
    row = lambda a: a.reshape(1, W)
    tm_rw = min(256, L)
    r, k, v, nk, bv, lw, g = _rwkv_prep(proj, B, L, mu3, mul, row(p["rwkv_w0"]), row(p["rwkv_a0"]),
                                        row(p["rwkv_k_k"]), row(p["rwkv_k_a"]), w2p, a2p, g2p,
                                        e_mat, et_mat, tm_rw)
    rw_out = _rwkv_chunk(r, k, v, nk, bv, lw, g, p["rwkv_r_k"].reshape(1, W), row(p["rwkv_ln_w"]),
                         row(p["rwkv_ln_b"]), B, L)

    w_out = p["w_out"].astype(BF16)
    w_route = jnp.concatenate([p["w_route_exp"], p["w_route_grp"]], axis=1)
    w_route = _pad_cols(w_route, LANES)
    wr_hi = w_route.astype(BF16)
    wr_lo = (w_route - wr_hi.astype(F32)).astype(BF16)
    b_route = _pad_cols(jnp.concatenate([p["b_route_exp"], p["b_route_grp"]])[None, :], LANES)
    tm_o = min(256, T)
    h, hn, rec = _outproj(s5_out, rw_out, x2, w_out[:W], w_out[W:], p["norm_ffn_g"].reshape(1, D),
                          wr_hi, wr_lo, b_route, tm_o)

    tb = min(1024, T)
    rank, cnt = _rank(rec, tb)
    n_blocks = -(-2 * T // MOE_BLOCK) + N_EXPERTS
    nb_rows = -(-(n_blocks + 1) // 8) * 8
    dest, blk = _dest(rec, rank, cnt, tb, nb_rows)
    blk_exp = blk[:n_blocks, 0]
    nact = blk[0:1, 1]
    lastblk = blk[nb_rows - 1, :N_EXPERTS]
    ts = min(256, T)
    dest2 = dest[:, :2].reshape(T // ts, 2 * ts)

    xs = _scatter(lastblk, nact, dest2, hn, ts, n_blocks)
    ys = _experts(blk_exp, nact, xs, p["w_gate"], p["w_up"], p["w_down"], n_blocks)
    return dest2, ys, h, rec, ts


def kernel(x, norm_mix_g, w_in, s5_lambda_re, s5_lambda_im, s5_log_step, s5_b_re, s5_b_im, s5_c_re, s5_c_im, s5_d, s5_w_glu, s5_b_glu, rwkv_mu, rwkv_w0, rwkv_w2, rwkv_a0, rwkv_a2, rwkv_g2, rwkv_k_k, rwkv_k_a, rwkv_r_k, rwkv_ln_w, rwkv_ln_b, w_out, norm_ffn_g, w_route_grp, b_route_grp, w_route_exp, b_route_exp, w_gate, w_up, w_down, norm_final_g):
    B, L, D = x.shape
    params = dict(
        norm_mix_g=norm_mix_g, w_in=w_in, s5_lambda_re=s5_lambda_re, s5_lambda_im=s5_lambda_im,
        s5_log_step=s5_log_step, s5_b_re=s5_b_re, s5_b_im=s5_b_im, s5_c_re=s5_c_re, s5_c_im=s5_c_im,
        s5_d=s5_d, s5_w_glu=s5_w_glu, s5_b_glu=s5_b_glu, rwkv_mu=rwkv_mu, rwkv_w0=rwkv_w0, rwkv_w2=rwkv_w2,
        rwkv_a0=rwkv_a0, rwkv_a2=rwkv_a2, rwkv_g2=rwkv_g2, rwkv_k_k=rwkv_k_k, rwkv_k_a=rwkv_k_a,
        rwkv_r_k=rwkv_r_k, rwkv_ln_w=rwkv_ln_w, rwkv_ln_b=rwkv_ln_b, w_out=w_out, norm_ffn_g=norm_ffn_g,
        w_route_grp=w_route_grp, b_route_grp=b_route_grp, w_route_exp=w_route_exp, b_route_exp=b_route_exp,
        w_gate=w_gate, w_up=w_up, w_down=w_down)
    depth = norm_mix_g.shape[0]
    h2 = x.reshape(B * L, D)
    for l in range(depth):
        p = {k_: v_[l] for k_, v_ in params.items()}
        dest2, ys, h, rec, ts = _layer(h2, B, L, p)
        h2 = _combine(dest2, ys, h, rec, norm_final_g.reshape(1, D), ts, normalize=(l == depth - 1))
    return h2.reshape(B, L, D)
```

```python
import functools

import jax
import jax.numpy as jnp
from jax import lax
from jax.experimental import pallas as pl
from jax.experimental.pallas import tpu as pltpu

F32 = jnp.float32
BF16 = jnp.bfloat16
I32 = jnp.int32

RMS_EPS = 1e-6
S5_CHUNK = 64
S5_HG = 16
S5_P = 64
RW_N = 64
RW_CHUNK = 64
RW_GN_EPS = 64e-5
N_GROUPS = 8
EPG = 8
N_EXPERTS = 64
MOE_BLOCK = 256
LANES = 128
VMEM_LIMIT = 56 * 1024 * 1024


def _cparams(*sem, **kw):
    return pltpu.CompilerParams(dimension_semantics=sem, vmem_limit_bytes=VMEM_LIMIT, **kw)


def _split2(x):
    hi = x.astype(BF16)
    lo = (x - hi.astype(F32)).astype(BF16)
    return hi, lo


def _split3(x):
    hi = x.astype(BF16)
    r = x - hi.astype(F32)
    mid = r.astype(BF16)
    lo = (r - mid.astype(F32)).astype(BF16)
    return hi, mid, lo


def _dot(a, b):
    return jnp.dot(a, b, preferred_element_type=F32)


def _dot_nt(a, b):
    return lax.dot_general(a, b, (((1,), (1,)), ((), ())), preferred_element_type=F32)


def _dot_tn(a, b):
    return lax.dot_general(a, b, (((0,), (0,)), ((), ())), preferred_element_type=F32)


def _dot_x3(a, b):
    ah, al = _split2(a)
    bh, bl = _split2(b)
    return _dot(ah, bh) + (_dot(ah, bl) + _dot(al, bh))


def _dot_exact_lhs(a_bf16, b):
    bh, bm, bl = _split3(b)
    return _dot(a_bf16, bh) + (_dot(a_bf16, bm) + _dot(a_bf16, bl))


def _cmul(ar, ai, br, bi):
    return ar * br - ai * bi, ar * bi + ai * br


U32 = jnp.uint32


def _pack_halves(x):
    half = x.shape[1] // 2
    lo = lax.bitcast_convert_type(x[:, :half].astype(BF16).astype(F32), U32)
    hi = lax.bitcast_convert_type(x[:, half:].astype(BF16).astype(F32), U32)
    return (lo >> 16) | hi


def _unpack_halves(p):
    lo = lax.bitcast_convert_type(p << 16, F32)
    hi = lax.bitcast_convert_type(p & jnp.uint32(0xFFFF0000), F32)
    return lo, hi


def _inproj_kernel(x_ref, g_ref, w_ref, o_ref, hn_ref):
    @pl.when(pl.program_id(1) == 0)
    def _():
        x = x_ref[...]
        ms = jnp.mean(x * x, axis=-1, keepdims=True)
        hn_ref[...] = (x * lax.rsqrt(ms + RMS_EPS) * g_ref[...]).astype(BF16)

    o_ref[...] = _dot(hn_ref[...], w_ref[...])


def _inproj(x2, g, w_bf16, tm, tn):
    T, D = x2.shape
    N = w_bf16.shape[1]
    return pl.pallas_call(
        _inproj_kernel,
        grid=(T // tm, N // tn),
        in_specs=[
            pl.BlockSpec((tm, D), lambda i, j: (i, 0)),
            pl.BlockSpec((1, D), lambda i, j: (0, 0)),
            pl.BlockSpec((D, tn), lambda i, j: (0, j)),
        ],
        out_specs=pl.BlockSpec((tm, tn), lambda i, j: (i, j)),
        out_shape=jax.ShapeDtypeStruct((T, N), F32),
        scratch_shapes=[pltpu.VMEM((tm, D), BF16)],
        compiler_params=_cparams("arbitrary", "arbitrary"),
        name="inproj",
    )(x2, g, w_bf16)


def _binpow(ar, ai, expo, nbits):
    pr = jnp.ones(expo.shape, F32)
    pi = jnp.zeros(expo.shape, F32)
    sr, si = ar, ai
    for bit in range(nbits):
        m = ((expo >> bit) & 1) == 1
        nr, ni = _cmul(pr, pi, sr, si)
        pr = jnp.where(m, nr, pr)
        pi = jnp.where(m, ni, pi)
        if bit + 1 < nbits:
            sr, si = _cmul(sr, si, sr, si)
    return pr, pi


def _frame_powers(ar, ai, reverse):
    C, HG, P = S5_CHUNK, S5_HG, ar.shape[0]
    fpt = LANES // HG
    j = lax.broadcasted_iota(I32, (P, LANES), 1) // HG
    inner_r, inner_i = _binpow(ar, ai, (fpt - 1 - j) if reverse else j, (fpt - 1).bit_length())
    sr, si = ar, ai
    for _ in range(fpt.bit_length() - 1):
        sr, si = _cmul(sr, si, sr, si)
    outer = [(jnp.ones_like(ar), jnp.zeros_like(ai))]
    for _ in range(C // fpt - 1):
        outer.append(_cmul(outer[-1][0], outer[-1][1], sr, si))
    if reverse:
        outer = outer[::-1]
    tiles = [_cmul(inner_r, inner_i, o_r, o_i) for o_r, o_i in outer]
    return (jnp.concatenate([t[0] for t in tiles], axis=1), jnp.concatenate([t[1] for t in tiles], axis=1))


def _zoh(lr, li, dt):
    mag = jnp.exp(lr * dt)
    ang = li * dt
    ar, ai = mag * jnp.cos(ang), mag * jnp.sin(ang)
    den = lr * lr + li * li
    nr, ni = ar - 1.0, ai
    fr = (nr * lr + ni * li) / den
    fi = (ni * lr - nr * li) / den
    return ar, ai, fr, fi


def _s5_ops_kernel(lrc_ref, lic_ref, dt_ref, ct_re_ref, ct_im_ref, bt_re_ref, bt_im_ref, c_re_ref, c_im_ref,
                   m_ref, wbt_ref, wct_ref, a64_ref):
    C, HG, P = S5_CHUNK, S5_HG, S5_P
    W = C * HG
    dt = jnp.exp(dt_ref[0])
    ar_c, ai_c, fr_c, fi_c = _zoh(lrc_ref[0], lic_ref[0], dt)

    pr, pi = _frame_powers(ar_c, ai_c, False)
    qr, qi = _frame_powers(ar_c, ai_c, True)

    sel = (lax.broadcasted_iota(I32, (HG, W), 1) % HG == lax.broadcasted_iota(I32, (HG, W), 0)).astype(BF16)
    tile = lambda ref: _dot_exact_lhs_rhs(ref[0], sel)

    ca_re, ca_im = _cmul(tile(ct_re_ref), tile(ct_im_ref), pr, pi)
    c1_re, c1_im = _cmul(ca_re, ca_im, ar_c, ai_c)
    wct_ref[0] = jnp.concatenate([c1_re, -c1_im], axis=0).T.astype(BF16)

    bb_re, bb_im = _cmul(tile(bt_re_ref), tile(bt_im_ref), fr_c, fi_c)
    ab_re, ab_im = _cmul(bb_re, bb_im, qr, qi)
    wbt_ref[0, :P, :] = ab_re.astype(BF16)
    wbt_ref[0, P:, :] = ab_im.astype(BF16)

    cr, ci = ar_c, ai_c
    for _ in range(C.bit_length() - 1):
        cr, ci = _cmul(cr, ci, cr, ci)
    a64_ref[0] = jnp.concatenate([cr, ci], axis=0)

    strip = _dot_x3(c_re_ref[0], ab_re) - _dot_x3(c_im_ref[0], ab_im)
    ext = jnp.concatenate([strip, jnp.zeros((HG, W), F32)], axis=1)
    for t in range(C):
        off = (C - 1 - t) * HG
        m_ref[0, t * HG:(t + 1) * HG, :] = ext[:, off:off + W].astype(BF16)


def _s5_ops(lam_re, lam_im, log_step, ct_re, ct_im, bt_re, bt_im, c_re, c_im):
    G, P = lam_re.shape
    C, HG = S5_CHUNK, S5_HG
    W = C * HG
    col = lambda a: a.reshape(G, P, 1)
    g3 = lambda s1, s2: pl.BlockSpec((1, s1, s2), lambda g: (g, 0, 0))
    return pl.pallas_call(
        _s5_ops_kernel,
        grid=(G,),
        in_specs=[g3(P, 1), g3(P, 1), g3(1, 1),
                  g3(P, HG), g3(P, HG), g3(P, HG), g3(P, HG), g3(HG, P), g3(HG, P)],
        out_specs=[g3(W, W), g3(2 * P, W), g3(W, 2 * P), g3(2 * P, 1)],
        out_shape=[jax.ShapeDtypeStruct((G, W, W), BF16),
                   jax.ShapeDtypeStruct((G, 2 * P, W), BF16),
                   jax.ShapeDtypeStruct((G, W, 2 * P), BF16),
                   jax.ShapeDtypeStruct((G, 2 * P, 1), F32)],
        compiler_params=_cparams("arbitrary"),
        name="s5_ops",
    )(col(lam_re), col(lam_im), log_step.reshape(G, 1, 1), ct_re, ct_im, bt_re, bt_im, c_re, c_im)


S5_FB = 8


def _s5_in_kernel(x_ref, z_ref):
    G, _, rb = z_ref.shape
    for f in range(S5_FB):
        xt = x_ref[:, f, :].T
        z_ref[:, f * S5_HG:(f + 1) * S5_HG, :] = xt.reshape(G, S5_HG, rb).astype(BF16)


def _s5_in(proj3, G, rb):
    R, C, _ = proj3.shape
    W = G * S5_HG
    return pl.pallas_call(
        _s5_in_kernel,
        grid=(R // rb, C // S5_FB),
        in_specs=[pl.BlockSpec((rb, S5_FB, W), lambda i, j: (i, j, 0))],
        out_specs=pl.BlockSpec((G, S5_FB * S5_HG, rb), lambda i, j: (0, j, i)),
        out_shape=jax.ShapeDtypeStruct((G, C * S5_HG, R), BF16),
        compiler_params=_cparams("arbitrary", "arbitrary"),
        name="s5_in",
    )(proj3)


def _s5_out_kernel(yt_ref, o_ref):
    G, _, rb = yt_ref.shape
    for f in range(S5_FB):
        slab = yt_ref[:, f * S5_HG:(f + 1) * S5_HG, :].reshape(G * S5_HG, rb)
        o_ref[:, f, :] = slab.T


def _s5_out(yt, rb):
    G, CW, R = yt.shape
    C = CW // S5_HG
    W = G * S5_HG
    return pl.pallas_call(
        _s5_out_kernel,
        grid=(R // rb, C // S5_FB),
        in_specs=[pl.BlockSpec((G, S5_FB * S5_HG, rb), lambda i, j: (0, j, i))],
        out_specs=pl.BlockSpec((rb, S5_FB, W), lambda i, j: (i, j, 0)),
        out_shape=jax.ShapeDtypeStruct((R, C, W), F32),
        compiler_params=_cparams("arbitrary", "arbitrary"),
        name="s5_out",
    )(yt)


def _s5_chunk_kernel(z_ref, m_ref, wbt_ref, wct_ref, a64_ref, y_ref, *, n_chunks):
    for g in range(z_ref.shape[0]):
        _s5_chunk_group(z_ref.at[g], m_ref.at[g], wbt_ref.at[g], wct_ref.at[g], a64_ref.at[g], y_ref.at[g],
                        n_chunks)


def _s5_chunk_group(z_ref, m_ref, wbt_ref, wct_ref, a64_ref, y_ref, n_chunks):
    P = S5_P
    z = z_ref[...]
    R = z.shape[1]
    x = _dot(wbt_ref[...], z)
    y_ref[...] = _dot(m_ref[...], z)
    xr, xi = x[:P], x[P:]
    a = a64_ref[...]
    ar, ai = a[:P], a[P:]
    cidx = lax.broadcasted_iota(I32, (1, R), 1) % n_chunks
    shift = 1
    while shift < n_chunks:
        keep = cidx >= shift
        sr = jnp.where(keep, pltpu.roll(xr, shift, 1), 0.0)
        si = jnp.where(keep, pltpu.roll(xi, shift, 1), 0.0)
        xr, xi = xr + (ar * sr - ai * si), xi + (ar * si + ai * sr)
        ar, ai = ar * ar - ai * ai, 2.0 * ar * ai
        shift *= 2
    keep = cidx >= 1
    s_in = jnp.concatenate([jnp.where(keep, pltpu.roll(xr, 1, 1), 0.0),
                            jnp.where(keep, pltpu.roll(xi, 1, 1), 0.0)], axis=0)
    sh, sl = _split2(s_in)
    wct = wct_ref[...]
    y_ref[...] = y_ref[...] + (_dot(wct, sh) + _dot(wct, sl))


def _s5_chunk(z, m, wbt, wct, a64, n_chunks):
    G, W, R = z.shape
    P2 = wbt.shape[1]
    gps = 2 if G % 2 == 0 else 1
    g3 = lambda s1, s2: pl.BlockSpec((gps, s1, s2), lambda g: (g, 0, 0))
    return pl.pallas_call(
        functools.partial(_s5_chunk_kernel, n_chunks=n_chunks),
        grid=(G // gps,),
        in_specs=[g3(W, R), g3(W, W), g3(P2, W), g3(W, P2), g3(P2, 1)],
        out_specs=g3(W, R),
        out_shape=jax.ShapeDtypeStruct((G, W, R), F32),
        compiler_params=_cparams("arbitrary"),
        name="s5_chunk",
    )(z, m, wbt, wct, a64)


def _gelu_tanh(x):
    return 0.5 * x * (1.0 + jnp.tanh(0.7978845608028654 * (x + 0.044715 * (x * x * x))))


def _s5_glu_kernel(y_ref, u_ref, d_ref, w_ref, b_ref, o_ref):
    y = y_ref[...] + d_ref[...] * u_ref[...]
    y = _gelu_tanh(y)
    z = _dot(y.astype(BF16), w_ref[...]) + b_ref[...]
    o_ref[...] = (y * jax.nn.sigmoid(z)).astype(BF16)


def _s5_glu(y_ssm, proj, d, w_bf16, b, tm):
    T, W = y_ssm.shape
    return pl.pallas_call(
        _s5_glu_kernel,
        grid=(T // tm,),
        in_specs=[
            pl.BlockSpec((tm, W), lambda i: (i, 0)),
            pl.BlockSpec((tm, W), lambda i: (i, 0)),
            pl.BlockSpec((1, W), lambda i: (0, 0)),
            pl.BlockSpec((W, W), lambda i: (0, 0)),
            pl.BlockSpec((1, W), lambda i: (0, 0)),
        ],
        out_specs=pl.BlockSpec((tm, W), lambda i: (i, 0)),
        out_shape=jax.ShapeDtypeStruct((T, W), BF16),
        compiler_params=_cparams("arbitrary"),
        name="s5_glu",
    )(y_ssm, proj, d, w_bf16, b)


def _head_sum(x, e_ref, et_ref):
    xh, xl = _split2(x)
    e = e_ref[...]
    s = _dot(xh, e) + _dot(xl, e)
    sh, sl = _split2(s)
    et = et_ref[...]
    return _dot(sh, et) + _dot(sl, et)


def _shift(z, prev_row):
    rolled = pltpu.roll(z, 1, 0)
    first = lax.broadcasted_iota(I32, (z.shape[0], 1), 0) == 0
    return jnp.where(first, prev_row, rolled)


def _rwkv_prep_kernel(zr_ref, zk_ref, zv_ref, zl_ref, mu_ref, mul_ref, w0_ref, a0_ref, kk_ref, ka_ref,
                      w2_ref, a2_ref, g2_ref, e_ref, et_ref,
                      r_ref, k_ref, v_ref, nk_ref, b_ref, lw_ref, g_ref,
                      car_ref, carl_ref):
    W = r_ref.shape[1]

    @pl.when(pl.program_id(1) == 0)
    def _():
        car_ref[...] = jnp.zeros_like(car_ref)
        carl_ref[...] = jnp.zeros_like(carl_ref)

    def lerp(z, prev_row, mu):
        return z + (_shift(z, prev_row) - z) * mu

    tm = zr_ref.shape[0]
    zr, zk, zv, zl = zr_ref[...], zk_ref[...], zv_ref[...], zl_ref[...]
    r = lerp(zr, car_ref[0:1, :], mu_ref[0:1, :])
    k = lerp(zk, car_ref[1:2, :], mu_ref[1:2, :])
    v = lerp(zv, car_ref[2:3, :], mu_ref[2:3, :])
    xl = lerp(zl, carl_ref[0:1, :], mul_ref[...])
    car_ref[0:1, :] = zr[tm - 1:tm, :]
    car_ref[1:2, :] = zk[tm - 1:tm, :]
    car_ref[2:3, :] = zv[tm - 1:tm, :]
    carl_ref[0:1, :] = zl[tm - 1:tm, :]

    xw, xa, xg = xl[:, 0:128], xl[:, 128:256], xl[:, 256:512]
    dw = _dot(jnp.tanh(xw).astype(BF16), w2_ref[...])
    da = _dot(xa.astype(BF16), a2_ref[...])
    g = _dot(jax.nn.sigmoid(xg).astype(BF16), g2_ref[...])

    zw = -(w0_ref[...] + dw)
    softplus = jnp.maximum(zw, 0.0) + jnp.log(1.0 + jnp.exp(-jnp.abs(zw)))
    w_log = -softplus - 0.5
    a = jax.nn.sigmoid(a0_ref[...] + da)

    kk = k * kk_ref[...]
    n2 = _head_sum(kk * kk, e_ref, et_ref)
    kk = kk / jnp.maximum(jnp.sqrt(n2), 1e-12)

    r_ref[...] = r.astype(BF16)
    k_ref[...] = (k * (1.0 + (a - 1.0) * ka_ref[...])).astype(BF16)
    v_ref[...] = v.astype(BF16)
    nk_ref[...] = kk.astype(BF16)
    b_ref[...] = (kk * a).astype(BF16)
    lw_ref[...] = -jnp.exp(w_log)
    g_ref[...] = g.astype(BF16)


def _rwkv_prep(proj, B, L, mu3, mul, w0, a0, k_k, k_a, w2p, a2p, g2p, e_mat, et_mat, tm):
    T = B * L
    W = w0.shape[1]
    nt = L // tm
    row = lambda c: pl.BlockSpec((tm, W), lambda b, i, c=c: (b * nt + i, c))
    full = lambda a: pl.BlockSpec(a.shape, lambda b, i: (0, 0))
    out = pl.BlockSpec((tm, W), lambda b, i: (b * nt + i, 0))
    lw = 512
    return pl.pallas_call(
        _rwkv_prep_kernel,
        grid=(B, nt),
        in_specs=[row(1), row(2), row(3),
                  pl.BlockSpec((tm, lw), lambda b, i: (b * nt + i, 4 * W // lw)),
                  full(mu3), full(mul), full(w0), full(a0), full(k_k), full(k_a),
                  full(w2p), full(a2p), full(g2p), full(e_mat), full(et_mat)],
        out_specs=[out] * 7,
        out_shape=[jax.ShapeDtypeStruct((T, W), dt) for dt in (BF16, BF16, BF16, BF16, BF16, F32, BF16)],
        scratch_shapes=[pltpu.VMEM((8, W), F32), pltpu.VMEM((8, lw), F32)],
        compiler_params=_cparams("arbitrary", "arbitrary"),
        name="rwkv_prep",
    )(proj, proj, proj, proj, mu3, mul, w0, a0, k_k, k_a, w2p, a2p, g2p, e_mat, et_mat)


def _rwkv_chunk_kernel(r_ref, k_ref, v_ref, nk_ref, b_ref, lw_ref, g_ref, rk_ref, lnw_ref, lnb_ref,
                       o_ref, z_ref, ys_ref, qs_ref, vs_ref, gs_ref):
    C, N = RW_CHUNK, RW_N
    H = r_ref.shape[1] // N

    @pl.when(pl.program_id(1) == 0)
    def _():
        z_ref[...] = jnp.zeros_like(z_ref)
        ys_ref[...] = jnp.zeros_like(ys_ref)
        qs_ref[...] = jnp.zeros_like(qs_ref)
        vs_ref[...] = jnp.zeros_like(vs_ref)
        gs_ref[...] = jnp.zeros_like(gs_ref)

    low = lax.broadcasted_iota(I32, (C, 2 * N), 1) < N
    inv_n = 1.0 / N

    def head_sums(t):
        s0 = jnp.sum(jnp.where(low, t, 0.0), axis=-1, keepdims=True)
        s1 = jnp.sum(jnp.where(low, 0.0, t), axis=-1, keepdims=True)
        return jnp.where(low, s0, s1)

    for p in range(H // 2):
        ps = slice(2 * p * N, 2 * (p + 1) * N)
        yp = ys_ref[:, ps]
        yc = yp - head_sums(yp) * inv_n
        var = head_sums(yc * yc) * inv_n
        yn = yc * lax.rsqrt(var + RW_GN_EPS) * lnw_ref[:, ps] + lnb_ref[:, ps]
        bonus = head_sums(qs_ref[:, ps]) * vs_ref[:, ps].astype(F32)
        o_ref[:, ps] = ((yn + bonus) * gs_ref[:, ps].astype(F32)).astype(BF16)

    ri = lax.broadcasted_iota(I32, (C, C), 0)
    ci = lax.broadcasted_iota(I32, (C, C), 1)
    tril = ri >= ci
    stril = ri > ci
    eye = (ri == ci).astype(F32)

    lw = lw_ref[...]
    cs = _dot_exact_lhs(tril.astype(BF16), lw)
    tot = cs[C - 1:C, :]
    p_inc = jnp.exp(cs)
    p_inv = jnp.exp(-cs)
    p_prev = jnp.exp(cs - lw)
    p_rest = jnp.exp(tot - cs)
    p_tot = jnp.exp(tot)

    r_f, k_f, b_f = r_ref[...].astype(F32), k_ref[...].astype(F32), b_ref[...].astype(F32)
    r_t = r_f * p_inc
    k_t = k_f * p_inv
    a_t = -nk_ref[...].astype(F32) * p_prev
    b_t = b_f * p_inv
    k_h = k_f * p_rest
    b_h = b_f * p_rest
    v_all = v_ref[...]

    hs = range(H)
    sl = [slice(h * N, (h + 1) * N) for h in hs]
    each = lambda f: [f(h) for h in hs]
    v = each(lambda h: v_all[:, sl[h]].astype(BF16))
    ar = each(lambda h: jnp.concatenate([a_t[:, sl[h]], r_t[:, sl[h]]], axis=0).astype(BF16))
    m_b = each(lambda h: _dot_nt(ar[h], b_t[:, sl[h]].astype(BF16)))
    m_k = each(lambda h: _dot_nt(ar[h], k_t[:, sl[h]].astype(BF16)))
    l_ab = each(lambda h: jnp.where(stril, m_b[h][:C], 0.0).astype(BF16))
    m_rb = each(lambda h: jnp.where(tril, m_b[h][C:], 0.0).astype(BF16))
    both = jnp.concatenate([stril, tril], axis=0)
    lm_k = each(lambda h: jnp.where(both, m_k[h], 0.0).astype(BF16))
    lmv = each(lambda h: _dot(lm_k[h], v[h]))

    x = each(lambda h: jnp.concatenate([lmv[h][:C], a_t[:, sl[h]]], axis=1))
    y0 = each(lambda h: lmv[h][C:])
    hk = each(lambda h: _dot_tn(k_h[:, sl[h]].astype(BF16), v[h]))
    lp = l_ab
    step = 1
    while step < C:
        x = each(lambda h: x[h] + _dot(lp[h], x[h].astype(BF16)))
        step *= 2
        if step < C:
            lp = each(lambda h: _dot(lp[h], lp[h]).astype(BF16))
    xb = each(lambda h: x[h].astype(BF16))
    yq = each(lambda h: _dot(m_rb[h], xb[h]))
    gh = each(lambda h: _dot_tn(b_h[:, sl[h]].astype(BF16), xb[h]))

    z = each(lambda h: _split2(z_ref[h]))
    qg = each(lambda h: jnp.concatenate([yq[h][:, N:] + r_t[:, sl[h]],
                                         gh[h][:, N:] + eye * p_tot[:, sl[h]]], axis=0).astype(BF16))
    qgz = each(lambda h: _dot(qg[h], z[h][0]) + _dot(qg[h], z[h][1]))
    yz = each(lambda h: qgz[h][:C])
    gz = each(lambda h: qgz[h][C:])
    for h in hs:
        z_ref[h] = (gh[h][:, :N] + hk[h]) + gz[h]
        ys_ref[:, sl[h]] = (yq[h][:, :N] + y0[h]) + yz[h]
    qs_ref[...] = r_f * k_f * rk_ref[...]
    vs_ref[...] = v_all
    gs_ref[...] = g_ref[...]


def _rwkv_chunk(r, k, v, nk, b, lw, g, r_k, ln_w, ln_b, B, L):
    T, W = r.shape
    C = RW_CHUNK
    nc = L // C
    H = W // RW_N
    assert H % 2 == 0
    spec = pl.BlockSpec((C, W), lambda bi, c: (bi * nc + jnp.minimum(c, nc - 1), 0))
    full = pl.BlockSpec((1, W), lambda bi, c: (0, 0))
    return pl.pallas_call(
        _rwkv_chunk_kernel,
        grid=(B, nc + 1),
        in_specs=[spec] * 7 + [full] * 3,
        out_specs=pl.BlockSpec((C, W), lambda bi, c: (bi * nc + jnp.maximum(c - 1, 0), 0)),
        out_shape=jax.ShapeDtypeStruct((T, W), BF16),
        scratch_shapes=[pltpu.VMEM((H, RW_N, RW_N), F32), pltpu.VMEM((C, W), F32), pltpu.VMEM((C, W), F32),
                        pltpu.VMEM((C, W), BF16), pltpu.VMEM((C, W), BF16)],
        compiler_params=_cparams("arbitrary", "arbitrary"),
        name="rwkv_chunk",
    )(r, k, v, nk, b, lw, g, r_k, ln_w, ln_b)


def _first_index_of_max(vals, lane, valid):
    neg = jnp.float32(-jnp.inf)
    masked = jnp.where(valid, vals, neg)
    m = jnp.max(masked, axis=-1, keepdims=True)
    idx = jnp.min(jnp.where(valid & (masked == m), lane, LANES), axis=-1, keepdims=True)
    return m, idx


def _outproj_kernel(s5_ref, rw_ref, x_ref, wt_ref, wb_ref, g_ref, wr_ref, br_ref,
                    h_ref, hn_ref, rec_ref):
    h = x_ref[...] + (_dot(s5_ref[...], wt_ref[...]) + _dot(rw_ref[...], wb_ref[...]))
    h_ref[...] = h
    ms = jnp.mean(h * h, axis=-1, keepdims=True)
    hn = h * lax.rsqrt(ms + RMS_EPS) * g_ref[...]
    hn_ref[...] = _pack_halves(hn).reshape(hn_ref.shape)

    hh, hl = _split2(hn)
    wh, wl = wr_ref[:, :LANES], wr_ref[:, LANES:]
    logits = _dot(hh, wh) + (_dot(hh, wl) + _dot(hl, wh)) + br_ref[...]
    lane = lax.broadcasted_iota(I32, logits.shape, 1)
    is_grp = (lane >= N_EXPERTS) & (lane < N_EXPERTS + N_GROUPS)
    gmax, gidx = _first_index_of_max(logits, lane, is_grp)
    gsum = jnp.sum(jnp.where(is_grp, jnp.exp(logits - gmax), 0.0), axis=-1, keepdims=True)
    p_grp = 1.0 / gsum
    grp = gidx - N_EXPERTS
    in_grp = (lane >= grp * EPG) & (lane < (grp + 1) * EPG)
    m1, i1 = _first_index_of_max(logits, lane, in_grp)
    m2, i2 = _first_index_of_max(logits, lane, in_grp & (lane != i1))
    e = jnp.exp(m2 - m1)
    g1 = p_grp / (1.0 + e)
    g2 = p_grp * e / (1.0 + e)
    rec = jnp.where(lane == 0, i1.astype(F32),
          jnp.where(lane == 1, i2.astype(F32),
          jnp.where(lane == 2, g1, jnp.where(lane == 3, g2, 0.0))))
    rec_ref[...] = rec


def _outproj(s5o, rwo, x2, w_top, w_bot, g, wr, b_route, tm):
    T, D = x2.shape
    W = s5o.shape[1]
    full = lambda a: pl.BlockSpec(a.shape, lambda i: (0, 0))
    return pl.pallas_call(
        _outproj_kernel,
        grid=(T // tm,),
        in_specs=[pl.BlockSpec((tm, W), lambda i: (i, 0)), pl.BlockSpec((tm, W), lambda i: (i, 0)),
                  pl.BlockSpec((tm, D), lambda i: (i, 0)),
                  full(w_top), full(w_bot), full(g), full(wr), full(b_route)],
        out_specs=[pl.BlockSpec((tm, D), lambda i: (i, 0)), pl.BlockSpec((tm, 1, D // 2), lambda i: (i, 0, 0)),
                   pl.BlockSpec((tm, LANES), lambda i: (i, 0))],
        out_shape=[jax.ShapeDtypeStruct((T, D), F32), jax.ShapeDtypeStruct((T, 1, D // 2), U32),
                   jax.ShapeDtypeStruct((T, LANES), F32)],
        compiler_params=_cparams("arbitrary"),
        name="outproj_route",
    )(s5o, rwo, x2, w_top, w_bot, g, wr, b_route)


def _onehots(rec, lane):
    oh0 = (lane == rec[:, 0:1].astype(I32)).astype(F32)
    oh1 = (lane == rec[:, 1:2].astype(I32)).astype(F32)
    return oh0, oh1


def _rank_kernel(rec_ref, rank_ref, cnt_ref, base_ref):
    tb = rec_ref.shape[0]

    @pl.when(pl.program_id(0) == 0)
    def _():
        base_ref[...] = jnp.zeros_like(base_ref)

    lane = lax.broadcasted_iota(I32, (tb, LANES), 1)
    oh0, oh1 = _onehots(rec_ref[...], lane)
    both = oh0 + oh1
    ri = lax.broadcasted_iota(I32, (tb, tb), 0)
    ci = lax.broadcasted_iota(I32, (tb, tb), 1)
    before = _dot((ri > ci).astype(BF16), both.astype(BF16)) + base_ref[0:1, :]
    rank0 = jnp.sum(oh0 * before, axis=-1, keepdims=True)
    rank1 = jnp.sum(oh1 * before, axis=-1, keepdims=True)
    rank_ref[...] = jnp.where(lane == 0, rank0, jnp.where(lane == 1, rank1, 0.0))
    total = base_ref[0:1, :] + jnp.sum(both, axis=0, keepdims=True)
    base_ref[0:1, :] = total
    cnt_ref[...] = jnp.broadcast_to(total, cnt_ref.shape)


def _rank(rec, tb):
    T = rec.shape[0]
    return pl.pallas_call(
        _rank_kernel,
        grid=(T // tb,),
        in_specs=[pl.BlockSpec((tb, LANES), lambda i: (i, 0))],
        out_specs=[pl.BlockSpec((tb, LANES), lambda i: (i, 0)), pl.BlockSpec((8, LANES), lambda i: (0, 0))],
        out_shape=[jax.ShapeDtypeStruct((T, LANES), F32), jax.ShapeDtypeStruct((8, LANES), F32)],
        scratch_shapes=[pltpu.VMEM((8, LANES), F32)],
        compiler_params=_cparams("arbitrary"),
        name="moe_rank",
    )(rec)


def _padded_starts(cnt):
    padded = jnp.ceil(cnt * (1.0 / MOE_BLOCK)) * MOE_BLOCK
    ri = lax.broadcasted_iota(I32, (LANES, LANES), 0)
    ci = lax.broadcasted_iota(I32, (LANES, LANES), 1)
    p8 = jnp.broadcast_to(padded, (8, LANES))
    pend = _dot_exact_lhs_rhs(p8, (ri <= ci).astype(BF16))[0:1, :]
    return pend - padded, pend


def _dot_exact_lhs_rhs(a, b_bf16):
    ah, am, al = _split3(a)
    return _dot(ah, b_bf16) + (_dot(am, b_bf16) + _dot(al, b_bf16))


def _dest_kernel(rec_ref, rank_ref, cnt_ref, dest_ref, blk_ref):
    tb = rec_ref.shape[0]
    cnt = cnt_ref[0:1, :]
    pstart, pend = _padded_starts(cnt)
    lane = lax.broadcasted_iota(I32, (tb, LANES), 1)
    oh0, oh1 = _onehots(rec_ref[...], lane)
    rank = rank_ref[...]
    d0 = jnp.sum(oh0 * pstart, axis=-1, keepdims=True) + rank[:, 0:1]
    d1 = jnp.sum(oh1 * pstart, axis=-1, keepdims=True) + rank[:, 1:2]
    dest_ref[...] = jnp.where(lane == 0, d0, jnp.where(lane == 1, d1, 0.0)).astype(I32)

    @pl.when(pl.program_id(0) == 0)
    def _():
        nb = blk_ref.shape[0]
        blane = lax.broadcasted_iota(I32, (nb, LANES), 1)
        bstart = (lax.broadcasted_iota(I32, (nb, 1), 0) * MOE_BLOCK).astype(F32)
        is_e = blane < N_EXPERTS
        bexp = jnp.sum(jnp.where(is_e & (pend <= bstart), 1.0, 0.0), axis=-1, keepdims=True)
        bexp = jnp.minimum(bexp, N_EXPERTS - 1.0)
        nact = jnp.max(jnp.where(is_e, pend, 0.0), axis=-1, keepdims=True) * (1.0 / MOE_BLOCK)
        lastblk = jnp.where(is_e & (pend > pstart), pend - MOE_BLOCK, -1.0)
        brow = lax.broadcasted_iota(I32, (nb, 1), 0)
        out = jnp.where(blane == 0, bexp, jnp.where(blane == 1, nact, 0.0))
        out = jnp.where(brow == nb - 1, lastblk, out)
        blk_ref[...] = out.astype(I32)


def _dest(rec, rank, cnt, tb, nb_rows):
    T = rec.shape[0]
    return pl.pallas_call(
        _dest_kernel,
        grid=(T // tb,),
        in_specs=[pl.BlockSpec((tb, LANES), lambda i: (i, 0)), pl.BlockSpec((tb, LANES), lambda i: (i, 0)),
                  pl.BlockSpec((8, LANES), lambda i: (0, 0))],
        out_specs=[pl.BlockSpec((tb, LANES), lambda i: (i, 0)), pl.BlockSpec((nb_rows, LANES), lambda i: (0, 0))],
        out_shape=[jax.ShapeDtypeStruct((T, LANES), I32), jax.ShapeDtypeStruct((nb_rows, LANES), I32)],
        compiler_params=_cparams("arbitrary"),
        name="moe_dest",
    )(rec, rank, cnt)


def _scatter_kernel(pad_ref, na_ref, dest_ref, hn_ref, xs_ref, zbuf_ref, idx_ref, sa_ref, sb_ref,
                    sem_a, sem_b, isem, zsem, *, tb, n_blocks, n_tail, n_steps):
    i = pl.program_id(0)

    @pl.when(i == 0)
    def _():
        zbuf_ref[...] = jnp.zeros_like(zbuf_ref)

        def zero_rows(start):
            return pltpu.make_async_copy(zbuf_ref, xs_ref.at[pl.ds(start, MOE_BLOCK)], zsem)

        def pad_start(e, c):
            @pl.when(pad_ref[e] >= 0)
            def _():
                zero_rows(pad_ref[e]).start()
            return c

        def pad_wait(e, c):
            @pl.when(pad_ref[e] >= 0)
            def _():
                zero_rows(pad_ref[e]).wait()
            return c

        def tail_start(k, c):
            @pl.when(na_ref[0] + k < n_blocks)
            def _():
                zero_rows((na_ref[0] + k) * MOE_BLOCK).start()
            return c

        def tail_wait(k, c):
            @pl.when(na_ref[0] + k < n_blocks)
            def _():
                zero_rows((na_ref[0] + k) * MOE_BLOCK).wait()
            return c

        lax.fori_loop(0, N_EXPERTS, pad_start, 0)
        lax.fori_loop(0, n_tail, tail_start, 0)
        lax.fori_loop(0, N_EXPERTS, pad_wait, 0)
        lax.fori_loop(0, n_tail, tail_wait, 0)

    def idx_copy(step, slot):
        return pltpu.make_async_copy(dest_ref.at[step], idx_ref.at[slot], isem)

    @pl.when(i == 0)
    def _():
        idx_copy(0, 0).start()

    slot = i % 2
    idx_copy(i, slot).wait()

    @pl.when(i + 1 < n_steps)
    def _():
        idx_copy(i + 1, 1 - slot).start()

    hb = tb // 2
    for half, (buf, sem) in enumerate(((sa_ref, sem_a), (sb_ref, sem_b))):
        def wait_rows(buf=buf, sem=sem):
            for _ in range(2):
                pltpu.make_async_copy(buf, xs_ref.at[pl.ds(0, hb)], sem).wait()

        @pl.when(i > 0)
        def _():
            wait_rows()

        buf[...] = hn_ref[half * hb:(half + 1) * hb]

        def issue(t, c, buf=buf, sem=sem, half=half):
            for s in range(2):
                d = idx_ref[slot, 2 * (half * hb + t) + s]
                pltpu.make_async_copy(buf.at[pl.ds(t, 1)], xs_ref.at[pl.ds(d, 1)], sem).start(priority=s)
            return c

        lax.fori_loop(0, hb, issue, 0, unroll=8)

    @pl.when(i == n_steps - 1)
    def _():
        for buf, sem in ((sa_ref, sem_a), (sb_ref, sem_b)):
            for _ in range(2):
                pltpu.make_async_copy(buf, xs_ref.at[pl.ds(0, hb)], sem).wait()


def _scatter(lastblk, nact, dest2, hn, tb, n_blocks):
    T, _, D = hn.shape
    n_tail = n_blocks - (-(-2 * T // MOE_BLOCK))
    hb = tb // 2
    return pl.pallas_call(
        functools.partial(_scatter_kernel, tb=tb, n_blocks=n_blocks, n_tail=n_tail, n_steps=T // tb),
        grid_spec=pltpu.PrefetchScalarGridSpec(
            num_scalar_prefetch=2, grid=(T // tb,),
            in_specs=[pl.BlockSpec(memory_space=pl.ANY),
                      pl.BlockSpec((tb, 1, D), lambda i, pad, na: (i, 0, 0))],
            out_specs=pl.BlockSpec(memory_space=pl.ANY),
            scratch_shapes=[pltpu.VMEM((MOE_BLOCK, 1, D), hn.dtype), pltpu.SMEM((2, 2 * tb), I32),
                            pltpu.VMEM((hb, 1, D), hn.dtype), pltpu.VMEM((hb, 1, D), hn.dtype),
                            pltpu.SemaphoreType.DMA, pltpu.SemaphoreType.DMA,
                            pltpu.SemaphoreType.DMA, pltpu.SemaphoreType.DMA]),
        out_shape=jax.ShapeDtypeStruct((n_blocks * MOE_BLOCK, 1, D), hn.dtype),
        compiler_params=_cparams("arbitrary"),
        name="moe_scatter",
    )(lastblk, nact, dest2, hn)


def _expert_kernel(be_ref, na_ref, x_ref, wg_hbm, wu_hbm, wd_hbm, y_ref,
                   wg_buf, wu_buf, wd_buf, wgb_ref, wub_ref, wdb_ref, x2_ref, slot_ref, sems):
    j = pl.program_id(0)
    na = na_ref[0]

    def weight_copies(e, slot):
        return [pltpu.make_async_copy(src.at[e], buf.at[slot], sems.at[slot, n])
                for n, (src, buf) in enumerate(((wg_hbm, wg_buf), (wu_hbm, wu_buf), (wd_hbm, wd_buf)))]

    @pl.when(j == 0)
    def _():
        slot_ref[0] = 0
        for cp in weight_copies(be_ref[0], 0):
            cp.start()

    @pl.when(j < na)
    def _():
        cur = be_ref[j]
        prev = be_ref[jnp.maximum(j - 1, 0)]

        @pl.when((j == 0) | (cur != prev))
        def _():
            slot = slot_ref[0]
            nxt = lax.while_loop(lambda k: (k < na) & (be_ref[jnp.minimum(k, na - 1)] == cur),
                                 lambda k: k + 1, j + 1)

            @pl.when(nxt < na)
            def _():
                for cp in weight_copies(be_ref[jnp.minimum(nxt, na - 1)], 1 - slot):
                    cp.start()

            for cp in weight_copies(cur, slot):
                cp.wait()
            wgb_ref[...] = wg_buf[slot].astype(BF16)
            wub_ref[...] = wu_buf[slot].astype(BF16)
            wdb_ref[...] = wd_buf[slot].astype(BF16)
            slot_ref[0] = 1 - slot

        x2_ref[...] = x_ref[...].reshape(x2_ref.shape)
        lo, hi = _unpack_halves(x2_ref[...])
        lo, hi = lo.astype(BF16), hi.astype(BF16)
        half = lo.shape[1]
        hg = _dot(lo, wgb_ref[:half, :]) + _dot(hi, wgb_ref[half:, :])
        hu = _dot(lo, wub_ref[:half, :]) + _dot(hi, wub_ref[half:, :])
        act = (hg * jax.nn.sigmoid(hg)) * hu
        y = _dot(act.astype(BF16), wdb_ref[...])
        y_ref[...] = _pack_halves(y).reshape(y_ref.shape)

    @pl.when(j >= na)
    def _():
        y_ref[...] = jnp.zeros_like(y_ref)


def _experts(blk_exp, nact, xs, w_gate, w_up, w_down, n_blocks):
    DP = xs.shape[2]
    D = 2 * DP
    rows = n_blocks * MOE_BLOCK
    DE = w_gate.shape[2]
    blk = lambda j, be, na: (jnp.minimum(j, na[0] - 1), 0, 0)
    hbm = pl.BlockSpec(memory_space=pl.ANY)
    return pl.pallas_call(
        _expert_kernel,
        grid_spec=pltpu.PrefetchScalarGridSpec(
            num_scalar_prefetch=2, grid=(n_blocks,),
            in_specs=[pl.BlockSpec((MOE_BLOCK, 1, DP), blk), hbm, hbm, hbm],
            out_specs=pl.BlockSpec((MOE_BLOCK, 1, DP), lambda j, be, na: (j, 0, 0)),
            scratch_shapes=[pltpu.VMEM((2, D, DE), F32), pltpu.VMEM((2, D, DE), F32), pltpu.VMEM((2, DE, D), F32),
                            pltpu.VMEM((D, DE), BF16), pltpu.VMEM((D, DE), BF16), pltpu.VMEM((DE, D), BF16),
                            pltpu.VMEM((MOE_BLOCK, DP), U32), pltpu.SMEM((1,), I32),
                            pltpu.SemaphoreType.DMA((2, 3))]),
        out_shape=jax.ShapeDtypeStruct((rows, 1, DP), U32),
        compiler_params=_cparams("arbitrary"),
        name="moe_experts",
    )(blk_exp, nact, xs, w_gate, w_up, w_down)


def _combine_kernel(dest_ref, ys_ref, h_ref, rec_ref, g_ref, o_ref, idx_ref, ya0_ref, ya1_ref, yb0_ref, yb1_ref,
                    y2_ref, sem_a, sem_b, isem, *, normalize, n_steps):
    i = pl.program_id(0)
    tb = h_ref.shape[0]
    hb = tb // 2
    halves = ((ya0_ref, ya1_ref, sem_a), (yb0_ref, yb1_ref, sem_b))

    def idx_copy(step, slot):
        return pltpu.make_async_copy(dest_ref.at[step], idx_ref.at[slot], isem)

    def issue(half, slot):
        bufs, sem = halves[half][:2], halves[half][2]

        def body(t, c):
            for s in range(2):
                d = idx_ref[slot, 2 * (half * hb + t) + s]
                pltpu.make_async_copy(ys_ref.at[pl.ds(d, 1)], bufs[s].at[pl.ds(t, 1)], sem).start(priority=s)
            return c

        lax.fori_loop(0, hb, body, 0, unroll=8)

    @pl.when(i == 0)
    def _():
        first = idx_copy(0, 0)
        first.start()
        first.wait()
        issue(0, 0)
        issue(1, 0)
        if n_steps > 1:
            idx_copy(1, 1).start()

    nslot = (i + 1) % 2

    @pl.when(i + 1 < n_steps)
    def _():
        idx_copy(i + 1, nslot).wait()

    rec = rec_ref[...]
    for half, (y0_ref, y1_ref, sem) in enumerate(halves):
        rows = slice(half * hb, (half + 1) * hb)
        pltpu.make_async_copy(ys_ref.at[pl.ds(0, hb)], y0_ref, sem).wait()
        pltpu.make_async_copy(ys_ref.at[pl.ds(0, hb)], y1_ref, sem).wait()
        y2_ref[...] = y0_ref[...].reshape(y2_ref.shape)
        h = h_ref[rows, :] + rec[rows, 2:3] * jnp.concatenate(_unpack_halves(y2_ref[...]), axis=1)
        y2_ref[...] = y1_ref[...].reshape(y2_ref.shape)
        h = h + rec[rows, 3:4] * jnp.concatenate(_unpack_halves(y2_ref[...]), axis=1)
        if normalize:
            ms = jnp.mean(h * h, axis=-1, keepdims=True)
            h = h * lax.rsqrt(ms + RMS_EPS) * g_ref[...]
        o_ref[rows, :] = h

        @pl.when(i + 1 < n_steps)
        def _():
            issue(half, nslot)

    @pl.when(i + 2 < n_steps)
    def _():
        idx_copy(i + 2, i % 2).start()


def _combine(dest2, ys, h, rec, g, tb, normalize):
    T, D = h.shape
    hb = tb // 2
    return pl.pallas_call(
        functools.partial(_combine_kernel, normalize=normalize, n_steps=T // tb),
        grid=(T // tb,),
        in_specs=[pl.BlockSpec(memory_space=pl.ANY), pl.BlockSpec(memory_space=pl.ANY),
                  pl.BlockSpec((tb, D), lambda i: (i, 0)), pl.BlockSpec((tb, LANES), lambda i: (i, 0)),
                  pl.BlockSpec((1, D), lambda i: (0, 0))],
        out_specs=pl.BlockSpec((tb, D), lambda i: (i, 0)),
        out_shape=jax.ShapeDtypeStruct((T, D), F32),
        scratch_shapes=[pltpu.SMEM((2, 2 * tb), I32)] + [pltpu.VMEM((hb, 1, D // 2), U32)] * 4 +
                       [pltpu.VMEM((hb, D // 2), U32)] + [pltpu.SemaphoreType.DMA] * 3,
        compiler_params=_cparams("arbitrary"),
        name="moe_combine",
    )(dest2, ys, h, rec, g)


def _pad_cols(a, n):
    return jnp.pad(a, ((0, 0), (0, n - a.shape[1])))


def _pad_rows(a, n, at=0):
    return jnp.pad(a, ((at, n - a.shape[0] - at), (0, 0)))


def _layer(x2, B, L, p):
    T, D = x2.shape
    W = p["s5_d"].shape[0]
    G, P = p["s5_lambda_re"].shape
    HG = W // G
    dl, al, gl = p["rwkv_w2"].shape[0], p["rwkv_a2"].shape[0], p["rwkv_g2"].shape[0]
    H = W // RW_N

    w_in = p["w_in"]
    o = W
    cols = [w_in[:, :W], w_in[:, o:o + 3 * W]]
    o += 3 * W
    cols += [_pad_cols(w_in[:, o:o + dl], 128), _pad_cols(w_in[:, o + dl:o + dl + al], 128),
             _pad_cols(w_in[:, o + dl + al:o + dl + al + gl], 256)]
    w_in_r = jnp.concatenate(cols, axis=1).astype(BF16)
    tm_in = min(1024, T)
    proj = _inproj(x2, p["norm_mix_g"].reshape(1, D), w_in_r, tm_in, 1152)

    C = S5_CHUNK
    nc = L // C
    ct_re = jnp.swapaxes(p["s5_c_re"], 1, 2)
    ct_im = jnp.swapaxes(p["s5_c_im"], 1, 2)
    m_op, wbt, wct, a64 = _s5_ops(p["s5_lambda_re"], p["s5_lambda_im"], p["s5_log_step"],
                                  ct_re, ct_im, p["s5_b_re"], p["s5_b_im"], p["s5_c_re"], p["s5_c_im"])
    rb = min(LANES, B * nc)
    z = _s5_in(proj.reshape(B * nc, C, proj.shape[1]), G, rb)
    y_t = _s5_chunk(z, m_op, wbt, wct, a64, nc)
    y_ssm = _s5_out(y_t, rb).reshape(T, W)
    tm = min(512, T)
    s5_out = _s5_glu(y_ssm, proj, p["s5_d"].reshape(1, W), p["s5_w_glu"].astype(BF16),
                     p["s5_b_glu"].reshape(1, W), tm)

    mu = p["rwkv_mu"]
    mu3 = _pad_rows(mu[:3 * W].reshape(3, W), 8)
    o = 3 * W
    mul = jnp.concatenate([_pad_cols(mu[None, o:o + dl], 128), _pad_cols(mu[None, o + dl:o + dl + al], 128),
                           _pad_cols(mu[None, o + dl + al:], 256)], axis=1)
    w2p = _pad_rows(p["rwkv_w2"], 128).astype(BF16)
    a2p = _pad_rows(p["rwkv_a2"], 128).astype(BF16)
    g2p = _pad_rows(p["rwkv_g2"], 256).astype(BF16)
    head_of = jnp.arange(W, dtype=I32) // RW_N
    e_mat = (head_of[:, None] == jnp.arange(LANES, dtype=I32)[None, :]).astype(BF16)
    et_mat = e_mat.T
    row = lambda a: a.reshape(1, W)
    tm_rw = min(256, L)
    r, k, v, nk, bv, lw, g = _rwkv_prep(proj, B, L, mu3, mul, row(p["rwkv_w0"]), row(p["rwkv_a0"]),
                                        row(p["rwkv_k_k"]), row(p["rwkv_k_a"]), w2p, a2p, g2p,
                                        e_mat, et_mat, tm_rw)
    rw_out = _rwkv_chunk(r, k, v, nk, bv, lw, g, p["rwkv_r_k"].reshape(1, W), row(p["rwkv_ln_w"]),
                         row(p["rwkv_ln_b"]), B, L)

    w_out = p["w_out"].astype(BF16)
    w_route = jnp.concatenate([p["w_route_exp"], p["w_route_grp"]], axis=1)
    w_route = _pad_cols(w_route, LANES)
    wr_hi = w_route.astype(BF16)
    wr_lo = (w_route - wr_hi.astype(F32)).astype(BF16)
    wr = jnp.concatenate([wr_hi, wr_lo], axis=1)
    b_route = _pad_cols(jnp.concatenate([p["b_route_exp"], p["b_route_grp"]])[None, :], LANES)
    tm_o = min(512, T)
    h, hn, rec = _outproj(s5_out, rw_out, x2, w_out[:W], w_out[W:], p["norm_ffn_g"].reshape(1, D),
                          wr, b_route, tm_o)

    tb = min(1024, T)
    rank, cnt = _rank(rec, tb)
    n_blocks = -(-2 * T // MOE_BLOCK) + N_EXPERTS
    nb_rows = -(-(n_blocks + 1) // 8) * 8
    dest, blk = _dest(rec, rank, cnt, tb, nb_rows)
    blk_exp = blk[:n_blocks, 0]
    nact = blk[0:1, 1]
    lastblk = blk[nb_rows - 1, :N_EXPERTS]
    ts = min(256, T)
    dest2 = dest[:, :2].reshape(T // ts, 2 * ts)

    xs = _scatter(lastblk, nact, dest2, hn, ts, n_blocks)
    ys = _experts(blk_exp, nact, xs, p["w_gate"], p["w_up"], p["w_down"], n_blocks)
    return dest2, ys, h, rec, ts


def kernel(x, norm_mix_g, w_in, s5_lambda_re, s5_lambda_im, s5_log_step, s5_b_re, s5_b_im, s5_c_re, s5_c_im, s5_d, s5_w_glu, s5_b_glu, rwkv_mu, rwkv_w0, rwkv_w2, rwkv_a0, rwkv_a2, rwkv_g2, rwkv_k_k, rwkv_k_a, rwkv_r_k, rwkv_ln_w, rwkv_ln_b, w_out, norm_ffn_g, w_route_grp, b_route_grp, w_route_exp, b_route_exp, w_gate, w_up, w_down, norm_final_g):
    B, L, D = x.shape
    params = dict(
        norm_mix_g=norm_mix_g, w_in=w_in, s5_lambda_re=s5_lambda_re, s5_lambda_im=s5_lambda_im,
        s5_log_step=s5_log_step, s5_b_re=s5_b_re, s5_b_im=s5_b_im, s5_c_re=s5_c_re, s5_c_im=s5_c_im,
        s5_d=s5_d, s5_w_glu=s5_w_glu, s5_b_glu=s5_b_glu, rwkv_mu=rwkv_mu, rwkv_w0=rwkv_w0, rwkv_w2=rwkv_w2,
        rwkv_a0=rwkv_a0, rwkv_a2=rwkv_a2, rwkv_g2=rwkv_g2, rwkv_k_k=rwkv_k_k, rwkv_k_a=rwkv_k_a,
        rwkv_r_k=rwkv_r_k, rwkv_ln_w=rwkv_ln_w, rwkv_ln_b=rwkv_ln_b, w_out=w_out, norm_ffn_g=norm_ffn_g,
        w_route_grp=w_route_grp, b_route_grp=b_route_grp, w_route_exp=w_route_exp, b_route_exp=b_route_exp,
        w_gate=w_gate, w_up=w_up, w_down=w_down)
    depth = norm_mix_g.shape[0]
    h2 = x.reshape(B * L, D)
    for l in range(depth):
        p = {k_: v_[l] for k_, v_ in params.items()}
        dest2, ys, h, rec, ts = _layer(h2, B, L, p)
        h2 = _combine(dest2, ys, h, rec, norm_final_g.reshape(1, D), ts, normalize=(l == depth - 1))
    return h2.reshape(B, L, D)
```

```python
import functools

import jax
import jax.numpy as jnp
from jax import lax
from jax.experimental import pallas as pl
from jax.experimental.pallas import tpu as pltpu

F32 = jnp.float32
BF16 = jnp.bfloat16
I32 = jnp.int32

RMS_EPS = 1e-6
S5_CHUNK = 64
S5_HG = 16
S5_P = 64
RW_N = 64
RW_CHUNK = 64
RW_GN_EPS = 64e-5
N_GROUPS = 8
EPG = 8
N_EXPERTS = 64
MOE_BLOCK = 256
LANES = 128
VMEM_LIMIT = 56 * 1024 * 1024


def _cparams(*sem, **kw):
    return pltpu.CompilerParams(dimension_semantics=sem, vmem_limit_bytes=VMEM_LIMIT, **kw)


def _split2(x):
    hi = x.astype(BF16)
    lo = (x - hi.astype(F32)).astype(BF16)
    return hi, lo


def _split3(x):
    hi = x.astype(BF16)
    r = x - hi.astype(F32)
    mid = r.astype(BF16)
    lo = (r - mid.astype(F32)).astype(BF16)
    return hi, mid, lo


def _dot(a, b):
    return jnp.dot(a, b, preferred_element_type=F32)


def _dot_nt(a, b):
    return lax.dot_general(a, b, (((1,), (1,)), ((), ())), preferred_element_type=F32)


def _dot_tn(a, b):
    return lax.dot_general(a, b, (((0,), (0,)), ((), ())), preferred_element_type=F32)


def _dot_x3(a, b):
    ah, al = _split2(a)
    bh, bl = _split2(b)
    return _dot(ah, bh) + (_dot(ah, bl) + _dot(al, bh))


def _dot_exact_lhs(a_bf16, b):
    bh, bm, bl = _split3(b)
    return _dot(a_bf16, bh) + (_dot(a_bf16, bm) + _dot(a_bf16, bl))


def _cmul(ar, ai, br, bi):
    return ar * br - ai * bi, ar * bi + ai * br


U32 = jnp.uint32


def _pack_halves(x):
    half = x.shape[1] // 2
    lo = lax.bitcast_convert_type(x[:, :half].astype(BF16).astype(F32), U32)
    hi = lax.bitcast_convert_type(x[:, half:].astype(BF16).astype(F32), U32)
    return (lo >> 16) | hi


def _unpack_halves(p):
    lo = lax.bitcast_convert_type(p << 16, F32)
    hi = lax.bitcast_convert_type(p & jnp.uint32(0xFFFF0000), F32)
    return lo, hi


def _inproj_kernel(x_ref, g_ref, w_ref, o_ref, hn_ref):
    @pl.when(pl.program_id(1) == 0)
    def _():
        x = x_ref[...]
        ms = jnp.mean(x * x, axis=-1, keepdims=True)
        hn_ref[...] = (x * lax.rsqrt(ms + RMS_EPS) * g_ref[...]).astype(BF16)

    o_ref[...] = _dot(hn_ref[...], w_ref[...])


def _inproj(x2, g, w_bf16, tm, tn):
    T, D = x2.shape
    N = w_bf16.shape[1]
    return pl.pallas_call(
        _inproj_kernel,
        grid=(T // tm, N // tn),
        in_specs=[
            pl.BlockSpec((tm, D), lambda i, j: (i, 0)),
            pl.BlockSpec((1, D), lambda i, j: (0, 0)),
            pl.BlockSpec((D, tn), lambda i, j: (0, j)),
        ],
        out_specs=pl.BlockSpec((tm, tn), lambda i, j: (i, j)),
        out_shape=jax.ShapeDtypeStruct((T, N), F32),
        scratch_shapes=[pltpu.VMEM((tm, D), BF16)],
        compiler_params=_cparams("arbitrary", "arbitrary"),
        name="inproj",
    )(x2, g, w_bf16)


def _binpow(ar, ai, expo, nbits):
    pr = jnp.ones(expo.shape, F32)
    pi = jnp.zeros(expo.shape, F32)
    sr, si = ar, ai
    for bit in range(nbits):
        m = ((expo >> bit) & 1) == 1
        nr, ni = _cmul(pr, pi, sr, si)
        pr = jnp.where(m, nr, pr)
        pi = jnp.where(m, ni, pi)
        if bit + 1 < nbits:
            sr, si = _cmul(sr, si, sr, si)
    return pr, pi


def _frame_powers(ar, ai, reverse, plus_one):
    C, HG, P = S5_CHUNK, S5_HG, ar.shape[0]
    fpt = LANES // HG
    j = lax.broadcasted_iota(I32, (P, LANES), 1) // HG
    inner_r, inner_i = _binpow(ar, ai, (fpt - 1 - j) if reverse else j, (fpt - 1).bit_length())
    sr, si = ar, ai
    for _ in range(fpt.bit_length() - 1):
        sr, si = _cmul(sr, si, sr, si)
    outer = [(ar, ai) if plus_one else (jnp.ones_like(ar), jnp.zeros_like(ai))]
    for _ in range(C // fpt - 1):
        outer.append(_cmul(outer[-1][0], outer[-1][1], sr, si))
    if reverse:
        outer = outer[::-1]
    tiles = [_cmul(inner_r, inner_i, o_r, o_i) for o_r, o_i in outer]
    return (jnp.concatenate([t[0] for t in tiles], axis=1), jnp.concatenate([t[1] for t in tiles], axis=1))


def _zoh(lr, li, dt):
    mag = jnp.exp(lr * dt)
    ang = li * dt
    ar, ai = mag * jnp.cos(ang), mag * jnp.sin(ang)
    den = lr * lr + li * li
    nr, ni = ar - 1.0, ai
    fr = (nr * lr + ni * li) / den
    fi = (ni * lr - nr * li) / den
    return ar, ai, fr, fi


def _s5_ops_group(lrc_ref, lic_ref, dt_ref, ct_re_ref, ct_im_ref, bt_re_ref, bt_im_ref, c_re_ref, c_im_ref,
                  m_ref, wbt_ref, wct_ref, a64_ref):
    C, HG, P = S5_CHUNK, S5_HG, S5_P
    W = C * HG
    dt = jnp.exp(dt_ref[...])
    ar_c, ai_c, fr_c, fi_c = _zoh(lrc_ref[...], lic_ref[...], dt)

    pr, pi = _frame_powers(ar_c, ai_c, False, True)
    qr, qi = _frame_powers(ar_c, ai_c, True, False)

    sel = (lax.broadcasted_iota(I32, (HG, W), 1) % HG == lax.broadcasted_iota(I32, (HG, W), 0)).astype(BF16)
    tile = lambda a: _dot_exact_lhs_rhs(a, sel)

    c1_re, c1_im = _cmul(tile(ct_re_ref[...]), tile(ct_im_ref[...]), pr, pi)
    wct_ref[...] = jnp.concatenate([c1_re, -c1_im], axis=0).T.astype(BF16)

    bb_re, bb_im = _cmul(bt_re_ref[...], bt_im_ref[...], fr_c, fi_c)
    ab_re, ab_im = _cmul(tile(bb_re), tile(bb_im), qr, qi)
    wbt_ref[:P, :] = ab_re.astype(BF16)
    wbt_ref[P:, :] = ab_im.astype(BF16)

    cr, ci = ar_c, ai_c
    for _ in range(C.bit_length() - 1):
        cr, ci = _cmul(cr, ci, cr, ci)
    a64_ref[...] = jnp.concatenate([cr, ci], axis=0)

    strip = _dot_x3(c_re_ref[...], ab_re) - _dot_x3(c_im_ref[...], ab_im)
    ext = jnp.concatenate([strip, jnp.zeros((HG, W), F32)], axis=1)
    for t in range(C):
        off = (C - 1 - t) * HG
        m_ref[t * HG:(t + 1) * HG, :] = ext[:, off:off + W].astype(BF16)


S5_FB = 8


def _s5_in_kernel(x_ref, z_ref):
    G, _, rb = z_ref.shape
    for f in range(S5_FB):
        xt = x_ref[:, f, :].T
        z_ref[:, f * S5_HG:(f + 1) * S5_HG, :] = xt.reshape(G, S5_HG, rb).astype(BF16)


def _s5_in(proj3, G, rb):
    R, C, _ = proj3.shape
    W = G * S5_HG
    return pl.pallas_call(
        _s5_in_kernel,
        grid=(R // rb, C // S5_FB),
        in_specs=[pl.BlockSpec((rb, S5_FB, W), lambda i, j: (i, j, 0))],
        out_specs=pl.BlockSpec((G, S5_FB * S5_HG, rb), lambda i, j: (0, j, i)),
        out_shape=jax.ShapeDtypeStruct((G, C * S5_HG, R), BF16),
        compiler_params=_cparams("arbitrary", "arbitrary"),
        name="s5_in",
    )(proj3)


def _s5_out_kernel(yt_ref, o_ref):
    G, _, rb = yt_ref.shape
    for f in range(S5_FB):
        slab = yt_ref[:, f * S5_HG:(f + 1) * S5_HG, :].reshape(G * S5_HG, rb)
        o_ref[:, f, :] = slab.T


def _s5_out(yt, rb):
    G, CW, R = yt.shape
    C = CW // S5_HG
    W = G * S5_HG
    return pl.pallas_call(
        _s5_out_kernel,
        grid=(R // rb, C // S5_FB),
        in_specs=[pl.BlockSpec((G, S5_FB * S5_HG, rb), lambda i, j: (0, j, i))],
        out_specs=pl.BlockSpec((rb, S5_FB, W), lambda i, j: (i, j, 0)),
        out_shape=jax.ShapeDtypeStruct((R, C, W), F32),
        compiler_params=_cparams("arbitrary", "arbitrary"),
        name="s5_out",
    )(yt)


def _s5_chunk_kernel(*refs, n_chunks):
    params, (z_ref, y_ref), ops = refs[:9], refs[9:11], refs[11:]
    for g in range(z_ref.shape[0]):
        at = lambda rs: [r.at[g] for r in rs]
        _s5_ops_group(*at(params), *at(ops))
        _s5_chunk_group(z_ref.at[g], *at(ops), y_ref.at[g], n_chunks)


def _s5_chunk_group(z_ref, m_ref, wbt_ref, wct_ref, a64_ref, y_ref, n_chunks):
    P = S5_P
    z = z_ref[...]
    R = z.shape[1]
    x = _dot(wbt_ref[...], z)
    y_ref[...] = _dot(m_ref[...], z)
    xr, xi = x[:P], x[P:]
    a = a64_ref[...]
    ar, ai = a[:P], a[P:]
    cidx = lax.broadcasted_iota(I32, (1, R), 1) % n_chunks
    shift = 1
    while shift < n_chunks:
        keep = cidx >= shift
        sr = jnp.where(keep, pltpu.roll(xr, shift, 1), 0.0)
        si = jnp.where(keep, pltpu.roll(xi, shift, 1), 0.0)
        xr, xi = xr + (ar * sr - ai * si), xi + (ar * si + ai * sr)
        ar, ai = ar * ar - ai * ai, 2.0 * ar * ai
        shift *= 2
    keep = cidx >= 1
    s_in = jnp.concatenate([jnp.where(keep, pltpu.roll(xr, 1, 1), 0.0),
                            jnp.where(keep, pltpu.roll(xi, 1, 1), 0.0)], axis=0)
    sh, sl = _split2(s_in)
    wct = wct_ref[...]
    y_ref[...] = y_ref[...] + (_dot(wct, sh) + _dot(wct, sl))


def _s5_chunk(z, lam_re, lam_im, log_step, ct_re, ct_im, bt_re, bt_im, c_re, c_im, n_chunks):
    G, W, R = z.shape
    P, HG = lam_re.shape[1], S5_HG
    P2 = 2 * P
    gps = 2 if G % 2 == 0 else 1
    col = lambda a: a.reshape(G, P, 1)
    g3 = lambda s1, s2: pl.BlockSpec((gps, s1, s2), lambda g: (g, 0, 0))
    return pl.pallas_call(
        functools.partial(_s5_chunk_kernel, n_chunks=n_chunks),
        grid=(G // gps,),
        in_specs=[g3(P, 1), g3(P, 1), g3(1, 1),
                  g3(P, HG), g3(P, HG), g3(P, HG), g3(P, HG), g3(HG, P), g3(HG, P), g3(W, R)],
        out_specs=g3(W, R),
        out_shape=jax.ShapeDtypeStruct((G, W, R), F32),
        scratch_shapes=[pltpu.VMEM((gps, W, W), BF16), pltpu.VMEM((gps, P2, W), BF16),
                        pltpu.VMEM((gps, W, P2), BF16), pltpu.VMEM((gps, P2, 1), F32)],
        compiler_params=_cparams("arbitrary"),
        name="s5_chunk",
    )(col(lam_re), col(lam_im), log_step.reshape(G, 1, 1), ct_re, ct_im, bt_re, bt_im, c_re, c_im, z)


def _gelu_tanh(x):
    return 0.5 * x * (1.0 + jnp.tanh(0.7978845608028654 * (x + 0.044715 * (x * x * x))))


def _s5_glu_kernel(y_ref, u_ref, d_ref, w_ref, b_ref, o_ref):
    y = y_ref[...] + d_ref[...] * u_ref[...]
    y = _gelu_tanh(y)
    z = _dot(y.astype(BF16), w_ref[...]) + b_ref[...]
    o_ref[...] = (y * jax.nn.sigmoid(z)).astype(BF16)


def _s5_glu(y_ssm, proj, d, w_bf16, b, tm):
    T, W = y_ssm.shape
    return pl.pallas_call(
        _s5_glu_kernel,
        grid=(T // tm,),
        in_specs=[
            pl.BlockSpec((tm, W), lambda i: (i, 0)),
            pl.BlockSpec((tm, W), lambda i: (i, 0)),
            pl.BlockSpec((1, W), lambda i: (0, 0)),
            pl.BlockSpec((W, W), lambda i: (0, 0)),
            pl.BlockSpec((1, W), lambda i: (0, 0)),
        ],
        out_specs=pl.BlockSpec((tm, W), lambda i: (i, 0)),
        out_shape=jax.ShapeDtypeStruct((T, W), BF16),
        compiler_params=_cparams("arbitrary"),
        name="s5_glu",
    )(y_ssm, proj, d, w_bf16, b)


def _head_sum(x, e_ref, et_ref):
    xh, xl = _split2(x)
    e = e_ref[...]
    s = _dot(xh, e) + _dot(xl, e)
    sh, sl = _split2(s)
    et = et_ref[...]
    return _dot(sh, et) + _dot(sl, et)


def _shift(z, prev_row):
    rolled = pltpu.roll(z, 1, 0)
    first = lax.broadcasted_iota(I32, (z.shape[0], 1), 0) == 0
    return jnp.where(first, prev_row, rolled)


def _rwkv_prep_kernel(zr_ref, zk_ref, zv_ref, zl_ref, mu_ref, mul_ref, w0_ref, a0_ref, kk_ref, ka_ref,
                      w2_ref, a2_ref, g2_ref, e_ref, et_ref,
                      r_ref, k_ref, v_ref, nk_ref, b_ref, lw_ref, g_ref,
                      car_ref, carl_ref):
    W = r_ref.shape[1]

    @pl.when(pl.program_id(1) == 0)
    def _():
        car_ref[...] = jnp.zeros_like(car_ref)
        carl_ref[...] = jnp.zeros_like(carl_ref)

    def lerp(z, prev_row, mu):
        return z + (_shift(z, prev_row) - z) * mu

    tm = zr_ref.shape[0]
    zr, zk, zv, zl = zr_ref[...], zk_ref[...], zv_ref[...], zl_ref[...]
    r = lerp(zr, car_ref[0:1, :], mu_ref[0:1, :])
    k = lerp(zk, car_ref[1:2, :], mu_ref[1:2, :])
    v = lerp(zv, car_ref[2:3, :], mu_ref[2:3, :])
    xl = lerp(zl, carl_ref[0:1, :], mul_ref[...])
    car_ref[0:1, :] = zr[tm - 1:tm, :]
    car_ref[1:2, :] = zk[tm - 1:tm, :]
    car_ref[2:3, :] = zv[tm - 1:tm, :]
    carl_ref[0:1, :] = zl[tm - 1:tm, :]

    xw, xa, xg = xl[:, 0:128], xl[:, 128:256], xl[:, 256:512]
    dw = _dot(jnp.tanh(xw).astype(BF16), w2_ref[...])
    da = _dot(xa.astype(BF16), a2_ref[...])
    g = _dot(jax.nn.sigmoid(xg).astype(BF16), g2_ref[...])

    zw = -(w0_ref[...] + dw)
    softplus = jnp.maximum(zw, 0.0) + jnp.log(1.0 + jnp.exp(-jnp.abs(zw)))
    w_log = -softplus - 0.5
    a = jax.nn.sigmoid(a0_ref[...] + da)

    kk = k * kk_ref[...]
    n2 = _head_sum(kk * kk, e_ref, et_ref)
    kk = kk / jnp.maximum(jnp.sqrt(n2), 1e-12)

    r_ref[...] = r.astype(BF16)
    k_ref[...] = (k * (1.0 + (a - 1.0) * ka_ref[...])).astype(BF16)
    v_ref[...] = v.astype(BF16)
    nk_ref[...] = kk.astype(BF16)
    b_ref[...] = (kk * a).astype(BF16)
    lw_ref[...] = -jnp.exp(w_log)
    g_ref[...] = g.astype(BF16)


def _rwkv_prep(proj, B, L, mu3, mul, w0, a0, k_k, k_a, w2p, a2p, g2p, e_mat, et_mat, tm):
    T = B * L
    W = w0.shape[1]
    nt = L // tm
    row = lambda c: pl.BlockSpec((tm, W), lambda b, i, c=c: (b * nt + i, c))
    full = lambda a: pl.BlockSpec(a.shape, lambda b, i: (0, 0))
    out = pl.BlockSpec((tm, W), lambda b, i: (b * nt + i, 0))
    lw = 512
    return pl.pallas_call(
        _rwkv_prep_kernel,
        grid=(B, nt),
        in_specs=[row(1), row(2), row(3),
                  pl.BlockSpec((tm, lw), lambda b, i: (b * nt + i, 4 * W // lw)),
                  full(mu3), full(mul), full(w0), full(a0), full(k_k), full(k_a),
                  full(w2p), full(a2p), full(g2p), full(e_mat), full(et_mat)],
        out_specs=[out] * 7,
        out_shape=[jax.ShapeDtypeStruct((T, W), dt) for dt in (BF16, BF16, BF16, BF16, BF16, F32, BF16)],
        scratch_shapes=[pltpu.VMEM((8, W), F32), pltpu.VMEM((8, lw), F32)],
        compiler_params=_cparams("arbitrary", "arbitrary"),
        name="rwkv_prep",
    )(proj, proj, proj, proj, mu3, mul, w0, a0, k_k, k_a, w2p, a2p, g2p, e_mat, et_mat)


def _rwkv_chunk_kernel(r_ref, k_ref, v_ref, nk_ref, b_ref, lw_ref, g_ref, rk_ref, lnw_ref, lnb_ref,
                       o_ref, z_ref, ys_ref, qs_ref, vs_ref, gs_ref):
    C, N = RW_CHUNK, RW_N
    H = r_ref.shape[1] // N

    @pl.when(pl.program_id(1) == 0)
    def _():
        z_ref[...] = jnp.zeros_like(z_ref)
        ys_ref[...] = jnp.zeros_like(ys_ref)
        qs_ref[...] = jnp.zeros_like(qs_ref)
        vs_ref[...] = jnp.zeros_like(vs_ref)
        gs_ref[...] = jnp.zeros_like(gs_ref)

    low = lax.broadcasted_iota(I32, (C, 2 * N), 1) < N
    inv_n = 1.0 / N

    def head_sums(t):
        s0 = jnp.sum(jnp.where(low, t, 0.0), axis=-1, keepdims=True)
        s1 = jnp.sum(jnp.where(low, 0.0, t), axis=-1, keepdims=True)
        return jnp.where(low, s0, s1)

    for p in range(H // 2):
        ps = slice(2 * p * N, 2 * (p + 1) * N)
        yp = ys_ref[:, ps]
        yc = yp - head_sums(yp) * inv_n
        var = head_sums(yc * yc) * inv_n
        yn = yc * lax.rsqrt(var + RW_GN_EPS) * lnw_ref[:, ps] + lnb_ref[:, ps]
        bonus = head_sums(qs_ref[:, ps]) * vs_ref[:, ps].astype(F32)
        o_ref[:, ps] = ((yn + bonus) * gs_ref[:, ps].astype(F32)).astype(BF16)

    ri = lax.broadcasted_iota(I32, (C, C), 0)
    ci = lax.broadcasted_iota(I32, (C, C), 1)
    tril = ri >= ci
    stril = ri > ci
    eye = (ri == ci).astype(F32)

    lw = lw_ref[...]
    cs = _dot_exact_lhs(tril.astype(BF16), lw)
    tot = cs[C - 1:C, :]
    p_inc = jnp.exp(cs)
    p_inv = jnp.exp(-cs)
    p_prev = jnp.exp(cs - lw)
    p_rest = jnp.exp(tot - cs)
    p_tot = jnp.exp(tot)

    r_f, k_f, b_f = r_ref[...].astype(F32), k_ref[...].astype(F32), b_ref[...].astype(F32)
    r_t = r_f * p_inc
    k_t = k_f * p_inv
    a_t = -nk_ref[...].astype(F32) * p_prev
    b_t = b_f * p_inv
    k_h = k_f * p_rest
    b_h = b_f * p_rest
    v_all = v_ref[...]

    hs = range(H)
    sl = [slice(h * N, (h + 1) * N) for h in hs]
    each = lambda f: [f(h) for h in hs]
    v = each(lambda h: v_all[:, sl[h]].astype(BF16))
    ar = each(lambda h: jnp.concatenate([a_t[:, sl[h]], r_t[:, sl[h]]], axis=0).astype(BF16))
    m_b = each(lambda h: _dot_nt(ar[h], b_t[:, sl[h]].astype(BF16)))
    m_k = each(lambda h: _dot_nt(ar[h], k_t[:, sl[h]].astype(BF16)))
    l_ab = each(lambda h: jnp.where(stril, m_b[h][:C], 0.0).astype(BF16))
    m_rb = each(lambda h: jnp.where(tril, m_b[h][C:], 0.0).astype(BF16))
    both = jnp.concatenate([stril, tril], axis=0)
    lm_k = each(lambda h: jnp.where(both, m_k[h], 0.0).astype(BF16))
    lmv = each(lambda h: _dot(lm_k[h], v[h]))

    x = each(lambda h: jnp.concatenate([lmv[h][:C], a_t[:, sl[h]]], axis=1))
    y0 = each(lambda h: lmv[h][C:])
    hk = each(lambda h: _dot_tn(k_h[:, sl[h]].astype(BF16), v[h]))
    lp = l_ab
    step = 1
    while step < C:
        x = each(lambda h: x[h] + _dot(lp[h], x[h].astype(BF16)))
        step *= 2
        if step < C:
            lp = each(lambda h: _dot(lp[h], lp[h]).astype(BF16))
    xb = each(lambda h: x[h].astype(BF16))
    yq = each(lambda h: _dot(m_rb[h], xb[h]))
    gh = each(lambda h: _dot_tn(b_h[:, sl[h]].astype(BF16), xb[h]))

    z = each(lambda h: _split2(z_ref[h]))
    qg = each(lambda h: jnp.concatenate([yq[h][:, N:] + r_t[:, sl[h]],
                                         gh[h][:, N:] + eye * p_tot[:, sl[h]]], axis=0).astype(BF16))
    qgz = each(lambda h: _dot(qg[h], z[h][0]) + _dot(qg[h], z[h][1]))
    yz = each(lambda h: qgz[h][:C])
    gz = each(lambda h: qgz[h][C:])
    for h in hs:
        z_ref[h] = (gh[h][:, :N] + hk[h]) + gz[h]
        ys_ref[:, sl[h]] = (yq[h][:, :N] + y0[h]) + yz[h]
    qs_ref[...] = r_f * k_f * rk_ref[...]
    vs_ref[...] = v_all
    gs_ref[...] = g_ref[...]


def _rwkv_chunk(r, k, v, nk, b, lw, g, r_k, ln_w, ln_b, B, L):
    T, W = r.shape
    C = RW_CHUNK
    nc = L // C
    H = W // RW_N
    assert H % 2 == 0
    spec = pl.BlockSpec((C, W), lambda bi, c: (bi * nc + jnp.minimum(c, nc - 1), 0))
    full = pl.BlockSpec((1, W), lambda bi, c: (0, 0))
    return pl.pallas_call(
        _rwkv_chunk_kernel,
        grid=(B, nc + 1),
        in_specs=[spec] * 7 + [full] * 3,
        out_specs=pl.BlockSpec((C, W), lambda bi, c: (bi * nc + jnp.maximum(c - 1, 0), 0)),
        out_shape=jax.ShapeDtypeStruct((T, W), BF16),
        scratch_shapes=[pltpu.VMEM((H, RW_N, RW_N), F32), pltpu.VMEM((C, W), F32), pltpu.VMEM((C, W), F32),
                        pltpu.VMEM((C, W), BF16), pltpu.VMEM((C, W), BF16)],
        compiler_params=_cparams("arbitrary", "arbitrary"),
        name="rwkv_chunk",
    )(r, k, v, nk, b, lw, g, r_k, ln_w, ln_b)


def _first_index_of_max(vals, lane, valid):
    neg = jnp.float32(-jnp.inf)
    masked = jnp.where(valid, vals, neg)
    m = jnp.max(masked, axis=-1, keepdims=True)
    idx = jnp.min(jnp.where(valid & (masked == m), lane, LANES), axis=-1, keepdims=True)
    return m, idx


def _outproj_kernel(s5_ref, rw_ref, x_ref, wt_ref, wb_ref, g_ref, wr_ref, br_ref,
                    h_ref, hn_ref, rec_ref):
    h = x_ref[...] + (_dot(s5_ref[...], wt_ref[...]) + _dot(rw_ref[...], wb_ref[...]))
    h_ref[...] = h
    ms = jnp.mean(h * h, axis=-1, keepdims=True)
    hn = h * lax.rsqrt(ms + RMS_EPS) * g_ref[...]
    hn_ref[...] = _pack_halves(hn).reshape(hn_ref.shape)

    hh, hl = _split2(hn)
    wh, wl = wr_ref[:, :LANES], wr_ref[:, LANES:]
    logits = _dot(hh, wh) + (_dot(hh, wl) + _dot(hl, wh)) + br_ref[...]
    lane = lax.broadcasted_iota(I32, logits.shape, 1)
    is_grp = (lane >= N_EXPERTS) & (lane < N_EXPERTS + N_GROUPS)
    gmax, gidx = _first_index_of_max(logits, lane, is_grp)
    gsum = jnp.sum(jnp.where(is_grp, jnp.exp(logits - gmax), 0.0), axis=-1, keepdims=True)
    p_grp = 1.0 / gsum
    grp = gidx - N_EXPERTS
    in_grp = (lane >= grp * EPG) & (lane < (grp + 1) * EPG)
    m1, i1 = _first_index_of_max(logits, lane, in_grp)
    m2, i2 = _first_index_of_max(logits, lane, in_grp & (lane != i1))
    e = jnp.exp(m2 - m1)
    g1 = p_grp / (1.0 + e)
    g2 = p_grp * e / (1.0 + e)
    rec = jnp.where(lane == 0, i1.astype(F32),
          jnp.where(lane == 1, i2.astype(F32),
          jnp.where(lane == 2, g1, jnp.where(lane == 3, g2, 0.0))))
    rec_ref[...] = rec


def _outproj(s5o, rwo, x2, w_top, w_bot, g, wr, b_route, tm):
    T, D = x2.shape
    W = s5o.shape[1]
    full = lambda a: pl.BlockSpec(a.shape, lambda i: (0, 0))
    return pl.pallas_call(
        _outproj_kernel,
        grid=(T // tm,),
        in_specs=[pl.BlockSpec((tm, W), lambda i: (i, 0)), pl.BlockSpec((tm, W), lambda i: (i, 0)),
                  pl.BlockSpec((tm, D), lambda i: (i, 0)),
                  full(w_top), full(w_bot), full(g), full(wr), full(b_route)],
        out_specs=[pl.BlockSpec((tm, D), lambda i: (i, 0)), pl.BlockSpec((tm, 1, D // 2), lambda i: (i, 0, 0)),
                   pl.BlockSpec((tm, LANES), lambda i: (i, 0))],
        out_shape=[jax.ShapeDtypeStruct((T, D), F32), jax.ShapeDtypeStruct((T, 1, D // 2), U32),
                   jax.ShapeDtypeStruct((T, LANES), F32)],
        compiler_params=_cparams("arbitrary"),
        name="outproj_route",
    )(s5o, rwo, x2, w_top, w_bot, g, wr, b_route)


def _onehots(rec, lane):
    oh0 = (lane == rec[:, 0:1].astype(I32)).astype(F32)
    oh1 = (lane == rec[:, 1:2].astype(I32)).astype(F32)
    return oh0, oh1


def _rank_kernel(rec_ref, rank_ref, cnt_ref, base_ref):
    tb = rec_ref.shape[0]

    @pl.when(pl.program_id(0) == 0)
    def _():
        base_ref[...] = jnp.zeros_like(base_ref)

    lane = lax.broadcasted_iota(I32, (tb, LANES), 1)
    oh0, oh1 = _onehots(rec_ref[...], lane)
    both = oh0 + oh1
    ri = lax.broadcasted_iota(I32, (tb, tb), 0)
    ci = lax.broadcasted_iota(I32, (tb, tb), 1)
    before = _dot((ri > ci).astype(BF16), both.astype(BF16)) + base_ref[0:1, :]
    rank0 = jnp.sum(oh0 * before, axis=-1, keepdims=True)
    rank1 = jnp.sum(oh1 * before, axis=-1, keepdims=True)
    rank_ref[...] = jnp.where(lane == 0, rank0, jnp.where(lane == 1, rank1, 0.0))
    total = base_ref[0:1, :] + jnp.sum(both, axis=0, keepdims=True)
    base_ref[0:1, :] = total
    cnt_ref[...] = jnp.broadcast_to(total, cnt_ref.shape)


def _rank(rec, tb):
    T = rec.shape[0]
    return pl.pallas_call(
        _rank_kernel,
        grid=(T // tb,),
        in_specs=[pl.BlockSpec((tb, LANES), lambda i: (i, 0))],
        out_specs=[pl.BlockSpec((tb, LANES), lambda i: (i, 0)), pl.BlockSpec((8, LANES), lambda i: (0, 0))],
        out_shape=[jax.ShapeDtypeStruct((T, LANES), F32), jax.ShapeDtypeStruct((8, LANES), F32)],
        scratch_shapes=[pltpu.VMEM((8, LANES), F32)],
        compiler_params=_cparams("arbitrary"),
        name="moe_rank",
    )(rec)


def _padded_starts(cnt):
    padded = jnp.ceil(cnt * (1.0 / MOE_BLOCK)) * MOE_BLOCK
    ri = lax.broadcasted_iota(I32, (LANES, LANES), 0)
    ci = lax.broadcasted_iota(I32, (LANES, LANES), 1)
    p8 = jnp.broadcast_to(padded, (8, LANES))
    pend = _dot_exact_lhs_rhs(p8, (ri <= ci).astype(BF16))[0:1, :]
    return pend - padded, pend


def _dot_exact_lhs_rhs(a, b_bf16):
    ah, am, al = _split3(a)
    return _dot(ah, b_bf16) + (_dot(am, b_bf16) + _dot(al, b_bf16))


def _dest_kernel(rec_ref, rank_ref, cnt_ref, dest_ref, blk_ref):
    tb = rec_ref.shape[0]
    cnt = cnt_ref[0:1, :]
    pstart, pend = _padded_starts(cnt)
    lane = lax.broadcasted_iota(I32, (tb, LANES), 1)
    oh0, oh1 = _onehots(rec_ref[...], lane)
    rank = rank_ref[...]
    d0 = jnp.sum(oh0 * pstart, axis=-1, keepdims=True) + rank[:, 0:1]
    d1 = jnp.sum(oh1 * pstart, axis=-1, keepdims=True) + rank[:, 1:2]
    dest_ref[...] = jnp.where(lane == 0, d0, jnp.where(lane == 1, d1, 0.0)).astype(I32)

    @pl.when(pl.program_id(0) == 0)
    def _():
        nb = blk_ref.shape[0]
        blane = lax.broadcasted_iota(I32, (nb, LANES), 1)
        bstart = (lax.broadcasted_iota(I32, (nb, 1), 0) * MOE_BLOCK).astype(F32)
        is_e = blane < N_EXPERTS
        bexp = jnp.sum(jnp.where(is_e & (pend <= bstart), 1.0, 0.0), axis=-1, keepdims=True)
        bexp = jnp.minimum(bexp, N_EXPERTS - 1.0)
        nact = jnp.max(jnp.where(is_e, pend, 0.0), axis=-1, keepdims=True) * (1.0 / MOE_BLOCK)
        lastblk = jnp.where(is_e & (pend > pstart), pend - MOE_BLOCK, -1.0)
        brow = lax.broadcasted_iota(I32, (nb, 1), 0)
        out = jnp.where(blane == 0, bexp, jnp.where(blane == 1, nact, 0.0))
        out = jnp.where(brow == nb - 1, lastblk, out)
        blk_ref[...] = out.astype(I32)


def _dest(rec, rank, cnt, tb, nb_rows):
    T = rec.shape[0]
    return pl.pallas_call(
        _dest_kernel,
        grid=(T // tb,),
        in_specs=[pl.BlockSpec((tb, LANES), lambda i: (i, 0)), pl.BlockSpec((tb, LANES), lambda i: (i, 0)),
                  pl.BlockSpec((8, LANES), lambda i: (0, 0))],
        out_specs=[pl.BlockSpec((tb, LANES), lambda i: (i, 0)), pl.BlockSpec((nb_rows, LANES), lambda i: (0, 0))],
        out_shape=[jax.ShapeDtypeStruct((T, LANES), I32), jax.ShapeDtypeStruct((nb_rows, LANES), I32)],
        compiler_params=_cparams("arbitrary"),
        name="moe_dest",
    )(rec, rank, cnt)


def _scatter_kernel(pad_ref, na_ref, dest_ref, hn_ref, xs_ref, zbuf_ref, idx_ref, sa_ref, sb_ref,
                    sem_a, sem_b, isem, zsem, *, tb, n_blocks, n_tail, n_steps):
    i = pl.program_id(0)

    @pl.when(i == 0)
    def _():
        zbuf_ref[...] = jnp.zeros_like(zbuf_ref)

        def zero_rows(start):
            return pltpu.make_async_copy(zbuf_ref, xs_ref.at[pl.ds(start, MOE_BLOCK)], zsem)

        def pad_start(e, c):
            @pl.when(pad_ref[e] >= 0)
            def _():
                zero_rows(pad_ref[e]).start()
            return c

        def pad_wait(e, c):
            @pl.when(pad_ref[e] >= 0)
            def _():
                zero_rows(pad_ref[e]).wait()
            return c

        def tail_start(k, c):
            @pl.when(na_ref[0] + k < n_blocks)
            def _():
                zero_rows((na_ref[0] + k) * MOE_BLOCK).start()
            return c

        def tail_wait(k, c):
            @pl.when(na_ref[0] + k < n_blocks)
            def _():
                zero_rows((na_ref[0] + k) * MOE_BLOCK).wait()
            return c

        lax.fori_loop(0, N_EXPERTS, pad_start, 0)
        lax.fori_loop(0, n_tail, tail_start, 0)
        lax.fori_loop(0, N_EXPERTS, pad_wait, 0)
        lax.fori_loop(0, n_tail, tail_wait, 0)

    def idx_copy(step, slot):
        return pltpu.make_async_copy(dest_ref.at[step], idx_ref.at[slot], isem)

    @pl.when(i == 0)
    def _():
        idx_copy(0, 0).start()

    slot = i % 2
    idx_copy(i, slot).wait()

    @pl.when(i + 1 < n_steps)
    def _():
        idx_copy(i + 1, 1 - slot).start()

    hb = tb // 2
    for half, (buf, sem) in enumerate(((sa_ref, sem_a), (sb_ref, sem_b))):
        def wait_rows(buf=buf, sem=sem):
            for _ in range(2):
                pltpu.make_async_copy(buf, xs_ref.at[pl.ds(0, hb)], sem).wait()

        @pl.when(i > 0)
        def _():
            wait_rows()

        buf[...] = hn_ref[half * hb:(half + 1) * hb]

        def issue(t, c, buf=buf, sem=sem, half=half):
            for s in range(2):
                d = idx_ref[slot, 2 * (half * hb + t) + s]
                pltpu.make_async_copy(buf.at[pl.ds(t, 1)], xs_ref.at[pl.ds(d, 1)], sem).start(priority=s)
            return c

        lax.fori_loop(0, hb, issue, 0, unroll=8)

    @pl.when(i == n_steps - 1)
    def _():
        for buf, sem in ((sa_ref, sem_a), (sb_ref, sem_b)):
            for _ in range(2):
                pltpu.make_async_copy(buf, xs_ref.at[pl.ds(0, hb)], sem).wait()


def _scatter(lastblk, nact, dest2, hn, tb, n_blocks):
    T, _, D = hn.shape
    n_tail = n_blocks - (-(-2 * T // MOE_BLOCK))
    hb = tb // 2
    return pl.pallas_call(
        functools.partial(_scatter_kernel, tb=tb, n_blocks=n_blocks, n_tail=n_tail, n_steps=T // tb),
        grid_spec=pltpu.PrefetchScalarGridSpec(
            num_scalar_prefetch=2, grid=(T // tb,),
            in_specs=[pl.BlockSpec(memory_space=pl.ANY),
                      pl.BlockSpec((tb, 1, D), lambda i, pad, na: (i, 0, 0))],
            out_specs=pl.BlockSpec(memory_space=pl.ANY),
            scratch_shapes=[pltpu.VMEM((MOE_BLOCK, 1, D), hn.dtype), pltpu.SMEM((2, 2 * tb), I32),
                            pltpu.VMEM((hb, 1, D), hn.dtype), pltpu.VMEM((hb, 1, D), hn.dtype),
                            pltpu.SemaphoreType.DMA, pltpu.SemaphoreType.DMA,
                            pltpu.SemaphoreType.DMA, pltpu.SemaphoreType.DMA]),
        out_shape=jax.ShapeDtypeStruct((n_blocks * MOE_BLOCK, 1, D), hn.dtype),
        compiler_params=_cparams("arbitrary"),
        name="moe_scatter",
    )(lastblk, nact, dest2, hn)


W_SLOTS = 2


def _expert_kernel(be_ref, na_ref, x_ref, wg_hbm, wu_hbm, wd_hbm, y_ref,
                   wg_buf, wu_buf, wd_buf, wgb_ref, wub_ref, wdb_ref, x2_ref, elist_ref, state_ref, sems):
    j = pl.program_id(0)
    na = na_ref[0]

    def weight_copies(e, slot):
        return [pltpu.make_async_copy(src.at[e], buf.at[slot], sems.at[slot, n])
                for n, (src, buf) in enumerate(((wg_hbm, wg_buf), (wu_hbm, wu_buf), (wd_hbm, wd_buf)))]

    @pl.when(j == 0)
    def _():
        def scan(k, n):
            is_new = (k == 0) | (be_ref[k] != be_ref[jnp.maximum(k - 1, 0)])

            @pl.when(is_new)
            def _():
                elist_ref[n] = be_ref[k]
            return n + is_new.astype(I32)

        n_exp = lax.fori_loop(0, na, scan, 0)
        state_ref[0] = 0
        state_ref[1] = n_exp
        for n in range(W_SLOTS):
            @pl.when(n < n_exp)
            def _():
                for cp in weight_copies(elist_ref[n], n):
                    cp.start()

    @pl.when(j < na)
    def _():
        @pl.when((j == 0) | (be_ref[j] != be_ref[jnp.maximum(j - 1, 0)]))
        def _():
            n = state_ref[0]
            slot = n % W_SLOTS
            for cp in weight_copies(elist_ref[n], slot):
                cp.wait()
            wgb_ref[...] = wg_buf[slot].astype(BF16)
            wub_ref[...] = wu_buf[slot].astype(BF16)
            wdb_ref[...] = wd_buf[slot].astype(BF16)

            @pl.when(n + W_SLOTS < state_ref[1])
            def _():
                for cp in weight_copies(elist_ref[n + W_SLOTS], slot):
                    cp.start()

            state_ref[0] = n + 1

        x2_ref[...] = x_ref[...].reshape(x2_ref.shape)
        lo, hi = _unpack_halves(x2_ref[...])
        lo, hi = lo.astype(BF16), hi.astype(BF16)
        half = lo.shape[1]
        hg = _dot(lo, wgb_ref[:half, :]) + _dot(hi, wgb_ref[half:, :])
        hu = _dot(lo, wub_ref[:half, :]) + _dot(hi, wub_ref[half:, :])
        act = (hg * jax.nn.sigmoid(hg)) * hu
        y = _dot(act.astype(BF16), wdb_ref[...])
        y_ref[...] = _pack_halves(y).reshape(y_ref.shape)

    @pl.when(j >= na)
    def _():
        y_ref[...] = jnp.zeros_like(y_ref)


def _experts(blk_exp, nact, xs, w_gate, w_up, w_down, n_blocks):
    DP = xs.shape[2]
    D = 2 * DP
    rows = n_blocks * MOE_BLOCK
    DE = w_gate.shape[2]
    blk = lambda j, be, na: (jnp.minimum(j, na[0] - 1), 0, 0)
    hbm = pl.BlockSpec(memory_space=pl.ANY)
    return pl.pallas_call(
        _expert_kernel,
        grid_spec=pltpu.PrefetchScalarGridSpec(
            num_scalar_prefetch=2, grid=(n_blocks,),
            in_specs=[pl.BlockSpec((MOE_BLOCK, 1, DP), blk), hbm, hbm, hbm],
            out_specs=pl.BlockSpec((MOE_BLOCK, 1, DP), lambda j, be, na: (j, 0, 0)),
            scratch_shapes=[pltpu.VMEM((W_SLOTS, D, DE), F32), pltpu.VMEM((W_SLOTS, D, DE), F32),
                            pltpu.VMEM((W_SLOTS, DE, D), F32),
                            pltpu.VMEM((D, DE), BF16), pltpu.VMEM((D, DE), BF16), pltpu.VMEM((DE, D), BF16),
                            pltpu.VMEM((MOE_BLOCK, DP), U32), pltpu.SMEM((N_EXPERTS,), I32), pltpu.SMEM((2,), I32),
                            pltpu.SemaphoreType.DMA((W_SLOTS, 3))]),
        out_shape=jax.ShapeDtypeStruct((rows, 1, DP), U32),
        compiler_params=_cparams("arbitrary"),
        name="moe_experts",
    )(blk_exp, nact, xs, w_gate, w_up, w_down)


def _combine_kernel(dest_ref, ys_ref, h_ref, rec_ref, g_ref, o_ref, idx_ref, ya0_ref, ya1_ref, yb0_ref, yb1_ref,
                    y2_ref, sem_a, sem_b, isem, *, normalize, n_steps):
    i = pl.program_id(0)
    tb = h_ref.shape[0]
    hb = tb // 2
    halves = ((ya0_ref, ya1_ref, sem_a), (yb0_ref, yb1_ref, sem_b))

    def idx_copy(step, slot):
        return pltpu.make_async_copy(dest_ref.at[step], idx_ref.at[slot], isem)

    def issue(half, slot):
        bufs, sem = halves[half][:2], halves[half][2]

        def body(t, c):
            for s in range(2):
                d = idx_ref[slot, 2 * (half * hb + t) + s]
                pltpu.make_async_copy(ys_ref.at[pl.ds(d, 1)], bufs[s].at[pl.ds(t, 1)], sem).start(priority=s)
            return c

        lax.fori_loop(0, hb, body, 0, unroll=8)

    @pl.when(i == 0)
    def _():
        first = idx_copy(0, 0)
        first.start()
        first.wait()
        issue(0, 0)
        issue(1, 0)
        if n_steps > 1:
            idx_copy(1, 1).start()

    nslot = (i + 1) % 2

    @pl.when(i + 1 < n_steps)
    def _():
        idx_copy(i + 1, nslot).wait()

    rec = rec_ref[...]
    for half, (y0_ref, y1_ref, sem) in enumerate(halves):
        rows = slice(half * hb, (half + 1) * hb)
        pltpu.make_async_copy(ys_ref.at[pl.ds(0, hb)], y0_ref, sem).wait()
        pltpu.make_async_copy(ys_ref.at[pl.ds(0, hb)], y1_ref, sem).wait()
        y2_ref[...] = y0_ref[...].reshape(y2_ref.shape)
        h = h_ref[rows, :] + rec[rows, 2:3] * jnp.concatenate(_unpack_halves(y2_ref[...]), axis=1)
        y2_ref[...] = y1_ref[...].reshape(y2_ref.shape)
        h = h + rec[rows, 3:4] * jnp.concatenate(_unpack_halves(y2_ref[...]), axis=1)
        if normalize:
            ms = jnp.mean(h * h, axis=-1, keepdims=True)
            h = h * lax.rsqrt(ms + RMS_EPS) * g_ref[...]
        o_ref[rows, :] = h

        @pl.when(i + 1 < n_steps)
        def _():
            issue(half, nslot)

    @pl.when(i + 2 < n_steps)
    def _():
        idx_copy(i + 2, i % 2).start()


def _combine(dest2, ys, h, rec, g, tb, normalize):
    T, D = h.shape
    hb = tb // 2
    return pl.pallas_call(
        functools.partial(_combine_kernel, normalize=normalize, n_steps=T // tb),
        grid=(T // tb,),
        in_specs=[pl.BlockSpec(memory_space=pl.ANY), pl.BlockSpec(memory_space=pl.ANY),
                  pl.BlockSpec((tb, D), lambda i: (i, 0)), pl.BlockSpec((tb, LANES), lambda i: (i, 0)),
                  pl.BlockSpec((1, D), lambda i: (0, 0))],
        out_specs=pl.BlockSpec((tb, D), lambda i: (i, 0)),
        out_shape=jax.ShapeDtypeStruct((T, D), F32),
        scratch_shapes=[pltpu.SMEM((2, 2 * tb), I32)] + [pltpu.VMEM((hb, 1, D // 2), U32)] * 4 +
                       [pltpu.VMEM((hb, D // 2), U32)] + [pltpu.SemaphoreType.DMA] * 3,
        compiler_params=_cparams("arbitrary"),
        name="moe_combine",
    )(dest2, ys, h, rec, g)


def _pad_cols(a, n):
    return jnp.pad(a, ((0, 0), (0, n - a.shape[1])))


def _pad_rows(a, n, at=0):
    return jnp.pad(a, ((at, n - a.shape[0] - at), (0, 0)))


def _layer(x2, B, L, p):
    T, D = x2.shape
    W = p["s5_d"].shape[0]
    G, P = p["s5_lambda_re"].shape
    HG = W // G
    dl, al, gl = p["rwkv_w2"].shape[0], p["rwkv_a2"].shape[0], p["rwkv_g2"].shape[0]
    H = W // RW_N

    w_in = p["w_in"]
    o = W
    cols = [w_in[:, :W], w_in[:, o:o + 3 * W]]
    o += 3 * W
    cols += [_pad_cols(w_in[:, o:o + dl], 128), _pad_cols(w_in[:, o + dl:o + dl + al], 128),
             _pad_cols(w_in[:, o + dl + al:o + dl + al + gl], 256)]
    w_in_r = jnp.concatenate(cols, axis=1).astype(BF16)
    tm_in = min(1024, T)
    proj = _inproj(x2, p["norm_mix_g"].reshape(1, D), w_in_r, tm_in, 1152)

    C = S5_CHUNK
    nc = L // C
    ct_re = jnp.swapaxes(p["s5_c_re"], 1, 2)
    ct_im = jnp.swapaxes(p["s5_c_im"], 1, 2)
    rb = min(LANES, B * nc)
    z = _s5_in(proj.reshape(B * nc, C, proj.shape[1]), G, rb)
    y_t = _s5_chunk(z, p["s5_lambda_re"], p["s5_lambda_im"], p["s5_log_step"],
                    ct_re, ct_im, p["s5_b_re"], p["s5_b_im"], p["s5_c_re"], p["s5_c_im"], nc)
    y_ssm = _s5_out(y_t, rb).reshape(T, W)
    tm = min(512, T)
    s5_out = _s5_glu(y_ssm, proj, p["s5_d"].reshape(1, W), p["s5_w_glu"].astype(BF16),
                     p["s5_b_glu"].reshape(1, W), tm)

    mu = p["rwkv_mu"]
    mu3 = _pad_rows(mu[:3 * W].reshape(3, W), 8)
    o = 3 * W
    mul = jnp.concatenate([_pad_cols(mu[None, o:o + dl], 128), _pad_cols(mu[None, o + dl:o + dl + al], 128),
                           _pad_cols(mu[None, o + dl + al:], 256)], axis=1)
    w2p = _pad_rows(p["rwkv_w2"], 128).astype(BF16)
    a2p = _pad_rows(p["rwkv_a2"], 128).astype(BF16)
    g2p = _pad_rows(p["rwkv_g2"], 256).astype(BF16)
    head_of = jnp.arange(W, dtype=I32) // RW_N
    e_mat = (head_of[:, None] == jnp.arange(LANES, dtype=I32)[None, :]).astype(BF16)
    et_mat = e_mat.T
    row = lambda a: a.reshape(1, W)
    tm_rw = min(256, L)
    r, k, v, nk, bv, lw, g = _rwkv_prep(proj, B, L, mu3, mul, row(p["rwkv_w0"]), row(p["rwkv_a0"]),
                                        row(p["rwkv_k_k"]), row(p["rwkv_k_a"]), w2p, a2p, g2p,
                                        e_mat, et_mat, tm_rw)
    rw_out = _rwkv_chunk(r, k, v, nk, bv, lw, g, p["rwkv_r_k"].reshape(1, W), row(p["rwkv_ln_w"]),
                         row(p["rwkv_ln_b"]), B, L)

    w_out = p["w_out"].astype(BF16)
    w_route = jnp.concatenate([p["w_route_exp"], p["w_route_grp"]], axis=1)
    w_route = _pad_cols(w_route, LANES)
    wr_hi = w_route.astype(BF16)
    wr_lo = (w_route - wr_hi.astype(F32)).astype(BF16)
    wr = jnp.concatenate([wr_hi, wr_lo], axis=1)
    b_route = _pad_cols(jnp.concatenate([p["b_route_exp"], p["b_route_grp"]])[None, :], LANES)
    tm_o = min(512, T)
    h, hn, rec = _outproj(s5_out, rw_out, x2, w_out[:W], w_out[W:], p["norm_ffn_g"].reshape(1, D),
                          wr, b_route, tm_o)

    tb = min(1024, T)
    rank, cnt = _rank(rec, tb)
    n_blocks = -(-2 * T // MOE_BLOCK) + N_EXPERTS
    nb_rows = -(-(n_blocks + 1) // 8) * 8
    dest, blk = _dest(rec, rank, cnt, tb, nb_rows)
    blk_exp = blk[:n_blocks, 0]
    nact = blk[0:1, 1]
    lastblk = blk[nb_rows - 1, :N_EXPERTS]
    ts = min(256, T)
    dest2 = dest[:, :2].reshape(T // ts, 2 * ts)

    xs = _scatter(lastblk, nact, dest2, hn, ts, n_blocks)
    ys = _experts(blk_exp, nact, xs, p["w_gate"], p["w_up"], p["w_down"], n_blocks)
    return dest2, ys, h, rec, ts


def kernel(x, norm_mix_g, w_in, s5_lambda_re, s5_lambda_im, s5_log_step, s5_b_re, s5_b_im, s5_c_re, s5_c_im, s5_d, s5_w_glu, s5_b_glu, rwkv_mu, rwkv_w0, rwkv_w2, rwkv_a0, rwkv_a2, rwkv_g2, rwkv_k_k, rwkv_k_a, rwkv_r_k, rwkv_ln_w, rwkv_ln_b, w_out, norm_ffn_g, w_route_grp, b_route_grp, w_route_exp, b_route_exp, w_gate, w_up, w_down, norm_final_g):
    B, L, D = x.shape
    params = dict(
        norm_mix_g=norm_mix_g, w_in=w_in, s5_lambda_re=s5_lambda_re, s5_lambda_im=s5_lambda_im,
        s5_log_step=s5_log_step, s5_b_re=s5_b_re, s5_b_im=s5_b_im, s5_c_re=s5_c_re, s5_c_im=s5_c_im,
        s5_d=s5_d, s5_w_glu=s5_w_glu, s5_b_glu=s5_b_glu, rwkv_mu=rwkv_mu, rwkv_w0=rwkv_w0, rwkv_w2=rwkv_w2,
        rwkv_a0=rwkv_a0, rwkv_a2=rwkv_a2, rwkv_g2=rwkv_g2, rwkv_k_k=rwkv_k_k, rwkv_k_a=rwkv_k_a,
        rwkv_r_k=rwkv_r_k, rwkv_ln_w=rwkv_ln_w, rwkv_ln_b=rwkv_ln_b, w_out=w_out, norm_ffn_g=norm_ffn_g,
        w_route_grp=w_route_grp, b_route_grp=b_route_grp, w_route_exp=w_route_exp, b_route_exp=b_route_exp,
        w_gate=w_gate, w_up=w_up, w_down=w_down)
    depth = norm_mix_g.shape[0]
    h2 = x.reshape(B * L, D)
    for l in range(depth):
        p = {k_: v_[l] for k_, v_ in params.items()}
        dest2, ys, h, rec, ts = _layer(h2, B, L, p)
        h2 = _combine(dest2, ys, h, rec, norm_final_g.reshape(1, D), ts, normalize=(l == depth - 1))
    return h2.reshape(B, L, D)
```

```python
import functools

import jax
import jax.numpy as jnp
from jax import lax
from jax.experimental import pallas as pl
from jax.experimental.pallas import tpu as pltpu

F32 = jnp.float32
BF16 = jnp.bfloat16
I32 = jnp.int32

RMS_EPS = 1e-6
S5_CHUNK = 64
S5_HG = 16
S5_P = 64
RW_N = 64
RW_CHUNK = 64
RW_GN_EPS = 64e-5
N_GROUPS = 8
EPG = 8
N_EXPERTS = 64
MOE_BLOCK = 256
LANES = 128
VMEM_LIMIT = 56 * 1024 * 1024


def _cparams(*sem, **kw):
    return pltpu.CompilerParams(dimension_semantics=sem, vmem_limit_bytes=VMEM_LIMIT, **kw)


def _split2(x):
    hi = x.astype(BF16)
    lo = (x - hi.astype(F32)).astype(BF16)
    return hi, lo


def _split3(x):
    hi = x.astype(BF16)
    r = x - hi.astype(F32)
    mid = r.astype(BF16)
    lo = (r - mid.astype(F32)).astype(BF16)
    return hi, mid, lo


def _dot(a, b):
    return jnp.dot(a, b, preferred_element_type=F32)


def _dot_nt(a, b):
    return lax.dot_general(a, b, (((1,), (1,)), ((), ())), preferred_element_type=F32)


def _dot_tn(a, b):
    return lax.dot_general(a, b, (((0,), (0,)), ((), ())), preferred_element_type=F32)


def _dot_x3(a, b):
    ah, al = _split2(a)
    bh, bl = _split2(b)
    return _dot(ah, bh) + (_dot(ah, bl) + _dot(al, bh))


def _dot_exact_lhs(a_bf16, b):
    bh, bm, bl = _split3(b)
    return _dot(a_bf16, bh) + (_dot(a_bf16, bm) + _dot(a_bf16, bl))


def _cmul(ar, ai, br, bi):
    return ar * br - ai * bi, ar * bi + ai * br


U32 = jnp.uint32


def _pack_halves(x):
    half = x.shape[1] // 2
    lo = lax.bitcast_convert_type(x[:, :half].astype(BF16).astype(F32), U32)
    hi = lax.bitcast_convert_type(x[:, half:].astype(BF16).astype(F32), U32)
    return (lo >> 16) | hi


def _unpack_halves(p):
    lo = lax.bitcast_convert_type(p << 16, F32)
    hi = lax.bitcast_convert_type(p & jnp.uint32(0xFFFF0000), F32)
    return lo, hi


def _inproj_kernel(x_ref, g_ref, w_ref, o_ref, hn_ref):
    @pl.when(pl.program_id(1) == 0)
    def _():
        x = x_ref[...]
        ms = jnp.mean(x * x, axis=-1, keepdims=True)
        hn_ref[...] = (x * lax.rsqrt(ms + RMS_EPS) * g_ref[...]).astype(BF16)

    o_ref[...] = _dot(hn_ref[...], w_ref[...])


def _inproj(x2, g, w_bf16, tm, tn):
    T, D = x2.shape
    N = w_bf16.shape[1]
    return pl.pallas_call(
        _inproj_kernel,
        grid=(T // tm, N // tn),
        in_specs=[
            pl.BlockSpec((tm, D), lambda i, j: (i, 0)),
            pl.BlockSpec((1, D), lambda i, j: (0, 0)),
            pl.BlockSpec((D, tn), lambda i, j: (0, j)),
        ],
        out_specs=pl.BlockSpec((tm, tn), lambda i, j: (i, j)),
        out_shape=jax.ShapeDtypeStruct((T, N), F32),
        scratch_shapes=[pltpu.VMEM((tm, D), BF16)],
        compiler_params=_cparams("arbitrary", "arbitrary"),
        name="inproj",
    )(x2, g, w_bf16)


def _binpow(ar, ai, expo, nbits):
    pr = jnp.ones(expo.shape, F32)
    pi = jnp.zeros(expo.shape, F32)
    sr, si = ar, ai
    for bit in range(nbits):
        m = ((expo >> bit) & 1) == 1
        nr, ni = _cmul(pr, pi, sr, si)
        pr = jnp.where(m, nr, pr)
        pi = jnp.where(m, ni, pi)
        if bit + 1 < nbits:
            sr, si = _cmul(sr, si, sr, si)
    return pr, pi


def _frame_powers(ar, ai, reverse, plus_one):
    C, HG, P = S5_CHUNK, S5_HG, ar.shape[0]
    fpt = LANES // HG
    j = lax.broadcasted_iota(I32, (P, LANES), 1) // HG
    inner_r, inner_i = _binpow(ar, ai, (fpt - 1 - j) if reverse else j, (fpt - 1).bit_length())
    sr, si = ar, ai
    for _ in range(fpt.bit_length() - 1):
        sr, si = _cmul(sr, si, sr, si)
    outer = [(ar, ai) if plus_one else (jnp.ones_like(ar), jnp.zeros_like(ai))]
    for _ in range(C // fpt - 1):
        outer.append(_cmul(outer[-1][0], outer[-1][1], sr, si))
    if reverse:
        outer = outer[::-1]
    tiles = [_cmul(inner_r, inner_i, o_r, o_i) for o_r, o_i in outer]
    return (jnp.concatenate([t[0] for t in tiles], axis=1), jnp.concatenate([t[1] for t in tiles], axis=1))


def _zoh(lr, li, dt):
    mag = jnp.exp(lr * dt)
    ang = li * dt
    ar, ai = mag * jnp.cos(ang), mag * jnp.sin(ang)
    den = lr * lr + li * li
    nr, ni = ar - 1.0, ai
    fr = (nr * lr + ni * li) / den
    fi = (ni * lr - nr * li) / den
    return ar, ai, fr, fi


def _s5_ops_group(lrc_ref, lic_ref, dt_ref, ct_re_ref, ct_im_ref, bt_re_ref, bt_im_ref, c_re_ref, c_im_ref,
                  m_ref, wbt_ref, wct_ref, a64_ref):
    C, HG, P = S5_CHUNK, S5_HG, S5_P
    W = C * HG
    dt = jnp.exp(dt_ref[...])
    ar_c, ai_c, fr_c, fi_c = _zoh(lrc_ref[...], lic_ref[...], dt)

    pr, pi = _frame_powers(ar_c, ai_c, False, True)
    qr, qi = _frame_powers(ar_c, ai_c, True, False)

    sel = (lax.broadcasted_iota(I32, (HG, W), 1) % HG == lax.broadcasted_iota(I32, (HG, W), 0)).astype(BF16)
    tile = lambda a: _dot_exact_lhs_rhs(a, sel)

    c1_re, c1_im = _cmul(tile(ct_re_ref[...]), tile(ct_im_ref[...]), pr, pi)
    wct_ref[...] = jnp.concatenate([c1_re, -c1_im], axis=0).T.astype(BF16)

    bb_re, bb_im = _cmul(bt_re_ref[...], bt_im_ref[...], fr_c, fi_c)
    ab_re, ab_im = _cmul(tile(bb_re), tile(bb_im), qr, qi)
    wbt_ref[:P, :] = ab_re.astype(BF16)
    wbt_ref[P:, :] = ab_im.astype(BF16)

    cr, ci = ar_c, ai_c
    for _ in range(C.bit_length() - 1):
        cr, ci = _cmul(cr, ci, cr, ci)
    a64_ref[...] = jnp.concatenate([cr, ci], axis=0)

    strip = _dot_x3(c_re_ref[...], ab_re) - _dot_x3(c_im_ref[...], ab_im)
    ext = jnp.concatenate([strip, jnp.zeros((HG, W), F32)], axis=1)
    for t in range(C):
        off = (C - 1 - t) * HG
        m_ref[t * HG:(t + 1) * HG, :] = ext[:, off:off + W].astype(BF16)


S5_FB = 8


def _s5_in_kernel(x_ref, z_ref):
    G, _, rb = z_ref.shape
    for f in range(S5_FB):
        xt = x_ref[:, f, :].T
        z_ref[:, f * S5_HG:(f + 1) * S5_HG, :] = xt.reshape(G, S5_HG, rb).astype(BF16)


def _s5_in(proj3, G, rb):
    R, C, _ = proj3.shape
    W = G * S5_HG
    return pl.pallas_call(
        _s5_in_kernel,
        grid=(R // rb, C // S5_FB),
        in_specs=[pl.BlockSpec((rb, S5_FB, W), lambda i, j: (i, j, 0))],
        out_specs=pl.BlockSpec((G, S5_FB * S5_HG, rb), lambda i, j: (0, j, i)),
        out_shape=jax.ShapeDtypeStruct((G, C * S5_HG, R), BF16),
        compiler_params=_cparams("arbitrary", "arbitrary"),
        name="s5_in",
    )(proj3)


def _s5_out_kernel(yt_ref, o_ref):
    G, _, rb = yt_ref.shape
    for f in range(S5_FB):
        slab = yt_ref[:, f * S5_HG:(f + 1) * S5_HG, :].reshape(G * S5_HG, rb)
        o_ref[:, f, :] = slab.T


def _s5_out(yt, rb):
    G, CW, R = yt.shape
    C = CW // S5_HG
    W = G * S5_HG
    return pl.pallas_call(
        _s5_out_kernel,
        grid=(R // rb, C // S5_FB),
        in_specs=[pl.BlockSpec((G, S5_FB * S5_HG, rb), lambda i, j: (0, j, i))],
        out_specs=pl.BlockSpec((rb, S5_FB, W), lambda i, j: (i, j, 0)),
        out_shape=jax.ShapeDtypeStruct((R, C, W), F32),
        compiler_params=_cparams("arbitrary", "arbitrary"),
        name="s5_out",
    )(yt)


def _s5_chunk_kernel(*refs, n_chunks):
    params, (z_ref, y_ref), ops = refs[:9], refs[9:11], refs[11:]
    for g in range(z_ref.shape[0]):
        at = lambda rs: [r.at[g] for r in rs]
        _s5_ops_group(*at(params), *at(ops))
        _s5_chunk_group(z_ref.at[g], *at(ops), y_ref.at[g], n_chunks)


def _s5_chunk_group(z_ref, m_ref, wbt_ref, wct_ref, a64_ref, y_ref, n_chunks):
    P = S5_P
    z = z_ref[...]
    R = z.shape[1]
    x = _dot(wbt_ref[...], z)
    y_ref[...] = _dot(m_ref[...], z)
    xr, xi = x[:P], x[P:]
    a = a64_ref[...]
    ar, ai = a[:P], a[P:]
    cidx = lax.broadcasted_iota(I32, (1, R), 1) % n_chunks
    shift = 1
    while shift < n_chunks:
        keep = cidx >= shift
        sr = jnp.where(keep, pltpu.roll(xr, shift, 1), 0.0)
        si = jnp.where(keep, pltpu.roll(xi, shift, 1), 0.0)
        xr, xi = xr + (ar * sr - ai * si), xi + (ar * si + ai * sr)
        ar, ai = ar * ar - ai * ai, 2.0 * ar * ai
        shift *= 2
    keep = cidx >= 1
    s_in = jnp.concatenate([jnp.where(keep, pltpu.roll(xr, 1, 1), 0.0),
                            jnp.where(keep, pltpu.roll(xi, 1, 1), 0.0)], axis=0)
    sh, sl = _split2(s_in)
    wct = wct_ref[...]
    y_ref[...] = y_ref[...] + (_dot(wct, sh) + _dot(wct, sl))


def _s5_chunk(z, lam_re, lam_im, log_step, ct_re, ct_im, bt_re, bt_im, c_re, c_im, n_chunks):
    G, W, R = z.shape
    P, HG = lam_re.shape[1], S5_HG
    P2 = 2 * P
    gps = 2 if G % 2 == 0 else 1
    col = lambda a: a.reshape(G, P, 1)
    g3 = lambda s1, s2: pl.BlockSpec((gps, s1, s2), lambda g: (g, 0, 0))
    return pl.pallas_call(
        functools.partial(_s5_chunk_kernel, n_chunks=n_chunks),
        grid=(G // gps,),
        in_specs=[g3(P, 1), g3(P, 1), g3(1, 1),
                  g3(P, HG), g3(P, HG), g3(P, HG), g3(P, HG), g3(HG, P), g3(HG, P), g3(W, R)],
        out_specs=g3(W, R),
        out_shape=jax.ShapeDtypeStruct((G, W, R), F32),
        scratch_shapes=[pltpu.VMEM((gps, W, W), BF16), pltpu.VMEM((gps, P2, W), BF16),
                        pltpu.VMEM((gps, W, P2), BF16), pltpu.VMEM((gps, P2, 1), F32)],
        compiler_params=_cparams("arbitrary"),
        name="s5_chunk",
    )(col(lam_re), col(lam_im), log_step.reshape(G, 1, 1), ct_re, ct_im, bt_re, bt_im, c_re, c_im, z)


def _gelu_tanh(x):
    return 0.5 * x * (1.0 + jnp.tanh(0.7978845608028654 * (x + 0.044715 * (x * x * x))))


def _s5_glu_kernel(y_ref, u_ref, d_ref, w_ref, b_ref, o_ref):
    y = y_ref[...] + d_ref[...] * u_ref[...]
    y = _gelu_tanh(y)
    z = _dot(y.astype(BF16), w_ref[...]) + b_ref[...]
    o_ref[...] = (y * jax.nn.sigmoid(z)).astype(BF16)


def _s5_glu(y_ssm, proj, d, w_bf16, b, tm):
    T, W = y_ssm.shape
    return pl.pallas_call(
        _s5_glu_kernel,
        grid=(T // tm,),
        in_specs=[
            pl.BlockSpec((tm, W), lambda i: (i, 0)),
            pl.BlockSpec((tm, W), lambda i: (i, 0)),
            pl.BlockSpec((1, W), lambda i: (0, 0)),
            pl.BlockSpec((W, W), lambda i: (0, 0)),
            pl.BlockSpec((1, W), lambda i: (0, 0)),
        ],
        out_specs=pl.BlockSpec((tm, W), lambda i: (i, 0)),
        out_shape=jax.ShapeDtypeStruct((T, W), BF16),
        compiler_params=_cparams("arbitrary"),
        name="s5_glu",
    )(y_ssm, proj, d, w_bf16, b)


def _head_sum(x, e_ref, et_ref):
    xh, xl = _split2(x)
    e = e_ref[...]
    s = _dot(xh, e) + _dot(xl, e)
    sh, sl = _split2(s)
    et = et_ref[...]
    return _dot(sh, et) + _dot(sl, et)


def _shift(z, prev_row):
    rolled = pltpu.roll(z, 1, 0)
    first = lax.broadcasted_iota(I32, (z.shape[0], 1), 0) == 0
    return jnp.where(first, prev_row, rolled)


def _rwkv_prep_kernel(zr_ref, zk_ref, zv_ref, zl_ref, mu_ref, mul_ref, w0_ref, a0_ref, kk_ref, ka_ref,
                      w2_ref, a2_ref, g2_ref, e_ref, et_ref,
                      r_ref, k_ref, v_ref, nk_ref, b_ref, lw_ref, g_ref,
                      car_ref, carl_ref):
    W = r_ref.shape[1]

    @pl.when(pl.program_id(1) == 0)
    def _():
        car_ref[...] = jnp.zeros_like(car_ref)
        carl_ref[...] = jnp.zeros_like(carl_ref)

    def lerp(z, prev_row, mu):
        return z + (_shift(z, prev_row) - z) * mu

    tm = zr_ref.shape[0]
    zr, zk, zv, zl = zr_ref[...], zk_ref[...], zv_ref[...], zl_ref[...]
    r = lerp(zr, car_ref[0:1, :], mu_ref[0:1, :])
    k = lerp(zk, car_ref[1:2, :], mu_ref[1:2, :])
    v = lerp(zv, car_ref[2:3, :], mu_ref[2:3, :])
    xl = lerp(zl, carl_ref[0:1, :], mul_ref[...])
    car_ref[0:1, :] = zr[tm - 1:tm, :]
    car_ref[1:2, :] = zk[tm - 1:tm, :]
    car_ref[2:3, :] = zv[tm - 1:tm, :]
    carl_ref[0:1, :] = zl[tm - 1:tm, :]

    xw, xa, xg = xl[:, 0:128], xl[:, 128:256], xl[:, 256:512]
    dw = _dot(jnp.tanh(xw).astype(BF16), w2_ref[...])
    da = _dot(xa.astype(BF16), a2_ref[...])
    g = _dot(jax.nn.sigmoid(xg).astype(BF16), g2_ref[...])

    zw = -(w0_ref[...] + dw)
    softplus = jnp.maximum(zw, 0.0) + jnp.log(1.0 + jnp.exp(-jnp.abs(zw)))
    w_log = -softplus - 0.5
    a = jax.nn.sigmoid(a0_ref[...] + da)

    kk = k * kk_ref[...]
    n2 = _head_sum(kk * kk, e_ref, et_ref)
    kk = kk / jnp.maximum(jnp.sqrt(n2), 1e-12)

    r_ref[...] = r.astype(BF16)
    k_ref[...] = (k * (1.0 + (a - 1.0) * ka_ref[...])).astype(BF16)
    v_ref[...] = v.astype(BF16)
    nk_ref[...] = kk.astype(BF16)
    b_ref[...] = (kk * a).astype(BF16)
    lw_ref[...] = -jnp.exp(w_log)
    g_ref[...] = g.astype(BF16)


def _rwkv_prep(proj, B, L, mu3, mul, w0, a0, k_k, k_a, w2p, a2p, g2p, e_mat, et_mat, tm):
    T = B * L
    W = w0.shape[1]
    nt = L // tm
    row = lambda c: pl.BlockSpec((tm, W), lambda b, i, c=c: (b * nt + i, c))
    full = lambda a: pl.BlockSpec(a.shape, lambda b, i: (0, 0))
    out = pl.BlockSpec((tm, W), lambda b, i: (b * nt + i, 0))
    lw = 512
    return pl.pallas_call(
        _rwkv_prep_kernel,
        grid=(B, nt),
        in_specs=[row(1), row(2), row(3),
                  pl.BlockSpec((tm, lw), lambda b, i: (b * nt + i, 4 * W // lw)),
                  full(mu3), full(mul), full(w0), full(a0), full(k_k), full(k_a),
                  full(w2p), full(a2p), full(g2p), full(e_mat), full(et_mat)],
        out_specs=[out] * 7,
        out_shape=[jax.ShapeDtypeStruct((T, W), dt) for dt in (BF16, BF16, BF16, BF16, BF16, F32, BF16)],
        scratch_shapes=[pltpu.VMEM((8, W), F32), pltpu.VMEM((8, lw), F32)],
        compiler_params=_cparams("arbitrary", "arbitrary"),
        name="rwkv_prep",
    )(proj, proj, proj, proj, mu3, mul, w0, a0, k_k, k_a, w2p, a2p, g2p, e_mat, et_mat)


def _rwkv_chunk_kernel(r_ref, k_ref, v_ref, nk_ref, b_ref, lw_ref, g_ref, rk_ref, lnw_ref, lnb_ref,
                       o_ref, z_ref, ys_ref, qs_ref, vs_ref, gs_ref):
    C, N = RW_CHUNK, RW_N
    H = r_ref.shape[1] // N

    @pl.when(pl.program_id(1) == 0)
    def _():
        z_ref[...] = jnp.zeros_like(z_ref)
        ys_ref[...] = jnp.zeros_like(ys_ref)
        qs_ref[...] = jnp.zeros_like(qs_ref)
        vs_ref[...] = jnp.zeros_like(vs_ref)
        gs_ref[...] = jnp.zeros_like(gs_ref)

    low = lax.broadcasted_iota(I32, (C, 2 * N), 1) < N
    inv_n = 1.0 / N

    def head_sums(t):
        s0 = jnp.sum(jnp.where(low, t, 0.0), axis=-1, keepdims=True)
        s1 = jnp.sum(jnp.where(low, 0.0, t), axis=-1, keepdims=True)
        return jnp.where(low, s0, s1)

    for p in range(H // 2):
        ps = slice(2 * p * N, 2 * (p + 1) * N)
        yp = ys_ref[:, ps]
        yc = yp - head_sums(yp) * inv_n
        var = head_sums(yc * yc) * inv_n
        yn = yc * lax.rsqrt(var + RW_GN_EPS) * lnw_ref[:, ps] + lnb_ref[:, ps]
        bonus = head_sums(qs_ref[:, ps]) * vs_ref[:, ps].astype(F32)
        o_ref[:, ps] = ((yn + bonus) * gs_ref[:, ps].astype(F32)).astype(BF16)

    ri = lax.broadcasted_iota(I32, (C, C), 0)
    ci = lax.broadcasted_iota(I32, (C, C), 1)
    tril = ri >= ci
    stril = ri > ci
    eye = (ri == ci).astype(F32)

    lw = lw_ref[...]
    cs = _dot_exact_lhs(tril.astype(BF16), lw)
    tot = cs[C - 1:C, :]
    p_inc = jnp.exp(cs)
    p_inv = jnp.exp(-cs)
    p_prev = jnp.exp(cs - lw)
    p_rest = jnp.exp(tot - cs)
    p_tot = jnp.exp(tot)

    r_f, k_f, b_f = r_ref[...].astype(F32), k_ref[...].astype(F32), b_ref[...].astype(F32)
    r_t = r_f * p_inc
    k_t = k_f * p_inv
    a_t = -nk_ref[...].astype(F32) * p_prev
    b_t = b_f * p_inv
    k_h = k_f * p_rest
    b_h = b_f * p_rest
    v_all = v_ref[...]

    hs = range(H)
    sl = [slice(h * N, (h + 1) * N) for h in hs]
    each = lambda f: [f(h) for h in hs]
    v = each(lambda h: v_all[:, sl[h]].astype(BF16))
    ar = each(lambda h: jnp.concatenate([a_t[:, sl[h]], r_t[:, sl[h]]], axis=0).astype(BF16))
    m_b = each(lambda h: _dot_nt(ar[h], b_t[:, sl[h]].astype(BF16)))
    m_k = each(lambda h: _dot_nt(ar[h], k_t[:, sl[h]].astype(BF16)))
    l_ab = each(lambda h: jnp.where(stril, m_b[h][:C], 0.0).astype(BF16))
    m_rb = each(lambda h: jnp.where(tril, m_b[h][C:], 0.0).astype(BF16))
    both = jnp.concatenate([stril, tril], axis=0)
    lm_k = each(lambda h: jnp.where(both, m_k[h], 0.0).astype(BF16))
    lmv = each(lambda h: _dot(lm_k[h], v[h]))

    x = each(lambda h: jnp.concatenate([lmv[h][:C], a_t[:, sl[h]]], axis=1))
    y0 = each(lambda h: lmv[h][C:])
    hk = each(lambda h: _dot_tn(k_h[:, sl[h]].astype(BF16), v[h]))
    lp = l_ab
    step = 1
    while step < C:
        x = each(lambda h: x[h] + _dot(lp[h], x[h].astype(BF16)))
        step *= 2
        if step < C:
            lp = each(lambda h: _dot(lp[h], lp[h]).astype(BF16))
    xb = each(lambda h: x[h].astype(BF16))
    yq = each(lambda h: _dot(m_rb[h], xb[h]))
    gh = each(lambda h: _dot_tn(b_h[:, sl[h]].astype(BF16), xb[h]))

    z = each(lambda h: _split2(z_ref[h]))
    qg = each(lambda h: jnp.concatenate([yq[h][:, N:] + r_t[:, sl[h]],
                                         gh[h][:, N:] + eye * p_tot[:, sl[h]]], axis=0).astype(BF16))
    qgz = each(lambda h: _dot(qg[h], z[h][0]) + _dot(qg[h], z[h][1]))
    yz = each(lambda h: qgz[h][:C])
    gz = each(lambda h: qgz[h][C:])
    for h in hs:
        z_ref[h] = (gh[h][:, :N] + hk[h]) + gz[h]
        ys_ref[:, sl[h]] = (yq[h][:, :N] + y0[h]) + yz[h]
    qs_ref[...] = r_f * k_f * rk_ref[...]
    vs_ref[...] = v_all
    gs_ref[...] = g_ref[...]


def _rwkv_chunk(r, k, v, nk, b, lw, g, r_k, ln_w, ln_b, B, L):
    T, W = r.shape
    C = RW_CHUNK
    nc = L // C
    H = W // RW_N
    assert H % 2 == 0
    spec = pl.BlockSpec((C, W), lambda bi, c: (bi * nc + jnp.minimum(c, nc - 1), 0))
    full = pl.BlockSpec((1, W), lambda bi, c: (0, 0))
    return pl.pallas_call(
        _rwkv_chunk_kernel,
        grid=(B, nc + 1),
        in_specs=[spec] * 7 + [full] * 3,
        out_specs=pl.BlockSpec((C, W), lambda bi, c: (bi * nc + jnp.maximum(c - 1, 0), 0)),
        out_shape=jax.ShapeDtypeStruct((T, W), BF16),
        scratch_shapes=[pltpu.VMEM((H, RW_N, RW_N), F32), pltpu.VMEM((C, W), F32), pltpu.VMEM((C, W), F32),
                        pltpu.VMEM((C, W), BF16), pltpu.VMEM((C, W), BF16)],
        compiler_params=_cparams("arbitrary", "arbitrary"),
        name="rwkv_chunk",
    )(r, k, v, nk, b, lw, g, r_k, ln_w, ln_b)


def _first_index_of_max(vals, lane, valid):
    neg = jnp.float32(-jnp.inf)
    masked = jnp.where(valid, vals, neg)
    m = jnp.max(masked, axis=-1, keepdims=True)
    idx = jnp.min(jnp.where(valid & (masked == m), lane, LANES), axis=-1, keepdims=True)
    return m, idx


def _outproj_kernel(s5_ref, rw_ref, x_ref, wt_ref, wb_ref, g_ref, wr_ref, br_ref,
                    h_ref, hn_ref, rec_ref):
    h = x_ref[...] + (_dot(s5_ref[...], wt_ref[...]) + _dot(rw_ref[...], wb_ref[...]))
    h_ref[...] = h
    ms = jnp.mean(h * h, axis=-1, keepdims=True)
    hn = h * lax.rsqrt(ms + RMS_EPS) * g_ref[...]
    hn_ref[...] = _pack_halves(hn).reshape(hn_ref.shape)

    hh, hl = _split2(hn)
    wh, wl = wr_ref[:, :LANES], wr_ref[:, LANES:]
    logits = _dot(hh, wh) + (_dot(hh, wl) + _dot(hl, wh)) + br_ref[...]
    lane = lax.broadcasted_iota(I32, logits.shape, 1)
    is_grp = (lane >= N_EXPERTS) & (lane < N_EXPERTS + N_GROUPS)
    gmax, gidx = _first_index_of_max(logits, lane, is_grp)
    gsum = jnp.sum(jnp.where(is_grp, jnp.exp(logits - gmax), 0.0), axis=-1, keepdims=True)
    p_grp = 1.0 / gsum
    grp = gidx - N_EXPERTS
    in_grp = (lane >= grp * EPG) & (lane < (grp + 1) * EPG)
    m1, i1 = _first_index_of_max(logits, lane, in_grp)
    m2, i2 = _first_index_of_max(logits, lane, in_grp & (lane != i1))
    e = jnp.exp(m2 - m1)
    g1 = p_grp / (1.0 + e)
    g2 = p_grp * e / (1.0 + e)
    rec = jnp.where(lane == 0, i1.astype(F32),
          jnp.where(lane == 1, i2.astype(F32),
          jnp.where(lane == 2, g1, jnp.where(lane == 3, g2, 0.0))))
    rec_ref[...] = rec


def _outproj(s5o, rwo, x2, w_top, w_bot, g, wr, b_route, tm):
    T, D = x2.shape
    W = s5o.shape[1]
    full = lambda a: pl.BlockSpec(a.shape, lambda i: (0, 0))
    return pl.pallas_call(
        _outproj_kernel,
        grid=(T // tm,),
        in_specs=[pl.BlockSpec((tm, W), lambda i: (i, 0)), pl.BlockSpec((tm, W), lambda i: (i, 0)),
                  pl.BlockSpec((tm, D), lambda i: (i, 0)),
                  full(w_top), full(w_bot), full(g), full(wr), full(b_route)],
        out_specs=[pl.BlockSpec((tm, D), lambda i: (i, 0)), pl.BlockSpec((tm, 1, D // 2), lambda i: (i, 0, 0)),
                   pl.BlockSpec((tm, LANES), lambda i: (i, 0))],
        out_shape=[jax.ShapeDtypeStruct((T, D), F32), jax.ShapeDtypeStruct((T, 1, D // 2), U32),
                   jax.ShapeDtypeStruct((T, LANES), F32)],
        compiler_params=_cparams("arbitrary"),
        name="outproj_route",
    )(s5o, rwo, x2, w_top, w_bot, g, wr, b_route)


def _onehots(rec, lane):
    oh0 = (lane == rec[:, 0:1].astype(I32)).astype(F32)
    oh1 = (lane == rec[:, 1:2].astype(I32)).astype(F32)
    return oh0, oh1


def _rank_kernel(rec_ref, rank_ref, cnt_ref, base_ref):
    tb = rec_ref.shape[0]

    @pl.when(pl.program_id(0) == 0)
    def _():
        base_ref[...] = jnp.zeros_like(base_ref)

    lane = lax.broadcasted_iota(I32, (tb, LANES), 1)
    oh0, oh1 = _onehots(rec_ref[...], lane)
    both = oh0 + oh1
    ri = lax.broadcasted_iota(I32, (tb, tb), 0)
    ci = lax.broadcasted_iota(I32, (tb, tb), 1)
    before = _dot((ri > ci).astype(BF16), both.astype(BF16)) + base_ref[0:1, :]
    rank0 = jnp.sum(oh0 * before, axis=-1, keepdims=True)
    rank1 = jnp.sum(oh1 * before, axis=-1, keepdims=True)
    rank_ref[...] = jnp.where(lane == 0, rank0, jnp.where(lane == 1, rank1, 0.0))
    total = base_ref[0:1, :] + jnp.sum(both, axis=0, keepdims=True)
    base_ref[0:1, :] = total
    cnt_ref[...] = jnp.broadcast_to(total, cnt_ref.shape)


def _rank(rec, tb):
    T = rec.shape[0]
    return pl.pallas_call(
        _rank_kernel,
        grid=(T // tb,),
        in_specs=[pl.BlockSpec((tb, LANES), lambda i: (i, 0))],
        out_specs=[pl.BlockSpec((tb, LANES), lambda i: (i, 0)), pl.BlockSpec((8, LANES), lambda i: (0, 0))],
        out_shape=[jax.ShapeDtypeStruct((T, LANES), F32), jax.ShapeDtypeStruct((8, LANES), F32)],
        scratch_shapes=[pltpu.VMEM((8, LANES), F32)],
        compiler_params=_cparams("arbitrary"),
        name="moe_rank",
    )(rec)


def _padded_starts(cnt):
    padded = jnp.ceil(cnt * (1.0 / MOE_BLOCK)) * MOE_BLOCK
    ri = lax.broadcasted_iota(I32, (LANES, LANES), 0)
    ci = lax.broadcasted_iota(I32, (LANES, LANES), 1)
    p8 = jnp.broadcast_to(padded, (8, LANES))
    pend = _dot_exact_lhs_rhs(p8, (ri <= ci).astype(BF16))[0:1, :]
    return pend - padded, pend


def _dot_exact_lhs_rhs(a, b_bf16):
    ah, am, al = _split3(a)
    return _dot(ah, b_bf16) + (_dot(am, b_bf16) + _dot(al, b_bf16))


def _dest_kernel(rec_ref, rank_ref, cnt_ref, dest_ref, blk_ref):
    tb = rec_ref.shape[0]
    cnt = cnt_ref[0:1, :]
    pstart, pend = _padded_starts(cnt)
    lane = lax.broadcasted_iota(I32, (tb, LANES), 1)
    oh0, oh1 = _onehots(rec_ref[...], lane)
    rank = rank_ref[...]
    d0 = jnp.sum(oh0 * pstart, axis=-1, keepdims=True) + rank[:, 0:1]
    d1 = jnp.sum(oh1 * pstart, axis=-1, keepdims=True) + rank[:, 1:2]
    dest_ref[...] = jnp.where(lane == 0, d0, jnp.where(lane == 1, d1, 0.0)).astype(I32)

    @pl.when(pl.program_id(0) == 0)
    def _():
        nb = blk_ref.shape[0]
        blane = lax.broadcasted_iota(I32, (nb, LANES), 1)
        bstart = (lax.broadcasted_iota(I32, (nb, 1), 0) * MOE_BLOCK).astype(F32)
        is_e = blane < N_EXPERTS
        bexp = jnp.sum(jnp.where(is_e & (pend <= bstart), 1.0, 0.0), axis=-1, keepdims=True)
        bexp = jnp.minimum(bexp, N_EXPERTS - 1.0)
        nact = jnp.max(jnp.where(is_e, pend, 0.0), axis=-1, keepdims=True) * (1.0 / MOE_BLOCK)
        lastblk = jnp.where(is_e & (pend > pstart), pend - MOE_BLOCK, -1.0)
        brow = lax.broadcasted_iota(I32, (nb, 1), 0)
        out = jnp.where(blane == 0, bexp, jnp.where(blane == 1, nact, 0.0))
        out = jnp.where(brow == nb - 1, lastblk, out)
        blk_ref[...] = out.astype(I32)


def _dest(rec, rank, cnt, tb, nb_rows):
    T = rec.shape[0]
    return pl.pallas_call(
        _dest_kernel,
        grid=(T // tb,),
        in_specs=[pl.BlockSpec((tb, LANES), lambda i: (i, 0)), pl.BlockSpec((tb, LANES), lambda i: (i, 0)),
                  pl.BlockSpec((8, LANES), lambda i: (0, 0))],
        out_specs=[pl.BlockSpec((tb, LANES), lambda i: (i, 0)), pl.BlockSpec((nb_rows, LANES), lambda i: (0, 0))],
        out_shape=[jax.ShapeDtypeStruct((T, LANES), I32), jax.ShapeDtypeStruct((nb_rows, LANES), I32)],
        compiler_params=_cparams("arbitrary"),
        name="moe_dest",
    )(rec, rank, cnt)


def _scatter_kernel(pad_ref, na_ref, dest_ref, hn_ref, xs_ref, zbuf_ref, idx_ref, sa_ref, sb_ref,
                    sem_a, sem_b, isem, zsem, *, tb, n_blocks, n_tail, n_steps):
    i = pl.program_id(0)

    @pl.when(i == 0)
    def _():
        zbuf_ref[...] = jnp.zeros_like(zbuf_ref)

        def zero_rows(start):
            return pltpu.make_async_copy(zbuf_ref, xs_ref.at[pl.ds(start, MOE_BLOCK)], zsem)

        def pad_start(e, c):
            @pl.when(pad_ref[e] >= 0)
            def _():
                zero_rows(pad_ref[e]).start()
            return c

        def pad_wait(e, c):
            @pl.when(pad_ref[e] >= 0)
            def _():
                zero_rows(pad_ref[e]).wait()
            return c

        def tail_start(k, c):
            @pl.when(na_ref[0] + k < n_blocks)
            def _():
                zero_rows((na_ref[0] + k) * MOE_BLOCK).start()
            return c

        def tail_wait(k, c):
            @pl.when(na_ref[0] + k < n_blocks)
            def _():
                zero_rows((na_ref[0] + k) * MOE_BLOCK).wait()
            return c

        lax.fori_loop(0, N_EXPERTS, pad_start, 0)
        lax.fori_loop(0, n_tail, tail_start, 0)
        lax.fori_loop(0, N_EXPERTS, pad_wait, 0)
        lax.fori_loop(0, n_tail, tail_wait, 0)

    def idx_copy(step, slot):
        return pltpu.make_async_copy(dest_ref.at[step], idx_ref.at[pl.ds(slot * (2 * tb), 2 * tb)], isem)

    @pl.when(i == 0)
    def _():
        idx_copy(0, 0).start()

    slot = i % 2
    idx_copy(i, slot).wait()

    @pl.when(i + 1 < n_steps)
    def _():
        idx_copy(i + 1, 1 - slot).start()

    hb = tb // 2
    for half, (buf, sem) in enumerate(((sa_ref, sem_a), (sb_ref, sem_b))):
        def wait_rows(buf=buf, sem=sem):
            for _ in range(2):
                pltpu.make_async_copy(buf, xs_ref.at[pl.ds(0, hb)], sem).wait()

        @pl.when(i > 0)
        def _():
            wait_rows()

        buf[...] = hn_ref[half * hb:(half + 1) * hb]

        def issue(t, c, buf=buf, sem=sem, half=half):
            for s in range(2):
                d = idx_ref[slot * (2 * tb) + 2 * (half * hb + t) + s]
                pltpu.make_async_copy(buf.at[pl.ds(t, 1)], xs_ref.at[pl.ds(d, 1)], sem).start(priority=s)
            return c

        lax.fori_loop(0, hb, issue, 0, unroll=8)

    @pl.when(i == n_steps - 1)
    def _():
        for buf, sem in ((sa_ref, sem_a), (sb_ref, sem_b)):
            for _ in range(2):
                pltpu.make_async_copy(buf, xs_ref.at[pl.ds(0, hb)], sem).wait()


def _scatter(lastblk, nact, dest2, hn, tb, n_blocks):
    T, _, D = hn.shape
    n_tail = n_blocks - (-(-2 * T // MOE_BLOCK))
    hb = tb // 2
    return pl.pallas_call(
        functools.partial(_scatter_kernel, tb=tb, n_blocks=n_blocks, n_tail=n_tail, n_steps=T // tb),
        grid_spec=pltpu.PrefetchScalarGridSpec(
            num_scalar_prefetch=2, grid=(T // tb,),
            in_specs=[pl.BlockSpec(memory_space=pl.ANY),
                      pl.BlockSpec((tb, 1, D), lambda i, pad, na: (i, 0, 0))],
            out_specs=pl.BlockSpec(memory_space=pl.ANY),
            scratch_shapes=[pltpu.VMEM((MOE_BLOCK, 1, D), hn.dtype), pltpu.SMEM((4 * tb,), I32),
                            pltpu.VMEM((hb, 1, D), hn.dtype), pltpu.VMEM((hb, 1, D), hn.dtype),
                            pltpu.SemaphoreType.DMA, pltpu.SemaphoreType.DMA,
                            pltpu.SemaphoreType.DMA, pltpu.SemaphoreType.DMA]),
        out_shape=jax.ShapeDtypeStruct((n_blocks * MOE_BLOCK, 1, D), hn.dtype),
        compiler_params=_cparams("arbitrary"),
        name="moe_scatter",
    )(lastblk, nact, dest2, hn)


W_SLOTS = 2


def _expert_kernel(be_ref, na_ref, x_ref, wg_hbm, wu_hbm, wd_hbm, y_ref,
                   wg_buf, wu_buf, wd_buf, wgb_ref, wub_ref, wdb_ref, x2_ref, elist_ref, state_ref, sems):
    j = pl.program_id(0)
    na = na_ref[0]

    def weight_copies(e, slot):
        return [pltpu.make_async_copy(src.at[e], buf.at[slot], sems.at[slot, n])
                for n, (src, buf) in enumerate(((wg_hbm, wg_buf), (wu_hbm, wu_buf), (wd_hbm, wd_buf)))]

    @pl.when(j == 0)
    def _():
        def scan(k, n):
            is_new = (k == 0) | (be_ref[k] != be_ref[jnp.maximum(k - 1, 0)])

            @pl.when(is_new)
            def _():
                elist_ref[n] = be_ref[k]
            return n + is_new.astype(I32)

        n_exp = lax.fori_loop(0, na, scan, 0)
        state_ref[0] = 0
        state_ref[1] = n_exp
        for n in range(W_SLOTS):
            @pl.when(n < n_exp)
            def _():
                for cp in weight_copies(elist_ref[n], n):
                    cp.start()

    @pl.when(j < na)
    def _():
        @pl.when((j == 0) | (be_ref[j] != be_ref[jnp.maximum(j - 1, 0)]))
        def _():
            n = state_ref[0]
            slot = n % W_SLOTS
            for cp in weight_copies(elist_ref[n], slot):
                cp.wait()
            wgb_ref[...] = wg_buf[slot].astype(BF16)
            wub_ref[...] = wu_buf[slot].astype(BF16)
            wdb_ref[...] = wd_buf[slot].astype(BF16)

            @pl.when(n + W_SLOTS < state_ref[1])
            def _():
                for cp in weight_copies(elist_ref[n + W_SLOTS], slot):
                    cp.start()

            state_ref[0] = n + 1

        x2_ref[...] = x_ref[...].reshape(x2_ref.shape)
        lo, hi = _unpack_halves(x2_ref[...])
        lo, hi = lo.astype(BF16), hi.astype(BF16)
        half = lo.shape[1]
        hg = _dot(lo, wgb_ref[:half, :]) + _dot(hi, wgb_ref[half:, :])
        hu = _dot(lo, wub_ref[:half, :]) + _dot(hi, wub_ref[half:, :])
        act = (hg * jax.nn.sigmoid(hg)) * hu
        y = _dot(act.astype(BF16), wdb_ref[...])
        y_ref[...] = _pack_halves(y).reshape(y_ref.shape)

    @pl.when(j >= na)
    def _():
        y_ref[...] = jnp.zeros_like(y_ref)


def _experts(blk_exp, nact, xs, w_gate, w_up, w_down, n_blocks):
    DP = xs.shape[2]
    D = 2 * DP
    rows = n_blocks * MOE_BLOCK
    DE = w_gate.shape[2]
    blk = lambda j, be, na: (jnp.minimum(j, na[0] - 1), 0, 0)
    hbm = pl.BlockSpec(memory_space=pl.ANY)
    return pl.pallas_call(
        _expert_kernel,
        grid_spec=pltpu.PrefetchScalarGridSpec(
            num_scalar_prefetch=2, grid=(n_blocks,),
            in_specs=[pl.BlockSpec((MOE_BLOCK, 1, DP), blk), hbm, hbm, hbm],
            out_specs=pl.BlockSpec((MOE_BLOCK, 1, DP), lambda j, be, na: (j, 0, 0)),
            scratch_shapes=[pltpu.VMEM((W_SLOTS, D, DE), F32), pltpu.VMEM((W_SLOTS, D, DE), F32),
                            pltpu.VMEM((W_SLOTS, DE, D), F32),
                            pltpu.VMEM((D, DE), BF16), pltpu.VMEM((D, DE), BF16), pltpu.VMEM((DE, D), BF16),
                            pltpu.VMEM((MOE_BLOCK, DP), U32), pltpu.SMEM((N_EXPERTS,), I32), pltpu.SMEM((2,), I32),
                            pltpu.SemaphoreType.DMA((W_SLOTS, 3))]),
        out_shape=jax.ShapeDtypeStruct((rows, 1, DP), U32),
        compiler_params=_cparams("arbitrary"),
        name="moe_experts",
    )(blk_exp, nact, xs, w_gate, w_up, w_down)


def _combine_kernel(dest_ref, ys_ref, h_ref, rec_ref, g_ref, o_ref, idx_ref, ya0_ref, ya1_ref, yb0_ref, yb1_ref,
                    y2_ref, sem_a, sem_b, isem, *, normalize, n_steps):
    i = pl.program_id(0)
    tb = h_ref.shape[0]
    hb = tb // 2
    halves = ((ya0_ref, ya1_ref, sem_a), (yb0_ref, yb1_ref, sem_b))

    def idx_copy(step, slot):
        return pltpu.make_async_copy(dest_ref.at[step], idx_ref.at[pl.ds(slot * (2 * tb), 2 * tb)], isem)

    def issue(half, slot):
        bufs, sem = halves[half][:2], halves[half][2]

        def body(t, c):
            for s in range(2):
                d = idx_ref[slot * (2 * tb) + 2 * (half * hb + t) + s]
                pltpu.make_async_copy(ys_ref.at[pl.ds(d, 1)], bufs[s].at[pl.ds(t, 1)], sem).start(priority=s)
            return c

        lax.fori_loop(0, hb, body, 0, unroll=8)

    @pl.when(i == 0)
    def _():
        first = idx_copy(0, 0)
        first.start()
        first.wait()
        issue(0, 0)
        issue(1, 0)
        if n_steps > 1:
            idx_copy(1, 1).start()

    nslot = (i + 1) % 2

    @pl.when(i + 1 < n_steps)
    def _():
        idx_copy(i + 1, nslot).wait()

    rec = rec_ref[...]
    for half, (y0_ref, y1_ref, sem) in enumerate(halves):
        rows = slice(half * hb, (half + 1) * hb)
        pltpu.make_async_copy(ys_ref.at[pl.ds(0, hb)], y0_ref, sem).wait()
        pltpu.make_async_copy(ys_ref.at[pl.ds(0, hb)], y1_ref, sem).wait()
        y2_ref[...] = y0_ref[...].reshape(y2_ref.shape)
        h = h_ref[rows, :] + rec[rows, 2:3] * jnp.concatenate(_unpack_halves(y2_ref[...]), axis=1)
        y2_ref[...] = y1_ref[...].reshape(y2_ref.shape)
        h = h + rec[rows, 3:4] * jnp.concatenate(_unpack_halves(y2_ref[...]), axis=1)
        if normalize:
            ms = jnp.mean(h * h, axis=-1, keepdims=True)
            h = h * lax.rsqrt(ms + RMS_EPS) * g_ref[...]
        o_ref[rows, :] = h

        @pl.when(i + 1 < n_steps)
        def _():
            issue(half, nslot)

    @pl.when(i + 2 < n_steps)
    def _():
        idx_copy(i + 2, i % 2).start()


def _combine(dest2, ys, h, rec, g, tb, normalize):
    T, D = h.shape
    hb = tb // 2
    return pl.pallas_call(
        functools.partial(_combine_kernel, normalize=normalize, n_steps=T // tb),
        grid=(T // tb,),
        in_specs=[pl.BlockSpec(memory_space=pl.ANY), pl.BlockSpec(memory_space=pl.ANY),
                  pl.BlockSpec((tb, D), lambda i: (i, 0)), pl.BlockSpec((tb, LANES), lambda i: (i, 0)),
                  pl.BlockSpec((1, D), lambda i: (0, 0))],
        out_specs=pl.BlockSpec((tb, D), lambda i: (i, 0)),
        out_shape=jax.ShapeDtypeStruct((T, D), F32),
        scratch_shapes=[pltpu.SMEM((4 * tb,), I32)] + [pltpu.VMEM((hb, 1, D // 2), U32)] * 4 +
                       [pltpu.VMEM((hb, D // 2), U32)] + [pltpu.SemaphoreType.DMA] * 3,
        compiler_params=_cparams("arbitrary"),
        name="moe_combine",
    )(dest2, ys, h, rec, g)


def _pad_cols(a, n):
    return jnp.pad(a, ((0, 0), (0, n - a.shape[1])))


def _pad_rows(a, n, at=0):
    return jnp.pad(a, ((at, n - a.shape[0] - at), (0, 0)))


def _layer(x2, B, L, p):
    T, D = x2.shape
    W = p["s5_d"].shape[0]
    G, P = p["s5_lambda_re"].shape
    HG = W // G
    dl, al, gl = p["rwkv_w2"].shape[0], p["rwkv_a2"].shape[0], p["rwkv_g2"].shape[0]
    H = W // RW_N

    w_in = p["w_in"]
    o = W
    cols = [w_in[:, :W], w_in[:, o:o + 3 * W]]
    o += 3 * W
    cols += [_pad_cols(w_in[:, o:o + dl], 128), _pad_cols(w_in[:, o + dl:o + dl + al], 128),
             _pad_cols(w_in[:, o + dl + al:o + dl + al + gl], 256)]
    w_in_r = jnp.concatenate(cols, axis=1).astype(BF16)
    tm_in = min(1024, T)
    proj = _inproj(x2, p["norm_mix_g"].reshape(1, D), w_in_r, tm_in, 1152)

    C = S5_CHUNK
    nc = L // C
    ct_re = jnp.swapaxes(p["s5_c_re"], 1, 2)
    ct_im = jnp.swapaxes(p["s5_c_im"], 1, 2)
    rb = min(LANES, B * nc)
    z = _s5_in(proj.reshape(B * nc, C, proj.shape[1]), G, rb)
    y_t = _s5_chunk(z, p["s5_lambda_re"], p["s5_lambda_im"], p["s5_log_step"],
                    ct_re, ct_im, p["s5_b_re"], p["s5_b_im"], p["s5_c_re"], p["s5_c_im"], nc)
    y_ssm = _s5_out(y_t, rb).reshape(T, W)
    tm = min(512, T)
    s5_out = _s5_glu(y_ssm, proj, p["s5_d"].reshape(1, W), p["s5_w_glu"].astype(BF16),
                     p["s5_b_glu"].reshape(1, W), tm)

    mu = p["rwkv_mu"]
    mu3 = _pad_rows(mu[:3 * W].reshape(3, W), 8)
    o = 3 * W
    mul = jnp.concatenate([_pad_cols(mu[None, o:o + dl], 128), _pad_cols(mu[None, o + dl:o + dl + al], 128),
                           _pad_cols(mu[None, o + dl + al:], 256)], axis=1)
    w2p = _pad_rows(p["rwkv_w2"], 128).astype(BF16)
    a2p = _pad_rows(p["rwkv_a2"], 128).astype(BF16)
    g2p = _pad_rows(p["rwkv_g2"], 256).astype(BF16)
    head_of = jnp.arange(W, dtype=I32) // RW_N
    e_mat = (head_of[:, None] == jnp.arange(LANES, dtype=I32)[None, :]).astype(BF16)
    et_mat = e_mat.T
    row = lambda a: a.reshape(1, W)
    tm_rw = min(256, L)
    r, k, v, nk, bv, lw, g = _rwkv_prep(proj, B, L, mu3, mul, row(p["rwkv_w0"]), row(p["rwkv_a0"]),
                                        row(p["rwkv_k_k"]), row(p["rwkv_k_a"]), w2p, a2p, g2p,
                                        e_mat, et_mat, tm_rw)
    rw_out = _rwkv_chunk(r, k, v, nk, bv, lw, g, p["rwkv_r_k"].reshape(1, W), row(p["rwkv_ln_w"]),
                         row(p["rwkv_ln_b"]), B, L)

    w_out = p["w_out"].astype(BF16)
    w_route = jnp.concatenate([p["w_route_exp"], p["w_route_grp"]], axis=1)
    w_route = _pad_cols(w_route, LANES)
    wr_hi = w_route.astype(BF16)
    wr_lo = (w_route - wr_hi.astype(F32)).astype(BF16)
    wr = jnp.concatenate([wr_hi, wr_lo], axis=1)
    b_route = _pad_cols(jnp.concatenate([p["b_route_exp"], p["b_route_grp"]])[None, :], LANES)
    tm_o = min(512, T)
    h, hn, rec = _outproj(s5_out, rw_out, x2, w_out[:W], w_out[W:], p["norm_ffn_g"].reshape(1, D),
                          wr, b_route, tm_o)

    tb = min(1024, T)
    rank, cnt = _rank(rec, tb)
    n_blocks = -(-2 * T // MOE_BLOCK) + N_EXPERTS
    nb_rows = -(-(n_blocks + 1) // 8) * 8
    dest, blk = _dest(rec, rank, cnt, tb, nb_rows)
    blk_exp = blk[:n_blocks, 0]
    nact = blk[0:1, 1]
    lastblk = blk[nb_rows - 1, :N_EXPERTS]
    ts = min(256, T)
    dest2 = dest[:, :2].reshape(T // ts, 2 * ts)

    xs = _scatter(lastblk, nact, dest2, hn, ts, n_blocks)
    ys = _experts(blk_exp, nact, xs, p["w_gate"], p["w_up"], p["w_down"], n_blocks)
    return dest2, ys, h, rec, ts


def kernel(x, norm_mix_g, w_in, s5_lambda_re, s5_lambda_im, s5_log_step, s5_b_re, s5_b_im, s5_c_re, s5_c_im, s5_d, s5_w_glu, s5_b_glu, rwkv_mu, rwkv_w0, rwkv_w2, rwkv_a0, rwkv_a2, rwkv_g2, rwkv_k_k, rwkv_k_a, rwkv_r_k, rwkv_ln_w, rwkv_ln_b, w_out, norm_ffn_g, w_route_grp, b_route_grp, w_route_exp, b_route_exp, w_gate, w_up, w_down, norm_final_g):
    B, L, D = x.shape
    params = dict(
        norm_mix_g=norm_mix_g, w_in=w_in, s5_lambda_re=s5_lambda_re, s5_lambda_im=s5_lambda_im,
        s5_log_step=s5_log_step, s5_b_re=s5_b_re, s5_b_im=s5_b_im, s5_c_re=s5_c_re, s5_c_im=s5_c_im,
        s5_d=s5_d, s5_w_glu=s5_w_glu, s5_b_glu=s5_b_glu, rwkv_mu=rwkv_mu, rwkv_w0=rwkv_w0, rwkv_w2=rwkv_w2,
        rwkv_a0=rwkv_a0, rwkv_a2=rwkv_a2, rwkv_g2=rwkv_g2, rwkv_k_k=rwkv_k_k, rwkv_k_a=rwkv_k_a,
        rwkv_r_k=rwkv_r_k, rwkv_ln_w=rwkv_ln_w, rwkv_ln_b=rwkv_ln_b, w_out=w_out, norm_ffn_g=norm_ffn_g,
        w_route_grp=w_route_grp, b_route_grp=b_route_grp, w_route_exp=w_route_exp, b_route_exp=b_route_exp,
        w_gate=w_gate, w_up=w_up, w_down=w_down)
    depth = norm_mix_g.shape[0]
    h2 = x.reshape(B * L, D)
    for l in range(depth):
        p = {k_: v_[l] for k_, v_ in params.items()}
        dest2, ys, h, rec, ts = _layer(h2, B, L, p)
        h2 = _combine(dest2, ys, h, rec, norm_final_g.reshape(1, D), ts, normalize=(l == depth - 1))
    return h2.reshape(B, L, D)
```

```python
import functools

import jax
import jax.numpy as jnp
from jax import lax
from jax.experimental import pallas as pl
from jax.experimental.pallas import tpu as pltpu

F32 = jnp.float32
BF16 = jnp.bfloat16
I32 = jnp.int32

RMS_EPS = 1e-6
S5_CHUNK = 64
S5_HG = 16
S5_P = 64
RW_N = 64
RW_CHUNK = 64
RW_GN_EPS = 64e-5
N_GROUPS = 8
EPG = 8
N_EXPERTS = 64
MOE_BLOCK = 256
LANES = 128
VMEM_LIMIT = 56 * 1024 * 1024


def _cparams(*sem, **kw):
    return pltpu.CompilerParams(dimension_semantics=sem, vmem_limit_bytes=VMEM_LIMIT, **kw)


def _split2(x):
    hi = x.astype(BF16)
    lo = (x - hi.astype(F32)).astype(BF16)
    return hi, lo


def _split3(x):
    hi = x.astype(BF16)
    r = x - hi.astype(F32)
    mid = r.astype(BF16)
    lo = (r - mid.astype(F32)).astype(BF16)
    return hi, mid, lo


def _dot(a, b):
    return jnp.dot(a, b, preferred_element_type=F32)


def _dot_nt(a, b):
    return lax.dot_general(a, b, (((1,), (1,)), ((), ())), preferred_element_type=F32)


def _dot_tn(a, b):
    return lax.dot_general(a, b, (((0,), (0,)), ((), ())), preferred_element_type=F32)


def _dot_x3(a, b):
    ah, al = _split2(a)
    bh, bl = _split2(b)
    return _dot(ah, bh) + (_dot(ah, bl) + _dot(al, bh))


def _dot_exact_lhs(a_bf16, b):
    bh, bm, bl = _split3(b)
    return _dot(a_bf16, bh) + (_dot(a_bf16, bm) + _dot(a_bf16, bl))


def _cmul(ar, ai, br, bi):
    return ar * br - ai * bi, ar * bi + ai * br


U32 = jnp.uint32


def _pack_halves(x):
    half = x.shape[1] // 2
    lo = lax.bitcast_convert_type(x[:, :half].astype(BF16).astype(F32), U32)
    hi = lax.bitcast_convert_type(x[:, half:].astype(BF16).astype(F32), U32)
    return (lo >> 16) | hi


def _unpack_halves(p):
    lo = lax.bitcast_convert_type(p << 16, F32)
    hi = lax.bitcast_convert_type(p & jnp.uint32(0xFFFF0000), F32)
    return lo, hi


def _inproj_kernel(x_ref, g_ref, w_ref, o_ref, hn_ref):
    @pl.when(pl.program_id(1) == 0)
    def _():
        x = x_ref[...]
        ms = jnp.mean(x * x, axis=-1, keepdims=True)
        hn_ref[...] = (x * lax.rsqrt(ms + RMS_EPS) * g_ref[...]).astype(BF16)

    o_ref[...] = _dot(hn_ref[...], w_ref[...])


def _inproj(x2, g, w_bf16, tm, tn):
    T, D = x2.shape
    N = w_bf16.shape[1]
    return pl.pallas_call(
        _inproj_kernel,
        grid=(T // tm, N // tn),
        in_specs=[
            pl.BlockSpec((tm, D), lambda i, j: (i, 0)),
            pl.BlockSpec((1, D), lambda i, j: (0, 0)),
            pl.BlockSpec((D, tn), lambda i, j: (0, j)),
        ],
        out_specs=pl.BlockSpec((tm, tn), lambda i, j: (i, j)),
        out_shape=jax.ShapeDtypeStruct((T, N), F32),
        scratch_shapes=[pltpu.VMEM((tm, D), BF16)],
        compiler_params=_cparams("arbitrary", "arbitrary"),
        name="inproj",
    )(x2, g, w_bf16)


def _binpow(ar, ai, expo, nbits):
    pr = jnp.ones(expo.shape, F32)
    pi = jnp.zeros(expo.shape, F32)
    sr, si = ar, ai
    for bit in range(nbits):
        m = ((expo >> bit) & 1) == 1
        nr, ni = _cmul(pr, pi, sr, si)
        pr = jnp.where(m, nr, pr)
        pi = jnp.where(m, ni, pi)
        if bit + 1 < nbits:
            sr, si = _cmul(sr, si, sr, si)
    return pr, pi


def _frame_powers(ar, ai, reverse, plus_one):
    C, HG, P = S5_CHUNK, S5_HG, ar.shape[0]
    fpt = LANES // HG
    j = lax.broadcasted_iota(I32, (P, LANES), 1) // HG
    inner_r, inner_i = _binpow(ar, ai, (fpt - 1 - j) if reverse else j, (fpt - 1).bit_length())
    sr, si = ar, ai
    for _ in range(fpt.bit_length() - 1):
        sr, si = _cmul(sr, si, sr, si)
    outer = [(ar, ai) if plus_one else (jnp.ones_like(ar), jnp.zeros_like(ai))]
    for _ in range(C // fpt - 1):
        outer.append(_cmul(outer[-1][0], outer[-1][1], sr, si))
    if reverse:
        outer = outer[::-1]
    tiles = [_cmul(inner_r, inner_i, o_r, o_i) for o_r, o_i in outer]
    return (jnp.concatenate([t[0] for t in tiles], axis=1), jnp.concatenate([t[1] for t in tiles], axis=1))


def _zoh(lr, li, dt):
    mag = jnp.exp(lr * dt)
    ang = li * dt
    ar, ai = mag * jnp.cos(ang), mag * jnp.sin(ang)
    den = lr * lr + li * li
    nr, ni = ar - 1.0, ai
    fr = (nr * lr + ni * li) / den
    fi = (ni * lr - nr * li) / den
    return ar, ai, fr, fi


def _s5_ops_group(lrc_ref, lic_ref, dt_ref, ct_re_ref, ct_im_ref, bt_re_ref, bt_im_ref, c_re_ref, c_im_ref,
                  m_ref, wbt_ref, wct_ref, a64_ref):
    C, HG, P = S5_CHUNK, S5_HG, S5_P
    W = C * HG
    dt = jnp.exp(dt_ref[...])
    ar_c, ai_c, fr_c, fi_c = _zoh(lrc_ref[...], lic_ref[...], dt)

    pr, pi = _frame_powers(ar_c, ai_c, False, True)
    qr, qi = _frame_powers(ar_c, ai_c, True, False)

    sel = (lax.broadcasted_iota(I32, (HG, W), 1) % HG == lax.broadcasted_iota(I32, (HG, W), 0)).astype(BF16)
    tile = lambda a: _dot_exact_lhs_rhs(a, sel)

    c1_re, c1_im = _cmul(tile(ct_re_ref[...]), tile(ct_im_ref[...]), pr, pi)
    wct_ref[...] = jnp.concatenate([c1_re, -c1_im], axis=0).T.astype(BF16)

    bb_re, bb_im = _cmul(bt_re_ref[...], bt_im_ref[...], fr_c, fi_c)
    ab_re, ab_im = _cmul(tile(bb_re), tile(bb_im), qr, qi)
    wbt_ref[:P, :] = ab_re.astype(BF16)
    wbt_ref[P:, :] = ab_im.astype(BF16)

    cr, ci = ar_c, ai_c
    for _ in range(C.bit_length() - 1):
        cr, ci = _cmul(cr, ci, cr, ci)
    a64_ref[...] = jnp.concatenate([cr, ci], axis=0)

    strip = _dot_x3(c_re_ref[...], ab_re) - _dot_x3(c_im_ref[...], ab_im)
    ext = jnp.concatenate([strip, jnp.zeros((HG, W), F32)], axis=1)
    for t in range(C):
        off = (C - 1 - t) * HG
        m_ref[t * HG:(t + 1) * HG, :] = ext[:, off:off + W].astype(BF16)


S5_FB = 8


def _s5_in_kernel(x_ref, z_ref):
    G, _, rb = z_ref.shape
    for f in range(S5_FB):
        xt = x_ref[:, f, :].T
        z_ref[:, f * S5_HG:(f + 1) * S5_HG, :] = xt.reshape(G, S5_HG, rb).astype(BF16)


def _s5_in(proj3, G, rb):
    R, C, _ = proj3.shape
    W = G * S5_HG
    return pl.pallas_call(
        _s5_in_kernel,
        grid=(R // rb, C // S5_FB),
        in_specs=[pl.BlockSpec((rb, S5_FB, W), lambda i, j: (i, j, 0))],
        out_specs=pl.BlockSpec((G, S5_FB * S5_HG, rb), lambda i, j: (0, j, i)),
        out_shape=jax.ShapeDtypeStruct((G, C * S5_HG, R), BF16),
        compiler_params=_cparams("arbitrary", "arbitrary"),
        name="s5_in",
    )(proj3)


def _s5_out_kernel(yt_ref, o_ref):
    G, _, rb = yt_ref.shape
    for f in range(S5_FB):
        slab = yt_ref[:, f * S5_HG:(f + 1) * S5_HG, :].reshape(G * S5_HG, rb)
        o_ref[:, f, :] = slab.T


def _s5_out(yt, rb):
    G, CW, R = yt.shape
    C = CW // S5_HG
    W = G * S5_HG
    return pl.pallas_call(
        _s5_out_kernel,
        grid=(R // rb, C // S5_FB),
        in_specs=[pl.BlockSpec((G, S5_FB * S5_HG, rb), lambda i, j: (0, j, i))],
        out_specs=pl.BlockSpec((rb, S5_FB, W), lambda i, j: (i, j, 0)),
        out_shape=jax.ShapeDtypeStruct((R, C, W), F32),
        compiler_params=_cparams("arbitrary", "arbitrary"),
        name="s5_out",
    )(yt)


def _s5_chunk_kernel(*refs, n_chunks):
    params, (z_ref, y_ref), ops = refs[:9], refs[9:11], refs[11:]
    for g in range(z_ref.shape[0]):
        at = lambda rs: [r.at[g] for r in rs]
        _s5_ops_group(*at(params), *at(ops))
        _s5_chunk_group(z_ref.at[g], *at(ops), y_ref.at[g], n_chunks)


def _s5_chunk_group(z_ref, m_ref, wbt_ref, wct_ref, a64_ref, y_ref, n_chunks):
    P = S5_P
    z = z_ref[...]
    R = z.shape[1]
    x = _dot(wbt_ref[...], z)
    y_ref[...] = _dot(m_ref[...], z)
    xr, xi = x[:P], x[P:]
    a = a64_ref[...]
    ar, ai = a[:P], a[P:]
    cidx = lax.broadcasted_iota(I32, (1, R), 1) % n_chunks
    shift = 1
    while shift < n_chunks:
        keep = cidx >= shift
        sr = jnp.where(keep, pltpu.roll(xr, shift, 1), 0.0)
        si = jnp.where(keep, pltpu.roll(xi, shift, 1), 0.0)
        xr, xi = xr + (ar * sr - ai * si), xi + (ar * si + ai * sr)
        ar, ai = ar * ar - ai * ai, 2.0 * ar * ai
        shift *= 2
    keep = cidx >= 1
    s_in = jnp.concatenate([jnp.where(keep, pltpu.roll(xr, 1, 1), 0.0),
                            jnp.where(keep, pltpu.roll(xi, 1, 1), 0.0)], axis=0)
    sh, sl = _split2(s_in)
    wct = wct_ref[...]
    y_ref[...] = y_ref[...] + (_dot(wct, sh) + _dot(wct, sl))


def _s5_chunk(z, lam_re, lam_im, log_step, ct_re, ct_im, bt_re, bt_im, c_re, c_im, n_chunks):
    G, W, R = z.shape
    P, HG = lam_re.shape[1], S5_HG
    P2 = 2 * P
    gps = 2 if G % 2 == 0 else 1
    col = lambda a: a.reshape(G, P, 1)
    g3 = lambda s1, s2: pl.BlockSpec((gps, s1, s2), lambda g: (g, 0, 0))
    return pl.pallas_call(
        functools.partial(_s5_chunk_kernel, n_chunks=n_chunks),
        grid=(G // gps,),
        in_specs=[g3(P, 1), g3(P, 1), g3(1, 1),
                  g3(P, HG), g3(P, HG), g3(P, HG), g3(P, HG), g3(HG, P), g3(HG, P), g3(W, R)],
        out_specs=g3(W, R),
        out_shape=jax.ShapeDtypeStruct((G, W, R), F32),
        scratch_shapes=[pltpu.VMEM((gps, W, W), BF16), pltpu.VMEM((gps, P2, W), BF16),
                        pltpu.VMEM((gps, W, P2), BF16), pltpu.VMEM((gps, P2, 1), F32)],
        compiler_params=_cparams("arbitrary"),
        name="s5_chunk",
    )(col(lam_re), col(lam_im), log_step.reshape(G, 1, 1), ct_re, ct_im, bt_re, bt_im, c_re, c_im, z)


def _gelu_tanh(x):
    return 0.5 * x * (1.0 + jnp.tanh(0.7978845608028654 * (x + 0.044715 * (x * x * x))))


def _s5_glu_kernel(y_ref, u_ref, d_ref, w_ref, b_ref, o_ref):
    y = y_ref[...] + d_ref[...] * u_ref[...]
    y = _gelu_tanh(y)
    z = _dot(y.astype(BF16), w_ref[...]) + b_ref[...]
    o_ref[...] = (y * jax.nn.sigmoid(z)).astype(BF16)


def _s5_glu(y_ssm, proj, d, w_bf16, b, tm):
    T, W = y_ssm.shape
    return pl.pallas_call(
        _s5_glu_kernel,
        grid=(T // tm,),
        in_specs=[
            pl.BlockSpec((tm, W), lambda i: (i, 0)),
            pl.BlockSpec((tm, W), lambda i: (i, 0)),
            pl.BlockSpec((1, W), lambda i: (0, 0)),
            pl.BlockSpec((W, W), lambda i: (0, 0)),
            pl.BlockSpec((1, W), lambda i: (0, 0)),
        ],
        out_specs=pl.BlockSpec((tm, W), lambda i: (i, 0)),
        out_shape=jax.ShapeDtypeStruct((T, W), BF16),
        compiler_params=_cparams("arbitrary"),
        name="s5_glu",
    )(y_ssm, proj, d, w_bf16, b)


def _shift(z, prev_row):
    rolled = pltpu.roll(z, 1, 0)
    first = lax.broadcasted_iota(I32, (z.shape[0], 1), 0) == 0
    return jnp.where(first, prev_row, rolled)


def _rwkv_chunk_kernel(zr_ref, zk_ref, zv_ref, zl_ref, mu_ref, mul_ref, w0_ref, a0_ref, kk_ref, ka_ref,
                       w2_ref, a2_ref, g2_ref, rk_ref, lnw_ref, lnb_ref,
                       o_ref, z_ref, ys_ref, qs_ref, vs_ref, gs_ref, car_ref, carl_ref):
    C, N = RW_CHUNK, RW_N
    NS = zr_ref.shape[0]
    H = zr_ref.shape[2] // N

    @pl.when(pl.program_id(1) == 0)
    def _():
        for ref in (z_ref, ys_ref, qs_ref, vs_ref, gs_ref, car_ref, carl_ref):
            ref[...] = jnp.zeros_like(ref)

    low = lax.broadcasted_iota(I32, (C, 2 * N), 1) < N
    inv_n = 1.0 / N

    def head_sums(t):
        s0 = jnp.sum(jnp.where(low, t, 0.0), axis=-1, keepdims=True)
        s1 = jnp.sum(jnp.where(low, 0.0, t), axis=-1, keepdims=True)
        return jnp.where(low, s0, s1)

    for s in range(NS):
        for p in range(H // 2):
            ps = slice(2 * p * N, 2 * (p + 1) * N)
            yp = ys_ref[s, :, ps]
            yc = yp - head_sums(yp) * inv_n
            var = head_sums(yc * yc) * inv_n
            yn = yc * lax.rsqrt(var + RW_GN_EPS) * lnw_ref[:, ps] + lnb_ref[:, ps]
            bonus = head_sums(qs_ref[s, :, ps]) * vs_ref[s, :, ps].astype(F32)
            o_ref[s, :, ps] = ((yn + bonus) * gs_ref[s, :, ps].astype(F32)).astype(BF16)

    ri = lax.broadcasted_iota(I32, (C, C), 0)
    ci = lax.broadcasted_iota(I32, (C, C), 1)
    tril = ri >= ci
    stril = ri > ci
    eye = (ri == ci).astype(F32)
    both = jnp.concatenate([stril, tril], axis=0)

    def head_sums_all(t):
        return jnp.concatenate([head_sums(t[:, 2 * p * N:2 * (p + 1) * N]) for p in range(H // 2)], axis=1)

    def lerp(zz, prev_row, mu):
        return zz + (_shift(zz, prev_row) - zz) * mu

    ops = {}
    for s in range(NS):
        zr, zk, zv, zl = zr_ref[s], zk_ref[s], zv_ref[s], zl_ref[s]
        r_f = lerp(zr, car_ref[s, 0:1, :], mu_ref[0:1, :])
        k_raw = lerp(zk, car_ref[s, 1:2, :], mu_ref[1:2, :])
        v_f = lerp(zv, car_ref[s, 2:3, :], mu_ref[2:3, :])
        xl = lerp(zl, carl_ref[s, 0:1, :], mul_ref[...])
        car_ref[s, 0:1, :] = zr[C - 1:C, :]
        car_ref[s, 1:2, :] = zk[C - 1:C, :]
        car_ref[s, 2:3, :] = zv[C - 1:C, :]
        carl_ref[s, 0:1, :] = zl[C - 1:C, :]
        xw, xa, xg = xl[:, 0:128], xl[:, 128:256], xl[:, 256:512]
        dw = _dot(jnp.tanh(xw).astype(BF16), w2_ref[...])
        da = _dot(xa.astype(BF16), a2_ref[...])
        g_f = _dot(jax.nn.sigmoid(xg).astype(BF16), g2_ref[...])
        zw = -(w0_ref[...] + dw)
        w_log = -(jnp.maximum(zw, 0.0) + jnp.log(1.0 + jnp.exp(-jnp.abs(zw)))) - 0.5
        a_sig = jax.nn.sigmoid(a0_ref[...] + da)
        kk = k_raw * kk_ref[...]
        kk = kk / jnp.maximum(jnp.sqrt(head_sums_all(kk * kk)), 1e-12)
        k_f = k_raw * (1.0 + (a_sig - 1.0) * ka_ref[...])
        b_f = kk * a_sig
        lw = -jnp.exp(w_log)

        cs = _dot_exact_lhs(tril.astype(BF16), lw)
        tot = cs[C - 1:C, :]
        p_inc = jnp.exp(cs)
        p_inv = jnp.exp(-cs)
        p_prev = jnp.exp(cs - lw)
        p_rest = jnp.exp(tot - cs)
        p_tot = jnp.exp(tot)

        v_b = v_f.astype(BF16)
        arrays = dict(r_t=(r_f * p_inc).astype(BF16), k_t=(k_f * p_inv).astype(BF16),
                      a_t=(-kk * p_prev).astype(BF16), b_t=(b_f * p_inv).astype(BF16),
                      k_h=(k_f * p_rest).astype(BF16), b_h=(b_f * p_rest).astype(BF16), v=v_b, p_tot=p_tot)
        for h in range(H):
            ops[(s, h)] = {name: a[:, h * N:(h + 1) * N] for name, a in arrays.items()}
        qs_ref[s] = r_f * k_f * rk_ref[...]
        vs_ref[s] = v_b
        gs_ref[s] = g_f.astype(BF16)

    _rwkv_heads(ops, stril, tril, both, eye, z_ref, ys_ref)


def _rwkv_heads(ops, stril, tril, both, eye, z_ref, ys_ref):
    C, N = RW_CHUNK, RW_N
    hs = list(ops)
    each = lambda f: {h: f(h) for h in hs}
    get = lambda name: (lambda h: ops[h][name])
    a_t, r_t, b_t, k_t, k_h, b_h, p_tot = (get(n) for n in ("a_t", "r_t", "b_t", "k_t", "k_h", "b_h", "p_tot"))
    v = each(lambda h: ops[h]["v"].astype(BF16))
    ar = each(lambda h: jnp.concatenate([a_t(h), r_t(h)], axis=0).astype(BF16))
    m_b = each(lambda h: _dot_nt(ar[h], b_t(h).astype(BF16)))
    m_k = each(lambda h: _dot_nt(ar[h], k_t(h).astype(BF16)))
    l_ab = each(lambda h: jnp.where(stril, m_b[h][:C], 0.0).astype(BF16))
    m_rb = each(lambda h: jnp.where(tril, m_b[h][C:], 0.0).astype(BF16))
    lm_k = each(lambda h: jnp.where(both, m_k[h], 0.0).astype(BF16))
    lmv = each(lambda h: _dot(lm_k[h], v[h]))

    x = each(lambda h: jnp.concatenate([lmv[h][:C], a_t(h).astype(F32)], axis=1))
    y0 = each(lambda h: lmv[h][C:])
    hk = each(lambda h: _dot_tn(k_h(h).astype(BF16), v[h]))
    lp = l_ab
    step = 1
    while step < C:
        x = each(lambda h: x[h] + _dot(lp[h], x[h].astype(BF16)))
        step *= 2
        if step < C:
            lp = each(lambda h: _dot(lp[h], lp[h]).astype(BF16))
    xb = each(lambda h: x[h].astype(BF16))
    yq = each(lambda h: _dot(m_rb[h], xb[h]))
    gh = each(lambda h: _dot_tn(b_h(h).astype(BF16), xb[h]))

    z = each(lambda h: _split2(z_ref[h[0], h[1]]))
    qg = each(lambda h: jnp.concatenate([yq[h][:, N:] + r_t(h).astype(F32),
                                         gh[h][:, N:] + eye * p_tot(h)], axis=0).astype(BF16))
    qgz = each(lambda h: _dot(qg[h], z[h][0]) + _dot(qg[h], z[h][1]))
    yz = each(lambda h: qgz[h][:C])
    gz = each(lambda h: qgz[h][C:])
    for s, hd in hs:
        z_ref[s, hd] = (gh[s, hd][:, :N] + hk[s, hd]) + gz[s, hd]
        ys_ref[s, :, hd * N:(hd + 1) * N] = (yq[s, hd][:, :N] + y0[s, hd]) + yz[s, hd]


RW_SEQS_PER_STEP = 1


def _rwkv_chunk(proj, B, L, mu3, mul, w0, a0, k_k, k_a, w2p, a2p, g2p, r_k, ln_w, ln_b):
    T = B * L
    W = w0.shape[1]
    C = RW_CHUNK
    nc = L // C
    H = W // RW_N
    assert H % 2 == 0
    ns = RW_SEQS_PER_STEP if B % RW_SEQS_PER_STEP == 0 else 1
    lw = mul.shape[1]
    proj3 = proj.reshape(B, L, proj.shape[1])
    cols = lambda width, j: pl.BlockSpec((ns, C, width), lambda bi, c: (bi, jnp.minimum(c, nc - 1), j))
    full = lambda a: pl.BlockSpec(a.shape, lambda bi, c: (0, 0))
    consts = (mu3, mul, w0, a0, k_k, k_a, w2p, a2p, g2p, r_k, ln_w, ln_b)
    out = pl.pallas_call(
        _rwkv_chunk_kernel,
        grid=(B // ns, nc + 1),
        in_specs=[cols(W, 1), cols(W, 2), cols(W, 3), cols(lw, 4 * W // lw)] + [full(a) for a in consts],
        out_specs=pl.BlockSpec((ns, C, W), lambda bi, c: (bi, jnp.maximum(c - 1, 0), 0)),
        out_shape=jax.ShapeDtypeStruct((B, L, W), BF16),
        scratch_shapes=[pltpu.VMEM((ns, H, RW_N, RW_N), F32), pltpu.VMEM((ns, C, W), F32),
                        pltpu.VMEM((ns, C, W), F32), pltpu.VMEM((ns, C, W), BF16), pltpu.VMEM((ns, C, W), BF16),
                        pltpu.VMEM((ns, 8, W), F32), pltpu.VMEM((ns, 8, lw), F32)],
        compiler_params=_cparams("arbitrary", "arbitrary"),
        name="rwkv_chunk",
    )(proj3, proj3, proj3, proj3, *consts)
    return out.reshape(T, W)


def _first_index_of_max(vals, lane, valid):
    neg = jnp.float32(-jnp.inf)
    masked = jnp.where(valid, vals, neg)
    m = jnp.max(masked, axis=-1, keepdims=True)
    idx = jnp.min(jnp.where(valid & (masked == m), lane, LANES), axis=-1, keepdims=True)
    return m, idx


def _outproj_kernel(s5_ref, rw_ref, x_ref, wt_ref, wb_ref, g_ref, wr_ref, br_ref,
                    h_ref, hn_ref, rec_ref):
    h = x_ref[...] + (_dot(s5_ref[...], wt_ref[...]) + _dot(rw_ref[...], wb_ref[...]))
    h_ref[...] = h
    ms = jnp.mean(h * h, axis=-1, keepdims=True)
    hn = h * lax.rsqrt(ms + RMS_EPS) * g_ref[...]
    hn_ref[...] = _pack_halves(hn).reshape(hn_ref.shape)

    hh, hl = _split2(hn)
    wh, wl = wr_ref[:, :LANES], wr_ref[:, LANES:]
    logits = _dot(hh, wh) + (_dot(hh, wl) + _dot(hl, wh)) + br_ref[...]
    lane = lax.broadcasted_iota(I32, logits.shape, 1)
    is_grp = (lane >= N_EXPERTS) & (lane < N_EXPERTS + N_GROUPS)
    gmax, gidx = _first_index_of_max(logits, lane, is_grp)
    gsum = jnp.sum(jnp.where(is_grp, jnp.exp(logits - gmax), 0.0), axis=-1, keepdims=True)
    p_grp = 1.0 / gsum
    grp = gidx - N_EXPERTS
    in_grp = (lane >= grp * EPG) & (lane < (grp + 1) * EPG)
    m1, i1 = _first_index_of_max(logits, lane, in_grp)
    m2, i2 = _first_index_of_max(logits, lane, in_grp & (lane != i1))
    e = jnp.exp(m2 - m1)
    g1 = p_grp / (1.0 + e)
    g2 = p_grp * e / (1.0 + e)
    rec = jnp.where(lane == 0, i1.astype(F32),
          jnp.where(lane == 1, i2.astype(F32),
          jnp.where(lane == 2, g1, jnp.where(lane == 3, g2, 0.0))))
    rec_ref[...] = rec


def _outproj(s5o, rwo, x2, w_top, w_bot, g, wr, b_route, tm):
    T, D = x2.shape
    W = s5o.shape[1]
    full = lambda a: pl.BlockSpec(a.shape, lambda i: (0, 0))
    return pl.pallas_call(
        _outproj_kernel,
        grid=(T // tm,),
        in_specs=[pl.BlockSpec((tm, W), lambda i: (i, 0)), pl.BlockSpec((tm, W), lambda i: (i, 0)),
                  pl.BlockSpec((tm, D), lambda i: (i, 0)),
                  full(w_top), full(w_bot), full(g), full(wr), full(b_route)],
        out_specs=[pl.BlockSpec((tm, D), lambda i: (i, 0)), pl.BlockSpec((tm, 1, D // 2), lambda i: (i, 0, 0)),
                   pl.BlockSpec((tm, LANES), lambda i: (i, 0))],
        out_shape=[jax.ShapeDtypeStruct((T, D), F32), jax.ShapeDtypeStruct((T, 1, D // 2), U32),
                   jax.ShapeDtypeStruct((T, LANES), F32)],
        compiler_params=_cparams("arbitrary"),
        name="outproj_route",
    )(s5o, rwo, x2, w_top, w_bot, g, wr, b_route)


def _onehots(rec, lane):
    oh0 = (lane == rec[:, 0:1].astype(I32)).astype(F32)
    oh1 = (lane == rec[:, 1:2].astype(I32)).astype(F32)
    return oh0, oh1


def _rank_kernel(rec_ref, rank_ref, cnt_ref, base_ref):
    tb = rec_ref.shape[0]

    @pl.when(pl.program_id(0) == 0)
    def _():
        base_ref[...] = jnp.zeros_like(base_ref)

    lane = lax.broadcasted_iota(I32, (tb, LANES), 1)
    oh0, oh1 = _onehots(rec_ref[...], lane)
    both = oh0 + oh1
    ri = lax.broadcasted_iota(I32, (tb, tb), 0)
    ci = lax.broadcasted_iota(I32, (tb, tb), 1)
    before = _dot((ri > ci).astype(BF16), both.astype(BF16)) + base_ref[0:1, :]
    rank0 = jnp.sum(oh0 * before, axis=-1, keepdims=True)
    rank1 = jnp.sum(oh1 * before, axis=-1, keepdims=True)
    rank_ref[...] = jnp.where(lane == 0, rank0, jnp.where(lane == 1, rank1, 0.0))
    total = base_ref[0:1, :] + jnp.sum(both, axis=0, keepdims=True)
    base_ref[0:1, :] = total
    cnt_ref[...] = jnp.broadcast_to(total, cnt_ref.shape)


def _rank(rec, tb):
    T = rec.shape[0]
    return pl.pallas_call(
        _rank_kernel,
        grid=(T // tb,),
        in_specs=[pl.BlockSpec((tb, LANES), lambda i: (i, 0))],
        out_specs=[pl.BlockSpec((tb, LANES), lambda i: (i, 0)), pl.BlockSpec((8, LANES), lambda i: (0, 0))],
        out_shape=[jax.ShapeDtypeStruct((T, LANES), F32), jax.ShapeDtypeStruct((8, LANES), F32)],
        scratch_shapes=[pltpu.VMEM((8, LANES), F32)],
        compiler_params=_cparams("arbitrary"),
        name="moe_rank",
    )(rec)


def _padded_starts(cnt):
    padded = jnp.ceil(cnt * (1.0 / MOE_BLOCK)) * MOE_BLOCK
    ri = lax.broadcasted_iota(I32, (LANES, LANES), 0)
    ci = lax.broadcasted_iota(I32, (LANES, LANES), 1)
    p8 = jnp.broadcast_to(padded, (8, LANES))
    pend = _dot_exact_lhs_rhs(p8, (ri <= ci).astype(BF16))[0:1, :]
    return pend - padded, pend


def _dot_exact_lhs_rhs(a, b_bf16):
    ah, am, al = _split3(a)
    return _dot(ah, b_bf16) + (_dot(am, b_bf16) + _dot(al, b_bf16))


def _dest_kernel(rec_ref, rank_ref, cnt_ref, dest_ref, blk_ref):
    tb = rec_ref.shape[0]
    cnt = cnt_ref[0:1, :]
    pstart, pend = _padded_starts(cnt)
    lane = lax.broadcasted_iota(I32, (tb, LANES), 1)
    oh0, oh1 = _onehots(rec_ref[...], lane)
    rank = rank_ref[...]
    d0 = jnp.sum(oh0 * pstart, axis=-1, keepdims=True) + rank[:, 0:1]
    d1 = jnp.sum(oh1 * pstart, axis=-1, keepdims=True) + rank[:, 1:2]
    dest_ref[...] = jnp.where(lane == 0, d0, jnp.where(lane == 1, d1, 0.0)).astype(I32)

    @pl.when(pl.program_id(0) == 0)
    def _():
        nb = blk_ref.shape[0]
        blane = lax.broadcasted_iota(I32, (nb, LANES), 1)
        bstart = (lax.broadcasted_iota(I32, (nb, 1), 0) * MOE_BLOCK).astype(F32)
        is_e = blane < N_EXPERTS
        bexp = jnp.sum(jnp.where(is_e & (pend <= bstart), 1.0, 0.0), axis=-1, keepdims=True)
        bexp = jnp.minimum(bexp, N_EXPERTS - 1.0)
        nact = jnp.max(jnp.where(is_e, pend, 0.0), axis=-1, keepdims=True) * (1.0 / MOE_BLOCK)
        lastblk = jnp.where(is_e & (pend > pstart), pend - MOE_BLOCK, -1.0)
        brow = lax.broadcasted_iota(I32, (nb, 1), 0)
        out = jnp.where(blane == 0, bexp, jnp.where(blane == 1, nact, 0.0))
        out = jnp.where(brow == nb - 1, lastblk, out)
        blk_ref[...] = out.astype(I32)


def _dest(rec, rank, cnt, tb, nb_rows):
    T = rec.shape[0]
    return pl.pallas_call(
        _dest_kernel,
        grid=(T // tb,),
        in_specs=[pl.BlockSpec((tb, LANES), lambda i: (i, 0)), pl.BlockSpec((tb, LANES), lambda i: (i, 0)),
                  pl.BlockSpec((8, LANES), lambda i: (0, 0))],
        out_specs=[pl.BlockSpec((tb, LANES), lambda i: (i, 0)), pl.BlockSpec((nb_rows, LANES), lambda i: (0, 0))],
        out_shape=[jax.ShapeDtypeStruct((T, LANES), I32), jax.ShapeDtypeStruct((nb_rows, LANES), I32)],
        compiler_params=_cparams("arbitrary"),
        name="moe_dest",
    )(rec, rank, cnt)


def _scatter_kernel(pad_ref, na_ref, dest_ref, hn_ref, xs_ref, zbuf_ref, idx_ref, sa_ref, sb_ref,
                    sem_a, sem_b, isem, zsem, *, tb, n_blocks, n_tail, n_steps):
    i = pl.program_id(0)

    @pl.when(i == 0)
    def _():
        zbuf_ref[...] = jnp.zeros_like(zbuf_ref)

        def zero_rows(start):
            return pltpu.make_async_copy(zbuf_ref, xs_ref.at[pl.ds(start, MOE_BLOCK)], zsem)

        def pad_start(e, c):
            @pl.when(pad_ref[e] >= 0)
            def _():
                zero_rows(pad_ref[e]).start()
            return c

        def pad_wait(e, c):
            @pl.when(pad_ref[e] >= 0)
            def _():
                zero_rows(pad_ref[e]).wait()
            return c

        def tail_start(k, c):
            @pl.when(na_ref[0] + k < n_blocks)
            def _():
                zero_rows((na_ref[0] + k) * MOE_BLOCK).start()
            return c

        def tail_wait(k, c):
            @pl.when(na_ref[0] + k < n_blocks)
            def _():
                zero_rows((na_ref[0] + k) * MOE_BLOCK).wait()
            return c

        lax.fori_loop(0, N_EXPERTS, pad_start, 0)
        lax.fori_loop(0, n_tail, tail_start, 0)
        lax.fori_loop(0, N_EXPERTS, pad_wait, 0)
        lax.fori_loop(0, n_tail, tail_wait, 0)

    def idx_copy(step, slot):
        return pltpu.make_async_copy(dest_ref.at[step], idx_ref.at[pl.ds(slot * (2 * tb), 2 * tb)], isem)

    @pl.when(i == 0)
    def _():
        idx_copy(0, 0).start()

    slot = i % 2
    idx_copy(i, slot).wait()

    @pl.when(i + 1 < n_steps)
    def _():
        idx_copy(i + 1, 1 - slot).start()

    hb = tb // 2
    for half, (buf, sem) in enumerate(((sa_ref, sem_a), (sb_ref, sem_b))):
        def wait_rows(buf=buf, sem=sem):
            for _ in range(2):
                pltpu.make_async_copy(buf, xs_ref.at[pl.ds(0, hb)], sem).wait()

        @pl.when(i > 0)
        def _():
            wait_rows()

        buf[...] = hn_ref[half * hb:(half + 1) * hb]

        def issue(t, c, buf=buf, sem=sem, half=half):
            for s in range(2):
                d = idx_ref[slot * (2 * tb) + 2 * (half * hb + t) + s]
                pltpu.make_async_copy(buf.at[pl.ds(t, 1)], xs_ref.at[pl.ds(d, 1)], sem).start(priority=s)
            return c

        lax.fori_loop(0, hb, issue, 0, unroll=8)

    @pl.when(i == n_steps - 1)
    def _():
        for buf, sem in ((sa_ref, sem_a), (sb_ref, sem_b)):
            for _ in range(2):
                pltpu.make_async_copy(buf, xs_ref.at[pl.ds(0, hb)], sem).wait()


def _scatter(lastblk, nact, dest2, hn, tb, n_blocks):
    T, _, D = hn.shape
    n_tail = n_blocks - (-(-2 * T // MOE_BLOCK))
    hb = tb // 2
    return pl.pallas_call(
        functools.partial(_scatter_kernel, tb=tb, n_blocks=n_blocks, n_tail=n_tail, n_steps=T // tb),
        grid_spec=pltpu.PrefetchScalarGridSpec(
            num_scalar_prefetch=2, grid=(T // tb,),
            in_specs=[pl.BlockSpec(memory_space=pl.ANY),
                      pl.BlockSpec((tb, 1, D), lambda i, pad, na: (i, 0, 0))],
            out_specs=pl.BlockSpec(memory_space=pl.ANY),
            scratch_shapes=[pltpu.VMEM((MOE_BLOCK, 1, D), hn.dtype), pltpu.SMEM((4 * tb,), I32),
                            pltpu.VMEM((hb, 1, D), hn.dtype), pltpu.VMEM((hb, 1, D), hn.dtype),
                            pltpu.SemaphoreType.DMA, pltpu.SemaphoreType.DMA,
                            pltpu.SemaphoreType.DMA, pltpu.SemaphoreType.DMA]),
        out_shape=jax.ShapeDtypeStruct((n_blocks * MOE_BLOCK, 1, D), hn.dtype),
        compiler_params=_cparams("arbitrary"),
        name="moe_scatter",
    )(lastblk, nact, dest2, hn)


W_SLOTS = 2


def _expert_kernel(be_ref, na_ref, x_ref, wg_hbm, wu_hbm, wd_hbm, y_ref,
                   wg_buf, wu_buf, wd_buf, wgb_ref, wub_ref, wdb_ref, x2_ref, elist_ref, state_ref, sems):
    j = pl.program_id(0)
    na = na_ref[0]

    def weight_copies(e, slot):
        return [pltpu.make_async_copy(src.at[e], buf.at[slot], sems.at[slot, n])
                for n, (src, buf) in enumerate(((wg_hbm, wg_buf), (wu_hbm, wu_buf), (wd_hbm, wd_buf)))]

    @pl.when(j == 0)
    def _():
        def scan(k, n):
            is_new = (k == 0) | (be_ref[k] != be_ref[jnp.maximum(k - 1, 0)])

            @pl.when(is_new)
            def _():
                elist_ref[n] = be_ref[k]
            return n + is_new.astype(I32)

        n_exp = lax.fori_loop(0, na, scan, 0)
        state_ref[0] = 0
        state_ref[1] = n_exp
        for n in range(W_SLOTS):
            @pl.when(n < n_exp)
            def _():
                for cp in weight_copies(elist_ref[n], n):
                    cp.start()

    @pl.when(j < na)
    def _():
        @pl.when((j == 0) | (be_ref[j] != be_ref[jnp.maximum(j - 1, 0)]))
        def _():
            n = state_ref[0]
            slot = n % W_SLOTS
            for cp in weight_copies(elist_ref[n], slot):
                cp.wait()
            wgb_ref[...] = wg_buf[slot].astype(BF16)
            wub_ref[...] = wu_buf[slot].astype(BF16)
            wdb_ref[...] = wd_buf[slot].astype(BF16)

            @pl.when(n + W_SLOTS < state_ref[1])
            def _():
                for cp in weight_copies(elist_ref[n + W_SLOTS], slot):
                    cp.start()

            state_ref[0] = n + 1

        x2_ref[...] = x_ref[...].reshape(x2_ref.shape)
        lo, hi = _unpack_halves(x2_ref[...])
        lo, hi = lo.astype(BF16), hi.astype(BF16)
        half = lo.shape[1]
        hg = _dot(lo, wgb_ref[:half, :]) + _dot(hi, wgb_ref[half:, :])
        hu = _dot(lo, wub_ref[:half, :]) + _dot(hi, wub_ref[half:, :])
        act = (hg * jax.nn.sigmoid(hg)) * hu
        y = _dot(act.astype(BF16), wdb_ref[...])
        y_ref[...] = _pack_halves(y).reshape(y_ref.shape)

    @pl.when(j >= na)
    def _():
        y_ref[...] = jnp.zeros_like(y_ref)


def _experts(blk_exp, nact, xs, w_gate, w_up, w_down, n_blocks):
    DP = xs.shape[2]
    D = 2 * DP
    rows = n_blocks * MOE_BLOCK
    DE = w_gate.shape[2]
    blk = lambda j, be, na: (jnp.minimum(j, na[0] - 1), 0, 0)
    hbm = pl.BlockSpec(memory_space=pl.ANY)
    return pl.pallas_call(
        _expert_kernel,
        grid_spec=pltpu.PrefetchScalarGridSpec(
            num_scalar_prefetch=2, grid=(n_blocks,),
            in_specs=[pl.BlockSpec((MOE_BLOCK, 1, DP), blk), hbm, hbm, hbm],
            out_specs=pl.BlockSpec((MOE_BLOCK, 1, DP), lambda j, be, na: (j, 0, 0)),
            scratch_shapes=[pltpu.VMEM((W_SLOTS, D, DE), F32), pltpu.VMEM((W_SLOTS, D, DE), F32),
                            pltpu.VMEM((W_SLOTS, DE, D), F32),
                            pltpu.VMEM((D, DE), BF16), pltpu.VMEM((D, DE), BF16), pltpu.VMEM((DE, D), BF16),
                            pltpu.VMEM((MOE_BLOCK, DP), U32), pltpu.SMEM((N_EXPERTS,), I32), pltpu.SMEM((2,), I32),
                            pltpu.SemaphoreType.DMA((W_SLOTS, 3))]),
        out_shape=jax.ShapeDtypeStruct((rows, 1, DP), U32),
        compiler_params=_cparams("arbitrary"),
        name="moe_experts",
    )(blk_exp, nact, xs, w_gate, w_up, w_down)


def _combine_kernel(dest_ref, ys_ref, h_ref, rec_ref, g_ref, o_ref, idx_ref, ya0_ref, ya1_ref, yb0_ref, yb1_ref,
                    y2_ref, sem_a, sem_b, isem, *, normalize, n_steps):
    i = pl.program_id(0)
    tb = h_ref.shape[0]
    hb = tb // 2
    halves = ((ya0_ref, ya1_ref, sem_a), (yb0_ref, yb1_ref, sem_b))

    def idx_copy(step, slot):
        return pltpu.make_async_copy(dest_ref.at[step], idx_ref.at[pl.ds(slot * (2 * tb), 2 * tb)], isem)

    def issue(half, slot):
        bufs, sem = halves[half][:2], halves[half][2]

        def body(t, c):
            for s in range(2):
                d = idx_ref[slot * (2 * tb) + 2 * (half * hb + t) + s]
                pltpu.make_async_copy(ys_ref.at[pl.ds(d, 1)], bufs[s].at[pl.ds(t, 1)], sem).start(priority=s)
            return c

        lax.fori_loop(0, hb, body, 0, unroll=8)

    @pl.when(i == 0)
    def _():
        first = idx_copy(0, 0)
        first.start()
        first.wait()
        issue(0, 0)
        issue(1, 0)
        if n_steps > 1:
            idx_copy(1, 1).start()

    nslot = (i + 1) % 2

    @pl.when(i + 1 < n_steps)
    def _():
        idx_copy(i + 1, nslot).wait()

    rec = rec_ref[...]
    for half, (y0_ref, y1_ref, sem) in enumerate(halves):
        rows = slice(half * hb, (half + 1) * hb)
        pltpu.make_async_copy(ys_ref.at[pl.ds(0, hb)], y0_ref, sem).wait()
        pltpu.make_async_copy(ys_ref.at[pl.ds(0, hb)], y1_ref, sem).wait()
        y2_ref[...] = y0_ref[...].reshape(y2_ref.shape)
        h = h_ref[rows, :] + rec[rows, 2:3] * jnp.concatenate(_unpack_halves(y2_ref[...]), axis=1)
        y2_ref[...] = y1_ref[...].reshape(y2_ref.shape)
        h = h + rec[rows, 3:4] * jnp.concatenate(_unpack_halves(y2_ref[...]), axis=1)
        if normalize:
            ms = jnp.mean(h * h, axis=-1, keepdims=True)
            h = h * lax.rsqrt(ms + RMS_EPS) * g_ref[...]
        o_ref[rows, :] = h

        @pl.when(i + 1 < n_steps)
        def _():
            issue(half, nslot)

    @pl.when(i + 2 < n_steps)
    def _():
        idx_copy(i + 2, i % 2).start()


def _combine(dest2, ys, h, rec, g, tb, normalize):
    T, D = h.shape
    hb = tb // 2
    return pl.pallas_call(
        functools.partial(_combine_kernel, normalize=normalize, n_steps=T // tb),
        grid=(T // tb,),
        in_specs=[pl.BlockSpec(memory_space=pl.ANY), pl.BlockSpec(memory_space=pl.ANY),
                  pl.BlockSpec((tb, D), lambda i: (i, 0)), pl.BlockSpec((tb, LANES), lambda i: (i, 0)),
                  pl.BlockSpec((1, D), lambda i: (0, 0))],
        out_specs=pl.BlockSpec((tb, D), lambda i: (i, 0)),
        out_shape=jax.ShapeDtypeStruct((T, D), F32),
        scratch_shapes=[pltpu.SMEM((4 * tb,), I32)] + [pltpu.VMEM((hb, 1, D // 2), U32)] * 4 +
                       [pltpu.VMEM((hb, D // 2), U32)] + [pltpu.SemaphoreType.DMA] * 3,
        compiler_params=_cparams("arbitrary"),
        name="moe_combine",
    )(dest2, ys, h, rec, g)


def _pad_cols(a, n):
    return jnp.pad(a, ((0, 0), (0, n - a.shape[1])))


def _pad_rows(a, n, at=0):
    return jnp.pad(a, ((at, n - a.shape[0] - at), (0, 0)))


def _layer(x2, B, L, p):
    T, D = x2.shape
    W = p["s5_d"].shape[0]
    G, P = p["s5_lambda_re"].shape
    HG = W // G
    dl, al, gl = p["rwkv_w2"].shape[0], p["rwkv_a2"].shape[0], p["rwkv_g2"].shape[0]
    H = W // RW_N

    w_in = p["w_in"]
    o = W
    cols = [w_in[:, :W], w_in[:, o:o + 3 * W]]
    o += 3 * W
    cols += [_pad_cols(w_in[:, o:o + dl], 128), _pad_cols(w_in[:, o + dl:o + dl + al], 128),
             _pad_cols(w_in[:, o + dl + al:o + dl + al + gl], 256)]
    w_in_r = jnp.concatenate(cols, axis=1).astype(BF16)
    tm_in = min(1024, T)
    proj = _inproj(x2, p["norm_mix_g"].reshape(1, D), w_in_r, tm_in, 1152)

    C = S5_CHUNK
    nc = L // C
    ct_re = jnp.swapaxes(p["s5_c_re"], 1, 2)
    ct_im = jnp.swapaxes(p["s5_c_im"], 1, 2)
    rb = min(LANES, B * nc)
    z = _s5_in(proj.reshape(B * nc, C, proj.shape[1]), G, rb)
    y_t = _s5_chunk(z, p["s5_lambda_re"], p["s5_lambda_im"], p["s5_log_step"],
                    ct_re, ct_im, p["s5_b_re"], p["s5_b_im"], p["s5_c_re"], p["s5_c_im"], nc)
    y_ssm = _s5_out(y_t, rb).reshape(T, W)
    tm = min(512, T)
    s5_out = _s5_glu(y_ssm, proj, p["s5_d"].reshape(1, W), p["s5_w_glu"].astype(BF16),
                     p["s5_b_glu"].reshape(1, W), tm)

    mu = p["rwkv_mu"]
    mu3 = _pad_rows(mu[:3 * W].reshape(3, W), 8)
    o = 3 * W
    mul = jnp.concatenate([_pad_cols(mu[None, o:o + dl], 128), _pad_cols(mu[None, o + dl:o + dl + al], 128),
                           _pad_cols(mu[None, o + dl + al:], 256)], axis=1)
    w2p = _pad_rows(p["rwkv_w2"], 128).astype(BF16)
    a2p = _pad_rows(p["rwkv_a2"], 128).astype(BF16)
    g2p = _pad_rows(p["rwkv_g2"], 256).astype(BF16)
    row = lambda a: a.reshape(1, W)
    rw_out = _rwkv_chunk(proj, B, L, mu3, mul, row(p["rwkv_w0"]), row(p["rwkv_a0"]), row(p["rwkv_k_k"]),
                         row(p["rwkv_k_a"]), w2p, a2p, g2p, row(p["rwkv_r_k"]), row(p["rwkv_ln_w"]),
                         row(p["rwkv_ln_b"]))

    w_out = p["w_out"].astype(BF16)
    w_route = jnp.concatenate([p["w_route_exp"], p["w_route_grp"]], axis=1)
    w_route = _pad_cols(w_route, LANES)
    wr_hi = w_route.astype(BF16)
    wr_lo = (w_route - wr_hi.astype(F32)).astype(BF16)
    wr = jnp.concatenate([wr_hi, wr_lo], axis=1)
    b_route = _pad_cols(jnp.concatenate([p["b_route_exp"], p["b_route_grp"]])[None, :], LANES)
    tm_o = min(512, T)
    h, hn, rec = _outproj(s5_out, rw_out, x2, w_out[:W], w_out[W:], p["norm_ffn_g"].reshape(1, D),
                          wr, b_route, tm_o)

    tb = min(1024, T)
    rank, cnt = _rank(rec, tb)
    n_blocks = -(-2 * T // MOE_BLOCK) + N_EXPERTS
    nb_rows = -(-(n_blocks + 1) // 8) * 8
    dest, blk = _dest(rec, rank, cnt, tb, nb_rows)
    blk_exp = blk[:n_blocks, 0]
    nact = blk[0:1, 1]
    lastblk = blk[nb_rows - 1, :N_EXPERTS]
    ts = min(256, T)
    dest2 = dest[:, :2].reshape(T // ts, 2 * ts)

    xs = _scatter(lastblk, nact, dest2, hn, ts, n_blocks)
    ys = _experts(blk_exp, nact, xs, p["w_gate"], p["w_up"], p["w_down"], n_blocks)
    return dest2, ys, h, rec, ts


def kernel(x, norm_mix_g, w_in, s5_lambda_re, s5_lambda_im, s5_log_step, s5_b_re, s5_b_im, s5_c_re, s5_c_im, s5_d, s5_w_glu, s5_b_glu, rwkv_mu, rwkv_w0, rwkv_w2, rwkv_a0, rwkv_a2, rwkv_g2, rwkv_k_k, rwkv_k_a, rwkv_r_k, rwkv_ln_w, rwkv_ln_b, w_out, norm_ffn_g, w_route_grp, b_route_grp, w_route_exp, b_route_exp, w_gate, w_up, w_down, norm_final_g):
    B, L, D = x.shape
    params = dict(
        norm_mix_g=norm_mix_g, w_in=w_in, s5_lambda_re=s5_lambda_re, s5_lambda_im=s5_lambda_im,
        s5_log_step=s5_log_step, s5_b_re=s5_b_re, s5_b_im=s5_b_im, s5_c_re=s5_c_re, s5_c_im=s5_c_im,
        s5_d=s5_d, s5_w_glu=s5_w_glu, s5_b_glu=s5_b_glu, rwkv_mu=rwkv_mu, rwkv_w0=rwkv_w0, rwkv_w2=rwkv_w2,
        rwkv_a0=rwkv_a0, rwkv_a2=rwkv_a2, rwkv_g2=rwkv_g2, rwkv_k_k=rwkv_k_k, rwkv_k_a=rwkv_k_a,
        rwkv_r_k=rwkv_r_k, rwkv_ln_w=rwkv_ln_w, rwkv_ln_b=rwkv_ln_b, w_out=w_out, norm_ffn_g=norm_ffn_g,
        w_route_grp=w_route_grp, b_route_grp=b_route_grp, w_route_exp=w_route_exp, b_route_exp=b_route_exp,
        w_gate=w_gate, w_up=w_up, w_down=w_down)
    depth = norm_mix_g.shape[0]
    h2 = x.reshape(B * L, D)
    for l in range(depth):
        p = {k_: v_[l] for k_, v_ in params.items()}
        dest2, ys, h, rec, ts = _layer(h2, B, L, p)
        h2 = _combine(dest2, ys, h, rec, norm_final_g.reshape(1, D), ts, normalize=(l == depth - 1))
    return h2.reshape(B, L, D)
```

```python
import functools

import jax
import jax.numpy as jnp
from jax import lax
from jax.experimental import pallas as pl
from jax.experimental.pallas import tpu as pltpu

F32 = jnp.float32
BF16 = jnp.bfloat16
I32 = jnp.int32

RMS_EPS = 1e-6
S5_CHUNK = 64
S5_HG = 16
S5_P = 64
RW_N = 64
RW_CHUNK = 64
RW_GN_EPS = 64e-5
N_GROUPS = 8
EPG = 8
N_EXPERTS = 64
MOE_BLOCK = 256
LANES = 128
VMEM_LIMIT = 56 * 1024 * 1024


def _cparams(*sem, **kw):
    return pltpu.CompilerParams(dimension_semantics=sem, vmem_limit_bytes=VMEM_LIMIT, **kw)


def _split2(x):
    hi = x.astype(BF16)
    lo = (x - hi.astype(F32)).astype(BF16)
    return hi, lo


def _split3(x):
    hi = x.astype(BF16)
    r = x - hi.astype(F32)
    mid = r.astype(BF16)
    lo = (r - mid.astype(F32)).astype(BF16)
    return hi, mid, lo


def _dot(a, b):
    return jnp.dot(a, b, preferred_element_type=F32)


def _dot_nt(a, b):
    return lax.dot_general(a, b, (((1,), (1,)), ((), ())), preferred_element_type=F32)


def _dot_tn(a, b):
    return lax.dot_general(a, b, (((0,), (0,)), ((), ())), preferred_element_type=F32)


def _dot_x3(a, b):
    ah, al = _split2(a)
    bh, bl = _split2(b)
    return _dot(ah, bh) + (_dot(ah, bl) + _dot(al, bh))


def _dot_exact_lhs(a_bf16, b):
    bh, bm, bl = _split3(b)
    return _dot(a_bf16, bh) + (_dot(a_bf16, bm) + _dot(a_bf16, bl))


def _cmul(ar, ai, br, bi):
    return ar * br - ai * bi, ar * bi + ai * br


U32 = jnp.uint32


def _pack_halves(x):
    half = x.shape[1] // 2
    lo = lax.bitcast_convert_type(x[:, :half].astype(BF16).astype(F32), U32)
    hi = lax.bitcast_convert_type(x[:, half:].astype(BF16).astype(F32), U32)
    return (lo >> 16) | hi


def _unpack_halves(p):
    lo = lax.bitcast_convert_type(p << 16, F32)
    hi = lax.bitcast_convert_type(p & jnp.uint32(0xFFFF0000), F32)
    return lo, hi


def _inproj_kernel(x_ref, g_ref, w_ref, o_ref, hn_ref):
    @pl.when(pl.program_id(1) == 0)
    def _():
        x = x_ref[...]
        ms = jnp.mean(x * x, axis=-1, keepdims=True)
        hn_ref[...] = (x * lax.rsqrt(ms + RMS_EPS) * g_ref[...]).astype(BF16)

    o_ref[...] = _dot(hn_ref[...], w_ref[...])


def _inproj(x2, g, w_bf16, tm, tn):
    T, D = x2.shape
    N = w_bf16.shape[1]
    return pl.pallas_call(
        _inproj_kernel,
        grid=(T // tm, N // tn),
        in_specs=[
            pl.BlockSpec((tm, D), lambda i, j: (i, 0)),
            pl.BlockSpec((1, D), lambda i, j: (0, 0)),
            pl.BlockSpec((D, tn), lambda i, j: (0, j)),
        ],
        out_specs=pl.BlockSpec((tm, tn), lambda i, j: (i, j)),
        out_shape=jax.ShapeDtypeStruct((T, N), F32),
        scratch_shapes=[pltpu.VMEM((tm, D), BF16)],
        compiler_params=_cparams("arbitrary", "arbitrary"),
        name="inproj",
    )(x2, g, w_bf16)


def _binpow(ar, ai, expo, nbits):
    pr = jnp.ones(expo.shape, F32)
    pi = jnp.zeros(expo.shape, F32)
    sr, si = ar, ai
    for bit in range(nbits):
        m = ((expo >> bit) & 1) == 1
        nr, ni = _cmul(pr, pi, sr, si)
        pr = jnp.where(m, nr, pr)
        pi = jnp.where(m, ni, pi)
        if bit + 1 < nbits:
            sr, si = _cmul(sr, si, sr, si)
    return pr, pi


def _frame_powers(ar, ai, reverse, plus_one):
    C, HG, P = S5_CHUNK, S5_HG, ar.shape[0]
    fpt = LANES // HG
    j = lax.broadcasted_iota(I32, (P, LANES), 1) // HG
    inner_r, inner_i = _binpow(ar, ai, (fpt - 1 - j) if reverse else j, (fpt - 1).bit_length())
    sr, si = ar, ai
    for _ in range(fpt.bit_length() - 1):
        sr, si = _cmul(sr, si, sr, si)
    outer = [(ar, ai) if plus_one else (jnp.ones_like(ar), jnp.zeros_like(ai))]
    for _ in range(C // fpt - 1):
        outer.append(_cmul(outer[-1][0], outer[-1][1], sr, si))
    if reverse:
        outer = outer[::-1]
    tiles = [_cmul(inner_r, inner_i, o_r, o_i) for o_r, o_i in outer]
    return (jnp.concatenate([t[0] for t in tiles], axis=1), jnp.concatenate([t[1] for t in tiles], axis=1))


def _zoh(lr, li, dt):
    mag = jnp.exp(lr * dt)
    ang = li * dt
    ar, ai = mag * jnp.cos(ang), mag * jnp.sin(ang)
    den = lr * lr + li * li
    nr, ni = ar - 1.0, ai
    fr = (nr * lr + ni * li) / den
    fi = (ni * lr - nr * li) / den
    return ar, ai, fr, fi


def _s5_ops_group(lrc_ref, lic_ref, dt_ref, ct_re_ref, ct_im_ref, bt_re_ref, bt_im_ref, c_re_ref, c_im_ref,
                  m_ref, wbt_ref, wct_ref, a64_ref):
    C, HG, P = S5_CHUNK, S5_HG, S5_P
    W = C * HG
    dt = jnp.exp(dt_ref[...])
    ar_c, ai_c, fr_c, fi_c = _zoh(lrc_ref[...], lic_ref[...], dt)

    pr, pi = _frame_powers(ar_c, ai_c, False, True)
    qr, qi = _frame_powers(ar_c, ai_c, True, False)

    sel = (lax.broadcasted_iota(I32, (HG, W), 1) % HG == lax.broadcasted_iota(I32, (HG, W), 0)).astype(BF16)
    tile = lambda a: _dot_exact_lhs_rhs(a, sel)

    c1_re, c1_im = _cmul(tile(ct_re_ref[...]), tile(ct_im_ref[...]), pr, pi)
    wct_ref[...] = jnp.concatenate([c1_re, -c1_im], axis=0).T.astype(BF16)

    bb_re, bb_im = _cmul(bt_re_ref[...], bt_im_ref[...], fr_c, fi_c)
    ab_re, ab_im = _cmul(tile(bb_re), tile(bb_im), qr, qi)
    wbt_ref[:P, :] = ab_re.astype(BF16)
    wbt_ref[P:, :] = ab_im.astype(BF16)

    cr, ci = ar_c, ai_c
    for _ in range(C.bit_length() - 1):
        cr, ci = _cmul(cr, ci, cr, ci)
    a64_ref[...] = jnp.concatenate([cr, ci], axis=0)

    strip = _dot_x3(c_re_ref[...], ab_re) - _dot_x3(c_im_ref[...], ab_im)
    ext = jnp.concatenate([strip, jnp.zeros((HG, W), F32)], axis=1)
    for t in range(C):
        off = (C - 1 - t) * HG
        m_ref[t * HG:(t + 1) * HG, :] = ext[:, off:off + W].astype(BF16)


S5_FB = 8


def _s5_in_kernel(x_ref, z_ref):
    G, _, rb = z_ref.shape
    for f in range(S5_FB):
        xt = x_ref[:, f, :].T
        z_ref[:, f * S5_HG:(f + 1) * S5_HG, :] = xt.reshape(G, S5_HG, rb).astype(BF16)


def _s5_in(proj3, G, rb):
    R, C, _ = proj3.shape
    W = G * S5_HG
    return pl.pallas_call(
        _s5_in_kernel,
        grid=(R // rb, C // S5_FB),
        in_specs=[pl.BlockSpec((rb, S5_FB, W), lambda i, j: (i, j, 0))],
        out_specs=pl.BlockSpec((G, S5_FB * S5_HG, rb), lambda i, j: (0, j, i)),
        out_shape=jax.ShapeDtypeStruct((G, C * S5_HG, R), BF16),
        compiler_params=_cparams("arbitrary", "arbitrary"),
        name="s5_in",
    )(proj3)


def _s5_out_kernel(yt_ref, o_ref):
    G, _, rb = yt_ref.shape
    for f in range(S5_FB):
        slab = yt_ref[:, f * S5_HG:(f + 1) * S5_HG, :].reshape(G * S5_HG, rb)
        o_ref[:, f, :] = slab.T


def _s5_out(yt, rb):
    G, CW, R = yt.shape
    C = CW // S5_HG
    W = G * S5_HG
    return pl.pallas_call(
        _s5_out_kernel,
        grid=(R // rb, C // S5_FB),
        in_specs=[pl.BlockSpec((G, S5_FB * S5_HG, rb), lambda i, j: (0, j, i))],
        out_specs=pl.BlockSpec((rb, S5_FB, W), lambda i, j: (i, j, 0)),
        out_shape=jax.ShapeDtypeStruct((R, C, W), F32),
        compiler_params=_cparams("arbitrary", "arbitrary"),
        name="s5_out",
    )(yt)


def _s5_chunk_kernel(*refs, n_chunks):
    params, (z_ref, y_ref), ops = refs[:9], refs[9:11], refs[11:]
    for g in range(z_ref.shape[0]):
        at = lambda rs: [r.at[g] for r in rs]
        _s5_ops_group(*at(params), *at(ops))
        _s5_chunk_group(z_ref.at[g], *at(ops), y_ref.at[g], n_chunks)


def _s5_chunk_group(z_ref, m_ref, wbt_ref, wct_ref, a64_ref, y_ref, n_chunks):
    P = S5_P
    z = z_ref[...]
    R = z.shape[1]
    x = _dot(wbt_ref[...], z)
    y_ref[...] = _dot(m_ref[...], z)
    xr, xi = x[:P], x[P:]
    a = a64_ref[...]
    ar, ai = a[:P], a[P:]
    cidx = lax.broadcasted_iota(I32, (1, R), 1) % n_chunks
    shift = 1
    while shift < n_chunks:
        keep = cidx >= shift
        sr = jnp.where(keep, pltpu.roll(xr, shift, 1), 0.0)
        si = jnp.where(keep, pltpu.roll(xi, shift, 1), 0.0)
        xr, xi = xr + (ar * sr - ai * si), xi + (ar * si + ai * sr)
        ar, ai = ar * ar - ai * ai, 2.0 * ar * ai
        shift *= 2
    keep = cidx >= 1
    s_in = jnp.concatenate([jnp.where(keep, pltpu.roll(xr, 1, 1), 0.0),
                            jnp.where(keep, pltpu.roll(xi, 1, 1), 0.0)], axis=0)
    sh, sl = _split2(s_in)
    wct = wct_ref[...]
    y_ref[...] = y_ref[...] + (_dot(wct, sh) + _dot(wct, sl))


def _s5_chunk(z, lam_re, lam_im, log_step, ct_re, ct_im, bt_re, bt_im, c_re, c_im, n_chunks):
    G, W, R = z.shape
    P, HG = lam_re.shape[1], S5_HG
    P2 = 2 * P
    gps = 2 if G % 2 == 0 else 1
    col = lambda a: a.reshape(G, P, 1)
    g3 = lambda s1, s2: pl.BlockSpec((gps, s1, s2), lambda g: (g, 0, 0))
    return pl.pallas_call(
        functools.partial(_s5_chunk_kernel, n_chunks=n_chunks),
        grid=(G // gps,),
        in_specs=[g3(P, 1), g3(P, 1), g3(1, 1),
                  g3(P, HG), g3(P, HG), g3(P, HG), g3(P, HG), g3(HG, P), g3(HG, P), g3(W, R)],
        out_specs=g3(W, R),
        out_shape=jax.ShapeDtypeStruct((G, W, R), F32),
        scratch_shapes=[pltpu.VMEM((gps, W, W), BF16), pltpu.VMEM((gps, P2, W), BF16),
                        pltpu.VMEM((gps, W, P2), BF16), pltpu.VMEM((gps, P2, 1), F32)],
        compiler_params=_cparams("arbitrary"),
        name="s5_chunk",
    )(col(lam_re), col(lam_im), log_step.reshape(G, 1, 1), ct_re, ct_im, bt_re, bt_im, c_re, c_im, z)


def _gelu_tanh(x):
    return 0.5 * x * (1.0 + jnp.tanh(0.7978845608028654 * (x + 0.044715 * (x * x * x))))


def _s5_glu_kernel(y_ref, u_ref, d_ref, w_ref, b_ref, o_ref):
    y = y_ref[...] + d_ref[...] * u_ref[...]
    y = _gelu_tanh(y)
    z = _dot(y.astype(BF16), w_ref[...]) + b_ref[...]
    o_ref[...] = (y * jax.nn.sigmoid(z)).astype(BF16)


def _s5_glu(y_ssm, proj, d, w_bf16, b, tm):
    T, W = y_ssm.shape
    return pl.pallas_call(
        _s5_glu_kernel,
        grid=(T // tm,),
        in_specs=[
            pl.BlockSpec((tm, W), lambda i: (i, 0)),
            pl.BlockSpec((tm, W), lambda i: (i, 0)),
            pl.BlockSpec((1, W), lambda i: (0, 0)),
            pl.BlockSpec((W, W), lambda i: (0, 0)),
            pl.BlockSpec((1, W), lambda i: (0, 0)),
        ],
        out_specs=pl.BlockSpec((tm, W), lambda i: (i, 0)),
        out_shape=jax.ShapeDtypeStruct((T, W), BF16),
        compiler_params=_cparams("arbitrary"),
        name="s5_glu",
    )(y_ssm, proj, d, w_bf16, b)


def _shift(z, prev_row):
    rolled = pltpu.roll(z, 1, 0)
    first = lax.broadcasted_iota(I32, (z.shape[0], 1), 0) == 0
    return jnp.where(first, prev_row, rolled)


RW_OPERANDS = ("r_t", "k_t", "a_t", "b_t", "k_h", "b_h", "v")
RW_PREP_PIECES = 14
RW_STAGES = 24


def _rwkv_chunk_kernel(zr_ref, zk_ref, zv_ref, zl_ref, mu_ref, mul_ref, w0_ref, a0_ref, kk_ref, ka_ref,
                       w2_ref, a2_ref, g2_ref, rk_ref, lnw_ref, lnb_ref,
                       o_ref, z_ref, ys_ref, qs_ref, vs_ref, gs_ref, car_ref, carl_ref, opnd_ref, ptot_ref):
    @pl.when(pl.program_id(1) == 0)
    def _():
        for ref in (z_ref, ys_ref, qs_ref, vs_ref, gs_ref, car_ref, carl_ref, opnd_ref, ptot_ref):
            ref[...] = jnp.zeros_like(ref)

    C = RW_CHUNK
    refs = (zr_ref, zk_ref, zv_ref, zl_ref, mu_ref, mul_ref, w0_ref, a0_ref, kk_ref, ka_ref,
            w2_ref, a2_ref, g2_ref, rk_ref, lnw_ref, lnb_ref,
            o_ref, z_ref, ys_ref, qs_ref, vs_ref, gs_ref, car_ref, carl_ref, opnd_ref, ptot_ref)
    for half in range(2):
        _rwkv_half_step(half, slice(half * C, (half + 1) * C), *refs)


def _rwkv_half_step(par, rows, zr_ref, zk_ref, zv_ref, zl_ref, mu_ref, mul_ref, w0_ref, a0_ref, kk_ref, ka_ref,
                    w2_ref, a2_ref, g2_ref, rk_ref, lnw_ref, lnb_ref,
                    o_ref, z_ref, ys_ref, qs_ref, vs_ref, gs_ref, car_ref, carl_ref, opnd_ref, ptot_ref):
    C, N = RW_CHUNK, RW_N
    NS = zr_ref.shape[0]
    H = zr_ref.shape[2] // N

    low = lax.broadcasted_iota(I32, (C, 2 * N), 1) < N
    inv_n = 1.0 / N

    def head_sums(t):
        s0 = jnp.sum(jnp.where(low, t, 0.0), axis=-1, keepdims=True)
        s1 = jnp.sum(jnp.where(low, 0.0, t), axis=-1, keepdims=True)
        return jnp.where(low, s0, s1)

    def tail():
        for s in range(NS):
            for p in range(H // 2):
                ps = slice(2 * p * N, 2 * (p + 1) * N)
                yp = ys_ref[s, :, ps]
                yc = yp - head_sums(yp) * inv_n
                var = head_sums(yc * yc) * inv_n
                yn = yc * lax.rsqrt(var + RW_GN_EPS) * lnw_ref[:, ps] + lnb_ref[:, ps]
                bonus = head_sums(qs_ref[par, s, :, ps]) * vs_ref[par, s, :, ps].astype(F32)
                o_ref[s, rows, ps] = ((yn + bonus) * gs_ref[par, s, :, ps].astype(F32)).astype(BF16)
                yield

    ri = lax.broadcasted_iota(I32, (C, C), 0)
    ci = lax.broadcasted_iota(I32, (C, C), 1)
    tril = ri >= ci
    stril = ri > ci
    eye = (ri == ci).astype(F32)
    both = jnp.concatenate([stril, tril], axis=0)
    prev = 1 - par
    ops = {}
    for s in range(NS):
        for h in range(H):
            cols = slice(h * N, (h + 1) * N)
            ops[(s, h)] = {name: functools.partial(lambda n, s, cols: opnd_ref[prev, n, s, :, cols], n, s, cols)
                           for n, name in enumerate(RW_OPERANDS)}
            ops[(s, h)]["p_tot"] = functools.partial(lambda s, cols: ptot_ref[prev, s, 0:1, cols], s, cols)
    def head_sums_all(t):
        return jnp.concatenate([head_sums(t[:, 2 * p * N:2 * (p + 1) * N]) for p in range(H // 2)], axis=1)

    def lerp(zz, prev_row, mu):
        return zz + (_shift(zz, prev_row) - zz) * mu

    def prepare():
        for s in range(NS):
            zr, zk, zv, zl = zr_ref[s, rows, :], zk_ref[s, rows, :], zv_ref[s, rows, :], zl_ref[s, rows, :]
            r_f = lerp(zr, car_ref[s, 0:1, :], mu_ref[0:1, :])
            yield
            k_raw = lerp(zk, car_ref[s, 1:2, :], mu_ref[1:2, :])
            yield
            v_f = lerp(zv, car_ref[s, 2:3, :], mu_ref[2:3, :])
            xl = lerp(zl, carl_ref[s, 0:1, :], mul_ref[...])
            car_ref[s, 0:1, :] = zr[C - 1:C, :]
            car_ref[s, 1:2, :] = zk[C - 1:C, :]
            car_ref[s, 2:3, :] = zv[C - 1:C, :]
            carl_ref[s, 0:1, :] = zl[C - 1:C, :]
            yield
            xw, xa, xg = xl[:, 0:128], xl[:, 128:256], xl[:, 256:512]
            dw = _dot(jnp.tanh(xw).astype(BF16), w2_ref[...])
            da = _dot(xa.astype(BF16), a2_ref[...])
            g_f = _dot(jax.nn.sigmoid(xg).astype(BF16), g2_ref[...])
            yield
            zw = -(w0_ref[...] + dw)
            w_log = -(jnp.maximum(zw, 0.0) + jnp.log(1.0 + jnp.exp(-jnp.abs(zw)))) - 0.5
            yield
            a_sig = jax.nn.sigmoid(a0_ref[...] + da)
            kk = k_raw * kk_ref[...]
            yield
            kk = kk / jnp.maximum(jnp.sqrt(head_sums_all(kk * kk)), 1e-12)
            yield
            k_f = k_raw * (1.0 + (a_sig - 1.0) * ka_ref[...])
            b_f = kk * a_sig
            lw = -jnp.exp(w_log)
            yield
            cs = _dot_exact_lhs(tril.astype(BF16), lw)
            tot = cs[C - 1:C, :]
            yield
            store = lambda name, a: opnd_ref.__setitem__((par, RW_OPERANDS.index(name), s), a.astype(BF16))
            p_inc = jnp.exp(cs)
            store("r_t", r_f * p_inc)
            yield
            p_inv = jnp.exp(-cs)
            store("k_t", k_f * p_inv)
            store("b_t", b_f * p_inv)
            yield
            store("a_t", -kk * jnp.exp(cs - lw))
            yield
            p_rest = jnp.exp(tot - cs)
            store("k_h", k_f * p_rest)
            store("b_h", b_f * p_rest)
            yield
            v_b = v_f.astype(BF16)
            store("v", v_b)
            ptot_ref[par, s, 0:1, :] = jnp.exp(tot)
            qs_ref[par, s] = r_f * k_f * rk_ref[...]
            vs_ref[par, s] = v_b
            gs_ref[par, s] = g_f.astype(BF16)
            yield

    tail_pieces, prep_pieces = tail(), prepare()
    n_pieces = NS * (H // 2 + RW_PREP_PIECES)
    state = dict(ticks=0, done=0)

    def tick():
        state["ticks"] += 1
        due = -(-state["ticks"] * n_pieces // RW_STAGES)
        while state["done"] < due:
            state["done"] += 1
            if next(tail_pieces, "done") == "done":
                next(prep_pieces, None)

    def drain_tail():
        for _ in tail_pieces:
            state["done"] += 1

    _rwkv_heads(ops, stril, tril, both, eye, z_ref, ys_ref, tick, drain_tail)
    for _ in prep_pieces:
        pass


def _rwkv_heads(ops, stril, tril, both, eye, z_ref, ys_ref, tick, drain_tail):
    C, N = RW_CHUNK, RW_N
    hs = list(ops)

    def each(f):
        out = {h: f(h) for h in hs}
        tick()
        return out

    get = lambda name: (lambda h: ops[h][name]())
    a_t, r_t, b_t, k_t, k_h, b_h, p_tot = (get(n) for n in ("a_t", "r_t", "b_t", "k_t", "k_h", "b_h", "p_tot"))
    v = each(lambda h: ops[h]["v"]())
    ar = each(lambda h: jnp.concatenate([a_t(h), r_t(h)], axis=0).astype(BF16))
    m_b = each(lambda h: _dot_nt(ar[h], b_t(h).astype(BF16)))
    m_k = each(lambda h: _dot_nt(ar[h], k_t(h).astype(BF16)))
    l_ab = each(lambda h: jnp.where(stril, m_b[h][:C], 0.0).astype(BF16))
    m_rb = each(lambda h: jnp.where(tril, m_b[h][C:], 0.0).astype(BF16))
    lm_k = each(lambda h: jnp.where(both, m_k[h], 0.0).astype(BF16))
    lmv = each(lambda h: _dot(lm_k[h], v[h]))

    x = each(lambda h: jnp.concatenate([lmv[h][:C], a_t(h).astype(F32)], axis=1))
    y0 = each(lambda h: lmv[h][C:])
    hk = each(lambda h: _dot_tn(k_h(h).astype(BF16), v[h]))
    lp = l_ab
    step = 1
    while step < C:
        x = each(lambda h: x[h] + _dot(lp[h], x[h].astype(BF16)))
        step *= 2
        if step < C:
            lp = each(lambda h: _dot(lp[h], lp[h]).astype(BF16))
    xb = each(lambda h: x[h].astype(BF16))
    yq = each(lambda h: _dot(m_rb[h], xb[h]))
    gh = each(lambda h: _dot_tn(b_h(h).astype(BF16), xb[h]))

    z = each(lambda h: _split2(z_ref[h[0], h[1]]))
    qg = each(lambda h: jnp.concatenate([yq[h][:, N:] + r_t(h).astype(F32),
                                         gh[h][:, N:] + eye * p_tot(h)], axis=0).astype(BF16))
    qgz = each(lambda h: _dot(qg[h], z[h][0]) + _dot(qg[h], z[h][1]))
    yz = each(lambda h: qgz[h][:C])
    gz = each(lambda h: qgz[h][C:])
    drain_tail()
    for s, hd in hs:
        z_ref[s, hd] = (gh[s, hd][:, :N] + hk[s, hd]) + gz[s, hd]
        ys_ref[s, :, hd * N:(hd + 1) * N] = (yq[s, hd][:, :N] + y0[s, hd]) + yz[s, hd]


RW_SEQS_PER_STEP = 1


def _rwkv_chunk(proj, B, L, mu3, mul, w0, a0, k_k, k_a, w2p, a2p, g2p, r_k, ln_w, ln_b):
    T = B * L
    W = w0.shape[1]
    C = RW_CHUNK
    nc = L // C
    H = W // RW_N
    assert H % 2 == 0
    ns = RW_SEQS_PER_STEP if B % RW_SEQS_PER_STEP == 0 else 1
    lw = mul.shape[1]
    proj3 = proj.reshape(B, L, proj.shape[1])
    assert nc % 2 == 0
    np2 = nc // 2
    cols = lambda width, j: pl.BlockSpec((ns, 2 * C, width), lambda bi, i: (bi, jnp.minimum(i, np2 - 1), j))
    full = lambda a: pl.BlockSpec(a.shape, lambda bi, i: (0, 0))
    consts = (mu3, mul, w0, a0, k_k, k_a, w2p, a2p, g2p, r_k, ln_w, ln_b)
    out = pl.pallas_call(
        _rwkv_chunk_kernel,
        grid=(B // ns, np2 + 1),
        in_specs=[cols(W, 1), cols(W, 2), cols(W, 3), cols(lw, 4 * W // lw)] + [full(a) for a in consts],
        out_specs=pl.BlockSpec((ns, 2 * C, W), lambda bi, i: (bi, jnp.maximum(i - 1, 0), 0)),
        out_shape=jax.ShapeDtypeStruct((B, L, W), BF16),
        scratch_shapes=[pltpu.VMEM((ns, H, RW_N, RW_N), F32), pltpu.VMEM((ns, C, W), F32),
                        pltpu.VMEM((2, ns, C, W), F32), pltpu.VMEM((2, ns, C, W), BF16),
                        pltpu.VMEM((2, ns, C, W), BF16),
                        pltpu.VMEM((ns, 8, W), F32), pltpu.VMEM((ns, 8, lw), F32),
                        pltpu.VMEM((2, len(RW_OPERANDS), ns, C, W), BF16), pltpu.VMEM((2, ns, 8, W), F32)],
        compiler_params=_cparams("arbitrary", "arbitrary"),
        name="rwkv_chunk",
    )(proj3, proj3, proj3, proj3, *consts)
    return out.reshape(T, W)


def _first_index_of_max(vals, lane, valid):
    neg = jnp.float32(-jnp.inf)
    masked = jnp.where(valid, vals, neg)
    m = jnp.max(masked, axis=-1, keepdims=True)
    idx = jnp.min(jnp.where(valid & (masked == m), lane, LANES), axis=-1, keepdims=True)
    return m, idx


def _outproj_kernel(s5_ref, rw_ref, x_ref, wt_ref, wb_ref, g_ref, wr_ref, br_ref,
                    h_ref, hn_ref, rec_ref):
    h = x_ref[...] + (_dot(s5_ref[...], wt_ref[...]) + _dot(rw_ref[...], wb_ref[...]))
    h_ref[...] = h
    ms = jnp.mean(h * h, axis=-1, keepdims=True)
    hn = h * lax.rsqrt(ms + RMS_EPS) * g_ref[...]
    hn_ref[...] = _pack_halves(hn).reshape(hn_ref.shape)

    hh, hl = _split2(hn)
    wh, wl = wr_ref[:, :LANES], wr_ref[:, LANES:]
    logits = _dot(hh, wh) + (_dot(hh, wl) + _dot(hl, wh)) + br_ref[...]
    lane = lax.broadcasted_iota(I32, logits.shape, 1)
    is_grp = (lane >= N_EXPERTS) & (lane < N_EXPERTS + N_GROUPS)
    gmax, gidx = _first_index_of_max(logits, lane, is_grp)
    gsum = jnp.sum(jnp.where(is_grp, jnp.exp(logits - gmax), 0.0), axis=-1, keepdims=True)
    p_grp = 1.0 / gsum
    grp = gidx - N_EXPERTS
    in_grp = (lane >= grp * EPG) & (lane < (grp + 1) * EPG)
    m1, i1 = _first_index_of_max(logits, lane, in_grp)
    m2, i2 = _first_index_of_max(logits, lane, in_grp & (lane != i1))
    e = jnp.exp(m2 - m1)
    g1 = p_grp / (1.0 + e)
    g2 = p_grp * e / (1.0 + e)
    rec = jnp.where(lane == 0, i1.astype(F32),
          jnp.where(lane == 1, i2.astype(F32),
          jnp.where(lane == 2, g1, jnp.where(lane == 3, g2, 0.0))))
    rec_ref[...] = rec


def _outproj(s5o, rwo, x2, w_top, w_bot, g, wr, b_route, tm):
    T, D = x2.shape
    W = s5o.shape[1]
    full = lambda a: pl.BlockSpec(a.shape, lambda i: (0, 0))
    return pl.pallas_call(
        _outproj_kernel,
        grid=(T // tm,),
        in_specs=[pl.BlockSpec((tm, W), lambda i: (i, 0)), pl.BlockSpec((tm, W), lambda i: (i, 0)),
                  pl.BlockSpec((tm, D), lambda i: (i, 0)),
                  full(w_top), full(w_bot), full(g), full(wr), full(b_route)],
        out_specs=[pl.BlockSpec((tm, D), lambda i: (i, 0)), pl.BlockSpec((tm, 1, D // 2), lambda i: (i, 0, 0)),
                   pl.BlockSpec((tm, LANES), lambda i: (i, 0))],
        out_shape=[jax.ShapeDtypeStruct((T, D), F32), jax.ShapeDtypeStruct((T, 1, D // 2), U32),
                   jax.ShapeDtypeStruct((T, LANES), F32)],
        compiler_params=_cparams("arbitrary"),
        name="outproj_route",
    )(s5o, rwo, x2, w_top, w_bot, g, wr, b_route)


def _onehots(rec, lane):
    oh0 = (lane == rec[:, 0:1].astype(I32)).astype(F32)
    oh1 = (lane == rec[:, 1:2].astype(I32)).astype(F32)
    return oh0, oh1


def _rank_kernel(rec_ref, rank_ref, cnt_ref, base_ref):
    tb = rec_ref.shape[0]

    @pl.when(pl.program_id(0) == 0)
    def _():
        base_ref[...] = jnp.zeros_like(base_ref)

    lane = lax.broadcasted_iota(I32, (tb, LANES), 1)
    oh0, oh1 = _onehots(rec_ref[...], lane)
    both = oh0 + oh1
    ri = lax.broadcasted_iota(I32, (tb, tb), 0)
    ci = lax.broadcasted_iota(I32, (tb, tb), 1)
    before = _dot((ri > ci).astype(BF16), both.astype(BF16)) + base_ref[0:1, :]
    rank0 = jnp.sum(oh0 * before, axis=-1, keepdims=True)
    rank1 = jnp.sum(oh1 * before, axis=-1, keepdims=True)
    rank_ref[...] = jnp.where(lane == 0, rank0, jnp.where(lane == 1, rank1, 0.0))
    total = base_ref[0:1, :] + jnp.sum(both, axis=0, keepdims=True)
    base_ref[0:1, :] = total
    cnt_ref[...] = jnp.broadcast_to(total, cnt_ref.shape)


def _rank(rec, tb):
    T = rec.shape[0]
    return pl.pallas_call(
        _rank_kernel,
        grid=(T // tb,),
        in_specs=[pl.BlockSpec((tb, LANES), lambda i: (i, 0))],
        out_specs=[pl.BlockSpec((tb, LANES), lambda i: (i, 0)), pl.BlockSpec((8, LANES), lambda i: (0, 0))],
        out_shape=[jax.ShapeDtypeStruct((T, LANES), F32), jax.ShapeDtypeStruct((8, LANES), F32)],
        scratch_shapes=[pltpu.VMEM((8, LANES), F32)],
        compiler_params=_cparams("arbitrary"),
        name="moe_rank",
    )(rec)


def _padded_starts(cnt):
    padded = jnp.ceil(cnt * (1.0 / MOE_BLOCK)) * MOE_BLOCK
    ri = lax.broadcasted_iota(I32, (LANES, LANES), 0)
    ci = lax.broadcasted_iota(I32, (LANES, LANES), 1)
    p8 = jnp.broadcast_to(padded, (8, LANES))
    pend = _dot_exact_lhs_rhs(p8, (ri <= ci).astype(BF16))[0:1, :]
    return pend - padded, pend


def _dot_exact_lhs_rhs(a, b_bf16):
    ah, am, al = _split3(a)
    return _dot(ah, b_bf16) + (_dot(am, b_bf16) + _dot(al, b_bf16))


def _dest_kernel(rec_ref, rank_ref, cnt_ref, dest_ref, blk_ref):
    tb = rec_ref.shape[0]
    cnt = cnt_ref[0:1, :]
    pstart, pend = _padded_starts(cnt)
    lane = lax.broadcasted_iota(I32, (tb, LANES), 1)
    oh0, oh1 = _onehots(rec_ref[...], lane)
    rank = rank_ref[...]
    d0 = jnp.sum(oh0 * pstart, axis=-1, keepdims=True) + rank[:, 0:1]
    d1 = jnp.sum(oh1 * pstart, axis=-1, keepdims=True) + rank[:, 1:2]
    dest_ref[...] = jnp.where(lane == 0, d0, jnp.where(lane == 1, d1, 0.0)).astype(I32)

    @pl.when(pl.program_id(0) == 0)
    def _():
        nb = blk_ref.shape[0]
        blane = lax.broadcasted_iota(I32, (nb, LANES), 1)
        bstart = (lax.broadcasted_iota(I32, (nb, 1), 0) * MOE_BLOCK).astype(F32)
        is_e = blane < N_EXPERTS
        bexp = jnp.sum(jnp.where(is_e & (pend <= bstart), 1.0, 0.0), axis=-1, keepdims=True)
        bexp = jnp.minimum(bexp, N_EXPERTS - 1.0)
        nact = jnp.max(jnp.where(is_e, pend, 0.0), axis=-1, keepdims=True) * (1.0 / MOE_BLOCK)
        lastblk = jnp.where(is_e & (pend > pstart), pend - MOE_BLOCK, -1.0)
        brow = lax.broadcasted_iota(I32, (nb, 1), 0)
        out = jnp.where(blane == 0, bexp, jnp.where(blane == 1, nact, 0.0))
        out = jnp.where(brow == nb - 1, lastblk, out)
        blk_ref[...] = out.astype(I32)


def _dest(rec, rank, cnt, tb, nb_rows):
    T = rec.shape[0]
    return pl.pallas_call(
        _dest_kernel,
        grid=(T // tb,),
        in_specs=[pl.BlockSpec((tb, LANES), lambda i: (i, 0)), pl.BlockSpec((tb, LANES), lambda i: (i, 0)),
                  pl.BlockSpec((8, LANES), lambda i: (0, 0))],
        out_specs=[pl.BlockSpec((tb, LANES), lambda i: (i, 0)), pl.BlockSpec((nb_rows, LANES), lambda i: (0, 0))],
        out_shape=[jax.ShapeDtypeStruct((T, LANES), I32), jax.ShapeDtypeStruct((nb_rows, LANES), I32)],
        compiler_params=_cparams("arbitrary"),
        name="moe_dest",
    )(rec, rank, cnt)


def _scatter_kernel(pad_ref, na_ref, dest_ref, hn_ref, xs_ref, zbuf_ref, idx_ref, sa_ref, sb_ref,
                    sem_a, sem_b, isem, zsem, *, tb, n_blocks, n_tail, n_steps):
    i = pl.program_id(0)

    @pl.when(i == 0)
    def _():
        zbuf_ref[...] = jnp.zeros_like(zbuf_ref)

        def zero_rows(start):
            return pltpu.make_async_copy(zbuf_ref, xs_ref.at[pl.ds(start, MOE_BLOCK)], zsem)

        def pad_start(e, c):
            @pl.when(pad_ref[e] >= 0)
            def _():
                zero_rows(pad_ref[e]).start()
            return c

        def pad_wait(e, c):
            @pl.when(pad_ref[e] >= 0)
            def _():
                zero_rows(pad_ref[e]).wait()
            return c

        def tail_start(k, c):
            @pl.when(na_ref[0] + k < n_blocks)
            def _():
                zero_rows((na_ref[0] + k) * MOE_BLOCK).start()
            return c

        def tail_wait(k, c):
            @pl.when(na_ref[0] + k < n_blocks)
            def _():
                zero_rows((na_ref[0] + k) * MOE_BLOCK).wait()
            return c

        lax.fori_loop(0, N_EXPERTS, pad_start, 0)
        lax.fori_loop(0, n_tail, tail_start, 0)
        lax.fori_loop(0, N_EXPERTS, pad_wait, 0)
        lax.fori_loop(0, n_tail, tail_wait, 0)

    def idx_copy(step, slot):
        return pltpu.make_async_copy(dest_ref.at[step], idx_ref.at[pl.ds(slot * (2 * tb), 2 * tb)], isem)

    @pl.when(i == 0)
    def _():
        idx_copy(0, 0).start()

    slot = i % 2
    idx_copy(i, slot).wait()

    @pl.when(i + 1 < n_steps)
    def _():
        idx_copy(i + 1, 1 - slot).start()

    hb = tb // 2
    for half, (buf, sem) in enumerate(((sa_ref, sem_a), (sb_ref, sem_b))):
        def wait_rows(buf=buf, sem=sem):
            for _ in range(2):
                pltpu.make_async_copy(buf, xs_ref.at[pl.ds(0, hb)], sem).wait()

        @pl.when(i > 0)
        def _():
            wait_rows()

        buf[...] = hn_ref[half * hb:(half + 1) * hb]

        def issue(t, c, buf=buf, sem=sem, half=half):
            for s in range(2):
                d = idx_ref[slot * (2 * tb) + 2 * (half * hb + t) + s]
                pltpu.make_async_copy(buf.at[pl.ds(t, 1)], xs_ref.at[pl.ds(d, 1)], sem).start(priority=s)
            return c

        lax.fori_loop(0, hb, issue, 0, unroll=8)

    @pl.when(i == n_steps - 1)
    def _():
        for buf, sem in ((sa_ref, sem_a), (sb_ref, sem_b)):
            for _ in range(2):
                pltpu.make_async_copy(buf, xs_ref.at[pl.ds(0, hb)], sem).wait()


def _scatter(lastblk, nact, dest2, hn, tb, n_blocks):
    T, _, D = hn.shape
    n_tail = n_blocks - (-(-2 * T // MOE_BLOCK))
    hb = tb // 2
    return pl.pallas_call(
        functools.partial(_scatter_kernel, tb=tb, n_blocks=n_blocks, n_tail=n_tail, n_steps=T // tb),
        grid_spec=pltpu.PrefetchScalarGridSpec(
            num_scalar_prefetch=2, grid=(T // tb,),
            in_specs=[pl.BlockSpec(memory_space=pl.ANY),
                      pl.BlockSpec((tb, 1, D), lambda i, pad, na: (i, 0, 0))],
            out_specs=pl.BlockSpec(memory_space=pl.ANY),
            scratch_shapes=[pltpu.VMEM((MOE_BLOCK, 1, D), hn.dtype), pltpu.SMEM((4 * tb,), I32),
                            pltpu.VMEM((hb, 1, D), hn.dtype), pltpu.VMEM((hb, 1, D), hn.dtype),
                            pltpu.SemaphoreType.DMA, pltpu.SemaphoreType.DMA,
                            pltpu.SemaphoreType.DMA, pltpu.SemaphoreType.DMA]),
        out_shape=jax.ShapeDtypeStruct((n_blocks * MOE_BLOCK, 1, D), hn.dtype),
        compiler_params=_cparams("arbitrary"),
        name="moe_scatter",
    )(lastblk, nact, dest2, hn)


W_SLOTS = 2


def _expert_kernel(be_ref, na_ref, x_ref, wg_hbm, wu_hbm, wd_hbm, y_ref,
                   wg_buf, wu_buf, wd_buf, wgb_ref, wub_ref, wdb_ref, x2_ref, elist_ref, state_ref, sems):
    j = pl.program_id(0)
    na = na_ref[0]

    def weight_copies(e, slot):
        return [pltpu.make_async_copy(src.at[e], buf.at[slot], sems.at[slot, n])
                for n, (src, buf) in enumerate(((wg_hbm, wg_buf), (wu_hbm, wu_buf), (wd_hbm, wd_buf)))]

    @pl.when(j == 0)
    def _():
        def scan(k, n):
            is_new = (k == 0) | (be_ref[k] != be_ref[jnp.maximum(k - 1, 0)])

            @pl.when(is_new)
            def _():
                elist_ref[n] = be_ref[k]
            return n + is_new.astype(I32)

        n_exp = lax.fori_loop(0, na, scan, 0)
        state_ref[0] = 0
        state_ref[1] = n_exp
        for n in range(W_SLOTS):
            @pl.when(n < n_exp)
            def _():
                for cp in weight_copies(elist_ref[n], n):
                    cp.start()

    @pl.when(j < na)
    def _():
        @pl.when((j == 0) | (be_ref[j] != be_ref[jnp.maximum(j - 1, 0)]))
        def _():
            n = state_ref[0]
            slot = n % W_SLOTS
            for cp in weight_copies(elist_ref[n], slot):
                cp.wait()
            wgb_ref[...] = wg_buf[slot].astype(BF16)
            wub_ref[...] = wu_buf[slot].astype(BF16)
            wdb_ref[...] = wd_buf[slot].astype(BF16)

            @pl.when(n + W_SLOTS < state_ref[1])
            def _():
                for cp in weight_copies(elist_ref[n + W_SLOTS], slot):
                    cp.start()

            state_ref[0] = n + 1

        x2_ref[...] = x_ref[...].reshape(x2_ref.shape)
        lo, hi = _unpack_halves(x2_ref[...])
        lo, hi = lo.astype(BF16), hi.astype(BF16)
        half = lo.shape[1]
        hg = _dot(lo, wgb_ref[:half, :]) + _dot(hi, wgb_ref[half:, :])
        hu = _dot(lo, wub_ref[:half, :]) + _dot(hi, wub_ref[half:, :])
        act = (hg * jax.nn.sigmoid(hg)) * hu
        y = _dot(act.astype(BF16), wdb_ref[...])
        y_ref[...] = _pack_halves(y).reshape(y_ref.shape)

    @pl.when(j >= na)
    def _():
        y_ref[...] = jnp.zeros_like(y_ref)


def _experts(blk_exp, nact, xs, w_gate, w_up, w_down, n_blocks):
    DP = xs.shape[2]
    D = 2 * DP
    rows = n_blocks * MOE_BLOCK
    DE = w_gate.shape[2]
    blk = lambda j, be, na: (jnp.minimum(j, na[0] - 1), 0, 0)
    hbm = pl.BlockSpec(memory_space=pl.ANY)
    return pl.pallas_call(
        _expert_kernel,
        grid_spec=pltpu.PrefetchScalarGridSpec(
            num_scalar_prefetch=2, grid=(n_blocks,),
            in_specs=[pl.BlockSpec((MOE_BLOCK, 1, DP), blk), hbm, hbm, hbm],
            out_specs=pl.BlockSpec((MOE_BLOCK, 1, DP), lambda j, be, na: (j, 0, 0)),
            scratch_shapes=[pltpu.VMEM((W_SLOTS, D, DE), F32), pltpu.VMEM((W_SLOTS, D, DE), F32),
                            pltpu.VMEM((W_SLOTS, DE, D), F32),
                            pltpu.VMEM((D, DE), BF16), pltpu.VMEM((D, DE), BF16), pltpu.VMEM((DE, D), BF16),
                            pltpu.VMEM((MOE_BLOCK, DP), U32), pltpu.SMEM((N_EXPERTS,), I32), pltpu.SMEM((2,), I32),
                            pltpu.SemaphoreType.DMA((W_SLOTS, 3))]),
        out_shape=jax.ShapeDtypeStruct((rows, 1, DP), U32),
        compiler_params=_cparams("arbitrary"),
        name="moe_experts",
    )(blk_exp, nact, xs, w_gate, w_up, w_down)


def _combine_kernel(dest_ref, ys_ref, h_ref, rec_ref, g_ref, o_ref, idx_ref, ya0_ref, ya1_ref, yb0_ref, yb1_ref,
                    y2_ref, sem_a, sem_b, isem, *, normalize, n_steps):
    i = pl.program_id(0)
    tb = h_ref.shape[0]
    hb = tb // 2
    halves = ((ya0_ref, ya1_ref, sem_a), (yb0_ref, yb1_ref, sem_b))

    def idx_copy(step, slot):
        return pltpu.make_async_copy(dest_ref.at[step], idx_ref.at[pl.ds(slot * (2 * tb), 2 * tb)], isem)

    def issue(half, slot):
        bufs, sem = halves[half][:2], halves[half][2]

        def body(t, c):
            for s in range(2):
                d = idx_ref[slot * (2 * tb) + 2 * (half * hb + t) + s]
                pltpu.make_async_copy(ys_ref.at[pl.ds(d, 1)], bufs[s].at[pl.ds(t, 1)], sem).start(priority=s)
            return c

        lax.fori_loop(0, hb, body, 0, unroll=8)

    @pl.when(i == 0)
    def _():
        first = idx_copy(0, 0)
        first.start()
        first.wait()
        issue(0, 0)
        issue(1, 0)
        if n_steps > 1:
            idx_copy(1, 1).start()

    nslot = (i + 1) % 2

    @pl.when(i + 1 < n_steps)
    def _():
        idx_copy(i + 1, nslot).wait()

    rec = rec_ref[...]
    for half, (y0_ref, y1_ref, sem) in enumerate(halves):
        rows = slice(half * hb, (half + 1) * hb)
        pltpu.make_async_copy(ys_ref.at[pl.ds(0, hb)], y0_ref, sem).wait()
        pltpu.make_async_copy(ys_ref.at[pl.ds(0, hb)], y1_ref, sem).wait()
        y2_ref[...] = y0_ref[...].reshape(y2_ref.shape)
        h = h_ref[rows, :] + rec[rows, 2:3] * jnp.concatenate(_unpack_halves(y2_ref[...]), axis=1)
        y2_ref[...] = y1_ref[...].reshape(y2_ref.shape)
        h = h + rec[rows, 3:4] * jnp.concatenate(_unpack_halves(y2_ref[...]), axis=1)
        if normalize:
            ms = jnp.mean(h * h, axis=-1, keepdims=True)
            h = h * lax.rsqrt(ms + RMS_EPS) * g_ref[...]
        o_ref[rows, :] = h

        @pl.when(i + 1 < n_steps)
        def _():
            issue(half, nslot)

    @pl.when(i + 2 < n_steps)
    def _():
        idx_copy(i + 2, i % 2).start()


def _combine(dest2, ys, h, rec, g, tb, normalize):
    T, D = h.shape
    hb = tb // 2
    return pl.pallas_call(
        functools.partial(_combine_kernel, normalize=normalize, n_steps=T // tb),
        grid=(T // tb,),
        in_specs=[pl.BlockSpec(memory_space=pl.ANY), pl.BlockSpec(memory_space=pl.ANY),
                  pl.BlockSpec((tb, D), lambda i: (i, 0)), pl.BlockSpec((tb, LANES), lambda i: (i, 0)),
                  pl.BlockSpec((1, D), lambda i: (0, 0))],
        out_specs=pl.BlockSpec((tb, D), lambda i: (i, 0)),
        out_shape=jax.ShapeDtypeStruct((T, D), F32),
        scratch_shapes=[pltpu.SMEM((4 * tb,), I32)] + [pltpu.VMEM((hb, 1, D // 2), U32)] * 4 +
                       [pltpu.VMEM((hb, D // 2), U32)] + [pltpu.SemaphoreType.DMA] * 3,
        compiler_params=_cparams("arbitrary"),
        name="moe_combine",
    )(dest2, ys, h, rec, g)


def _pad_cols(a, n):
    return jnp.pad(a, ((0, 0), (0, n - a.shape[1])))


def _pad_rows(a, n, at=0):
    return jnp.pad(a, ((at, n - a.shape[0] - at), (0, 0)))


def _layer(x2, B, L, p):
    T, D = x2.shape
    W = p["s5_d"].shape[0]
    G, P = p["s5_lambda_re"].shape
    HG = W // G
    dl, al, gl = p["rwkv_w2"].shape[0], p["rwkv_a2"].shape[0], p["rwkv_g2"].shape[0]
    H = W // RW_N

    w_in = p["w_in"]
    o = W
    cols = [w_in[:, :W], w_in[:, o:o + 3 * W]]
    o += 3 * W
    cols += [_pad_cols(w_in[:, o:o + dl], 128), _pad_cols(w_in[:, o + dl:o + dl + al], 128),
             _pad_cols(w_in[:, o + dl + al:o + dl + al + gl], 256)]
    w_in_r = jnp.concatenate(cols, axis=1).astype(BF16)
    tm_in = min(1024, T)
    proj = _inproj(x2, p["norm_mix_g"].reshape(1, D), w_in_r, tm_in, 1152)

    C = S5_CHUNK
    nc = L // C
    ct_re = jnp.swapaxes(p["s5_c_re"], 1, 2)
    ct_im = jnp.swapaxes(p["s5_c_im"], 1, 2)
    rb = min(LANES, B * nc)
    z = _s5_in(proj.reshape(B * nc, C, proj.shape[1]), G, rb)
    y_t = _s5_chunk(z, p["s5_lambda_re"], p["s5_lambda_im"], p["s5_log_step"],
                    ct_re, ct_im, p["s5_b_re"], p["s5_b_im"], p["s5_c_re"], p["s5_c_im"], nc)
    y_ssm = _s5_out(y_t, rb).reshape(T, W)
    tm = min(512, T)
    s5_out = _s5_glu(y_ssm, proj, p["s5_d"].reshape(1, W), p["s5_w_glu"].astype(BF16),
                     p["s5_b_glu"].reshape(1, W), tm)

    mu = p["rwkv_mu"]
    mu3 = _pad_rows(mu[:3 * W].reshape(3, W), 8)
    o = 3 * W
    mul = jnp.concatenate([_pad_cols(mu[None, o:o + dl], 128), _pad_cols(mu[None, o + dl:o + dl + al], 128),
                           _pad_cols(mu[None, o + dl + al:], 256)], axis=1)
    w2p = _pad_rows(p["rwkv_w2"], 128).astype(BF16)
    a2p = _pad_rows(p["rwkv_a2"], 128).astype(BF16)
    g2p = _pad_rows(p["rwkv_g2"], 256).astype(BF16)
    row = lambda a: a.reshape(1, W)
    rw_out = _rwkv_chunk(proj, B, L, mu3, mul, row(p["rwkv_w0"]), row(p["rwkv_a0"]), row(p["rwkv_k_k"]),
                         row(p["rwkv_k_a"]), w2p, a2p, g2p, row(p["rwkv_r_k"]), row(p["rwkv_ln_w"]),
                         row(p["rwkv_ln_b"]))

    w_out = p["w_out"].astype(BF16)
    w_route = jnp.concatenate([p["w_route_exp"], p["w_route_grp"]], axis=1)
    w_route = _pad_cols(w_route, LANES)
    wr_hi = w_route.astype(BF16)
    wr_lo = (w_route - wr_hi.astype(F32)).astype(BF16)
    wr = jnp.concatenate([wr_hi, wr_lo], axis=1)
    b_route = _pad_cols(jnp.concatenate([p["b_route_exp"], p["b_route_grp"]])[None, :], LANES)
    tm_o = min(512, T)
    h, hn, rec = _outproj(s5_out, rw_out, x2, w_out[:W], w_out[W:], p["norm_ffn_g"].reshape(1, D),
                          wr, b_route, tm_o)

    tb = min(1024, T)
    rank, cnt = _rank(rec, tb)
    n_blocks = -(-2 * T // MOE_BLOCK) + N_EXPERTS
    nb_rows = -(-(n_blocks + 1) // 8) * 8
    dest, blk = _dest(rec, rank, cnt, tb, nb_rows)
    blk_exp = blk[:n_blocks, 0]
    nact = blk[0:1, 1]
    lastblk = blk[nb_rows - 1, :N_EXPERTS]
    ts = min(256, T)
    dest2 = dest[:, :2].reshape(T // ts, 2 * ts)

    xs = _scatter(lastblk, nact, dest2, hn, ts, n_blocks)
    ys = _experts(blk_exp, nact, xs, p["w_gate"], p["w_up"], p["w_down"], n_blocks)
    return dest2, ys, h, rec, ts


def kernel(x, norm_mix_g, w_in, s5_lambda_re, s5_lambda_im, s5_log_step, s5_b_re, s5_b_im, s5_c_re, s5_c_im, s5_d, s5_w_glu, s5_b_glu, rwkv_mu, rwkv_w0, rwkv_w2, rwkv_a0, rwkv_a2, rwkv_g2, rwkv_k_k, rwkv_k_a, rwkv_r_k, rwkv_ln_w, rwkv_ln_b, w_out, norm_ffn_g, w_route_grp, b_route_grp, w_route_exp, b_route_exp, w_gate, w_up, w_down, norm_final_g):
    B, L, D = x.shape
    params = dict(
        norm_mix_g=norm_mix_g, w_in=w_in, s5_lambda_re=s5_lambda_re, s5_lambda_im=s5_lambda_im,
        s5_log_step=s5_log_step, s5_b_re=s5_b_re, s5_b_im=s5_b_im, s5_c_re=s5_c_re, s5_c_im=s5_c_im,
        s5_d=s5_d, s5_w_glu=s5_w_glu, s5_b_glu=s5_b_glu, rwkv_mu=rwkv_mu, rwkv_w0=rwkv_w0, rwkv_w2=rwkv_w2,
        rwkv_a0=rwkv_a0, rwkv_a2=rwkv_a2, rwkv_g2=rwkv_g2, rwkv_k_k=rwkv_k_k, rwkv_k_a=rwkv_k_a,
        rwkv_r_k=rwkv_r_k, rwkv_ln_w=rwkv_ln_w, rwkv_ln_b=rwkv_ln_b, w_out=w_out, norm_ffn_g=norm_ffn_g,
        w_route_grp=w_route_grp, b_route_grp=b_route_grp, w_route_exp=w_route_exp, b_route_exp=b_route_exp,
        w_gate=w_gate, w_up=w_up, w_down=w_down)
    depth = norm_mix_g.shape[0]
    h2 = x.reshape(B * L, D)
    for l in range(depth):
        p = {k_: v_[l] for k_, v_ in params.items()}
        dest2, ys, h, rec, ts = _layer(h2, B, L, p)
        h2 = _combine(dest2, ys, h, rec, norm_final_g.reshape(1, D), ts, normalize=(l == depth - 1))
    return h2.reshape(B, L, D)
```

```python
import functools

import jax
import jax.numpy as jnp
from jax import lax
from jax.experimental import pallas as pl
from jax.experimental.pallas import tpu as pltpu

F32 = jnp.float32
BF16 = jnp.bfloat16
I32 = jnp.int32

RMS_EPS = 1e-6
S5_CHUNK = 64
S5_HG = 16
S5_P = 64
RW_N = 64
RW_CHUNK = 64
RW_GN_EPS = 64e-5
N_GROUPS = 8
EPG = 8
N_EXPERTS = 64
MOE_BLOCK = 256
LANES = 128
VMEM_LIMIT = 56 * 1024 * 1024


def _cparams(*sem, **kw):
    return pltpu.CompilerParams(dimension_semantics=sem, vmem_limit_bytes=VMEM_LIMIT, **kw)


def _split2(x):
    hi = x.astype(BF16)
    lo = (x - hi.astype(F32)).astype(BF16)
    return hi, lo


def _split3(x):
    hi = x.astype(BF16)
    r = x - hi.astype(F32)
    mid = r.astype(BF16)
    lo = (r - mid.astype(F32)).astype(BF16)
    return hi, mid, lo


def _dot(a, b):
    return jnp.dot(a, b, preferred_element_type=F32)


def _dot_nt(a, b):
    return lax.dot_general(a, b, (((1,), (1,)), ((), ())), preferred_element_type=F32)


def _dot_tn(a, b):
    return lax.dot_general(a, b, (((0,), (0,)), ((), ())), preferred_element_type=F32)


def _dot_x3(a, b):
    ah, al = _split2(a)
    bh, bl = _split2(b)
    return _dot(ah, bh) + (_dot(ah, bl) + _dot(al, bh))


def _dot_exact_lhs(a_bf16, b):
    bh, bm, bl = _split3(b)
    return _dot(a_bf16, bh) + (_dot(a_bf16, bm) + _dot(a_bf16, bl))


def _cmul(ar, ai, br, bi):
    return ar * br - ai * bi, ar * bi + ai * br


U32 = jnp.uint32


def _pack_halves(x):
    half = x.shape[1] // 2
    lo = lax.bitcast_convert_type(x[:, :half].astype(BF16).astype(F32), U32)
    hi = lax.bitcast_convert_type(x[:, half:].astype(BF16).astype(F32), U32)
    return (lo >> 16) | hi


def _unpack_halves(p):
    lo = lax.bitcast_convert_type(p << 16, F32)
    hi = lax.bitcast_convert_type(p & jnp.uint32(0xFFFF0000), F32)
    return lo, hi


def _inproj_kernel(x_ref, g_ref, w_ref, o_ref, hn_ref):
    @pl.when(pl.program_id(1) == 0)
    def _():
        x = x_ref[...]
        ms = jnp.mean(x * x, axis=-1, keepdims=True)
        hn_ref[...] = (x * lax.rsqrt(ms + RMS_EPS) * g_ref[...]).astype(BF16)

    o_ref[...] = _dot(hn_ref[...], w_ref[...])


def _inproj(x2, g, w_bf16, tm, tn):
    T, D = x2.shape
    N = w_bf16.shape[1]
    return pl.pallas_call(
        _inproj_kernel,
        grid=(T // tm, N // tn),
        in_specs=[
            pl.BlockSpec((tm, D), lambda i, j: (i, 0)),
            pl.BlockSpec((1, D), lambda i, j: (0, 0)),
            pl.BlockSpec((D, tn), lambda i, j: (0, j)),
        ],
        out_specs=pl.BlockSpec((tm, tn), lambda i, j: (i, j)),
        out_shape=jax.ShapeDtypeStruct((T, N), F32),
        scratch_shapes=[pltpu.VMEM((tm, D), BF16)],
        compiler_params=_cparams("arbitrary", "arbitrary"),
        name="inproj",
    )(x2, g, w_bf16)


def _binpow(ar, ai, expo, nbits):
    pr = jnp.ones(expo.shape, F32)
    pi = jnp.zeros(expo.shape, F32)
    sr, si = ar, ai
    for bit in range(nbits):
        m = ((expo >> bit) & 1) == 1
        nr, ni = _cmul(pr, pi, sr, si)
        pr = jnp.where(m, nr, pr)
        pi = jnp.where(m, ni, pi)
        if bit + 1 < nbits:
            sr, si = _cmul(sr, si, sr, si)
    return pr, pi


def _frame_powers(ar, ai, reverse, plus_one):
    C, HG, P = S5_CHUNK, S5_HG, ar.shape[0]
    fpt = LANES // HG
    j = lax.broadcasted_iota(I32, (P, LANES), 1) // HG
    inner_r, inner_i = _binpow(ar, ai, (fpt - 1 - j) if reverse else j, (fpt - 1).bit_length())
    sr, si = ar, ai
    for _ in range(fpt.bit_length() - 1):
        sr, si = _cmul(sr, si, sr, si)
    outer = [(ar, ai) if plus_one else (jnp.ones_like(ar), jnp.zeros_like(ai))]
    for _ in range(C // fpt - 1):
        outer.append(_cmul(outer[-1][0], outer[-1][1], sr, si))
    if reverse:
        outer = outer[::-1]
    tiles = [_cmul(inner_r, inner_i, o_r, o_i) for o_r, o_i in outer]
    return (jnp.concatenate([t[0] for t in tiles], axis=1), jnp.concatenate([t[1] for t in tiles], axis=1))


def _zoh(lr, li, dt):
    mag = jnp.exp(lr * dt)
    ang = li * dt
    ar, ai = mag * jnp.cos(ang), mag * jnp.sin(ang)
    den = lr * lr + li * li
    nr, ni = ar - 1.0, ai
    fr = (nr * lr + ni * li) / den
    fi = (ni * lr - nr * li) / den
    return ar, ai, fr, fi


def _s5_ops_group(lrc_ref, lic_ref, dt_ref, ct_re_ref, ct_im_ref, bt_re_ref, bt_im_ref, c_re_ref, c_im_ref,
                  m_ref, wbt_ref, wct_ref, a64_ref):
    C, HG, P = S5_CHUNK, S5_HG, S5_P
    W = C * HG
    dt = jnp.exp(dt_ref[...])
    ar_c, ai_c, fr_c, fi_c = _zoh(lrc_ref[...], lic_ref[...], dt)

    pr, pi = _frame_powers(ar_c, ai_c, False, True)
    qr, qi = _frame_powers(ar_c, ai_c, True, False)

    sel = (lax.broadcasted_iota(I32, (HG, W), 1) % HG == lax.broadcasted_iota(I32, (HG, W), 0)).astype(BF16)
    tile = lambda a: _dot_exact_lhs_rhs(a, sel)

    c1_re, c1_im = _cmul(tile(ct_re_ref[...]), tile(ct_im_ref[...]), pr, pi)
    wct_ref[...] = jnp.concatenate([c1_re, -c1_im], axis=0).T.astype(BF16)

    bb_re, bb_im = _cmul(bt_re_ref[...], bt_im_ref[...], fr_c, fi_c)
    ab_re, ab_im = _cmul(tile(bb_re), tile(bb_im), qr, qi)
    wbt_ref[:P, :] = ab_re.astype(BF16)
    wbt_ref[P:, :] = ab_im.astype(BF16)

    cr, ci = ar_c, ai_c
    for _ in range(C.bit_length() - 1):
        cr, ci = _cmul(cr, ci, cr, ci)
    a64_ref[...] = jnp.concatenate([cr, ci], axis=0)

    strip = _dot_x3(c_re_ref[...], ab_re) - _dot_x3(c_im_ref[...], ab_im)
    ext = jnp.concatenate([strip, jnp.zeros((HG, W), F32)], axis=1)
    for t in range(C):
        off = (C - 1 - t) * HG
        m_ref[t * HG:(t + 1) * HG, :] = ext[:, off:off + W].astype(BF16)


S5_FB = 8


def _s5_in_kernel(x_ref, z_ref):
    G, _, rb = z_ref.shape
    for f in range(S5_FB):
        xt = x_ref[:, f, :].T
        z_ref[:, f * S5_HG:(f + 1) * S5_HG, :] = xt.reshape(G, S5_HG, rb).astype(BF16)


def _s5_in(proj3, G, rb):
    R, C, _ = proj3.shape
    W = G * S5_HG
    return pl.pallas_call(
        _s5_in_kernel,
        grid=(R // rb, C // S5_FB),
        in_specs=[pl.BlockSpec((rb, S5_FB, W), lambda i, j: (i, j, 0))],
        out_specs=pl.BlockSpec((G, S5_FB * S5_HG, rb), lambda i, j: (0, j, i)),
        out_shape=jax.ShapeDtypeStruct((G, C * S5_HG, R), BF16),
        compiler_params=_cparams("arbitrary", "arbitrary"),
        name="s5_in",
    )(proj3)


def _s5_out_kernel(yt_ref, o_ref):
    G, _, rb = yt_ref.shape
    for f in range(S5_FB):
        slab = yt_ref[:, f * S5_HG:(f + 1) * S5_HG, :].reshape(G * S5_HG, rb)
        o_ref[:, f, :] = slab.T


def _s5_out(yt, rb):
    G, CW, R = yt.shape
    C = CW // S5_HG
    W = G * S5_HG
    return pl.pallas_call(
        _s5_out_kernel,
        grid=(R // rb, C // S5_FB),
        in_specs=[pl.BlockSpec((G, S5_FB * S5_HG, rb), lambda i, j: (0, j, i))],
        out_specs=pl.BlockSpec((rb, S5_FB, W), lambda i, j: (i, j, 0)),
        out_shape=jax.ShapeDtypeStruct((R, C, W), F32),
        compiler_params=_cparams("arbitrary", "arbitrary"),
        name="s5_out",
    )(yt)


def _s5_chunk_kernel(*refs, n_chunks):
    params, (z_ref, y_ref), ops = refs[:9], refs[9:11], refs[11:]
    for g in range(z_ref.shape[0]):
        at = lambda rs: [r.at[g] for r in rs]
        _s5_ops_group(*at(params), *at(ops))
        _s5_chunk_group(z_ref.at[g], *at(ops), y_ref.at[g], n_chunks)


def _s5_chunk_group(z_ref, m_ref, wbt_ref, wct_ref, a64_ref, y_ref, n_chunks):
    P = S5_P
    z = z_ref[...]
    R = z.shape[1]
    x = _dot(wbt_ref[...], z)
    y_ref[...] = _dot(m_ref[...], z)
    xr, xi = x[:P], x[P:]
    a = a64_ref[...]
    ar, ai = a[:P], a[P:]
    cidx = lax.broadcasted_iota(I32, (1, R), 1) % n_chunks
    shift = 1
    while shift < n_chunks:
        keep = cidx >= shift
        sr = jnp.where(keep, pltpu.roll(xr, shift, 1), 0.0)
        si = jnp.where(keep, pltpu.roll(xi, shift, 1), 0.0)
        xr, xi = xr + (ar * sr - ai * si), xi + (ar * si + ai * sr)
        ar, ai = ar * ar - ai * ai, 2.0 * ar * ai
        shift *= 2
    keep = cidx >= 1
    s_in = jnp.concatenate([jnp.where(keep, pltpu.roll(xr, 1, 1), 0.0),
                            jnp.where(keep, pltpu.roll(xi, 1, 1), 0.0)], axis=0)
    sh, sl = _split2(s_in)
    wct = wct_ref[...]
    y_ref[...] = y_ref[...] + (_dot(wct, sh) + _dot(wct, sl))


def _s5_chunk(z, lam_re, lam_im, log_step, ct_re, ct_im, bt_re, bt_im, c_re, c_im, n_chunks):
    G, W, R = z.shape
    P, HG = lam_re.shape[1], S5_HG
    P2 = 2 * P
    gps = 2 if G % 2 == 0 else 1
    col = lambda a: a.reshape(G, P, 1)
    g3 = lambda s1, s2: pl.BlockSpec((gps, s1, s2), lambda g: (g, 0, 0))
    return pl.pallas_call(
        functools.partial(_s5_chunk_kernel, n_chunks=n_chunks),
        grid=(G // gps,),
        in_specs=[g3(P, 1), g3(P, 1), g3(1, 1),
                  g3(P, HG), g3(P, HG), g3(P, HG), g3(P, HG), g3(HG, P), g3(HG, P), g3(W, R)],
        out_specs=g3(W, R),
        out_shape=jax.ShapeDtypeStruct((G, W, R), F32),
        scratch_shapes=[pltpu.VMEM((gps, W, W), BF16), pltpu.VMEM((gps, P2, W), BF16),
                        pltpu.VMEM((gps, W, P2), BF16), pltpu.VMEM((gps, P2, 1), F32)],
        compiler_params=_cparams("arbitrary"),
        name="s5_chunk",
    )(col(lam_re), col(lam_im), log_step.reshape(G, 1, 1), ct_re, ct_im, bt_re, bt_im, c_re, c_im, z)


def _gelu_tanh(x):
    return 0.5 * x * (1.0 + jnp.tanh(0.7978845608028654 * (x + 0.044715 * (x * x * x))))


def _s5_glu_kernel(y_ref, u_ref, d_ref, w_ref, b_ref, o_ref):
    y = y_ref[...] + d_ref[...] * u_ref[...]
    y = _gelu_tanh(y)
    z = _dot(y.astype(BF16), w_ref[...]) + b_ref[...]
    o_ref[...] = (y * jax.nn.sigmoid(z)).astype(BF16)


def _s5_glu(y_ssm, proj, d, w_bf16, b, tm):
    T, W = y_ssm.shape
    return pl.pallas_call(
        _s5_glu_kernel,
        grid=(T // tm,),
        in_specs=[
            pl.BlockSpec((tm, W), lambda i: (i, 0)),
            pl.BlockSpec((tm, W), lambda i: (i, 0)),
            pl.BlockSpec((1, W), lambda i: (0, 0)),
            pl.BlockSpec((W, W), lambda i: (0, 0)),
            pl.BlockSpec((1, W), lambda i: (0, 0)),
        ],
        out_specs=pl.BlockSpec((tm, W), lambda i: (i, 0)),
        out_shape=jax.ShapeDtypeStruct((T, W), BF16),
        compiler_params=_cparams("arbitrary"),
        name="s5_glu",
    )(y_ssm, proj, d, w_bf16, b)


def _shift(z, prev_row):
    rolled = pltpu.roll(z, 1, 0)
    first = lax.broadcasted_iota(I32, (z.shape[0], 1), 0) == 0
    return jnp.where(first, prev_row, rolled)


RW_OPERANDS = ("r_t", "k_t", "a_t", "b_t", "k_h", "b_h", "v")
RW_LORA_PAD = 512
RW_PREP_PIECES = 14
RW_STAGES = 24


def _rwkv_chunk_kernel(zr_ref, zk_ref, zv_ref, zl_ref, mu_ref, mul_ref, w0_ref, a0_ref, kk_ref, ka_ref,
                       w2_ref, a2_ref, g2_ref, rk_ref, lnw_ref, lnb_ref,
                       o_ref, z_ref, ys_ref, qs_ref, vs_ref, gs_ref, car_ref, carl_ref, opnd_ref, ptot_ref):
    @pl.when(pl.program_id(1) == 0)
    def _():
        for ref in (z_ref, ys_ref, qs_ref, vs_ref, gs_ref, car_ref, carl_ref, opnd_ref, ptot_ref):
            ref[...] = jnp.zeros_like(ref)

    C = RW_CHUNK
    refs = (zr_ref, zk_ref, zv_ref, zl_ref, mu_ref, mul_ref, w0_ref, a0_ref, kk_ref, ka_ref,
            w2_ref, a2_ref, g2_ref, rk_ref, lnw_ref, lnb_ref,
            o_ref, z_ref, ys_ref, qs_ref, vs_ref, gs_ref, car_ref, carl_ref, opnd_ref, ptot_ref)
    for half in range(2):
        _rwkv_half_step(half, slice(half * C, (half + 1) * C), *refs)


def _rwkv_half_step(par, rows, zr_ref, zk_ref, zv_ref, zl_ref, mu_ref, mul_ref, w0_ref, a0_ref, kk_ref, ka_ref,
                    w2_ref, a2_ref, g2_ref, rk_ref, lnw_ref, lnb_ref,
                    o_ref, z_ref, ys_ref, qs_ref, vs_ref, gs_ref, car_ref, carl_ref, opnd_ref, ptot_ref):
    C, N = RW_CHUNK, RW_N
    NS = zr_ref.shape[0]
    H = zr_ref.shape[2] // N

    low = lax.broadcasted_iota(I32, (C, 2 * N), 1) < N
    inv_n = 1.0 / N

    def head_sums(t):
        s0 = jnp.sum(jnp.where(low, t, 0.0), axis=-1, keepdims=True)
        s1 = jnp.sum(jnp.where(low, 0.0, t), axis=-1, keepdims=True)
        return jnp.where(low, s0, s1)

    def tail():
        for s in range(NS):
            for p in range(H // 2):
                ps = slice(2 * p * N, 2 * (p + 1) * N)
                yp = ys_ref[s, :, ps]
                yc = yp - head_sums(yp) * inv_n
                var = head_sums(yc * yc) * inv_n
                yn = yc * lax.rsqrt(var + RW_GN_EPS) * lnw_ref[:, ps] + lnb_ref[:, ps]
                bonus = head_sums(qs_ref[par, s, :, ps]) * vs_ref[par, s, :, ps].astype(F32)
                o_ref[s, rows, ps] = ((yn + bonus) * gs_ref[par, s, :, ps].astype(F32)).astype(BF16)
                yield

    ri = lax.broadcasted_iota(I32, (C, C), 0)
    ci = lax.broadcasted_iota(I32, (C, C), 1)
    tril = ri >= ci
    stril = ri > ci
    eye = (ri == ci).astype(F32)
    both = jnp.concatenate([stril, tril], axis=0)
    prev = 1 - par
    ops = {}
    for s in range(NS):
        for h in range(H):
            cols = slice(h * N, (h + 1) * N)
            ops[(s, h)] = {name: functools.partial(lambda n, s, cols: opnd_ref[prev, n, s, :, cols], n, s, cols)
                           for n, name in enumerate(RW_OPERANDS)}
            ops[(s, h)]["p_tot"] = functools.partial(lambda s, cols: ptot_ref[prev, s, 0:1, cols], s, cols)
    def head_sums_all(t):
        return jnp.concatenate([head_sums(t[:, 2 * p * N:2 * (p + 1) * N]) for p in range(H // 2)], axis=1)

    def lerp(zz, prev_row, mu):
        return zz + (_shift(zz, prev_row) - zz) * mu

    def prepare():
        for s in range(NS):
            zr, zk, zv, zl = zr_ref[s, rows, :], zk_ref[s, rows, :], zv_ref[s, rows, :], zl_ref[s, rows, :]
            r_f = lerp(zr, car_ref[s, 0:1, :], mu_ref[0:1, :])
            yield
            k_raw = lerp(zk, car_ref[s, 1:2, :], mu_ref[1:2, :])
            yield
            v_f = lerp(zv, car_ref[s, 2:3, :], mu_ref[2:3, :])
            xl = lerp(zl, carl_ref[s, 0:1, :], mul_ref[...])
            car_ref[s, 0:1, :] = zr[C - 1:C, :]
            car_ref[s, 1:2, :] = zk[C - 1:C, :]
            car_ref[s, 2:3, :] = zv[C - 1:C, :]
            carl_ref[s, 0:1, :] = zl[C - 1:C, :]
            yield
            xwa, xg = xl[:, 0:LANES], xl[:, LANES:LANES + g2_ref.shape[0]]
            dw = _dot(jnp.tanh(xwa).astype(BF16), w2_ref[...])
            da = _dot(xwa.astype(BF16), a2_ref[...])
            g_f = _dot(jax.nn.sigmoid(xg).astype(BF16), g2_ref[...])
            yield
            zw = -(w0_ref[...] + dw)
            w_log = -(jnp.maximum(zw, 0.0) + jnp.log(1.0 + jnp.exp(-jnp.abs(zw)))) - 0.5
            yield
            a_sig = jax.nn.sigmoid(a0_ref[...] + da)
            kk = k_raw * kk_ref[...]
            yield
            kk = kk / jnp.maximum(jnp.sqrt(head_sums_all(kk * kk)), 1e-12)
            yield
            k_f = k_raw * (1.0 + (a_sig - 1.0) * ka_ref[...])
            b_f = kk * a_sig
            lw = -jnp.exp(w_log)
            yield
            cs = _dot_exact_lhs(tril.astype(BF16), lw)
            tot = cs[C - 1:C, :]
            yield
            store = lambda name, a: opnd_ref.__setitem__((par, RW_OPERANDS.index(name), s), a.astype(BF16))
            p_inc = jnp.exp(cs)
            store("r_t", r_f * p_inc)
            yield
            p_inv = jnp.exp(-cs)
            store("k_t", k_f * p_inv)
            store("b_t", b_f * p_inv)
            yield
            store("a_t", -kk * jnp.exp(cs - lw))
            yield
            p_rest = jnp.exp(tot - cs)
            store("k_h", k_f * p_rest)
            store("b_h", b_f * p_rest)
            yield
            v_b = v_f.astype(BF16)
            store("v", v_b)
            ptot_ref[par, s, 0:1, :] = jnp.exp(tot)
            qs_ref[par, s] = r_f * k_f * rk_ref[...]
            vs_ref[par, s] = v_b
            gs_ref[par, s] = g_f.astype(BF16)
            yield

    tail_pieces, prep_pieces = tail(), prepare()
    n_pieces = NS * (H // 2 + RW_PREP_PIECES)
    state = dict(ticks=0, done=0)

    def tick():
        state["ticks"] += 1
        due = -(-state["ticks"] * n_pieces // RW_STAGES)
        while state["done"] < due:
            state["done"] += 1
            if next(tail_pieces, "done") == "done":
                next(prep_pieces, None)

    def drain_tail():
        for _ in tail_pieces:
            state["done"] += 1

    _rwkv_heads(ops, stril, tril, both, eye, z_ref, ys_ref, tick, drain_tail)
    for _ in prep_pieces:
        pass


def _rwkv_heads(ops, stril, tril, both, eye, z_ref, ys_ref, tick, drain_tail):
    C, N = RW_CHUNK, RW_N
    hs = list(ops)

    def each(f):
        out = {h: f(h) for h in hs}
        tick()
        return out

    get = lambda name: (lambda h: ops[h][name]())
    a_t, r_t, b_t, k_t, k_h, b_h, p_tot = (get(n) for n in ("a_t", "r_t", "b_t", "k_t", "k_h", "b_h", "p_tot"))
    v = each(lambda h: ops[h]["v"]())
    ar = each(lambda h: jnp.concatenate([a_t(h), r_t(h)], axis=0).astype(BF16))
    m_b = each(lambda h: _dot_nt(ar[h], b_t(h).astype(BF16)))
    m_k = each(lambda h: _dot_nt(ar[h], k_t(h).astype(BF16)))
    l_ab = each(lambda h: jnp.where(stril, m_b[h][:C], 0.0).astype(BF16))
    m_rb = each(lambda h: jnp.where(tril, m_b[h][C:], 0.0).astype(BF16))
    lm_k = each(lambda h: jnp.where(both, m_k[h], 0.0).astype(BF16))
    lmv = each(lambda h: _dot(lm_k[h], v[h]))

    x = each(lambda h: jnp.concatenate([lmv[h][:C], a_t(h).astype(F32)], axis=1))
    y0 = each(lambda h: lmv[h][C:])
    hk = each(lambda h: _dot_tn(k_h(h).astype(BF16), v[h]))
    lp = l_ab
    step = 1
    while step < C:
        x = each(lambda h: x[h] + _dot(lp[h], x[h].astype(BF16)))
        step *= 2
        if step < C:
            lp = each(lambda h: _dot(lp[h], lp[h]).astype(BF16))
    xb = each(lambda h: x[h].astype(BF16))
    yq = each(lambda h: _dot(m_rb[h], xb[h]))
    gh = each(lambda h: _dot_tn(b_h(h).astype(BF16), xb[h]))

    z = each(lambda h: _split2(z_ref[h[0], h[1]]))
    qg = each(lambda h: jnp.concatenate([yq[h][:, N:] + r_t(h).astype(F32),
                                         gh[h][:, N:] + eye * p_tot(h)], axis=0).astype(BF16))
    qgz = each(lambda h: _dot(qg[h], z[h][0]) + _dot(qg[h], z[h][1]))
    yz = each(lambda h: qgz[h][:C])
    gz = each(lambda h: qgz[h][C:])
    drain_tail()
    for s, hd in hs:
        z_ref[s, hd] = (gh[s, hd][:, :N] + hk[s, hd]) + gz[s, hd]
        ys_ref[s, :, hd * N:(hd + 1) * N] = (yq[s, hd][:, :N] + y0[s, hd]) + yz[s, hd]


RW_SEQS_PER_STEP = 1


def _rwkv_chunk(proj, B, L, mu3, mul, w0, a0, k_k, k_a, w2p, a2p, g2p, r_k, ln_w, ln_b):
    T = B * L
    W = w0.shape[1]
    C = RW_CHUNK
    nc = L // C
    H = W // RW_N
    assert H % 2 == 0
    ns = RW_SEQS_PER_STEP if B % RW_SEQS_PER_STEP == 0 else 1
    lw = mul.shape[1]
    proj3 = proj.reshape(B, L, proj.shape[1])
    assert nc % 2 == 0
    np2 = nc // 2
    cols = lambda width, j: pl.BlockSpec((ns, 2 * C, width), lambda bi, i: (bi, jnp.minimum(i, np2 - 1), j))
    full = lambda a: pl.BlockSpec(a.shape, lambda bi, i: (0, 0))
    consts = (mu3, mul, w0, a0, k_k, k_a, w2p, a2p, g2p, r_k, ln_w, ln_b)
    out = pl.pallas_call(
        _rwkv_chunk_kernel,
        grid=(B // ns, np2 + 1),
        in_specs=[cols(W, 1), cols(W, 2), cols(W, 3), cols(lw, 4 * W // lw)] + [full(a) for a in consts],
        out_specs=pl.BlockSpec((ns, 2 * C, W), lambda bi, i: (bi, jnp.maximum(i - 1, 0), 0)),
        out_shape=jax.ShapeDtypeStruct((B, L, W), BF16),
        scratch_shapes=[pltpu.VMEM((ns, H, RW_N, RW_N), F32), pltpu.VMEM((ns, C, W), F32),
                        pltpu.VMEM((2, ns, C, W), F32), pltpu.VMEM((2, ns, C, W), BF16),
                        pltpu.VMEM((2, ns, C, W), BF16),
                        pltpu.VMEM((ns, 8, W), F32), pltpu.VMEM((ns, 8, lw), F32),
                        pltpu.VMEM((2, len(RW_OPERANDS), ns, C, W), BF16), pltpu.VMEM((2, ns, 8, W), F32)],
        compiler_params=_cparams("arbitrary", "arbitrary"),
        name="rwkv_chunk",
    )(proj3, proj3, proj3, proj3, *consts)
    return out.reshape(T, W)


def _first_index_of_max(vals, lane, valid):
    neg = jnp.float32(-jnp.inf)
    masked = jnp.where(valid, vals, neg)
    m = jnp.max(masked, axis=-1, keepdims=True)
    idx = jnp.min(jnp.where(valid & (masked == m), lane, LANES), axis=-1, keepdims=True)
    return m, idx


def _outproj_kernel(s5_ref, rw_ref, x_ref, wt_ref, wb_ref, g_ref, wr_ref, br_ref,
                    h_ref, hn_ref, rec_ref):
    h = x_ref[...] + (_dot(s5_ref[...], wt_ref[...]) + _dot(rw_ref[...], wb_ref[...]))
    h_ref[...] = h
    ms = jnp.mean(h * h, axis=-1, keepdims=True)
    hn = h * lax.rsqrt(ms + RMS_EPS) * g_ref[...]
    hn_ref[...] = _pack_halves(hn).reshape(hn_ref.shape)

    hh, hl = _split2(hn)
    wh, wl = wr_ref[:, :LANES], wr_ref[:, LANES:]
    logits = _dot(hh, wh) + (_dot(hh, wl) + _dot(hl, wh)) + br_ref[...]
    lane = lax.broadcasted_iota(I32, logits.shape, 1)
    is_grp = (lane >= N_EXPERTS) & (lane < N_EXPERTS + N_GROUPS)
    gmax, gidx = _first_index_of_max(logits, lane, is_grp)
    gsum = jnp.sum(jnp.where(is_grp, jnp.exp(logits - gmax), 0.0), axis=-1, keepdims=True)
    p_grp = 1.0 / gsum
    grp = gidx - N_EXPERTS
    in_grp = (lane >= grp * EPG) & (lane < (grp + 1) * EPG)
    m1, i1 = _first_index_of_max(logits, lane, in_grp)
    m2, i2 = _first_index_of_max(logits, lane, in_grp & (lane != i1))
    e = jnp.exp(m2 - m1)
    g1 = p_grp / (1.0 + e)
    g2 = p_grp * e / (1.0 + e)
    rec = jnp.where(lane == 0, i1.astype(F32),
          jnp.where(lane == 1, i2.astype(F32),
          jnp.where(lane == 2, g1, jnp.where(lane == 3, g2, 0.0))))
    rec_ref[...] = rec


def _outproj(s5o, rwo, x2, w_top, w_bot, g, wr, b_route, tm):
    T, D = x2.shape
    W = s5o.shape[1]
    full = lambda a: pl.BlockSpec(a.shape, lambda i: (0, 0))
    return pl.pallas_call(
        _outproj_kernel,
        grid=(T // tm,),
        in_specs=[pl.BlockSpec((tm, W), lambda i: (i, 0)), pl.BlockSpec((tm, W), lambda i: (i, 0)),
                  pl.BlockSpec((tm, D), lambda i: (i, 0)),
                  full(w_top), full(w_bot), full(g), full(wr), full(b_route)],
        out_specs=[pl.BlockSpec((tm, D), lambda i: (i, 0)), pl.BlockSpec((tm, 1, D // 2), lambda i: (i, 0, 0)),
                   pl.BlockSpec((tm, LANES), lambda i: (i, 0))],
        out_shape=[jax.ShapeDtypeStruct((T, D), F32), jax.ShapeDtypeStruct((T, 1, D // 2), U32),
                   jax.ShapeDtypeStruct((T, LANES), F32)],
        compiler_params=_cparams("arbitrary"),
        name="outproj_route",
    )(s5o, rwo, x2, w_top, w_bot, g, wr, b_route)


def _onehots(rec, lane):
    oh0 = (lane == rec[:, 0:1].astype(I32)).astype(F32)
    oh1 = (lane == rec[:, 1:2].astype(I32)).astype(F32)
    return oh0, oh1


def _rank_kernel(rec_ref, rank_ref, cnt_ref, base_ref):
    tb = rec_ref.shape[0]

    @pl.when(pl.program_id(0) == 0)
    def _():
        base_ref[...] = jnp.zeros_like(base_ref)

    lane = lax.broadcasted_iota(I32, (tb, LANES), 1)
    oh0, oh1 = _onehots(rec_ref[...], lane)
    both = oh0 + oh1
    ri = lax.broadcasted_iota(I32, (tb, tb), 0)
    ci = lax.broadcasted_iota(I32, (tb, tb), 1)
    before = _dot((ri > ci).astype(BF16), both.astype(BF16)) + base_ref[0:1, :]
    rank0 = jnp.sum(oh0 * before, axis=-1, keepdims=True)
    rank1 = jnp.sum(oh1 * before, axis=-1, keepdims=True)
    rank_ref[...] = jnp.where(lane == 0, rank0, jnp.where(lane == 1, rank1, 0.0))
    total = base_ref[0:1, :] + jnp.sum(both, axis=0, keepdims=True)
    base_ref[0:1, :] = total
    cnt_ref[...] = jnp.broadcast_to(total, cnt_ref.shape)


def _rank(rec, tb):
    T = rec.shape[0]
    return pl.pallas_call(
        _rank_kernel,
        grid=(T // tb,),
        in_specs=[pl.BlockSpec((tb, LANES), lambda i: (i, 0))],
        out_specs=[pl.BlockSpec((tb, LANES), lambda i: (i, 0)), pl.BlockSpec((8, LANES), lambda i: (0, 0))],
        out_shape=[jax.ShapeDtypeStruct((T, LANES), F32), jax.ShapeDtypeStruct((8, LANES), F32)],
        scratch_shapes=[pltpu.VMEM((8, LANES), F32)],
        compiler_params=_cparams("arbitrary"),
        name="moe_rank",
    )(rec)


def _padded_starts(cnt):
    padded = jnp.ceil(cnt * (1.0 / MOE_BLOCK)) * MOE_BLOCK
    ri = lax.broadcasted_iota(I32, (LANES, LANES), 0)
    ci = lax.broadcasted_iota(I32, (LANES, LANES), 1)
    p8 = jnp.broadcast_to(padded, (8, LANES))
    pend = _dot_exact_lhs_rhs(p8, (ri <= ci).astype(BF16))[0:1, :]
    return pend - padded, pend


def _dot_exact_lhs_rhs(a, b_bf16):
    ah, am, al = _split3(a)
    return _dot(ah, b_bf16) + (_dot(am, b_bf16) + _dot(al, b_bf16))


def _dest_kernel(rec_ref, rank_ref, cnt_ref, dest_ref, blk_ref):
    tb = rec_ref.shape[0]
    cnt = cnt_ref[0:1, :]
    pstart, pend = _padded_starts(cnt)
    lane = lax.broadcasted_iota(I32, (tb, LANES), 1)
    oh0, oh1 = _onehots(rec_ref[...], lane)
    rank = rank_ref[...]
    d0 = jnp.sum(oh0 * pstart, axis=-1, keepdims=True) + rank[:, 0:1]
    d1 = jnp.sum(oh1 * pstart, axis=-1, keepdims=True) + rank[:, 1:2]
    dest_ref[...] = jnp.where(lane == 0, d0, jnp.where(lane == 1, d1, 0.0)).astype(I32)

    @pl.when(pl.program_id(0) == 0)
    def _():
        nb = blk_ref.shape[0]
        blane = lax.broadcasted_iota(I32, (nb, LANES), 1)
        bstart = (lax.broadcasted_iota(I32, (nb, 1), 0) * MOE_BLOCK).astype(F32)
        is_e = blane < N_EXPERTS
        bexp = jnp.sum(jnp.where(is_e & (pend <= bstart), 1.0, 0.0), axis=-1, keepdims=True)
        bexp = jnp.minimum(bexp, N_EXPERTS - 1.0)
        nact = jnp.max(jnp.where(is_e, pend, 0.0), axis=-1, keepdims=True) * (1.0 / MOE_BLOCK)
        lastblk = jnp.where(is_e & (pend > pstart), pend - MOE_BLOCK, -1.0)
        brow = lax.broadcasted_iota(I32, (nb, 1), 0)
        out = jnp.where(blane == 0, bexp, jnp.where(blane == 1, nact, 0.0))
        out = jnp.where(brow == nb - 1, lastblk, out)
        blk_ref[...] = out.astype(I32)


def _dest(rec, rank, cnt, tb, nb_rows):
    T = rec.shape[0]
    return pl.pallas_call(
        _dest_kernel,
        grid=(T // tb,),
        in_specs=[pl.BlockSpec((tb, LANES), lambda i: (i, 0)), pl.BlockSpec((tb, LANES), lambda i: (i, 0)),
                  pl.BlockSpec((8, LANES), lambda i: (0, 0))],
        out_specs=[pl.BlockSpec((tb, LANES), lambda i: (i, 0)), pl.BlockSpec((nb_rows, LANES), lambda i: (0, 0))],
        out_shape=[jax.ShapeDtypeStruct((T, LANES), I32), jax.ShapeDtypeStruct((nb_rows, LANES), I32)],
        compiler_params=_cparams("arbitrary"),
        name="moe_dest",
    )(rec, rank, cnt)


def _scatter_kernel(pad_ref, na_ref, dest_ref, hn_ref, xs_ref, zbuf_ref, idx_ref, sa_ref, sb_ref,
                    sem_a, sem_b, isem, zsem, *, tb, n_blocks, n_tail, n_steps):
    i = pl.program_id(0)

    @pl.when(i == 0)
    def _():
        zbuf_ref[...] = jnp.zeros_like(zbuf_ref)

        def zero_rows(start):
            return pltpu.make_async_copy(zbuf_ref, xs_ref.at[pl.ds(start, MOE_BLOCK)], zsem)

        def pad_start(e, c):
            @pl.when(pad_ref[e] >= 0)
            def _():
                zero_rows(pad_ref[e]).start()
            return c

        def pad_wait(e, c):
            @pl.when(pad_ref[e] >= 0)
            def _():
                zero_rows(pad_ref[e]).wait()
            return c

        def tail_start(k, c):
            @pl.when(na_ref[0] + k < n_blocks)
            def _():
                zero_rows((na_ref[0] + k) * MOE_BLOCK).start()
            return c

        def tail_wait(k, c):
            @pl.when(na_ref[0] + k < n_blocks)
            def _():
                zero_rows((na_ref[0] + k) * MOE_BLOCK).wait()
            return c

        lax.fori_loop(0, N_EXPERTS, pad_start, 0)
        lax.fori_loop(0, n_tail, tail_start, 0)
        lax.fori_loop(0, N_EXPERTS, pad_wait, 0)
        lax.fori_loop(0, n_tail, tail_wait, 0)

    def idx_copy(step, slot):
        return pltpu.make_async_copy(dest_ref.at[step], idx_ref.at[pl.ds(slot * (2 * tb), 2 * tb)], isem)

    @pl.when(i == 0)
    def _():
        idx_copy(0, 0).start()

    slot = i % 2
    idx_copy(i, slot).wait()

    @pl.when(i + 1 < n_steps)
    def _():
        idx_copy(i + 1, 1 - slot).start()

    hb = tb // 2
    for half, (buf, sem) in enumerate(((sa_ref, sem_a), (sb_ref, sem_b))):
        def wait_rows(buf=buf, sem=sem):
            for _ in range(2):
                pltpu.make_async_copy(buf, xs_ref.at[pl.ds(0, hb)], sem).wait()

        @pl.when(i > 0)
        def _():
            wait_rows()

        buf[...] = hn_ref[half * hb:(half + 1) * hb]

        def issue(t, c, buf=buf, sem=sem, half=half):
            for s in range(2):
                d = idx_ref[slot * (2 * tb) + 2 * (half * hb + t) + s]
                pltpu.make_async_copy(buf.at[pl.ds(t, 1)], xs_ref.at[pl.ds(d, 1)], sem).start(priority=s)
            return c

        lax.fori_loop(0, hb, issue, 0, unroll=8)

    @pl.when(i == n_steps - 1)
    def _():
        for buf, sem in ((sa_ref, sem_a), (sb_ref, sem_b)):
            for _ in range(2):
                pltpu.make_async_copy(buf, xs_ref.at[pl.ds(0, hb)], sem).wait()


def _scatter(lastblk, nact, dest2, hn, tb, n_blocks):
    T, _, D = hn.shape
    n_tail = n_blocks - (-(-2 * T // MOE_BLOCK))
    hb = tb // 2
    return pl.pallas_call(
        functools.partial(_scatter_kernel, tb=tb, n_blocks=n_blocks, n_tail=n_tail, n_steps=T // tb),
        grid_spec=pltpu.PrefetchScalarGridSpec(
            num_scalar_prefetch=2, grid=(T // tb,),
            in_specs=[pl.BlockSpec(memory_space=pl.ANY),
                      pl.BlockSpec((tb, 1, D), lambda i, pad, na: (i, 0, 0))],
            out_specs=pl.BlockSpec(memory_space=pl.ANY),
            scratch_shapes=[pltpu.VMEM((MOE_BLOCK, 1, D), hn.dtype), pltpu.SMEM((4 * tb,), I32),
                            pltpu.VMEM((hb, 1, D), hn.dtype), pltpu.VMEM((hb, 1, D), hn.dtype),
                            pltpu.SemaphoreType.DMA, pltpu.SemaphoreType.DMA,
                            pltpu.SemaphoreType.DMA, pltpu.SemaphoreType.DMA]),
        out_shape=jax.ShapeDtypeStruct((n_blocks * MOE_BLOCK, 1, D), hn.dtype),
        compiler_params=_cparams("arbitrary"),
        name="moe_scatter",
    )(lastblk, nact, dest2, hn)


W_SLOTS = 2


def _expert_kernel(be_ref, na_ref, x_ref, wg_hbm, wu_hbm, wd_hbm, y_ref,
                   wg_buf, wu_buf, wd_buf, wgb_ref, wub_ref, wdb_ref, x2_ref, elist_ref, state_ref, sems):
    j = pl.program_id(0)
    na = na_ref[0]

    def weight_copies(e, slot):
        return [pltpu.make_async_copy(src.at[e], buf.at[slot], sems.at[slot, n])
                for n, (src, buf) in enumerate(((wg_hbm, wg_buf), (wu_hbm, wu_buf), (wd_hbm, wd_buf)))]

    @pl.when(j == 0)
    def _():
        def scan(k, n):
            is_new = (k == 0) | (be_ref[k] != be_ref[jnp.maximum(k - 1, 0)])

            @pl.when(is_new)
            def _():
                elist_ref[n] = be_ref[k]
            return n + is_new.astype(I32)

        n_exp = lax.fori_loop(0, na, scan, 0)
        state_ref[0] = 0
        state_ref[1] = n_exp
        for n in range(W_SLOTS):
            @pl.when(n < n_exp)
            def _():
                for cp in weight_copies(elist_ref[n], n):
                    cp.start()

    @pl.when(j < na)
    def _():
        @pl.when((j == 0) | (be_ref[j] != be_ref[jnp.maximum(j - 1, 0)]))
        def _():
            n = state_ref[0]
            slot = n % W_SLOTS
            for cp in weight_copies(elist_ref[n], slot):
                cp.wait()
            wgb_ref[...] = wg_buf[slot].astype(BF16)
            wub_ref[...] = wu_buf[slot].astype(BF16)
            wdb_ref[...] = wd_buf[slot].astype(BF16)

            @pl.when(n + W_SLOTS < state_ref[1])
            def _():
                for cp in weight_copies(elist_ref[n + W_SLOTS], slot):
                    cp.start()

            state_ref[0] = n + 1

        x2_ref[...] = x_ref[...].reshape(x2_ref.shape)
        lo, hi = _unpack_halves(x2_ref[...])
        lo, hi = lo.astype(BF16), hi.astype(BF16)
        half = lo.shape[1]
        hg = _dot(lo, wgb_ref[:half, :]) + _dot(hi, wgb_ref[half:, :])
        hu = _dot(lo, wub_ref[:half, :]) + _dot(hi, wub_ref[half:, :])
        act = (hg * jax.nn.sigmoid(hg)) * hu
        y = _dot(act.astype(BF16), wdb_ref[...])
        y_ref[...] = _pack_halves(y).reshape(y_ref.shape)

    @pl.when(j >= na)
    def _():
        y_ref[...] = jnp.zeros_like(y_ref)


def _experts(blk_exp, nact, xs, w_gate, w_up, w_down, n_blocks):
    DP = xs.shape[2]
    D = 2 * DP
    rows = n_blocks * MOE_BLOCK
    DE = w_gate.shape[2]
    blk = lambda j, be, na: (jnp.minimum(j, na[0] - 1), 0, 0)
    hbm = pl.BlockSpec(memory_space=pl.ANY)
    return pl.pallas_call(
        _expert_kernel,
        grid_spec=pltpu.PrefetchScalarGridSpec(
            num_scalar_prefetch=2, grid=(n_blocks,),
            in_specs=[pl.BlockSpec((MOE_BLOCK, 1, DP), blk), hbm, hbm, hbm],
            out_specs=pl.BlockSpec((MOE_BLOCK, 1, DP), lambda j, be, na: (j, 0, 0)),
            scratch_shapes=[pltpu.VMEM((W_SLOTS, D, DE), F32), pltpu.VMEM((W_SLOTS, D, DE), F32),
                            pltpu.VMEM((W_SLOTS, DE, D), F32),
                            pltpu.VMEM((D, DE), BF16), pltpu.VMEM((D, DE), BF16), pltpu.VMEM((DE, D), BF16),
                            pltpu.VMEM((MOE_BLOCK, DP), U32), pltpu.SMEM((N_EXPERTS,), I32), pltpu.SMEM((2,), I32),
                            pltpu.SemaphoreType.DMA((W_SLOTS, 3))]),
        out_shape=jax.ShapeDtypeStruct((rows, 1, DP), U32),
        compiler_params=_cparams("arbitrary"),
        name="moe_experts",
    )(blk_exp, nact, xs, w_gate, w_up, w_down)


def _combine_kernel(dest_ref, ys_ref, h_ref, rec_ref, g_ref, o_ref, idx_ref, ya0_ref, ya1_ref, yb0_ref, yb1_ref,
                    y2_ref, sem_a, sem_b, isem, *, normalize, n_steps):
    i = pl.program_id(0)
    tb = h_ref.shape[0]
    hb = tb // 2
    halves = ((ya0_ref, ya1_ref, sem_a), (yb0_ref, yb1_ref, sem_b))

    def idx_copy(step, slot):
        return pltpu.make_async_copy(dest_ref.at[step], idx_ref.at[pl.ds(slot * (2 * tb), 2 * tb)], isem)

    def issue(half, slot):
        bufs, sem = halves[half][:2], halves[half][2]

        def body(t, c):
            for s in range(2):
                d = idx_ref[slot * (2 * tb) + 2 * (half * hb + t) + s]
                pltpu.make_async_copy(ys_ref.at[pl.ds(d, 1)], bufs[s].at[pl.ds(t, 1)], sem).start(priority=s)
            return c

        lax.fori_loop(0, hb, body, 0, unroll=8)

    @pl.when(i == 0)
    def _():
        first = idx_copy(0, 0)
        first.start()
        first.wait()
        issue(0, 0)
        issue(1, 0)
        if n_steps > 1:
            idx_copy(1, 1).start()

    nslot = (i + 1) % 2

    @pl.when(i + 1 < n_steps)
    def _():
        idx_copy(i + 1, nslot).wait()

    rec = rec_ref[...]
    for half, (y0_ref, y1_ref, sem) in enumerate(halves):
        rows = slice(half * hb, (half + 1) * hb)
        pltpu.make_async_copy(ys_ref.at[pl.ds(0, hb)], y0_ref, sem).wait()
        pltpu.make_async_copy(ys_ref.at[pl.ds(0, hb)], y1_ref, sem).wait()
        y2_ref[...] = y0_ref[...].reshape(y2_ref.shape)
        h = h_ref[rows, :] + rec[rows, 2:3] * jnp.concatenate(_unpack_halves(y2_ref[...]), axis=1)
        y2_ref[...] = y1_ref[...].reshape(y2_ref.shape)
        h = h + rec[rows, 3:4] * jnp.concatenate(_unpack_halves(y2_ref[...]), axis=1)
        if normalize:
            ms = jnp.mean(h * h, axis=-1, keepdims=True)
            h = h * lax.rsqrt(ms + RMS_EPS) * g_ref[...]
        o_ref[rows, :] = h

        @pl.when(i + 1 < n_steps)
        def _():
            issue(half, nslot)

    @pl.when(i + 2 < n_steps)
    def _():
        idx_copy(i + 2, i % 2).start()


def _combine(dest2, ys, h, rec, g, tb, normalize):
    T, D = h.shape
    hb = tb // 2
    return pl.pallas_call(
        functools.partial(_combine_kernel, normalize=normalize, n_steps=T // tb),
        grid=(T // tb,),
        in_specs=[pl.BlockSpec(memory_space=pl.ANY), pl.BlockSpec(memory_space=pl.ANY),
                  pl.BlockSpec((tb, D), lambda i: (i, 0)), pl.BlockSpec((tb, LANES), lambda i: (i, 0)),
                  pl.BlockSpec((1, D), lambda i: (0, 0))],
        out_specs=pl.BlockSpec((tb, D), lambda i: (i, 0)),
        out_shape=jax.ShapeDtypeStruct((T, D), F32),
        scratch_shapes=[pltpu.SMEM((4 * tb,), I32)] + [pltpu.VMEM((hb, 1, D // 2), U32)] * 4 +
                       [pltpu.VMEM((hb, D // 2), U32)] + [pltpu.SemaphoreType.DMA] * 3,
        compiler_params=_cparams("arbitrary"),
        name="moe_combine",
    )(dest2, ys, h, rec, g)


def _pad_cols(a, n):
    return jnp.pad(a, ((0, 0), (0, n - a.shape[1])))


def _pad_rows(a, n, at=0):
    return jnp.pad(a, ((at, n - a.shape[0] - at), (0, 0)))


def _layer(x2, B, L, p):
    T, D = x2.shape
    W = p["s5_d"].shape[0]
    G, P = p["s5_lambda_re"].shape
    HG = W // G
    dl, al, gl = p["rwkv_w2"].shape[0], p["rwkv_a2"].shape[0], p["rwkv_g2"].shape[0]
    H = W // RW_N

    assert dl + al == LANES and gl <= RW_LORA_PAD - LANES
    w_in_r = _pad_cols(p["w_in"].astype(BF16), 4 * W + RW_LORA_PAD)
    tm_in = min(1024, T)
    proj = _inproj(x2, p["norm_mix_g"].reshape(1, D), w_in_r, tm_in, 1152)

    C = S5_CHUNK
    nc = L // C
    ct_re = jnp.swapaxes(p["s5_c_re"], 1, 2)
    ct_im = jnp.swapaxes(p["s5_c_im"], 1, 2)
    rb = min(LANES, B * nc)
    z = _s5_in(proj.reshape(B * nc, C, proj.shape[1]), G, rb)
    y_t = _s5_chunk(z, p["s5_lambda_re"], p["s5_lambda_im"], p["s5_log_step"],
                    ct_re, ct_im, p["s5_b_re"], p["s5_b_im"], p["s5_c_re"], p["s5_c_im"], nc)
    y_ssm = _s5_out(y_t, rb).reshape(T, W)
    tm = min(512, T)
    s5_out = _s5_glu(y_ssm, proj, p["s5_d"].reshape(1, W), p["s5_w_glu"].astype(BF16),
                     p["s5_b_glu"].reshape(1, W), tm)

    mu = p["rwkv_mu"]
    mu3 = _pad_rows(mu[:3 * W].reshape(3, W), 8)
    mul = _pad_cols(mu[None, 3 * W:], RW_LORA_PAD)
    w2p = _pad_rows(p["rwkv_w2"], LANES).astype(BF16)
    a2p = _pad_rows(p["rwkv_a2"], LANES, at=dl).astype(BF16)
    g2p = _pad_rows(p["rwkv_g2"], -(-gl // LANES) * LANES).astype(BF16)
    row = lambda a: a.reshape(1, W)
    rw_out = _rwkv_chunk(proj, B, L, mu3, mul, row(p["rwkv_w0"]), row(p["rwkv_a0"]), row(p["rwkv_k_k"]),
                         row(p["rwkv_k_a"]), w2p, a2p, g2p, row(p["rwkv_r_k"]), row(p["rwkv_ln_w"]),
                         row(p["rwkv_ln_b"]))

    w_out = p["w_out"].astype(BF16)
    w_route = jnp.concatenate([p["w_route_exp"], p["w_route_grp"]], axis=1)
    w_route = _pad_cols(w_route, LANES)
    wr_hi = w_route.astype(BF16)
    wr_lo = (w_route - wr_hi.astype(F32)).astype(BF16)
    wr = jnp.concatenate([wr_hi, wr_lo], axis=1)
    b_route = _pad_cols(jnp.concatenate([p["b_route_exp"], p["b_route_grp"]])[None, :], LANES)
    tm_o = min(512, T)
    h, hn, rec = _outproj(s5_out, rw_out, x2, w_out[:W], w_out[W:], p["norm_ffn_g"].reshape(1, D),
                          wr, b_route, tm_o)

    tb = min(1024, T)
    rank, cnt = _rank(rec, tb)
    n_blocks = -(-2 * T // MOE_BLOCK) + N_EXPERTS
    nb_rows = -(-(n_blocks + 1) // 8) * 8
    dest, blk = _dest(rec, rank, cnt, tb, nb_rows)
    blk_exp = blk[:n_blocks, 0]
    nact = blk[0:1, 1]
    lastblk = blk[nb_rows - 1, :N_EXPERTS]
    ts = min(256, T)
    dest2 = dest[:, :2].reshape(T // ts, 2 * ts)

    xs = _scatter(lastblk, nact, dest2, hn, ts, n_blocks)
    ys = _experts(blk_exp, nact, xs, p["w_gate"], p["w_up"], p["w_down"], n_blocks)
    return dest2, ys, h, rec, ts


def kernel(x, norm_mix_g, w_in, s5_lambda_re, s5_lambda_im, s5_log_step, s5_b_re, s5_b_im, s5_c_re, s5_c_im, s5_d, s5_w_glu, s5_b_glu, rwkv_mu, rwkv_w0, rwkv_w2, rwkv_a0, rwkv_a2, rwkv_g2, rwkv_k_k, rwkv_k_a, rwkv_r_k, rwkv_ln_w, rwkv_ln_b, w_out, norm_ffn_g, w_route_grp, b_route_grp, w_route_exp, b_route_exp, w_gate, w_up, w_down, norm_final_g):
    B, L, D = x.shape
    params = dict(
        norm_mix_g=norm_mix_g, w_in=w_in, s5_lambda_re=s5_lambda_re, s5_lambda_im=s5_lambda_im,
        s5_log_step=s5_log_step, s5_b_re=s5_b_re, s5_b_im=s5_b_im, s5_c_re=s5_c_re, s5_c_im=s5_c_im,
        s5_d=s5_d, s5_w_glu=s5_w_glu, s5_b_glu=s5_b_glu, rwkv_mu=rwkv_mu, rwkv_w0=rwkv_w0, rwkv_w2=rwkv_w2,
        rwkv_a0=rwkv_a0, rwkv_a2=rwkv_a2, rwkv_g2=rwkv_g2, rwkv_k_k=rwkv_k_k, rwkv_k_a=rwkv_k_a,
        rwkv_r_k=rwkv_r_k, rwkv_ln_w=rwkv_ln_w, rwkv_ln_b=rwkv_ln_b, w_out=w_out, norm_ffn_g=norm_ffn_g,
        w_route_grp=w_route_grp, b_route_grp=b_route_grp, w_route_exp=w_route_exp, b_route_exp=b_route_exp,
        w_gate=w_gate, w_up=w_up, w_down=w_down)
    depth = norm_mix_g.shape[0]
    h2 = x.reshape(B * L, D)
    for l in range(depth):
        p = {k_: v_[l] for k_, v_ in params.items()}
        dest2, ys, h, rec, ts = _layer(h2, B, L, p)
        h2 = _combine(dest2, ys, h, rec, norm_final_g.reshape(1, D), ts, normalize=(l == depth - 1))
    return h2.reshape(B, L, D)
```

```python
import functools

import jax
import jax.numpy as jnp
from jax import lax
from jax.experimental import pallas as pl
from jax.experimental.pallas import tpu as pltpu

F32 = jnp.float32
BF16 = jnp.bfloat16
I32 = jnp.int32

RMS_EPS = 1e-6
S5_CHUNK = 64
S5_HG = 16
S5_P = 64
RW_N = 64
RW_CHUNK = 64
RW_GN_EPS = 64e-5
N_GROUPS = 8
EPG = 8
N_EXPERTS = 64
MOE_BLOCK = 256
LANES = 128
VMEM_LIMIT = 56 * 1024 * 1024


def _cparams(*sem, **kw):
    return pltpu.CompilerParams(dimension_semantics=sem, vmem_limit_bytes=VMEM_LIMIT, **kw)


def _split2(x):
    hi = x.astype(BF16)
    lo = (x - hi.astype(F32)).astype(BF16)
    return hi, lo


def _split3(x):
    hi = x.astype(BF16)
    r = x - hi.astype(F32)
    mid = r.astype(BF16)
    lo = (r - mid.astype(F32)).astype(BF16)
    return hi, mid, lo


def _dot(a, b):
    return jnp.dot(a, b, preferred_element_type=F32)


def _dot_nt(a, b):
    return lax.dot_general(a, b, (((1,), (1,)), ((), ())), preferred_element_type=F32)


def _dot_tn(a, b):
    return lax.dot_general(a, b, (((0,), (0,)), ((), ())), preferred_element_type=F32)


def _dot_x3(a, b):
    ah, al = _split2(a)
    bh, bl = _split2(b)
    return _dot(ah, bh) + (_dot(ah, bl) + _dot(al, bh))


def _dot_exact_lhs(a_bf16, b):
    bh, bm, bl = _split3(b)
    return _dot(a_bf16, bh) + (_dot(a_bf16, bm) + _dot(a_bf16, bl))


def _cmul(ar, ai, br, bi):
    return ar * br - ai * bi, ar * bi + ai * br


U32 = jnp.uint32


def _pack_halves(x):
    half = x.shape[1] // 2
    lo = lax.bitcast_convert_type(x[:, :half].astype(BF16).astype(F32), U32)
    hi = lax.bitcast_convert_type(x[:, half:].astype(BF16).astype(F32), U32)
    return (lo >> 16) | hi


def _unpack_halves(p):
    lo = lax.bitcast_convert_type(p << 16, F32)
    hi = lax.bitcast_convert_type(p & jnp.uint32(0xFFFF0000), F32)
    return lo, hi


def _inproj_kernel(x_ref, g_ref, w_ref, o_ref, hn_ref):
    @pl.when(pl.program_id(1) == 0)
    def _():
        x = x_ref[...]
        ms = jnp.mean(x * x, axis=-1, keepdims=True)
        hn_ref[...] = (x * lax.rsqrt(ms + RMS_EPS) * g_ref[...]).astype(BF16)

    o_ref[...] = _dot(hn_ref[...], w_ref[...])


def _inproj(x2, g, w_bf16, tm, tn):
    T, D = x2.shape
    N = w_bf16.shape[1]
    return pl.pallas_call(
        _inproj_kernel,
        grid=(T // tm, N // tn),
        in_specs=[
            pl.BlockSpec((tm, D), lambda i, j: (i, 0)),
            pl.BlockSpec((1, D), lambda i, j: (0, 0)),
            pl.BlockSpec((D, tn), lambda i, j: (0, j)),
        ],
        out_specs=pl.BlockSpec((tm, tn), lambda i, j: (i, j)),
        out_shape=jax.ShapeDtypeStruct((T, N), F32),
        scratch_shapes=[pltpu.VMEM((tm, D), BF16)],
        compiler_params=_cparams("arbitrary", "arbitrary"),
        name="inproj",
    )(x2, g, w_bf16)


def _binpow(ar, ai, expo, nbits):
    pr = jnp.ones(expo.shape, F32)
    pi = jnp.zeros(expo.shape, F32)
    sr, si = ar, ai
    for bit in range(nbits):
        m = ((expo >> bit) & 1) == 1
        nr, ni = _cmul(pr, pi, sr, si)
        pr = jnp.where(m, nr, pr)
        pi = jnp.where(m, ni, pi)
        if bit + 1 < nbits:
            sr, si = _cmul(sr, si, sr, si)
    return pr, pi


def _frame_powers(ar, ai, reverse, plus_one):
    C, HG, P = S5_CHUNK, S5_HG, ar.shape[0]
    fpt = LANES // HG
    j = lax.broadcasted_iota(I32, (P, LANES), 1) // HG
    inner_r, inner_i = _binpow(ar, ai, (fpt - 1 - j) if reverse else j, (fpt - 1).bit_length())
    sr, si = ar, ai
    for _ in range(fpt.bit_length() - 1):
        sr, si = _cmul(sr, si, sr, si)
    outer = [(ar, ai) if plus_one else (jnp.ones_like(ar), jnp.zeros_like(ai))]
    for _ in range(C // fpt - 1):
        outer.append(_cmul(outer[-1][0], outer[-1][1], sr, si))
    if reverse:
        outer = outer[::-1]
    tiles = [_cmul(inner_r, inner_i, o_r, o_i) for o_r, o_i in outer]
    return (jnp.concatenate([t[0] for t in tiles], axis=1), jnp.concatenate([t[1] for t in tiles], axis=1))


def _zoh(lr, li, dt):
    mag = jnp.exp(lr * dt)
    ang = li * dt
    ar, ai = mag * jnp.cos(ang), mag * jnp.sin(ang)
    den = lr * lr + li * li
    nr, ni = ar - 1.0, ai
    fr = (nr * lr + ni * li) / den
    fi = (ni * lr - nr * li) / den
    return ar, ai, fr, fi


def _s5_ops_group(lrc_ref, lic_ref, dt_ref, ct_re_ref, ct_im_ref, bt_re_ref, bt_im_ref, c_re_ref, c_im_ref,
                  m_ref, wbt_ref, wct_ref, a64_ref):
    C, HG, P = S5_CHUNK, S5_HG, S5_P
    W = C * HG
    dt = jnp.exp(dt_ref[...])
    ar_c, ai_c, fr_c, fi_c = _zoh(lrc_ref[...], lic_ref[...], dt)

    pr, pi = _frame_powers(ar_c, ai_c, False, True)
    qr, qi = _frame_powers(ar_c, ai_c, True, False)

    sel = (lax.broadcasted_iota(I32, (HG, W), 1) % HG == lax.broadcasted_iota(I32, (HG, W), 0)).astype(BF16)
    tile = lambda a: _dot_exact_lhs_rhs(a, sel)

    c1_re, c1_im = _cmul(tile(ct_re_ref[...]), tile(ct_im_ref[...]), pr, pi)
    wct_ref[...] = jnp.concatenate([c1_re, -c1_im], axis=0).T.astype(BF16)

    bb_re, bb_im = _cmul(bt_re_ref[...], bt_im_ref[...], fr_c, fi_c)
    ab_re, ab_im = _cmul(tile(bb_re), tile(bb_im), qr, qi)
    wbt_ref[:P, :] = ab_re.astype(BF16)
    wbt_ref[P:, :] = ab_im.astype(BF16)

    cr, ci = ar_c, ai_c
    for _ in range(C.bit_length() - 1):
        cr, ci = _cmul(cr, ci, cr, ci)
    a64_ref[...] = jnp.concatenate([cr, ci], axis=0)

    strip = _dot_x3(c_re_ref[...], ab_re) - _dot_x3(c_im_ref[...], ab_im)
    ext = jnp.concatenate([strip, jnp.zeros((HG, W), F32)], axis=1)
    for t in range(C):
        off = (C - 1 - t) * HG
        m_ref[t * HG:(t + 1) * HG, :] = ext[:, off:off + W].astype(BF16)


S5_FB = 8


def _s5_in_kernel(x_ref, z_ref):
    G, _, rb = z_ref.shape
    for f in range(S5_FB):
        xt = x_ref[:, f, :].T
        z_ref[:, f * S5_HG:(f + 1) * S5_HG, :] = xt.reshape(G, S5_HG, rb).astype(BF16)


def _s5_in(proj3, G, rb):
    R, C, _ = proj3.shape
    W = G * S5_HG
    return pl.pallas_call(
        _s5_in_kernel,
        grid=(R // rb, C // S5_FB),
        in_specs=[pl.BlockSpec((rb, S5_FB, W), lambda i, j: (i, j, 0))],
        out_specs=pl.BlockSpec((G, S5_FB * S5_HG, rb), lambda i, j: (0, j, i)),
        out_shape=jax.ShapeDtypeStruct((G, C * S5_HG, R), BF16),
        compiler_params=_cparams("arbitrary", "arbitrary"),
        name="s5_in",
    )(proj3)


def _s5_out_kernel(yt_ref, o_ref):
    G, _, rb = yt_ref.shape
    for f in range(S5_FB):
        slab = yt_ref[:, f * S5_HG:(f + 1) * S5_HG, :].reshape(G * S5_HG, rb)
        o_ref[:, f, :] = slab.T


def _s5_out(yt, rb):
    G, CW, R = yt.shape
    C = CW // S5_HG
    W = G * S5_HG
    return pl.pallas_call(
        _s5_out_kernel,
        grid=(R // rb, C // S5_FB),
        in_specs=[pl.BlockSpec((G, S5_FB * S5_HG, rb), lambda i, j: (0, j, i))],
        out_specs=pl.BlockSpec((rb, S5_FB, W), lambda i, j: (i, j, 0)),
        out_shape=jax.ShapeDtypeStruct((R, C, W), F32),
        compiler_params=_cparams("arbitrary", "arbitrary"),
        name="s5_out",
    )(yt)


def _s5_chunk_kernel(*refs, n_chunks):
    params, (z_ref, y_ref), ops = refs[:9], refs[9:11], refs[11:]
    for g in range(z_ref.shape[0]):
        at = lambda rs: [r.at[g] for r in rs]
        _s5_ops_group(*at(params), *at(ops))
        _s5_chunk_group(z_ref.at[g], *at(ops), y_ref.at[g], n_chunks)


def _s5_chunk_group(z_ref, m_ref, wbt_ref, wct_ref, a64_ref, y_ref, n_chunks):
    P = S5_P
    z = z_ref[...]
    R = z.shape[1]
    x = _dot(wbt_ref[...], z)
    y_ref[...] = _dot(m_ref[...], z)
    xr, xi = x[:P], x[P:]
    a = a64_ref[...]
    ar, ai = a[:P], a[P:]
    cidx = lax.broadcasted_iota(I32, (1, R), 1) % n_chunks
    shift = 1
    while shift < n_chunks:
        keep = cidx >= shift
        sr = jnp.where(keep, pltpu.roll(xr, shift, 1), 0.0)
        si = jnp.where(keep, pltpu.roll(xi, shift, 1), 0.0)
        xr, xi = xr + (ar * sr - ai * si), xi + (ar * si + ai * sr)
        ar, ai = ar * ar - ai * ai, 2.0 * ar * ai
        shift *= 2
    keep = cidx >= 1
    s_in = jnp.concatenate([jnp.where(keep, pltpu.roll(xr, 1, 1), 0.0),
                            jnp.where(keep, pltpu.roll(xi, 1, 1), 0.0)], axis=0)
    sh, sl = _split2(s_in)
    wct = wct_ref[...]
    y_ref[...] = y_ref[...] + (_dot(wct, sh) + _dot(wct, sl))


def _s5_chunk(z, lam_re, lam_im, log_step, ct_re, ct_im, bt_re, bt_im, c_re, c_im, n_chunks):
    G, W, R = z.shape
    P, HG = lam_re.shape[1], S5_HG
    P2 = 2 * P
    gps = 2 if G % 2 == 0 else 1
    col = lambda a: a.reshape(G, P, 1)
    g3 = lambda s1, s2: pl.BlockSpec((gps, s1, s2), lambda g: (g, 0, 0))
    return pl.pallas_call(
        functools.partial(_s5_chunk_kernel, n_chunks=n_chunks),
        grid=(G // gps,),
        in_specs=[g3(P, 1), g3(P, 1), g3(1, 1),
                  g3(P, HG), g3(P, HG), g3(P, HG), g3(P, HG), g3(HG, P), g3(HG, P), g3(W, R)],
        out_specs=g3(W, R),
        out_shape=jax.ShapeDtypeStruct((G, W, R), F32),
        scratch_shapes=[pltpu.VMEM((gps, W, W), BF16), pltpu.VMEM((gps, P2, W), BF16),
                        pltpu.VMEM((gps, W, P2), BF16), pltpu.VMEM((gps, P2, 1), F32)],
        compiler_params=_cparams("arbitrary"),
        name="s5_chunk",
    )(col(lam_re), col(lam_im), log_step.reshape(G, 1, 1), ct_re, ct_im, bt_re, bt_im, c_re, c_im, z)


def _gelu_tanh(x):
    return 0.5 * x * (1.0 + jnp.tanh(0.7978845608028654 * (x + 0.044715 * (x * x * x))))


def _s5_glu_kernel(y_ref, u_ref, d_ref, w_ref, b_ref, o_ref):
    y = y_ref[...] + d_ref[...] * u_ref[...]
    y = _gelu_tanh(y)
    z = _dot(y.astype(BF16), w_ref[...]) + b_ref[...]
    o_ref[...] = (y * jax.nn.sigmoid(z)).astype(BF16)


def _s5_glu(y_ssm, proj, d, w_bf16, b, tm):
    T, W = y_ssm.shape
    return pl.pallas_call(
        _s5_glu_kernel,
        grid=(T // tm,),
        in_specs=[
            pl.BlockSpec((tm, W), lambda i: (i, 0)),
            pl.BlockSpec((tm, W), lambda i: (i, 0)),
            pl.BlockSpec((1, W), lambda i: (0, 0)),
            pl.BlockSpec((W, W), lambda i: (0, 0)),
            pl.BlockSpec((1, W), lambda i: (0, 0)),
        ],
        out_specs=pl.BlockSpec((tm, W), lambda i: (i, 0)),
        out_shape=jax.ShapeDtypeStruct((T, W), BF16),
        compiler_params=_cparams("arbitrary"),
        name="s5_glu",
    )(y_ssm, proj, d, w_bf16, b)


def _shift(z, prev_row):
    rolled = pltpu.roll(z, 1, 0)
    first = lax.broadcasted_iota(I32, (z.shape[0], 1), 0) == 0
    return jnp.where(first, prev_row, rolled)


RW_OPERANDS = ("r_t", "k_t", "a_t", "b_t", "k_h", "b_h", "v")
RW_LORA_PAD = 512
RW_PREP_PIECES = 14
RW_STAGES = 24


def _rwkv_chunk_kernel(zr_ref, zk_ref, zv_ref, zl_ref, mu_ref, mul_ref, w0_ref, a0_ref, kk_ref, ka_ref,
                       w2_ref, a2_ref, g2_ref, rk_ref, lnw_ref, lnb_ref,
                       o_ref, z_ref, ys_ref, qs_ref, vs_ref, gs_ref, car_ref, carl_ref, opnd_ref, ptot_ref):
    @pl.when(pl.program_id(1) == 0)
    def _():
        for ref in (z_ref, ys_ref, qs_ref, vs_ref, gs_ref, car_ref, carl_ref, opnd_ref, ptot_ref):
            ref[...] = jnp.zeros_like(ref)

    C = RW_CHUNK
    refs = (zr_ref, zk_ref, zv_ref, zl_ref, mu_ref, mul_ref, w0_ref, a0_ref, kk_ref, ka_ref,
            w2_ref, a2_ref, g2_ref, rk_ref, lnw_ref, lnb_ref,
            o_ref, z_ref, ys_ref, qs_ref, vs_ref, gs_ref, car_ref, carl_ref, opnd_ref, ptot_ref)
    for half in range(2):
        _rwkv_half_step(half, slice(half * C, (half + 1) * C), *refs)


def _rwkv_half_step(par, rows, zr_ref, zk_ref, zv_ref, zl_ref, mu_ref, mul_ref, w0_ref, a0_ref, kk_ref, ka_ref,
                    w2_ref, a2_ref, g2_ref, rk_ref, lnw_ref, lnb_ref,
                    o_ref, z_ref, ys_ref, qs_ref, vs_ref, gs_ref, car_ref, carl_ref, opnd_ref, ptot_ref):
    C, N = RW_CHUNK, RW_N
    NS = zr_ref.shape[0]
    H = zr_ref.shape[2] // N

    low = lax.broadcasted_iota(I32, (C, 2 * N), 1) < N
    inv_n = 1.0 / N

    def head_sums(t):
        s0 = jnp.sum(jnp.where(low, t, 0.0), axis=-1, keepdims=True)
        s1 = jnp.sum(jnp.where(low, 0.0, t), axis=-1, keepdims=True)
        return jnp.where(low, s0, s1)

    def tail():
        for s in range(NS):
            for p in range(H // 2):
                ps = slice(2 * p * N, 2 * (p + 1) * N)
                yp = ys_ref[s, :, ps]
                yc = yp - head_sums(yp) * inv_n
                var = head_sums(yc * yc) * inv_n
                yn = yc * lax.rsqrt(var + RW_GN_EPS) * lnw_ref[:, ps] + lnb_ref[:, ps]
                bonus = head_sums(qs_ref[par, s, :, ps]) * vs_ref[par, s, :, ps].astype(F32)
                o_ref[s, rows, ps] = ((yn + bonus) * gs_ref[par, s, :, ps].astype(F32)).astype(BF16)
                yield

    ri = lax.broadcasted_iota(I32, (C, C), 0)
    ci = lax.broadcasted_iota(I32, (C, C), 1)
    tril = ri >= ci
    stril = ri > ci
    eye = (ri == ci).astype(F32)
    both = jnp.concatenate([stril, tril], axis=0)
    prev = 1 - par
    ops = {}
    for s in range(NS):
        for h in range(H):
            cols = slice(h * N, (h + 1) * N)
            ops[(s, h)] = {name: functools.partial(lambda n, s, cols: opnd_ref[prev, n, s, :, cols], n, s, cols)
                           for n, name in enumerate(RW_OPERANDS)}
            ops[(s, h)]["p_tot"] = functools.partial(lambda s, cols: ptot_ref[prev, s, 0:1, cols], s, cols)
    def head_sums_all(t):
        return jnp.concatenate([head_sums(t[:, 2 * p * N:2 * (p + 1) * N]) for p in range(H // 2)], axis=1)

    def lerp(zz, prev_row, mu):
        return zz + (_shift(zz, prev_row) - zz) * mu

    def prepare():
        for s in range(NS):
            zr, zk, zv, zl = zr_ref[s, rows, :], zk_ref[s, rows, :], zv_ref[s, rows, :], zl_ref[s, rows, :]
            r_f = lerp(zr, car_ref[s, 0:1, :], mu_ref[0:1, :])
            yield
            k_raw = lerp(zk, car_ref[s, 1:2, :], mu_ref[1:2, :])
            yield
            v_f = lerp(zv, car_ref[s, 2:3, :], mu_ref[2:3, :])
            xl = lerp(zl, carl_ref[s, 0:1, :], mul_ref[...])
            car_ref[s, 0:1, :] = zr[C - 1:C, :]
            car_ref[s, 1:2, :] = zk[C - 1:C, :]
            car_ref[s, 2:3, :] = zv[C - 1:C, :]
            carl_ref[s, 0:1, :] = zl[C - 1:C, :]
            yield
            xwa, xg = xl[:, 0:LANES], xl[:, LANES:LANES + g2_ref.shape[0]]
            dw = _dot(jnp.tanh(xwa).astype(BF16), w2_ref[...])
            da = _dot(xwa.astype(BF16), a2_ref[...])
            g_f = _dot(jax.nn.sigmoid(xg).astype(BF16), g2_ref[...])
            yield
            zw = -(w0_ref[...] + dw)
            w_log = -(jnp.maximum(zw, 0.0) + jnp.log(1.0 + jnp.exp(-jnp.abs(zw)))) - 0.5
            yield
            a_sig = jax.nn.sigmoid(a0_ref[...] + da)
            kk = k_raw * kk_ref[...]
            yield
            kk = kk / jnp.maximum(jnp.sqrt(head_sums_all(kk * kk)), 1e-12)
            yield
            k_f = k_raw * (1.0 + (a_sig - 1.0) * ka_ref[...])
            b_f = kk * a_sig
            lw = -jnp.exp(w_log)
            yield
            cs = _dot_exact_lhs(tril.astype(BF16), lw)
            tot = cs[C - 1:C, :]
            yield
            store = lambda name, a: opnd_ref.__setitem__((par, RW_OPERANDS.index(name), s), a.astype(BF16))
            p_inc = jnp.exp(cs)
            store("r_t", r_f * p_inc)
            yield
            p_inv = jnp.exp(-cs)
            store("k_t", k_f * p_inv)
            store("b_t", b_f * p_inv)
            yield
            store("a_t", -kk * jnp.exp(cs - lw))
            yield
            p_rest = jnp.exp(tot - cs)
            store("k_h", k_f * p_rest)
            store("b_h", b_f * p_rest)
            yield
            v_b = v_f.astype(BF16)
            store("v", v_b)
            ptot_ref[par, s, 0:1, :] = jnp.exp(tot)
            qs_ref[par, s] = r_f * k_f * rk_ref[...]
            vs_ref[par, s] = v_b
            gs_ref[par, s] = g_f.astype(BF16)
            yield

    tail_pieces, prep_pieces = tail(), prepare()
    n_pieces = NS * (H // 2 + RW_PREP_PIECES)
    state = dict(ticks=0, done=0)

    def tick():
        state["ticks"] += 1
        due = -(-state["ticks"] * n_pieces // RW_STAGES)
        while state["done"] < due:
            state["done"] += 1
            if next(tail_pieces, "done") == "done":
                next(prep_pieces, None)

    def drain_tail():
        for _ in tail_pieces:
            state["done"] += 1

    _rwkv_heads(ops, stril, tril, both, eye, z_ref, ys_ref, tick, drain_tail)
    for _ in prep_pieces:
        pass


def _rwkv_heads(ops, stril, tril, both, eye, z_ref, ys_ref, tick, drain_tail):
    C, N = RW_CHUNK, RW_N
    hs = list(ops)

    def each(f):
        out = {h: f(h) for h in hs}
        tick()
        return out

    get = lambda name: (lambda h: ops[h][name]())
    a_t, r_t, b_t, k_t, k_h, b_h, p_tot = (get(n) for n in ("a_t", "r_t", "b_t", "k_t", "k_h", "b_h", "p_tot"))
    v = each(lambda h: ops[h]["v"]())
    ar = each(lambda h: jnp.concatenate([a_t(h), r_t(h)], axis=0).astype(BF16))
    m_b = each(lambda h: _dot_nt(ar[h], b_t(h).astype(BF16)))
    m_k = each(lambda h: _dot_nt(ar[h], k_t(h).astype(BF16)))
    l_ab = each(lambda h: jnp.where(stril, m_b[h][:C], 0.0).astype(BF16))
    m_rb = each(lambda h: jnp.where(tril, m_b[h][C:], 0.0).astype(BF16))
    lm_k = each(lambda h: jnp.where(both, m_k[h], 0.0).astype(BF16))
    lmv = each(lambda h: _dot(lm_k[h], v[h]))

    x = each(lambda h: jnp.concatenate([lmv[h][:C], a_t(h).astype(F32)], axis=1))
    y0 = each(lambda h: lmv[h][C:])
    hk = each(lambda h: _dot_tn(k_h(h).astype(BF16), v[h]))
    lp = l_ab
    step = 1
    while step < C:
        x = each(lambda h: x[h] + _dot(lp[h], x[h].astype(BF16)))
        step *= 2
        if step < C:
            lp = each(lambda h: _dot(lp[h], lp[h]).astype(BF16))
    xb = each(lambda h: x[h].astype(BF16))
    yq = each(lambda h: _dot(m_rb[h], xb[h]))
    gh = each(lambda h: _dot_tn(b_h(h).astype(BF16), xb[h]))

    z = each(lambda h: _split2(z_ref[h[0], h[1]]))
    qg = each(lambda h: jnp.concatenate([yq[h][:, N:] + r_t(h).astype(F32),
                                         gh[h][:, N:] + eye * p_tot(h)], axis=0).astype(BF16))
    qgz = each(lambda h: _dot(qg[h], z[h][0]) + _dot(qg[h], z[h][1]))
    yz = each(lambda h: qgz[h][:C])
    gz = each(lambda h: qgz[h][C:])
    drain_tail()
    for s, hd in hs:
        z_ref[s, hd] = (gh[s, hd][:, :N] + hk[s, hd]) + gz[s, hd]
        ys_ref[s, :, hd * N:(hd + 1) * N] = (yq[s, hd][:, :N] + y0[s, hd]) + yz[s, hd]


RW_SEQS_PER_STEP = 1


def _rwkv_chunk(proj, B, L, mu3, mul, w0, a0, k_k, k_a, w2p, a2p, g2p, r_k, ln_w, ln_b):
    T = B * L
    W = w0.shape[1]
    C = RW_CHUNK
    nc = L // C
    H = W // RW_N
    assert H % 2 == 0
    ns = RW_SEQS_PER_STEP if B % RW_SEQS_PER_STEP == 0 else 1
    lw = mul.shape[1]
    proj3 = proj.reshape(B, L, proj.shape[1])
    assert nc % 2 == 0
    np2 = nc // 2
    cols = lambda width, j: pl.BlockSpec((ns, 2 * C, width), lambda bi, i: (bi, jnp.minimum(i, np2 - 1), j))
    full = lambda a: pl.BlockSpec(a.shape, lambda bi, i: (0, 0))
    consts = (mu3, mul, w0, a0, k_k, k_a, w2p, a2p, g2p, r_k, ln_w, ln_b)
    out = pl.pallas_call(
        _rwkv_chunk_kernel,
        grid=(B // ns, np2 + 1),
        in_specs=[cols(W, 1), cols(W, 2), cols(W, 3), cols(lw, 4 * W // lw)] + [full(a) for a in consts],
        out_specs=pl.BlockSpec((ns, 2 * C, W), lambda bi, i: (bi, jnp.maximum(i - 1, 0), 0)),
        out_shape=jax.ShapeDtypeStruct((B, L, W), BF16),
        scratch_shapes=[pltpu.VMEM((ns, H, RW_N, RW_N), F32), pltpu.VMEM((ns, C, W), F32),
                        pltpu.VMEM((2, ns, C, W), F32), pltpu.VMEM((2, ns, C, W), BF16),
                        pltpu.VMEM((2, ns, C, W), BF16),
                        pltpu.VMEM((ns, 8, W), F32), pltpu.VMEM((ns, 8, lw), F32),
                        pltpu.VMEM((2, len(RW_OPERANDS), ns, C, W), BF16), pltpu.VMEM((2, ns, 8, W), F32)],
        compiler_params=_cparams("arbitrary", "arbitrary"),
        name="rwkv_chunk",
    )(proj3, proj3, proj3, proj3, *consts)
    return out.reshape(T, W)


def _first_index_of_max(vals, lane, valid):
    neg = jnp.float32(-jnp.inf)
    masked = jnp.where(valid, vals, neg)
    m = jnp.max(masked, axis=-1, keepdims=True)
    idx = jnp.min(jnp.where(valid & (masked == m), lane, LANES), axis=-1, keepdims=True)
    return m, idx


def _outproj_kernel(s5_ref, rw_ref, x_ref, wt_ref, wb_ref, g_ref, wr_ref, br_ref,
                    h_ref, hn_ref, rec_ref):
    h = x_ref[...] + (_dot(s5_ref[...], wt_ref[...]) + _dot(rw_ref[...], wb_ref[...]))
    h_ref[...] = h
    ms = jnp.mean(h * h, axis=-1, keepdims=True)
    hn = h * lax.rsqrt(ms + RMS_EPS) * g_ref[...]
    hn_ref[...] = _pack_halves(hn).reshape(hn_ref.shape)

    hh, hl = _split2(hn)
    wh, wl = wr_ref[:, :LANES], wr_ref[:, LANES:]
    logits = _dot(hh, wh) + (_dot(hh, wl) + _dot(hl, wh)) + br_ref[...]
    lane = lax.broadcasted_iota(I32, logits.shape, 1)
    is_grp = (lane >= N_EXPERTS) & (lane < N_EXPERTS + N_GROUPS)
    gmax, gidx = _first_index_of_max(logits, lane, is_grp)
    gsum = jnp.sum(jnp.where(is_grp, jnp.exp(logits - gmax), 0.0), axis=-1, keepdims=True)
    p_grp = 1.0 / gsum
    grp = gidx - N_EXPERTS
    in_grp = (lane >= grp * EPG) & (lane < (grp + 1) * EPG)
    m1, i1 = _first_index_of_max(logits, lane, in_grp)
    m2, i2 = _first_index_of_max(logits, lane, in_grp & (lane != i1))
    e = jnp.exp(m2 - m1)
    g1 = p_grp / (1.0 + e)
    g2 = p_grp * e / (1.0 + e)
    rec = jnp.where(lane == 0, i1.astype(F32),
          jnp.where(lane == 1, i2.astype(F32),
          jnp.where(lane == 2, g1, jnp.where(lane == 3, g2, 0.0))))
    rec_ref[...] = rec


def _outproj(s5o, rwo, x2, w_top, w_bot, g, wr, b_route, tm):
    T, D = x2.shape
    W = s5o.shape[1]
    full = lambda a: pl.BlockSpec(a.shape, lambda i: (0, 0))
    return pl.pallas_call(
        _outproj_kernel,
        grid=(T // tm,),
        in_specs=[pl.BlockSpec((tm, W), lambda i: (i, 0)), pl.BlockSpec((tm, W), lambda i: (i, 0)),
                  pl.BlockSpec((tm, D), lambda i: (i, 0)),
                  full(w_top), full(w_bot), full(g), full(wr), full(b_route)],
        out_specs=[pl.BlockSpec((tm, D), lambda i: (i, 0)), pl.BlockSpec((tm, 1, D // 2), lambda i: (i, 0, 0)),
                   pl.BlockSpec((tm, LANES), lambda i: (i, 0))],
        out_shape=[jax.ShapeDtypeStruct((T, D), F32), jax.ShapeDtypeStruct((T, 1, D // 2), U32),
                   jax.ShapeDtypeStruct((T, LANES), F32)],
        compiler_params=_cparams("arbitrary"),
        name="outproj_route",
    )(s5o, rwo, x2, w_top, w_bot, g, wr, b_route)


def _onehots(rec, lane):
    oh0 = (lane == rec[:, 0:1].astype(I32)).astype(F32)
    oh1 = (lane == rec[:, 1:2].astype(I32)).astype(F32)
    return oh0, oh1


def _rank_kernel(rec_ref, rank_ref, cnt_ref, base_ref):
    tb = rec_ref.shape[0]

    @pl.when(pl.program_id(0) == 0)
    def _():
        base_ref[...] = jnp.zeros_like(base_ref)

    lane = lax.broadcasted_iota(I32, (tb, LANES), 1)
    oh0, oh1 = _onehots(rec_ref[...], lane)
    both = oh0 + oh1
    ri = lax.broadcasted_iota(I32, (tb, tb), 0)
    ci = lax.broadcasted_iota(I32, (tb, tb), 1)
    before = _dot((ri > ci).astype(BF16), both.astype(BF16)) + base_ref[0:1, :]
    rank0 = jnp.sum(oh0 * before, axis=-1, keepdims=True)
    rank1 = jnp.sum(oh1 * before, axis=-1, keepdims=True)
    rank_ref[...] = jnp.where(lane == 0, rank0, jnp.where(lane == 1, rank1, 0.0))
    total = base_ref[0:1, :] + jnp.sum(both, axis=0, keepdims=True)
    base_ref[0:1, :] = total
    cnt_ref[...] = jnp.broadcast_to(total, cnt_ref.shape)


def _rank(rec, tb):
    T = rec.shape[0]
    return pl.pallas_call(
        _rank_kernel,
        grid=(T // tb,),
        in_specs=[pl.BlockSpec((tb, LANES), lambda i: (i, 0))],
        out_specs=[pl.BlockSpec((tb, LANES), lambda i: (i, 0)), pl.BlockSpec((8, LANES), lambda i: (0, 0))],
        out_shape=[jax.ShapeDtypeStruct((T, LANES), F32), jax.ShapeDtypeStruct((8, LANES), F32)],
        scratch_shapes=[pltpu.VMEM((8, LANES), F32)],
        compiler_params=_cparams("arbitrary"),
        name="moe_rank",
    )(rec)


def _padded_starts(cnt):
    padded = jnp.ceil(cnt * (1.0 / MOE_BLOCK)) * MOE_BLOCK
    ri = lax.broadcasted_iota(I32, (LANES, LANES), 0)
    ci = lax.broadcasted_iota(I32, (LANES, LANES), 1)
    p8 = jnp.broadcast_to(padded, (8, LANES))
    pend = _dot_exact_lhs_rhs(p8, (ri <= ci).astype(BF16))[0:1, :]
    return pend - padded, pend


def _dot_exact_lhs_rhs(a, b_bf16):
    ah, am, al = _split3(a)
    return _dot(ah, b_bf16) + (_dot(am, b_bf16) + _dot(al, b_bf16))


def _dest_kernel(rec_ref, rank_ref, cnt_ref, dest_ref, blk_ref):
    tb = rec_ref.shape[0]
    cnt = cnt_ref[0:1, :]
    pstart, pend = _padded_starts(cnt)
    lane = lax.broadcasted_iota(I32, (tb, LANES), 1)
    oh0, oh1 = _onehots(rec_ref[...], lane)
    rank = rank_ref[...]
    d0 = jnp.sum(oh0 * pstart, axis=-1, keepdims=True) + rank[:, 0:1]
    d1 = jnp.sum(oh1 * pstart, axis=-1, keepdims=True) + rank[:, 1:2]
    dest_ref[...] = jnp.where(lane == 0, d0, jnp.where(lane == 1, d1, 0.0)).astype(I32)

    @pl.when(pl.program_id(0) == 0)
    def _():
        nb = blk_ref.shape[0]
        blane = lax.broadcasted_iota(I32, (nb, LANES), 1)
        bstart = (lax.broadcasted_iota(I32, (nb, 1), 0) * MOE_BLOCK).astype(F32)
        is_e = blane < N_EXPERTS
        bexp = jnp.sum(jnp.where(is_e & (pend <= bstart), 1.0, 0.0), axis=-1, keepdims=True)
        bexp = jnp.minimum(bexp, N_EXPERTS - 1.0)
        nact = jnp.max(jnp.where(is_e, pend, 0.0), axis=-1, keepdims=True) * (1.0 / MOE_BLOCK)
        lastblk = jnp.where(is_e & (pend > pstart), pend - MOE_BLOCK, -1.0)
        brow = lax.broadcasted_iota(I32, (nb, 1), 0)
        out = jnp.where(blane == 0, bexp, jnp.where(blane == 1, nact, 0.0))
        out = jnp.where(brow == nb - 1, lastblk, out)
        blk_ref[...] = out.astype(I32)


def _dest(rec, rank, cnt, tb, nb_rows):
    T = rec.shape[0]
    return pl.pallas_call(
        _dest_kernel,
        grid=(T // tb,),
        in_specs=[pl.BlockSpec((tb, LANES), lambda i: (i, 0)), pl.BlockSpec((tb, LANES), lambda i: (i, 0)),
                  pl.BlockSpec((8, LANES), lambda i: (0, 0))],
        out_specs=[pl.BlockSpec((tb, LANES), lambda i: (i, 0)), pl.BlockSpec((nb_rows, LANES), lambda i: (0, 0))],
        out_shape=[jax.ShapeDtypeStruct((T, LANES), I32), jax.ShapeDtypeStruct((nb_rows, LANES), I32)],
        compiler_params=_cparams("arbitrary"),
        name="moe_dest",
    )(rec, rank, cnt)


def _scatter_kernel(pad_ref, na_ref, dest_ref, hn_ref, xs_ref, zbuf_ref, idx_ref, sa_ref, sb_ref,
                    sem_a, sem_b, isem, zsem, *, tb, n_blocks, n_tail, n_steps):
    i = pl.program_id(0)

    @pl.when(i == 0)
    def _():
        zbuf_ref[...] = jnp.zeros_like(zbuf_ref)

        def zero_rows(start):
            return pltpu.make_async_copy(zbuf_ref, xs_ref.at[pl.ds(start, MOE_BLOCK)], zsem)

        def pad_start(e, c):
            @pl.when(pad_ref[e] >= 0)
            def _():
                zero_rows(pad_ref[e]).start()
            return c

        def pad_wait(e, c):
            @pl.when(pad_ref[e] >= 0)
            def _():
                zero_rows(pad_ref[e]).wait()
            return c

        def tail_start(k, c):
            @pl.when(na_ref[0] + k < n_blocks)
            def _():
                zero_rows((na_ref[0] + k) * MOE_BLOCK).start()
            return c

        def tail_wait(k, c):
            @pl.when(na_ref[0] + k < n_blocks)
            def _():
                zero_rows((na_ref[0] + k) * MOE_BLOCK).wait()
            return c

        lax.fori_loop(0, N_EXPERTS, pad_start, 0)
        lax.fori_loop(0, n_tail, tail_start, 0)
        lax.fori_loop(0, N_EXPERTS, pad_wait, 0)
        lax.fori_loop(0, n_tail, tail_wait, 0)

    def idx_copy(step, slot):
        return pltpu.make_async_copy(dest_ref.at[step], idx_ref.at[pl.ds(slot * (2 * tb), 2 * tb)], isem)

    @pl.when(i == 0)
    def _():
        idx_copy(0, 0).start()

    slot = i % 2
    idx_copy(i, slot).wait()

    @pl.when(i + 1 < n_steps)
    def _():
        idx_copy(i + 1, 1 - slot).start()

    hb = tb // 2
    for half, (buf, sem) in enumerate(((sa_ref, sem_a), (sb_ref, sem_b))):
        def wait_rows(buf=buf, sem=sem):
            for _ in range(2):
                pltpu.make_async_copy(buf, xs_ref.at[pl.ds(0, hb)], sem).wait()

        @pl.when(i > 0)
        def _():
            wait_rows()

        buf[...] = hn_ref[half * hb:(half + 1) * hb]

        def issue(t, c, buf=buf, sem=sem, half=half):
            for s in range(2):
                d = idx_ref[slot * (2 * tb) + 2 * (half * hb + t) + s]
                pltpu.make_async_copy(buf.at[pl.ds(t, 1)], xs_ref.at[pl.ds(d, 1)], sem).start(priority=s)
            return c

        lax.fori_loop(0, hb, issue, 0, unroll=8)

    @pl.when(i == n_steps - 1)
    def _():
        for buf, sem in ((sa_ref, sem_a), (sb_ref, sem_b)):
            for _ in range(2):
                pltpu.make_async_copy(buf, xs_ref.at[pl.ds(0, hb)], sem).wait()


def _scatter(lastblk, nact, dest2, hn, tb, n_blocks):
    T, _, D = hn.shape
    n_tail = n_blocks - (-(-2 * T // MOE_BLOCK))
    hb = tb // 2
    return pl.pallas_call(
        functools.partial(_scatter_kernel, tb=tb, n_blocks=n_blocks, n_tail=n_tail, n_steps=T // tb),
        grid_spec=pltpu.PrefetchScalarGridSpec(
            num_scalar_prefetch=2, grid=(T // tb,),
            in_specs=[pl.BlockSpec(memory_space=pl.ANY),
                      pl.BlockSpec((tb, 1, D), lambda i, pad, na: (i, 0, 0))],
            out_specs=pl.BlockSpec(memory_space=pl.ANY),
            scratch_shapes=[pltpu.VMEM((MOE_BLOCK, 1, D), hn.dtype), pltpu.SMEM((4 * tb,), I32),
                            pltpu.VMEM((hb, 1, D), hn.dtype), pltpu.VMEM((hb, 1, D), hn.dtype),
                            pltpu.SemaphoreType.DMA, pltpu.SemaphoreType.DMA,
                            pltpu.SemaphoreType.DMA, pltpu.SemaphoreType.DMA]),
        out_shape=jax.ShapeDtypeStruct((n_blocks * MOE_BLOCK, 1, D), hn.dtype),
        compiler_params=_cparams("arbitrary"),
        name="moe_scatter",
    )(lastblk, nact, dest2, hn)


W_SLOTS = 2


def _expert_kernel(be_ref, na_ref, x_ref, wg_hbm, wu_hbm, wd_hbm, y_ref,
                   wg_buf, wu_buf, wd_buf, wgb_ref, wub_ref, wdb_ref, x2_ref, elist_ref, state_ref, sems):
    j = pl.program_id(0)
    na = na_ref[0]

    def weight_copies(e, slot):
        return [pltpu.make_async_copy(src.at[e], buf.at[slot], sems.at[slot, n])
                for n, (src, buf) in enumerate(((wg_hbm, wg_buf), (wu_hbm, wu_buf), (wd_hbm, wd_buf)))]

    @pl.when(j == 0)
    def _():
        def scan(k, n):
            is_new = (k == 0) | (be_ref[k] != be_ref[jnp.maximum(k - 1, 0)])

            @pl.when(is_new)
            def _():
                elist_ref[n] = be_ref[k]
            return n + is_new.astype(I32)

        n_exp = lax.fori_loop(0, na, scan, 0)
        state_ref[0] = 0
        state_ref[1] = n_exp
        for n in range(W_SLOTS):
            @pl.when(n < n_exp)
            def _():
                for cp in weight_copies(elist_ref[n], n):
                    cp.start()

    @pl.when(j < na)
    def _():
        @pl.when((j == 0) | (be_ref[j] != be_ref[jnp.maximum(j - 1, 0)]))
        def _():
            n = state_ref[0]
            slot = n % W_SLOTS
            for cp in weight_copies(elist_ref[n], slot):
                cp.wait()
            wgb_ref[...] = wg_buf[slot].astype(BF16)
            wub_ref[...] = wu_buf[slot].astype(BF16)
            wdb_ref[...] = wd_buf[slot].astype(BF16)

            @pl.when(n + W_SLOTS < state_ref[1])
            def _():
                for cp in weight_copies(elist_ref[n + W_SLOTS], slot):
                    cp.start()

            state_ref[0] = n + 1

        x2_ref[...] = x_ref[...].reshape(x2_ref.shape)
        lo, hi = _unpack_halves(x2_ref[...])
        lo, hi = lo.astype(BF16), hi.astype(BF16)
        half = lo.shape[1]
        hg = _dot(lo, wgb_ref[:half, :]) + _dot(hi, wgb_ref[half:, :])
        hu = _dot(lo, wub_ref[:half, :]) + _dot(hi, wub_ref[half:, :])
        act = (hg * jax.nn.sigmoid(hg)) * hu
        y = _dot(act.astype(BF16), wdb_ref[...])
        y_ref[...] = _pack_halves(y).reshape(y_ref.shape)

    @pl.when(j >= na)
    def _():
        y_ref[...] = jnp.zeros_like(y_ref)


def _experts(blk_exp, nact, xs, w_gate, w_up, w_down, n_blocks):
    DP = xs.shape[2]
    D = 2 * DP
    rows = n_blocks * MOE_BLOCK
    DE = w_gate.shape[2]
    blk = lambda j, be, na: (jnp.minimum(j, na[0] - 1), 0, 0)
    hbm = pl.BlockSpec(memory_space=pl.ANY)
    return pl.pallas_call(
        _expert_kernel,
        grid_spec=pltpu.PrefetchScalarGridSpec(
            num_scalar_prefetch=2, grid=(n_blocks,),
            in_specs=[pl.BlockSpec((MOE_BLOCK, 1, DP), blk), hbm, hbm, hbm],
            out_specs=pl.BlockSpec((MOE_BLOCK, 1, DP), lambda j, be, na: (j, 0, 0)),
            scratch_shapes=[pltpu.VMEM((W_SLOTS, D, DE), F32), pltpu.VMEM((W_SLOTS, D, DE), F32),
                            pltpu.VMEM((W_SLOTS, DE, D), F32),
                            pltpu.VMEM((D, DE), BF16), pltpu.VMEM((D, DE), BF16), pltpu.VMEM((DE, D), BF16),
                            pltpu.VMEM((MOE_BLOCK, DP), U32), pltpu.SMEM((N_EXPERTS,), I32), pltpu.SMEM((2,), I32),
                            pltpu.SemaphoreType.DMA((W_SLOTS, 3))]),
        out_shape=jax.ShapeDtypeStruct((rows, 1, DP), U32),
        compiler_params=_cparams("arbitrary"),
        name="moe_experts",
    )(blk_exp, nact, xs, w_gate, w_up, w_down)


def _combine_kernel(dest_ref, ys_ref, h_ref, rec_ref, g_ref, o_ref, idx_ref, ya0_ref, ya1_ref, yb0_ref, yb1_ref,
                    y2_ref, sem_a, sem_b, isem, *, normalize, n_steps):
    i = pl.program_id(0)
    tb = h_ref.shape[0]
    hb = tb // 2
    halves = ((ya0_ref, ya1_ref, sem_a), (yb0_ref, yb1_ref, sem_b))

    def idx_copy(step, slot):
        return pltpu.make_async_copy(dest_ref.at[step], idx_ref.at[pl.ds(slot * (2 * tb), 2 * tb)], isem)

    def issue(half, slot):
        bufs, sem = halves[half][:2], halves[half][2]

        def body(t, c):
            for s in range(2):
                d = idx_ref[slot * (2 * tb) + 2 * (half * hb + t) + s]
                pltpu.make_async_copy(ys_ref.at[pl.ds(d, 1)], bufs[s].at[pl.ds(t, 1)], sem).start(priority=s)
            return c

        lax.fori_loop(0, hb, body, 0, unroll=8)

    @pl.when(i == 0)
    def _():
        first = idx_copy(0, 0)
        first.start()
        first.wait()
        issue(0, 0)
        issue(1, 0)
        if n_steps > 1:
            idx_copy(1, 1).start()

    nslot = (i + 1) % 2

    @pl.when(i + 1 < n_steps)
    def _():
        idx_copy(i + 1, nslot).wait()

    rec = rec_ref[...]
    for half, (y0_ref, y1_ref, sem) in enumerate(halves):
        rows = slice(half * hb, (half + 1) * hb)
        pltpu.make_async_copy(ys_ref.at[pl.ds(0, hb)], y0_ref, sem).wait()
        pltpu.make_async_copy(ys_ref.at[pl.ds(0, hb)], y1_ref, sem).wait()
        y2_ref[...] = y0_ref[...].reshape(y2_ref.shape)
        h = h_ref[rows, :] + rec[rows, 2:3] * jnp.concatenate(_unpack_halves(y2_ref[...]), axis=1)
        y2_ref[...] = y1_ref[...].reshape(y2_ref.shape)
        h = h + rec[rows, 3:4] * jnp.concatenate(_unpack_halves(y2_ref[...]), axis=1)
        if normalize:
            ms = jnp.mean(h * h, axis=-1, keepdims=True)
            h = h * lax.rsqrt(ms + RMS_EPS) * g_ref[...]
        o_ref[rows, :] = h

        @pl.when(i + 1 < n_steps)
        def _():
            issue(half, nslot)

    @pl.when(i + 2 < n_steps)
    def _():
        idx_copy(i + 2, i % 2).start()


def _combine(dest2, ys, h, rec, g, tb, normalize):
    T, D = h.shape
    hb = tb // 2
    return pl.pallas_call(
        functools.partial(_combine_kernel, normalize=normalize, n_steps=T // tb),
        grid=(T // tb,),
        in_specs=[pl.BlockSpec(memory_space=pl.ANY), pl.BlockSpec(memory_space=pl.ANY),
                  pl.BlockSpec((tb, D), lambda i: (i, 0)), pl.BlockSpec((tb, LANES), lambda i: (i, 0)),
                  pl.BlockSpec((1, D), lambda i: (0, 0))],
        out_specs=pl.BlockSpec((tb, D), lambda i: (i, 0)),
        out_shape=jax.ShapeDtypeStruct((T, D), F32),
        scratch_shapes=[pltpu.SMEM((4 * tb,), I32)] + [pltpu.VMEM((hb, 1, D // 2), U32)] * 4 +
                       [pltpu.VMEM((hb, D // 2), U32)] + [pltpu.SemaphoreType.DMA] * 3,
        compiler_params=_cparams("arbitrary"),
        name="moe_combine",
    )(dest2, ys, h, rec, g)


def _pad_cols(a, n):
    return jnp.pad(a, ((0, 0), (0, n - a.shape[1])))


def _pad_rows(a, n, at=0):
    return jnp.pad(a, ((at, n - a.shape[0] - at), (0, 0)))


def _layer(x2, B, L, p):
    T, D = x2.shape
    W = p["s5_d"].shape[0]
    G, P = p["s5_lambda_re"].shape
    HG = W // G
    dl, al, gl = p["rwkv_w2"].shape[0], p["rwkv_a2"].shape[0], p["rwkv_g2"].shape[0]
    H = W // RW_N

    assert dl + al == LANES and gl <= RW_LORA_PAD - LANES
    w_in_r = _pad_cols(p["w_in"], 4 * W + RW_LORA_PAD).astype(BF16)
    tm_in = min(1024, T)
    proj = _inproj(x2, p["norm_mix_g"].reshape(1, D), w_in_r, tm_in, 1152)

    C = S5_CHUNK
    nc = L // C
    ct_re = jnp.swapaxes(p["s5_c_re"], 1, 2)
    ct_im = jnp.swapaxes(p["s5_c_im"], 1, 2)
    rb = min(LANES, B * nc)
    z = _s5_in(proj.reshape(B * nc, C, proj.shape[1]), G, rb)
    y_t = _s5_chunk(z, p["s5_lambda_re"], p["s5_lambda_im"], p["s5_log_step"],
                    ct_re, ct_im, p["s5_b_re"], p["s5_b_im"], p["s5_c_re"], p["s5_c_im"], nc)
    y_ssm = _s5_out(y_t, rb).reshape(T, W)
    tm = min(512, T)
    s5_out = _s5_glu(y_ssm, proj, p["s5_d"].reshape(1, W), p["s5_w_glu"].astype(BF16),
                     p["s5_b_glu"].reshape(1, W), tm)

    mu = p["rwkv_mu"]
    mu3 = _pad_rows(mu[:3 * W].reshape(3, W), 8)
    mul = _pad_cols(mu[None, 3 * W:], RW_LORA_PAD)
    w2p = _pad_rows(p["rwkv_w2"], LANES).astype(BF16)
    a2p = _pad_rows(p["rwkv_a2"], LANES, at=dl).astype(BF16)
    g2p = _pad_rows(p["rwkv_g2"], -(-gl // LANES) * LANES).astype(BF16)
    row = lambda a: a.reshape(1, W)
    rw_out = _rwkv_chunk(proj, B, L, mu3, mul, row(p["rwkv_w0"]), row(p["rwkv_a0"]), row(p["rwkv_k_k"]),
                         row(p["rwkv_k_a"]), w2p, a2p, g2p, row(p["rwkv_r_k"]), row(p["rwkv_ln_w"]),
                         row(p["rwkv_ln_b"]))

    w_out = p["w_out"].astype(BF16)
    w_route = jnp.concatenate([p["w_route_exp"], p["w_route_grp"]], axis=1)
    w_route = _pad_cols(w_route, LANES)
    wr_hi = w_route.astype(BF16)
    wr_lo = (w_route - wr_hi.astype(F32)).astype(BF16)
    wr = jnp.concatenate([wr_hi, wr_lo], axis=1)
    b_route = _pad_cols(jnp.concatenate([p["b_route_exp"], p["b_route_grp"]])[None, :], LANES)
    tm_o = min(512, T)
    h, hn, rec = _outproj(s5_out, rw_out, x2, w_out[:W], w_out[W:], p["norm_ffn_g"].reshape(1, D),
                          wr, b_route, tm_o)

    tb = min(1024, T)
    rank, cnt = _rank(rec, tb)
    n_blocks = -(-2 * T // MOE_BLOCK) + N_EXPERTS
    nb_rows = -(-(n_blocks + 1) // 8) * 8
    dest, blk = _dest(rec, rank, cnt, tb, nb_rows)
    blk_exp = blk[:n_blocks, 0]
    nact = blk[0:1, 1]
    lastblk = blk[nb_rows - 1, :N_EXPERTS]
    ts = min(512, T)
    dest2 = dest[:, :2].reshape(T // ts, 2 * ts)

    xs = _scatter(lastblk, nact, dest2, hn, ts, n_blocks)
    ys = _experts(blk_exp, nact, xs, p["w_gate"], p["w_up"], p["w_down"], n_blocks)
    return dest2, ys, h, rec, ts


def kernel(x, norm_mix_g, w_in, s5_lambda_re, s5_lambda_im, s5_log_step, s5_b_re, s5_b_im, s5_c_re, s5_c_im, s5_d, s5_w_glu, s5_b_glu, rwkv_mu, rwkv_w0, rwkv_w2, rwkv_a0, rwkv_a2, rwkv_g2, rwkv_k_k, rwkv_k_a, rwkv_r_k, rwkv_ln_w, rwkv_ln_b, w_out, norm_ffn_g, w_route_grp, b_route_grp, w_route_exp, b_route_exp, w_gate, w_up, w_down, norm_final_g):
    B, L, D = x.shape
    params = dict(
        norm_mix_g=norm_mix_g, w_in=w_in, s5_lambda_re=s5_lambda_re, s5_lambda_im=s5_lambda_im,
        s5_log_step=s5_log_step, s5_b_re=s5_b_re, s5_b_im=s5_b_im, s5_c_re=s5_c_re, s5_c_im=s5_c_im,
        s5_d=s5_d, s5_w_glu=s5_w_glu, s5_b_glu=s5_b_glu, rwkv_mu=rwkv_mu, rwkv_w0=rwkv_w0, rwkv_w2=rwkv_w2,
        rwkv_a0=rwkv_a0, rwkv_a2=rwkv_a2, rwkv_g2=rwkv_g2, rwkv_k_k=rwkv_k_k, rwkv_k_a=rwkv_k_a,
        rwkv_r_k=rwkv_r_k, rwkv_ln_w=rwkv_ln_w, rwkv_ln_b=rwkv_ln_b, w_out=w_out, norm_ffn_g=norm_ffn_g,
        w_route_grp=w_route_grp, b_route_grp=b_route_grp, w_route_exp=w_route_exp, b_route_exp=b_route_exp,
        w_gate=w_gate, w_up=w_up, w_down=w_down)
    depth = norm_mix_g.shape[0]
    h2 = x.reshape(B * L, D)
    for l in range(depth):
        p = {k_: v_[l] for k_, v_ in params.items()}
        dest2, ys, h, rec, ts = _layer(h2, B, L, p)
        h2 = _combine(dest2, ys, h, rec, norm_final_g.reshape(1, D), ts, normalize=(l == depth - 1))
    return h2.reshape(B, L, D)
```

```python
import functools

import jax
import jax.numpy as jnp
from jax import lax
from jax.experimental import pallas as pl
from jax.experimental.pallas import tpu as pltpu

F32 = jnp.float32
BF16 = jnp.bfloat16
I32 = jnp.int32

RMS_EPS = 1e-6
S5_CHUNK = 64
S5_HG = 16
S5_P = 64
RW_N = 64
RW_CHUNK = 64
RW_GN_EPS = 64e-5
N_GROUPS = 8
EPG = 8
N_EXPERTS = 64
MOE_BLOCK = 256
LANES = 128
VMEM_LIMIT = 56 * 1024 * 1024


def _cparams(*sem, **kw):
    return pltpu.CompilerParams(dimension_semantics=sem, vmem_limit_bytes=VMEM_LIMIT, **kw)


def _split2(x):
    hi = x.astype(BF16)
    lo = (x - hi.astype(F32)).astype(BF16)
    return hi, lo


def _split3(x):
    hi = x.astype(BF16)
    r = x - hi.astype(F32)
    mid = r.astype(BF16)
    lo = (r - mid.astype(F32)).astype(BF16)
    return hi, mid, lo


def _dot(a, b):
    return jnp.dot(a, b, preferred_element_type=F32)


def _dot_nt(a, b):
    return lax.dot_general(a, b, (((1,), (1,)), ((), ())), preferred_element_type=F32)


def _dot_tn(a, b):
    return lax.dot_general(a, b, (((0,), (0,)), ((), ())), preferred_element_type=F32)


def _dot_x3(a, b):
    ah, al = _split2(a)
    bh, bl = _split2(b)
    return _dot(ah, bh) + (_dot(ah, bl) + _dot(al, bh))


def _dot_exact_lhs(a_bf16, b):
    bh, bm, bl = _split3(b)
    return _dot(a_bf16, bh) + (_dot(a_bf16, bm) + _dot(a_bf16, bl))


def _cmul(ar, ai, br, bi):
    return ar * br - ai * bi, ar * bi + ai * br


U32 = jnp.uint32


def _pack_halves(x):
    half = x.shape[1] // 2
    lo = lax.bitcast_convert_type(x[:, :half].astype(BF16).astype(F32), U32)
    hi = lax.bitcast_convert_type(x[:, half:].astype(BF16).astype(F32), U32)
    return (lo >> 16) | hi


def _unpack_halves(p):
    lo = lax.bitcast_convert_type(p << 16, F32)
    hi = lax.bitcast_convert_type(p & jnp.uint32(0xFFFF0000), F32)
    return lo, hi


def _inproj_kernel(x_ref, g_ref, w_ref, o_ref, hn_ref):
    @pl.when(pl.program_id(1) == 0)
    def _():
        x = x_ref[...]
        ms = jnp.mean(x * x, axis=-1, keepdims=True)
        hn_ref[...] = (x * lax.rsqrt(ms + RMS_EPS) * g_ref[...]).astype(BF16)

    o_ref[...] = _dot(hn_ref[...], w_ref[...])


def _inproj(x2, g, w_bf16, tm, tn):
    T, D = x2.shape
    N = w_bf16.shape[1]
    return pl.pallas_call(
        _inproj_kernel,
        grid=(T // tm, N // tn),
        in_specs=[
            pl.BlockSpec((tm, D), lambda i, j: (i, 0)),
            pl.BlockSpec((1, D), lambda i, j: (0, 0)),
            pl.BlockSpec((D, tn), lambda i, j: (0, j)),
        ],
        out_specs=pl.BlockSpec((tm, tn), lambda i, j: (i, j)),
        out_shape=jax.ShapeDtypeStruct((T, N), F32),
        scratch_shapes=[pltpu.VMEM((tm, D), BF16)],
        compiler_params=_cparams("arbitrary", "arbitrary"),
        name="inproj",
    )(x2, g, w_bf16)


def _binpow(ar, ai, expo, nbits):
    pr = jnp.ones(expo.shape, F32)
    pi = jnp.zeros(expo.shape, F32)
    sr, si = ar, ai
    for bit in range(nbits):
        m = ((expo >> bit) & 1) == 1
        nr, ni = _cmul(pr, pi, sr, si)
        pr = jnp.where(m, nr, pr)
        pi = jnp.where(m, ni, pi)
        if bit + 1 < nbits:
            sr, si = _cmul(sr, si, sr, si)
    return pr, pi


def _frame_powers(ar, ai, reverse, plus_one):
    C, HG, P = S5_CHUNK, S5_HG, ar.shape[0]
    fpt = LANES // HG
    j = lax.broadcasted_iota(I32, (P, LANES), 1) // HG
    inner_r, inner_i = _binpow(ar, ai, (fpt - 1 - j) if reverse else j, (fpt - 1).bit_length())
    sr, si = ar, ai
    for _ in range(fpt.bit_length() - 1):
        sr, si = _cmul(sr, si, sr, si)
    outer = [(ar, ai) if plus_one else (jnp.ones_like(ar), jnp.zeros_like(ai))]
    for _ in range(C // fpt - 1):
        outer.append(_cmul(outer[-1][0], outer[-1][1], sr, si))
    if reverse:
        outer = outer[::-1]
    tiles = [_cmul(inner_r, inner_i, o_r, o_i) for o_r, o_i in outer]
    return (jnp.concatenate([t[0] for t in tiles], axis=1), jnp.concatenate([t[1] for t in tiles], axis=1))


def _zoh(lr, li, dt):
    mag = jnp.exp(lr * dt)
    ang = li * dt
    ar, ai = mag * jnp.cos(ang), mag * jnp.sin(ang)
    den = lr * lr + li * li
    nr, ni = ar - 1.0, ai
    fr = (nr * lr + ni * li) / den
    fi = (ni * lr - nr * li) / den
    return ar, ai, fr, fi


def _s5_ops_group(lrc_ref, lic_ref, dt_ref, ct_re_ref, ct_im_ref, bt_re_ref, bt_im_ref, c_re_ref, c_im_ref,
                  m_ref, wbt_ref, wct_ref, a64_ref):
    C, HG, P = S5_CHUNK, S5_HG, S5_P
    W = C * HG
    dt = jnp.exp(dt_ref[...])
    ar_c, ai_c, fr_c, fi_c = _zoh(lrc_ref[...], lic_ref[...], dt)

    pr, pi = _frame_powers(ar_c, ai_c, False, True)
    qr, qi = _frame_powers(ar_c, ai_c, True, False)

    sel = (lax.broadcasted_iota(I32, (HG, W), 1) % HG == lax.broadcasted_iota(I32, (HG, W), 0)).astype(BF16)
    tile = lambda a: _dot_exact_lhs_rhs(a, sel)

    c1_re, c1_im = _cmul(tile(ct_re_ref[...]), tile(ct_im_ref[...]), pr, pi)
    wct_ref[...] = jnp.concatenate([c1_re, -c1_im], axis=0).T.astype(BF16)

    bb_re, bb_im = _cmul(bt_re_ref[...], bt_im_ref[...], fr_c, fi_c)
    ab_re, ab_im = _cmul(tile(bb_re), tile(bb_im), qr, qi)
    wbt_ref[:P, :] = ab_re.astype(BF16)
    wbt_ref[P:, :] = ab_im.astype(BF16)

    cr, ci = ar_c, ai_c
    for _ in range(C.bit_length() - 1):
        cr, ci = _cmul(cr, ci, cr, ci)
    a64_ref[...] = jnp.concatenate([cr, ci], axis=0)

    strip = _dot_x3(c_re_ref[...], ab_re) - _dot_x3(c_im_ref[...], ab_im)
    ext = jnp.concatenate([strip, jnp.zeros((HG, W), F32)], axis=1)
    for t in range(C):
        off = (C - 1 - t) * HG
        m_ref[t * HG:(t + 1) * HG, :] = ext[:, off:off + W].astype(BF16)


S5_FB = 8


def _s5_in_kernel(x_ref, z_ref):
    G, _, rb = z_ref.shape
    for f in range(S5_FB):
        xt = x_ref[:, f, :].T
        z_ref[:, f * S5_HG:(f + 1) * S5_HG, :] = xt.reshape(G, S5_HG, rb).astype(BF16)


def _s5_in(proj3, G, rb):
    R, C, _ = proj3.shape
    W = G * S5_HG
    return pl.pallas_call(
        _s5_in_kernel,
        grid=(R // rb, C // S5_FB),
        in_specs=[pl.BlockSpec((rb, S5_FB, W), lambda i, j: (i, j, 0))],
        out_specs=pl.BlockSpec((G, S5_FB * S5_HG, rb), lambda i, j: (0, j, i)),
        out_shape=jax.ShapeDtypeStruct((G, C * S5_HG, R), BF16),
        compiler_params=_cparams("arbitrary", "arbitrary"),
        name="s5_in",
    )(proj3)


def _s5_out_kernel(yt_ref, o_ref):
    G, _, rb = yt_ref.shape
    for f in range(S5_FB):
        slab = yt_ref[:, f * S5_HG:(f + 1) * S5_HG, :].reshape(G * S5_HG, rb)
        o_ref[:, f, :] = slab.T


def _s5_out(yt, rb):
    G, CW, R = yt.shape
    C = CW // S5_HG
    W = G * S5_HG
    return pl.pallas_call(
        _s5_out_kernel,
        grid=(R // rb, C // S5_FB),
        in_specs=[pl.BlockSpec((G, S5_FB * S5_HG, rb), lambda i, j: (0, j, i))],
        out_specs=pl.BlockSpec((rb, S5_FB, W), lambda i, j: (i, j, 0)),
        out_shape=jax.ShapeDtypeStruct((R, C, W), F32),
        compiler_params=_cparams("arbitrary", "arbitrary"),
        name="s5_out",
    )(yt)


def _s5_chunk_kernel(*refs, n_chunks):
    params, (z_ref, y_ref), ops = refs[:9], refs[9:11], refs[11:]
    for g in range(z_ref.shape[0]):
        at = lambda rs: [r.at[g] for r in rs]
        _s5_ops_group(*at(params), *at(ops))
        _s5_chunk_group(z_ref.at[g], *at(ops), y_ref.at[g], n_chunks)


def _s5_chunk_group(z_ref, m_ref, wbt_ref, wct_ref, a64_ref, y_ref, n_chunks):
    P = S5_P
    z = z_ref[...]
    R = z.shape[1]
    x = _dot(wbt_ref[...], z)
    y_ref[...] = _dot(m_ref[...], z)
    xr, xi = x[:P], x[P:]
    a = a64_ref[...]
    ar, ai = a[:P], a[P:]
    cidx = lax.broadcasted_iota(I32, (1, R), 1) % n_chunks
    shift = 1
    while shift < n_chunks:
        keep = cidx >= shift
        sr = jnp.where(keep, pltpu.roll(xr, shift, 1), 0.0)
        si = jnp.where(keep, pltpu.roll(xi, shift, 1), 0.0)
        xr, xi = xr + (ar * sr - ai * si), xi + (ar * si + ai * sr)
        ar, ai = ar * ar - ai * ai, 2.0 * ar * ai
        shift *= 2
    keep = cidx >= 1
    s_in = jnp.concatenate([jnp.where(keep, pltpu.roll(xr, 1, 1), 0.0),
                            jnp.where(keep, pltpu.roll(xi, 1, 1), 0.0)], axis=0)
    sh, sl = _split2(s_in)
    wct = wct_ref[...]
    y_ref[...] = y_ref[...] + (_dot(wct, sh) + _dot(wct, sl))


def _s5_chunk(z, lam_re, lam_im, log_step, ct_re, ct_im, bt_re, bt_im, c_re, c_im, n_chunks):
    G, W, R = z.shape
    P, HG = lam_re.shape[1], S5_HG
    P2 = 2 * P
    gps = 2 if G % 2 == 0 else 1
    col = lambda a: a.reshape(G, P, 1)
    g3 = lambda s1, s2: pl.BlockSpec((gps, s1, s2), lambda g: (g, 0, 0))
    return pl.pallas_call(
        functools.partial(_s5_chunk_kernel, n_chunks=n_chunks),
        grid=(G // gps,),
        in_specs=[g3(P, 1), g3(P, 1), g3(1, 1),
                  g3(P, HG), g3(P, HG), g3(P, HG), g3(P, HG), g3(HG, P), g3(HG, P), g3(W, R)],
        out_specs=g3(W, R),
        out_shape=jax.ShapeDtypeStruct((G, W, R), F32),
        scratch_shapes=[pltpu.VMEM((gps, W, W), BF16), pltpu.VMEM((gps, P2, W), BF16),
                        pltpu.VMEM((gps, W, P2), BF16), pltpu.VMEM((gps, P2, 1), F32)],
        compiler_params=_cparams("arbitrary"),
        name="s5_chunk",
    )(col(lam_re), col(lam_im), log_step.reshape(G, 1, 1), ct_re, ct_im, bt_re, bt_im, c_re, c_im, z)


def _gelu_tanh(x):
    return 0.5 * x * (1.0 + jnp.tanh(0.7978845608028654 * (x + 0.044715 * (x * x * x))))


def _s5_glu_kernel(y_ref, u_ref, d_ref, w_ref, b_ref, o_ref):
    y = y_ref[...] + d_ref[...] * u_ref[...]
    y = _gelu_tanh(y)
    z = _dot(y.astype(BF16), w_ref[...]) + b_ref[...]
    o_ref[...] = (y * jax.nn.sigmoid(z)).astype(BF16)


def _s5_glu(y_ssm, proj, d, w_bf16, b, tm):
    T, W = y_ssm.shape
    return pl.pallas_call(
        _s5_glu_kernel,
        grid=(T // tm,),
        in_specs=[
            pl.BlockSpec((tm, W), lambda i: (i, 0)),
            pl.BlockSpec((tm, W), lambda i: (i, 0)),
            pl.BlockSpec((1, W), lambda i: (0, 0)),
            pl.BlockSpec((W, W), lambda i: (0, 0)),
            pl.BlockSpec((1, W), lambda i: (0, 0)),
        ],
        out_specs=pl.BlockSpec((tm, W), lambda i: (i, 0)),
        out_shape=jax.ShapeDtypeStruct((T, W), BF16),
        compiler_params=_cparams("arbitrary"),
        name="s5_glu",
    )(y_ssm, proj, d, w_bf16, b)


def _shift(z, prev_row):
    rolled = pltpu.roll(z, 1, 0)
    first = lax.broadcasted_iota(I32, (z.shape[0], 1), 0) == 0
    return jnp.where(first, prev_row, rolled)


RW_OPERANDS = ("r_t", "k_t", "a_t", "b_t", "k_h", "b_h", "v")
RW_LORA_PAD = 512
RW_PREP_PIECES = 14
RW_STAGES = 24


def _rwkv_chunk_kernel(zr_ref, zk_ref, zv_ref, zl_ref, mu_ref, mul_ref, w0_ref, a0_ref, kk_ref, ka_ref,
                       w2_ref, a2_ref, g2_ref, rk_ref, lnw_ref, lnb_ref,
                       o_ref, z_ref, ys_ref, qs_ref, vs_ref, gs_ref, car_ref, carl_ref, opnd_ref, ptot_ref):
    @pl.when(pl.program_id(1) == 0)
    def _():
        for ref in (z_ref, ys_ref, qs_ref, vs_ref, gs_ref, car_ref, carl_ref, opnd_ref, ptot_ref):
            ref[...] = jnp.zeros_like(ref)

    C = RW_CHUNK
    refs = (zr_ref, zk_ref, zv_ref, zl_ref, mu_ref, mul_ref, w0_ref, a0_ref, kk_ref, ka_ref,
            w2_ref, a2_ref, g2_ref, rk_ref, lnw_ref, lnb_ref,
            o_ref, z_ref, ys_ref, qs_ref, vs_ref, gs_ref, car_ref, carl_ref, opnd_ref, ptot_ref)
    for half in range(2):
        _rwkv_half_step(half, slice(half * C, (half + 1) * C), *refs)


def _rwkv_half_step(par, rows, zr_ref, zk_ref, zv_ref, zl_ref, mu_ref, mul_ref, w0_ref, a0_ref, kk_ref, ka_ref,
                    w2_ref, a2_ref, g2_ref, rk_ref, lnw_ref, lnb_ref,
                    o_ref, z_ref, ys_ref, qs_ref, vs_ref, gs_ref, car_ref, carl_ref, opnd_ref, ptot_ref):
    C, N = RW_CHUNK, RW_N
    NS = zr_ref.shape[0]
    H = zr_ref.shape[2] // N

    low = lax.broadcasted_iota(I32, (C, 2 * N), 1) < N
    inv_n = 1.0 / N

    def head_sums(t):
        s0 = jnp.sum(jnp.where(low, t, 0.0), axis=-1, keepdims=True)
        s1 = jnp.sum(jnp.where(low, 0.0, t), axis=-1, keepdims=True)
        return jnp.where(low, s0, s1)

    def tail():
        for s in range(NS):
            for p in range(H // 2):
                ps = slice(2 * p * N, 2 * (p + 1) * N)
                yp = ys_ref[s, :, ps]
                yc = yp - head_sums(yp) * inv_n
                var = head_sums(yc * yc) * inv_n
                yn = yc * lax.rsqrt(var + RW_GN_EPS) * lnw_ref[:, ps] + lnb_ref[:, ps]
                bonus = head_sums(qs_ref[par, s, :, ps]) * vs_ref[par, s, :, ps].astype(F32)
                o_ref[s, rows, ps] = ((yn + bonus) * gs_ref[par, s, :, ps].astype(F32)).astype(BF16)
                yield

    ri = lax.broadcasted_iota(I32, (C, C), 0)
    ci = lax.broadcasted_iota(I32, (C, C), 1)
    tril = ri >= ci
    stril = ri > ci
    eye = (ri == ci).astype(F32)
    both = jnp.concatenate([stril, tril], axis=0)
    prev = 1 - par
    ops = {}
    for s in range(NS):
        for h in range(H):
            cols = slice(h * N, (h + 1) * N)
            ops[(s, h)] = {name: functools.partial(lambda n, s, cols: opnd_ref[prev, n, s, :, cols], n, s, cols)
                           for n, name in enumerate(RW_OPERANDS)}
            ops[(s, h)]["p_tot"] = functools.partial(lambda s, cols: ptot_ref[prev, s, 0:1, cols], s, cols)
    def head_sums_all(t):
        return jnp.concatenate([head_sums(t[:, 2 * p * N:2 * (p + 1) * N]) for p in range(H // 2)], axis=1)

    def lerp(zz, prev_row, mu):
        return zz + (_shift(zz, prev_row) - zz) * mu

    def prepare():
        for s in range(NS):
            zr, zk, zv, zl = zr_ref[s, rows, :], zk_ref[s, rows, :], zv_ref[s, rows, :], zl_ref[s, rows, :]
            r_f = lerp(zr, car_ref[s, 0:1, :], mu_ref[0:1, :])
            yield
            k_raw = lerp(zk, car_ref[s, 1:2, :], mu_ref[1:2, :])
            yield
            v_f = lerp(zv, car_ref[s, 2:3, :], mu_ref[2:3, :])
            xl = lerp(zl, carl_ref[s, 0:1, :], mul_ref[...])
            car_ref[s, 0:1, :] = zr[C - 1:C, :]
            car_ref[s, 1:2, :] = zk[C - 1:C, :]
            car_ref[s, 2:3, :] = zv[C - 1:C, :]
            carl_ref[s, 0:1, :] = zl[C - 1:C, :]
            yield
            xwa, xg = xl[:, 0:LANES], xl[:, LANES:LANES + g2_ref.shape[0]]
            dw = _dot(jnp.tanh(xwa).astype(BF16), w2_ref[...])
            da = _dot(xwa.astype(BF16), a2_ref[...])
            g_f = _dot(jax.nn.sigmoid(xg).astype(BF16), g2_ref[...])
            yield
            zw = -(w0_ref[...] + dw)
            w_log = -(jnp.maximum(zw, 0.0) + jnp.log(1.0 + jnp.exp(-jnp.abs(zw)))) - 0.5
            yield
            a_sig = jax.nn.sigmoid(a0_ref[...] + da)
            kk = k_raw * kk_ref[...]
            yield
            kk = kk / jnp.maximum(jnp.sqrt(head_sums_all(kk * kk)), 1e-12)
            yield
            k_f = k_raw * (1.0 + (a_sig - 1.0) * ka_ref[...])
            b_f = kk * a_sig
            lw = -jnp.exp(w_log)
            yield
            cs = _dot_exact_lhs(tril.astype(BF16), lw)
            tot = cs[C - 1:C, :]
            yield
            store = lambda name, a: opnd_ref.__setitem__((par, RW_OPERANDS.index(name), s), a.astype(BF16))
            p_inc = jnp.exp(cs)
            store("r_t", r_f * p_inc)
            yield
            p_inv = jnp.exp(-cs)
            store("k_t", k_f * p_inv)
            store("b_t", b_f * p_inv)
            yield
            store("a_t", -kk * jnp.exp(cs - lw))
            yield
            p_rest = jnp.exp(tot - cs)
            store("k_h", k_f * p_rest)
            store("b_h", b_f * p_rest)
            yield
            v_b = v_f.astype(BF16)
            store("v", v_b)
            ptot_ref[par, s, 0:1, :] = jnp.exp(tot)
            qs_ref[par, s] = r_f * k_f * rk_ref[...]
            vs_ref[par, s] = v_b
            gs_ref[par, s] = g_f.astype(BF16)
            yield

    tail_pieces, prep_pieces = tail(), prepare()
    n_pieces = NS * (H // 2 + RW_PREP_PIECES)
    state = dict(ticks=0, done=0)

    def tick():
        state["ticks"] += 1
        due = -(-state["ticks"] * n_pieces // RW_STAGES)
        while state["done"] < due:
            state["done"] += 1
            if next(tail_pieces, "done") == "done":
                next(prep_pieces, None)

    def drain_tail():
        for _ in tail_pieces:
            state["done"] += 1

    _rwkv_heads(ops, stril, tril, both, eye, z_ref, ys_ref, tick, drain_tail)
    for _ in prep_pieces:
        pass


def _rwkv_heads(ops, stril, tril, both, eye, z_ref, ys_ref, tick, drain_tail):
    C, N = RW_CHUNK, RW_N
    hs = list(ops)

    def each(f):
        out = {h: f(h) for h in hs}
        tick()
        return out

    get = lambda name: (lambda h: ops[h][name]())
    a_t, r_t, b_t, k_t, k_h, b_h, p_tot = (get(n) for n in ("a_t", "r_t", "b_t", "k_t", "k_h", "b_h", "p_tot"))
    v = each(lambda h: ops[h]["v"]())
    ar = each(lambda h: jnp.concatenate([a_t(h), r_t(h)], axis=0).astype(BF16))
    m_b = each(lambda h: _dot_nt(ar[h], b_t(h).astype(BF16)))
    m_k = each(lambda h: _dot_nt(ar[h], k_t(h).astype(BF16)))
    l_ab = each(lambda h: jnp.where(stril, m_b[h][:C], 0.0).astype(BF16))
    m_rb = each(lambda h: jnp.where(tril, m_b[h][C:], 0.0).astype(BF16))
    lm_k = each(lambda h: jnp.where(both, m_k[h], 0.0).astype(BF16))
    lmv = each(lambda h: _dot(lm_k[h], v[h]))

    x = each(lambda h: jnp.concatenate([lmv[h][:C], a_t(h).astype(F32)], axis=1))
    y0 = each(lambda h: lmv[h][C:])
    hk = each(lambda h: _dot_tn(k_h(h).astype(BF16), v[h]))
    lp = l_ab
    step = 1
    while step < C:
        x = each(lambda h: x[h] + _dot(lp[h], x[h].astype(BF16)))
        step *= 2
        if step < C:
            lp = each(lambda h: _dot(lp[h], lp[h]).astype(BF16))
    xb = each(lambda h: x[h].astype(BF16))
    yq = each(lambda h: _dot(m_rb[h], xb[h]))
    gh = each(lambda h: _dot_tn(b_h(h).astype(BF16), xb[h]))

    z = each(lambda h: _split2(z_ref[h[0], h[1]]))
    qg = each(lambda h: jnp.concatenate([yq[h][:, N:] + r_t(h).astype(F32),
                                         gh[h][:, N:] + eye * p_tot(h)], axis=0).astype(BF16))
    qgz = each(lambda h: _dot(qg[h], z[h][0]) + _dot(qg[h], z[h][1]))
    yz = each(lambda h: qgz[h][:C])
    gz = each(lambda h: qgz[h][C:])
    drain_tail()
    for s, hd in hs:
        z_ref[s, hd] = (gh[s, hd][:, :N] + hk[s, hd]) + gz[s, hd]
        ys_ref[s, :, hd * N:(hd + 1) * N] = (yq[s, hd][:, :N] + y0[s, hd]) + yz[s, hd]


RW_SEQS_PER_STEP = 1


def _rwkv_chunk(proj, B, L, mu3, mul, w0, a0, k_k, k_a, w2p, a2p, g2p, r_k, ln_w, ln_b):
    T = B * L
    W = w0.shape[1]
    C = RW_CHUNK
    nc = L // C
    H = W // RW_N
    assert H % 2 == 0
    ns = RW_SEQS_PER_STEP if B % RW_SEQS_PER_STEP == 0 else 1
    lw = mul.shape[1]
    proj3 = proj.reshape(B, L, proj.shape[1])
    assert nc % 2 == 0
    np2 = nc // 2
    cols = lambda width, j: pl.BlockSpec((ns, 2 * C, width), lambda bi, i: (bi, jnp.minimum(i, np2 - 1), j))
    full = lambda a: pl.BlockSpec(a.shape, lambda bi, i: (0, 0))
    consts = (mu3, mul, w0, a0, k_k, k_a, w2p, a2p, g2p, r_k, ln_w, ln_b)
    out = pl.pallas_call(
        _rwkv_chunk_kernel,
        grid=(B // ns, np2 + 1),
        in_specs=[cols(W, 1), cols(W, 2), cols(W, 3), cols(lw, 4 * W // lw)] + [full(a) for a in consts],
        out_specs=pl.BlockSpec((ns, 2 * C, W), lambda bi, i: (bi, jnp.maximum(i - 1, 0), 0)),
        out_shape=jax.ShapeDtypeStruct((B, L, W), BF16),
        scratch_shapes=[pltpu.VMEM((ns, H, RW_N, RW_N), F32), pltpu.VMEM((ns, C, W), F32),
                        pltpu.VMEM((2, ns, C, W), F32), pltpu.VMEM((2, ns, C, W), BF16),
                        pltpu.VMEM((2, ns, C, W), BF16),
                        pltpu.VMEM((ns, 8, W), F32), pltpu.VMEM((ns, 8, lw), F32),
                        pltpu.VMEM((2, len(RW_OPERANDS), ns, C, W), BF16), pltpu.VMEM((2, ns, 8, W), F32)],
        compiler_params=_cparams("arbitrary", "arbitrary"),
        name="rwkv_chunk",
    )(proj3, proj3, proj3, proj3, *consts)
    return out.reshape(T, W)


def _first_index_of_max(vals, lane, valid):
    neg = jnp.float32(-jnp.inf)
    masked = jnp.where(valid, vals, neg)
    m = jnp.max(masked, axis=-1, keepdims=True)
    idx = jnp.min(jnp.where(valid & (masked == m), lane, LANES), axis=-1, keepdims=True)
    return m, idx


def _outproj_kernel(s5_ref, rw_ref, x_ref, wt_ref, wb_ref, g_ref, wr_ref, br_ref,
                    h_ref, hn_ref, rec_ref):
    h = x_ref[...] + (_dot(s5_ref[...], wt_ref[...]) + _dot(rw_ref[...], wb_ref[...]))
    h_ref[...] = h
    ms = jnp.mean(h * h, axis=-1, keepdims=True)
    hn = h * lax.rsqrt(ms + RMS_EPS) * g_ref[...]
    hn_ref[...] = _pack_halves(hn).reshape(hn_ref.shape)

    hh, hl = _split2(hn)
    wh, wl = wr_ref[:, :LANES], wr_ref[:, LANES:]
    logits = _dot(hh, wh) + (_dot(hh, wl) + _dot(hl, wh)) + br_ref[...]
    lane = lax.broadcasted_iota(I32, logits.shape, 1)
    is_grp = (lane >= N_EXPERTS) & (lane < N_EXPERTS + N_GROUPS)
    gmax, gidx = _first_index_of_max(logits, lane, is_grp)
    gsum = jnp.sum(jnp.where(is_grp, jnp.exp(logits - gmax), 0.0), axis=-1, keepdims=True)
    p_grp = 1.0 / gsum
    grp = gidx - N_EXPERTS
    in_grp = (lane >= grp * EPG) & (lane < (grp + 1) * EPG)
    m1, i1 = _first_index_of_max(logits, lane, in_grp)
    m2, i2 = _first_index_of_max(logits, lane, in_grp & (lane != i1))
    e = jnp.exp(m2 - m1)
    g1 = p_grp / (1.0 + e)
    g2 = p_grp * e / (1.0 + e)
    rec = jnp.where(lane == 0, i1.astype(F32),
          jnp.where(lane == 1, i2.astype(F32),
          jnp.where(lane == 2, g1, jnp.where(lane == 3, g2, 0.0))))
    rec_ref[...] = rec


def _outproj(s5o, rwo, x2, w_top, w_bot, g, wr, b_route, tm):
    T, D = x2.shape
    W = s5o.shape[1]
    full = lambda a: pl.BlockSpec(a.shape, lambda i: (0, 0))
    return pl.pallas_call(
        _outproj_kernel,
        grid=(T // tm,),
        in_specs=[pl.BlockSpec((tm, W), lambda i: (i, 0)), pl.BlockSpec((tm, W), lambda i: (i, 0)),
                  pl.BlockSpec((tm, D), lambda i: (i, 0)),
                  full(w_top), full(w_bot), full(g), full(wr), full(b_route)],
        out_specs=[pl.BlockSpec((tm, D), lambda i: (i, 0)), pl.BlockSpec((tm, 1, D // 2), lambda i: (i, 0, 0)),
                   pl.BlockSpec((tm, LANES), lambda i: (i, 0))],
        out_shape=[jax.ShapeDtypeStruct((T, D), F32), jax.ShapeDtypeStruct((T, 1, D // 2), U32),
                   jax.ShapeDtypeStruct((T, LANES), F32)],
        compiler_params=_cparams("arbitrary"),
        name="outproj_route",
    )(s5o, rwo, x2, w_top, w_bot, g, wr, b_route)


def _onehots(rec, lane):
    oh0 = (lane == rec[:, 0:1].astype(I32)).astype(F32)
    oh1 = (lane == rec[:, 1:2].astype(I32)).astype(F32)
    return oh0, oh1


def _rank_kernel(rec_ref, rank_ref, cnt_ref, base_ref):
    tb = rec_ref.shape[0]

    @pl.when(pl.program_id(0) == 0)
    def _():
        base_ref[...] = jnp.zeros_like(base_ref)

    lane = lax.broadcasted_iota(I32, (tb, LANES), 1)
    oh0, oh1 = _onehots(rec_ref[...], lane)
    both = oh0 + oh1
    ri = lax.broadcasted_iota(I32, (tb, tb), 0)
    ci = lax.broadcasted_iota(I32, (tb, tb), 1)
    before = _dot((ri > ci).astype(BF16), both.astype(BF16)) + base_ref[0:1, :]
    rank0 = jnp.sum(oh0 * before, axis=-1, keepdims=True)
    rank1 = jnp.sum(oh1 * before, axis=-1, keepdims=True)
    rank_ref[...] = jnp.where(lane == 0, rank0, jnp.where(lane == 1, rank1, 0.0))
    total = base_ref[0:1, :] + jnp.sum(both, axis=0, keepdims=True)
    base_ref[0:1, :] = total
    cnt_ref[...] = jnp.broadcast_to(total, cnt_ref.shape)


def _rank(rec, tb):
    T = rec.shape[0]
    return pl.pallas_call(
        _rank_kernel,
        grid=(T // tb,),
        in_specs=[pl.BlockSpec((tb, LANES), lambda i: (i, 0))],
        out_specs=[pl.BlockSpec((tb, LANES), lambda i: (i, 0)), pl.BlockSpec((8, LANES), lambda i: (0, 0))],
        out_shape=[jax.ShapeDtypeStruct((T, LANES), F32), jax.ShapeDtypeStruct((8, LANES), F32)],
        scratch_shapes=[pltpu.VMEM((8, LANES), F32)],
        compiler_params=_cparams("arbitrary"),
        name="moe_rank",
    )(rec)


def _padded_starts(cnt):
    padded = jnp.ceil(cnt * (1.0 / MOE_BLOCK)) * MOE_BLOCK
    ri = lax.broadcasted_iota(I32, (LANES, LANES), 0)
    ci = lax.broadcasted_iota(I32, (LANES, LANES), 1)
    p8 = jnp.broadcast_to(padded, (8, LANES))
    pend = _dot_exact_lhs_rhs(p8, (ri <= ci).astype(BF16))[0:1, :]
    return pend - padded, pend


def _dot_exact_lhs_rhs(a, b_bf16):
    ah, am, al = _split3(a)
    return _dot(ah, b_bf16) + (_dot(am, b_bf16) + _dot(al, b_bf16))


def _dest_kernel(rec_ref, rank_ref, cnt_ref, dest_ref, blk_ref):
    tb = rec_ref.shape[0]
    cnt = cnt_ref[0:1, :]
    pstart, pend = _padded_starts(cnt)
    lane = lax.broadcasted_iota(I32, (tb, LANES), 1)
    oh0, oh1 = _onehots(rec_ref[...], lane)
    rank = rank_ref[...]
    d0 = jnp.sum(oh0 * pstart, axis=-1, keepdims=True) + rank[:, 0:1]
    d1 = jnp.sum(oh1 * pstart, axis=-1, keepdims=True) + rank[:, 1:2]
    dest_ref[...] = jnp.where(lane == 0, d0, jnp.where(lane == 1, d1, 0.0)).astype(I32)

    @pl.when(pl.program_id(0) == 0)
    def _():
        nb = blk_ref.shape[0]
        blane = lax.broadcasted_iota(I32, (nb, LANES), 1)
        bstart = (lax.broadcasted_iota(I32, (nb, 1), 0) * MOE_BLOCK).astype(F32)
        is_e = blane < N_EXPERTS
        bexp = jnp.sum(jnp.where(is_e & (pend <= bstart), 1.0, 0.0), axis=-1, keepdims=True)
        bexp = jnp.minimum(bexp, N_EXPERTS - 1.0)
        nact = jnp.max(jnp.where(is_e, pend, 0.0), axis=-1, keepdims=True) * (1.0 / MOE_BLOCK)
        lastblk = jnp.where(is_e & (pend > pstart), pend - MOE_BLOCK, -1.0)
        brow = lax.broadcasted_iota(I32, (nb, 1), 0)
        out = jnp.where(blane == 0, bexp, jnp.where(blane == 1, nact, 0.0))
        out = jnp.where(brow == nb - 1, lastblk, out)
        blk_ref[...] = out.astype(I32)


def _dest(rec, rank, cnt, tb, nb_rows):
    T = rec.shape[0]
    return pl.pallas_call(
        _dest_kernel,
        grid=(T // tb,),
        in_specs=[pl.BlockSpec((tb, LANES), lambda i: (i, 0)), pl.BlockSpec((tb, LANES), lambda i: (i, 0)),
                  pl.BlockSpec((8, LANES), lambda i: (0, 0))],
        out_specs=[pl.BlockSpec((tb, LANES), lambda i: (i, 0)), pl.BlockSpec((nb_rows, LANES), lambda i: (0, 0))],
        out_shape=[jax.ShapeDtypeStruct((T, LANES), I32), jax.ShapeDtypeStruct((nb_rows, LANES), I32)],
        compiler_params=_cparams("arbitrary"),
        name="moe_dest",
    )(rec, rank, cnt)


def _scatter_kernel(pad_ref, na_ref, dest_ref, hn_ref, xs_ref, zbuf_ref, idx_ref, sa_ref, sb_ref,
                    sem_a, sem_b, isem, zsem, *, tb, n_blocks, n_tail, n_steps):
    i = pl.program_id(0)

    @pl.when(i == 0)
    def _():
        zbuf_ref[...] = jnp.zeros_like(zbuf_ref)

        def zero_rows(start):
            return pltpu.make_async_copy(zbuf_ref, xs_ref.at[pl.ds(start, MOE_BLOCK)], zsem)

        def pad_start(e, c):
            @pl.when(pad_ref[e] >= 0)
            def _():
                zero_rows(pad_ref[e]).start()
            return c

        def pad_wait(e, c):
            @pl.when(pad_ref[e] >= 0)
            def _():
                zero_rows(pad_ref[e]).wait()
            return c

        def tail_start(k, c):
            @pl.when(na_ref[0] + k < n_blocks)
            def _():
                zero_rows((na_ref[0] + k) * MOE_BLOCK).start()
            return c

        def tail_wait(k, c):
            @pl.when(na_ref[0] + k < n_blocks)
            def _():
                zero_rows((na_ref[0] + k) * MOE_BLOCK).wait()
            return c

        lax.fori_loop(0, N_EXPERTS, pad_start, 0)
        lax.fori_loop(0, n_tail, tail_start, 0)
        lax.fori_loop(0, N_EXPERTS, pad_wait, 0)
        lax.fori_loop(0, n_tail, tail_wait, 0)

    def idx_copy(step, slot):
        return pltpu.make_async_copy(dest_ref.at[step], idx_ref.at[pl.ds(slot * (2 * tb), 2 * tb)], isem)

    @pl.when(i == 0)
    def _():
        idx_copy(0, 0).start()

    slot = i % 2
    idx_copy(i, slot).wait()

    @pl.when(i + 1 < n_steps)
    def _():
        idx_copy(i + 1, 1 - slot).start()

    hb = tb // 2
    for half, (buf, sem) in enumerate(((sa_ref, sem_a), (sb_ref, sem_b))):
        def wait_rows(buf=buf, sem=sem):
            for _ in range(2):
                pltpu.make_async_copy(buf, xs_ref.at[pl.ds(0, hb)], sem).wait()

        @pl.when(i > 0)
        def _():
            wait_rows()

        buf[...] = hn_ref[half * hb:(half + 1) * hb]

        def issue(t, c, buf=buf, sem=sem, half=half):
            for s in range(2):
                d = idx_ref[slot * (2 * tb) + 2 * (half * hb + t) + s]
                pltpu.make_async_copy(buf.at[pl.ds(t, 1)], xs_ref.at[pl.ds(d, 1)], sem).start(priority=s)
            return c

        lax.fori_loop(0, hb, issue, 0, unroll=8)

    @pl.when(i == n_steps - 1)
    def _():
        for buf, sem in ((sa_ref, sem_a), (sb_ref, sem_b)):
            for _ in range(2):
                pltpu.make_async_copy(buf, xs_ref.at[pl.ds(0, hb)], sem).wait()


def _scatter(lastblk, nact, dest2, hn, tb, n_blocks):
    T, _, D = hn.shape
    n_tail = n_blocks - (-(-2 * T // MOE_BLOCK))
    hb = tb // 2
    return pl.pallas_call(
        functools.partial(_scatter_kernel, tb=tb, n_blocks=n_blocks, n_tail=n_tail, n_steps=T // tb),
        grid_spec=pltpu.PrefetchScalarGridSpec(
            num_scalar_prefetch=2, grid=(T // tb,),
            in_specs=[pl.BlockSpec(memory_space=pl.ANY),
                      pl.BlockSpec((tb, 1, D), lambda i, pad, na: (i, 0, 0))],
            out_specs=pl.BlockSpec(memory_space=pl.ANY),
            scratch_shapes=[pltpu.VMEM((MOE_BLOCK, 1, D), hn.dtype), pltpu.SMEM((4 * tb,), I32),
                            pltpu.VMEM((hb, 1, D), hn.dtype), pltpu.VMEM((hb, 1, D), hn.dtype),
                            pltpu.SemaphoreType.DMA, pltpu.SemaphoreType.DMA,
                            pltpu.SemaphoreType.DMA, pltpu.SemaphoreType.DMA]),
        out_shape=jax.ShapeDtypeStruct((n_blocks * MOE_BLOCK, 1, D), hn.dtype),
        compiler_params=_cparams("arbitrary"),
        name="moe_scatter",
    )(lastblk, nact, dest2, hn)


W_SLOTS = 2


def _expert_kernel(be_ref, na_ref, x_ref, wg_hbm, wu_hbm, wd_hbm, y_ref,
                   wg_buf, wu_buf, wd_buf, wgb_ref, wub_ref, wdb_ref, x2_ref, elist_ref, state_ref, sems):
    j = pl.program_id(0)
    na = na_ref[0]

    def weight_copies(e, slot):
        return [pltpu.make_async_copy(src.at[e], buf.at[slot], sems.at[slot, n])
                for n, (src, buf) in enumerate(((wg_hbm, wg_buf), (wu_hbm, wu_buf), (wd_hbm, wd_buf)))]

    @pl.when(j == 0)
    def _():
        def scan(k, n):
            is_new = (k == 0) | (be_ref[k] != be_ref[jnp.maximum(k - 1, 0)])

            @pl.when(is_new)
            def _():
                elist_ref[n] = be_ref[k]
            return n + is_new.astype(I32)

        n_exp = lax.fori_loop(0, na, scan, 0)
        state_ref[0] = 0
        state_ref[1] = n_exp
        for n in range(W_SLOTS):
            @pl.when(n < n_exp)
            def _():
                for cp in weight_copies(elist_ref[n], n):
                    cp.start()

    @pl.when(j < na)
    def _():
        @pl.when((j == 0) | (be_ref[j] != be_ref[jnp.maximum(j - 1, 0)]))
        def _():
            n = state_ref[0]
            slot = n % W_SLOTS
            for cp in weight_copies(elist_ref[n], slot):
                cp.wait()
            wgb_ref[...] = wg_buf[slot].astype(BF16)
            wub_ref[...] = wu_buf[slot].astype(BF16)
            wdb_ref[...] = wd_buf[slot].astype(BF16)

            @pl.when(n + W_SLOTS < state_ref[1])
            def _():
                for cp in weight_copies(elist_ref[n + W_SLOTS], slot):
                    cp.start()

            state_ref[0] = n + 1

        x2_ref[...] = x_ref[...].reshape(x2_ref.shape)
        lo, hi = _unpack_halves(x2_ref[...])
        lo, hi = lo.astype(BF16), hi.astype(BF16)
        half = lo.shape[1]
        hg = _dot(lo, wgb_ref[:half, :]) + _dot(hi, wgb_ref[half:, :])
        hu = _dot(lo, wub_ref[:half, :]) + _dot(hi, wub_ref[half:, :])
        act = (hg * jax.nn.sigmoid(hg)) * hu
        y = _dot(act.astype(BF16), wdb_ref[...])
        y_ref[...] = _pack_halves(y).reshape(y_ref.shape)

    @pl.when(j >= na)
    def _():
        y_ref[...] = jnp.zeros_like(y_ref)


def _experts(blk_exp, nact, xs, w_gate, w_up, w_down, n_blocks):
    DP = xs.shape[2]
    D = 2 * DP
    rows = n_blocks * MOE_BLOCK
    DE = w_gate.shape[2]
    blk = lambda j, be, na: (jnp.minimum(j, na[0] - 1), 0, 0)
    hbm = pl.BlockSpec(memory_space=pl.ANY)
    return pl.pallas_call(
        _expert_kernel,
        grid_spec=pltpu.PrefetchScalarGridSpec(
            num_scalar_prefetch=2, grid=(n_blocks,),
            in_specs=[pl.BlockSpec((MOE_BLOCK, 1, DP), blk), hbm, hbm, hbm],
            out_specs=pl.BlockSpec((MOE_BLOCK, 1, DP), lambda j, be, na: (j, 0, 0)),
            scratch_shapes=[pltpu.VMEM((W_SLOTS, D, DE), F32), pltpu.VMEM((W_SLOTS, D, DE), F32),
                            pltpu.VMEM((W_SLOTS, DE, D), F32),
                            pltpu.VMEM((D, DE), BF16), pltpu.VMEM((D, DE), BF16), pltpu.VMEM((DE, D), BF16),
                            pltpu.VMEM((MOE_BLOCK, DP), U32), pltpu.SMEM((N_EXPERTS,), I32), pltpu.SMEM((2,), I32),
                            pltpu.SemaphoreType.DMA((W_SLOTS, 3))]),
        out_shape=jax.ShapeDtypeStruct((rows, 1, DP), U32),
        compiler_params=_cparams("arbitrary"),
        name="moe_experts",
    )(blk_exp, nact, xs, w_gate, w_up, w_down)


def _combine_kernel(dest_ref, ys_ref, h_ref, rec_ref, g_ref, o_ref, idx_ref, ya0_ref, ya1_ref, yb0_ref, yb1_ref,
                    y2_ref, sem_a, sem_b, isem, *, normalize, n_steps):
    i = pl.program_id(0)
    tb = h_ref.shape[0]
    hb = tb // 2
    halves = ((ya0_ref, ya1_ref, sem_a), (yb0_ref, yb1_ref, sem_b))

    def idx_copy(step, slot):
        return pltpu.make_async_copy(dest_ref.at[step], idx_ref.at[pl.ds(slot * (2 * tb), 2 * tb)], isem)

    def issue(half, slot):
        bufs, sem = halves[half][:2], halves[half][2]

        def body(t, c):
            for s in range(2):
                d = idx_ref[slot * (2 * tb) + 2 * (half * hb + t) + s]
                pltpu.make_async_copy(ys_ref.at[pl.ds(d, 1)], bufs[s].at[pl.ds(t, 1)], sem).start(priority=s)
            return c

        lax.fori_loop(0, hb, body, 0, unroll=8)

    @pl.when(i == 0)
    def _():
        first = idx_copy(0, 0)
        first.start()
        first.wait()
        issue(0, 0)
        issue(1, 0)
        if n_steps > 1:
            idx_copy(1, 1).start()

    nslot = (i + 1) % 2

    @pl.when(i + 1 < n_steps)
    def _():
        idx_copy(i + 1, nslot).wait()

    rec = rec_ref[...]
    for half, (y0_ref, y1_ref, sem) in enumerate(halves):
        rows = slice(half * hb, (half + 1) * hb)
        pltpu.make_async_copy(ys_ref.at[pl.ds(0, hb)], y0_ref, sem).wait()
        pltpu.make_async_copy(ys_ref.at[pl.ds(0, hb)], y1_ref, sem).wait()
        y2_ref[...] = y0_ref[...].reshape(y2_ref.shape)
        h = h_ref[rows, :] + rec[rows, 2:3] * jnp.concatenate(_unpack_halves(y2_ref[...]), axis=1)
        y2_ref[...] = y1_ref[...].reshape(y2_ref.shape)
        h = h + rec[rows, 3:4] * jnp.concatenate(_unpack_halves(y2_ref[...]), axis=1)
        if normalize:
            ms = jnp.mean(h * h, axis=-1, keepdims=True)
            h = h * lax.rsqrt(ms + RMS_EPS) * g_ref[...]
        o_ref[rows, :] = h

        @pl.when(i + 1 < n_steps)
        def _():
            issue(half, nslot)

    @pl.when(i + 2 < n_steps)
    def _():
        idx_copy(i + 2, i % 2).start()


def _combine(dest2, ys, h, rec, g, tb, normalize):
    T, D = h.shape
    hb = tb // 2
    return pl.pallas_call(
        functools.partial(_combine_kernel, normalize=normalize, n_steps=T // tb),
        grid=(T // tb,),
        in_specs=[pl.BlockSpec(memory_space=pl.ANY), pl.BlockSpec(memory_space=pl.ANY),
                  pl.BlockSpec((tb, D), lambda i: (i, 0)), pl.BlockSpec((tb, LANES), lambda i: (i, 0)),
                  pl.BlockSpec((1, D), lambda i: (0, 0))],
        out_specs=pl.BlockSpec((tb, D), lambda i: (i, 0)),
        out_shape=jax.ShapeDtypeStruct((T, D), F32),
        scratch_shapes=[pltpu.SMEM((4 * tb,), I32)] + [pltpu.VMEM((hb, 1, D // 2), U32)] * 4 +
                       [pltpu.VMEM((hb, D // 2), U32)] + [pltpu.SemaphoreType.DMA] * 3,
        compiler_params=_cparams("arbitrary"),
        name="moe_combine",
    )(dest2, ys, h, rec, g)


def _pad_cols(a, n):
    return jnp.pad(a, ((0, 0), (0, n - a.shape[1])))


def _pad_rows(a, n, at=0):
    return jnp.pad(a, ((at, n - a.shape[0] - at), (0, 0)))


def _layer(x2, B, L, p):
    T, D = x2.shape
    W = p["s5_d"].shape[0]
    G, P = p["s5_lambda_re"].shape
    HG = W // G
    dl, al, gl = p["rwkv_w2"].shape[0], p["rwkv_a2"].shape[0], p["rwkv_g2"].shape[0]
    H = W // RW_N

    assert dl + al == LANES and gl <= RW_LORA_PAD - LANES
    w_in_r = _pad_cols(p["w_in"], 4 * W + RW_LORA_PAD).astype(BF16)
    tm_in = min(1024, T)
    proj = _inproj(x2, p["norm_mix_g"].reshape(1, D), w_in_r, tm_in, 1152)

    C = S5_CHUNK
    nc = L // C
    ct_re = jnp.swapaxes(p["s5_c_re"], 1, 2)
    ct_im = jnp.swapaxes(p["s5_c_im"], 1, 2)
    rb = min(LANES, B * nc)
    z = _s5_in(proj.reshape(B * nc, C, proj.shape[1]), G, rb)
    y_t = _s5_chunk(z, p["s5_lambda_re"], p["s5_lambda_im"], p["s5_log_step"],
                    ct_re, ct_im, p["s5_b_re"], p["s5_b_im"], p["s5_c_re"], p["s5_c_im"], nc)
    y_ssm = _s5_out(y_t, rb).reshape(T, W)
    tm = min(512, T)
    s5_out = _s5_glu(y_ssm, proj, p["s5_d"].reshape(1, W), p["s5_w_glu"].astype(BF16),
                     p["s5_b_glu"].reshape(1, W), tm)

    mu = p["rwkv_mu"]
    mu3 = _pad_rows(mu[:3 * W].reshape(3, W), 8)
    mul = _pad_cols(mu[None, 3 * W:], RW_LORA_PAD)
    w2p = _pad_rows(p["rwkv_w2"], LANES).astype(BF16)
    a2p = _pad_rows(p["rwkv_a2"], LANES, at=dl).astype(BF16)
    g2p = _pad_rows(p["rwkv_g2"], -(-gl // LANES) * LANES).astype(BF16)
    row = lambda a: a.reshape(1, W)
    rw_out = _rwkv_chunk(proj, B, L, mu3, mul, row(p["rwkv_w0"]), row(p["rwkv_a0"]), row(p["rwkv_k_k"]),
                         row(p["rwkv_k_a"]), w2p, a2p, g2p, row(p["rwkv_r_k"]), row(p["rwkv_ln_w"]),
                         row(p["rwkv_ln_b"]))

    w_out = p["w_out"].astype(BF16)
    w_route = jnp.concatenate([p["w_route_exp"], p["w_route_grp"]], axis=1)
    w_route = _pad_cols(w_route, LANES)
    wr_hi = w_route.astype(BF16)
    wr_lo = (w_route - wr_hi.astype(F32)).astype(BF16)
    wr = jnp.concatenate([wr_hi, wr_lo], axis=1)
    b_route = _pad_cols(jnp.concatenate([p["b_route_exp"], p["b_route_grp"]])[None, :], LANES)
    tm_o = min(512, T)
    h, hn, rec = _outproj(s5_out, rw_out, x2, w_out[:W], w_out[W:], p["norm_ffn_g"].reshape(1, D),
                          wr, b_route, tm_o)

    tb = min(1024, T)
    rank, cnt = _rank(rec, tb)
    n_blocks = -(-2 * T // MOE_BLOCK) + N_EXPERTS
    nb_rows = -(-(n_blocks + 1) // 8) * 8
    dest, blk = _dest(rec, rank, cnt, tb, nb_rows)
    blk_exp = blk[:n_blocks, 0]
    nact = blk[0:1, 1]
    lastblk = blk[nb_rows - 1, :N_EXPERTS]
    ts = min(1024, T)
    dest2 = dest[:, :2].reshape(T // ts, 2 * ts)

    xs = _scatter(lastblk, nact, dest2, hn, ts, n_blocks)
    ys = _experts(blk_exp, nact, xs, p["w_gate"], p["w_up"], p["w_down"], n_blocks)
    return dest2, ys, h, rec, ts


def kernel(x, norm_mix_g, w_in, s5_lambda_re, s5_lambda_im, s5_log_step, s5_b_re, s5_b_im, s5_c_re, s5_c_im, s5_d, s5_w_glu, s5_b_glu, rwkv_mu, rwkv_w0, rwkv_w2, rwkv_a0, rwkv_a2, rwkv_g2, rwkv_k_k, rwkv_k_a, rwkv_r_k, rwkv_ln_w, rwkv_ln_b, w_out, norm_ffn_g, w_route_grp, b_route_grp, w_route_exp, b_route_exp, w_gate, w_up, w_down, norm_final_g):
    B, L, D = x.shape
    params = dict(
        norm_mix_g=norm_mix_g, w_in=w_in, s5_lambda_re=s5_lambda_re, s5_lambda_im=s5_lambda_im,
        s5_log_step=s5_log_step, s5_b_re=s5_b_re, s5_b_im=s5_b_im, s5_c_re=s5_c_re, s5_c_im=s5_c_im,
        s5_d=s5_d, s5_w_glu=s5_w_glu, s5_b_glu=s5_b_glu, rwkv_mu=rwkv_mu, rwkv_w0=rwkv_w0, rwkv_w2=rwkv_w2,
        rwkv_a0=rwkv_a0, rwkv_a2=rwkv_a2, rwkv_g2=rwkv_g2, rwkv_k_k=rwkv_k_k, rwkv_k_a=rwkv_k_a,
        rwkv_r_k=rwkv_r_k, rwkv_ln_w=rwkv_ln_w, rwkv_ln_b=rwkv_ln_b, w_out=w_out, norm_ffn_g=norm_ffn_g,
        w_route_grp=w_route_grp, b_route_grp=b_route_grp, w_route_exp=w_route_exp, b_route_exp=b_route_exp,
        w_gate=w_gate, w_up=w_up, w_down=w_down)
    depth = norm_mix_g.shape[0]
    h2 = x.reshape(B * L, D)
    for l in range(depth):
        p = {k_: v_[l] for k_, v_ in params.items()}
        dest2, ys, h, rec, ts = _layer(h2, B, L, p)
        h2 = _combine(dest2, ys, h, rec, norm_final_g.reshape(1, D), ts, normalize=(l == depth - 1))
    return h2.reshape(B, L, D)
```

```python
import functools

import jax
import jax.numpy as jnp
from jax import lax
from jax.experimental import pallas as pl
from jax.experimental.pallas import tpu as pltpu

F32 = jnp.float32
BF16 = jnp.bfloat16
I32 = jnp.int32

RMS_EPS = 1e-6
S5_CHUNK = 64
S5_HG = 16
S5_P = 64
RW_N = 64
RW_CHUNK = 64
RW_GN_EPS = 64e-5
N_GROUPS = 8
EPG = 8
N_EXPERTS = 64
MOE_BLOCK = 256
LANES = 128
INPROJ_N_TILES = 4
VMEM_LIMIT = 56 * 1024 * 1024


def _cparams(*sem, **kw):
    return pltpu.CompilerParams(dimension_semantics=sem, vmem_limit_bytes=VMEM_LIMIT, **kw)


def _split2(x):
    hi = x.astype(BF16)
    lo = (x - hi.astype(F32)).astype(BF16)
    return hi, lo


def _split3(x):
    hi = x.astype(BF16)
    r = x - hi.astype(F32)
    mid = r.astype(BF16)
    lo = (r - mid.astype(F32)).astype(BF16)
    return hi, mid, lo


def _dot(a, b):
    return jnp.dot(a, b, preferred_element_type=F32)


def _dot_nt(a, b):
    return lax.dot_general(a, b, (((1,), (1,)), ((), ())), preferred_element_type=F32)


def _dot_tn(a, b):
    return lax.dot_general(a, b, (((0,), (0,)), ((), ())), preferred_element_type=F32)


def _dot_x3(a, b):
    ah, al = _split2(a)
    bh, bl = _split2(b)
    return _dot(ah, bh) + (_dot(ah, bl) + _dot(al, bh))


def _dot_exact_lhs(a_bf16, b):
    bh, bm, bl = _split3(b)
    return _dot(a_bf16, bh) + (_dot(a_bf16, bm) + _dot(a_bf16, bl))


def _cmul(ar, ai, br, bi):
    return ar * br - ai * bi, ar * bi + ai * br


U32 = jnp.uint32


def _pack_halves(x):
    half = x.shape[1] // 2
    lo = lax.bitcast_convert_type(x[:, :half].astype(BF16).astype(F32), U32)
    hi = lax.bitcast_convert_type(x[:, half:].astype(BF16).astype(F32), U32)
    return (lo >> 16) | hi


def _unpack_halves(p):
    lo = lax.bitcast_convert_type(p << 16, F32)
    hi = lax.bitcast_convert_type(p & jnp.uint32(0xFFFF0000), F32)
    return lo, hi


def _inproj_kernel(x_ref, g_ref, w_ref, o_ref, hn_ref):
    @pl.when(pl.program_id(1) == 0)
    def _():
        x = x_ref[...]
        ms = jnp.mean(x * x, axis=-1, keepdims=True)
        hn_ref[...] = (x * lax.rsqrt(ms + RMS_EPS) * g_ref[...]).astype(BF16)

    o_ref[...] = _dot(hn_ref[...], w_ref[...])


def _inproj(x2, g, w_bf16, tm, tn):
    T, D = x2.shape
    N = w_bf16.shape[1]
    return pl.pallas_call(
        _inproj_kernel,
        grid=(T // tm, N // tn),
        in_specs=[
            pl.BlockSpec((tm, D), lambda i, j: (i, 0)),
            pl.BlockSpec((1, D), lambda i, j: (0, 0)),
            pl.BlockSpec((D, tn), lambda i, j: (0, j)),
        ],
        out_specs=pl.BlockSpec((tm, tn), lambda i, j: (i, j)),
        out_shape=jax.ShapeDtypeStruct((T, N), F32),
        scratch_shapes=[pltpu.VMEM((tm, D), BF16)],
        compiler_params=_cparams("arbitrary", "arbitrary"),
        name="inproj",
    )(x2, g, w_bf16)


def _binpow(ar, ai, expo, nbits):
    pr = jnp.ones(expo.shape, F32)
    pi = jnp.zeros(expo.shape, F32)
    sr, si = ar, ai
    for bit in range(nbits):
        m = ((expo >> bit) & 1) == 1
        nr, ni = _cmul(pr, pi, sr, si)
        pr = jnp.where(m, nr, pr)
        pi = jnp.where(m, ni, pi)
        if bit + 1 < nbits:
            sr, si = _cmul(sr, si, sr, si)
    return pr, pi


def _frame_powers(ar, ai, reverse, plus_one):
    C, HG, P = S5_CHUNK, S5_HG, ar.shape[0]
    fpt = LANES // HG
    j = lax.broadcasted_iota(I32, (P, LANES), 1) // HG
    inner_r, inner_i = _binpow(ar, ai, (fpt - 1 - j) if reverse else j, (fpt - 1).bit_length())
    sr, si = ar, ai
    for _ in range(fpt.bit_length() - 1):
        sr, si = _cmul(sr, si, sr, si)
    outer = [(ar, ai) if plus_one else (jnp.ones_like(ar), jnp.zeros_like(ai))]
    for _ in range(C // fpt - 1):
        outer.append(_cmul(outer[-1][0], outer[-1][1], sr, si))
    if reverse:
        outer = outer[::-1]
    tiles = [_cmul(inner_r, inner_i, o_r, o_i) for o_r, o_i in outer]
    return (jnp.concatenate([t[0] for t in tiles], axis=1), jnp.concatenate([t[1] for t in tiles], axis=1))


def _zoh(lr, li, dt):
    mag = jnp.exp(lr * dt)
    ang = li * dt
    ar, ai = mag * jnp.cos(ang), mag * jnp.sin(ang)
    den = lr * lr + li * li
    nr, ni = ar - 1.0, ai
    fr = (nr * lr + ni * li) / den
    fi = (ni * lr - nr * li) / den
    return ar, ai, fr, fi


def _s5_ops_group(lrc_ref, lic_ref, dt_ref, ct_re_ref, ct_im_ref, bt_re_ref, bt_im_ref, c_re_ref, c_im_ref,
                  m_ref, wbt_ref, wct_ref, a64_ref):
    C, HG, P = S5_CHUNK, S5_HG, S5_P
    W = C * HG
    dt = jnp.exp(dt_ref[...])
    ar_c, ai_c, fr_c, fi_c = _zoh(lrc_ref[...], lic_ref[...], dt)

    pr, pi = _frame_powers(ar_c, ai_c, False, True)
    qr, qi = _frame_powers(ar_c, ai_c, True, False)

    sel = (lax.broadcasted_iota(I32, (HG, W), 1) % HG == lax.broadcasted_iota(I32, (HG, W), 0)).astype(BF16)
    tile = lambda a: _dot_exact_lhs_rhs(a, sel)

    c1_re, c1_im = _cmul(tile(ct_re_ref[...]), tile(ct_im_ref[...]), pr, pi)
    wct_ref[...] = jnp.concatenate([c1_re, -c1_im], axis=0).T.astype(BF16)

    bb_re, bb_im = _cmul(bt_re_ref[...], bt_im_ref[...], fr_c, fi_c)
    ab_re, ab_im = _cmul(tile(bb_re), tile(bb_im), qr, qi)
    wbt_ref[:P, :] = ab_re.astype(BF16)
    wbt_ref[P:, :] = ab_im.astype(BF16)

    cr, ci = ar_c, ai_c
    for _ in range(C.bit_length() - 1):
        cr, ci = _cmul(cr, ci, cr, ci)
    a64_ref[...] = jnp.concatenate([cr, ci], axis=0)

    strip = _dot_x3(c_re_ref[...], ab_re) - _dot_x3(c_im_ref[...], ab_im)
    ext = jnp.concatenate([strip, jnp.zeros((HG, W), F32)], axis=1)
    for t in range(C):
        off = (C - 1 - t) * HG
        m_ref[t * HG:(t + 1) * HG, :] = ext[:, off:off + W].astype(BF16)


S5_FB = 8


def _s5_in_kernel(x_ref, z_ref):
    G, _, rb = z_ref.shape
    for f in range(S5_FB):
        xt = x_ref[:, f, :].T
        z_ref[:, f * S5_HG:(f + 1) * S5_HG, :] = xt.reshape(G, S5_HG, rb).astype(BF16)


def _s5_in(proj3, G, rb):
    R, C, _ = proj3.shape
    W = G * S5_HG
    return pl.pallas_call(
        _s5_in_kernel,
        grid=(R // rb, C // S5_FB),
        in_specs=[pl.BlockSpec((rb, S5_FB, W), lambda i, j: (i, j, 0))],
        out_specs=pl.BlockSpec((G, S5_FB * S5_HG, rb), lambda i, j: (0, j, i)),
        out_shape=jax.ShapeDtypeStruct((G, C * S5_HG, R), BF16),
        compiler_params=_cparams("arbitrary", "arbitrary"),
        name="s5_in",
    )(proj3)


def _s5_out_kernel(yt_ref, o_ref):
    G, _, rb = yt_ref.shape
    for f in range(S5_FB):
        slab = yt_ref[:, f * S5_HG:(f + 1) * S5_HG, :].reshape(G * S5_HG, rb)
        o_ref[:, f, :] = slab.T


def _s5_out(yt, rb):
    G, CW, R = yt.shape
    C = CW // S5_HG
    W = G * S5_HG
    return pl.pallas_call(
        _s5_out_kernel,
        grid=(R // rb, C // S5_FB),
        in_specs=[pl.BlockSpec((G, S5_FB * S5_HG, rb), lambda i, j: (0, j, i))],
        out_specs=pl.BlockSpec((rb, S5_FB, W), lambda i, j: (i, j, 0)),
        out_shape=jax.ShapeDtypeStruct((R, C, W), F32),
        compiler_params=_cparams("arbitrary", "arbitrary"),
        name="s5_out",
    )(yt)


def _s5_chunk_kernel(*refs, n_chunks):
    params, (z_ref, y_ref), ops = refs[:9], refs[9:11], refs[11:]
    for g in range(z_ref.shape[0]):
        at = lambda rs: [r.at[g] for r in rs]
        _s5_ops_group(*at(params), *at(ops))
        _s5_chunk_group(z_ref.at[g], *at(ops), y_ref.at[g], n_chunks)


def _s5_chunk_group(z_ref, m_ref, wbt_ref, wct_ref, a64_ref, y_ref, n_chunks):
    P = S5_P
    z = z_ref[...]
    R = z.shape[1]
    x = _dot(wbt_ref[...], z)
    y_ref[...] = _dot(m_ref[...], z)
    xr, xi = x[:P], x[P:]
    a = a64_ref[...]
    ar, ai = a[:P], a[P:]
    cidx = lax.broadcasted_iota(I32, (1, R), 1) % n_chunks
    shift = 1
    while shift < n_chunks:
        keep = cidx >= shift
        sr = jnp.where(keep, pltpu.roll(xr, shift, 1), 0.0)
        si = jnp.where(keep, pltpu.roll(xi, shift, 1), 0.0)
        xr, xi = xr + (ar * sr - ai * si), xi + (ar * si + ai * sr)
        ar, ai = ar * ar - ai * ai, 2.0 * ar * ai
        shift *= 2
    keep = cidx >= 1
    s_in = jnp.concatenate([jnp.where(keep, pltpu.roll(xr, 1, 1), 0.0),
                            jnp.where(keep, pltpu.roll(xi, 1, 1), 0.0)], axis=0)
    sh, sl = _split2(s_in)
    wct = wct_ref[...]
    y_ref[...] = y_ref[...] + (_dot(wct, sh) + _dot(wct, sl))


def _s5_chunk(z, lam_re, lam_im, log_step, ct_re, ct_im, bt_re, bt_im, c_re, c_im, n_chunks):
    G, W, R = z.shape
    P, HG = lam_re.shape[1], S5_HG
    P2 = 2 * P
    gps = 2 if G % 2 == 0 else 1
    col = lambda a: a.reshape(G, P, 1)
    g3 = lambda s1, s2: pl.BlockSpec((gps, s1, s2), lambda g: (g, 0, 0))
    return pl.pallas_call(
        functools.partial(_s5_chunk_kernel, n_chunks=n_chunks),
        grid=(G // gps,),
        in_specs=[g3(P, 1), g3(P, 1), g3(1, 1),
                  g3(P, HG), g3(P, HG), g3(P, HG), g3(P, HG), g3(HG, P), g3(HG, P), g3(W, R)],
        out_specs=g3(W, R),
        out_shape=jax.ShapeDtypeStruct((G, W, R), F32),
        scratch_shapes=[pltpu.VMEM((gps, W, W), BF16), pltpu.VMEM((gps, P2, W), BF16),
                        pltpu.VMEM((gps, W, P2), BF16), pltpu.VMEM((gps, P2, 1), F32)],
        compiler_params=_cparams("arbitrary"),
        name="s5_chunk",
    )(col(lam_re), col(lam_im), log_step.reshape(G, 1, 1), ct_re, ct_im, bt_re, bt_im, c_re, c_im, z)


def _gelu_tanh(x):
    return 0.5 * x * (1.0 + jnp.tanh(0.7978845608028654 * (x + 0.044715 * (x * x * x))))


def _s5_glu_kernel(y_ref, u_ref, d_ref, w_ref, b_ref, o_ref):
    y = y_ref[...] + d_ref[...] * u_ref[...]
    y = _gelu_tanh(y)
    z = _dot(y.astype(BF16), w_ref[...]) + b_ref[...]
    o_ref[...] = (y * jax.nn.sigmoid(z)).astype(BF16)


def _s5_glu(y_ssm, proj, d, w_bf16, b, tm):
    T, W = y_ssm.shape
    return pl.pallas_call(
        _s5_glu_kernel,
        grid=(T // tm,),
        in_specs=[
            pl.BlockSpec((tm, W), lambda i: (i, 0)),
            pl.BlockSpec((tm, W), lambda i: (i, 0)),
            pl.BlockSpec((1, W), lambda i: (0, 0)),
            pl.BlockSpec((W, W), lambda i: (0, 0)),
            pl.BlockSpec((1, W), lambda i: (0, 0)),
        ],
        out_specs=pl.BlockSpec((tm, W), lambda i: (i, 0)),
        out_shape=jax.ShapeDtypeStruct((T, W), BF16),
        compiler_params=_cparams("arbitrary"),
        name="s5_glu",
    )(y_ssm, proj, d, w_bf16, b)


def _shift(z, prev_row):
    rolled = pltpu.roll(z, 1, 0)
    first = lax.broadcasted_iota(I32, (z.shape[0], 1), 0) == 0
    return jnp.where(first, prev_row, rolled)


RW_OPERANDS = ("r_t", "k_t", "a_t", "b_t", "k_h", "b_h", "v")
RW_LORA_PAD = 512
RW_PREP_PIECES = 14
RW_STAGES = 24


def _rwkv_chunk_kernel(zr_ref, zk_ref, zv_ref, zl_ref, mu_ref, mul_ref, w0_ref, a0_ref, kk_ref, ka_ref,
                       w2_ref, a2_ref, g2_ref, rk_ref, lnw_ref, lnb_ref,
                       o_ref, z_ref, ys_ref, qs_ref, vs_ref, gs_ref, car_ref, carl_ref, opnd_ref, ptot_ref):
    @pl.when(pl.program_id(1) == 0)
    def _():
        for ref in (z_ref, ys_ref, qs_ref, vs_ref, gs_ref, car_ref, carl_ref, opnd_ref, ptot_ref):
            ref[...] = jnp.zeros_like(ref)

    C = RW_CHUNK
    refs = (zr_ref, zk_ref, zv_ref, zl_ref, mu_ref, mul_ref, w0_ref, a0_ref, kk_ref, ka_ref,
            w2_ref, a2_ref, g2_ref, rk_ref, lnw_ref, lnb_ref,
            o_ref, z_ref, ys_ref, qs_ref, vs_ref, gs_ref, car_ref, carl_ref, opnd_ref, ptot_ref)
    for half in range(2):
        _rwkv_half_step(half, slice(half * C, (half + 1) * C), *refs)


def _rwkv_half_step(par, rows, zr_ref, zk_ref, zv_ref, zl_ref, mu_ref, mul_ref, w0_ref, a0_ref, kk_ref, ka_ref,
                    w2_ref, a2_ref, g2_ref, rk_ref, lnw_ref, lnb_ref,
                    o_ref, z_ref, ys_ref, qs_ref, vs_ref, gs_ref, car_ref, carl_ref, opnd_ref, ptot_ref):
    C, N = RW_CHUNK, RW_N
    NS = zr_ref.shape[0]
    H = zr_ref.shape[2] // N

    low = lax.broadcasted_iota(I32, (C, 2 * N), 1) < N
    inv_n = 1.0 / N

    def head_sums(t):
        s0 = jnp.sum(jnp.where(low, t, 0.0), axis=-1, keepdims=True)
        s1 = jnp.sum(jnp.where(low, 0.0, t), axis=-1, keepdims=True)
        return jnp.where(low, s0, s1)

    def tail():
        for s in range(NS):
            for p in range(H // 2):
                ps = slice(2 * p * N, 2 * (p + 1) * N)
                yp = ys_ref[s, :, ps]
                yc = yp - head_sums(yp) * inv_n
                var = head_sums(yc * yc) * inv_n
                yn = yc * lax.rsqrt(var + RW_GN_EPS) * lnw_ref[:, ps] + lnb_ref[:, ps]
                bonus = head_sums(qs_ref[par, s, :, ps]) * vs_ref[par, s, :, ps].astype(F32)
                o_ref[s, rows, ps] = ((yn + bonus) * gs_ref[par, s, :, ps].astype(F32)).astype(BF16)
                yield

    ri = lax.broadcasted_iota(I32, (C, C), 0)
    ci = lax.broadcasted_iota(I32, (C, C), 1)
    tril = ri >= ci
    stril = ri > ci
    eye = (ri == ci).astype(F32)
    both = jnp.concatenate([stril, tril], axis=0)
    prev = 1 - par
    ops = {}
    for s in range(NS):
        for h in range(H):
            cols = slice(h * N, (h + 1) * N)
            ops[(s, h)] = {name: functools.partial(lambda n, s, cols: opnd_ref[prev, n, s, :, cols], n, s, cols)
                           for n, name in enumerate(RW_OPERANDS)}
            ops[(s, h)]["p_tot"] = functools.partial(lambda s, cols: ptot_ref[prev, s, 0:1, cols], s, cols)
    def head_sums_all(t):
        return jnp.concatenate([head_sums(t[:, 2 * p * N:2 * (p + 1) * N]) for p in range(H // 2)], axis=1)

    def lerp(zz, prev_row, mu):
        return zz + (_shift(zz, prev_row) - zz) * mu

    def prepare():
        for s in range(NS):
            zr, zk, zv, zl = zr_ref[s, rows, :], zk_ref[s, rows, :], zv_ref[s, rows, :], zl_ref[s, rows, :]
            r_f = lerp(zr, car_ref[s, 0:1, :], mu_ref[0:1, :])
            yield
            k_raw = lerp(zk, car_ref[s, 1:2, :], mu_ref[1:2, :])
            yield
            v_f = lerp(zv, car_ref[s, 2:3, :], mu_ref[2:3, :])
            xl = lerp(zl, carl_ref[s, 0:1, :], mul_ref[...])
            car_ref[s, 0:1, :] = zr[C - 1:C, :]
            car_ref[s, 1:2, :] = zk[C - 1:C, :]
            car_ref[s, 2:3, :] = zv[C - 1:C, :]
            carl_ref[s, 0:1, :] = zl[C - 1:C, :]
            yield
            xwa, xg = xl[:, 0:LANES], xl[:, LANES:LANES + g2_ref.shape[0]]
            dw = _dot(jnp.tanh(xwa).astype(BF16), w2_ref[...])
            da = _dot(xwa.astype(BF16), a2_ref[...])
            g_f = _dot(jax.nn.sigmoid(xg).astype(BF16), g2_ref[...])
            yield
            zw = -(w0_ref[...] + dw)
            w_log = -(jnp.maximum(zw, 0.0) + jnp.log(1.0 + jnp.exp(-jnp.abs(zw)))) - 0.5
            yield
            a_sig = jax.nn.sigmoid(a0_ref[...] + da)
            kk = k_raw * kk_ref[...]
            yield
            kk = kk / jnp.maximum(jnp.sqrt(head_sums_all(kk * kk)), 1e-12)
            yield
            k_f = k_raw * (1.0 + (a_sig - 1.0) * ka_ref[...])
            b_f = kk * a_sig
            lw = -jnp.exp(w_log)
            yield
            cs = _dot_exact_lhs(tril.astype(BF16), lw)
            tot = cs[C - 1:C, :]
            yield
            store = lambda name, a: opnd_ref.__setitem__((par, RW_OPERANDS.index(name), s), a.astype(BF16))
            p_inc = jnp.exp(cs)
            store("r_t", r_f * p_inc)
            yield
            p_inv = jnp.exp(-cs)
            store("k_t", k_f * p_inv)
            store("b_t", b_f * p_inv)
            yield
            store("a_t", -kk * jnp.exp(cs - lw))
            yield
            p_rest = jnp.exp(tot - cs)
            store("k_h", k_f * p_rest)
            store("b_h", b_f * p_rest)
            yield
            v_b = v_f.astype(BF16)
            store("v", v_b)
            ptot_ref[par, s, 0:1, :] = jnp.exp(tot)
            qs_ref[par, s] = r_f * k_f * rk_ref[...]
            vs_ref[par, s] = v_b
            gs_ref[par, s] = g_f.astype(BF16)
            yield

    tail_pieces, prep_pieces = tail(), prepare()
    n_pieces = NS * (H // 2 + RW_PREP_PIECES)
    state = dict(ticks=0, done=0)

    def tick():
        state["ticks"] += 1
        due = -(-state["ticks"] * n_pieces // RW_STAGES)
        while state["done"] < due:
            state["done"] += 1
            if next(tail_pieces, "done") == "done":
                next(prep_pieces, None)

    def drain_tail():
        for _ in tail_pieces:
            state["done"] += 1

    _rwkv_heads(ops, stril, tril, both, eye, z_ref, ys_ref, tick, drain_tail)
    for _ in prep_pieces:
        pass


def _rwkv_heads(ops, stril, tril, both, eye, z_ref, ys_ref, tick, drain_tail):
    C, N = RW_CHUNK, RW_N
    hs = list(ops)

    def each(f):
        out = {h: f(h) for h in hs}
        tick()
        return out

    get = lambda name: (lambda h: ops[h][name]())
    a_t, r_t, b_t, k_t, k_h, b_h, p_tot = (get(n) for n in ("a_t", "r_t", "b_t", "k_t", "k_h", "b_h", "p_tot"))
    v = each(lambda h: ops[h]["v"]())
    ar = each(lambda h: jnp.concatenate([a_t(h), r_t(h)], axis=0).astype(BF16))
    m_b = each(lambda h: _dot_nt(ar[h], b_t(h).astype(BF16)))
    m_k = each(lambda h: _dot_nt(ar[h], k_t(h).astype(BF16)))
    l_ab = each(lambda h: jnp.where(stril, m_b[h][:C], 0.0).astype(BF16))
    m_rb = each(lambda h: jnp.where(tril, m_b[h][C:], 0.0).astype(BF16))
    lm_k = each(lambda h: jnp.where(both, m_k[h], 0.0).astype(BF16))
    lmv = each(lambda h: _dot(lm_k[h], v[h]))

    x = each(lambda h: jnp.concatenate([lmv[h][:C], a_t(h).astype(F32)], axis=1))
    y0 = each(lambda h: lmv[h][C:])
    hk = each(lambda h: _dot_tn(k_h(h).astype(BF16), v[h]))
    lp = l_ab
    step = 1
    while step < C:
        x = each(lambda h: x[h] + _dot(lp[h], x[h].astype(BF16)))
        step *= 2
        if step < C:
            lp = each(lambda h: _dot(lp[h], lp[h]).astype(BF16))
    xb = each(lambda h: x[h].astype(BF16))
    yq = each(lambda h: _dot(m_rb[h], xb[h]))
    gh = each(lambda h: _dot_tn(b_h(h).astype(BF16), xb[h]))

    z = each(lambda h: _split2(z_ref[h[0], h[1]]))
    qg = each(lambda h: jnp.concatenate([yq[h][:, N:] + r_t(h).astype(F32),
                                         gh[h][:, N:] + eye * p_tot(h)], axis=0).astype(BF16))
    qgz = each(lambda h: _dot(qg[h], z[h][0]) + _dot(qg[h], z[h][1]))
    yz = each(lambda h: qgz[h][:C])
    gz = each(lambda h: qgz[h][C:])
    drain_tail()
    for s, hd in hs:
        z_ref[s, hd] = (gh[s, hd][:, :N] + hk[s, hd]) + gz[s, hd]
        ys_ref[s, :, hd * N:(hd + 1) * N] = (yq[s, hd][:, :N] + y0[s, hd]) + yz[s, hd]


RW_SEQS_PER_STEP = 1


def _rwkv_chunk(proj, B, L, mu3, mul, w0, a0, k_k, k_a, w2p, a2p, g2p, r_k, ln_w, ln_b):
    T = B * L
    W = w0.shape[1]
    C = RW_CHUNK
    nc = L // C
    H = W // RW_N
    assert H % 2 == 0
    ns = RW_SEQS_PER_STEP if B % RW_SEQS_PER_STEP == 0 else 1
    lw = mul.shape[1]
    proj3 = proj.reshape(B, L, proj.shape[1])
    assert nc % 2 == 0
    np2 = nc // 2
    cols = lambda width, j: pl.BlockSpec((ns, 2 * C, width), lambda bi, i: (bi, jnp.minimum(i, np2 - 1), j))
    full = lambda a: pl.BlockSpec(a.shape, lambda bi, i: (0, 0))
    consts = (mu3, mul, w0, a0, k_k, k_a, w2p, a2p, g2p, r_k, ln_w, ln_b)
    out = pl.pallas_call(
        _rwkv_chunk_kernel,
        grid=(B // ns, np2 + 1),
        in_specs=[cols(W, 1), cols(W, 2), cols(W, 3), cols(lw, 4 * W // lw)] + [full(a) for a in consts],
        out_specs=pl.BlockSpec((ns, 2 * C, W), lambda bi, i: (bi, jnp.maximum(i - 1, 0), 0)),
        out_shape=jax.ShapeDtypeStruct((B, L, W), BF16),
        scratch_shapes=[pltpu.VMEM((ns, H, RW_N, RW_N), F32), pltpu.VMEM((ns, C, W), F32),
                        pltpu.VMEM((2, ns, C, W), F32), pltpu.VMEM((2, ns, C, W), BF16),
                        pltpu.VMEM((2, ns, C, W), BF16),
                        pltpu.VMEM((ns, 8, W), F32), pltpu.VMEM((ns, 8, lw), F32),
                        pltpu.VMEM((2, len(RW_OPERANDS), ns, C, W), BF16), pltpu.VMEM((2, ns, 8, W), F32)],
        compiler_params=_cparams("arbitrary", "arbitrary"),
        name="rwkv_chunk",
    )(proj3, proj3, proj3, proj3, *consts)
    return out.reshape(T, W)


def _first_index_of_max(vals, lane, valid):
    neg = jnp.float32(-jnp.inf)
    masked = jnp.where(valid, vals, neg)
    m = jnp.max(masked, axis=-1, keepdims=True)
    idx = jnp.min(jnp.where(valid & (masked == m), lane, LANES), axis=-1, keepdims=True)
    return m, idx


def _outproj_kernel(s5_ref, rw_ref, x_ref, wt_ref, wb_ref, g_ref, wr_ref, br_ref,
                    h_ref, hn_ref, rec_ref):
    h = x_ref[...] + (_dot(s5_ref[...], wt_ref[...]) + _dot(rw_ref[...], wb_ref[...]))
    h_ref[...] = h
    ms = jnp.mean(h * h, axis=-1, keepdims=True)
    hn = h * lax.rsqrt(ms + RMS_EPS) * g_ref[...]
    hn_ref[...] = _pack_halves(hn).reshape(hn_ref.shape)

    hh, hl = _split2(hn)
    wh, wl = wr_ref[:, :LANES], wr_ref[:, LANES:]
    logits = _dot(hh, wh) + (_dot(hh, wl) + _dot(hl, wh)) + br_ref[...]
    lane = lax.broadcasted_iota(I32, logits.shape, 1)
    is_grp = (lane >= N_EXPERTS) & (lane < N_EXPERTS + N_GROUPS)
    gmax, gidx = _first_index_of_max(logits, lane, is_grp)
    gsum = jnp.sum(jnp.where(is_grp, jnp.exp(logits - gmax), 0.0), axis=-1, keepdims=True)
    p_grp = 1.0 / gsum
    grp = gidx - N_EXPERTS
    in_grp = (lane >= grp * EPG) & (lane < (grp + 1) * EPG)
    m1, i1 = _first_index_of_max(logits, lane, in_grp)
    m2, i2 = _first_index_of_max(logits, lane, in_grp & (lane != i1))
    e = jnp.exp(m2 - m1)
    g1 = p_grp / (1.0 + e)
    g2 = p_grp * e / (1.0 + e)
    rec = jnp.where(lane == 0, i1.astype(F32),
          jnp.where(lane == 1, i2.astype(F32),
          jnp.where(lane == 2, g1, jnp.where(lane == 3, g2, 0.0))))
    rec_ref[...] = rec


def _outproj(s5o, rwo, x2, w_top, w_bot, g, wr, b_route, tm):
    T, D = x2.shape
    W = s5o.shape[1]
    full = lambda a: pl.BlockSpec(a.shape, lambda i: (0, 0))
    return pl.pallas_call(
        _outproj_kernel,
        grid=(T // tm,),
        in_specs=[pl.BlockSpec((tm, W), lambda i: (i, 0)), pl.BlockSpec((tm, W), lambda i: (i, 0)),
                  pl.BlockSpec((tm, D), lambda i: (i, 0)),
                  full(w_top), full(w_bot), full(g), full(wr), full(b_route)],
        out_specs=[pl.BlockSpec((tm, D), lambda i: (i, 0)), pl.BlockSpec((tm, 1, D // 2), lambda i: (i, 0, 0)),
                   pl.BlockSpec((tm, LANES), lambda i: (i, 0))],
        out_shape=[jax.ShapeDtypeStruct((T, D), F32), jax.ShapeDtypeStruct((T, 1, D // 2), U32),
                   jax.ShapeDtypeStruct((T, LANES), F32)],
        compiler_params=_cparams("arbitrary"),
        name="outproj_route",
    )(s5o, rwo, x2, w_top, w_bot, g, wr, b_route)


def _onehots(rec, lane):
    oh0 = (lane == rec[:, 0:1].astype(I32)).astype(F32)
    oh1 = (lane == rec[:, 1:2].astype(I32)).astype(F32)
    return oh0, oh1


def _rank_kernel(rec_ref, rank_ref, cnt_ref, base_ref):
    tb = rec_ref.shape[0]

    @pl.when(pl.program_id(0) == 0)
    def _():
        base_ref[...] = jnp.zeros_like(base_ref)

    lane = lax.broadcasted_iota(I32, (tb, LANES), 1)
    oh0, oh1 = _onehots(rec_ref[...], lane)
    both = oh0 + oh1
    ri = lax.broadcasted_iota(I32, (tb, tb), 0)
    ci = lax.broadcasted_iota(I32, (tb, tb), 1)
    before = _dot((ri > ci).astype(BF16), both.astype(BF16)) + base_ref[0:1, :]
    rank0 = jnp.sum(oh0 * before, axis=-1, keepdims=True)
    rank1 = jnp.sum(oh1 * before, axis=-1, keepdims=True)
    rank_ref[...] = jnp.where(lane == 0, rank0, jnp.where(lane == 1, rank1, 0.0))
    total = base_ref[0:1, :] + jnp.sum(both, axis=0, keepdims=True)
    base_ref[0:1, :] = total
    cnt_ref[...] = jnp.broadcast_to(total, cnt_ref.shape)


def _rank(rec, tb):
    T = rec.shape[0]
    return pl.pallas_call(
        _rank_kernel,
        grid=(T // tb,),
        in_specs=[pl.BlockSpec((tb, LANES), lambda i: (i, 0))],
        out_specs=[pl.BlockSpec((tb, LANES), lambda i: (i, 0)), pl.BlockSpec((8, LANES), lambda i: (0, 0))],
        out_shape=[jax.ShapeDtypeStruct((T, LANES), F32), jax.ShapeDtypeStruct((8, LANES), F32)],
        scratch_shapes=[pltpu.VMEM((8, LANES), F32)],
        compiler_params=_cparams("arbitrary"),
        name="moe_rank",
    )(rec)


def _padded_starts(cnt):
    padded = jnp.ceil(cnt * (1.0 / MOE_BLOCK)) * MOE_BLOCK
    ri = lax.broadcasted_iota(I32, (LANES, LANES), 0)
    ci = lax.broadcasted_iota(I32, (LANES, LANES), 1)
    p8 = jnp.broadcast_to(padded, (8, LANES))
    pend = _dot_exact_lhs_rhs(p8, (ri <= ci).astype(BF16))[0:1, :]
    return pend - padded, pend


def _dot_exact_lhs_rhs(a, b_bf16):
    ah, am, al = _split3(a)
    return _dot(ah, b_bf16) + (_dot(am, b_bf16) + _dot(al, b_bf16))


def _dest_kernel(rec_ref, rank_ref, cnt_ref, dest_ref, blk_ref):
    tb = rec_ref.shape[0]
    cnt = cnt_ref[0:1, :]
    pstart, pend = _padded_starts(cnt)
    lane = lax.broadcasted_iota(I32, (tb, LANES), 1)
    oh0, oh1 = _onehots(rec_ref[...], lane)
    rank = rank_ref[...]
    d0 = jnp.sum(oh0 * pstart, axis=-1, keepdims=True) + rank[:, 0:1]
    d1 = jnp.sum(oh1 * pstart, axis=-1, keepdims=True) + rank[:, 1:2]
    dest_ref[...] = jnp.where(lane == 0, d0, jnp.where(lane == 1, d1, 0.0)).astype(I32)

    @pl.when(pl.program_id(0) == 0)
    def _():
        nb = blk_ref.shape[0]
        blane = lax.broadcasted_iota(I32, (nb, LANES), 1)
        bstart = (lax.broadcasted_iota(I32, (nb, 1), 0) * MOE_BLOCK).astype(F32)
        is_e = blane < N_EXPERTS
        bexp = jnp.sum(jnp.where(is_e & (pend <= bstart), 1.0, 0.0), axis=-1, keepdims=True)
        bexp = jnp.minimum(bexp, N_EXPERTS - 1.0)
        nact = jnp.max(jnp.where(is_e, pend, 0.0), axis=-1, keepdims=True) * (1.0 / MOE_BLOCK)
        lastblk = jnp.where(is_e & (pend > pstart), pend - MOE_BLOCK, -1.0)
        brow = lax.broadcasted_iota(I32, (nb, 1), 0)
        out = jnp.where(blane == 0, bexp, jnp.where(blane == 1, nact, 0.0))
        out = jnp.where(brow == nb - 1, lastblk, out)
        blk_ref[...] = out.astype(I32)


def _dest(rec, rank, cnt, tb, nb_rows):
    T = rec.shape[0]
    return pl.pallas_call(
        _dest_kernel,
        grid=(T // tb,),
        in_specs=[pl.BlockSpec((tb, LANES), lambda i: (i, 0)), pl.BlockSpec((tb, LANES), lambda i: (i, 0)),
                  pl.BlockSpec((8, LANES), lambda i: (0, 0))],
        out_specs=[pl.BlockSpec((tb, LANES), lambda i: (i, 0)), pl.BlockSpec((nb_rows, LANES), lambda i: (0, 0))],
        out_shape=[jax.ShapeDtypeStruct((T, LANES), I32), jax.ShapeDtypeStruct((nb_rows, LANES), I32)],
        compiler_params=_cparams("arbitrary"),
        name="moe_dest",
    )(rec, rank, cnt)


def _scatter_kernel(pad_ref, na_ref, dest_ref, hn_ref, xs_ref, zbuf_ref, idx_ref, sa_ref, sb_ref,
                    sem_a, sem_b, isem, zsem, *, tb, n_blocks, n_tail, n_steps):
    i = pl.program_id(0)

    @pl.when(i == 0)
    def _():
        zbuf_ref[...] = jnp.zeros_like(zbuf_ref)

        def zero_rows(start):
            return pltpu.make_async_copy(zbuf_ref, xs_ref.at[pl.ds(start, MOE_BLOCK)], zsem)

        def pad_start(e, c):
            @pl.when(pad_ref[e] >= 0)
            def _():
                zero_rows(pad_ref[e]).start()
            return c

        def pad_wait(e, c):
            @pl.when(pad_ref[e] >= 0)
            def _():
                zero_rows(pad_ref[e]).wait()
            return c

        def tail_start(k, c):
            @pl.when(na_ref[0] + k < n_blocks)
            def _():
                zero_rows((na_ref[0] + k) * MOE_BLOCK).start()
            return c

        def tail_wait(k, c):
            @pl.when(na_ref[0] + k < n_blocks)
            def _():
                zero_rows((na_ref[0] + k) * MOE_BLOCK).wait()
            return c

        lax.fori_loop(0, N_EXPERTS, pad_start, 0)
        lax.fori_loop(0, n_tail, tail_start, 0)
        lax.fori_loop(0, N_EXPERTS, pad_wait, 0)
        lax.fori_loop(0, n_tail, tail_wait, 0)

    def idx_copy(step, slot):
        return pltpu.make_async_copy(dest_ref.at[step], idx_ref.at[pl.ds(slot * (2 * tb), 2 * tb)], isem)

    @pl.when(i == 0)
    def _():
        idx_copy(0, 0).start()

    slot = i % 2
    idx_copy(i, slot).wait()

    @pl.when(i + 1 < n_steps)
    def _():
        idx_copy(i + 1, 1 - slot).start()

    hb = tb // 2
    for half, (buf, sem) in enumerate(((sa_ref, sem_a), (sb_ref, sem_b))):
        def wait_rows(buf=buf, sem=sem):
            for _ in range(2):
                pltpu.make_async_copy(buf, xs_ref.at[pl.ds(0, hb)], sem).wait()

        @pl.when(i > 0)
        def _():
            wait_rows()

        buf[...] = hn_ref[half * hb:(half + 1) * hb]

        def issue(t, c, buf=buf, sem=sem, half=half):
            for s in range(2):
                d = idx_ref[slot * (2 * tb) + 2 * (half * hb + t) + s]
                pltpu.make_async_copy(buf.at[pl.ds(t, 1)], xs_ref.at[pl.ds(d, 1)], sem).start(priority=s)
            return c

        lax.fori_loop(0, hb, issue, 0, unroll=8)

    @pl.when(i == n_steps - 1)
    def _():
        for buf, sem in ((sa_ref, sem_a), (sb_ref, sem_b)):
            for _ in range(2):
                pltpu.make_async_copy(buf, xs_ref.at[pl.ds(0, hb)], sem).wait()


def _scatter(lastblk, nact, dest2, hn, tb, n_blocks):
    T, _, D = hn.shape
    n_tail = n_blocks - (-(-2 * T // MOE_BLOCK))
    hb = tb // 2
    return pl.pallas_call(
        functools.partial(_scatter_kernel, tb=tb, n_blocks=n_blocks, n_tail=n_tail, n_steps=T // tb),
        grid_spec=pltpu.PrefetchScalarGridSpec(
            num_scalar_prefetch=2, grid=(T // tb,),
            in_specs=[pl.BlockSpec(memory_space=pl.ANY),
                      pl.BlockSpec((tb, 1, D), lambda i, pad, na: (i, 0, 0))],
            out_specs=pl.BlockSpec(memory_space=pl.ANY),
            scratch_shapes=[pltpu.VMEM((MOE_BLOCK, 1, D), hn.dtype), pltpu.SMEM((4 * tb,), I32),
                            pltpu.VMEM((hb, 1, D), hn.dtype), pltpu.VMEM((hb, 1, D), hn.dtype),
                            pltpu.SemaphoreType.DMA, pltpu.SemaphoreType.DMA,
                            pltpu.SemaphoreType.DMA, pltpu.SemaphoreType.DMA]),
        out_shape=jax.ShapeDtypeStruct((n_blocks * MOE_BLOCK, 1, D), hn.dtype),
        compiler_params=_cparams("arbitrary"),
        name="moe_scatter",
    )(lastblk, nact, dest2, hn)


W_SLOTS = 2


def _expert_kernel(be_ref, na_ref, x_ref, wg_hbm, wu_hbm, wd_hbm, y_ref,
                   wg_buf, wu_buf, wd_buf, wgb_ref, wub_ref, wdb_ref, x2_ref, elist_ref, state_ref, sems):
    j = pl.program_id(0)
    na = na_ref[0]

    def weight_copies(e, slot):
        return [pltpu.make_async_copy(src.at[e], buf.at[slot], sems.at[slot, n])
                for n, (src, buf) in enumerate(((wg_hbm, wg_buf), (wu_hbm, wu_buf), (wd_hbm, wd_buf)))]

    @pl.when(j == 0)
    def _():
        def scan(k, n):
            is_new = (k == 0) | (be_ref[k] != be_ref[jnp.maximum(k - 1, 0)])

            @pl.when(is_new)
            def _():
                elist_ref[n] = be_ref[k]
            return n + is_new.astype(I32)

        n_exp = lax.fori_loop(0, na, scan, 0)
        state_ref[0] = 0
        state_ref[1] = n_exp
        for n in range(W_SLOTS):
            @pl.when(n < n_exp)
            def _():
                for cp in weight_copies(elist_ref[n], n):
                    cp.start()

    @pl.when(j < na)
    def _():
        @pl.when((j == 0) | (be_ref[j] != be_ref[jnp.maximum(j - 1, 0)]))
        def _():
            n = state_ref[0]
            slot = n % W_SLOTS
            for cp in weight_copies(elist_ref[n], slot):
                cp.wait()
            wgb_ref[...] = wg_buf[slot].astype(BF16)
            wub_ref[...] = wu_buf[slot].astype(BF16)
            wdb_ref[...] = wd_buf[slot].astype(BF16)

            @pl.when(n + W_SLOTS < state_ref[1])
            def _():
                for cp in weight_copies(elist_ref[n + W_SLOTS], slot):
                    cp.start()

            state_ref[0] = n + 1

        x2_ref[...] = x_ref[...].reshape(x2_ref.shape)
        lo, hi = _unpack_halves(x2_ref[...])
        lo, hi = lo.astype(BF16), hi.astype(BF16)
        half = lo.shape[1]
        hg = _dot(lo, wgb_ref[:half, :]) + _dot(hi, wgb_ref[half:, :])
        hu = _dot(lo, wub_ref[:half, :]) + _dot(hi, wub_ref[half:, :])
        act = (hg * jax.nn.sigmoid(hg)) * hu
        y = _dot(act.astype(BF16), wdb_ref[...])
        y_ref[...] = _pack_halves(y).reshape(y_ref.shape)

    @pl.when(j >= na)
    def _():
        y_ref[...] = jnp.zeros_like(y_ref)


def _experts(blk_exp, nact, xs, w_gate, w_up, w_down, n_blocks):
    DP = xs.shape[2]
    D = 2 * DP
    rows = n_blocks * MOE_BLOCK
    DE = w_gate.shape[2]
    blk = lambda j, be, na: (jnp.minimum(j, na[0] - 1), 0, 0)
    hbm = pl.BlockSpec(memory_space=pl.ANY)
    return pl.pallas_call(
        _expert_kernel,
        grid_spec=pltpu.PrefetchScalarGridSpec(
            num_scalar_prefetch=2, grid=(n_blocks,),
            in_specs=[pl.BlockSpec((MOE_BLOCK, 1, DP), blk), hbm, hbm, hbm],
            out_specs=pl.BlockSpec((MOE_BLOCK, 1, DP), lambda j, be, na: (j, 0, 0)),
            scratch_shapes=[pltpu.VMEM((W_SLOTS, D, DE), F32), pltpu.VMEM((W_SLOTS, D, DE), F32),
                            pltpu.VMEM((W_SLOTS, DE, D), F32),
                            pltpu.VMEM((D, DE), BF16), pltpu.VMEM((D, DE), BF16), pltpu.VMEM((DE, D), BF16),
                            pltpu.VMEM((MOE_BLOCK, DP), U32), pltpu.SMEM((N_EXPERTS,), I32), pltpu.SMEM((2,), I32),
                            pltpu.SemaphoreType.DMA((W_SLOTS, 3))]),
        out_shape=jax.ShapeDtypeStruct((rows, 1, DP), U32),
        compiler_params=_cparams("arbitrary"),
        name="moe_experts",
    )(blk_exp, nact, xs, w_gate, w_up, w_down)


def _combine_kernel(dest_ref, ys_ref, h_ref, rec_ref, g_ref, o_ref, idx_ref, ya0_ref, ya1_ref, yb0_ref, yb1_ref,
                    y2_ref, sem_a, sem_b, isem, *, normalize, n_steps):
    i = pl.program_id(0)
    tb = h_ref.shape[0]
    hb = tb // 2
    halves = ((ya0_ref, ya1_ref, sem_a), (yb0_ref, yb1_ref, sem_b))

    def idx_copy(step, slot):
        return pltpu.make_async_copy(dest_ref.at[step], idx_ref.at[pl.ds(slot * (2 * tb), 2 * tb)], isem)

    def issue(half, slot):
        bufs, sem = halves[half][:2], halves[half][2]

        def body(t, c):
            for s in range(2):
                d = idx_ref[slot * (2 * tb) + 2 * (half * hb + t) + s]
                pltpu.make_async_copy(ys_ref.at[pl.ds(d, 1)], bufs[s].at[pl.ds(t, 1)], sem).start(priority=s)
            return c

        lax.fori_loop(0, hb, body, 0, unroll=8)

    @pl.when(i == 0)
    def _():
        first = idx_copy(0, 0)
        first.start()
        first.wait()
        issue(0, 0)
        issue(1, 0)
        if n_steps > 1:
            idx_copy(1, 1).start()

    nslot = (i + 1) % 2

    @pl.when(i + 1 < n_steps)
    def _():
        idx_copy(i + 1, nslot).wait()

    rec = rec_ref[...]
    for half, (y0_ref, y1_ref, sem) in enumerate(halves):
        rows = slice(half * hb, (half + 1) * hb)
        pltpu.make_async_copy(ys_ref.at[pl.ds(0, hb)], y0_ref, sem).wait()
        pltpu.make_async_copy(ys_ref.at[pl.ds(0, hb)], y1_ref, sem).wait()
        y2_ref[...] = y0_ref[...].reshape(y2_ref.shape)
        h = h_ref[rows, :] + rec[rows, 2:3] * jnp.concatenate(_unpack_halves(y2_ref[...]), axis=1)
        y2_ref[...] = y1_ref[...].reshape(y2_ref.shape)
        h = h + rec[rows, 3:4] * jnp.concatenate(_unpack_halves(y2_ref[...]), axis=1)
        if normalize:
            ms = jnp.mean(h * h, axis=-1, keepdims=True)
            h = h * lax.rsqrt(ms + RMS_EPS) * g_ref[...]
        o_ref[rows, :] = h

        @pl.when(i + 1 < n_steps)
        def _():
            issue(half, nslot)

    @pl.when(i + 2 < n_steps)
    def _():
        idx_copy(i + 2, i % 2).start()


def _combine(dest2, ys, h, rec, g, tb, normalize):
    T, D = h.shape
    hb = tb // 2
    return pl.pallas_call(
        functools.partial(_combine_kernel, normalize=normalize, n_steps=T // tb),
        grid=(T // tb,),
        in_specs=[pl.BlockSpec(memory_space=pl.ANY), pl.BlockSpec(memory_space=pl.ANY),
                  pl.BlockSpec((tb, D), lambda i: (i, 0)), pl.BlockSpec((tb, LANES), lambda i: (i, 0)),
                  pl.BlockSpec((1, D), lambda i: (0, 0))],
        out_specs=pl.BlockSpec((tb, D), lambda i: (i, 0)),
        out_shape=jax.ShapeDtypeStruct((T, D), F32),
        scratch_shapes=[pltpu.SMEM((4 * tb,), I32)] + [pltpu.VMEM((hb, 1, D // 2), U32)] * 4 +
                       [pltpu.VMEM((hb, D // 2), U32)] + [pltpu.SemaphoreType.DMA] * 3,
        compiler_params=_cparams("arbitrary"),
        name="moe_combine",
    )(dest2, ys, h, rec, g)


def _pad_cols(a, n):
    return jnp.pad(a, ((0, 0), (0, n - a.shape[1])))


def _pad_rows(a, n, at=0):
    return jnp.pad(a, ((at, n - a.shape[0] - at), (0, 0)))


def _layer(x2, B, L, p):
    T, D = x2.shape
    W = p["s5_d"].shape[0]
    G = p["s5_lambda_re"].shape[0]
    assert W == G * S5_HG
    dl, al, gl = p["rwkv_w2"].shape[0], p["rwkv_a2"].shape[0], p["rwkv_g2"].shape[0]

    assert dl + al == LANES and gl <= RW_LORA_PAD - LANES
    w_in_r = _pad_cols(p["w_in"], 4 * W + RW_LORA_PAD).astype(BF16)
    n_proj = w_in_r.shape[1]
    tm_in = min(1024, T)
    proj = _inproj(x2, p["norm_mix_g"].reshape(1, D), w_in_r, tm_in, n_proj // INPROJ_N_TILES)

    C = S5_CHUNK
    nc = L // C
    ct_re = jnp.swapaxes(p["s5_c_re"], 1, 2)
    ct_im = jnp.swapaxes(p["s5_c_im"], 1, 2)
    rb = min(LANES, B * nc)
    z = _s5_in(proj.reshape(B * nc, C, proj.shape[1]), G, rb)
    y_t = _s5_chunk(z, p["s5_lambda_re"], p["s5_lambda_im"], p["s5_log_step"],
                    ct_re, ct_im, p["s5_b_re"], p["s5_b_im"], p["s5_c_re"], p["s5_c_im"], nc)
    y_ssm = _s5_out(y_t, rb).reshape(T, W)
    tm = min(512, T)
    s5_out = _s5_glu(y_ssm, proj, p["s5_d"].reshape(1, W), p["s5_w_glu"].astype(BF16),
                     p["s5_b_glu"].reshape(1, W), tm)

    mu = p["rwkv_mu"]
    mu3 = _pad_rows(mu[:3 * W].reshape(3, W), 8)
    mul = _pad_cols(mu[None, 3 * W:], RW_LORA_PAD)
    w2p = _pad_rows(p["rwkv_w2"], LANES).astype(BF16)
    a2p = _pad_rows(p["rwkv_a2"], LANES, at=dl).astype(BF16)
    g2p = _pad_rows(p["rwkv_g2"], -(-gl // LANES) * LANES).astype(BF16)
    row = lambda a: a.reshape(1, W)
    rw_out = _rwkv_chunk(proj, B, L, mu3, mul, row(p["rwkv_w0"]), row(p["rwkv_a0"]), row(p["rwkv_k_k"]),
                         row(p["rwkv_k_a"]), w2p, a2p, g2p, row(p["rwkv_r_k"]), row(p["rwkv_ln_w"]),
                         row(p["rwkv_ln_b"]))

    w_out = p["w_out"].astype(BF16)
    w_route = jnp.concatenate([p["w_route_exp"], p["w_route_grp"]], axis=1)
    w_route = _pad_cols(w_route, LANES)
    wr_hi = w_route.astype(BF16)
    wr_lo = (w_route - wr_hi.astype(F32)).astype(BF16)
    wr = jnp.concatenate([wr_hi, wr_lo], axis=1)
    b_route = _pad_cols(jnp.concatenate([p["b_route_exp"], p["b_route_grp"]])[None, :], LANES)
    tm_o = min(512, T)
    h, hn, rec = _outproj(s5_out, rw_out, x2, w_out[:W], w_out[W:], p["norm_ffn_g"].reshape(1, D),
                          wr, b_route, tm_o)

    tb = min(1024, T)
    rank, cnt = _rank(rec, tb)
    n_blocks = -(-2 * T // MOE_BLOCK) + N_EXPERTS
    nb_rows = -(-(n_blocks + 1) // 8) * 8
    dest, blk = _dest(rec, rank, cnt, tb, nb_rows)
    blk_exp = blk[:n_blocks, 0]
    nact = blk[0:1, 1]
    lastblk = blk[nb_rows - 1, :N_EXPERTS]
    ts = min(1024, T)
    dest2 = dest[:, :2].reshape(T // ts, 2 * ts)

    xs = _scatter(lastblk, nact, dest2, hn, ts, n_blocks)
    ys = _experts(blk_exp, nact, xs, p["w_gate"], p["w_up"], p["w_down"], n_blocks)
    return dest2, ys, h, rec, ts


def kernel(x, norm_mix_g, w_in, s5_lambda_re, s5_lambda_im, s5_log_step, s5_b_re, s5_b_im, s5_c_re, s5_c_im, s5_d, s5_w_glu, s5_b_glu, rwkv_mu, rwkv_w0, rwkv_w2, rwkv_a0, rwkv_a2, rwkv_g2, rwkv_k_k, rwkv_k_a, rwkv_r_k, rwkv_ln_w, rwkv_ln_b, w_out, norm_ffn_g, w_route_grp, b_route_grp, w_route_exp, b_route_exp, w_gate, w_up, w_down, norm_final_g):
    B, L, D = x.shape
    params = dict(
        norm_mix_g=norm_mix_g, w_in=w_in, s5_lambda_re=s5_lambda_re, s5_lambda_im=s5_lambda_im,
        s5_log_step=s5_log_step, s5_b_re=s5_b_re, s5_b_im=s5_b_im, s5_c_re=s5_c_re, s5_c_im=s5_c_im,
        s5_d=s5_d, s5_w_glu=s5_w_glu, s5_b_glu=s5_b_glu, rwkv_mu=rwkv_mu, rwkv_w0=rwkv_w0, rwkv_w2=rwkv_w2,
        rwkv_a0=rwkv_a0, rwkv_a2=rwkv_a2, rwkv_g2=rwkv_g2, rwkv_k_k=rwkv_k_k, rwkv_k_a=rwkv_k_a,
        rwkv_r_k=rwkv_r_k, rwkv_ln_w=rwkv_ln_w, rwkv_ln_b=rwkv_ln_b, w_out=w_out, norm_ffn_g=norm_ffn_g,
        w_route_grp=w_route_grp, b_route_grp=b_route_grp, w_route_exp=w_route_exp, b_route_exp=b_route_exp,
        w_gate=w_gate, w_up=w_up, w_down=w_down)
    depth = norm_mix_g.shape[0]
    h2 = x.reshape(B * L, D)
    for l in range(depth):
        p = {k_: v_[l] for k_, v_ in params.items()}
        dest2, ys, h, rec, ts = _layer(h2, B, L, p)
        h2 = _combine(dest2, ys, h, rec, norm_final_g.reshape(1, D), ts, normalize=(l == depth - 1))
    return h2.reshape(B, L, D)
```

```python
import functools

import jax
import jax.numpy as jnp
from jax import lax
from jax.experimental import pallas as pl
from jax.experimental.pallas import tpu as pltpu

F32 = jnp.float32
BF16 = jnp.bfloat16
I32 = jnp.int32

RMS_EPS = 1e-6
S5_CHUNK = 64
S5_HG = 16
S5_P = 64
RW_N = 64
RW_CHUNK = 64
RW_GN_EPS = 64e-5
N_GROUPS = 8
EPG = 8
N_EXPERTS = 64
MOE_BLOCK = 256
LANES = 128
INPROJ_N_TILES = 4
VMEM_LIMIT = 56 * 1024 * 1024


def _cparams(*sem, **kw):
    return pltpu.CompilerParams(dimension_semantics=sem, vmem_limit_bytes=VMEM_LIMIT, **kw)


def _split2(x):
    hi = x.astype(BF16)
    lo = (x - hi.astype(F32)).astype(BF16)
    return hi, lo


def _split3(x):
    hi = x.astype(BF16)
    r = x - hi.astype(F32)
    mid = r.astype(BF16)
    lo = (r - mid.astype(F32)).astype(BF16)
    return hi, mid, lo


def _dot(a, b):
    return jnp.dot(a, b, preferred_element_type=F32)


def _dot_nt(a, b):
    return lax.dot_general(a, b, (((1,), (1,)), ((), ())), preferred_element_type=F32)


def _dot_tn(a, b):
    return lax.dot_general(a, b, (((0,), (0,)), ((), ())), preferred_element_type=F32)


def _dot_x3(a, b):
    ah, al = _split2(a)
    bh, bl = _split2(b)
    return _dot(ah, bh) + (_dot(ah, bl) + _dot(al, bh))


def _dot_exact_lhs(a_bf16, b):
    bh, bm, bl = _split3(b)
    return _dot(a_bf16, bh) + (_dot(a_bf16, bm) + _dot(a_bf16, bl))


def _cmul(ar, ai, br, bi):
    return ar * br - ai * bi, ar * bi + ai * br


U32 = jnp.uint32


def _pack_halves(x):
    half = x.shape[1] // 2
    lo = lax.bitcast_convert_type(x[:, :half].astype(BF16).astype(F32), U32)
    hi = lax.bitcast_convert_type(x[:, half:].astype(BF16).astype(F32), U32)
    return (lo >> 16) | hi


def _unpack_halves(p):
    lo = lax.bitcast_convert_type(p << 16, F32)
    hi = lax.bitcast_convert_type(p & jnp.uint32(0xFFFF0000), F32)
    return lo, hi


def _inproj_kernel(x_ref, g_ref, w_ref, o_ref, hn_ref):
    @pl.when(pl.program_id(1) == 0)
    def _():
        x = x_ref[...]
        ms = jnp.mean(x * x, axis=-1, keepdims=True)
        hn_ref[...] = (x * lax.rsqrt(ms + RMS_EPS) * g_ref[...]).astype(BF16)

    o_ref[...] = _dot(hn_ref[...], w_ref[...])


def _inproj(x2, g, w_bf16, tm, tn):
    T, D = x2.shape
    N = w_bf16.shape[1]
    return pl.pallas_call(
        _inproj_kernel,
        grid=(T // tm, N // tn),
        in_specs=[
            pl.BlockSpec((tm, D), lambda i, j: (i, 0)),
            pl.BlockSpec((1, D), lambda i, j: (0, 0)),
            pl.BlockSpec((D, tn), lambda i, j: (0, j)),
        ],
        out_specs=pl.BlockSpec((tm, tn), lambda i, j: (i, j)),
        out_shape=jax.ShapeDtypeStruct((T, N), F32),
        scratch_shapes=[pltpu.VMEM((tm, D), BF16)],
        compiler_params=_cparams("arbitrary", "arbitrary"),
        name="inproj",
    )(x2, g, w_bf16)


def _binpow(ar, ai, expo, nbits):
    pr = jnp.ones(expo.shape, F32)
    pi = jnp.zeros(expo.shape, F32)
    sr, si = ar, ai
    for bit in range(nbits):
        m = ((expo >> bit) & 1) == 1
        nr, ni = _cmul(pr, pi, sr, si)
        pr = jnp.where(m, nr, pr)
        pi = jnp.where(m, ni, pi)
        if bit + 1 < nbits:
            sr, si = _cmul(sr, si, sr, si)
    return pr, pi


def _frame_powers(ar, ai, reverse, plus_one):
    C, HG, P = S5_CHUNK, S5_HG, ar.shape[0]
    fpt = LANES // HG
    j = lax.broadcasted_iota(I32, (P, LANES), 1) // HG
    inner_r, inner_i = _binpow(ar, ai, (fpt - 1 - j) if reverse else j, (fpt - 1).bit_length())
    sr, si = ar, ai
    for _ in range(fpt.bit_length() - 1):
        sr, si = _cmul(sr, si, sr, si)
    outer = [(ar, ai) if plus_one else (jnp.ones_like(ar), jnp.zeros_like(ai))]
    for _ in range(C // fpt - 1):
        outer.append(_cmul(outer[-1][0], outer[-1][1], sr, si))
    if reverse:
        outer = outer[::-1]
    tiles = [_cmul(inner_r, inner_i, o_r, o_i) for o_r, o_i in outer]
    return (jnp.concatenate([t[0] for t in tiles], axis=1), jnp.concatenate([t[1] for t in tiles], axis=1))


def _zoh(lr, li, dt):
    mag = jnp.exp(lr * dt)
    ang = li * dt
    ar, ai = mag * jnp.cos(ang), mag * jnp.sin(ang)
    den = lr * lr + li * li
    nr, ni = ar - 1.0, ai
    fr = (nr * lr + ni * li) / den
    fi = (ni * lr - nr * li) / den
    return ar, ai, fr, fi


def _s5_ops_group(lrc_ref, lic_ref, dt_ref, ct_re_ref, ct_im_ref, bt_re_ref, bt_im_ref, c_re_ref, c_im_ref,
                  m_ref, wbt_ref, wct_ref, a64_ref):
    C, HG, P = S5_CHUNK, S5_HG, S5_P
    W = C * HG
    dt = jnp.exp(dt_ref[...])
    ar_c, ai_c, fr_c, fi_c = _zoh(lrc_ref[...], lic_ref[...], dt)

    pr, pi = _frame_powers(ar_c, ai_c, False, True)
    qr, qi = _frame_powers(ar_c, ai_c, True, False)

    sel = (lax.broadcasted_iota(I32, (HG, W), 1) % HG == lax.broadcasted_iota(I32, (HG, W), 0)).astype(BF16)
    tile = lambda a: _dot_exact_lhs_rhs(a, sel)

    c1_re, c1_im = _cmul(tile(ct_re_ref[...]), tile(ct_im_ref[...]), pr, pi)
    wct_ref[...] = jnp.concatenate([c1_re, -c1_im], axis=0).T.astype(BF16)

    bb_re, bb_im = _cmul(bt_re_ref[...], bt_im_ref[...], fr_c, fi_c)
    ab_re, ab_im = _cmul(tile(bb_re), tile(bb_im), qr, qi)
    wbt_ref[:P, :] = ab_re.astype(BF16)
    wbt_ref[P:, :] = ab_im.astype(BF16)

    cr, ci = ar_c, ai_c
    for _ in range(C.bit_length() - 1):
        cr, ci = _cmul(cr, ci, cr, ci)
    a64_ref[...] = jnp.concatenate([cr, ci], axis=0)

    strip = _dot_x3(c_re_ref[...], ab_re) - _dot_x3(c_im_ref[...], ab_im)
    ext = jnp.concatenate([strip, jnp.zeros((HG, W), F32)], axis=1)
    for t in range(C):
        off = (C - 1 - t) * HG
        m_ref[t * HG:(t + 1) * HG, :] = ext[:, off:off + W].astype(BF16)


S5_FB = 8


def _s5_in_kernel(x_ref, z_ref):
    G, _, rb = z_ref.shape
    for f in range(S5_FB):
        xt = x_ref[:, f, :].T
        z_ref[:, f * S5_HG:(f + 1) * S5_HG, :] = xt.reshape(G, S5_HG, rb).astype(BF16)


def _s5_in(proj3, G, rb):
    R, C, _ = proj3.shape
    W = G * S5_HG
    return pl.pallas_call(
        _s5_in_kernel,
        grid=(R // rb, C // S5_FB),
        in_specs=[pl.BlockSpec((rb, S5_FB, W), lambda i, j: (i, j, 0))],
        out_specs=pl.BlockSpec((G, S5_FB * S5_HG, rb), lambda i, j: (0, j, i)),
        out_shape=jax.ShapeDtypeStruct((G, C * S5_HG, R), BF16),
        compiler_params=_cparams("arbitrary", "arbitrary"),
        name="s5_in",
    )(proj3)


def _s5_out_kernel(yt_ref, o_ref):
    G, _, rb = yt_ref.shape
    for f in range(S5_FB):
        slab = yt_ref[:, f * S5_HG:(f + 1) * S5_HG, :].reshape(G * S5_HG, rb)
        o_ref[:, f, :] = slab.T


def _s5_out(yt, rb):
    G, CW, R = yt.shape
    C = CW // S5_HG
    W = G * S5_HG
    return pl.pallas_call(
        _s5_out_kernel,
        grid=(R // rb, C // S5_FB),
        in_specs=[pl.BlockSpec((G, S5_FB * S5_HG, rb), lambda i, j: (0, j, i))],
        out_specs=pl.BlockSpec((rb, S5_FB, W), lambda i, j: (i, j, 0)),
        out_shape=jax.ShapeDtypeStruct((R, C, W), F32),
        compiler_params=_cparams("arbitrary", "arbitrary"),
        name="s5_out",
    )(yt)


def _s5_chunk_kernel(*refs, n_chunks):
    params, (z_ref, y_ref), ops = refs[:9], refs[9:11], refs[11:]
    for g in range(z_ref.shape[0]):
        at = lambda rs: [r.at[g] for r in rs]
        _s5_ops_group(*at(params), *at(ops))
        _s5_chunk_group(z_ref.at[g], *at(ops), y_ref.at[g], n_chunks)


def _s5_chunk_group(z_ref, m_ref, wbt_ref, wct_ref, a64_ref, y_ref, n_chunks):
    P = S5_P
    z = z_ref[...]
    R = z.shape[1]
    x = _dot(wbt_ref[...], z)
    y_ref[...] = _dot(m_ref[...], z)
    xr, xi = x[:P], x[P:]
    a = a64_ref[...]
    ar, ai = a[:P], a[P:]
    cidx = lax.broadcasted_iota(I32, (1, R), 1) % n_chunks
    shift = 1
    while shift < n_chunks:
        keep = cidx >= shift
        sr = jnp.where(keep, pltpu.roll(xr, shift, 1), 0.0)
        si = jnp.where(keep, pltpu.roll(xi, shift, 1), 0.0)
        xr, xi = xr + (ar * sr - ai * si), xi + (ar * si + ai * sr)
        ar, ai = ar * ar - ai * ai, 2.0 * ar * ai
        shift *= 2
    keep = cidx >= 1
    s_in = jnp.concatenate([jnp.where(keep, pltpu.roll(xr, 1, 1), 0.0),
                            jnp.where(keep, pltpu.roll(xi, 1, 1), 0.0)], axis=0)
    sh, sl = _split2(s_in)
    wct = wct_ref[...]
    y_ref[...] = y_ref[...] + (_dot(wct, sh) + _dot(wct, sl))


def _s5_chunk(z, lam_re, lam_im, log_step, ct_re, ct_im, bt_re, bt_im, c_re, c_im, n_chunks):
    G, W, R = z.shape
    P, HG = lam_re.shape[1], S5_HG
    P2 = 2 * P
    gps = 2 if G % 2 == 0 else 1
    col = lambda a: a.reshape(G, P, 1)
    g3 = lambda s1, s2: pl.BlockSpec((gps, s1, s2), lambda g: (g, 0, 0))
    return pl.pallas_call(
        functools.partial(_s5_chunk_kernel, n_chunks=n_chunks),
        grid=(G // gps,),
        in_specs=[g3(P, 1), g3(P, 1), g3(1, 1),
                  g3(P, HG), g3(P, HG), g3(P, HG), g3(P, HG), g3(HG, P), g3(HG, P), g3(W, R)],
        out_specs=g3(W, R),
        out_shape=jax.ShapeDtypeStruct((G, W, R), F32),
        scratch_shapes=[pltpu.VMEM((gps, W, W), BF16), pltpu.VMEM((gps, P2, W), BF16),
                        pltpu.VMEM((gps, W, P2), BF16), pltpu.VMEM((gps, P2, 1), F32)],
        compiler_params=_cparams("arbitrary"),
        name="s5_chunk",
    )(col(lam_re), col(lam_im), log_step.reshape(G, 1, 1), ct_re, ct_im, bt_re, bt_im, c_re, c_im, z)


def _gelu_tanh(x):
    return 0.5 * x * (1.0 + jnp.tanh(0.7978845608028654 * (x + 0.044715 * (x * x * x))))


def _s5_glu_kernel(y_ref, u_ref, d_ref, w_ref, b_ref, o_ref):
    y = y_ref[...] + d_ref[...] * u_ref[...]
    y = _gelu_tanh(y)
    z = _dot(y.astype(BF16), w_ref[...]) + b_ref[...]
    o_ref[...] = (y * jax.nn.sigmoid(z)).astype(BF16)


def _s5_glu(y_ssm, proj, d, w_bf16, b, tm):
    T, W = y_ssm.shape
    return pl.pallas_call(
        _s5_glu_kernel,
        grid=(T // tm,),
        in_specs=[
            pl.BlockSpec((tm, W), lambda i: (i, 0)),
            pl.BlockSpec((tm, W), lambda i: (i, 0)),
            pl.BlockSpec((1, W), lambda i: (0, 0)),
            pl.BlockSpec((W, W), lambda i: (0, 0)),
            pl.BlockSpec((1, W), lambda i: (0, 0)),
        ],
        out_specs=pl.BlockSpec((tm, W), lambda i: (i, 0)),
        out_shape=jax.ShapeDtypeStruct((T, W), BF16),
        compiler_params=_cparams("arbitrary"),
        name="s5_glu",
    )(y_ssm, proj, d, w_bf16, b)


def _shift(z, prev_row):
    rolled = pltpu.roll(z, 1, 0)
    first = lax.broadcasted_iota(I32, (z.shape[0], 1), 0) == 0
    return jnp.where(first, prev_row, rolled)


RW_OPERANDS = ("r_t", "k_t", "a_t", "b_t", "k_h", "b_h", "v")
RW_LORA_PAD = 512
RW_PREP_PIECES = 14
RW_STAGES = 24


def _rwkv_chunk_kernel(zr_ref, zk_ref, zv_ref, zl_ref, mu_ref, mul_ref, w0_ref, a0_ref, kk_ref, ka_ref,
                       w2_ref, a2_ref, g2_ref, rk_ref, lnw_ref, lnb_ref,
                       o_ref, z_ref, ys_ref, qs_ref, vs_ref, gs_ref, car_ref, carl_ref, opnd_ref, ptot_ref):
    @pl.when(pl.program_id(1) == 0)
    def _():
        for ref in (z_ref, ys_ref, qs_ref, vs_ref, gs_ref, car_ref, carl_ref, opnd_ref, ptot_ref):
            ref[...] = jnp.zeros_like(ref)

    C = RW_CHUNK
    refs = (zr_ref, zk_ref, zv_ref, zl_ref, mu_ref, mul_ref, w0_ref, a0_ref, kk_ref, ka_ref,
            w2_ref, a2_ref, g2_ref, rk_ref, lnw_ref, lnb_ref,
            o_ref, z_ref, ys_ref, qs_ref, vs_ref, gs_ref, car_ref, carl_ref, opnd_ref, ptot_ref)
    for half in range(2):
        _rwkv_half_step(half, slice(half * C, (half + 1) * C), *refs)


def _rwkv_half_step(par, rows, zr_ref, zk_ref, zv_ref, zl_ref, mu_ref, mul_ref, w0_ref, a0_ref, kk_ref, ka_ref,
                    w2_ref, a2_ref, g2_ref, rk_ref, lnw_ref, lnb_ref,
                    o_ref, z_ref, ys_ref, qs_ref, vs_ref, gs_ref, car_ref, carl_ref, opnd_ref, ptot_ref):
    C, N = RW_CHUNK, RW_N
    NS = zr_ref.shape[0]
    H = zr_ref.shape[2] // N

    low = lax.broadcasted_iota(I32, (C, 2 * N), 1) < N
    inv_n = 1.0 / N

    def head_sums(t):
        s0 = jnp.sum(jnp.where(low, t, 0.0), axis=-1, keepdims=True)
        s1 = jnp.sum(jnp.where(low, 0.0, t), axis=-1, keepdims=True)
        return jnp.where(low, s0, s1)

    def tail():
        for s in range(NS):
            for p in range(H // 2):
                ps = slice(2 * p * N, 2 * (p + 1) * N)
                yp = ys_ref[s, :, ps]
                yc = yp - head_sums(yp) * inv_n
                var = head_sums(yc * yc) * inv_n
                yn = yc * lax.rsqrt(var + RW_GN_EPS) * lnw_ref[:, ps] + lnb_ref[:, ps]
                bonus = head_sums(qs_ref[par, s, :, ps]) * vs_ref[par, s, :, ps].astype(F32)
                o_ref[s, rows, ps] = ((yn + bonus) * gs_ref[par, s, :, ps].astype(F32)).astype(BF16)
                yield

    ri = lax.broadcasted_iota(I32, (C, C), 0)
    ci = lax.broadcasted_iota(I32, (C, C), 1)
    tril = ri >= ci
    stril = ri > ci
    eye = (ri == ci).astype(F32)
    both = jnp.concatenate([stril, tril], axis=0)
    prev = 1 - par
    ops = {}
    for s in range(NS):
        for h in range(H):
            cols = slice(h * N, (h + 1) * N)
            ops[(s, h)] = {name: functools.partial(lambda n, s, cols: opnd_ref[prev, n, s, :, cols], n, s, cols)
                           for n, name in enumerate(RW_OPERANDS)}
            ops[(s, h)]["p_tot"] = functools.partial(lambda s, cols: ptot_ref[prev, s, 0:1, cols], s, cols)
    def head_sums_all(t):
        return jnp.concatenate([head_sums(t[:, 2 * p * N:2 * (p + 1) * N]) for p in range(H // 2)], axis=1)

    def lerp(zz, prev_row, mu):
        return zz + (_shift(zz, prev_row) - zz) * mu

    def prepare():
        for s in range(NS):
            zr, zk, zv, zl = zr_ref[s, rows, :], zk_ref[s, rows, :], zv_ref[s, rows, :], zl_ref[s, rows, :]
            r_f = lerp(zr, car_ref[s, 0:1, :], mu_ref[0:1, :])
            yield
            k_raw = lerp(zk, car_ref[s, 1:2, :], mu_ref[1:2, :])
            yield
            v_f = lerp(zv, car_ref[s, 2:3, :], mu_ref[2:3, :])
            xl = lerp(zl, carl_ref[s, 0:1, :], mul_ref[...])
            car_ref[s, 0:1, :] = zr[C - 1:C, :]
            car_ref[s, 1:2, :] = zk[C - 1:C, :]
            car_ref[s, 2:3, :] = zv[C - 1:C, :]
            carl_ref[s, 0:1, :] = zl[C - 1:C, :]
            yield
            xwa, xg = xl[:, 0:LANES], xl[:, LANES:LANES + g2_ref.shape[0]]
            dw = _dot(jnp.tanh(xwa).astype(BF16), w2_ref[...])
            da = _dot(xwa.astype(BF16), a2_ref[...])
            g_f = _dot(jax.nn.sigmoid(xg).astype(BF16), g2_ref[...])
            yield
            zw = -(w0_ref[...] + dw)
            w_log = -(jnp.maximum(zw, 0.0) + jnp.log(1.0 + jnp.exp(-jnp.abs(zw)))) - 0.5
            yield
            a_sig = jax.nn.sigmoid(a0_ref[...] + da)
            kk = k_raw * kk_ref[...]
            yield
            kk = kk / jnp.maximum(jnp.sqrt(head_sums_all(kk * kk)), 1e-12)
            yield
            k_f = k_raw * (1.0 + (a_sig - 1.0) * ka_ref[...])
            b_f = kk * a_sig
            lw = -jnp.exp(w_log)
            yield
            cs = _dot_exact_lhs(tril.astype(BF16), lw)
            tot = cs[C - 1:C, :]
            yield
            store = lambda name, a: opnd_ref.__setitem__((par, RW_OPERANDS.index(name), s), a.astype(BF16))
            p_inc = jnp.exp(cs)
            store("r_t", r_f * p_inc)
            yield
            p_inv = jnp.exp(-cs)
            store("k_t", k_f * p_inv)
            store("b_t", b_f * p_inv)
            yield
            store("a_t", -kk * jnp.exp(cs - lw))
            yield
            p_rest = jnp.exp(tot - cs)
            store("k_h", k_f * p_rest)
            store("b_h", b_f * p_rest)
            yield
            v_b = v_f.astype(BF16)
            store("v", v_b)
            ptot_ref[par, s, 0:1, :] = jnp.exp(tot)
            qs_ref[par, s] = r_f * k_f * rk_ref[...]
            vs_ref[par, s] = v_b
            gs_ref[par, s] = g_f.astype(BF16)
            yield

    tail_pieces, prep_pieces = tail(), prepare()
    n_pieces = NS * (H // 2 + RW_PREP_PIECES)
    state = dict(ticks=0, done=0)

    def tick():
        state["ticks"] += 1
        due = -(-state["ticks"] * n_pieces // RW_STAGES)
        while state["done"] < due:
            state["done"] += 1
            if next(tail_pieces, "done") == "done":
                next(prep_pieces, None)

    def drain_tail():
        for _ in tail_pieces:
            state["done"] += 1

    _rwkv_heads(ops, stril, tril, both, eye, z_ref, ys_ref, tick, drain_tail)
    for _ in prep_pieces:
        pass


def _rwkv_heads(ops, stril, tril, both, eye, z_ref, ys_ref, tick, drain_tail):
    C, N = RW_CHUNK, RW_N
    hs = list(ops)

    def each(f):
        out = {h: f(h) for h in hs}
        tick()
        return out

    get = lambda name: (lambda h: ops[h][name]())
    a_t, r_t, b_t, k_t, k_h, b_h, p_tot = (get(n) for n in ("a_t", "r_t", "b_t", "k_t", "k_h", "b_h", "p_tot"))
    v = each(lambda h: ops[h]["v"]())
    ar = each(lambda h: jnp.concatenate([a_t(h), r_t(h)], axis=0).astype(BF16))
    m_b = each(lambda h: _dot_nt(ar[h], b_t(h).astype(BF16)))
    m_k = each(lambda h: _dot_nt(ar[h], k_t(h).astype(BF16)))
    l_ab = each(lambda h: jnp.where(stril, m_b[h][:C], 0.0).astype(BF16))
    m_rb = each(lambda h: jnp.where(tril, m_b[h][C:], 0.0).astype(BF16))
    lm_k = each(lambda h: jnp.where(both, m_k[h], 0.0).astype(BF16))
    lmv = each(lambda h: _dot(lm_k[h], v[h]))

    x = each(lambda h: jnp.concatenate([lmv[h][:C], a_t(h).astype(F32)], axis=1))
    y0 = each(lambda h: lmv[h][C:])
    hk = each(lambda h: _dot_tn(k_h(h).astype(BF16), v[h]))
    lp = l_ab
    step = 1
    while step < C:
        x = each(lambda h: x[h] + _dot(lp[h], x[h].astype(BF16)))
        step *= 2
        if step < C:
            lp = each(lambda h: _dot(lp[h], lp[h]).astype(BF16))
    xb = each(lambda h: x[h].astype(BF16))
    yq = each(lambda h: _dot(m_rb[h], xb[h]))
    gh = each(lambda h: _dot_tn(b_h(h).astype(BF16), xb[h]))

    z = each(lambda h: _split2(z_ref[h[0], h[1]]))
    qg = each(lambda h: jnp.concatenate([yq[h][:, N:] + r_t(h).astype(F32),
                                         gh[h][:, N:] + eye * p_tot(h)], axis=0).astype(BF16))
    qgz = each(lambda h: _dot(qg[h], z[h][0]) + _dot(qg[h], z[h][1]))
    yz = each(lambda h: qgz[h][:C])
    gz = each(lambda h: qgz[h][C:])
    drain_tail()
    for s, hd in hs:
        z_ref[s, hd] = (gh[s, hd][:, :N] + hk[s, hd]) + gz[s, hd]
        ys_ref[s, :, hd * N:(hd + 1) * N] = (yq[s, hd][:, :N] + y0[s, hd]) + yz[s, hd]


RW_SEQS_PER_STEP = 1


def _rwkv_chunk(proj, B, L, mu3, mul, w0, a0, k_k, k_a, w2p, a2p, g2p, r_k, ln_w, ln_b):
    T = B * L
    W = w0.shape[1]
    C = RW_CHUNK
    nc = L // C
    H = W // RW_N
    assert H % 2 == 0
    ns = RW_SEQS_PER_STEP if B % RW_SEQS_PER_STEP == 0 else 1
    lw = mul.shape[1]
    proj3 = proj.reshape(B, L, proj.shape[1])
    assert nc % 2 == 0
    np2 = nc // 2
    cols = lambda width, j: pl.BlockSpec((ns, 2 * C, width), lambda bi, i: (bi, jnp.minimum(i, np2 - 1), j))
    full = lambda a: pl.BlockSpec(a.shape, lambda bi, i: (0, 0))
    consts = (mu3, mul, w0, a0, k_k, k_a, w2p, a2p, g2p, r_k, ln_w, ln_b)
    out = pl.pallas_call(
        _rwkv_chunk_kernel,
        grid=(B // ns, np2 + 1),
        in_specs=[cols(W, 1), cols(W, 2), cols(W, 3), cols(lw, 4 * W // lw)] + [full(a) for a in consts],
        out_specs=pl.BlockSpec((ns, 2 * C, W), lambda bi, i: (bi, jnp.maximum(i - 1, 0), 0)),
        out_shape=jax.ShapeDtypeStruct((B, L, W), BF16),
        scratch_shapes=[pltpu.VMEM((ns, H, RW_N, RW_N), F32), pltpu.VMEM((ns, C, W), F32),
                        pltpu.VMEM((2, ns, C, W), F32), pltpu.VMEM((2, ns, C, W), BF16),
                        pltpu.VMEM((2, ns, C, W), BF16),
                        pltpu.VMEM((ns, 8, W), F32), pltpu.VMEM((ns, 8, lw), F32),
                        pltpu.VMEM((2, len(RW_OPERANDS), ns, C, W), BF16), pltpu.VMEM((2, ns, 8, W), F32)],
        compiler_params=_cparams("arbitrary", "arbitrary"),
        name="rwkv_chunk",
    )(proj3, proj3, proj3, proj3, *consts)
    return out.reshape(T, W)


def _first_index_of_max(vals, lane, valid):
    neg = jnp.float32(-jnp.inf)
    masked = jnp.where(valid, vals, neg)
    m = jnp.max(masked, axis=-1, keepdims=True)
    idx = jnp.min(jnp.where(valid & (masked == m), lane, LANES), axis=-1, keepdims=True)
    return m, idx


def _outproj_kernel(s5_ref, rw_ref, x_ref, wt_ref, wb_ref, g_ref, wr_ref, br_ref,
                    h_ref, hn_ref, rec_ref):
    h = x_ref[...] + (_dot(s5_ref[...], wt_ref[...]) + _dot(rw_ref[...], wb_ref[...]))
    h_ref[...] = h
    ms = jnp.mean(h * h, axis=-1, keepdims=True)
    hn = h * lax.rsqrt(ms + RMS_EPS) * g_ref[...]
    hn_ref[...] = _pack_halves(hn).reshape(hn_ref.shape)

    hh, hl = _split2(hn)
    wh, wl = wr_ref[:, :LANES], wr_ref[:, LANES:]
    logits = _dot(hh, wh) + (_dot(hh, wl) + _dot(hl, wh)) + br_ref[...]
    lane = lax.broadcasted_iota(I32, logits.shape, 1)
    is_grp = (lane >= N_EXPERTS) & (lane < N_EXPERTS + N_GROUPS)
    gmax, gidx = _first_index_of_max(logits, lane, is_grp)
    gsum = jnp.sum(jnp.where(is_grp, jnp.exp(logits - gmax), 0.0), axis=-1, keepdims=True)
    p_grp = 1.0 / gsum
    grp = gidx - N_EXPERTS
    in_grp = (lane >= grp * EPG) & (lane < (grp + 1) * EPG)
    m1, i1 = _first_index_of_max(logits, lane, in_grp)
    m2, i2 = _first_index_of_max(logits, lane, in_grp & (lane != i1))
    e = jnp.exp(m2 - m1)
    g1 = p_grp / (1.0 + e)
    g2 = p_grp * e / (1.0 + e)
    rec = jnp.where(lane == 0, i1.astype(F32),
          jnp.where(lane == 1, i2.astype(F32),
          jnp.where(lane == 2, g1, jnp.where(lane == 3, g2, 0.0))))
    rec_ref[...] = rec


def _outproj(s5o, rwo, x2, w_top, w_bot, g, wr, b_route, tm):
    T, D = x2.shape
    W = s5o.shape[1]
    full = lambda a: pl.BlockSpec(a.shape, lambda i: (0, 0))
    return pl.pallas_call(
        _outproj_kernel,
        grid=(T // tm,),
        in_specs=[pl.BlockSpec((tm, W), lambda i: (i, 0)), pl.BlockSpec((tm, W), lambda i: (i, 0)),
                  pl.BlockSpec((tm, D), lambda i: (i, 0)),
                  full(w_top), full(w_bot), full(g), full(wr), full(b_route)],
        out_specs=[pl.BlockSpec((tm, D), lambda i: (i, 0)), pl.BlockSpec((tm, 1, D // 2), lambda i: (i, 0, 0)),
                   pl.BlockSpec((tm, LANES), lambda i: (i, 0))],
        out_shape=[jax.ShapeDtypeStruct((T, D), F32), jax.ShapeDtypeStruct((T, 1, D // 2), U32),
                   jax.ShapeDtypeStruct((T, LANES), F32)],
        compiler_params=_cparams("arbitrary"),
        name="outproj_route",
    )(s5o, rwo, x2, w_top, w_bot, g, wr, b_route)


def _onehots(rec, lane):
    oh0 = (lane == rec[:, 0:1].astype(I32)).astype(F32)
    oh1 = (lane == rec[:, 1:2].astype(I32)).astype(F32)
    return oh0, oh1


def _rank_kernel(rec_ref, rank_ref, cnt_ref, base_ref):
    tb = rec_ref.shape[0]

    @pl.when(pl.program_id(0) == 0)
    def _():
        base_ref[...] = jnp.zeros_like(base_ref)

    lane = lax.broadcasted_iota(I32, (tb, LANES), 1)
    oh0, oh1 = _onehots(rec_ref[...], lane)
    both = oh0 + oh1
    ri = lax.broadcasted_iota(I32, (tb, tb), 0)
    ci = lax.broadcasted_iota(I32, (tb, tb), 1)
    before = _dot((ri > ci).astype(BF16), both.astype(BF16)) + base_ref[0:1, :]
    rank0 = jnp.sum(oh0 * before, axis=-1, keepdims=True)
    rank1 = jnp.sum(oh1 * before, axis=-1, keepdims=True)
    rank_ref[...] = jnp.where(lane == 0, rank0, jnp.where(lane == 1, rank1, 0.0))
    total = base_ref[0:1, :] + jnp.sum(both, axis=0, keepdims=True)
    base_ref[0:1, :] = total
    cnt_ref[...] = jnp.broadcast_to(total, cnt_ref.shape)


def _rank(rec, tb):
    T = rec.shape[0]
    return pl.pallas_call(
        _rank_kernel,
        grid=(T // tb,),
        in_specs=[pl.BlockSpec((tb, LANES), lambda i: (i, 0))],
        out_specs=[pl.BlockSpec((tb, LANES), lambda i: (i, 0)), pl.BlockSpec((8, LANES), lambda i: (0, 0))],
        out_shape=[jax.ShapeDtypeStruct((T, LANES), F32), jax.ShapeDtypeStruct((8, LANES), F32)],
        scratch_shapes=[pltpu.VMEM((8, LANES), F32)],
        compiler_params=_cparams("arbitrary"),
        name="moe_rank",
    )(rec)


def _padded_starts(cnt):
    padded = jnp.ceil(cnt * (1.0 / MOE_BLOCK)) * MOE_BLOCK
    ri = lax.broadcasted_iota(I32, (LANES, LANES), 0)
    ci = lax.broadcasted_iota(I32, (LANES, LANES), 1)
    p8 = jnp.broadcast_to(padded, (8, LANES))
    pend = _dot_exact_lhs_rhs(p8, (ri <= ci).astype(BF16))[0:1, :]
    return pend - padded, pend


def _dot_exact_lhs_rhs(a, b_bf16):
    ah, am, al = _split3(a)
    return _dot(ah, b_bf16) + (_dot(am, b_bf16) + _dot(al, b_bf16))


def _dest_kernel(rec_ref, rank_ref, cnt_ref, dest_ref, blk_ref):
    tb = rec_ref.shape[0]
    cnt = cnt_ref[0:1, :]
    pstart, pend = _padded_starts(cnt)
    lane = lax.broadcasted_iota(I32, (tb, LANES), 1)
    oh0, oh1 = _onehots(rec_ref[...], lane)
    rank = rank_ref[...]
    d0 = jnp.sum(oh0 * pstart, axis=-1, keepdims=True) + rank[:, 0:1]
    d1 = jnp.sum(oh1 * pstart, axis=-1, keepdims=True) + rank[:, 1:2]
    dest_ref[...] = jnp.where(lane == 0, d0, jnp.where(lane == 1, d1, 0.0)).astype(I32)

    @pl.when(pl.program_id(0) == 0)
    def _():
        nb = blk_ref.shape[0]
        blane = lax.broadcasted_iota(I32, (nb, LANES), 1)
        bstart = (lax.broadcasted_iota(I32, (nb, 1), 0) * MOE_BLOCK).astype(F32)
        is_e = blane < N_EXPERTS
        bexp = jnp.sum(jnp.where(is_e & (pend <= bstart), 1.0, 0.0), axis=-1, keepdims=True)
        bexp = jnp.minimum(bexp, N_EXPERTS - 1.0)
        nact = jnp.max(jnp.where(is_e, pend, 0.0), axis=-1, keepdims=True) * (1.0 / MOE_BLOCK)
        brow = lax.broadcasted_iota(I32, (nb, 1), 0)
        out = jnp.where(blane == 0, bexp, jnp.where(blane == 1, nact, 0.0))
        out = jnp.where(brow == nb - 1, jnp.where(is_e, pstart + cnt, 0.0), out)
        out = jnp.where(brow == nb - 2, jnp.where(is_e, pend, 0.0), out)
        blk_ref[...] = out.astype(I32)


def _dest(rec, rank, cnt, tb, nb_rows):
    T = rec.shape[0]
    return pl.pallas_call(
        _dest_kernel,
        grid=(T // tb,),
        in_specs=[pl.BlockSpec((tb, LANES), lambda i: (i, 0)), pl.BlockSpec((tb, LANES), lambda i: (i, 0)),
                  pl.BlockSpec((8, LANES), lambda i: (0, 0))],
        out_specs=[pl.BlockSpec((tb, LANES), lambda i: (i, 0)), pl.BlockSpec((nb_rows, LANES), lambda i: (0, 0))],
        out_shape=[jax.ShapeDtypeStruct((T, LANES), I32), jax.ShapeDtypeStruct((nb_rows, LANES), I32)],
        compiler_params=_cparams("arbitrary"),
        name="moe_dest",
    )(rec, rank, cnt)


def _scatter_kernel(pad_ref, pend_ref, na_ref, dest_ref, hn_ref, xs_ref, zbuf_ref, idx_ref, sa_ref, sb_ref,
                    sem_a, sem_b, isem, zsem, *, tb, n_blocks, n_tail, n_steps):
    i = pl.program_id(0)

    def zero_fill(act):
        def per_expert(e, c):
            row = pad_ref[e]
            left = pend_ref[e] - row
            size = MOE_BLOCK // 2
            while size >= 1:
                @pl.when((left & size) != 0)
                def _(row=row, size=size):
                    act(pltpu.make_async_copy(zbuf_ref.at[pl.ds(0, size)], xs_ref.at[pl.ds(row, size)], zsem))
                row = row + (left & size)
                size //= 2
            return c

        def per_tail_block(k, c):
            @pl.when(na_ref[0] + k < n_blocks)
            def _():
                act(pltpu.make_async_copy(zbuf_ref, xs_ref.at[pl.ds((na_ref[0] + k) * MOE_BLOCK, MOE_BLOCK)], zsem))
            return c

        lax.fori_loop(0, N_EXPERTS, per_expert, 0)
        lax.fori_loop(0, n_tail, per_tail_block, 0)

    @pl.when(i == 0)
    def _():
        zbuf_ref[...] = jnp.zeros_like(zbuf_ref)
        zero_fill(lambda cp: cp.start())

    def idx_copy(step, slot):
        return pltpu.make_async_copy(dest_ref.at[step], idx_ref.at[pl.ds(slot * (2 * tb), 2 * tb)], isem)

    @pl.when(i == 0)
    def _():
        idx_copy(0, 0).start()

    slot = i % 2
    idx_copy(i, slot).wait()

    @pl.when(i + 1 < n_steps)
    def _():
        idx_copy(i + 1, 1 - slot).start()

    hb = tb // 2
    for half, (buf, sem) in enumerate(((sa_ref, sem_a), (sb_ref, sem_b))):
        def wait_rows(buf=buf, sem=sem):
            for _ in range(2):
                pltpu.make_async_copy(buf, xs_ref.at[pl.ds(0, hb)], sem).wait()

        @pl.when(i > 0)
        def _():
            wait_rows()

        buf[...] = hn_ref[half * hb:(half + 1) * hb]

        def issue(t, c, buf=buf, sem=sem, half=half):
            for s in range(2):
                d = idx_ref[slot * (2 * tb) + 2 * (half * hb + t) + s]
                pltpu.make_async_copy(buf.at[pl.ds(t, 1)], xs_ref.at[pl.ds(d, 1)], sem).start(priority=s)
            return c

        lax.fori_loop(0, hb, issue, 0, unroll=8)

    @pl.when(i == n_steps - 1)
    def _():
        for buf, sem in ((sa_ref, sem_a), (sb_ref, sem_b)):
            for _ in range(2):
                pltpu.make_async_copy(buf, xs_ref.at[pl.ds(0, hb)], sem).wait()
        zero_fill(lambda cp: cp.wait())


def _scatter(padrow, pend, nact, dest2, hn, tb, n_blocks):
    T, _, D = hn.shape
    n_tail = n_blocks - (-(-2 * T // MOE_BLOCK))
    hb = tb // 2
    return pl.pallas_call(
        functools.partial(_scatter_kernel, tb=tb, n_blocks=n_blocks, n_tail=n_tail, n_steps=T // tb),
        grid_spec=pltpu.PrefetchScalarGridSpec(
            num_scalar_prefetch=3, grid=(T // tb,),
            in_specs=[pl.BlockSpec(memory_space=pl.ANY),
                      pl.BlockSpec((tb, 1, D), lambda i, pad, pend, na: (i, 0, 0))],
            out_specs=pl.BlockSpec(memory_space=pl.ANY),
            scratch_shapes=[pltpu.VMEM((MOE_BLOCK, 1, D), hn.dtype), pltpu.SMEM((4 * tb,), I32),
                            pltpu.VMEM((hb, 1, D), hn.dtype), pltpu.VMEM((hb, 1, D), hn.dtype),
                            pltpu.SemaphoreType.DMA, pltpu.SemaphoreType.DMA,
                            pltpu.SemaphoreType.DMA, pltpu.SemaphoreType.DMA]),
        out_shape=jax.ShapeDtypeStruct((n_blocks * MOE_BLOCK, 1, D), hn.dtype),
        compiler_params=_cparams("arbitrary"),
        name="moe_scatter",
    )(padrow, pend, nact, dest2, hn)


W_SLOTS = 2


def _expert_kernel(be_ref, na_ref, x_ref, wg_hbm, wu_hbm, wd_hbm, y_ref,
                   wg_buf, wu_buf, wd_buf, wgb_ref, wub_ref, wdb_ref, x2_ref, elist_ref, state_ref, sems):
    j = pl.program_id(0)
    na = na_ref[0]

    def weight_copies(e, slot):
        return [pltpu.make_async_copy(src.at[e], buf.at[slot], sems.at[slot, n])
                for n, (src, buf) in enumerate(((wg_hbm, wg_buf), (wu_hbm, wu_buf), (wd_hbm, wd_buf)))]

    @pl.when(j == 0)
    def _():
        def scan(k, n):
            is_new = (k == 0) | (be_ref[k] != be_ref[jnp.maximum(k - 1, 0)])

            @pl.when(is_new)
            def _():
                elist_ref[n] = be_ref[k]
            return n + is_new.astype(I32)

        n_exp = lax.fori_loop(0, na, scan, 0)
        state_ref[0] = 0
        state_ref[1] = n_exp
        for n in range(W_SLOTS):
            @pl.when(n < n_exp)
            def _():
                for cp in weight_copies(elist_ref[n], n):
                    cp.start()

    @pl.when(j < na)
    def _():
        @pl.when((j == 0) | (be_ref[j] != be_ref[jnp.maximum(j - 1, 0)]))
        def _():
            n = state_ref[0]
            slot = n % W_SLOTS
            for cp in weight_copies(elist_ref[n], slot):
                cp.wait()
            wgb_ref[...] = wg_buf[slot].astype(BF16)
            wub_ref[...] = wu_buf[slot].astype(BF16)
            wdb_ref[...] = wd_buf[slot].astype(BF16)

            @pl.when(n + W_SLOTS < state_ref[1])
            def _():
                for cp in weight_copies(elist_ref[n + W_SLOTS], slot):
                    cp.start()

            state_ref[0] = n + 1

        x2_ref[...] = x_ref[...].reshape(x2_ref.shape)
        lo, hi = _unpack_halves(x2_ref[...])
        lo, hi = lo.astype(BF16), hi.astype(BF16)
        half = lo.shape[1]
        hg = _dot(lo, wgb_ref[:half, :]) + _dot(hi, wgb_ref[half:, :])
        hu = _dot(lo, wub_ref[:half, :]) + _dot(hi, wub_ref[half:, :])
        act = (hg * jax.nn.sigmoid(hg)) * hu
        y = _dot(act.astype(BF16), wdb_ref[...])
        y_ref[...] = _pack_halves(y).reshape(y_ref.shape)

    @pl.when(j >= na)
    def _():
        y_ref[...] = jnp.zeros_like(y_ref)


def _experts(blk_exp, nact, xs, w_gate, w_up, w_down, n_blocks):
    DP = xs.shape[2]
    D = 2 * DP
    rows = n_blocks * MOE_BLOCK
    DE = w_gate.shape[2]
    blk = lambda j, be, na: (jnp.minimum(j, na[0] - 1), 0, 0)
    hbm = pl.BlockSpec(memory_space=pl.ANY)
    return pl.pallas_call(
        _expert_kernel,
        grid_spec=pltpu.PrefetchScalarGridSpec(
            num_scalar_prefetch=2, grid=(n_blocks,),
            in_specs=[pl.BlockSpec((MOE_BLOCK, 1, DP), blk), hbm, hbm, hbm],
            out_specs=pl.BlockSpec((MOE_BLOCK, 1, DP), lambda j, be, na: (j, 0, 0)),
            scratch_shapes=[pltpu.VMEM((W_SLOTS, D, DE), F32), pltpu.VMEM((W_SLOTS, D, DE), F32),
                            pltpu.VMEM((W_SLOTS, DE, D), F32),
                            pltpu.VMEM((D, DE), BF16), pltpu.VMEM((D, DE), BF16), pltpu.VMEM((DE, D), BF16),
                            pltpu.VMEM((MOE_BLOCK, DP), U32), pltpu.SMEM((N_EXPERTS,), I32), pltpu.SMEM((2,), I32),
                            pltpu.SemaphoreType.DMA((W_SLOTS, 3))]),
        out_shape=jax.ShapeDtypeStruct((rows, 1, DP), U32),
        compiler_params=_cparams("arbitrary"),
        name="moe_experts",
    )(blk_exp, nact, xs, w_gate, w_up, w_down)


def _combine_kernel(dest_ref, ys_ref, h_ref, rec_ref, g_ref, o_ref, idx_ref, ya0_ref, ya1_ref, yb0_ref, yb1_ref,
                    y2_ref, sem_a, sem_b, isem, *, normalize, n_steps):
    i = pl.program_id(0)
    tb = h_ref.shape[0]
    hb = tb // 2
    halves = ((ya0_ref, ya1_ref, sem_a), (yb0_ref, yb1_ref, sem_b))

    def idx_copy(step, slot):
        return pltpu.make_async_copy(dest_ref.at[step], idx_ref.at[pl.ds(slot * (2 * tb), 2 * tb)], isem)

    def issue(half, slot):
        bufs, sem = halves[half][:2], halves[half][2]

        def body(t, c):
            for s in range(2):
                d = idx_ref[slot * (2 * tb) + 2 * (half * hb + t) + s]
                pltpu.make_async_copy(ys_ref.at[pl.ds(d, 1)], bufs[s].at[pl.ds(t, 1)], sem).start(priority=s)
            return c

        lax.fori_loop(0, hb, body, 0, unroll=8)

    @pl.when(i == 0)
    def _():
        first = idx_copy(0, 0)
        first.start()
        first.wait()
        issue(0, 0)
        issue(1, 0)
        if n_steps > 1:
            idx_copy(1, 1).start()

    nslot = (i + 1) % 2

    @pl.when(i + 1 < n_steps)
    def _():
        idx_copy(i + 1, nslot).wait()

    rec = rec_ref[...]
    for half, (y0_ref, y1_ref, sem) in enumerate(halves):
        rows = slice(half * hb, (half + 1) * hb)
        pltpu.make_async_copy(ys_ref.at[pl.ds(0, hb)], y0_ref, sem).wait()
        pltpu.make_async_copy(ys_ref.at[pl.ds(0, hb)], y1_ref, sem).wait()
        y2_ref[...] = y0_ref[...].reshape(y2_ref.shape)
        h = h_ref[rows, :] + rec[rows, 2:3] * jnp.concatenate(_unpack_halves(y2_ref[...]), axis=1)
        y2_ref[...] = y1_ref[...].reshape(y2_ref.shape)
        h = h + rec[rows, 3:4] * jnp.concatenate(_unpack_halves(y2_ref[...]), axis=1)
        if normalize:
            ms = jnp.mean(h * h, axis=-1, keepdims=True)
            h = h * lax.rsqrt(ms + RMS_EPS) * g_ref[...]
        o_ref[rows, :] = h

        @pl.when(i + 1 < n_steps)
        def _():
            issue(half, nslot)

    @pl.when(i + 2 < n_steps)
    def _():
        idx_copy(i + 2, i % 2).start()


def _combine(dest2, ys, h, rec, g, tb, normalize):
    T, D = h.shape
    hb = tb // 2
    return pl.pallas_call(
        functools.partial(_combine_kernel, normalize=normalize, n_steps=T // tb),
        grid=(T // tb,),
        in_specs=[pl.BlockSpec(memory_space=pl.ANY), pl.BlockSpec(memory_space=pl.ANY),
                  pl.BlockSpec((tb, D), lambda i: (i, 0)), pl.BlockSpec((tb, LANES), lambda i: (i, 0)),
                  pl.BlockSpec((1, D), lambda i: (0, 0))],
        out_specs=pl.BlockSpec((tb, D), lambda i: (i, 0)),
        out_shape=jax.ShapeDtypeStruct((T, D), F32),
        scratch_shapes=[pltpu.SMEM((4 * tb,), I32)] + [pltpu.VMEM((hb, 1, D // 2), U32)] * 4 +
                       [pltpu.VMEM((hb, D // 2), U32)] + [pltpu.SemaphoreType.DMA] * 3,
        compiler_params=_cparams("arbitrary"),
        name="moe_combine",
    )(dest2, ys, h, rec, g)


def _pad_cols(a, n):
    return jnp.pad(a, ((0, 0), (0, n - a.shape[1])))


def _pad_rows(a, n, at=0):
    return jnp.pad(a, ((at, n - a.shape[0] - at), (0, 0)))


def _layer(x2, B, L, p):
    T, D = x2.shape
    W = p["s5_d"].shape[0]
    G = p["s5_lambda_re"].shape[0]
    assert W == G * S5_HG
    dl, al, gl = p["rwkv_w2"].shape[0], p["rwkv_a2"].shape[0], p["rwkv_g2"].shape[0]

    assert dl + al == LANES and gl <= RW_LORA_PAD - LANES
    w_in_r = _pad_cols(p["w_in"], 4 * W + RW_LORA_PAD).astype(BF16)
    n_proj = w_in_r.shape[1]
    tm_in = min(1024, T)
    proj = _inproj(x2, p["norm_mix_g"].reshape(1, D), w_in_r, tm_in, n_proj // INPROJ_N_TILES)

    C = S5_CHUNK
    nc = L // C
    ct_re = jnp.swapaxes(p["s5_c_re"], 1, 2)
    ct_im = jnp.swapaxes(p["s5_c_im"], 1, 2)
    rb = min(LANES, B * nc)
    z = _s5_in(proj.reshape(B * nc, C, proj.shape[1]), G, rb)
    y_t = _s5_chunk(z, p["s5_lambda_re"], p["s5_lambda_im"], p["s5_log_step"],
                    ct_re, ct_im, p["s5_b_re"], p["s5_b_im"], p["s5_c_re"], p["s5_c_im"], nc)
    y_ssm = _s5_out(y_t, rb).reshape(T, W)
    tm = min(512, T)
    s5_out = _s5_glu(y_ssm, proj, p["s5_d"].reshape(1, W), p["s5_w_glu"].astype(BF16),
                     p["s5_b_glu"].reshape(1, W), tm)

    mu = p["rwkv_mu"]
    mu3 = _pad_rows(mu[:3 * W].reshape(3, W), 8)
    mul = _pad_cols(mu[None, 3 * W:], RW_LORA_PAD)
    w2p = _pad_rows(p["rwkv_w2"], LANES).astype(BF16)
    a2p = _pad_rows(p["rwkv_a2"], LANES, at=dl).astype(BF16)
    g2p = _pad_rows(p["rwkv_g2"], -(-gl // LANES) * LANES).astype(BF16)
    row = lambda a: a.reshape(1, W)
    rw_out = _rwkv_chunk(proj, B, L, mu3, mul, row(p["rwkv_w0"]), row(p["rwkv_a0"]), row(p["rwkv_k_k"]),
                         row(p["rwkv_k_a"]), w2p, a2p, g2p, row(p["rwkv_r_k"]), row(p["rwkv_ln_w"]),
                         row(p["rwkv_ln_b"]))

    w_out = p["w_out"].astype(BF16)
    w_route = jnp.concatenate([p["w_route_exp"], p["w_route_grp"]], axis=1)
    w_route = _pad_cols(w_route, LANES)
    wr_hi = w_route.astype(BF16)
    wr_lo = (w_route - wr_hi.astype(F32)).astype(BF16)
    wr = jnp.concatenate([wr_hi, wr_lo], axis=1)
    b_route = _pad_cols(jnp.concatenate([p["b_route_exp"], p["b_route_grp"]])[None, :], LANES)
    tm_o = min(512, T)
    h, hn, rec = _outproj(s5_out, rw_out, x2, w_out[:W], w_out[W:], p["norm_ffn_g"].reshape(1, D),
                          wr, b_route, tm_o)

    tb = min(1024, T)
    rank, cnt = _rank(rec, tb)
    n_blocks = -(-2 * T // MOE_BLOCK) + N_EXPERTS
    nb_rows = -(-(n_blocks + 1) // 8) * 8
    dest, blk = _dest(rec, rank, cnt, tb, nb_rows)
    blk_exp = blk[:n_blocks, 0]
    nact = blk[0:1, 1]
    padrow, pend = blk[nb_rows - 1, :N_EXPERTS], blk[nb_rows - 2, :N_EXPERTS]
    ts = min(1024, T)
    dest2 = dest[:, :2].reshape(T // ts, 2 * ts)

    xs = _scatter(padrow, pend, nact, dest2, hn, ts, n_blocks)
    ys = _experts(blk_exp, nact, xs, p["w_gate"], p["w_up"], p["w_down"], n_blocks)
    return dest2, ys, h, rec, ts


def kernel(x, norm_mix_g, w_in, s5_lambda_re, s5_lambda_im, s5_log_step, s5_b_re, s5_b_im, s5_c_re, s5_c_im, s5_d, s5_w_glu, s5_b_glu, rwkv_mu, rwkv_w0, rwkv_w2, rwkv_a0, rwkv_a2, rwkv_g2, rwkv_k_k, rwkv_k_a, rwkv_r_k, rwkv_ln_w, rwkv_ln_b, w_out, norm_ffn_g, w_route_grp, b_route_grp, w_route_exp, b_route_exp, w_gate, w_up, w_down, norm_final_g):
    B, L, D = x.shape
    params = dict(
        norm_mix_g=norm_mix_g, w_in=w_in, s5_lambda_re=s5_lambda_re, s5_lambda_im=s5_lambda_im,
        s5_log_step=s5_log_step, s5_b_re=s5_b_re, s5_b_im=s5_b_im, s5_c_re=s5_c_re, s5_c_im=s5_c_im,
        s5_d=s5_d, s5_w_glu=s5_w_glu, s5_b_glu=s5_b_glu, rwkv_mu=rwkv_mu, rwkv_w0=rwkv_w0, rwkv_w2=rwkv_w2,
        rwkv_a0=rwkv_a0, rwkv_a2=rwkv_a2, rwkv_g2=rwkv_g2, rwkv_k_k=rwkv_k_k, rwkv_k_a=rwkv_k_a,
        rwkv_r_k=rwkv_r_k, rwkv_ln_w=rwkv_ln_w, rwkv_ln_b=rwkv_ln_b, w_out=w_out, norm_ffn_g=norm_ffn_g,
        w_route_grp=w_route_grp, b_route_grp=b_route_grp, w_route_exp=w_route_exp, b_route_exp=b_route_exp,
        w_gate=w_gate, w_up=w_up, w_down=w_down)
    depth = norm_mix_g.shape[0]
    h2 = x.reshape(B * L, D)
    for l in range(depth):
        p = {k_: v_[l] for k_, v_ in params.items()}
        dest2, ys, h, rec, ts = _layer(h2, B, L, p)
        h2 = _combine(dest2, ys, h, rec, norm_final_g.reshape(1, D), ts, normalize=(l == depth - 1))
    return h2.reshape(B, L, D)
```

```python
import functools

import jax
import jax.numpy as jnp
from jax import lax
from jax.experimental import pallas as pl
from jax.experimental.pallas import tpu as pltpu

F32 = jnp.float32
BF16 = jnp.bfloat16
I32 = jnp.int32

RMS_EPS = 1e-6
S5_CHUNK = 64
S5_HG = 16
S5_P = 64
RW_N = 64
RW_CHUNK = 64
RW_GN_EPS = 64e-5
N_GROUPS = 8
EPG = 8
N_EXPERTS = 64
MOE_BLOCK = 256
LANES = 128
INPROJ_N_TILES = 4
VMEM_LIMIT = 56 * 1024 * 1024


def _cparams(*sem, **kw):
    return pltpu.CompilerParams(dimension_semantics=sem, vmem_limit_bytes=VMEM_LIMIT, **kw)


def _split2(x):
    hi = x.astype(BF16)
    lo = (x - hi.astype(F32)).astype(BF16)
    return hi, lo


def _split3(x):
    hi = x.astype(BF16)
    r = x - hi.astype(F32)
    mid = r.astype(BF16)
    lo = (r - mid.astype(F32)).astype(BF16)
    return hi, mid, lo


def _dot(a, b):
    return jnp.dot(a, b, preferred_element_type=F32)


def _dot_nt(a, b):
    return lax.dot_general(a, b, (((1,), (1,)), ((), ())), preferred_element_type=F32)


def _dot_tn(a, b):
    return lax.dot_general(a, b, (((0,), (0,)), ((), ())), preferred_element_type=F32)


def _dot_x3(a, b):
    ah, al = _split2(a)
    bh, bl = _split2(b)
    return _dot(ah, bh) + (_dot(ah, bl) + _dot(al, bh))


def _dot_exact_lhs(a_bf16, b):
    bh, bm, bl = _split3(b)
    return _dot(a_bf16, bh) + (_dot(a_bf16, bm) + _dot(a_bf16, bl))


def _cmul(ar, ai, br, bi):
    return ar * br - ai * bi, ar * bi + ai * br


U32 = jnp.uint32


def _pack_halves(x):
    half = x.shape[1] // 2
    lo = lax.bitcast_convert_type(x[:, :half].astype(BF16).astype(F32), U32)
    hi = lax.bitcast_convert_type(x[:, half:].astype(BF16).astype(F32), U32)
    return (lo >> 16) | hi


def _unpack_halves(p):
    lo = lax.bitcast_convert_type(p << 16, F32)
    hi = lax.bitcast_convert_type(p & jnp.uint32(0xFFFF0000), F32)
    return lo, hi


def _inproj_kernel(x_ref, g_ref, w_ref, o_ref, hn_ref):
    @pl.when(pl.program_id(1) == 0)
    def _():
        x = x_ref[...]
        ms = jnp.mean(x * x, axis=-1, keepdims=True)
        hn_ref[...] = (x * lax.rsqrt(ms + RMS_EPS) * g_ref[...]).astype(BF16)

    o_ref[...] = _dot(hn_ref[...], w_ref[...])


def _inproj(x2, g, w_bf16, tm, tn):
    T, D = x2.shape
    N = w_bf16.shape[1]
    return pl.pallas_call(
        _inproj_kernel,
        grid=(T // tm, N // tn),
        in_specs=[
            pl.BlockSpec((tm, D), lambda i, j: (i, 0)),
            pl.BlockSpec((1, D), lambda i, j: (0, 0)),
            pl.BlockSpec((D, tn), lambda i, j: (0, j)),
        ],
        out_specs=pl.BlockSpec((tm, tn), lambda i, j: (i, j)),
        out_shape=jax.ShapeDtypeStruct((T, N), F32),
        scratch_shapes=[pltpu.VMEM((tm, D), BF16)],
        compiler_params=_cparams("arbitrary", "arbitrary"),
        name="inproj",
    )(x2, g, w_bf16)


def _binpow(ar, ai, expo, nbits):
    pr = jnp.ones(expo.shape, F32)
    pi = jnp.zeros(expo.shape, F32)
    sr, si = ar, ai
    for bit in range(nbits):
        m = ((expo >> bit) & 1) == 1
        nr, ni = _cmul(pr, pi, sr, si)
        pr = jnp.where(m, nr, pr)
        pi = jnp.where(m, ni, pi)
        if bit + 1 < nbits:
            sr, si = _cmul(sr, si, sr, si)
    return pr, pi


def _frame_powers(ar, ai, reverse, plus_one):
    C, HG, P = S5_CHUNK, S5_HG, ar.shape[0]
    fpt = LANES // HG
    j = lax.broadcasted_iota(I32, (P, LANES), 1) // HG
    inner_r, inner_i = _binpow(ar, ai, (fpt - 1 - j) if reverse else j, (fpt - 1).bit_length())
    sr, si = ar, ai
    for _ in range(fpt.bit_length() - 1):
        sr, si = _cmul(sr, si, sr, si)
    outer = [(ar, ai) if plus_one else (jnp.ones_like(ar), jnp.zeros_like(ai))]
    for _ in range(C // fpt - 1):
        outer.append(_cmul(outer[-1][0], outer[-1][1], sr, si))
    if reverse:
        outer = outer[::-1]
    tiles = [_cmul(inner_r, inner_i, o_r, o_i) for o_r, o_i in outer]
    return (jnp.concatenate([t[0] for t in tiles], axis=1), jnp.concatenate([t[1] for t in tiles], axis=1))


def _zoh(lr, li, dt):
    mag = jnp.exp(lr * dt)
    ang = li * dt
    ar, ai = mag * jnp.cos(ang), mag * jnp.sin(ang)
    den = lr * lr + li * li
    nr, ni = ar - 1.0, ai
    fr = (nr * lr + ni * li) / den
    fi = (ni * lr - nr * li) / den
    return ar, ai, fr, fi


def _s5_ops_group(lrc_ref, lic_ref, dt_ref, ct_re_ref, ct_im_ref, bt_re_ref, bt_im_ref, c_re_ref, c_im_ref,
                  m_ref, wbt_ref, wct_ref, a64_ref):
    C, HG, P = S5_CHUNK, S5_HG, S5_P
    W = C * HG
    dt = jnp.exp(dt_ref[...])
    ar_c, ai_c, fr_c, fi_c = _zoh(lrc_ref[...], lic_ref[...], dt)

    pr, pi = _frame_powers(ar_c, ai_c, False, True)
    qr, qi = _frame_powers(ar_c, ai_c, True, False)

    sel = (lax.broadcasted_iota(I32, (HG, W), 1) % HG == lax.broadcasted_iota(I32, (HG, W), 0)).astype(BF16)
    tile = lambda a: _dot_exact_lhs_rhs(a, sel)

    c1_re, c1_im = _cmul(tile(ct_re_ref[...]), tile(ct_im_ref[...]), pr, pi)
    wct_ref[...] = jnp.concatenate([c1_re, -c1_im], axis=0).T.astype(BF16)

    bb_re, bb_im = _cmul(bt_re_ref[...], bt_im_ref[...], fr_c, fi_c)
    ab_re, ab_im = _cmul(tile(bb_re), tile(bb_im), qr, qi)
    wbt_ref[:P, :] = ab_re.astype(BF16)
    wbt_ref[P:, :] = ab_im.astype(BF16)

    cr, ci = ar_c, ai_c
    for _ in range(C.bit_length() - 1):
        cr, ci = _cmul(cr, ci, cr, ci)
    a64_ref[...] = jnp.concatenate([cr, ci], axis=0)

    strip = _dot_x3(c_re_ref[...], ab_re) - _dot_x3(c_im_ref[...], ab_im)
    ext = jnp.concatenate([strip, jnp.zeros((HG, W), F32)], axis=1)
    for t in range(C):
        off = (C - 1 - t) * HG
        m_ref[t * HG:(t + 1) * HG, :] = ext[:, off:off + W].astype(BF16)


S5_FB = 8


def _s5_in_kernel(x_ref, z_ref):
    G, _, rb = z_ref.shape
    for f in range(S5_FB):
        xt = x_ref[:, f, :].T
        z_ref[:, f * S5_HG:(f + 1) * S5_HG, :] = xt.reshape(G, S5_HG, rb).astype(BF16)


def _s5_in(proj3, G, rb):
    R, C, _ = proj3.shape
    W = G * S5_HG
    return pl.pallas_call(
        _s5_in_kernel,
        grid=(R // rb, C // S5_FB),
        in_specs=[pl.BlockSpec((rb, S5_FB, W), lambda i, j: (i, j, 0))],
        out_specs=pl.BlockSpec((G, S5_FB * S5_HG, rb), lambda i, j: (0, j, i)),
        out_shape=jax.ShapeDtypeStruct((G, C * S5_HG, R), BF16),
        compiler_params=_cparams("arbitrary", "arbitrary"),
        name="s5_in",
    )(proj3)


def _s5_out_kernel(yt_ref, o_ref):
    G, _, rb = yt_ref.shape
    for f in range(S5_FB):
        slab = yt_ref[:, f * S5_HG:(f + 1) * S5_HG, :].reshape(G * S5_HG, rb)
        o_ref[:, f, :] = slab.T


def _s5_out(yt, rb):
    G, CW, R = yt.shape
    C = CW // S5_HG
    W = G * S5_HG
    return pl.pallas_call(
        _s5_out_kernel,
        grid=(R // rb, C // S5_FB),
        in_specs=[pl.BlockSpec((G, S5_FB * S5_HG, rb), lambda i, j: (0, j, i))],
        out_specs=pl.BlockSpec((rb, S5_FB, W), lambda i, j: (i, j, 0)),
        out_shape=jax.ShapeDtypeStruct((R, C, W), F32),
        compiler_params=_cparams("arbitrary", "arbitrary"),
        name="s5_out",
    )(yt)


def _s5_chunk_kernel(*refs, n_chunks):
    params, (z_ref, y_ref), ops = refs[:9], refs[9:11], refs[11:]
    for g in range(z_ref.shape[0]):
        at = lambda rs: [r.at[g] for r in rs]
        _s5_ops_group(*at(params), *at(ops))
        _s5_chunk_group(z_ref.at[g], *at(ops), y_ref.at[g], n_chunks)


def _s5_chunk_group(z_ref, m_ref, wbt_ref, wct_ref, a64_ref, y_ref, n_chunks):
    P = S5_P
    z = z_ref[...]
    R = z.shape[1]
    x = _dot(wbt_ref[...], z)
    y_ref[...] = _dot(m_ref[...], z)
    xr, xi = x[:P], x[P:]
    a = a64_ref[...]
    ar, ai = a[:P], a[P:]
    cidx = lax.broadcasted_iota(I32, (1, R), 1) % n_chunks
    shift = 1
    while shift < n_chunks:
        keep = cidx >= shift
        sr = jnp.where(keep, pltpu.roll(xr, shift, 1), 0.0)
        si = jnp.where(keep, pltpu.roll(xi, shift, 1), 0.0)
        xr, xi = xr + (ar * sr - ai * si), xi + (ar * si + ai * sr)
        ar, ai = ar * ar - ai * ai, 2.0 * ar * ai
        shift *= 2
    keep = cidx >= 1
    s_in = jnp.concatenate([jnp.where(keep, pltpu.roll(xr, 1, 1), 0.0),
                            jnp.where(keep, pltpu.roll(xi, 1, 1), 0.0)], axis=0)
    sh, sl = _split2(s_in)
    wct = wct_ref[...]
    y_ref[...] = y_ref[...] + (_dot(wct, sh) + _dot(wct, sl))


def _s5_chunk(z, lam_re, lam_im, log_step, ct_re, ct_im, bt_re, bt_im, c_re, c_im, n_chunks):
    G, W, R = z.shape
    P, HG = lam_re.shape[1], S5_HG
    P2 = 2 * P
    gps = 2 if G % 2 == 0 else 1
    col = lambda a: a.reshape(G, P, 1)
    g3 = lambda s1, s2: pl.BlockSpec((gps, s1, s2), lambda g: (g, 0, 0))
    return pl.pallas_call(
        functools.partial(_s5_chunk_kernel, n_chunks=n_chunks),
        grid=(G // gps,),
        in_specs=[g3(P, 1), g3(P, 1), g3(1, 1),
                  g3(P, HG), g3(P, HG), g3(P, HG), g3(P, HG), g3(HG, P), g3(HG, P), g3(W, R)],
        out_specs=g3(W, R),
        out_shape=jax.ShapeDtypeStruct((G, W, R), F32),
        scratch_shapes=[pltpu.VMEM((gps, W, W), BF16), pltpu.VMEM((gps, P2, W), BF16),
                        pltpu.VMEM((gps, W, P2), BF16), pltpu.VMEM((gps, P2, 1), F32)],
        compiler_params=_cparams("arbitrary"),
        name="s5_chunk",
    )(col(lam_re), col(lam_im), log_step.reshape(G, 1, 1), ct_re, ct_im, bt_re, bt_im, c_re, c_im, z)


def _gelu_tanh(x):
    return 0.5 * x * (1.0 + jnp.tanh(0.7978845608028654 * (x + 0.044715 * (x * x * x))))


def _s5_glu_kernel(y_ref, u_ref, d_ref, w_ref, b_ref, o_ref):
    y = y_ref[...] + d_ref[...] * u_ref[...]
    y = _gelu_tanh(y)
    z = _dot(y.astype(BF16), w_ref[...]) + b_ref[...]
    o_ref[...] = (y * jax.nn.sigmoid(z)).astype(BF16)


def _s5_glu(y_ssm, proj, d, w_bf16, b, tm):
    T, W = y_ssm.shape
    return pl.pallas_call(
        _s5_glu_kernel,
        grid=(T // tm,),
        in_specs=[
            pl.BlockSpec((tm, W), lambda i: (i, 0)),
            pl.BlockSpec((tm, W), lambda i: (i, 0)),
            pl.BlockSpec((1, W), lambda i: (0, 0)),
            pl.BlockSpec((W, W), lambda i: (0, 0)),
            pl.BlockSpec((1, W), lambda i: (0, 0)),
        ],
        out_specs=pl.BlockSpec((tm, W), lambda i: (i, 0)),
        out_shape=jax.ShapeDtypeStruct((T, W), BF16),
        compiler_params=_cparams("arbitrary"),
        name="s5_glu",
    )(y_ssm, proj, d, w_bf16, b)


def _shift(z, prev_row):
    rolled = pltpu.roll(z, 1, 0)
    first = lax.broadcasted_iota(I32, (z.shape[0], 1), 0) == 0
    return jnp.where(first, prev_row, rolled)


RW_OPERANDS = ("r_t", "k_t", "a_t", "b_t", "k_h", "b_h", "v")
RW_LORA_PAD = 512
RW_PREP_PIECES = 14
RW_STAGES = 24


def _rwkv_chunk_kernel(zr_ref, zk_ref, zv_ref, zl_ref, mu_ref, mul_ref, w0_ref, a0_ref, kk_ref, ka_ref,
                       w2_ref, a2_ref, g2_ref, rk_ref, lnw_ref, lnb_ref,
                       o_ref, z_ref, ys_ref, qs_ref, vs_ref, gs_ref, car_ref, carl_ref, opnd_ref, ptot_ref):
    @pl.when(pl.program_id(1) == 0)
    def _():
        for ref in (z_ref, ys_ref, qs_ref, vs_ref, gs_ref, car_ref, carl_ref, opnd_ref, ptot_ref):
            ref[...] = jnp.zeros_like(ref)

    C = RW_CHUNK
    refs = (zr_ref, zk_ref, zv_ref, zl_ref, mu_ref, mul_ref, w0_ref, a0_ref, kk_ref, ka_ref,
            w2_ref, a2_ref, g2_ref, rk_ref, lnw_ref, lnb_ref,
            o_ref, z_ref, ys_ref, qs_ref, vs_ref, gs_ref, car_ref, carl_ref, opnd_ref, ptot_ref)
    for half in range(2):
        _rwkv_half_step(half, slice(half * C, (half + 1) * C), *refs)


def _rwkv_half_step(par, rows, zr_ref, zk_ref, zv_ref, zl_ref, mu_ref, mul_ref, w0_ref, a0_ref, kk_ref, ka_ref,
                    w2_ref, a2_ref, g2_ref, rk_ref, lnw_ref, lnb_ref,
                    o_ref, z_ref, ys_ref, qs_ref, vs_ref, gs_ref, car_ref, carl_ref, opnd_ref, ptot_ref):
    C, N = RW_CHUNK, RW_N
    NS = zr_ref.shape[0]
    H = zr_ref.shape[2] // N

    low = lax.broadcasted_iota(I32, (C, 2 * N), 1) < N
    inv_n = 1.0 / N

    def head_sums(t):
        s0 = jnp.sum(jnp.where(low, t, 0.0), axis=-1, keepdims=True)
        s1 = jnp.sum(jnp.where(low, 0.0, t), axis=-1, keepdims=True)
        return jnp.where(low, s0, s1)

    def tail():
        for s in range(NS):
            for p in range(H // 2):
                ps = slice(2 * p * N, 2 * (p + 1) * N)
                yp = ys_ref[s, :, ps]
                yc = yp - head_sums(yp) * inv_n
                var = head_sums(yc * yc) * inv_n
                yn = yc * lax.rsqrt(var + RW_GN_EPS) * lnw_ref[:, ps] + lnb_ref[:, ps]
                bonus = head_sums(qs_ref[par, s, :, ps]) * vs_ref[par, s, :, ps].astype(F32)
                o_ref[s, rows, ps] = ((yn + bonus) * gs_ref[par, s, :, ps].astype(F32)).astype(BF16)
                yield

    ri = lax.broadcasted_iota(I32, (C, C), 0)
    ci = lax.broadcasted_iota(I32, (C, C), 1)
    tril = ri >= ci
    stril = ri > ci
    eye = (ri == ci).astype(F32)
    both = jnp.concatenate([stril, tril], axis=0)
    prev = 1 - par
    ops = {}
    for s in range(NS):
        for h in range(H):
            cols = slice(h * N, (h + 1) * N)
            ops[(s, h)] = {name: functools.partial(lambda n, s, cols: opnd_ref[prev, n, s, :, cols], n, s, cols)
                           for n, name in enumerate(RW_OPERANDS)}
            ops[(s, h)]["p_tot"] = functools.partial(lambda s, cols: ptot_ref[prev, s, 0:1, cols], s, cols)
    def head_sums_all(t):
        return jnp.concatenate([head_sums(t[:, 2 * p * N:2 * (p + 1) * N]) for p in range(H // 2)], axis=1)

    def lerp(zz, prev_row, mu):
        return zz + (_shift(zz, prev_row) - zz) * mu

    def prepare():
        for s in range(NS):
            zr, zk, zv, zl = zr_ref[s, rows, :], zk_ref[s, rows, :], zv_ref[s, rows, :], zl_ref[s, rows, :]
            r_f = lerp(zr, car_ref[s, 0:1, :], mu_ref[0:1, :])
            yield
            k_raw = lerp(zk, car_ref[s, 1:2, :], mu_ref[1:2, :])
            yield
            v_f = lerp(zv, car_ref[s, 2:3, :], mu_ref[2:3, :])
            xl = lerp(zl, carl_ref[s, 0:1, :], mul_ref[...])
            car_ref[s, 0:1, :] = zr[C - 1:C, :]
            car_ref[s, 1:2, :] = zk[C - 1:C, :]
            car_ref[s, 2:3, :] = zv[C - 1:C, :]
            carl_ref[s, 0:1, :] = zl[C - 1:C, :]
            yield
            xwa, xg = xl[:, 0:LANES], xl[:, LANES:LANES + g2_ref.shape[0]]
            dw = _dot(jnp.tanh(xwa).astype(BF16), w2_ref[...])
            da = _dot(xwa.astype(BF16), a2_ref[...])
            g_f = _dot(jax.nn.sigmoid(xg).astype(BF16), g2_ref[...])
            yield
            zw = -(w0_ref[...] + dw)
            w_log = -(jnp.maximum(zw, 0.0) + jnp.log(1.0 + jnp.exp(-jnp.abs(zw)))) - 0.5
            yield
            a_sig = jax.nn.sigmoid(a0_ref[...] + da)
            kk = k_raw * kk_ref[...]
            yield
            kk = kk / jnp.maximum(jnp.sqrt(head_sums_all(kk * kk)), 1e-12)
            yield
            k_f = k_raw * (1.0 + (a_sig - 1.0) * ka_ref[...])
            b_f = kk * a_sig
            lw = -jnp.exp(w_log)
            yield
            cs = _dot_exact_lhs(tril.astype(BF16), lw)
            tot = cs[C - 1:C, :]
            yield
            store = lambda name, a: opnd_ref.__setitem__((par, RW_OPERANDS.index(name), s), a.astype(BF16))
            p_inc = jnp.exp(cs)
            store("r_t", r_f * p_inc)
            yield
            p_inv = jnp.exp(-cs)
            store("k_t", k_f * p_inv)
            store("b_t", b_f * p_inv)
            yield
            store("a_t", -kk * jnp.exp(cs - lw))
            yield
            p_rest = jnp.exp(tot - cs)
            store("k_h", k_f * p_rest)
            store("b_h", b_f * p_rest)
            yield
            v_b = v_f.astype(BF16)
            store("v", v_b)
            ptot_ref[par, s, 0:1, :] = jnp.exp(tot)
            qs_ref[par, s] = r_f * k_f * rk_ref[...]
            vs_ref[par, s] = v_b
            gs_ref[par, s] = g_f.astype(BF16)
            yield

    tail_pieces, prep_pieces = tail(), prepare()
    n_pieces = NS * (H // 2 + RW_PREP_PIECES)
    state = dict(ticks=0, done=0)

    def tick():
        state["ticks"] += 1
        due = -(-state["ticks"] * n_pieces // RW_STAGES)
        while state["done"] < due:
            state["done"] += 1
            if next(tail_pieces, "done") == "done":
                next(prep_pieces, None)

    def drain_tail():
        for _ in tail_pieces:
            state["done"] += 1

    _rwkv_heads(ops, stril, tril, both, eye, z_ref, ys_ref, tick, drain_tail)
    for _ in prep_pieces:
        pass


def _rwkv_heads(ops, stril, tril, both, eye, z_ref, ys_ref, tick, drain_tail):
    C, N = RW_CHUNK, RW_N
    hs = list(ops)

    def each(f):
        out = {h: f(h) for h in hs}
        tick()
        return out

    get = lambda name: (lambda h: ops[h][name]())
    a_t, r_t, b_t, k_t, k_h, b_h, p_tot = (get(n) for n in ("a_t", "r_t", "b_t", "k_t", "k_h", "b_h", "p_tot"))
    v = each(lambda h: ops[h]["v"]())
    ar = each(lambda h: jnp.concatenate([a_t(h), r_t(h)], axis=0).astype(BF16))
    m_b = each(lambda h: _dot_nt(ar[h], b_t(h).astype(BF16)))
    m_k = each(lambda h: _dot_nt(ar[h], k_t(h).astype(BF16)))
    l_ab = each(lambda h: jnp.where(stril, m_b[h][:C], 0.0).astype(BF16))
    m_rb = each(lambda h: jnp.where(tril, m_b[h][C:], 0.0).astype(BF16))
    lm_k = each(lambda h: jnp.where(both, m_k[h], 0.0).astype(BF16))
    lmv = each(lambda h: _dot(lm_k[h], v[h]))

    x = each(lambda h: jnp.concatenate([lmv[h][:C], a_t(h).astype(F32)], axis=1))
    y0 = each(lambda h: lmv[h][C:])
    hk = each(lambda h: _dot_tn(k_h(h).astype(BF16), v[h]))
    lp = l_ab
    step = 1
    while step < C:
        x = each(lambda h: x[h] + _dot(lp[h], x[h].astype(BF16)))
        step *= 2
        if step < C:
            lp = each(lambda h: _dot(lp[h], lp[h]).astype(BF16))
    xb = each(lambda h: x[h].astype(BF16))
    yq = each(lambda h: _dot(m_rb[h], xb[h]))
    gh = each(lambda h: _dot_tn(b_h(h).astype(BF16), xb[h]))

    z = each(lambda h: _split2(z_ref[h[0], h[1]]))
    qg = each(lambda h: jnp.concatenate([yq[h][:, N:] + r_t(h).astype(F32),
                                         gh[h][:, N:] + eye * p_tot(h)], axis=0).astype(BF16))
    qgz = each(lambda h: _dot(qg[h], z[h][0]) + _dot(qg[h], z[h][1]))
    yz = each(lambda h: qgz[h][:C])
    gz = each(lambda h: qgz[h][C:])
    drain_tail()
    for s, hd in hs:
        z_ref[s, hd] = (gh[s, hd][:, :N] + hk[s, hd]) + gz[s, hd]
        ys_ref[s, :, hd * N:(hd + 1) * N] = (yq[s, hd][:, :N] + y0[s, hd]) + yz[s, hd]


RW_SEQS_PER_STEP = 1


def _rwkv_chunk(proj, B, L, mu3, mul, w0, a0, k_k, k_a, w2p, a2p, g2p, r_k, ln_w, ln_b):
    T = B * L
    W = w0.shape[1]
    C = RW_CHUNK
    nc = L // C
    H = W // RW_N
    assert H % 2 == 0
    ns = RW_SEQS_PER_STEP if B % RW_SEQS_PER_STEP == 0 else 1
    lw = mul.shape[1]
    proj3 = proj.reshape(B, L, proj.shape[1])
    assert nc % 2 == 0
    np2 = nc // 2
    cols = lambda width, j: pl.BlockSpec((ns, 2 * C, width), lambda bi, i: (bi, jnp.minimum(i, np2 - 1), j))
    full = lambda a: pl.BlockSpec(a.shape, lambda bi, i: (0, 0))
    consts = (mu3, mul, w0, a0, k_k, k_a, w2p, a2p, g2p, r_k, ln_w, ln_b)
    out = pl.pallas_call(
        _rwkv_chunk_kernel,
        grid=(B // ns, np2 + 1),
        in_specs=[cols(W, 1), cols(W, 2), cols(W, 3), cols(lw, 4 * W // lw)] + [full(a) for a in consts],
        out_specs=pl.BlockSpec((ns, 2 * C, W), lambda bi, i: (bi, jnp.maximum(i - 1, 0), 0)),
        out_shape=jax.ShapeDtypeStruct((B, L, W), BF16),
        scratch_shapes=[pltpu.VMEM((ns, H, RW_N, RW_N), F32), pltpu.VMEM((ns, C, W), F32),
                        pltpu.VMEM((2, ns, C, W), F32), pltpu.VMEM((2, ns, C, W), BF16),
                        pltpu.VMEM((2, ns, C, W), BF16),
                        pltpu.VMEM((ns, 8, W), F32), pltpu.VMEM((ns, 8, lw), F32),
                        pltpu.VMEM((2, len(RW_OPERANDS), ns, C, W), BF16), pltpu.VMEM((2, ns, 8, W), F32)],
        compiler_params=_cparams("arbitrary", "arbitrary"),
        name="rwkv_chunk",
    )(proj3, proj3, proj3, proj3, *consts)
    return out.reshape(T, W)


def _first_index_of_max(vals, lane, valid):
    neg = jnp.float32(-jnp.inf)
    masked = jnp.where(valid, vals, neg)
    m = jnp.max(masked, axis=-1, keepdims=True)
    idx = jnp.min(jnp.where(valid & (masked == m), lane, LANES), axis=-1, keepdims=True)
    return m, idx


def _outproj_kernel(s5_ref, rw_ref, x_ref, wt_ref, wb_ref, g_ref, wr_ref, br_ref,
                    h_ref, hn_ref, rec_ref):
    h = x_ref[...] + (_dot(s5_ref[...], wt_ref[...]) + _dot(rw_ref[...], wb_ref[...]))
    h_ref[...] = h
    ms = jnp.mean(h * h, axis=-1, keepdims=True)
    hn = h * lax.rsqrt(ms + RMS_EPS) * g_ref[...]
    hn_ref[...] = _pack_halves(hn).reshape(hn_ref.shape)

    hh, hl = _split2(hn)
    wh, wl = wr_ref[:, :LANES], wr_ref[:, LANES:]
    logits = _dot(hh, wh) + (_dot(hh, wl) + _dot(hl, wh)) + br_ref[...]
    lane = lax.broadcasted_iota(I32, logits.shape, 1)
    is_grp = (lane >= N_EXPERTS) & (lane < N_EXPERTS + N_GROUPS)
    gmax, gidx = _first_index_of_max(logits, lane, is_grp)
    gsum = jnp.sum(jnp.where(is_grp, jnp.exp(logits - gmax), 0.0), axis=-1, keepdims=True)
    p_grp = 1.0 / gsum
    grp = gidx - N_EXPERTS
    in_grp = (lane >= grp * EPG) & (lane < (grp + 1) * EPG)
    m1, i1 = _first_index_of_max(logits, lane, in_grp)
    m2, i2 = _first_index_of_max(logits, lane, in_grp & (lane != i1))
    e = jnp.exp(m2 - m1)
    g1 = p_grp / (1.0 + e)
    g2 = p_grp * e / (1.0 + e)
    rec = jnp.where(lane == 0, i1.astype(F32),
          jnp.where(lane == 1, i2.astype(F32),
          jnp.where(lane == 2, g1, jnp.where(lane == 3, g2, 0.0))))
    rec_ref[...] = rec


def _outproj(s5o, rwo, x2, w_top, w_bot, g, wr, b_route, tm):
    T, D = x2.shape
    W = s5o.shape[1]
    full = lambda a: pl.BlockSpec(a.shape, lambda i: (0, 0))
    return pl.pallas_call(
        _outproj_kernel,
        grid=(T // tm,),
        in_specs=[pl.BlockSpec((tm, W), lambda i: (i, 0)), pl.BlockSpec((tm, W), lambda i: (i, 0)),
                  pl.BlockSpec((tm, D), lambda i: (i, 0)),
                  full(w_top), full(w_bot), full(g), full(wr), full(b_route)],
        out_specs=[pl.BlockSpec((tm, D), lambda i: (i, 0)), pl.BlockSpec((tm, 1, D // 2), lambda i: (i, 0, 0)),
                   pl.BlockSpec((tm, LANES), lambda i: (i, 0))],
        out_shape=[jax.ShapeDtypeStruct((T, D), F32), jax.ShapeDtypeStruct((T, 1, D // 2), U32),
                   jax.ShapeDtypeStruct((T, LANES), F32)],
        compiler_params=_cparams("arbitrary"),
        name="outproj_route",
    )(s5o, rwo, x2, w_top, w_bot, g, wr, b_route)


def _onehots(rec, lane):
    oh0 = (lane == rec[:, 0:1].astype(I32)).astype(F32)
    oh1 = (lane == rec[:, 1:2].astype(I32)).astype(F32)
    return oh0, oh1


def _rank_kernel(rec_ref, rank_ref, cnt_ref, base_ref):
    tb = rec_ref.shape[0]

    @pl.when(pl.program_id(0) == 0)
    def _():
        base_ref[...] = jnp.zeros_like(base_ref)

    lane = lax.broadcasted_iota(I32, (tb, LANES), 1)
    oh0, oh1 = _onehots(rec_ref[...], lane)
    both = oh0 + oh1
    ri = lax.broadcasted_iota(I32, (tb, tb), 0)
    ci = lax.broadcasted_iota(I32, (tb, tb), 1)
    before = _dot((ri > ci).astype(BF16), both.astype(BF16)) + base_ref[0:1, :]
    rank0 = jnp.sum(oh0 * before, axis=-1, keepdims=True)
    rank1 = jnp.sum(oh1 * before, axis=-1, keepdims=True)
    rank_ref[...] = jnp.where(lane == 0, rank0, jnp.where(lane == 1, rank1, 0.0))
    total = base_ref[0:1, :] + jnp.sum(both, axis=0, keepdims=True)
    base_ref[0:1, :] = total
    cnt_ref[...] = jnp.broadcast_to(total, cnt_ref.shape)


def _rank(rec, tb):
    T = rec.shape[0]
    return pl.pallas_call(
        _rank_kernel,
        grid=(T // tb,),
        in_specs=[pl.BlockSpec((tb, LANES), lambda i: (i, 0))],
        out_specs=[pl.BlockSpec((tb, LANES), lambda i: (i, 0)), pl.BlockSpec((8, LANES), lambda i: (0, 0))],
        out_shape=[jax.ShapeDtypeStruct((T, LANES), F32), jax.ShapeDtypeStruct((8, LANES), F32)],
        scratch_shapes=[pltpu.VMEM((8, LANES), F32)],
        compiler_params=_cparams("arbitrary"),
        name="moe_rank",
    )(rec)


def _padded_starts(cnt):
    padded = jnp.ceil(cnt * (1.0 / MOE_BLOCK)) * MOE_BLOCK
    ri = lax.broadcasted_iota(I32, (LANES, LANES), 0)
    ci = lax.broadcasted_iota(I32, (LANES, LANES), 1)
    p8 = jnp.broadcast_to(padded, (8, LANES))
    pend = _dot_exact_lhs_rhs(p8, (ri <= ci).astype(BF16))[0:1, :]
    return pend - padded, pend


def _dot_exact_lhs_rhs(a, b_bf16):
    ah, am, al = _split3(a)
    return _dot(ah, b_bf16) + (_dot(am, b_bf16) + _dot(al, b_bf16))


def _dest_kernel(rec_ref, rank_ref, cnt_ref, dest_ref, blk_ref):
    tb = rec_ref.shape[0]
    cnt = cnt_ref[0:1, :]
    pstart, pend = _padded_starts(cnt)
    lane = lax.broadcasted_iota(I32, (tb, LANES), 1)
    oh0, oh1 = _onehots(rec_ref[...], lane)
    rank = rank_ref[...]
    d0 = jnp.sum(oh0 * pstart, axis=-1, keepdims=True) + rank[:, 0:1]
    d1 = jnp.sum(oh1 * pstart, axis=-1, keepdims=True) + rank[:, 1:2]
    dest_ref[...] = jnp.where(lane == 0, d0, jnp.where(lane == 1, d1, 0.0)).astype(I32)

    @pl.when(pl.program_id(0) == 0)
    def _():
        nb = blk_ref.shape[0]
        blane = lax.broadcasted_iota(I32, (nb, LANES), 1)
        bstart = (lax.broadcasted_iota(I32, (nb, 1), 0) * MOE_BLOCK).astype(F32)
        is_e = blane < N_EXPERTS
        bexp = jnp.sum(jnp.where(is_e & (pend <= bstart), 1.0, 0.0), axis=-1, keepdims=True)
        bexp = jnp.minimum(bexp, N_EXPERTS - 1.0)
        nact = jnp.max(jnp.where(is_e, pend, 0.0), axis=-1, keepdims=True) * (1.0 / MOE_BLOCK)
        brow = lax.broadcasted_iota(I32, (nb, 1), 0)
        out = jnp.where(blane == 0, bexp, jnp.where(blane == 1, nact, 0.0))
        out = jnp.where(brow == nb - 1, jnp.where(is_e, pstart + cnt, 0.0), out)
        out = jnp.where(brow == nb - 2, jnp.where(is_e, pend, 0.0), out)
        blk_ref[...] = out.astype(I32)


def _dest(rec, rank, cnt, tb, nb_rows):
    T = rec.shape[0]
    return pl.pallas_call(
        _dest_kernel,
        grid=(T // tb,),
        in_specs=[pl.BlockSpec((tb, LANES), lambda i: (i, 0)), pl.BlockSpec((tb, LANES), lambda i: (i, 0)),
                  pl.BlockSpec((8, LANES), lambda i: (0, 0))],
        out_specs=[pl.BlockSpec((tb, LANES), lambda i: (i, 0)), pl.BlockSpec((nb_rows, LANES), lambda i: (0, 0))],
        out_shape=[jax.ShapeDtypeStruct((T, LANES), I32), jax.ShapeDtypeStruct((nb_rows, LANES), I32)],
        compiler_params=_cparams("arbitrary"),
        name="moe_dest",
    )(rec, rank, cnt)


def _scatter_kernel(pad_ref, pend_ref, na_ref, dest_ref, hn_ref, xs_ref, zbuf_ref, idx_ref, sa_ref, sb_ref,
                    sem_a, sem_b, isem, zsem, *, tb, n_blocks, n_tail, n_steps):
    i = pl.program_id(0)

    def zero_fill(act):
        def per_expert(e, c):
            row = pad_ref[e]
            left = pend_ref[e] - row
            size = MOE_BLOCK // 2
            while size >= 1:
                @pl.when((left & size) != 0)
                def _(row=row, size=size):
                    act(pltpu.make_async_copy(zbuf_ref.at[pl.ds(0, size)], xs_ref.at[pl.ds(row, size)], zsem))
                row = row + (left & size)
                size //= 2
            return c

        def per_tail_block(k, c):
            @pl.when(na_ref[0] + k < n_blocks)
            def _():
                act(pltpu.make_async_copy(zbuf_ref, xs_ref.at[pl.ds((na_ref[0] + k) * MOE_BLOCK, MOE_BLOCK)], zsem))
            return c

        lax.fori_loop(0, N_EXPERTS, per_expert, 0)
        lax.fori_loop(0, n_tail, per_tail_block, 0)

    @pl.when(i == 0)
    def _():
        zbuf_ref[...] = jnp.zeros_like(zbuf_ref)
        zero_fill(lambda cp: cp.start())

    def idx_copy(step, slot):
        return pltpu.make_async_copy(dest_ref.at[step], idx_ref.at[pl.ds(slot * (2 * tb), 2 * tb)], isem)

    @pl.when(i == 0)
    def _():
        idx_copy(0, 0).start()

    slot = i % 2
    idx_copy(i, slot).wait()

    @pl.when(i + 1 < n_steps)
    def _():
        idx_copy(i + 1, 1 - slot).start()

    hb = tb // 2
    for half, (buf, sem) in enumerate(((sa_ref, sem_a), (sb_ref, sem_b))):
        def wait_rows(buf=buf, sem=sem):
            for _ in range(2):
                pltpu.make_async_copy(buf, xs_ref.at[pl.ds(0, hb)], sem).wait()

        @pl.when(i > 0)
        def _():
            wait_rows()

        buf[...] = hn_ref[half * hb:(half + 1) * hb]

        def issue(t, c, buf=buf, sem=sem, half=half):
            for s in range(2):
                d = idx_ref[slot * (2 * tb) + 2 * (half * hb + t) + s]
                pltpu.make_async_copy(buf.at[pl.ds(t, 1)], xs_ref.at[pl.ds(d, 1)], sem).start(priority=s)
            return c

        lax.fori_loop(0, hb, issue, 0, unroll=8)

    @pl.when(i == n_steps - 1)
    def _():
        for buf, sem in ((sa_ref, sem_a), (sb_ref, sem_b)):
            for _ in range(2):
                pltpu.make_async_copy(buf, xs_ref.at[pl.ds(0, hb)], sem).wait()
        zero_fill(lambda cp: cp.wait())


def _scatter(padrow, pend, nact, dest2, hn, tb, n_blocks):
    T, _, D = hn.shape
    n_tail = n_blocks - (-(-2 * T // MOE_BLOCK))
    hb = tb // 2
    return pl.pallas_call(
        functools.partial(_scatter_kernel, tb=tb, n_blocks=n_blocks, n_tail=n_tail, n_steps=T // tb),
        grid_spec=pltpu.PrefetchScalarGridSpec(
            num_scalar_prefetch=3, grid=(T // tb,),
            in_specs=[pl.BlockSpec(memory_space=pl.ANY),
                      pl.BlockSpec((tb, 1, D), lambda i, pad, pend, na: (i, 0, 0))],
            out_specs=pl.BlockSpec(memory_space=pl.ANY),
            scratch_shapes=[pltpu.VMEM((MOE_BLOCK, 1, D), hn.dtype), pltpu.SMEM((4 * tb,), I32),
                            pltpu.VMEM((hb, 1, D), hn.dtype), pltpu.VMEM((hb, 1, D), hn.dtype),
                            pltpu.SemaphoreType.DMA, pltpu.SemaphoreType.DMA,
                            pltpu.SemaphoreType.DMA, pltpu.SemaphoreType.DMA]),
        out_shape=jax.ShapeDtypeStruct((n_blocks * MOE_BLOCK, 1, D), hn.dtype),
        compiler_params=_cparams("arbitrary"),
        name="moe_scatter",
    )(padrow, pend, nact, dest2, hn)


W_SLOTS = 2


def _expert_kernel(be_ref, na_ref, x_ref, wg_hbm, wu_hbm, wd_hbm, y_ref,
                   wg_buf, wu_buf, wd_buf, wgb_ref, wub_ref, wdb_ref, x2_ref, elist_ref, state_ref, sems):
    j = pl.program_id(0)
    na = na_ref[0]

    def weight_copies(e, slot):
        return [pltpu.make_async_copy(src.at[e], buf.at[slot], sems.at[slot, n])
                for n, (src, buf) in enumerate(((wg_hbm, wg_buf), (wu_hbm, wu_buf), (wd_hbm, wd_buf)))]

    @pl.when(j == 0)
    def _():
        def scan(k, n):
            is_new = (k == 0) | (be_ref[k] != be_ref[jnp.maximum(k - 1, 0)])

            @pl.when(is_new)
            def _():
                elist_ref[n] = be_ref[k]
            return n + is_new.astype(I32)

        n_exp = lax.fori_loop(0, na, scan, 0)
        state_ref[0] = 0
        state_ref[1] = n_exp
        for n in range(W_SLOTS):
            @pl.when(n < n_exp)
            def _():
                for cp in weight_copies(elist_ref[n], n):
                    cp.start()

    @pl.when(j < na)
    def _():
        @pl.when((j == 0) | (be_ref[j] != be_ref[jnp.maximum(j - 1, 0)]))
        def _():
            n = state_ref[0]
            slot = n % W_SLOTS
            for cp in weight_copies(elist_ref[n], slot):
                cp.wait()
            wgb_ref[...] = wg_buf[slot].astype(BF16)
            wub_ref[...] = wu_buf[slot].astype(BF16)
            wdb_ref[...] = wd_buf[slot].astype(BF16)

            @pl.when(n + W_SLOTS < state_ref[1])
            def _():
                for cp in weight_copies(elist_ref[n + W_SLOTS], slot):
                    cp.start()

            state_ref[0] = n + 1

        x2_ref[...] = x_ref[...].reshape(x2_ref.shape)
        lo, hi = _unpack_halves(x2_ref[...])
        lo, hi = lo.astype(BF16), hi.astype(BF16)
        half = lo.shape[1]
        hg = _dot(lo, wgb_ref[:half, :]) + _dot(hi, wgb_ref[half:, :])
        hu = _dot(lo, wub_ref[:half, :]) + _dot(hi, wub_ref[half:, :])
        act = (hg * jax.nn.sigmoid(hg)) * hu
        y = _dot(act.astype(BF16), wdb_ref[...])
        y_ref[...] = _pack_halves(y).reshape(y_ref.shape)

    @pl.when(j >= na)
    def _():
        y_ref[...] = jnp.zeros_like(y_ref)


def _experts(blk_exp, nact, xs, w_gate, w_up, w_down, n_blocks):
    DP = xs.shape[2]
    D = 2 * DP
    rows = n_blocks * MOE_BLOCK
    DE = w_gate.shape[2]
    blk = lambda j, be, na: (jnp.minimum(j, na[0] - 1), 0, 0)
    hbm = pl.BlockSpec(memory_space=pl.ANY)
    return pl.pallas_call(
        _expert_kernel,
        grid_spec=pltpu.PrefetchScalarGridSpec(
            num_scalar_prefetch=2, grid=(n_blocks,),
            in_specs=[pl.BlockSpec((MOE_BLOCK, 1, DP), blk), hbm, hbm, hbm],
            out_specs=pl.BlockSpec((MOE_BLOCK, 1, DP), lambda j, be, na: (j, 0, 0)),
            scratch_shapes=[pltpu.VMEM((W_SLOTS, D, DE), F32), pltpu.VMEM((W_SLOTS, D, DE), F32),
                            pltpu.VMEM((W_SLOTS, DE, D), F32),
                            pltpu.VMEM((D, DE), BF16), pltpu.VMEM((D, DE), BF16), pltpu.VMEM((DE, D), BF16),
                            pltpu.VMEM((MOE_BLOCK, DP), U32), pltpu.SMEM((N_EXPERTS,), I32), pltpu.SMEM((2,), I32),
                            pltpu.SemaphoreType.DMA((W_SLOTS, 3))]),
        out_shape=jax.ShapeDtypeStruct((rows, 1, DP), U32),
        compiler_params=_cparams("arbitrary"),
        name="moe_experts",
    )(blk_exp, nact, xs, w_gate, w_up, w_down)


def _combine_kernel(dest_ref, ys_ref, h_ref, rec_ref, g_ref, o_ref, idx_ref, ya0_ref, ya1_ref, yb0_ref, yb1_ref,
                    y2_ref, y3_ref, sem_a, sem_b, isem, *, normalize, n_steps):
    i = pl.program_id(0)
    tb = h_ref.shape[0]
    hb = tb // 2
    halves = ((ya0_ref, ya1_ref, sem_a), (yb0_ref, yb1_ref, sem_b))

    def idx_copy(step, slot):
        return pltpu.make_async_copy(dest_ref.at[step], idx_ref.at[pl.ds(slot * (2 * tb), 2 * tb)], isem)

    def issue(half, slot):
        bufs, sem = halves[half][:2], halves[half][2]

        def body(t, c):
            for s in range(2):
                d = idx_ref[slot * (2 * tb) + 2 * (half * hb + t) + s]
                pltpu.make_async_copy(ys_ref.at[pl.ds(d, 1)], bufs[s].at[pl.ds(t, 1)], sem).start(priority=s)
            return c

        lax.fori_loop(0, hb, body, 0, unroll=8)

    @pl.when(i == 0)
    def _():
        first = idx_copy(0, 0)
        first.start()
        first.wait()
        issue(0, 0)
        issue(1, 0)
        if n_steps > 1:
            idx_copy(1, 1).start()

    nslot = (i + 1) % 2

    @pl.when(i + 1 < n_steps)
    def _():
        idx_copy(i + 1, nslot).wait()

    rec = rec_ref[...]
    for half, (y0_ref, y1_ref, sem) in enumerate(halves):
        rows = slice(half * hb, (half + 1) * hb)
        pltpu.make_async_copy(ys_ref.at[pl.ds(0, hb)], y0_ref, sem).wait()
        pltpu.make_async_copy(ys_ref.at[pl.ds(0, hb)], y1_ref, sem).wait()
        y2_ref[...] = y0_ref[...].reshape(y2_ref.shape)
        y3_ref[...] = y1_ref[...].reshape(y3_ref.shape)

        @pl.when(i + 1 < n_steps)
        def _():
            issue(half, nslot)

        h = h_ref[rows, :] + rec[rows, 2:3] * jnp.concatenate(_unpack_halves(y2_ref[...]), axis=1)
        h = h + rec[rows, 3:4] * jnp.concatenate(_unpack_halves(y3_ref[...]), axis=1)
        if normalize:
            ms = jnp.mean(h * h, axis=-1, keepdims=True)
            h = h * lax.rsqrt(ms + RMS_EPS) * g_ref[...]
        o_ref[rows, :] = h

    @pl.when(i + 2 < n_steps)
    def _():
        idx_copy(i + 2, i % 2).start()


def _combine(dest2, ys, h, rec, g, tb, normalize):
    T, D = h.shape
    hb = tb // 2
    return pl.pallas_call(
        functools.partial(_combine_kernel, normalize=normalize, n_steps=T // tb),
        grid=(T // tb,),
        in_specs=[pl.BlockSpec(memory_space=pl.ANY), pl.BlockSpec(memory_space=pl.ANY),
                  pl.BlockSpec((tb, D), lambda i: (i, 0)), pl.BlockSpec((tb, LANES), lambda i: (i, 0)),
                  pl.BlockSpec((1, D), lambda i: (0, 0))],
        out_specs=pl.BlockSpec((tb, D), lambda i: (i, 0)),
        out_shape=jax.ShapeDtypeStruct((T, D), F32),
        scratch_shapes=[pltpu.SMEM((4 * tb,), I32)] + [pltpu.VMEM((hb, 1, D // 2), U32)] * 4 +
                       [pltpu.VMEM((hb, D // 2), U32)] * 2 + [pltpu.SemaphoreType.DMA] * 3,
        compiler_params=_cparams("arbitrary"),
        name="moe_combine",
    )(dest2, ys, h, rec, g)


def _pad_cols(a, n):
    return jnp.pad(a, ((0, 0), (0, n - a.shape[1])))


def _pad_rows(a, n, at=0):
    return jnp.pad(a, ((at, n - a.shape[0] - at), (0, 0)))


def _layer(x2, B, L, p):
    T, D = x2.shape
    W = p["s5_d"].shape[0]
    G = p["s5_lambda_re"].shape[0]
    assert W == G * S5_HG
    dl, al, gl = p["rwkv_w2"].shape[0], p["rwkv_a2"].shape[0], p["rwkv_g2"].shape[0]

    assert dl + al == LANES and gl <= RW_LORA_PAD - LANES
    w_in_r = _pad_cols(p["w_in"], 4 * W + RW_LORA_PAD).astype(BF16)
    n_proj = w_in_r.shape[1]
    tm_in = min(1024, T)
    proj = _inproj(x2, p["norm_mix_g"].reshape(1, D), w_in_r, tm_in, n_proj // INPROJ_N_TILES)

    C = S5_CHUNK
    nc = L // C
    ct_re = jnp.swapaxes(p["s5_c_re"], 1, 2)
    ct_im = jnp.swapaxes(p["s5_c_im"], 1, 2)
    rb = min(LANES, B * nc)
    z = _s5_in(proj.reshape(B * nc, C, proj.shape[1]), G, rb)
    y_t = _s5_chunk(z, p["s5_lambda_re"], p["s5_lambda_im"], p["s5_log_step"],
                    ct_re, ct_im, p["s5_b_re"], p["s5_b_im"], p["s5_c_re"], p["s5_c_im"], nc)
    y_ssm = _s5_out(y_t, rb).reshape(T, W)
    tm = min(512, T)
    s5_out = _s5_glu(y_ssm, proj, p["s5_d"].reshape(1, W), p["s5_w_glu"].astype(BF16),
                     p["s5_b_glu"].reshape(1, W), tm)

    mu = p["rwkv_mu"]
    mu3 = _pad_rows(mu[:3 * W].reshape(3, W), 8)
    mul = _pad_cols(mu[None, 3 * W:], RW_LORA_PAD)
    w2p = _pad_rows(p["rwkv_w2"], LANES).astype(BF16)
    a2p = _pad_rows(p["rwkv_a2"], LANES, at=dl).astype(BF16)
    g2p = _pad_rows(p["rwkv_g2"], -(-gl // LANES) * LANES).astype(BF16)
    row = lambda a: a.reshape(1, W)
    rw_out = _rwkv_chunk(proj, B, L, mu3, mul, row(p["rwkv_w0"]), row(p["rwkv_a0"]), row(p["rwkv_k_k"]),
                         row(p["rwkv_k_a"]), w2p, a2p, g2p, row(p["rwkv_r_k"]), row(p["rwkv_ln_w"]),
                         row(p["rwkv_ln_b"]))

    w_out = p["w_out"].astype(BF16)
    w_route = jnp.concatenate([p["w_route_exp"], p["w_route_grp"]], axis=1)
    w_route = _pad_cols(w_route, LANES)
    wr_hi = w_route.astype(BF16)
    wr_lo = (w_route - wr_hi.astype(F32)).astype(BF16)
    wr = jnp.concatenate([wr_hi, wr_lo], axis=1)
    b_route = _pad_cols(jnp.concatenate([p["b_route_exp"], p["b_route_grp"]])[None, :], LANES)
    tm_o = min(512, T)
    h, hn, rec = _outproj(s5_out, rw_out, x2, w_out[:W], w_out[W:], p["norm_ffn_g"].reshape(1, D),
                          wr, b_route, tm_o)

    tb = min(1024, T)
    rank, cnt = _rank(rec, tb)
    n_blocks = -(-2 * T // MOE_BLOCK) + N_EXPERTS
    nb_rows = -(-(n_blocks + 1) // 8) * 8
    dest, blk = _dest(rec, rank, cnt, tb, nb_rows)
    blk_exp = blk[:n_blocks, 0]
    nact = blk[0:1, 1]
    padrow, pend = blk[nb_rows - 1, :N_EXPERTS], blk[nb_rows - 2, :N_EXPERTS]
    ts = min(1024, T)
    dest2 = dest[:, :2].reshape(T // ts, 2 * ts)

    xs = _scatter(padrow, pend, nact, dest2, hn, ts, n_blocks)
    ys = _experts(blk_exp, nact, xs, p["w_gate"], p["w_up"], p["w_down"], n_blocks)
    return dest2, ys, h, rec, ts


def kernel(x, norm_mix_g, w_in, s5_lambda_re, s5_lambda_im, s5_log_step, s5_b_re, s5_b_im, s5_c_re, s5_c_im, s5_d, s5_w_glu, s5_b_glu, rwkv_mu, rwkv_w0, rwkv_w2, rwkv_a0, rwkv_a2, rwkv_g2, rwkv_k_k, rwkv_k_a, rwkv_r_k, rwkv_ln_w, rwkv_ln_b, w_out, norm_ffn_g, w_route_grp, b_route_grp, w_route_exp, b_route_exp, w_gate, w_up, w_down, norm_final_g):
    B, L, D = x.shape
    params = dict(
        norm_mix_g=norm_mix_g, w_in=w_in, s5_lambda_re=s5_lambda_re, s5_lambda_im=s5_lambda_im,
        s5_log_step=s5_log_step, s5_b_re=s5_b_re, s5_b_im=s5_b_im, s5_c_re=s5_c_re, s5_c_im=s5_c_im,
        s5_d=s5_d, s5_w_glu=s5_w_glu, s5_b_glu=s5_b_glu, rwkv_mu=rwkv_mu, rwkv_w0=rwkv_w0, rwkv_w2=rwkv_w2,
        rwkv_a0=rwkv_a0, rwkv_a2=rwkv_a2, rwkv_g2=rwkv_g2, rwkv_k_k=rwkv_k_k, rwkv_k_a=rwkv_k_a,
        rwkv_r_k=rwkv_r_k, rwkv_ln_w=rwkv_ln_w, rwkv_ln_b=rwkv_ln_b, w_out=w_out, norm_ffn_g=norm_ffn_g,
        w_route_grp=w_route_grp, b_route_grp=b_route_grp, w_route_exp=w_route_exp, b_route_exp=b_route_exp,
        w_gate=w_gate, w_up=w_up, w_down=w_down)
    depth = norm_mix_g.shape[0]
    h2 = x.reshape(B * L, D)
    for l in range(depth):
        p = {k_: v_[l] for k_, v_ in params.items()}
        dest2, ys, h, rec, ts = _layer(h2, B, L, p)
        h2 = _combine(dest2, ys, h, rec, norm_final_g.reshape(1, D), ts, normalize=(l == depth - 1))
    return h2.reshape(B, L, D)
```
